```python
import jax, jax.numpy as jnp
from jax import lax
import numpy as np

D_MODEL = 1024
BATCH = 8
SEQ = 2048
DEPTH = 4

N_MIXERS = 3
N_LAYERS_A = (DEPTH + 2) // 3
N_LAYERS_B = (DEPTH + 1) // 3
N_LAYERS_C = DEPTH // 3

DILATED_CFG = ((128, 1), (512, 4), (2048, 16))
N_GROUPS_A = len(DILATED_CFG)
HEADS_PER_GROUP = 8
HEAD_DIM = 64
N_HEADS_A = N_GROUPS_A * HEADS_PER_GROUP
ATTN_WIDTH = HEADS_PER_GROUP * HEAD_DIM
QKV_WIDTH = N_GROUPS_A * 3 * ATTN_WIDTH
BLOCK = 128
NEG_INF = -1e30

SHORT_CONV_WIDTH = 3

POOL_WINDOWS = (2, 4, 8, 16)
N_POOL_GROUPS = len(POOL_WINDOWS)
POOL_GROUP_DIM = D_MODEL // N_POOL_GROUPS

D_FF = 2816
FFN_CONV_WIDTH = 3

RMS_EPS = 1e-6

kernel_name = "interleaved_dilated_attn_shortconv_pool_convffn"


def rms_norm(x, g):
    xf = x.astype(jnp.float32)
    y = xf * lax.rsqrt(jnp.mean(xf * xf, axis=-1, keepdims=True) + RMS_EPS)
    return (y * g.astype(jnp.float32)).astype(x.dtype)


def causal_depthwise_conv(x, w):
    k_w, c = w.shape
    return lax.conv_general_dilated(
        x, w.astype(x.dtype)[:, None, :], window_strides=(1,), padding=[(k_w - 1, 0)],
        dimension_numbers=("NWC", "WIO", "NWC"), feature_group_count=c)


def alibi_slopes():
    return jnp.asarray(2.0 ** (-8.0 * np.arange(1, N_HEADS_A + 1) / N_HEADS_A), jnp.float32)


def dilated_group_attention(q, k, v, slopes, window, dilation):
    b, s, h, dh = q.shape
    span = window // dilation
    length = s // dilation
    nb = -(-length // BLOCK)
    lp = nb * BLOCK

    def residues(t):
        return t.reshape(b, length, dilation, h, dh).transpose(0, 2, 3, 1, 4)

    qb = jnp.pad(residues(q), ((0, 0), (0, 0), (0, 0), (0, lp - length), (0, 0)))
    qb = qb.reshape(b, dilation, h, nb, BLOCK, dh)

    def band(t):
        tb = jnp.pad(residues(t), ((0, 0), (0, 0), (0, 0), (BLOCK, lp - length), (0, 0)))
        tb = tb.reshape(b, dilation, h, nb + 1, BLOCK, dh)
        return jnp.concatenate([tb[:, :, :, :-1], tb[:, :, :, 1:]], axis=4)

    kb, vb = band(k), band(v)
    scores = jnp.einsum("brhnqc,brhnkc->brhnqk", qb, kb).astype(jnp.float32) * (HEAD_DIM ** -0.5)

    delta = BLOCK + np.arange(BLOCK)[:, None] - np.arange(2 * BLOCK)[None, :]
    valid = (delta >= 0) & (delta <= span)
    valid = valid[None] & ((np.arange(nb)[:, None, None] > 0) | (np.arange(2 * BLOCK)[None, None, :] >= BLOCK))
    bias = -slopes[:, None, None] * jnp.asarray(delta * dilation, jnp.float32)[None]
    scores = jnp.where(valid, scores + bias[:, None], NEG_INF)

    lse = jax.nn.logsumexp(scores, axis=-1)
    p = jnp.exp(scores - lse[..., None])
    o = jnp.einsum("brhnqk,brhnkc->brhnqc", p.astype(vb.dtype), vb)
    o = o.reshape(b, dilation, h, lp, dh)[:, :, :, :length].transpose(0, 3, 1, 2, 4).reshape(b, s, h, dh)
    lse = lse.reshape(b, dilation, h, lp)[..., :length].transpose(0, 3, 1, 2).reshape(b, s, h)
    return o, lse


def dilated_attention_mixer(x, w_qkv, w_o):
    b, s, _ = x.shape
    qkv = (x @ w_qkv).reshape(b, s, N_GROUPS_A, 3, HEADS_PER_GROUP, HEAD_DIM)
    slopes = alibi_slopes()
    outs, lses = [], []
    for g, (window, dilation) in enumerate(DILATED_CFG):
        o, lse = dilated_group_attention(
            qkv[:, :, g, 0], qkv[:, :, g, 1], qkv[:, :, g, 2],
            slopes[g * HEADS_PER_GROUP:(g + 1) * HEADS_PER_GROUP], window, dilation)
        outs.append(o)
        lses.append(lse)
    wts = jax.nn.softmax(jnp.stack(lses, 0), axis=0)
    merged = jnp.einsum("gbsh,gbshc->bshc", wts, jnp.stack(outs, 0).astype(jnp.float32))
    return merged.reshape(b, s, ATTN_WIDTH).astype(x.dtype) @ w_o


def short_conv_mixer(x, w_in, w_dw, w_out):
    gate_b, gate_c, h = jnp.split(x @ w_in, 3, axis=-1)
    return (gate_b * causal_depthwise_conv(gate_c * h, w_dw)) @ w_out


def pooling_mixer(x, w_in, w_grp, scale, w_out):
    b, s, _ = x.shape
    u = (x @ w_in).astype(jnp.float32).reshape(b, s, N_POOL_GROUPS, POOL_GROUP_DIM)
    cs = jnp.cumsum(u, axis=1)
    pos = jnp.arange(1, s + 1, dtype=jnp.float32)[None, :, None]
    pooled = []
    for g, w in enumerate(POOL_WINDOWS):
        c = cs[:, :, g]
        prev = jnp.pad(c, ((0, 0), (w, 0), (0, 0)))[:, :s]
        pooled.append((c - prev) / jnp.minimum(pos, float(w)) - u[:, :, g])
    p = jnp.stack(pooled, axis=2).astype(x.dtype)
    y = jnp.einsum("bsgc,gcd->bsgd", p, w_grp).reshape(b, s, D_MODEL) * scale
    return y @ w_out


def conv_ffn(x, w_up, w_dw, w_down):
    h = causal_depthwise_conv(x @ w_up, w_dw)
    g, u = jnp.split(h, 2, axis=-1)
    return (jax.nn.silu(g) * u) @ w_down


def _fwd_setup_inputs(seed: int = 0) -> dict:
    key = jax.random.key(seed)
    ks = jax.random.split(key, 16)
    f32 = jnp.float32

    def nrm(k, shape, fan_in):
        return jax.random.normal(k, shape, f32) * (fan_in ** -0.5)

    return {
        "x": jax.random.normal(ks[0], (BATCH, SEQ, D_MODEL), f32),
        "norm_g": 1.0 + 0.05 * jax.random.normal(ks[1], (DEPTH, 4, D_MODEL), f32),
        "attn_w_qkv": nrm(ks[2], (N_LAYERS_A, D_MODEL, QKV_WIDTH), D_MODEL),
        "attn_w_o": nrm(ks[3], (N_LAYERS_A, ATTN_WIDTH, D_MODEL), ATTN_WIDTH),
        "conv_w_in": nrm(ks[4], (N_LAYERS_B, D_MODEL, 3 * D_MODEL), D_MODEL),
        "conv_w_dw": nrm(ks[5], (N_LAYERS_B, SHORT_CONV_WIDTH, D_MODEL), SHORT_CONV_WIDTH),
        "conv_w_out": nrm(ks[6], (N_LAYERS_B, D_MODEL, D_MODEL), D_MODEL),
        "pool_w_in": nrm(ks[7], (N_LAYERS_C, D_MODEL, D_MODEL), D_MODEL),
        "pool_w_grp": nrm(ks[8], (N_LAYERS_C, N_POOL_GROUPS, POOL_GROUP_DIM, POOL_GROUP_DIM), POOL_GROUP_DIM),
        "pool_scale": 1.0 + 0.1 * jax.random.normal(ks[9], (N_LAYERS_C, D_MODEL), f32),
        "pool_w_out": nrm(ks[10], (N_LAYERS_C, D_MODEL, D_MODEL), D_MODEL),
        "ffn_w_up": nrm(ks[11], (DEPTH, D_MODEL, 2 * D_FF), D_MODEL),
        "ffn_w_dw": nrm(ks[12], (DEPTH, FFN_CONV_WIDTH, 2 * D_FF), FFN_CONV_WIDTH),
        "ffn_w_down": nrm(ks[13], (DEPTH, D_FF, D_MODEL), D_FF),
    }


def _fwd_reference(x, norm_g, attn_w_qkv, attn_w_o, conv_w_in, conv_w_dw, conv_w_out,
              pool_w_in, pool_w_grp, pool_scale, pool_w_out, ffn_w_up, ffn_w_dw, ffn_w_down):
    ia = ib = ic = 0
    for i in range(DEPTH):
        h = rms_norm(x, norm_g[i, 0])
        kind = i % N_MIXERS
        if kind == 0:
            h = dilated_attention_mixer(h, attn_w_qkv[ia], attn_w_o[ia])
            ia += 1
        elif kind == 1:
            h = short_conv_mixer(h, conv_w_in[ib], conv_w_dw[ib], conv_w_out[ib])
            ib += 1
        else:
            h = pooling_mixer(h, pool_w_in[ic], pool_w_grp[ic], pool_scale[ic], pool_w_out[ic])
            ic += 1
        x = x + rms_norm(h, norm_g[i, 1])
        h = conv_ffn(rms_norm(x, norm_g[i, 2]), ffn_w_up[i], ffn_w_dw[i], ffn_w_down[i])
        x = x + rms_norm(h, norm_g[i, 3])
    return x


import jax as _jax
import jax.numpy as _jnp

TWIN_FORMAT = 'train_step'
FWD_PARAMS = ['x', 'norm_g', 'attn_w_qkv', 'attn_w_o', 'conv_w_in', 'conv_w_dw', 'conv_w_out', 'pool_w_in', 'pool_w_grp', 'pool_scale', 'pool_w_out', 'ffn_w_up', 'ffn_w_dw', 'ffn_w_down']
TWIN_WEIGHTS = ['norm_g', 'attn_w_qkv', 'attn_w_o', 'conv_w_in', 'conv_w_dw', 'conv_w_out', 'pool_w_in', 'pool_w_grp', 'pool_scale', 'pool_w_out', 'ffn_w_up', 'ffn_w_dw', 'ffn_w_down']
TWIN_DIFF_INPUT = 'x'
TWIN_INPUTS = ['x', 'norm_g', 'attn_w_qkv', 'attn_w_o', 'conv_w_in', 'conv_w_dw', 'conv_w_out', 'pool_w_in', 'pool_w_grp', 'pool_scale', 'pool_w_out', 'ffn_w_up', 'ffn_w_dw', 'ffn_w_down', 'loss_target', 'm_norm_g', 'm_attn_w_qkv', 'm_attn_w_o', 'm_conv_w_in', 'm_conv_w_dw', 'm_conv_w_out', 'm_pool_w_in', 'm_pool_w_grp', 'm_pool_scale', 'm_pool_w_out', 'm_ffn_w_up', 'm_ffn_w_dw', 'm_ffn_w_down', 'v_norm_g', 'v_attn_w_qkv', 'v_attn_w_o', 'v_conv_w_in', 'v_conv_w_dw', 'v_conv_w_out', 'v_pool_w_in', 'v_pool_w_grp', 'v_pool_scale', 'v_pool_w_out', 'v_ffn_w_up', 'v_ffn_w_dw', 'v_ffn_w_down']
TWIN_OUTPUTS = ['loss', 'grad_x', 'grad_norm_g', 'grad_attn_w_qkv', 'grad_attn_w_o', 'grad_conv_w_in', 'grad_conv_w_dw', 'grad_conv_w_out', 'grad_pool_w_in', 'grad_pool_w_grp', 'grad_pool_scale', 'grad_pool_w_out', 'grad_ffn_w_up', 'grad_ffn_w_dw', 'grad_ffn_w_down', 'delta_norm_g', 'delta_attn_w_qkv', 'delta_attn_w_o', 'delta_conv_w_in', 'delta_conv_w_dw', 'delta_conv_w_out', 'delta_pool_w_in', 'delta_pool_w_grp', 'delta_pool_scale', 'delta_pool_w_out', 'delta_ffn_w_up', 'delta_ffn_w_dw', 'delta_ffn_w_down', 'new_m_norm_g', 'new_m_attn_w_qkv', 'new_m_attn_w_o', 'new_m_conv_w_in', 'new_m_conv_w_dw', 'new_m_conv_w_out', 'new_m_pool_w_in', 'new_m_pool_w_grp', 'new_m_pool_scale', 'new_m_pool_w_out', 'new_m_ffn_w_up', 'new_m_ffn_w_dw', 'new_m_ffn_w_down', 'new_v_norm_g', 'new_v_attn_w_qkv', 'new_v_attn_w_o', 'new_v_conv_w_in', 'new_v_conv_w_dw', 'new_v_conv_w_out', 'new_v_pool_w_in', 'new_v_pool_w_grp', 'new_v_pool_scale', 'new_v_pool_w_out', 'new_v_ffn_w_up', 'new_v_ffn_w_dw', 'new_v_ffn_w_down']
TWIN_LEAF_KINDS = {'loss': 'loss', 'grad_x': 'grad_x', 'grad_norm_g': 'grad_w', 'grad_attn_w_qkv': 'grad_w', 'grad_attn_w_o': 'grad_w', 'grad_conv_w_in': 'grad_w', 'grad_conv_w_dw': 'grad_w', 'grad_conv_w_out': 'grad_w', 'grad_pool_w_in': 'grad_w', 'grad_pool_w_grp': 'grad_w', 'grad_pool_scale': 'grad_w', 'grad_pool_w_out': 'grad_w', 'grad_ffn_w_up': 'grad_w', 'grad_ffn_w_dw': 'grad_w', 'grad_ffn_w_down': 'grad_w', 'delta_norm_g': 'delta_w', 'delta_attn_w_qkv': 'delta_w', 'delta_attn_w_o': 'delta_w', 'delta_conv_w_in': 'delta_w', 'delta_conv_w_dw': 'delta_w', 'delta_conv_w_out': 'delta_w', 'delta_pool_w_in': 'delta_w', 'delta_pool_w_grp': 'delta_w', 'delta_pool_scale': 'delta_w', 'delta_pool_w_out': 'delta_w', 'delta_ffn_w_up': 'delta_w', 'delta_ffn_w_dw': 'delta_w', 'delta_ffn_w_down': 'delta_w', 'new_m_norm_g': 'new_m', 'new_m_attn_w_qkv': 'new_m', 'new_m_attn_w_o': 'new_m', 'new_m_conv_w_in': 'new_m', 'new_m_conv_w_dw': 'new_m', 'new_m_conv_w_out': 'new_m', 'new_m_pool_w_in': 'new_m', 'new_m_pool_w_grp': 'new_m', 'new_m_pool_scale': 'new_m', 'new_m_pool_w_out': 'new_m', 'new_m_ffn_w_up': 'new_m', 'new_m_ffn_w_dw': 'new_m', 'new_m_ffn_w_down': 'new_m', 'new_v_norm_g': 'new_v', 'new_v_attn_w_qkv': 'new_v', 'new_v_attn_w_o': 'new_v', 'new_v_conv_w_in': 'new_v', 'new_v_conv_w_dw': 'new_v', 'new_v_conv_w_out': 'new_v', 'new_v_pool_w_in': 'new_v', 'new_v_pool_w_grp': 'new_v', 'new_v_pool_scale': 'new_v', 'new_v_pool_w_out': 'new_v', 'new_v_ffn_w_up': 'new_v', 'new_v_ffn_w_dw': 'new_v', 'new_v_ffn_w_down': 'new_v'}


def _forward(args):
    return _fwd_reference(*[args[k] for k in FWD_PARAMS])


def _output_shape():
    out = _jax.eval_shape(lambda: _forward(_fwd_setup_inputs(0)))
    return out.shape, out.dtype

N_MICROBATCH = 1
ADAM_LR = 0.001
ADAM_B1 = 0.9
ADAM_B2 = 0.999
ADAM_EPS = 1e-08
ADAM_WD = 0.01
ADAM_STEP = 10
PER_EXAMPLE_BATCH_AXIS = {'x': 0, 'loss_target': 0}
SHARED_INPUTS = []
_WEIGHT_DTYPES = {'norm_g': _jnp.float32, 'attn_w_qkv': _jnp.float32, 'attn_w_o': _jnp.float32, 'conv_w_in': _jnp.float32, 'conv_w_dw': _jnp.float32, 'conv_w_out': _jnp.float32, 'pool_w_in': _jnp.float32, 'pool_w_grp': _jnp.float32, 'pool_scale': _jnp.float32, 'pool_w_out': _jnp.float32, 'ffn_w_up': _jnp.float32, 'ffn_w_dw': _jnp.float32, 'ffn_w_down': _jnp.float32}
MOMENT_SCALE = {'norm_g': 1.103506e+01, 'attn_w_qkv': 8.694782e-01, 'attn_w_o': 1.246676e+00, 'conv_w_in': 7.703825e-01, 'conv_w_dw': 8.021803e-01, 'conv_w_out': 8.235040e-01, 'pool_w_in': 1.071556e+00, 'pool_w_grp': 1.085116e+00, 'pool_scale': 1.200646e+00, 'pool_w_out': 1.087962e+00, 'ffn_w_up': 4.399242e-01, 'ffn_w_dw': 4.514143e-01, 'ffn_w_down': 7.579395e-01}


def _to_microbatches(a, axis):
    t = _jnp.moveaxis(a, axis, 0)
    t = t.reshape((N_MICROBATCH, t.shape[0] // N_MICROBATCH) + t.shape[1:])
    return _jnp.moveaxis(t, 1, axis + 1)


def setup_inputs(seed: int = 0) -> dict:
    inp = _fwd_setup_inputs(seed)
    key = _jax.random.fold_in(_jax.random.key(seed), 7919)
    shape, _ = _output_shape()
    out = dict(inp)
    out["loss_target"] = _jax.random.normal(_jax.random.fold_in(key, 0), shape, _jnp.float32)
    for i, name in enumerate(TWIN_WEIGHTS):
        w = inp[name].astype(_jnp.float32)
        if MOMENT_SCALE is None:
            s = _jnp.sqrt(_jnp.mean(_jnp.square(w)) + 1e-30)
        else:
            s = MOMENT_SCALE[name]
        km, kv = _jax.random.split(_jax.random.fold_in(key, i + 1))
        out[name] = w
        out["m_" + name] = s * _jax.random.normal(km, w.shape, _jnp.float32)
        out["v_" + name] = (s * s) * _jax.random.uniform(kv, w.shape, _jnp.float32, 0.5, 1.5)
    if N_MICROBATCH > 1:
        for name, axis in PER_EXAMPLE_BATCH_AXIS.items():
            out[name] = _to_microbatches(out[name], axis)
    return {'x': out['x'], 'norm_g': out['norm_g'], 'attn_w_qkv': out['attn_w_qkv'], 'attn_w_o': out['attn_w_o'], 'conv_w_in': out['conv_w_in'], 'conv_w_dw': out['conv_w_dw'], 'conv_w_out': out['conv_w_out'], 'pool_w_in': out['pool_w_in'], 'pool_w_grp': out['pool_w_grp'], 'pool_scale': out['pool_scale'], 'pool_w_out': out['pool_w_out'], 'ffn_w_up': out['ffn_w_up'], 'ffn_w_dw': out['ffn_w_dw'], 'ffn_w_down': out['ffn_w_down'], 'loss_target': out['loss_target'], 'm_norm_g': out['m_norm_g'], 'm_attn_w_qkv': out['m_attn_w_qkv'], 'm_attn_w_o': out['m_attn_w_o'], 'm_conv_w_in': out['m_conv_w_in'], 'm_conv_w_dw': out['m_conv_w_dw'], 'm_conv_w_out': out['m_conv_w_out'], 'm_pool_w_in': out['m_pool_w_in'], 'm_pool_w_grp': out['m_pool_w_grp'], 'm_pool_scale': out['m_pool_scale'], 'm_pool_w_out': out['m_pool_w_out'], 'm_ffn_w_up': out['m_ffn_w_up'], 'm_ffn_w_dw': out['m_ffn_w_dw'], 'm_ffn_w_down': out['m_ffn_w_down'], 'v_norm_g': out['v_norm_g'], 'v_attn_w_qkv': out['v_attn_w_qkv'], 'v_attn_w_o': out['v_attn_w_o'], 'v_conv_w_in': out['v_conv_w_in'], 'v_conv_w_dw': out['v_conv_w_dw'], 'v_conv_w_out': out['v_conv_w_out'], 'v_pool_w_in': out['v_pool_w_in'], 'v_pool_w_grp': out['v_pool_w_grp'], 'v_pool_scale': out['v_pool_scale'], 'v_pool_w_out': out['v_pool_w_out'], 'v_ffn_w_up': out['v_ffn_w_up'], 'v_ffn_w_dw': out['v_ffn_w_dw'], 'v_ffn_w_down': out['v_ffn_w_down']}


def _loss(weights, diff, rest, loss_target):
    with _jax.named_scope("forward"):
        args = {**rest, TWIN_DIFF_INPUT: diff, **{k: w.astype(_WEIGHT_DTYPES[k]) for k, w in weights.items()}}
        y = _forward(args)
    with _jax.named_scope("loss_head"):
        err = _jnp.square(y.astype(_jnp.float32) - loss_target)
        return 0.5 * _jnp.sum(_jnp.mean(err, axis=-1)) if err.ndim else 0.5 * err


def _adamw(w, g, m, v):
    m = ADAM_B1 * m + (1.0 - ADAM_B1) * g
    v = ADAM_B2 * v + (1.0 - ADAM_B2) * _jnp.square(g)
    m_hat = m / (1.0 - ADAM_B1 ** ADAM_STEP)
    v_hat = v / (1.0 - ADAM_B2 ** ADAM_STEP)
    delta = -ADAM_LR * (m_hat / (_jnp.sqrt(v_hat) + ADAM_EPS) + ADAM_WD * w)
    return delta, m, v


def reference(x, norm_g, attn_w_qkv, attn_w_o, conv_w_in, conv_w_dw, conv_w_out, pool_w_in, pool_w_grp, pool_scale, pool_w_out, ffn_w_up, ffn_w_dw, ffn_w_down, loss_target, m_norm_g, m_attn_w_qkv, m_attn_w_o, m_conv_w_in, m_conv_w_dw, m_conv_w_out, m_pool_w_in, m_pool_w_grp, m_pool_scale, m_pool_w_out, m_ffn_w_up, m_ffn_w_dw, m_ffn_w_down, v_norm_g, v_attn_w_qkv, v_attn_w_o, v_conv_w_in, v_conv_w_dw, v_conv_w_out, v_pool_w_in, v_pool_w_grp, v_pool_scale, v_pool_w_out, v_ffn_w_up, v_ffn_w_dw, v_ffn_w_down):
    given = dict(x=x, norm_g=norm_g, attn_w_qkv=attn_w_qkv, attn_w_o=attn_w_o, conv_w_in=conv_w_in, conv_w_dw=conv_w_dw, conv_w_out=conv_w_out, pool_w_in=pool_w_in, pool_w_grp=pool_w_grp, pool_scale=pool_scale, pool_w_out=pool_w_out, ffn_w_up=ffn_w_up, ffn_w_dw=ffn_w_dw, ffn_w_down=ffn_w_down, loss_target=loss_target, m_norm_g=m_norm_g, m_attn_w_qkv=m_attn_w_qkv, m_attn_w_o=m_attn_w_o, m_conv_w_in=m_conv_w_in, m_conv_w_dw=m_conv_w_dw, m_conv_w_out=m_conv_w_out, m_pool_w_in=m_pool_w_in, m_pool_w_grp=m_pool_w_grp, m_pool_scale=m_pool_scale, m_pool_w_out=m_pool_w_out, m_ffn_w_up=m_ffn_w_up, m_ffn_w_dw=m_ffn_w_dw, m_ffn_w_down=m_ffn_w_down, v_norm_g=v_norm_g, v_attn_w_qkv=v_attn_w_qkv, v_attn_w_o=v_attn_w_o, v_conv_w_in=v_conv_w_in, v_conv_w_dw=v_conv_w_dw, v_conv_w_out=v_conv_w_out, v_pool_w_in=v_pool_w_in, v_pool_w_grp=v_pool_w_grp, v_pool_scale=v_pool_scale, v_pool_w_out=v_pool_w_out, v_ffn_w_up=v_ffn_w_up, v_ffn_w_dw=v_ffn_w_dw, v_ffn_w_down=v_ffn_w_down)
    weights = {n: given[n] for n in TWIN_WEIGHTS}
    shared = {n: given[n] for n in SHARED_INPUTS}
    per_example = {n: given[n] for n in ['x']}
    grad_fn = _jax.value_and_grad(_loss, argnums=(0, 1))

    def one_microbatch(ex, loss_target):
        ex = dict(ex)
        diff = ex.pop(TWIN_DIFF_INPUT)
        return grad_fn(weights, diff, {**shared, **ex}, loss_target)

    if N_MICROBATCH == 1:
        loss, (grad_w, grad_x) = one_microbatch(per_example, given["loss_target"])
    else:
        def body(carry, xs):
            loss_sum, grad_sum = carry
            l_k, (gw_k, gx_k) = one_microbatch(xs[0], xs[1])
            with _jax.named_scope("update"):
                return (loss_sum + l_k, _jax.tree.map(_jnp.add, grad_sum, gw_k)), gx_k

        init = (_jnp.zeros((), _jnp.float32), _jax.tree.map(_jnp.zeros_like, weights))
        (loss, grad_w), grad_x = _jax.lax.scan(body, init, (per_example, given["loss_target"]))
    with _jax.named_scope("update"):
        delta_w, new_m, new_v = {}, {}, {}
        for n in TWIN_WEIGHTS:
            delta_w[n], new_m[n], new_v[n] = _adamw(weights[n], grad_w[n], given["m_" + n], given["v_" + n])
    return (loss, grad_x, *[grad_w[n] for n in TWIN_WEIGHTS], *[delta_w[n] for n in TWIN_WEIGHTS],
            *[new_m[n] for n in TWIN_WEIGHTS], *[new_v[n] for n in TWIN_WEIGHTS])
```

```python
import functools
import math

import numpy as np
import jax
import jax.numpy as jnp
from jax import lax
from jax.experimental import pallas as pl
from jax.experimental.pallas import tpu as pltpu

F32, BF16 = jnp.float32, jnp.bfloat16

D_MODEL = 1024
SEQ = 2048
DEPTH = 4
DILATED_CFG = ((128, 1), (512, 4), (2048, 16))
N_GROUPS_A = 3
HEADS = 8
HEAD_DIM = 64
ATTN_WIDTH = HEADS * HEAD_DIM
N_HEADS_A = N_GROUPS_A * HEADS
BLOCK = 128
NEG_INF = -1e30
POOL_GROUP_DIM = 256
D_FF = 2816
RMS_EPS = 1e-6
ADAM_LR, ADAM_B1, ADAM_B2, ADAM_EPS, ADAM_WD, ADAM_STEP = 0.001, 0.9, 0.999, 1e-08, 0.01, 10

N_DEV = 8
LANES = 128
V7X_VMEM_BYTES = 64 * 2 ** 20
VMEM_LIMIT_BYTES = V7X_VMEM_BYTES - 8 * 2 ** 20
COL_TILE = 256
ROW_TILE = 256

NN = (((1,), (0,)), ((), ()))
NT = (((1,), (1,)), ((), ()))
TN = (((0,), (0,)), ((), ()))


def _dot(a, b, dims=NN):
    return lax.dot_general(a, b, dims, preferred_element_type=F32)


def _params(sem=None):
    return pltpu.CompilerParams(dimension_semantics=sem, vmem_limit_bytes=VMEM_LIMIT_BYTES)


def _pick(n, prefs):
    for p in prefs:
        if n % p == 0:
            return p
    return n


def _matmul(a, b, mode, out_dtype, name, a_parts=1, b_parts=1):
    if mode == "nn":
        m, k = a.shape[-2], a.shape[-1] * a_parts
        n = b.shape[-1] * b_parts
    elif mode == "nt":
        m, k = a.shape[-2], a.shape[-1] * a_parts
        n = b.shape[-2]
    else:
        k, m = a.shape[-2], a.shape[-1] * a_parts
        n = b.shape[-1] * b_parts
    tm = _pick(m, (1024, 512, 256, 128))
    tn = _pick(n // b_parts if mode != "nt" else n, (1024, 512, 256, 128))
    kk = k // a_parts if mode != "tn" else k
    tk = _pick(kk, (1024, 512, 256, 128))
    if mode == "tn":
        tm = _pick(m // a_parts, (1024, 512, 256, 128))
    gm, gn, gk = m // tm, n // tn, k // tk

    def a_idx(i, j, kq):
        if mode == "tn":
            r, c, per = kq, i, (m // a_parts) // tm
        else:
            r, c, per = i, kq, (k // a_parts) // tk
        return (r, c) if a_parts == 1 else (c // per, r, c % per)

    def b_idx(i, j, kq):
        if mode == "nt":
            return (j, kq)
        per = (n // b_parts) // tn
        return (kq, j) if b_parts == 1 else (j // per, kq, j % per)

    a_blk = (tk, tm) if mode == "tn" else (tm, tk)
    b_blk = (tn, tk) if mode == "nt" else (tk, tn)
    if a_parts > 1:
        a_blk = (None,) + a_blk
    if b_parts > 1:
        b_blk = (None,) + b_blk
    dims = {"nn": NN, "nt": NT, "tn": TN}[mode]

    def body(a_ref, b_ref, o_ref, acc_ref):
        kq = pl.program_id(2)

        @pl.when(kq == 0)
        def _():
            acc_ref[...] = jnp.zeros_like(acc_ref)

        acc_ref[...] += _dot(a_ref[...], b_ref[...], dims)

        @pl.when(kq == gk - 1)
        def _():
            o_ref[...] = acc_ref[...].astype(o_ref.dtype)

    return pl.pallas_call(
        body, name=name, grid=(gm, gn, gk),
        in_specs=[pl.BlockSpec(a_blk, a_idx), pl.BlockSpec(b_blk, b_idx)],
        out_specs=pl.BlockSpec((tm, tn), lambda i, j, kq: (i, j)),
        out_shape=jax.ShapeDtypeStruct((m, n), out_dtype),
        scratch_shapes=[pltpu.VMEM((tm, tn), F32)],
        compiler_params=_params(("parallel", "parallel", "arbitrary")),
    )(a, b)


def _rms_fwd(xin, g, res, out_dtype, name):
    s, d = xin.shape
    has_res = res is not None

    def body(*refs):
        x_ref, g_ref = refs[0], refs[1]
        o_ref = refs[-1]
        x = x_ref[...]
        r = lax.rsqrt(jnp.mean(x * x, axis=-1, keepdims=True) + RMS_EPS)
        y = x * r * g_ref[...]
        if has_res:
            y = refs[2][...] + y
        o_ref[...] = y.astype(o_ref.dtype)

    row = pl.BlockSpec((ROW_TILE, d), lambda i: (i, 0))
    vec = pl.BlockSpec((1, d), lambda i: (0, 0))
    ins = [xin, g] + ([res] if has_res else [])
    return pl.pallas_call(
        body, name=name, grid=(s // ROW_TILE,),
        in_specs=[row, vec] + ([row] if has_res else []),
        out_specs=row, out_shape=jax.ShapeDtypeStruct((s, d), out_dtype),
        compiler_params=_params(("parallel",)),
    )(*ins)


def _rms_bwd(xin, g, dy, dres, out_dtype, name):
    s, d = xin.shape
    has_res = dres is not None

    def body(*refs):
        x_ref, g_ref, dy_ref = refs[0], refs[1], refs[2]
        dx_ref, dg_ref = refs[-2], refs[-1]

        @pl.when(pl.program_id(0) == 0)
        def _():
            dg_ref[...] = jnp.zeros_like(dg_ref)

        x = x_ref[...]
        dyv = dy_ref[...].astype(F32)
        r = lax.rsqrt(jnp.mean(x * x, axis=-1, keepdims=True) + RMS_EPS)
        xhat = x * r
        u = dyv * g_ref[...]
        dx = r * (u - xhat * jnp.mean(u * xhat, axis=-1, keepdims=True))
        if has_res:
            dx = refs[3][...] + dx
        dx_ref[...] = dx.astype(dx_ref.dtype)
        dg_ref[...] += jnp.sum(dyv * xhat, axis=0, keepdims=True)

    row = pl.BlockSpec((ROW_TILE, d), lambda i: (i, 0))
    vec = pl.BlockSpec((1, d), lambda i: (0, 0))
    ins = [xin, g, dy] + ([dres] if has_res else [])
    return pl.pallas_call(
        body, name=name, grid=(s // ROW_TILE,),
        in_specs=[row, vec, row] + ([row] if has_res else []),
        out_specs=[row, vec],
        out_shape=[jax.ShapeDtypeStruct((s, d), out_dtype), jax.ShapeDtypeStruct((1, d), F32)],
        compiler_params=_params(("arbitrary",)),
    )(*ins)


def _loss_head(y, tgt, name):
    s, d = y.shape

    def body(y_ref, t_ref, l_ref, dy_ref):
        @pl.when(pl.program_id(0) == 0)
        def _():
            l_ref[...] = jnp.zeros_like(l_ref)

        e = y_ref[...] - t_ref[...]
        dy_ref[...] = e / d
        per_tok = jnp.mean(e * e, axis=-1, keepdims=True)
        l_ref[...] += 0.5 * jnp.sum(per_tok, axis=0, keepdims=True)

    row = pl.BlockSpec((ROW_TILE, d), lambda i: (i, 0))
    return pl.pallas_call(
        body, name=name, grid=(s // ROW_TILE,),
        in_specs=[row, row],
        out_specs=[pl.BlockSpec((1, 1), lambda i: (0, 0)), row],
        out_shape=[jax.ShapeDtypeStruct((1, 1), F32), jax.ShapeDtypeStruct((s, d), F32)],
        compiler_params=_params(("arbitrary",)),
    )(y, tgt)


def _shift_down(x, k):
    rows = lax.broadcasted_iota(jnp.int32, x.shape, 0)
    return jnp.where(rows >= k, pltpu.roll(x, k, axis=0), 0.0)


def _shift_up(x, k):
    t = x.shape[0]
    rows = lax.broadcasted_iota(jnp.int32, x.shape, 0)
    return jnp.where(rows < t - k, pltpu.roll(x, t - k, axis=0), 0.0)


def _conv3(h, w):
    return w[2:3] * h + w[1:2] * _shift_down(h, 1) + w[0:1] * _shift_down(h, 2)


def _conv3_bwd_x(dc, w):
    return w[2:3] * dc + w[1:2] * _shift_up(dc, 1) + w[0:1] * _shift_up(dc, 2)


def _conv3_bwd_w(dc, h, dw_ref):
    dw_ref[0:1, :] = jnp.sum(dc * _shift_down(h, 2), axis=0, keepdims=True)
    dw_ref[1:2, :] = jnp.sum(dc * _shift_down(h, 1), axis=0, keepdims=True)
    dw_ref[2:3, :] = jnp.sum(dc * h, axis=0, keepdims=True)


def _ffn_up(n, wup, wdw, name):
    s, d = n.shape
    f = wup.shape[1] // 2
    tn = COL_TILE
    nj = f // tn

    def body(n_ref, wg_ref, wu_ref, dg_ref, du_ref, h_ref, a_ref):
        x = n_ref[...]
        hg = _dot(x, wg_ref[...])
        hu = _dot(x, wu_ref[...])
        h_ref[0] = hg.astype(BF16)
        h_ref[1] = hu.astype(BF16)
        cg = _conv3(hg, dg_ref[...])
        cu = _conv3(hu, du_ref[...])
        a_ref[...] = (cg * jax.nn.sigmoid(cg) * cu).astype(BF16)

    return pl.pallas_call(
        body, name=name, grid=(nj,),
        in_specs=[pl.BlockSpec((s, d), lambda j: (0, 0)),
                  pl.BlockSpec((d, tn), lambda j: (0, j)), pl.BlockSpec((d, tn), lambda j: (0, j + nj)),
                  pl.BlockSpec((3, tn), lambda j: (0, j)), pl.BlockSpec((3, tn), lambda j: (0, j + nj))],
        out_specs=[pl.BlockSpec((2, s, tn), lambda j: (0, 0, j)), pl.BlockSpec((s, tn), lambda j: (0, j))],
        out_shape=[jax.ShapeDtypeStruct((2, s, f), BF16), jax.ShapeDtypeStruct((s, f), BF16)],
        compiler_params=_params(("parallel",)),
    )(n, wup, wup, wdw, wdw)


def _ffn_mid_bwd(do, wdown, h, wdw, name):
    s, d = do.shape
    f = wdown.shape[0]
    tn = COL_TILE
    nj = f // tn

    def body(do_ref, wd_ref, h_ref, wg_ref, wu_ref, dh_ref, dwg_ref, dwu_ref):
        da = _dot(do_ref[...], wd_ref[...], NT)
        hg = h_ref[0].astype(F32)
        hu = h_ref[1].astype(F32)
        wg, wu = wg_ref[...], wu_ref[...]
        cg = _conv3(hg, wg)
        cu = _conv3(hu, wu)
        sg = jax.nn.sigmoid(cg)
        dcu = da * (cg * sg)
        dcg = da * cu * (sg * (1.0 + cg * (1.0 - sg)))
        dh_ref[0] = _conv3_bwd_x(dcg, wg).astype(BF16)
        dh_ref[1] = _conv3_bwd_x(dcu, wu).astype(BF16)
        _conv3_bwd_w(dcg, hg, dwg_ref)
        _conv3_bwd_w(dcu, hu, dwu_ref)

    return pl.pallas_call(
        body, name=name, grid=(nj,),
        in_specs=[pl.BlockSpec((s, d), lambda j: (0, 0)), pl.BlockSpec((tn, d), lambda j: (j, 0)),
                  pl.BlockSpec((2, s, tn), lambda j: (0, 0, j)),
                  pl.BlockSpec((3, tn), lambda j: (0, j)), pl.BlockSpec((3, tn), lambda j: (0, j + nj))],
        out_specs=[pl.BlockSpec((2, s, tn), lambda j: (0, 0, j)),
                   pl.BlockSpec((3, tn), lambda j: (0, j)), pl.BlockSpec((3, tn), lambda j: (0, j))],
        out_shape=[jax.ShapeDtypeStruct((2, s, f), BF16), jax.ShapeDtypeStruct((3, f), F32),
                   jax.ShapeDtypeStruct((3, f), F32)],
        compiler_params=_params(("parallel",)),
    )(do, wdown, h, wdw, wdw)


def _sconv_fwd(n, win, wdw, name):
    s, d = n.shape
    tn = COL_TILE
    nj = d // tn

    def body(n_ref, wb_ref, wc_ref, wh_ref, dw_ref, z_ref, y_ref):
        x = n_ref[...]
        zb = _dot(x, wb_ref[...])
        zc = _dot(x, wc_ref[...])
        zh = _dot(x, wh_ref[...])
        z_ref[0] = zb.astype(BF16)
        z_ref[1] = zc.astype(BF16)
        z_ref[2] = zh.astype(BF16)
        y_ref[...] = (zb * _conv3(zc * zh, dw_ref[...])).astype(BF16)

    return pl.pallas_call(
        body, name=name, grid=(nj,),
        in_specs=[pl.BlockSpec((s, d), lambda j: (0, 0)),
                  pl.BlockSpec((d, tn), lambda j: (0, j)), pl.BlockSpec((d, tn), lambda j: (0, j + nj)),
                  pl.BlockSpec((d, tn), lambda j: (0, j + 2 * nj)), pl.BlockSpec((3, tn), lambda j: (0, j))],
        out_specs=[pl.BlockSpec((3, s, tn), lambda j: (0, 0, j)), pl.BlockSpec((s, tn), lambda j: (0, j))],
        out_shape=[jax.ShapeDtypeStruct((3, s, d), BF16), jax.ShapeDtypeStruct((s, d), BF16)],
        compiler_params=_params(("parallel",)),
    )(n, win, win, win, wdw)


def _sconv_mid_bwd(dm, wout, z, wdw, name):
    s, d = dm.shape
    tn = COL_TILE
    nj = d // tn

    def body(dm_ref, wo_ref, z_ref, w_ref, dz_ref, dw_ref):
        dy = _dot(dm_ref[...], wo_ref[...], NT)
        zb = z_ref[0].astype(F32)
        zc = z_ref[1].astype(F32)
        zh = z_ref[2].astype(F32)
        w = w_ref[...]
        p = zc * zh
        cp = _conv3(p, w)
        dz_ref[0] = (dy * cp).astype(BF16)
        dcp = dy * zb
        dp = _conv3_bwd_x(dcp, w)
        _conv3_bwd_w(dcp, p, dw_ref)
        dz_ref[1] = (dp * zh).astype(BF16)
        dz_ref[2] = (dp * zc).astype(BF16)

    return pl.pallas_call(
        body, name=name, grid=(nj,),
        in_specs=[pl.BlockSpec((s, d), lambda j: (0, 0)), pl.BlockSpec((tn, d), lambda j: (j, 0)),
                  pl.BlockSpec((3, s, tn), lambda j: (0, 0, j)), pl.BlockSpec((3, tn), lambda j: (0, j))],
        out_specs=[pl.BlockSpec((3, s, tn), lambda j: (0, 0, j)), pl.BlockSpec((3, tn), lambda j: (0, j))],
        out_shape=[jax.ShapeDtypeStruct((3, s, d), BF16), jax.ShapeDtypeStruct((3, d), F32)],
        compiler_params=_params(("parallel",)),
    )(dm, wout, z, wdw)


def _pool_select(g, c2, c4, c8, c16):
    return jnp.where(g == 0, c2, jnp.where(g == 1, c4, jnp.where(g == 2, c8, c16)))


def _pool_inv_count(g, shape):
    pos = lax.broadcasted_iota(jnp.int32, shape, 0).astype(F32) + 1.0
    win = (2 << g).astype(F32)
    return jnp.minimum(pos, win)


def _pool_fwd(n, win, wgrp, scale, name):
    s, d = n.shape
    tn = POOL_GROUP_DIM

    def body(n_ref, wi_ref, wg_ref, sc_ref, p_ref, y_ref):
        g = pl.program_id(0)
        u = _dot(n_ref[...], wi_ref[...])
        s2 = u + _shift_down(u, 1)
        s4 = s2 + _shift_down(s2, 2)
        s8 = s4 + _shift_down(s4, 4)
        s16 = s8 + _shift_down(s8, 8)
        tot = _pool_select(g, s2, s4, s8, s16)
        p = (tot / _pool_inv_count(g, u.shape) - u).astype(BF16)
        p_ref[...] = p
        y_ref[...] = (_dot(p, wg_ref[...]) * sc_ref[...]).astype(BF16)

    return pl.pallas_call(
        body, name=name, grid=(d // tn,),
        in_specs=[pl.BlockSpec((s, d), lambda g: (0, 0)), pl.BlockSpec((d, tn), lambda g: (0, g)),
                  pl.BlockSpec((None, tn, tn), lambda g: (g, 0, 0)), pl.BlockSpec((1, tn), lambda g: (0, g))],
        out_specs=[pl.BlockSpec((s, tn), lambda g: (0, g)), pl.BlockSpec((s, tn), lambda g: (0, g))],
        out_shape=[jax.ShapeDtypeStruct((s, d), BF16), jax.ShapeDtypeStruct((s, d), BF16)],
        compiler_params=_params(("parallel",)),
    )(n, win, wgrp, scale)


def _pool_mid_bwd(dm, wout, p, wgrp, scale, name):
    s, d = dm.shape
    tn = POOL_GROUP_DIM

    def body(dm_ref, wo_ref, p_ref, wg_ref, sc_ref, du_ref, dwg_ref, dsc_ref):
        g = pl.program_id(0)
        dy = _dot(dm_ref[...], wo_ref[...], NT)
        pv = p_ref[...]
        wg = wg_ref[...]
        ypre = _dot(pv, wg)
        dsc_ref[...] = jnp.sum(dy * ypre, axis=0, keepdims=True)
        dypre = (dy * sc_ref[...]).astype(BF16)
        dwg_ref[...] = _dot(pv, dypre, TN)
        dp = _dot(dypre, wg, NT)
        e = dp / _pool_inv_count(g, dp.shape)
        f2 = e + _shift_up(e, 1)
        f4 = f2 + _shift_up(f2, 2)
        f8 = f4 + _shift_up(f4, 4)
        f16 = f8 + _shift_up(f8, 8)
        du_ref[...] = (_pool_select(g, f2, f4, f8, f16) - dp).astype(BF16)

    return pl.pallas_call(
        body, name=name, grid=(d // tn,),
        in_specs=[pl.BlockSpec((s, d), lambda g: (0, 0)), pl.BlockSpec((tn, d), lambda g: (g, 0)),
                  pl.BlockSpec((s, tn), lambda g: (0, g)), pl.BlockSpec((None, tn, tn), lambda g: (g, 0, 0)),
                  pl.BlockSpec((1, tn), lambda g: (0, g))],
        out_specs=[pl.BlockSpec((s, tn), lambda g: (0, g)), pl.BlockSpec((None, tn, tn), lambda g: (g, 0, 0)),
                   pl.BlockSpec((1, tn), lambda g: (0, g))],
        out_shape=[jax.ShapeDtypeStruct((s, d), BF16), jax.ShapeDtypeStruct((4, tn, tn), F32),
                   jax.ShapeDtypeStruct((1, d), F32)],
        compiler_params=_params(("parallel",)),
    )(dm, wout, p, wgrp, scale)


def _alibi_slopes():
    return jnp.asarray(2.0 ** (-8.0 * np.arange(1, N_HEADS_A + 1) / N_HEADS_A), F32)


def _attn_scores(q, kc, kp, slope, n):
    qi = lax.broadcasted_iota(jnp.int32, (BLOCK, BLOCK), 0)
    kj = lax.broadcasted_iota(jnp.int32, (BLOCK, BLOCK), 1)
    dist = (qi - kj).astype(F32)
    sc = _dot(q, kc, NT) * (HEAD_DIM ** -0.5) - slope * dist
    sp = _dot(q, kp, NT) * (HEAD_DIM ** -0.5) - slope * (dist + BLOCK)
    sc = jnp.where(kj <= qi, sc, NEG_INF)
    sp = jnp.where((kj >= qi) & (n > 0), sp, NEG_INF)
    return sc, sp


def _attn_specs(hh, dil, ln):
    def part(t):
        return pl.BlockSpec((None, None, None, ln, HEAD_DIM), lambda h, r: (t, h, r, 0, 0))
    row = pl.BlockSpec((None, None, ln, HEAD_DIM), lambda h, r: (h, r, 0, 0))
    stat = pl.BlockSpec((None, None, ln, 1), lambda h, r: (h, r, 0, 0))
    return part, row, stat


def _attn_fwd(qkv, slopes, dil, name):
    _, hh, _, ln, _ = qkv.shape
    nb = ln // BLOCK

    def body(sl_ref, q_ref, k_ref, v_ref, o_ref, lse_ref):
        slope = sl_ref[pl.program_id(0)] * float(dil)

        def blk(n, carry):
            q0 = pl.multiple_of(n * BLOCK, BLOCK)
            p0 = pl.multiple_of(jnp.maximum(n - 1, 0) * BLOCK, BLOCK)
            cur, prev = pl.ds(q0, BLOCK), pl.ds(p0, BLOCK)
            sc, sp = _attn_scores(q_ref[cur, :], k_ref[cur, :], k_ref[prev, :], slope, n)
            m = jnp.maximum(jnp.max(sc, axis=-1, keepdims=True), jnp.max(sp, axis=-1, keepdims=True))
            pc = jnp.exp(sc - m)
            pp = jnp.exp(sp - m)
            den = jnp.sum(pc, axis=-1, keepdims=True) + jnp.sum(pp, axis=-1, keepdims=True)
            o = _dot(pc.astype(BF16), v_ref[cur, :]) + _dot(pp.astype(BF16), v_ref[prev, :])
            o_ref[cur, :] = o / den
            lse_ref[cur, :] = m + jnp.log(den)
            return carry

        lax.fori_loop(0, nb, blk, 0)

    part, row, stat = _attn_specs(hh, dil, ln)
    return pl.pallas_call(
        body, name=name, grid=(hh, dil),
        in_specs=[pl.BlockSpec(memory_space=pltpu.SMEM), part(0), part(1), part(2)],
        out_specs=[row, stat],
        out_shape=[jax.ShapeDtypeStruct((hh, dil, ln, HEAD_DIM), F32), jax.ShapeDtypeStruct((hh, dil, ln, 1), F32)],
        compiler_params=_params(("parallel", "parallel")),
    )(slopes, qkv, qkv, qkv)


def _attn_bwd(qkv, do, lse, dvec, slopes, dil, name):
    _, hh, _, ln, _ = qkv.shape
    nb = ln // BLOCK
    scale = HEAD_DIM ** -0.5

    def body(sl_ref, q_ref, k_ref, v_ref, do_ref, lse_ref, dd_ref, dq_ref, dk_ref, dv_ref):
        slope = sl_ref[pl.program_id(0)] * float(dil)
        dk_ref[...] = jnp.zeros_like(dk_ref)
        dv_ref[...] = jnp.zeros_like(dv_ref)

        def blk(n, carry):
            q0 = pl.multiple_of(n * BLOCK, BLOCK)
            p0 = pl.multiple_of(jnp.maximum(n - 1, 0) * BLOCK, BLOCK)
            cur, prev = pl.ds(q0, BLOCK), pl.ds(p0, BLOCK)
            q, kc, kp = q_ref[cur, :], k_ref[cur, :], k_ref[prev, :]
            sc, sp = _attn_scores(q, kc, kp, slope, n)
            lse_b = lse_ref[cur, :]
            pc = jnp.exp(sc - lse_b)
            pp = jnp.exp(sp - lse_b)
            dob = do_ref[cur, :]
            dd = dd_ref[cur, :]
            dsc = (pc * (_dot(dob, v_ref[cur, :], NT) - dd)).astype(BF16)
            dsp = (pp * (_dot(dob, v_ref[prev, :], NT) - dd)).astype(BF16)
            dq_ref[cur, :] = scale * (_dot(dsc, kc) + _dot(dsp, kp))
            dv_ref[cur, :] += _dot(pc.astype(BF16), dob, TN)
            dk_ref[cur, :] += scale * _dot(dsc, q, TN)
            dv_ref[prev, :] += _dot(pp.astype(BF16), dob, TN)
            dk_ref[prev, :] += scale * _dot(dsp, q, TN)
            return carry

        lax.fori_loop(0, nb, blk, 0)

    part, row, stat = _attn_specs(hh, dil, ln)
    shp = jax.ShapeDtypeStruct((hh, dil, ln, HEAD_DIM), F32)
    return pl.pallas_call(
        body, name=name, grid=(hh, dil),
        in_specs=[pl.BlockSpec(memory_space=pltpu.SMEM), part(0), part(1), part(2), row, stat, stat],
        out_specs=[row, row, row], out_shape=[shp, shp, shp],
        compiler_params=_params(("parallel", "parallel")),
    )(slopes, qkv, qkv, qkv, do, lse, dvec)


def _attn_merge(o, lse, name):
    _, s, w = o.shape

    def body(o_ref, l_ref, m_ref, mb_ref, lse_ref):
        l = l_ref[...]
        mx = jnp.max(l, axis=0)
        e = jnp.exp(l - mx[None])
        z = jnp.sum(e, axis=0)
        lse_ref[...] = mx + jnp.log(z)
        wts = e / z[None]
        for h in range(HEADS):
            cols = slice(h * HEAD_DIM, (h + 1) * HEAD_DIM)
            acc = wts[0][:, h:h + 1] * o_ref[0, :, cols]
            for g in range(1, N_GROUPS_A):
                acc = acc + wts[g][:, h:h + 1] * o_ref[g, :, cols]
            m_ref[:, cols] = acc
            mb_ref[:, cols] = acc.astype(BF16)

    return pl.pallas_call(
        body, name=name, grid=(s // ROW_TILE,),
        in_specs=[pl.BlockSpec((N_GROUPS_A, ROW_TILE, w), lambda i: (0, i, 0)),
                  pl.BlockSpec((N_GROUPS_A, ROW_TILE, HEADS), lambda i: (0, i, 0))],
        out_specs=[pl.BlockSpec((ROW_TILE, w), lambda i: (i, 0)), pl.BlockSpec((ROW_TILE, w), lambda i: (i, 0)),
                   pl.BlockSpec((ROW_TILE, HEADS), lambda i: (i, 0))],
        out_shape=[jax.ShapeDtypeStruct((s, w), F32), jax.ShapeDtypeStruct((s, w), BF16),
                   jax.ShapeDtypeStruct((s, HEADS), F32)],
        compiler_params=_params(("parallel",)),
    )(o, lse)


def _attn_dvec(dmerged, merged, name):
    s, w = merged.shape

    def body(dm_ref, m_ref, dd_ref, dmb_ref):
        dmv = dm_ref[...]
        dmb_ref[...] = dmv.astype(BF16)
        prod = dmv * m_ref[...]
        for h in range(HEADS):
            dd_ref[:, h:h + 1] = jnp.sum(prod[:, h * HEAD_DIM:(h + 1) * HEAD_DIM], axis=-1, keepdims=True)

    row = pl.BlockSpec((ROW_TILE, w), lambda i: (i, 0))
    return pl.pallas_call(
        body, name=name, grid=(s // ROW_TILE,),
        in_specs=[row, row],
        out_specs=[pl.BlockSpec((ROW_TILE, HEADS), lambda i: (i, 0)), row],
        out_shape=[jax.ShapeDtypeStruct((s, HEADS), F32), jax.ShapeDtypeStruct((s, w), BF16)],
        compiler_params=_params(("parallel",)),
    )(dmerged, merged)


def _to_group(a, dil):
    s, w = a.shape
    c = w // HEADS
    return a.reshape(s // dil, dil, HEADS, c).transpose(2, 1, 0, 3)


def _from_group(a):
    hh, dil, ln, c = a.shape
    return a.transpose(2, 1, 0, 3).reshape(ln * dil, hh * c)


def _attention_fwd(n, wqkv, wo, tag):
    s = n.shape[0]
    qkv = _matmul(n, wqkv, "nn", BF16, f"{tag}_qkv")
    slopes = _alibi_slopes()
    qkv6 = qkv.reshape(s, N_GROUPS_A, 3, HEADS, HEAD_DIM)
    groups, outs, lses = [], [], []
    for g, (_, dil) in enumerate(DILATED_CFG):
        qg = qkv6[:, g].reshape(s // dil, dil, 3, HEADS, HEAD_DIM).transpose(2, 3, 1, 0, 4)
        o, lse = _attn_fwd(qg, slopes[g * HEADS:(g + 1) * HEADS], dil, f"{tag}_fwd_g{g}")
        groups.append(qg)
        outs.append(_from_group(o))
        lses.append(_from_group(lse))
    merged, merged_bf, lse_all = _attn_merge(jnp.stack(outs), jnp.stack(lses), f"{tag}_merge")
    m = _matmul(merged_bf, wo, "nn", F32, f"{tag}_wo")
    return m, (groups, merged, merged_bf, lse_all)


def _attention_bwd(dm, n, wqkv, wo, saved, tag):
    groups, merged, merged_bf, lse_all = saved
    slopes = _alibi_slopes()
    d_wo = _matmul(merged_bf, dm, "tn", F32, f"{tag}_dwo")
    dmerged = _matmul(dm, wo, "nt", F32, f"{tag}_dmerged")
    dvec, dmerged_bf = _attn_dvec(dmerged, merged, f"{tag}_dvec")
    cols = []
    for g, (_, dil) in enumerate(DILATED_CFG):
        dq, dk, dv = _attn_bwd(groups[g], _to_group(dmerged_bf, dil), _to_group(lse_all, dil),
                               _to_group(dvec, dil), slopes[g * HEADS:(g + 1) * HEADS], dil, f"{tag}_bwd_g{g}")
        cols += [_from_group(dq), _from_group(dk), _from_group(dv)]
    dqkv = jnp.concatenate(cols, axis=1).astype(BF16)
    d_wqkv = _matmul(n, dqkv, "tn", F32, f"{tag}_dwqkv")
    dn = _matmul(dqkv, wqkv, "nt", F32, f"{tag}_dn")
    return dn, d_wqkv, d_wo


def _local_step(x, tgt, w):
    ng = w["norm_g"]

    def gain(i, j):
        return ng[i, j][None, :]

    saved = []
    ia = ib = ic = 0
    n = _rms_fwd(x, gain(0, 0), None, BF16, "norm_first")
    for i in range(DEPTH):
        kind = i % 3
        if kind == 0:
            m, ms = _attention_fwd(n, w["attn_w_qkv"][ia], w["attn_w_o"][ia], "attn")
            mixer = ("a", ia, ms)
            ia += 1
        elif kind == 1:
            z, y = _sconv_fwd(n, w["conv_w_in"][ib], w["conv_w_dw"][ib], "sconv_fwd")
            m = _matmul(y, w["conv_w_out"][ib], "nn", F32, "sconv_out")
            mixer = ("b", ib, (z, y))
            ib += 1
        else:
            p, y = _pool_fwd(n, w["pool_w_in"][ic], w["pool_w_grp"][ic], w["pool_scale"][ic][None, :], "pool_fwd")
            m = _matmul(y, w["pool_w_out"][ic], "nn", F32, "pool_out")
            mixer = ("c", ic, (p, y))
            ic += 1
        x1 = _rms_fwd(m, gain(i, 1), x, F32, "norm_res")
        n2 = _rms_fwd(x1, gain(i, 2), None, BF16, "norm_pre")
        h, a = _ffn_up(n2, w["ffn_w_up"][i], w["ffn_w_dw"][i], "ffn_up")
        f = _matmul(a, w["ffn_w_down"][i], "nn", F32, "ffn_down")
        x2 = _rms_fwd(f, gain(i, 3), x1, F32, "norm_res")
        saved.append((x, n, m, mixer, x1, n2, h, a, f))
        x = x2
        if i + 1 < DEPTH:
            n = _rms_fwd(x, gain(i + 1, 0), None, BF16, "norm_pre")

    loss, dx = _loss_head(x, tgt, "loss_head")

    g_norm = [[None] * 4 for _ in range(DEPTH)]
    grads = {k: [] for k in ("attn_w_qkv", "attn_w_o", "conv_w_in", "conv_w_dw", "conv_w_out", "pool_w_in",
                             "pool_w_grp", "pool_scale", "pool_w_out", "ffn_w_up", "ffn_w_dw", "ffn_w_down")}
    for i in reversed(range(DEPTH)):
        xin, n, m, mixer, x1, n2, h, a, f = saved[i]
        df, g_norm[i][3] = _rms_bwd(f, gain(i, 3), dx, None, BF16, "norm_bwd_sub")
        grads["ffn_w_down"].append(_matmul(a, df, "tn", F32, "ffn_dwdown"))
        dh, dwg, dwu = _ffn_mid_bwd(df, w["ffn_w_down"][i], h, w["ffn_w_dw"][i], "ffn_mid_bwd")
        grads["ffn_w_dw"].append(jnp.concatenate([dwg, dwu], axis=1))
        grads["ffn_w_up"].append(_matmul(n2, dh, "tn", F32, "ffn_dwup", b_parts=2))
        dn2 = _matmul(dh, w["ffn_w_up"][i], "nt", F32, "ffn_dn", a_parts=2)
        dx1, g_norm[i][2] = _rms_bwd(x1, gain(i, 2), dn2, dx, F32, "norm_bwd_res")
        dm, g_norm[i][1] = _rms_bwd(m, gain(i, 1), dx1, None, BF16, "norm_bwd_sub")
        kind, idx, ms = mixer
        if kind == "a":
            dn, d_wqkv, d_wo = _attention_bwd(dm, n, w["attn_w_qkv"][idx], w["attn_w_o"][idx], ms, "attn")
            grads["attn_w_qkv"].append(d_wqkv)
            grads["attn_w_o"].append(d_wo)
        elif kind == "b":
            z, y = ms
            grads["conv_w_out"].append(_matmul(y, dm, "tn", F32, "sconv_dwout"))
            dz, ddw = _sconv_mid_bwd(dm, w["conv_w_out"][idx], z, w["conv_w_dw"][idx], "sconv_mid_bwd")
            grads["conv_w_dw"].append(ddw)
            grads["conv_w_in"].append(_matmul(n, dz, "tn", F32, "sconv_dwin", b_parts=3))
            dn = _matmul(dz, w["conv_w_in"][idx], "nt", F32, "sconv_dn", a_parts=3)
        else:
            p, y = ms
            grads["pool_w_out"].append(_matmul(y, dm, "tn", F32, "pool_dwout"))
            du, dwgrp, dscale = _pool_mid_bwd(dm, w["pool_w_out"][idx], p, w["pool_w_grp"][idx],
                                              w["pool_scale"][idx][None, :], "pool_mid_bwd")
            grads["pool_w_grp"].append(dwgrp)
            grads["pool_scale"].append(dscale[0])
            grads["pool_w_in"].append(_matmul(n, du, "tn", F32, "pool_dwin"))
            dn = _matmul(du, w["pool_w_in"][idx], "nt", F32, "pool_dn")
        dx, g_norm[i][0] = _rms_bwd(xin, gain(i, 0), dn, dx1, F32, "norm_bwd_res")

    out = {k: jnp.stack(v[::-1]) for k, v in grads.items()}
    out["norm_g"] = jnp.stack([jnp.concatenate(row, axis=0) for row in g_norm])
    return loss, dx, out


_AXES = ("x", "y", "c")


def _exchange(ops, name):
    n = len(ops)

    def body(*refs):
        srcs, outs = refs[:n], refs[n:2 * n]
        send_sems, recv_sems, local_sems = refs[2 * n:]
        pos = {a: lax.axis_index(a) for a in _AXES}
        remote, local = [], []
        for i, (_, axis, mode) in enumerate(ops):
            me = pos[axis]
            peer = tuple(1 - pos[a] if a == axis else pos[a] for a in _AXES)
            if mode == "gather":
                lc = pltpu.make_async_copy(srcs[i], outs[i].at[me], local_sems.at[i])
                lc.start()
                local.append(lc)
                src, dst = srcs[i], outs[i].at[me]
            else:
                src, dst = srcs[i].at[1 - me], outs[i]
            rc = pltpu.make_async_remote_copy(src_ref=src, dst_ref=dst, send_sem=send_sems.at[i],
                                              recv_sem=recv_sems.at[i], device_id=peer,
                                              device_id_type=pl.DeviceIdType.MESH)
            rc.start()
            remote.append(rc)
        for rc in remote:
            rc.wait_send()
            rc.wait_recv()
        for lc in local:
            lc.wait()

    out_shape = [jax.ShapeDtypeStruct((2,) + a.shape if mode == "gather" else a.shape[1:], a.dtype)
                 for a, _, mode in ops]
    hbm = pl.BlockSpec(memory_space=pl.ANY)
    return pl.pallas_call(
        body, name=name, in_specs=[hbm] * n, out_specs=[hbm] * n, out_shape=out_shape,
        scratch_shapes=[pltpu.SemaphoreType.DMA((n,)), pltpu.SemaphoreType.DMA((n,)), pltpu.SemaphoreType.DMA((n,))],
    )(*[a for a, _, _ in ops])


ADD_ROWS = 4096


def _add_half(a2, recv, me, out_dtype, name):
    n = recv.shape[0]

    def body(me_ref, a_ref, b_ref, o_ref):
        o_ref[...] = (a_ref[...].astype(F32) + b_ref[...].astype(F32)).astype(o_ref.dtype)

    return pl.pallas_call(
        body, name=name,
        grid_spec=pltpu.PrefetchScalarGridSpec(
            num_scalar_prefetch=1, grid=(n // ADD_ROWS,),
            in_specs=[pl.BlockSpec((None, ADD_ROWS, LANES), lambda i, m: (m[0], i, 0)),
                      pl.BlockSpec((ADD_ROWS, LANES), lambda i, m: (i, 0))],
            out_specs=pl.BlockSpec((ADD_ROWS, LANES), lambda i, m: (i, 0))),
        out_shape=jax.ShapeDtypeStruct((n, LANES), out_dtype),
        compiler_params=_params(("parallel",)),
    )(me, a2, recv)


_WEIGHTS = {
    "norm_g": ((DEPTH, 4, D_MODEL), 2, True),
    "attn_w_qkv": ((2, D_MODEL, 4608), 2, False),
    "attn_w_o": ((2, ATTN_WIDTH, D_MODEL), 2, False),
    "conv_w_in": ((1, D_MODEL, 3 * D_MODEL), 2, False),
    "conv_w_dw": ((1, 3, D_MODEL), 2, True),
    "conv_w_out": ((1, D_MODEL, D_MODEL), 1, False),
    "pool_w_in": ((1, D_MODEL, D_MODEL), 1, False),
    "pool_w_grp": ((1, 4, POOL_GROUP_DIM, POOL_GROUP_DIM), 2, False),
    "pool_scale": ((1, D_MODEL), 1, True),
    "pool_w_out": ((1, D_MODEL, D_MODEL), 1, False),
    "ffn_w_up": ((DEPTH, D_MODEL, 2 * D_FF), 2, False),
    "ffn_w_dw": ((DEPTH, 3, 2 * D_FF), 2, True),
    "ffn_w_down": ((DEPTH, D_FF, D_MODEL), 1, False),
}
_NAMES = tuple(_WEIGHTS)
_BUF_A = ("ffn_w_up", "attn_w_o", "conv_w_out", "pool_w_in")
_BUF_B = tuple(k for k in _NAMES if k not in _BUF_A)


def _shard_shape(name):
    shape, ax, _ = _WEIGHTS[name]
    return tuple(s // N_DEV if i == ax else s for i, s in enumerate(shape))


def _round_up(n, m):
    return -(-n // m) * m


def _weight_rows(name):
    n = math.prod(_shard_shape(name))
    return (2 * n if _WEIGHTS[name][2] else n) // LANES


def _grad_rows(name):
    return math.prod(_shard_shape(name)) // LANES


def _pack_weight_shards(shards, names):
    parts = []
    for k in names:
        if _WEIGHTS[k][2]:
            parts.append(lax.bitcast_convert_type(shards[k], BF16).reshape(-1, LANES))
        else:
            parts.append(shards[k].astype(BF16).reshape(-1, LANES))
    rows = sum(p.shape[0] for p in parts)
    pad = _round_up(rows, 16) - rows
    if pad:
        parts.append(jnp.zeros((pad, LANES), BF16))
    return jnp.concatenate(parts, axis=0)


def _full_from_slots(slots, name):
    shape, ax, _ = _WEIGHTS[name]
    return jnp.moveaxis(slots, 0, ax).reshape(shape)


def _slots_from_full(full, name):
    shape, ax, _ = _WEIGHTS[name]
    split = shape[:ax] + (N_DEV, shape[ax] // N_DEV) + shape[ax + 1:]
    return jnp.moveaxis(full.reshape(split), ax, 0)


def _unpack_gathered(buf, names):
    out, r0 = {}, 0
    for k in names:
        rows = _weight_rows(k)
        piece = buf[:, r0:r0 + rows]
        r0 += rows
        if _WEIGHTS[k][2]:
            piece = lax.bitcast_convert_type(piece.reshape(N_DEV, -1, 2), F32)
        out[k] = _full_from_slots(piece.reshape((N_DEV,) + _shard_shape(k)), k)
    return out


def _pack_grads(grads, names):
    parts = [_slots_from_full(grads[k], k).astype(BF16).reshape(N_DEV, -1, LANES) for k in names]
    rows = sum(p.shape[1] for p in parts)
    pad = _round_up(rows, ADD_ROWS) - rows
    if pad:
        parts.append(jnp.zeros((N_DEV, pad, LANES), BF16))
    return jnp.concatenate(parts, axis=1)


def _unpack_grad_shards(buf, names):
    out, r0 = {}, 0
    for k in names:
        rows = _grad_rows(k)
        out[k] = buf[r0:r0 + rows].reshape(_shard_shape(k))
        r0 += rows
    return out


def _all_gather_weights(shards):
    pa = _pack_weight_shards(shards, _BUF_A)
    pb = _pack_weight_shards(shards, _BUF_B)
    a1, b1 = _exchange([(pa, "y", "gather"), (pb, "x", "gather")], "gather_1")
    a2, b2 = _exchange([(a1, "x", "gather"), (b1, "y", "gather")], "gather_2")
    a3, b3 = _exchange([(a2, "c", "gather"), (b2, "c", "gather")], "gather_3")
    ga = a3.transpose(1, 2, 0, 3, 4).reshape((N_DEV,) + pa.shape)
    gb = b3.transpose(2, 1, 0, 3, 4).reshape((N_DEV,) + pb.shape)
    return {**_unpack_gathered(ga, _BUF_A), **_unpack_gathered(gb, _BUF_B)}


def _reduce_scatter_grads(grads):
    cx = lax.axis_index("x").astype(jnp.int32).reshape(1)
    cy = lax.axis_index("y").astype(jnp.int32).reshape(1)
    cc = lax.axis_index("c").astype(jnp.int32).reshape(1)
    ga = _pack_grads(grads, _BUF_A)
    gb = _pack_grads(grads, _BUF_B)
    ra, rb = ga.shape[1], gb.shape[1]
    a0 = ga.reshape(2, 2, 2, ra, LANES).transpose(2, 0, 1, 3, 4).reshape(2, 4 * ra, LANES)
    b0 = gb.reshape(2, 2, 2, rb, LANES).transpose(2, 1, 0, 3, 4).reshape(2, 4 * rb, LANES)
    ar, br = _exchange([(a0, "c", "half"), (b0, "c", "half")], "scatter_1")
    a1 = _add_half(a0, ar, cc, BF16, "scatter_add_1a").reshape(2, 2 * ra, LANES)
    b1 = _add_half(b0, br, cc, BF16, "scatter_add_1b").reshape(2, 2 * rb, LANES)
    ar, br = _exchange([(a1, "x", "half"), (b1, "y", "half")], "scatter_2")
    a2 = _add_half(a1, ar, cx, BF16, "scatter_add_2a").reshape(2, ra, LANES)
    b2 = _add_half(b1, br, cy, BF16, "scatter_add_2b").reshape(2, rb, LANES)
    ar, br = _exchange([(a2, "y", "half"), (b2, "x", "half")], "scatter_3")
    a3 = _add_half(a2, ar, cy, F32, "scatter_add_3a")
    b3 = _add_half(b2, br, cx, F32, "scatter_add_3b")
    return {**_unpack_grad_shards(a3, _BUF_A), **_unpack_grad_shards(b3, _BUF_B)}


def _adamw(w, g, m, v, name):
    shape = w.shape
    cols = shape[-1]
    rows = math.prod(shape[:-1])
    tr = _pick(rows, (512, 256, 128, 64, 32, 16, 8))

    def body(w_ref, g_ref, m_ref, v_ref, d_ref, nm_ref, nv_ref):
        gv = g_ref[...]
        nm = ADAM_B1 * m_ref[...] + (1.0 - ADAM_B1) * gv
        nv = ADAM_B2 * v_ref[...] + (1.0 - ADAM_B2) * jnp.square(gv)
        m_hat = nm / (1.0 - ADAM_B1 ** ADAM_STEP)
        v_hat = nv / (1.0 - ADAM_B2 ** ADAM_STEP)
        d_ref[...] = -ADAM_LR * (m_hat / (jnp.sqrt(v_hat) + ADAM_EPS) + ADAM_WD * w_ref[...])
        nm_ref[...] = nm
        nv_ref[...] = nv

    blk = pl.BlockSpec((tr, cols), lambda i: (i, 0))
    shp = jax.ShapeDtypeStruct((rows, cols), F32)
    outs = pl.pallas_call(
        body, name=name, grid=(rows // tr,), in_specs=[blk] * 4, out_specs=[blk] * 3, out_shape=[shp] * 3,
        compiler_params=_params(("parallel",)),
    )(*[t.reshape(rows, cols) for t in (w, g, m, v)])
    return [o.reshape(shape) for o in outs]


def kernel(x, norm_g, attn_w_qkv, attn_w_o, conv_w_in, conv_w_dw, conv_w_out, pool_w_in, pool_w_grp, pool_scale, pool_w_out, ffn_w_up, ffn_w_dw, ffn_w_down, loss_target, m_norm_g, m_attn_w_qkv, m_attn_w_o, m_conv_w_in, m_conv_w_dw, m_conv_w_out, m_pool_w_in, m_pool_w_grp, m_pool_scale, m_pool_w_out, m_ffn_w_up, m_ffn_w_dw, m_ffn_w_down, v_norm_g, v_attn_w_qkv, v_attn_w_o, v_conv_w_in, v_conv_w_dw, v_conv_w_out, v_pool_w_in, v_pool_w_grp, v_pool_scale, v_pool_w_out, v_ffn_w_up, v_ffn_w_dw, v_ffn_w_down):
    shards = dict(zip(_NAMES, (norm_g, attn_w_qkv, attn_w_o, conv_w_in, conv_w_dw, conv_w_out, pool_w_in,
                               pool_w_grp, pool_scale, pool_w_out, ffn_w_up, ffn_w_dw, ffn_w_down)))
    moms = dict(zip(_NAMES, (m_norm_g, m_attn_w_qkv, m_attn_w_o, m_conv_w_in, m_conv_w_dw, m_conv_w_out,
                             m_pool_w_in, m_pool_w_grp, m_pool_scale, m_pool_w_out, m_ffn_w_up, m_ffn_w_dw,
                             m_ffn_w_down)))
    vels = dict(zip(_NAMES, (v_norm_g, v_attn_w_qkv, v_attn_w_o, v_conv_w_in, v_conv_w_dw, v_conv_w_out,
                             v_pool_w_in, v_pool_w_grp, v_pool_scale, v_pool_w_out, v_ffn_w_up, v_ffn_w_dw,
                             v_ffn_w_down)))
    full = _all_gather_weights(shards)
    loss, grad_x, grads = _local_step(x[0], loss_target[0], full)
    loss = lax.psum(loss[0, 0], _AXES)
    gsh = _reduce_scatter_grads(grads)
    deltas, new_m, new_v = [], [], []
    for k in _NAMES:
        d, nm, nv = _adamw(shards[k], gsh[k], moms[k], vels[k], f"adamw_{k}")
        deltas.append(d)
        new_m.append(nm)
        new_v.append(nv)
    return (loss, grad_x[None], *[gsh[k] for k in _NAMES], *deltas, *new_m, *new_v)
```

```python
import functools
import math

import numpy as np
import jax
import jax.numpy as jnp
from jax import lax
from jax.experimental import pallas as pl
from jax.experimental.pallas import tpu as pltpu

F32, BF16 = jnp.float32, jnp.bfloat16

D_MODEL = 1024
SEQ = 2048
DEPTH = 4
DILATED_CFG = ((128, 1), (512, 4), (2048, 16))
N_GROUPS_A = 3
HEADS = 8
HEAD_DIM = 64
ATTN_WIDTH = HEADS * HEAD_DIM
N_HEADS_A = N_GROUPS_A * HEADS
BLOCK = 128
NEG_INF = -1e30
POOL_GROUP_DIM = 256
D_FF = 2816
RMS_EPS = 1e-6
ADAM_LR, ADAM_B1, ADAM_B2, ADAM_EPS, ADAM_WD, ADAM_STEP = 0.001, 0.9, 0.999, 1e-08, 0.01, 10

N_DEV = 8
LANES = 128
V7X_VMEM_BYTES = 64 * 2 ** 20
VMEM_LIMIT_BYTES = V7X_VMEM_BYTES - 8 * 2 ** 20
COL_TILE = 256
ROW_TILE = 256

NN = (((1,), (0,)), ((), ()))
NT = (((1,), (1,)), ((), ()))
TN = (((0,), (0,)), ((), ()))


def _dot(a, b, dims=NN):
    return lax.dot_general(a, b, dims, preferred_element_type=F32)


def _params(sem=None):
    return pltpu.CompilerParams(dimension_semantics=sem, vmem_limit_bytes=VMEM_LIMIT_BYTES)


def _pick(n, prefs):
    for p in prefs:
        if n % p == 0:
            return p
    return n


def _matmul(a, b, mode, out_dtype, name, a_parts=1, b_parts=1):
    if mode == "nn":
        m, k = a.shape[-2], a.shape[-1] * a_parts
        n = b.shape[-1] * b_parts
    elif mode == "nt":
        m, k = a.shape[-2], a.shape[-1] * a_parts
        n = b.shape[-2]
    else:
        k, m = a.shape[-2], a.shape[-1] * a_parts
        n = b.shape[-1] * b_parts
    tm = _pick(m, (1024, 512, 256, 128))
    tn = _pick(n // b_parts if mode != "nt" else n, (1024, 512, 256, 128))
    kk = k // a_parts if mode != "tn" else k
    tk = _pick(kk, (1024, 512, 256, 128))
    if mode == "tn":
        tm = _pick(m // a_parts, (1024, 512, 256, 128))
    gm, gn, gk = m // tm, n // tn, k // tk

    def a_idx(i, j, kq):
        if mode == "tn":
            r, c, per = kq, i, (m // a_parts) // tm
        else:
            r, c, per = i, kq, (k // a_parts) // tk
        return (r, c) if a_parts == 1 else (c // per, r, c % per)

    def b_idx(i, j, kq):
        if mode == "nt":
            return (j, kq)
        per = (n // b_parts) // tn
        return (kq, j) if b_parts == 1 else (j // per, kq, j % per)

    a_blk = (tk, tm) if mode == "tn" else (tm, tk)
    b_blk = (tn, tk) if mode == "nt" else (tk, tn)
    if a_parts > 1:
        a_blk = (None,) + a_blk
    if b_parts > 1:
        b_blk = (None,) + b_blk
    dims = {"nn": NN, "nt": NT, "tn": TN}[mode]

    def body(a_ref, b_ref, o_ref, acc_ref):
        kq = pl.program_id(2)

        @pl.when(kq == 0)
        def _():
            acc_ref[...] = jnp.zeros_like(acc_ref)

        acc_ref[...] += _dot(a_ref[...], b_ref[...], dims)

        @pl.when(kq == gk - 1)
        def _():
            o_ref[...] = acc_ref[...].astype(o_ref.dtype)

    return pl.pallas_call(
        body, name=name, grid=(gm, gn, gk),
        in_specs=[pl.BlockSpec(a_blk, a_idx), pl.BlockSpec(b_blk, b_idx)],
        out_specs=pl.BlockSpec((tm, tn), lambda i, j, kq: (i, j)),
        out_shape=jax.ShapeDtypeStruct((m, n), out_dtype),
        scratch_shapes=[pltpu.VMEM((tm, tn), F32)],
        compiler_params=_params(("parallel", "parallel", "arbitrary")),
    )(a, b)


def _rms_fwd(xin, g, res, out_dtype, name):
    s, d = xin.shape
    has_res = res is not None

    def body(*refs):
        x_ref, g_ref = refs[0], refs[1]
        o_ref = refs[-1]
        x = x_ref[...]
        r = lax.rsqrt(jnp.mean(x * x, axis=-1, keepdims=True) + RMS_EPS)
        y = x * r * g_ref[...]
        if has_res:
            y = refs[2][...] + y
        o_ref[...] = y.astype(o_ref.dtype)

    row = pl.BlockSpec((ROW_TILE, d), lambda i: (i, 0))
    vec = pl.BlockSpec((1, d), lambda i: (0, 0))
    ins = [xin, g] + ([res] if has_res else [])
    return pl.pallas_call(
        body, name=name, grid=(s // ROW_TILE,),
        in_specs=[row, vec] + ([row] if has_res else []),
        out_specs=row, out_shape=jax.ShapeDtypeStruct((s, d), out_dtype),
        compiler_params=_params(("parallel",)),
    )(*ins)


def _rms_bwd(xin, g, dy, dres, out_dtype, name):
    s, d = xin.shape
    has_res = dres is not None

    def body(*refs):
        x_ref, g_ref, dy_ref = refs[0], refs[1], refs[2]
        dx_ref, dg_ref = refs[-2], refs[-1]

        @pl.when(pl.program_id(0) == 0)
        def _():
            dg_ref[...] = jnp.zeros_like(dg_ref)

        x = x_ref[...]
        dyv = dy_ref[...].astype(F32)
        r = lax.rsqrt(jnp.mean(x * x, axis=-1, keepdims=True) + RMS_EPS)
        xhat = x * r
        u = dyv * g_ref[...]
        dx = r * (u - xhat * jnp.mean(u * xhat, axis=-1, keepdims=True))
        if has_res:
            dx = refs[3][...] + dx
        dx_ref[...] = dx.astype(dx_ref.dtype)
        dg_ref[...] += jnp.sum(dyv * xhat, axis=0, keepdims=True)

    row = pl.BlockSpec((ROW_TILE, d), lambda i: (i, 0))
    vec = pl.BlockSpec((1, d), lambda i: (0, 0))
    ins = [xin, g, dy] + ([dres] if has_res else [])
    return pl.pallas_call(
        body, name=name, grid=(s // ROW_TILE,),
        in_specs=[row, vec, row] + ([row] if has_res else []),
        out_specs=[row, vec],
        out_shape=[jax.ShapeDtypeStruct((s, d), out_dtype), jax.ShapeDtypeStruct((1, d), F32)],
        compiler_params=_params(("arbitrary",)),
    )(*ins)


def _loss_head(y, tgt, name):
    s, d = y.shape

    def body(y_ref, t_ref, l_ref, dy_ref):
        @pl.when(pl.program_id(0) == 0)
        def _():
            l_ref[...] = jnp.zeros_like(l_ref)

        e = y_ref[...] - t_ref[...]
        dy_ref[...] = e / d
        per_tok = jnp.mean(e * e, axis=-1, keepdims=True)
        l_ref[...] += 0.5 * jnp.sum(per_tok, axis=0, keepdims=True)

    row = pl.BlockSpec((ROW_TILE, d), lambda i: (i, 0))
    return pl.pallas_call(
        body, name=name, grid=(s // ROW_TILE,),
        in_specs=[row, row],
        out_specs=[pl.BlockSpec((1, 1), lambda i: (0, 0)), row],
        out_shape=[jax.ShapeDtypeStruct((1, 1), F32), jax.ShapeDtypeStruct((s, d), F32)],
        compiler_params=_params(("arbitrary",)),
    )(y, tgt)


def _shift_down(x, k):
    rows = lax.broadcasted_iota(jnp.int32, x.shape, 0)
    return jnp.where(rows >= k, pltpu.roll(x, k, axis=0), 0.0)


def _shift_up(x, k):
    t = x.shape[0]
    rows = lax.broadcasted_iota(jnp.int32, x.shape, 0)
    return jnp.where(rows < t - k, pltpu.roll(x, t - k, axis=0), 0.0)


def _conv3(h, w):
    return w[2:3] * h + w[1:2] * _shift_down(h, 1) + w[0:1] * _shift_down(h, 2)


def _conv3_bwd_x(dc, w):
    return w[2:3] * dc + w[1:2] * _shift_up(dc, 1) + w[0:1] * _shift_up(dc, 2)


def _conv3_bwd_w(dc, h, dw_ref):
    dw_ref[0:1, :] = jnp.sum(dc * _shift_down(h, 2), axis=0, keepdims=True)
    dw_ref[1:2, :] = jnp.sum(dc * _shift_down(h, 1), axis=0, keepdims=True)
    dw_ref[2:3, :] = jnp.sum(dc * h, axis=0, keepdims=True)


def _ffn_up(n, wup, wdw, name):
    s, d = n.shape
    f = wup.shape[1] // 2
    tn = COL_TILE
    nj = f // tn

    def body(n_ref, wg_ref, wu_ref, dg_ref, du_ref, h_ref, a_ref):
        x = n_ref[...]
        hg = _dot(x, wg_ref[...])
        hu = _dot(x, wu_ref[...])
        h_ref[0] = hg.astype(BF16)
        h_ref[1] = hu.astype(BF16)
        cg = _conv3(hg, dg_ref[...])
        cu = _conv3(hu, du_ref[...])
        a_ref[...] = (cg * jax.nn.sigmoid(cg) * cu).astype(BF16)

    return pl.pallas_call(
        body, name=name, grid=(nj,),
        in_specs=[pl.BlockSpec((s, d), lambda j: (0, 0)),
                  pl.BlockSpec((d, tn), lambda j: (0, j)), pl.BlockSpec((d, tn), lambda j: (0, j + nj)),
                  pl.BlockSpec((3, tn), lambda j: (0, j)), pl.BlockSpec((3, tn), lambda j: (0, j + nj))],
        out_specs=[pl.BlockSpec((2, s, tn), lambda j: (0, 0, j)), pl.BlockSpec((s, tn), lambda j: (0, j))],
        out_shape=[jax.ShapeDtypeStruct((2, s, f), BF16), jax.ShapeDtypeStruct((s, f), BF16)],
        compiler_params=_params(("parallel",)),
    )(n, wup, wup, wdw, wdw)


def _ffn_mid_bwd(do, wdown, h, wdw, name):
    s, d = do.shape
    f = wdown.shape[0]
    tn = COL_TILE
    nj = f // tn

    def body(do_ref, wd_ref, h_ref, wg_ref, wu_ref, dh_ref, dwg_ref, dwu_ref):
        da = _dot(do_ref[...], wd_ref[...], NT)
        hg = h_ref[0].astype(F32)
        hu = h_ref[1].astype(F32)
        wg, wu = wg_ref[...], wu_ref[...]
        cg = _conv3(hg, wg)
        cu = _conv3(hu, wu)
        sg = jax.nn.sigmoid(cg)
        dcu = da * (cg * sg)
        dcg = da * cu * (sg * (1.0 + cg * (1.0 - sg)))
        dh_ref[0] = _conv3_bwd_x(dcg, wg).astype(BF16)
        dh_ref[1] = _conv3_bwd_x(dcu, wu).astype(BF16)
        _conv3_bwd_w(dcg, hg, dwg_ref)
        _conv3_bwd_w(dcu, hu, dwu_ref)

    return pl.pallas_call(
        body, name=name, grid=(nj,),
        in_specs=[pl.BlockSpec((s, d), lambda j: (0, 0)), pl.BlockSpec((tn, d), lambda j: (j, 0)),
                  pl.BlockSpec((2, s, tn), lambda j: (0, 0, j)),
                  pl.BlockSpec((3, tn), lambda j: (0, j)), pl.BlockSpec((3, tn), lambda j: (0, j + nj))],
        out_specs=[pl.BlockSpec((2, s, tn), lambda j: (0, 0, j)),
                   pl.BlockSpec((3, tn), lambda j: (0, j)), pl.BlockSpec((3, tn), lambda j: (0, j))],
        out_shape=[jax.ShapeDtypeStruct((2, s, f), BF16), jax.ShapeDtypeStruct((3, f), F32),
                   jax.ShapeDtypeStruct((3, f), F32)],
        compiler_params=_params(("parallel",)),
    )(do, wdown, h, wdw, wdw)


def _sconv_fwd(n, win, wdw, name):
    s, d = n.shape
    tn = COL_TILE
    nj = d // tn

    def body(n_ref, wb_ref, wc_ref, wh_ref, dw_ref, z_ref, y_ref):
        x = n_ref[...]
        zb = _dot(x, wb_ref[...])
        zc = _dot(x, wc_ref[...])
        zh = _dot(x, wh_ref[...])
        z_ref[0] = zb.astype(BF16)
        z_ref[1] = zc.astype(BF16)
        z_ref[2] = zh.astype(BF16)
        y_ref[...] = (zb * _conv3(zc * zh, dw_ref[...])).astype(BF16)

    return pl.pallas_call(
        body, name=name, grid=(nj,),
        in_specs=[pl.BlockSpec((s, d), lambda j: (0, 0)),
                  pl.BlockSpec((d, tn), lambda j: (0, j)), pl.BlockSpec((d, tn), lambda j: (0, j + nj)),
                  pl.BlockSpec((d, tn), lambda j: (0, j + 2 * nj)), pl.BlockSpec((3, tn), lambda j: (0, j))],
        out_specs=[pl.BlockSpec((3, s, tn), lambda j: (0, 0, j)), pl.BlockSpec((s, tn), lambda j: (0, j))],
        out_shape=[jax.ShapeDtypeStruct((3, s, d), BF16), jax.ShapeDtypeStruct((s, d), BF16)],
        compiler_params=_params(("parallel",)),
    )(n, win, win, win, wdw)


def _sconv_mid_bwd(dm, wout, z, wdw, name):
    s, d = dm.shape
    tn = COL_TILE
    nj = d // tn

    def body(dm_ref, wo_ref, z_ref, w_ref, dz_ref, dw_ref):
        dy = _dot(dm_ref[...], wo_ref[...], NT)
        zb = z_ref[0].astype(F32)
        zc = z_ref[1].astype(F32)
        zh = z_ref[2].astype(F32)
        w = w_ref[...]
        p = zc * zh
        cp = _conv3(p, w)
        dz_ref[0] = (dy * cp).astype(BF16)
        dcp = dy * zb
        dp = _conv3_bwd_x(dcp, w)
        _conv3_bwd_w(dcp, p, dw_ref)
        dz_ref[1] = (dp * zh).astype(BF16)
        dz_ref[2] = (dp * zc).astype(BF16)

    return pl.pallas_call(
        body, name=name, grid=(nj,),
        in_specs=[pl.BlockSpec((s, d), lambda j: (0, 0)), pl.BlockSpec((tn, d), lambda j: (j, 0)),
                  pl.BlockSpec((3, s, tn), lambda j: (0, 0, j)), pl.BlockSpec((3, tn), lambda j: (0, j))],
        out_specs=[pl.BlockSpec((3, s, tn), lambda j: (0, 0, j)), pl.BlockSpec((3, tn), lambda j: (0, j))],
        out_shape=[jax.ShapeDtypeStruct((3, s, d), BF16), jax.ShapeDtypeStruct((3, d), F32)],
        compiler_params=_params(("parallel",)),
    )(dm, wout, z, wdw)


def _pool_select(g, c2, c4, c8, c16):
    return jnp.where(g == 0, c2, jnp.where(g == 1, c4, jnp.where(g == 2, c8, c16)))


def _pool_inv_count(g, shape):
    pos = lax.broadcasted_iota(jnp.int32, shape, 0).astype(F32) + 1.0
    win = (2 << g).astype(F32)
    return jnp.minimum(pos, win)


def _pool_fwd(n, win, wgrp, scale, name):
    s, d = n.shape
    tn = POOL_GROUP_DIM

    def body(n_ref, wi_ref, wg_ref, sc_ref, p_ref, y_ref):
        g = pl.program_id(0)
        u = _dot(n_ref[...], wi_ref[...])
        s2 = u + _shift_down(u, 1)
        s4 = s2 + _shift_down(s2, 2)
        s8 = s4 + _shift_down(s4, 4)
        s16 = s8 + _shift_down(s8, 8)
        tot = _pool_select(g, s2, s4, s8, s16)
        p = (tot / _pool_inv_count(g, u.shape) - u).astype(BF16)
        p_ref[...] = p
        y_ref[...] = (_dot(p, wg_ref[...]) * sc_ref[...]).astype(BF16)

    return pl.pallas_call(
        body, name=name, grid=(d // tn,),
        in_specs=[pl.BlockSpec((s, d), lambda g: (0, 0)), pl.BlockSpec((d, tn), lambda g: (0, g)),
                  pl.BlockSpec((None, tn, tn), lambda g: (g, 0, 0)), pl.BlockSpec((1, tn), lambda g: (0, g))],
        out_specs=[pl.BlockSpec((s, tn), lambda g: (0, g)), pl.BlockSpec((s, tn), lambda g: (0, g))],
        out_shape=[jax.ShapeDtypeStruct((s, d), BF16), jax.ShapeDtypeStruct((s, d), BF16)],
        compiler_params=_params(("parallel",)),
    )(n, win, wgrp, scale)


def _pool_mid_bwd(dm, wout, p, wgrp, scale, name):
    s, d = dm.shape
    tn = POOL_GROUP_DIM

    def body(dm_ref, wo_ref, p_ref, wg_ref, sc_ref, du_ref, dwg_ref, dsc_ref):
        g = pl.program_id(0)
        dy = _dot(dm_ref[...], wo_ref[...], NT)
        pv = p_ref[...]
        wg = wg_ref[...]
        ypre = _dot(pv, wg)
        dsc_ref[...] = jnp.sum(dy * ypre, axis=0, keepdims=True)
        dypre = (dy * sc_ref[...]).astype(BF16)
        dwg_ref[...] = _dot(pv, dypre, TN)
        dp = _dot(dypre, wg, NT)
        e = dp / _pool_inv_count(g, dp.shape)
        f2 = e + _shift_up(e, 1)
        f4 = f2 + _shift_up(f2, 2)
        f8 = f4 + _shift_up(f4, 4)
        f16 = f8 + _shift_up(f8, 8)
        du_ref[...] = (_pool_select(g, f2, f4, f8, f16) - dp).astype(BF16)

    return pl.pallas_call(
        body, name=name, grid=(d // tn,),
        in_specs=[pl.BlockSpec((s, d), lambda g: (0, 0)), pl.BlockSpec((tn, d), lambda g: (g, 0)),
                  pl.BlockSpec((s, tn), lambda g: (0, g)), pl.BlockSpec((None, tn, tn), lambda g: (g, 0, 0)),
                  pl.BlockSpec((1, tn), lambda g: (0, g))],
        out_specs=[pl.BlockSpec((s, tn), lambda g: (0, g)), pl.BlockSpec((None, tn, tn), lambda g: (g, 0, 0)),
                   pl.BlockSpec((1, tn), lambda g: (0, g))],
        out_shape=[jax.ShapeDtypeStruct((s, d), BF16), jax.ShapeDtypeStruct((4, tn, tn), F32),
                   jax.ShapeDtypeStruct((1, d), F32)],
        compiler_params=_params(("parallel",)),
    )(dm, wout, p, wgrp, scale)


def _alibi_slopes(g):
    all_slopes = 2.0 ** (-8.0 * np.arange(1, N_HEADS_A + 1) / N_HEADS_A)
    return jnp.asarray(all_slopes[g * HEADS:(g + 1) * HEADS], F32).reshape(HEADS, 1, 1)


BHQK = (((2,), (2,)), ((0,), (0,)))
BHQC = (((2,), (1,)), ((0,), (0,)))
BHKC = (((1,), (1,)), ((0,), (0,)))


def _attn_window(n, ln):
    if ln == BLOCK:
        return 0, BLOCK
    return pl.multiple_of(jnp.maximum(n - 1, 0) * BLOCK, BLOCK), 2 * BLOCK


def _attn_scores(q, keys, slope, n, k0):
    kw = keys.shape[1]
    qpos = n * BLOCK + lax.broadcasted_iota(jnp.int32, (BLOCK, kw), 0)
    kpos = k0 + lax.broadcasted_iota(jnp.int32, (BLOCK, kw), 1)
    dist = qpos - kpos
    valid = (dist >= 0) & (dist <= BLOCK)
    s = _dot(q, keys, BHQK) * (HEAD_DIM ** -0.5) - slope * dist.astype(F32)[None]
    return jnp.where(valid[None], s, NEG_INF)


def _attn_specs(hb, ln):
    def part(t):
        return pl.BlockSpec((None, hb, None, ln, HEAD_DIM), lambda h, r, n: (t, h, r, 0, 0))
    qblk = pl.BlockSpec((None, hb, None, BLOCK, HEAD_DIM), lambda h, r, n: (0, h, r, n, 0))
    row = pl.BlockSpec((hb, None, BLOCK, HEAD_DIM), lambda h, r, n: (h, r, n, 0))
    stat = pl.BlockSpec((hb, None, BLOCK, 1), lambda h, r, n: (h, r, n, 0))
    whole = pl.BlockSpec((hb, None, ln, HEAD_DIM), lambda h, r, n: (h, r, 0, 0))
    slope = pl.BlockSpec((hb, 1, 1), lambda h, r, n: (h, 0, 0))
    return part, qblk, row, stat, whole, slope


def _attn_fwd(qkv, slopes, dil, name):
    _, hh, _, ln, _ = qkv.shape
    nb = ln // BLOCK

    def body(sl_ref, q_ref, k_ref, v_ref, o_ref, lse_ref):
        n = pl.program_id(2)
        k0, kw = _attn_window(n, ln)
        s = _attn_scores(q_ref[...], k_ref[:, pl.ds(k0, kw), :], sl_ref[...] * float(dil), n, k0)
        m = jnp.max(s, axis=-1, keepdims=True)
        p = jnp.exp(s - m)
        den = jnp.sum(p, axis=-1, keepdims=True)
        o_ref[...] = _dot(p.astype(BF16), v_ref[:, pl.ds(k0, kw), :], BHQC) / den
        lse_ref[...] = m + jnp.log(den)

    part, qblk, row, stat, _, slope = _attn_specs(hh, ln)
    return pl.pallas_call(
        body, name=name, grid=(1, dil, nb),
        in_specs=[slope, qblk, part(1), part(2)],
        out_specs=[row, stat],
        out_shape=[jax.ShapeDtypeStruct((hh, dil, ln, HEAD_DIM), F32), jax.ShapeDtypeStruct((hh, dil, ln, 1), F32)],
        compiler_params=_params(("parallel", "parallel", "parallel")),
    )(slopes, qkv, qkv, qkv)


ATTN_BWD_HEADS = 4


def _attn_bwd(qkv, do, lse, dvec, slopes, dil, name):
    _, hh, _, ln, _ = qkv.shape
    nb = ln // BLOCK
    scale = HEAD_DIM ** -0.5

    def body(sl_ref, q_ref, k_ref, v_ref, do_ref, lse_ref, dd_ref, dq_ref, dk_ref, dv_ref):
        n = pl.program_id(2)

        @pl.when(n == 0)
        def _():
            dk_ref[...] = jnp.zeros_like(dk_ref)
            dv_ref[...] = jnp.zeros_like(dv_ref)

        k0, kw = _attn_window(n, ln)
        win = pl.ds(k0, kw)
        q, keys, dob = q_ref[...], k_ref[:, win, :], do_ref[...]
        s = _attn_scores(q, keys, sl_ref[...] * float(dil), n, k0)
        p = jnp.exp(s - lse_ref[...])
        ds = (p * (_dot(dob, v_ref[:, win, :], BHQK) - dd_ref[...])).astype(BF16)
        dq_ref[...] = scale * _dot(ds, keys, BHQC)
        dv_ref[:, win, :] += _dot(p.astype(BF16), dob, BHKC)
        dk_ref[:, win, :] += scale * _dot(ds, q, BHKC)

    part, qblk, row, stat, whole, slope = _attn_specs(ATTN_BWD_HEADS, ln)
    shp = jax.ShapeDtypeStruct((hh, dil, ln, HEAD_DIM), F32)
    return pl.pallas_call(
        body, name=name, grid=(hh // ATTN_BWD_HEADS, dil, nb),
        in_specs=[slope, qblk, part(1), part(2), row, stat, stat],
        out_specs=[row, whole, whole], out_shape=[shp, shp, shp],
        compiler_params=_params(("parallel", "parallel", "arbitrary")),
    )(slopes, qkv, qkv, qkv, do, lse, dvec)


def _attn_merge(o, lse, name):
    _, s, w = o.shape

    def body(o_ref, l_ref, m_ref, mb_ref, lse_ref):
        l = l_ref[...]
        mx = jnp.max(l, axis=0)
        e = jnp.exp(l - mx[None])
        z = jnp.sum(e, axis=0)
        lse_ref[...] = mx + jnp.log(z)
        wts = e / z[None]
        for h in range(HEADS):
            cols = slice(h * HEAD_DIM, (h + 1) * HEAD_DIM)
            acc = wts[0][:, h:h + 1] * o_ref[0, :, cols]
            for g in range(1, N_GROUPS_A):
                acc = acc + wts[g][:, h:h + 1] * o_ref[g, :, cols]
            m_ref[:, cols] = acc
            mb_ref[:, cols] = acc.astype(BF16)

    return pl.pallas_call(
        body, name=name, grid=(s // ROW_TILE,),
        in_specs=[pl.BlockSpec((N_GROUPS_A, ROW_TILE, w), lambda i: (0, i, 0)),
                  pl.BlockSpec((N_GROUPS_A, ROW_TILE, HEADS), lambda i: (0, i, 0))],
        out_specs=[pl.BlockSpec((ROW_TILE, w), lambda i: (i, 0)), pl.BlockSpec((ROW_TILE, w), lambda i: (i, 0)),
                   pl.BlockSpec((ROW_TILE, HEADS), lambda i: (i, 0))],
        out_shape=[jax.ShapeDtypeStruct((s, w), F32), jax.ShapeDtypeStruct((s, w), BF16),
                   jax.ShapeDtypeStruct((s, HEADS), F32)],
        compiler_params=_params(("parallel",)),
    )(o, lse)


def _attn_dvec(dmerged, merged, name):
    s, w = merged.shape

    def body(dm_ref, m_ref, dd_ref, dmb_ref):
        dmv = dm_ref[...]
        dmb_ref[...] = dmv.astype(BF16)
        prod = dmv * m_ref[...]
        for h in range(HEADS):
            dd_ref[:, h:h + 1] = jnp.sum(prod[:, h * HEAD_DIM:(h + 1) * HEAD_DIM], axis=-1, keepdims=True)

    row = pl.BlockSpec((ROW_TILE, w), lambda i: (i, 0))
    return pl.pallas_call(
        body, name=name, grid=(s // ROW_TILE,),
        in_specs=[row, row],
        out_specs=[pl.BlockSpec((ROW_TILE, HEADS), lambda i: (i, 0)), row],
        out_shape=[jax.ShapeDtypeStruct((s, HEADS), F32), jax.ShapeDtypeStruct((s, w), BF16)],
        compiler_params=_params(("parallel",)),
    )(dmerged, merged)


def _to_group(a, dil):
    s, w = a.shape
    c = w // HEADS
    return a.reshape(s // dil, dil, HEADS, c).transpose(2, 1, 0, 3)


def _from_group(a):
    hh, dil, ln, c = a.shape
    return a.transpose(2, 1, 0, 3).reshape(ln * dil, hh * c)


def _attention_fwd(n, wqkv, wo, tag):
    s = n.shape[0]
    qkv = _matmul(n, wqkv, "nn", BF16, f"{tag}_qkv")
    qkv6 = qkv.reshape(s, N_GROUPS_A, 3, HEADS, HEAD_DIM)
    groups, outs, lses = [], [], []
    for g, (_, dil) in enumerate(DILATED_CFG):
        qg = qkv6[:, g].reshape(s // dil, dil, 3, HEADS, HEAD_DIM).transpose(2, 3, 1, 0, 4)
        o, lse = _attn_fwd(qg, _alibi_slopes(g), dil, f"{tag}_fwd_g{g}")
        groups.append(qg)
        outs.append(_from_group(o))
        lses.append(_from_group(lse))
    merged, merged_bf, lse_all = _attn_merge(jnp.stack(outs), jnp.stack(lses), f"{tag}_merge")
    m = _matmul(merged_bf, wo, "nn", F32, f"{tag}_wo")
    return m, (groups, merged, merged_bf, lse_all)


def _attention_bwd(dm, n, wqkv, wo, saved, tag):
    groups, merged, merged_bf, lse_all = saved
    d_wo = _matmul(merged_bf, dm, "tn", F32, f"{tag}_dwo")
    dmerged = _matmul(dm, wo, "nt", F32, f"{tag}_dmerged")
    dvec, dmerged_bf = _attn_dvec(dmerged, merged, f"{tag}_dvec")
    cols = []
    for g, (_, dil) in enumerate(DILATED_CFG):
        dq, dk, dv = _attn_bwd(groups[g], _to_group(dmerged_bf, dil), _to_group(lse_all, dil),
                               _to_group(dvec, dil), _alibi_slopes(g), dil, f"{tag}_bwd_g{g}")
        cols += [_from_group(dq), _from_group(dk), _from_group(dv)]
    dqkv = jnp.concatenate(cols, axis=1).astype(BF16)
    d_wqkv = _matmul(n, dqkv, "tn", F32, f"{tag}_dwqkv")
    dn = _matmul(dqkv, wqkv, "nt", F32, f"{tag}_dn")
    return dn, d_wqkv, d_wo


def _local_step(x, tgt, w):
    ng = w["norm_g"]

    def gain(i, j):
        return ng[i, j][None, :]

    saved = []
    ia = ib = ic = 0
    n = _rms_fwd(x, gain(0, 0), None, BF16, "norm_first")
    for i in range(DEPTH):
        kind = i % 3
        if kind == 0:
            m, ms = _attention_fwd(n, w["attn_w_qkv"][ia], w["attn_w_o"][ia], "attn")
            mixer = ("a", ia, ms)
            ia += 1
        elif kind == 1:
            z, y = _sconv_fwd(n, w["conv_w_in"][ib], w["conv_w_dw"][ib], "sconv_fwd")
            m = _matmul(y, w["conv_w_out"][ib], "nn", F32, "sconv_out")
            mixer = ("b", ib, (z, y))
            ib += 1
        else:
            p, y = _pool_fwd(n, w["pool_w_in"][ic], w["pool_w_grp"][ic], w["pool_scale"][ic][None, :], "pool_fwd")
            m = _matmul(y, w["pool_w_out"][ic], "nn", F32, "pool_out")
            mixer = ("c", ic, (p, y))
            ic += 1
        x1 = _rms_fwd(m, gain(i, 1), x, F32, "norm_res")
        n2 = _rms_fwd(x1, gain(i, 2), None, BF16, "norm_pre")
        h, a = _ffn_up(n2, w["ffn_w_up"][i], w["ffn_w_dw"][i], "ffn_up")
        f = _matmul(a, w["ffn_w_down"][i], "nn", F32, "ffn_down")
        x2 = _rms_fwd(f, gain(i, 3), x1, F32, "norm_res")
        saved.append((x, n, m, mixer, x1, n2, h, a, f))
        x = x2
        if i + 1 < DEPTH:
            n = _rms_fwd(x, gain(i + 1, 0), None, BF16, "norm_pre")

    loss, dx = _loss_head(x, tgt, "loss_head")

    g_norm = [[None] * 4 for _ in range(DEPTH)]
    grads = {k: [] for k in ("attn_w_qkv", "attn_w_o", "conv_w_in", "conv_w_dw", "conv_w_out", "pool_w_in",
                             "pool_w_grp", "pool_scale", "pool_w_out", "ffn_w_up", "ffn_w_dw", "ffn_w_down")}
    for i in reversed(range(DEPTH)):
        xin, n, m, mixer, x1, n2, h, a, f = saved[i]
        df, g_norm[i][3] = _rms_bwd(f, gain(i, 3), dx, None, BF16, "norm_bwd_sub")
        grads["ffn_w_down"].append(_matmul(a, df, "tn", F32, "ffn_dwdown"))
        dh, dwg, dwu = _ffn_mid_bwd(df, w["ffn_w_down"][i], h, w["ffn_w_dw"][i], "ffn_mid_bwd")
        grads["ffn_w_dw"].append(jnp.concatenate([dwg, dwu], axis=1))
        grads["ffn_w_up"].append(_matmul(n2, dh, "tn", F32, "ffn_dwup", b_parts=2))
        dn2 = _matmul(dh, w["ffn_w_up"][i], "nt", F32, "ffn_dn", a_parts=2)
        dx1, g_norm[i][2] = _rms_bwd(x1, gain(i, 2), dn2, dx, F32, "norm_bwd_res")
        dm, g_norm[i][1] = _rms_bwd(m, gain(i, 1), dx1, None, BF16, "norm_bwd_sub")
        kind, idx, ms = mixer
        if kind == "a":
            dn, d_wqkv, d_wo = _attention_bwd(dm, n, w["attn_w_qkv"][idx], w["attn_w_o"][idx], ms, "attn")
            grads["attn_w_qkv"].append(d_wqkv)
            grads["attn_w_o"].append(d_wo)
        elif kind == "b":
            z, y = ms
            grads["conv_w_out"].append(_matmul(y, dm, "tn", F32, "sconv_dwout"))
            dz, ddw = _sconv_mid_bwd(dm, w["conv_w_out"][idx], z, w["conv_w_dw"][idx], "sconv_mid_bwd")
            grads["conv_w_dw"].append(ddw)
            grads["conv_w_in"].append(_matmul(n, dz, "tn", F32, "sconv_dwin", b_parts=3))
            dn = _matmul(dz, w["conv_w_in"][idx], "nt", F32, "sconv_dn", a_parts=3)
        else:
            p, y = ms
            grads["pool_w_out"].append(_matmul(y, dm, "tn", F32, "pool_dwout"))
            du, dwgrp, dscale = _pool_mid_bwd(dm, w["pool_w_out"][idx], p, w["pool_w_grp"][idx],
                                              w["pool_scale"][idx][None, :], "pool_mid_bwd")
            grads["pool_w_grp"].append(dwgrp)
            grads["pool_scale"].append(dscale[0])
            grads["pool_w_in"].append(_matmul(n, du, "tn", F32, "pool_dwin"))
            dn = _matmul(du, w["pool_w_in"][idx], "nt", F32, "pool_dn")
        dx, g_norm[i][0] = _rms_bwd(xin, gain(i, 0), dn, dx1, F32, "norm_bwd_res")

    out = {k: jnp.stack(v[::-1]) for k, v in grads.items()}
    out["norm_g"] = jnp.stack([jnp.concatenate(row, axis=0) for row in g_norm])
    return loss, dx, out


_AXES = ("x", "y", "c")


def _exchange(ops, name):
    n = len(ops)

    def body(*refs):
        srcs, outs = refs[:n], refs[n:2 * n]
        send_sems, recv_sems = refs[2 * n:]
        pos = {a: lax.axis_index(a) for a in _AXES}
        copies = []
        for i, (_, axis, mode) in enumerate(ops):
            me = pos[axis]
            peer = tuple(1 - pos[a] if a == axis else pos[a] for a in _AXES)
            if mode == "gather":
                src, dst = srcs[i], outs[i].at[me]
            else:
                src, dst = srcs[i].at[1 - me], outs[i]
            rc = pltpu.make_async_remote_copy(src_ref=src, dst_ref=dst, send_sem=send_sems.at[i],
                                              recv_sem=recv_sems.at[i], device_id=peer,
                                              device_id_type=pl.DeviceIdType.MESH)
            rc.start()
            copies.append(rc)
        for rc in copies:
            rc.wait_send()
            rc.wait_recv()

    out_shape = [jax.ShapeDtypeStruct((2,) + a.shape if mode == "gather" else a.shape[1:], a.dtype)
                 for a, _, mode in ops]
    hbm = pl.BlockSpec(memory_space=pl.ANY)
    return pl.pallas_call(
        body, name=name, in_specs=[hbm] * n, out_specs=[hbm] * n, out_shape=out_shape,
        scratch_shapes=[pltpu.SemaphoreType.DMA((n,)), pltpu.SemaphoreType.DMA((n,))],
    )(*[a for a, _, _ in ops])


def _gather_stage(arrays, axes, name):
    outs = _exchange([(a, ax, "gather") for a, ax in zip(arrays, axes)], name)
    return [lax.dynamic_update_index_in_dim(o, a, lax.axis_index(ax), 0) for o, a, ax in zip(outs, arrays, axes)]


def _add_half(a2, recv, me, out_dtype, name):
    shape = recv.shape
    cols = shape[-1]
    rows = math.prod(shape[:-1])
    tr = _pick(rows, (1024, 512, 256, 128, 64, 32, 16))

    def body(me_ref, a_ref, b_ref, o_ref):
        o_ref[...] = (a_ref[...].astype(F32) + b_ref[...].astype(F32)).astype(o_ref.dtype)

    out = pl.pallas_call(
        body, name=name,
        grid_spec=pltpu.PrefetchScalarGridSpec(
            num_scalar_prefetch=1, grid=(rows // tr,),
            in_specs=[pl.BlockSpec((None, tr, cols), lambda i, m: (m[0], i, 0)),
                      pl.BlockSpec((tr, cols), lambda i, m: (i, 0))],
            out_specs=pl.BlockSpec((tr, cols), lambda i, m: (i, 0))),
        out_shape=jax.ShapeDtypeStruct((rows, cols), out_dtype),
        compiler_params=_params(("parallel",)),
    )(me, a2.reshape(2, rows, cols), recv.reshape(rows, cols))
    return out.reshape(shape)


_WEIGHTS = {
    "norm_g": ((DEPTH, 4, D_MODEL), 2, True),
    "attn_w_qkv": ((2, D_MODEL, 4608), 2, False),
    "attn_w_o": ((2, ATTN_WIDTH, D_MODEL), 2, False),
    "conv_w_in": ((1, D_MODEL, 3 * D_MODEL), 2, False),
    "conv_w_dw": ((1, 3, D_MODEL), 2, True),
    "conv_w_out": ((1, D_MODEL, D_MODEL), 1, False),
    "pool_w_in": ((1, D_MODEL, D_MODEL), 1, False),
    "pool_w_grp": ((1, 4, POOL_GROUP_DIM, POOL_GROUP_DIM), 2, False),
    "pool_scale": ((1, D_MODEL), 1, True),
    "pool_w_out": ((1, D_MODEL, D_MODEL), 1, False),
    "ffn_w_up": ((DEPTH, D_MODEL, 2 * D_FF), 2, False),
    "ffn_w_dw": ((DEPTH, 3, 2 * D_FF), 2, True),
    "ffn_w_down": ((DEPTH, D_FF, D_MODEL), 1, False),
}
_NAMES = tuple(_WEIGHTS)
_VECTORS = tuple(k for k in _NAMES if _WEIGHTS[k][2])
_ROUTE_A = ("ffn_w_up", "attn_w_o", "conv_w_out", "pool_w_in")
_ROUTE_B = tuple(k for k in _NAMES if k not in _ROUTE_A and not _WEIGHTS[k][2]) + ("vectors",)
_AXES_A = ("y", "x", "c")
_AXES_B = ("x", "y", "c")
VECTOR_ROWS = 96


def _shard_shape(name):
    shape, ax, _ = _WEIGHTS[name]
    return tuple(s // N_DEV if i == ax else s for i, s in enumerate(shape))


def _full_from_slots(slots, name):
    shape, ax, _ = _WEIGHTS[name]
    return jnp.moveaxis(slots, 0, ax).reshape(shape)


def _slots_from_full(full, name):
    shape, ax, _ = _WEIGHTS[name]
    split = shape[:ax] + (N_DEV, shape[ax] // N_DEV) + shape[ax + 1:]
    return jnp.moveaxis(full.reshape(split), ax, 0)


def _pack_vectors(parts, lead):
    rows = [parts[k].reshape(lead + (-1, LANES)) for k in _VECTORS]
    used = sum(r.shape[-2] for r in rows)
    rows.append(jnp.zeros(lead + (VECTOR_ROWS - used, LANES), rows[0].dtype))
    return jnp.concatenate(rows, axis=len(lead))


def _unpack_vectors(buf, lead):
    out, r0 = {}, 0
    for k in _VECTORS:
        shard = _shard_shape(k)
        rows = math.prod(shard) // LANES
        out[k] = buf[..., r0:r0 + rows, :].reshape(lead + shard)
        r0 += rows
    return out


def _device_order(a, route_a):
    perm = (1, 2, 0) if route_a else (2, 1, 0)
    a = a.transpose(perm + tuple(range(3, a.ndim)))
    return a.reshape((N_DEV,) + a.shape[3:])


def _route_order(a, route_a):
    a = a.reshape((2, 2, 2) + a.shape[1:])
    perm = (2, 0, 1) if route_a else (2, 1, 0)
    return a.transpose(perm + tuple(range(3, a.ndim)))


def _all_gather_weights(shards):
    send = {k: shards[k].astype(BF16) for k in _NAMES if not _WEIGHTS[k][2]}
    send["vectors"] = _pack_vectors(shards, ())
    names = _ROUTE_A + _ROUTE_B
    arrays = [send[k] for k in names]
    for stage in range(3):
        axes = [_AXES_A[stage]] * len(_ROUTE_A) + [_AXES_B[stage]] * len(_ROUTE_B)
        arrays = _gather_stage(arrays, axes, f"gather_{stage + 1}")
    slots = {k: _device_order(a, k in _ROUTE_A) for k, a in zip(names, arrays)}
    full = {k: _full_from_slots(slots[k], k) for k in names if k != "vectors"}
    vec = _unpack_vectors(slots["vectors"], (N_DEV,))
    full.update({k: _full_from_slots(vec[k], k) for k in _VECTORS})
    return full


def _reduce_scatter_grads(grads):
    coord = {a: lax.axis_index(a).astype(jnp.int32).reshape(1) for a in _AXES}
    slots = {k: _slots_from_full(grads[k], k) for k in _NAMES}
    send = {k: slots[k].astype(BF16) for k in _NAMES if not _WEIGHTS[k][2]}
    send["vectors"] = _pack_vectors(slots, (N_DEV,)).astype(BF16)
    names = _ROUTE_A + _ROUTE_B
    arrays = [_route_order(send[k], k in _ROUTE_A) for k in names]
    arrays = [a.reshape(2, -1, a.shape[-1]) for a in arrays]
    for stage in range(3):
        axes = [_AXES_A[2 - stage]] * len(_ROUTE_A) + [_AXES_B[2 - stage]] * len(_ROUTE_B)
        recv = _exchange([(a, ax, "half") for a, ax in zip(arrays, axes)], f"scatter_{stage + 1}")
        dt = F32 if stage == 2 else BF16
        arrays = [_add_half(a, r, coord[ax], dt, f"scatter_add_{stage + 1}_{k}")
                  for a, r, ax, k in zip(arrays, recv, axes, names)]
        if stage < 2:
            arrays = [a.reshape((2, a.shape[0] // 2) + a.shape[1:]) for a in arrays]
    out = {k: a.reshape(_shard_shape(k)) for k, a in zip(names, arrays) if k != "vectors"}
    out.update(_unpack_vectors(arrays[names.index("vectors")], ()))
    return out


def _adamw(w, g, m, v, name):
    shape = w.shape
    cols = shape[-1]
    rows = math.prod(shape[:-1])
    tr = _pick(rows, (512, 256, 128, 64, 32, 16, 8))

    def body(w_ref, g_ref, m_ref, v_ref, d_ref, nm_ref, nv_ref):
        gv = g_ref[...]
        nm = ADAM_B1 * m_ref[...] + (1.0 - ADAM_B1) * gv
        nv = ADAM_B2 * v_ref[...] + (1.0 - ADAM_B2) * jnp.square(gv)
        m_hat = nm / (1.0 - ADAM_B1 ** ADAM_STEP)
        v_hat = nv / (1.0 - ADAM_B2 ** ADAM_STEP)
        d_ref[...] = -ADAM_LR * (m_hat / (jnp.sqrt(v_hat) + ADAM_EPS) + ADAM_WD * w_ref[...])
        nm_ref[...] = nm
        nv_ref[...] = nv

    blk = pl.BlockSpec((tr, cols), lambda i: (i, 0))
    shp = jax.ShapeDtypeStruct((rows, cols), F32)
    outs = pl.pallas_call(
        body, name=name, grid=(rows // tr,), in_specs=[blk] * 4, out_specs=[blk] * 3, out_shape=[shp] * 3,
        compiler_params=_params(("parallel",)),
    )(*[t.reshape(rows, cols) for t in (w, g, m, v)])
    return [o.reshape(shape) for o in outs]


def kernel(x, norm_g, attn_w_qkv, attn_w_o, conv_w_in, conv_w_dw, conv_w_out, pool_w_in, pool_w_grp, pool_scale, pool_w_out, ffn_w_up, ffn_w_dw, ffn_w_down, loss_target, m_norm_g, m_attn_w_qkv, m_attn_w_o, m_conv_w_in, m_conv_w_dw, m_conv_w_out, m_pool_w_in, m_pool_w_grp, m_pool_scale, m_pool_w_out, m_ffn_w_up, m_ffn_w_dw, m_ffn_w_down, v_norm_g, v_attn_w_qkv, v_attn_w_o, v_conv_w_in, v_conv_w_dw, v_conv_w_out, v_pool_w_in, v_pool_w_grp, v_pool_scale, v_pool_w_out, v_ffn_w_up, v_ffn_w_dw, v_ffn_w_down):
    shards = dict(zip(_NAMES, (norm_g, attn_w_qkv, attn_w_o, conv_w_in, conv_w_dw, conv_w_out, pool_w_in,
                               pool_w_grp, pool_scale, pool_w_out, ffn_w_up, ffn_w_dw, ffn_w_down)))
    moms = dict(zip(_NAMES, (m_norm_g, m_attn_w_qkv, m_attn_w_o, m_conv_w_in, m_conv_w_dw, m_conv_w_out,
                             m_pool_w_in, m_pool_w_grp, m_pool_scale, m_pool_w_out, m_ffn_w_up, m_ffn_w_dw,
                             m_ffn_w_down)))
    vels = dict(zip(_NAMES, (v_norm_g, v_attn_w_qkv, v_attn_w_o, v_conv_w_in, v_conv_w_dw, v_conv_w_out,
                             v_pool_w_in, v_pool_w_grp, v_pool_scale, v_pool_w_out, v_ffn_w_up, v_ffn_w_dw,
                             v_ffn_w_down)))
    full = _all_gather_weights(shards)
    loss, grad_x, grads = _local_step(x[0], loss_target[0], full)
    loss = lax.psum(loss[0, 0], _AXES)
    gsh = _reduce_scatter_grads(grads)
    deltas, new_m, new_v = [], [], []
    for k in _NAMES:
        d, nm, nv = _adamw(shards[k], gsh[k], moms[k], vels[k], f"adamw_{k}")
        deltas.append(d)
        new_m.append(nm)
        new_v.append(nv)
    return (loss, grad_x[None], *[gsh[k] for k in _NAMES], *deltas, *new_m, *new_v)
```

```python
import functools
import math

import numpy as np
import jax
import jax.numpy as jnp
from jax import lax
from jax.experimental import pallas as pl
from jax.experimental.pallas import tpu as pltpu

F32, BF16 = jnp.float32, jnp.bfloat16

D_MODEL = 1024
SEQ = 2048
DEPTH = 4
DILATED_CFG = ((128, 1), (512, 4), (2048, 16))
N_GROUPS_A = 3
HEADS = 8
HEAD_DIM = 64
ATTN_WIDTH = HEADS * HEAD_DIM
N_HEADS_A = N_GROUPS_A * HEADS
BLOCK = 128
NEG_INF = -1e30
POOL_GROUP_DIM = 256
D_FF = 2816
RMS_EPS = 1e-6
ADAM_LR, ADAM_B1, ADAM_B2, ADAM_EPS, ADAM_WD, ADAM_STEP = 0.001, 0.9, 0.999, 1e-08, 0.01, 10

N_DEV = 8
LANES = 128
V7X_VMEM_BYTES = 64 * 2 ** 20
VMEM_LIMIT_BYTES = V7X_VMEM_BYTES - 8 * 2 ** 20
COL_TILE = 256
ROW_TILE = 256
MATMUL_TILES = (1024, 1408, 512, 256, 128)
TN_RESIDENT_K = 2048

NN = (((1,), (0,)), ((), ()))
NT = (((1,), (1,)), ((), ()))
TN = (((0,), (0,)), ((), ()))


def _dot(a, b, dims=NN):
    return lax.dot_general(a, b, dims, preferred_element_type=F32)


def _params(sem=None):
    return pltpu.CompilerParams(dimension_semantics=sem, vmem_limit_bytes=VMEM_LIMIT_BYTES)


def _pick(n, prefs):
    for p in prefs:
        if n % p == 0:
            return p
    return n


def _matmul(a, b, mode, out_dtype, name, a_parts=1, b_parts=1):
    if mode == "nn":
        m, k = a.shape[-2], a.shape[-1] * a_parts
        n = b.shape[-1] * b_parts
    elif mode == "nt":
        m, k = a.shape[-2], a.shape[-1] * a_parts
        n = b.shape[-2]
    else:
        k, m = a.shape[-2], a.shape[-1] * a_parts
        n = b.shape[-1] * b_parts
    tm = _pick(m, MATMUL_TILES)
    tn = _pick(n // b_parts if mode != "nt" else n, MATMUL_TILES)
    kk = k // a_parts if mode != "tn" else k
    tk = _pick(kk, MATMUL_TILES)
    if mode == "tn":
        tm = _pick(m // a_parts, MATMUL_TILES)
        if k <= TN_RESIDENT_K:
            tk = k
    gm, gn, gk = m // tm, n // tn, k // tk

    def a_idx(i, j, kq):
        if mode == "tn":
            r, c, per = kq, i, (m // a_parts) // tm
        else:
            r, c, per = i, kq, (k // a_parts) // tk
        return (r, c) if a_parts == 1 else (c // per, r, c % per)

    def b_idx(i, j, kq):
        if mode == "nt":
            return (j, kq)
        per = (n // b_parts) // tn
        return (kq, j) if b_parts == 1 else (j // per, kq, j % per)

    a_blk = (tk, tm) if mode == "tn" else (tm, tk)
    b_blk = (tn, tk) if mode == "nt" else (tk, tn)
    if a_parts > 1:
        a_blk = (None,) + a_blk
    if b_parts > 1:
        b_blk = (None,) + b_blk
    dims = {"nn": NN, "nt": NT, "tn": TN}[mode]

    def body_single(a_ref, b_ref, o_ref):
        o_ref[...] = _dot(a_ref[...], b_ref[...], dims).astype(o_ref.dtype)

    def body(a_ref, b_ref, o_ref, acc_ref):
        kq = pl.program_id(2)

        @pl.when(kq == 0)
        def _():
            acc_ref[...] = jnp.zeros_like(acc_ref)

        acc_ref[...] += _dot(a_ref[...], b_ref[...], dims)

        @pl.when(kq == gk - 1)
        def _():
            o_ref[...] = acc_ref[...].astype(o_ref.dtype)

    return pl.pallas_call(
        body_single if gk == 1 else body, name=name, grid=(gm, gn, gk),
        in_specs=[pl.BlockSpec(a_blk, a_idx), pl.BlockSpec(b_blk, b_idx)],
        out_specs=pl.BlockSpec((tm, tn), lambda i, j, kq: (i, j)),
        out_shape=jax.ShapeDtypeStruct((m, n), out_dtype),
        scratch_shapes=[] if gk == 1 else [pltpu.VMEM((tm, tn), F32)],
        compiler_params=_params(("parallel", "parallel", "arbitrary")),
    )(a, b)


def _rms_fwd(xin, g, res, out_dtype, name):
    s, d = xin.shape
    has_res = res is not None

    def body(*refs):
        x_ref, g_ref = refs[0], refs[1]
        o_ref = refs[-1]
        x = x_ref[...]
        r = lax.rsqrt(jnp.mean(x * x, axis=-1, keepdims=True) + RMS_EPS)
        y = x * r * g_ref[...]
        if has_res:
            y = refs[2][...] + y
        o_ref[...] = y.astype(o_ref.dtype)

    row = pl.BlockSpec((ROW_TILE, d), lambda i: (i, 0))
    vec = pl.BlockSpec((1, d), lambda i: (0, 0))
    ins = [xin, g] + ([res] if has_res else [])
    return pl.pallas_call(
        body, name=name, grid=(s // ROW_TILE,),
        in_specs=[row, vec] + ([row] if has_res else []),
        out_specs=row, out_shape=jax.ShapeDtypeStruct((s, d), out_dtype),
        compiler_params=_params(("parallel",)),
    )(*ins)


def _rms_bwd(xin, g, dy, dres, out_dtype, name):
    s, d = xin.shape
    has_res = dres is not None

    def body(*refs):
        x_ref, g_ref, dy_ref = refs[0], refs[1], refs[2]
        dx_ref, dg_ref = refs[-2], refs[-1]

        @pl.when(pl.program_id(0) == 0)
        def _():
            dg_ref[...] = jnp.zeros_like(dg_ref)

        x = x_ref[...]
        dyv = dy_ref[...].astype(F32)
        r = lax.rsqrt(jnp.mean(x * x, axis=-1, keepdims=True) + RMS_EPS)
        xhat = x * r
        u = dyv * g_ref[...]
        dx = r * (u - xhat * jnp.mean(u * xhat, axis=-1, keepdims=True))
        if has_res:
            dx = refs[3][...] + dx
        dx_ref[...] = dx.astype(dx_ref.dtype)
        dg_ref[...] += jnp.sum(dyv * xhat, axis=0, keepdims=True)

    row = pl.BlockSpec((ROW_TILE, d), lambda i: (i, 0))
    vec = pl.BlockSpec((1, d), lambda i: (0, 0))
    ins = [xin, g, dy] + ([dres] if has_res else [])
    return pl.pallas_call(
        body, name=name, grid=(s // ROW_TILE,),
        in_specs=[row, vec, row] + ([row] if has_res else []),
        out_specs=[row, vec],
        out_shape=[jax.ShapeDtypeStruct((s, d), out_dtype), jax.ShapeDtypeStruct((1, d), F32)],
        compiler_params=_params(("arbitrary",)),
    )(*ins)


def _rms(x):
    r = lax.rsqrt(jnp.mean(x * x, axis=-1, keepdims=True) + RMS_EPS)
    return r, x * r


def _rms_grad(r, xhat, dy, g):
    u = dy * g
    return r * (u - xhat * jnp.mean(u * xhat, axis=-1, keepdims=True))


def _rms_res_pre(sub, g_post, res, g_pre, name):
    s, d = sub.shape

    def body(sub_ref, gp_ref, res_ref, gn_ref, x_ref, n_ref):
        xnew = res_ref[...] + _rms(sub_ref[...])[1] * gp_ref[...]
        x_ref[...] = xnew
        n_ref[...] = (_rms(xnew)[1] * gn_ref[...]).astype(BF16)

    row = pl.BlockSpec((ROW_TILE, d), lambda i: (i, 0))
    vec = pl.BlockSpec((1, d), lambda i: (0, 0))
    return pl.pallas_call(
        body, name=name, grid=(s // ROW_TILE,),
        in_specs=[row, vec, row, vec], out_specs=[row, row],
        out_shape=[jax.ShapeDtypeStruct((s, d), F32), jax.ShapeDtypeStruct((s, d), BF16)],
        compiler_params=_params(("parallel",)),
    )(sub, g_post, res, g_pre)


def _rms_bwd_pair(xmid, g_pre, dn, dres, sub, g_post, name):
    s, d = xmid.shape

    def body(x_ref, gn_ref, dn_ref, dres_ref, sub_ref, gp_ref, dx_ref, dsub_ref, dgn_ref, dgp_ref):
        @pl.when(pl.program_id(0) == 0)
        def _():
            dgn_ref[...] = jnp.zeros_like(dgn_ref)
            dgp_ref[...] = jnp.zeros_like(dgp_ref)

        dnv = dn_ref[...].astype(F32)
        r, xhat = _rms(x_ref[...])
        dx = dres_ref[...] + _rms_grad(r, xhat, dnv, gn_ref[...])
        dx_ref[...] = dx
        dgn_ref[...] += jnp.sum(dnv * xhat, axis=0, keepdims=True)
        rs, shat = _rms(sub_ref[...])
        dsub_ref[...] = _rms_grad(rs, shat, dx, gp_ref[...]).astype(BF16)
        dgp_ref[...] += jnp.sum(dx * shat, axis=0, keepdims=True)

    row = pl.BlockSpec((ROW_TILE, d), lambda i: (i, 0))
    vec = pl.BlockSpec((1, d), lambda i: (0, 0))
    return pl.pallas_call(
        body, name=name, grid=(s // ROW_TILE,),
        in_specs=[row, vec, row, row, row, vec], out_specs=[row, row, vec, vec],
        out_shape=[jax.ShapeDtypeStruct((s, d), F32), jax.ShapeDtypeStruct((s, d), BF16),
                   jax.ShapeDtypeStruct((1, d), F32), jax.ShapeDtypeStruct((1, d), F32)],
        compiler_params=_params(("arbitrary",)),
    )(xmid, g_pre, dn, dres, sub, g_post)


def _loss_head(y, tgt, name):
    s, d = y.shape

    def body(y_ref, t_ref, l_ref, dy_ref):
        @pl.when(pl.program_id(0) == 0)
        def _():
            l_ref[...] = jnp.zeros_like(l_ref)

        e = y_ref[...] - t_ref[...]
        dy_ref[...] = e / d
        per_tok = jnp.mean(e * e, axis=-1, keepdims=True)
        l_ref[...] += 0.5 * jnp.sum(per_tok, axis=0, keepdims=True)

    row = pl.BlockSpec((ROW_TILE, d), lambda i: (i, 0))
    return pl.pallas_call(
        body, name=name, grid=(s // ROW_TILE,),
        in_specs=[row, row],
        out_specs=[pl.BlockSpec((1, 1), lambda i: (0, 0)), row],
        out_shape=[jax.ShapeDtypeStruct((1, 1), F32), jax.ShapeDtypeStruct((s, d), F32)],
        compiler_params=_params(("arbitrary",)),
    )(y, tgt)


def _shift_down(x, k):
    rows = lax.broadcasted_iota(jnp.int32, x.shape, 0)
    return jnp.where(rows >= k, pltpu.roll(x, k, axis=0), 0.0)


def _shift_up(x, k):
    t = x.shape[0]
    rows = lax.broadcasted_iota(jnp.int32, x.shape, 0)
    return jnp.where(rows < t - k, pltpu.roll(x, t - k, axis=0), 0.0)


def _conv3(h, w):
    return w[2:3] * h + w[1:2] * _shift_down(h, 1) + w[0:1] * _shift_down(h, 2)


def _conv3_bwd_x(dc, w):
    return w[2:3] * dc + w[1:2] * _shift_up(dc, 1) + w[0:1] * _shift_up(dc, 2)


def _conv3_bwd_w(dc, h, dw_ref):
    dw_ref[0:1, :] = jnp.sum(dc * _shift_down(h, 2), axis=0, keepdims=True)
    dw_ref[1:2, :] = jnp.sum(dc * _shift_down(h, 1), axis=0, keepdims=True)
    dw_ref[2:3, :] = jnp.sum(dc * h, axis=0, keepdims=True)


def _ffn_up(n, wup, wdw, name):
    s, d = n.shape
    f = wup.shape[1] // 2
    tn = COL_TILE
    nj = f // tn

    def body(n_ref, wg_ref, wu_ref, dg_ref, du_ref, h_ref, a_ref):
        x = n_ref[...]
        hg = _dot(x, wg_ref[...])
        hu = _dot(x, wu_ref[...])
        h_ref[0] = hg.astype(BF16)
        h_ref[1] = hu.astype(BF16)
        cg = _conv3(hg, dg_ref[...])
        cu = _conv3(hu, du_ref[...])
        a_ref[...] = (cg * jax.nn.sigmoid(cg) * cu).astype(BF16)

    return pl.pallas_call(
        body, name=name, grid=(nj,),
        in_specs=[pl.BlockSpec((s, d), lambda j: (0, 0)),
                  pl.BlockSpec((d, tn), lambda j: (0, j)), pl.BlockSpec((d, tn), lambda j: (0, j + nj)),
                  pl.BlockSpec((3, tn), lambda j: (0, j)), pl.BlockSpec((3, tn), lambda j: (0, j + nj))],
        out_specs=[pl.BlockSpec((2, s, tn), lambda j: (0, 0, j)), pl.BlockSpec((s, tn), lambda j: (0, j))],
        out_shape=[jax.ShapeDtypeStruct((2, s, f), BF16), jax.ShapeDtypeStruct((s, f), BF16)],
        compiler_params=_params(("parallel",)),
    )(n, wup, wup, wdw, wdw)


def _ffn_mid_bwd(do, wdown, h, wdw, name):
    s, d = do.shape
    f = wdown.shape[0]
    tn = COL_TILE
    nj = f // tn

    def body(do_ref, wd_ref, h_ref, wg_ref, wu_ref, dh_ref, dwg_ref, dwu_ref):
        da = _dot(do_ref[...], wd_ref[...], NT)
        hg = h_ref[0].astype(F32)
        hu = h_ref[1].astype(F32)
        wg, wu = wg_ref[...], wu_ref[...]
        cg = _conv3(hg, wg)
        cu = _conv3(hu, wu)
        sg = jax.nn.sigmoid(cg)
        dcu = da * (cg * sg)
        dcg = da * cu * (sg * (1.0 + cg * (1.0 - sg)))
        dh_ref[0] = _conv3_bwd_x(dcg, wg).astype(BF16)
        dh_ref[1] = _conv3_bwd_x(dcu, wu).astype(BF16)
        _conv3_bwd_w(dcg, hg, dwg_ref)
        _conv3_bwd_w(dcu, hu, dwu_ref)

    return pl.pallas_call(
        body, name=name, grid=(nj,),
        in_specs=[pl.BlockSpec((s, d), lambda j: (0, 0)), pl.BlockSpec((tn, d), lambda j: (j, 0)),
                  pl.BlockSpec((2, s, tn), lambda j: (0, 0, j)),
                  pl.BlockSpec((3, tn), lambda j: (0, j)), pl.BlockSpec((3, tn), lambda j: (0, j + nj))],
        out_specs=[pl.BlockSpec((2, s, tn), lambda j: (0, 0, j)),
                   pl.BlockSpec((3, tn), lambda j: (0, j)), pl.BlockSpec((3, tn), lambda j: (0, j))],
        out_shape=[jax.ShapeDtypeStruct((2, s, f), BF16), jax.ShapeDtypeStruct((3, f), F32),
                   jax.ShapeDtypeStruct((3, f), F32)],
        compiler_params=_params(("parallel",)),
    )(do, wdown, h, wdw, wdw)


def _sconv_fwd(n, win, wdw, name):
    s, d = n.shape
    tn = COL_TILE
    nj = d // tn

    def body(n_ref, wb_ref, wc_ref, wh_ref, dw_ref, z_ref, y_ref):
        x = n_ref[...]
        zb = _dot(x, wb_ref[...])
        zc = _dot(x, wc_ref[...])
        zh = _dot(x, wh_ref[...])
        z_ref[0] = zb.astype(BF16)
        z_ref[1] = zc.astype(BF16)
        z_ref[2] = zh.astype(BF16)
        y_ref[...] = (zb * _conv3(zc * zh, dw_ref[...])).astype(BF16)

    return pl.pallas_call(
        body, name=name, grid=(nj,),
        in_specs=[pl.BlockSpec((s, d), lambda j: (0, 0)),
                  pl.BlockSpec((d, tn), lambda j: (0, j)), pl.BlockSpec((d, tn), lambda j: (0, j + nj)),
                  pl.BlockSpec((d, tn), lambda j: (0, j + 2 * nj)), pl.BlockSpec((3, tn), lambda j: (0, j))],
        out_specs=[pl.BlockSpec((3, s, tn), lambda j: (0, 0, j)), pl.BlockSpec((s, tn), lambda j: (0, j))],
        out_shape=[jax.ShapeDtypeStruct((3, s, d), BF16), jax.ShapeDtypeStruct((s, d), BF16)],
        compiler_params=_params(("parallel",)),
    )(n, win, win, win, wdw)


def _sconv_mid_bwd(dm, wout, z, wdw, name):
    s, d = dm.shape
    tn = COL_TILE
    nj = d // tn

    def body(dm_ref, wo_ref, z_ref, w_ref, dz_ref, dw_ref):
        dy = _dot(dm_ref[...], wo_ref[...], NT)
        zb = z_ref[0].astype(F32)
        zc = z_ref[1].astype(F32)
        zh = z_ref[2].astype(F32)
        w = w_ref[...]
        p = zc * zh
        cp = _conv3(p, w)
        dz_ref[0] = (dy * cp).astype(BF16)
        dcp = dy * zb
        dp = _conv3_bwd_x(dcp, w)
        _conv3_bwd_w(dcp, p, dw_ref)
        dz_ref[1] = (dp * zh).astype(BF16)
        dz_ref[2] = (dp * zc).astype(BF16)

    return pl.pallas_call(
        body, name=name, grid=(nj,),
        in_specs=[pl.BlockSpec((s, d), lambda j: (0, 0)), pl.BlockSpec((tn, d), lambda j: (j, 0)),
                  pl.BlockSpec((3, s, tn), lambda j: (0, 0, j)), pl.BlockSpec((3, tn), lambda j: (0, j))],
        out_specs=[pl.BlockSpec((3, s, tn), lambda j: (0, 0, j)), pl.BlockSpec((3, tn), lambda j: (0, j))],
        out_shape=[jax.ShapeDtypeStruct((3, s, d), BF16), jax.ShapeDtypeStruct((3, d), F32)],
        compiler_params=_params(("parallel",)),
    )(dm, wout, z, wdw)


def _pool_select(g, c2, c4, c8, c16):
    return jnp.where(g == 0, c2, jnp.where(g == 1, c4, jnp.where(g == 2, c8, c16)))


def _pool_inv_count(g, shape):
    pos = lax.broadcasted_iota(jnp.int32, shape, 0).astype(F32) + 1.0
    win = (2 << g).astype(F32)
    return jnp.minimum(pos, win)


def _pool_fwd(n, win, wgrp, scale, name):
    s, d = n.shape
    tn = POOL_GROUP_DIM

    def body(n_ref, wi_ref, wg_ref, sc_ref, p_ref, y_ref):
        g = pl.program_id(0)
        u = _dot(n_ref[...], wi_ref[...])
        s2 = u + _shift_down(u, 1)
        s4 = s2 + _shift_down(s2, 2)
        s8 = s4 + _shift_down(s4, 4)
        s16 = s8 + _shift_down(s8, 8)
        tot = _pool_select(g, s2, s4, s8, s16)
        p = (tot / _pool_inv_count(g, u.shape) - u).astype(BF16)
        p_ref[...] = p
        y_ref[...] = (_dot(p, wg_ref[...]) * sc_ref[...]).astype(BF16)

    return pl.pallas_call(
        body, name=name, grid=(d // tn,),
        in_specs=[pl.BlockSpec((s, d), lambda g: (0, 0)), pl.BlockSpec((d, tn), lambda g: (0, g)),
                  pl.BlockSpec((None, tn, tn), lambda g: (g, 0, 0)), pl.BlockSpec((1, tn), lambda g: (0, g))],
        out_specs=[pl.BlockSpec((s, tn), lambda g: (0, g)), pl.BlockSpec((s, tn), lambda g: (0, g))],
        out_shape=[jax.ShapeDtypeStruct((s, d), BF16), jax.ShapeDtypeStruct((s, d), BF16)],
        compiler_params=_params(("parallel",)),
    )(n, win, wgrp, scale)


def _pool_mid_bwd(dm, wout, p, wgrp, scale, name):
    s, d = dm.shape
    tn = POOL_GROUP_DIM

    def body(dm_ref, wo_ref, p_ref, wg_ref, sc_ref, du_ref, dwg_ref, dsc_ref):
        g = pl.program_id(0)
        dy = _dot(dm_ref[...], wo_ref[...], NT)
        pv = p_ref[...]
        wg = wg_ref[...]
        ypre = _dot(pv, wg)
        dsc_ref[...] = jnp.sum(dy * ypre, axis=0, keepdims=True)
        dypre = (dy * sc_ref[...]).astype(BF16)
        dwg_ref[...] = _dot(pv, dypre, TN)
        dp = _dot(dypre, wg, NT)
        e = dp / _pool_inv_count(g, dp.shape)
        f2 = e + _shift_up(e, 1)
        f4 = f2 + _shift_up(f2, 2)
        f8 = f4 + _shift_up(f4, 4)
        f16 = f8 + _shift_up(f8, 8)
        du_ref[...] = (_pool_select(g, f2, f4, f8, f16) - dp).astype(BF16)

    return pl.pallas_call(
        body, name=name, grid=(d // tn,),
        in_specs=[pl.BlockSpec((s, d), lambda g: (0, 0)), pl.BlockSpec((tn, d), lambda g: (g, 0)),
                  pl.BlockSpec((s, tn), lambda g: (0, g)), pl.BlockSpec((None, tn, tn), lambda g: (g, 0, 0)),
                  pl.BlockSpec((1, tn), lambda g: (0, g))],
        out_specs=[pl.BlockSpec((s, tn), lambda g: (0, g)), pl.BlockSpec((None, tn, tn), lambda g: (g, 0, 0)),
                   pl.BlockSpec((1, tn), lambda g: (0, g))],
        out_shape=[jax.ShapeDtypeStruct((s, d), BF16), jax.ShapeDtypeStruct((4, tn, tn), F32),
                   jax.ShapeDtypeStruct((1, d), F32)],
        compiler_params=_params(("parallel",)),
    )(dm, wout, p, wgrp, scale)


def _alibi_slopes(g):
    all_slopes = 2.0 ** (-8.0 * np.arange(1, N_HEADS_A + 1) / N_HEADS_A)
    return jnp.asarray(all_slopes[g * HEADS:(g + 1) * HEADS], F32).reshape(HEADS, 1, 1)


BHQK = (((2,), (2,)), ((0,), (0,)))
BHQC = (((2,), (1,)), ((0,), (0,)))
BHKC = (((1,), (1,)), ((0,), (0,)))


def _attn_window(n, ln):
    if ln == BLOCK:
        return 0, BLOCK
    return pl.multiple_of(jnp.maximum(n - 1, 0) * BLOCK, BLOCK), 2 * BLOCK


def _attn_scores(q, keys, slope, n, k0):
    kw = keys.shape[1]
    qpos = n * BLOCK + lax.broadcasted_iota(jnp.int32, (BLOCK, kw), 0)
    kpos = k0 + lax.broadcasted_iota(jnp.int32, (BLOCK, kw), 1)
    dist = qpos - kpos
    valid = (dist >= 0) & (dist <= BLOCK)
    s = _dot(q, keys, BHQK) * (HEAD_DIM ** -0.5) - slope * dist.astype(F32)[None]
    return jnp.where(valid[None], s, NEG_INF)


def _attn_specs(hb, ln):
    def part(t):
        return pl.BlockSpec((None, hb, None, ln, HEAD_DIM), lambda h, r, n: (t, h, r, 0, 0))
    qblk = pl.BlockSpec((None, hb, None, BLOCK, HEAD_DIM), lambda h, r, n: (0, h, r, n, 0))
    row = pl.BlockSpec((hb, None, BLOCK, HEAD_DIM), lambda h, r, n: (h, r, n, 0))
    stat = pl.BlockSpec((hb, None, BLOCK, 1), lambda h, r, n: (h, r, n, 0))
    whole = pl.BlockSpec((hb, None, ln, HEAD_DIM), lambda h, r, n: (h, r, 0, 0))
    slope = pl.BlockSpec((hb, 1, 1), lambda h, r, n: (h, 0, 0))
    return part, qblk, row, stat, whole, slope


def _attn_fwd(qkv, slopes, dil, name):
    _, hh, _, ln, _ = qkv.shape
    nb = ln // BLOCK

    def body(sl_ref, q_ref, k_ref, v_ref, o_ref, lse_ref):
        n = pl.program_id(2)
        k0, kw = _attn_window(n, ln)
        s = _attn_scores(q_ref[...], k_ref[:, pl.ds(k0, kw), :], sl_ref[...] * float(dil), n, k0)
        m = jnp.max(s, axis=-1, keepdims=True)
        p = jnp.exp(s - m)
        den = jnp.sum(p, axis=-1, keepdims=True)
        o_ref[...] = _dot(p.astype(BF16), v_ref[:, pl.ds(k0, kw), :], BHQC) / den
        lse_ref[...] = m + jnp.log(den)

    part, qblk, row, stat, _, slope = _attn_specs(hh, ln)
    return pl.pallas_call(
        body, name=name, grid=(1, dil, nb),
        in_specs=[slope, qblk, part(1), part(2)],
        out_specs=[row, stat],
        out_shape=[jax.ShapeDtypeStruct((hh, dil, ln, HEAD_DIM), F32), jax.ShapeDtypeStruct((hh, dil, ln, 1), F32)],
        compiler_params=_params(("parallel", "parallel", "parallel")),
    )(slopes, qkv, qkv, qkv)


ATTN_BWD_HEADS = 4


def _attn_bwd(qkv, do, lse, dvec, slopes, dil, name):
    _, hh, _, ln, _ = qkv.shape
    nb = ln // BLOCK
    scale = HEAD_DIM ** -0.5

    def body(sl_ref, q_ref, k_ref, v_ref, do_ref, lse_ref, dd_ref, dq_ref, dk_ref, dv_ref):
        n = pl.program_id(2)

        @pl.when(n == 0)
        def _():
            dk_ref[...] = jnp.zeros_like(dk_ref)
            dv_ref[...] = jnp.zeros_like(dv_ref)

        k0, kw = _attn_window(n, ln)
        win = pl.ds(k0, kw)
        q, keys, dob = q_ref[...], k_ref[:, win, :], do_ref[...]
        s = _attn_scores(q, keys, sl_ref[...] * float(dil), n, k0)
        p = jnp.exp(s - lse_ref[...])
        ds = (p * (_dot(dob, v_ref[:, win, :], BHQK) - dd_ref[...])).astype(BF16)
        dq_ref[...] = scale * _dot(ds, keys, BHQC)
        dv_ref[:, win, :] += _dot(p.astype(BF16), dob, BHKC)
        dk_ref[:, win, :] += scale * _dot(ds, q, BHKC)

    part, qblk, row, stat, whole, slope = _attn_specs(ATTN_BWD_HEADS, ln)
    shp = jax.ShapeDtypeStruct((hh, dil, ln, HEAD_DIM), F32)
    return pl.pallas_call(
        body, name=name, grid=(hh // ATTN_BWD_HEADS, dil, nb),
        in_specs=[slope, qblk, part(1), part(2), row, stat, stat],
        out_specs=[row, whole, whole], out_shape=[shp, shp, shp],
        compiler_params=_params(("parallel", "parallel", "arbitrary")),
    )(slopes, qkv, qkv, qkv, do, lse, dvec)


def _attn_merge(o, lse, name):
    _, s, w = o.shape

    def body(o_ref, l_ref, m_ref, mb_ref, lse_ref):
        l = l_ref[...]
        mx = jnp.max(l, axis=0)
        e = jnp.exp(l - mx[None])
        z = jnp.sum(e, axis=0)
        lse_ref[...] = mx + jnp.log(z)
        wts = e / z[None]
        for h in range(HEADS):
            cols = slice(h * HEAD_DIM, (h + 1) * HEAD_DIM)
            acc = wts[0][:, h:h + 1] * o_ref[0, :, cols]
            for g in range(1, N_GROUPS_A):
                acc = acc + wts[g][:, h:h + 1] * o_ref[g, :, cols]
            m_ref[:, cols] = acc
            mb_ref[:, cols] = acc.astype(BF16)

    return pl.pallas_call(
        body, name=name, grid=(s // ROW_TILE,),
        in_specs=[pl.BlockSpec((N_GROUPS_A, ROW_TILE, w), lambda i: (0, i, 0)),
                  pl.BlockSpec((N_GROUPS_A, ROW_TILE, HEADS), lambda i: (0, i, 0))],
        out_specs=[pl.BlockSpec((ROW_TILE, w), lambda i: (i, 0)), pl.BlockSpec((ROW_TILE, w), lambda i: (i, 0)),
                   pl.BlockSpec((ROW_TILE, HEADS), lambda i: (i, 0))],
        out_shape=[jax.ShapeDtypeStruct((s, w), F32), jax.ShapeDtypeStruct((s, w), BF16),
                   jax.ShapeDtypeStruct((s, HEADS), F32)],
        compiler_params=_params(("parallel",)),
    )(o, lse)


def _attn_dvec(dmerged, merged, name):
    s, w = merged.shape

    def body(dm_ref, m_ref, dd_ref, dmb_ref):
        dmv = dm_ref[...]
        dmb_ref[...] = dmv.astype(BF16)
        prod = dmv * m_ref[...]
        for h in range(HEADS):
            dd_ref[:, h:h + 1] = jnp.sum(prod[:, h * HEAD_DIM:(h + 1) * HEAD_DIM], axis=-1, keepdims=True)

    row = pl.BlockSpec((ROW_TILE, w), lambda i: (i, 0))
    return pl.pallas_call(
        body, name=name, grid=(s // ROW_TILE,),
        in_specs=[row, row],
        out_specs=[pl.BlockSpec((ROW_TILE, HEADS), lambda i: (i, 0)), row],
        out_shape=[jax.ShapeDtypeStruct((s, HEADS), F32), jax.ShapeDtypeStruct((s, w), BF16)],
        compiler_params=_params(("parallel",)),
    )(dmerged, merged)


def _to_group(a, dil):
    s, w = a.shape
    c = w // HEADS
    return a.reshape(s // dil, dil, HEADS, c).transpose(2, 1, 0, 3)


def _from_group(a):
    hh, dil, ln, c = a.shape
    return a.transpose(2, 1, 0, 3).reshape(ln * dil, hh * c)


def _attention_fwd(n, wqkv, wo, tag):
    s = n.shape[0]
    qkv = _matmul(n, wqkv, "nn", BF16, f"{tag}_qkv")
    qkv6 = qkv.reshape(s, N_GROUPS_A, 3, HEADS, HEAD_DIM)
    groups, outs, lses = [], [], []
    for g, (_, dil) in enumerate(DILATED_CFG):
        qg = qkv6[:, g].reshape(s // dil, dil, 3, HEADS, HEAD_DIM).transpose(2, 3, 1, 0, 4)
        o, lse = _attn_fwd(qg, _alibi_slopes(g), dil, f"{tag}_fwd_g{g}")
        groups.append(qg)
        outs.append(_from_group(o))
        lses.append(_from_group(lse))
    merged, merged_bf, lse_all = _attn_merge(jnp.stack(outs), jnp.stack(lses), f"{tag}_merge")
    m = _matmul(merged_bf, wo, "nn", F32, f"{tag}_wo")
    return m, (groups, merged, merged_bf, lse_all)


def _attention_bwd(dm, n, wqkv, wo, saved, tag):
    groups, merged, merged_bf, lse_all = saved
    d_wo = _matmul(merged_bf, dm, "tn", BF16, f"{tag}_dwo")
    dmerged = _matmul(dm, wo, "nt", F32, f"{tag}_dmerged")
    dvec, dmerged_bf = _attn_dvec(dmerged, merged, f"{tag}_dvec")
    cols = []
    for g, (_, dil) in enumerate(DILATED_CFG):
        dq, dk, dv = _attn_bwd(groups[g], _to_group(dmerged_bf, dil), _to_group(lse_all, dil),
                               _to_group(dvec, dil), _alibi_slopes(g), dil, f"{tag}_bwd_g{g}")
        cols += [_from_group(dq), _from_group(dk), _from_group(dv)]
    dqkv = jnp.concatenate(cols, axis=1).astype(BF16)
    d_wqkv = _matmul(n, dqkv, "tn", BF16, f"{tag}_dwqkv")
    dn = _matmul(dqkv, wqkv, "nt", F32, f"{tag}_dn")
    return dn, d_wqkv, d_wo


def _local_step(x, tgt, w):
    ng = w["norm_g"]

    def gain(i, j):
        return ng[i, j][None, :]

    saved = []
    ia = ib = ic = 0
    n = _rms_fwd(x, gain(0, 0), None, BF16, "norm_first")
    for i in range(DEPTH):
        kind = i % 3
        if kind == 0:
            m, ms = _attention_fwd(n, w["attn_w_qkv"][ia], w["attn_w_o"][ia], "attn")
            mixer = ("a", ia, ms)
            ia += 1
        elif kind == 1:
            z, y = _sconv_fwd(n, w["conv_w_in"][ib], w["conv_w_dw"][ib], "sconv_fwd")
            m = _matmul(y, w["conv_w_out"][ib], "nn", F32, "sconv_out")
            mixer = ("b", ib, (z, y))
            ib += 1
        else:
            p, y = _pool_fwd(n, w["pool_w_in"][ic], w["pool_w_grp"][ic], w["pool_scale"][ic][None, :], "pool_fwd")
            m = _matmul(y, w["pool_w_out"][ic], "nn", F32, "pool_out")
            mixer = ("c", ic, (p, y))
            ic += 1
        x1, n2 = _rms_res_pre(m, gain(i, 1), x, gain(i, 2), "norm_res_pre")
        h, a = _ffn_up(n2, w["ffn_w_up"][i], w["ffn_w_dw"][i], "ffn_up")
        f = _matmul(a, w["ffn_w_down"][i], "nn", F32, "ffn_down")
        saved.append((x, n, m, mixer, x1, n2, h, a, f))
        if i + 1 < DEPTH:
            x, n = _rms_res_pre(f, gain(i, 3), x1, gain(i + 1, 0), "norm_res_pre")
        else:
            x = _rms_fwd(f, gain(i, 3), x1, F32, "norm_res")

    loss, dx = _loss_head(x, tgt, "loss_head")

    g_norm = [[None] * 4 for _ in range(DEPTH)]
    grads = {k: [] for k in ("attn_w_qkv", "attn_w_o", "conv_w_in", "conv_w_dw", "conv_w_out", "pool_w_in",
                             "pool_w_grp", "pool_scale", "pool_w_out", "ffn_w_up", "ffn_w_dw", "ffn_w_down")}
    df, g_norm[DEPTH - 1][3] = _rms_bwd(saved[-1][-1], gain(DEPTH - 1, 3), dx, None, BF16, "norm_bwd_sub")
    for i in reversed(range(DEPTH)):
        xin, n, m, mixer, x1, n2, h, a, f = saved[i]
        grads["ffn_w_down"].append(_matmul(a, df, "tn", BF16, "ffn_dwdown"))
        dh, dwg, dwu = _ffn_mid_bwd(df, w["ffn_w_down"][i], h, w["ffn_w_dw"][i], "ffn_mid_bwd")
        grads["ffn_w_dw"].append(jnp.concatenate([dwg, dwu], axis=1))
        grads["ffn_w_up"].append(_matmul(n2, dh, "tn", BF16, "ffn_dwup", b_parts=2))
        dn2 = _matmul(dh, w["ffn_w_up"][i], "nt", F32, "ffn_dn", a_parts=2)
        dx1, dm, g_norm[i][2], g_norm[i][1] = _rms_bwd_pair(x1, gain(i, 2), dn2, dx, m, gain(i, 1), "norm_bwd_pair")
        kind, idx, ms = mixer
        if kind == "a":
            dn, d_wqkv, d_wo = _attention_bwd(dm, n, w["attn_w_qkv"][idx], w["attn_w_o"][idx], ms, "attn")
            grads["attn_w_qkv"].append(d_wqkv)
            grads["attn_w_o"].append(d_wo)
        elif kind == "b":
            z, y = ms
            grads["conv_w_out"].append(_matmul(y, dm, "tn", BF16, "sconv_dwout"))
            dz, ddw = _sconv_mid_bwd(dm, w["conv_w_out"][idx], z, w["conv_w_dw"][idx], "sconv_mid_bwd")
            grads["conv_w_dw"].append(ddw)
            grads["conv_w_in"].append(_matmul(n, dz, "tn", BF16, "sconv_dwin", b_parts=3))
            dn = _matmul(dz, w["conv_w_in"][idx], "nt", F32, "sconv_dn", a_parts=3)
        else:
            p, y = ms
            grads["pool_w_out"].append(_matmul(y, dm, "tn", BF16, "pool_dwout"))
            du, dwgrp, dscale = _pool_mid_bwd(dm, w["pool_w_out"][idx], p, w["pool_w_grp"][idx],
                                              w["pool_scale"][idx][None, :], "pool_mid_bwd")
            grads["pool_w_grp"].append(dwgrp)
            grads["pool_scale"].append(dscale[0])
            grads["pool_w_in"].append(_matmul(n, du, "tn", BF16, "pool_dwin"))
            dn = _matmul(du, w["pool_w_in"][idx], "nt", F32, "pool_dn")
        if i > 0:
            dx, df, g_norm[i][0], g_norm[i - 1][3] = _rms_bwd_pair(xin, gain(i, 0), dn, dx1, saved[i - 1][-1],
                                                                   gain(i - 1, 3), "norm_bwd_pair")
        else:
            dx, g_norm[0][0] = _rms_bwd(xin, gain(0, 0), dn, dx1, F32, "norm_bwd_res")

    out = {k: jnp.stack(v[::-1]) for k, v in grads.items()}
    out["norm_g"] = jnp.stack([jnp.concatenate(row, axis=0) for row in g_norm])
    return loss, dx, out


_AXES = ("x", "y", "c")


def _exchange(ops, name):
    n = len(ops)

    def body(*refs):
        srcs, outs = refs[:n], refs[n:2 * n]
        send_sems, recv_sems = refs[2 * n:]
        pos = {a: lax.axis_index(a) for a in _AXES}
        copies = []
        for i, (_, axis, mode) in enumerate(ops):
            me = pos[axis]
            peer = tuple(1 - pos[a] if a == axis else pos[a] for a in _AXES)
            if mode == "gather":
                src, dst = srcs[i], outs[i].at[me]
            else:
                src, dst = srcs[i].at[1 - me], outs[i]
            rc = pltpu.make_async_remote_copy(src_ref=src, dst_ref=dst, send_sem=send_sems.at[i],
                                              recv_sem=recv_sems.at[i], device_id=peer,
                                              device_id_type=pl.DeviceIdType.MESH)
            rc.start()
            copies.append(rc)
        for rc in copies:
            rc.wait_send()
            rc.wait_recv()

    out_shape = [jax.ShapeDtypeStruct((2,) + a.shape if mode == "gather" else a.shape[1:], a.dtype)
                 for a, _, mode in ops]
    hbm = pl.BlockSpec(memory_space=pl.ANY)
    return pl.pallas_call(
        body, name=name, in_specs=[hbm] * n, out_specs=[hbm] * n, out_shape=out_shape,
        scratch_shapes=[pltpu.SemaphoreType.DMA((n,)), pltpu.SemaphoreType.DMA((n,))],
    )(*[a for a, _, _ in ops])


def _gather_stage(arrays, axes, name):
    outs = _exchange([(a, ax, "gather") for a, ax in zip(arrays, axes)], name)
    return [lax.dynamic_update_index_in_dim(o, a, lax.axis_index(ax), 0) for o, a, ax in zip(outs, arrays, axes)]


def _add_half(a2, recv, me, out_dtype, name):
    shape = recv.shape
    cols = shape[-1]
    rows = math.prod(shape[:-1])
    tr = _pick(rows, (1024, 512, 256, 128, 64, 32, 16))

    def body(me_ref, a_ref, b_ref, o_ref):
        o_ref[...] = (a_ref[...].astype(F32) + b_ref[...].astype(F32)).astype(o_ref.dtype)

    out = pl.pallas_call(
        body, name=name,
        grid_spec=pltpu.PrefetchScalarGridSpec(
            num_scalar_prefetch=1, grid=(rows // tr,),
            in_specs=[pl.BlockSpec((None, tr, cols), lambda i, m: (m[0], i, 0)),
                      pl.BlockSpec((tr, cols), lambda i, m: (i, 0))],
            out_specs=pl.BlockSpec((tr, cols), lambda i, m: (i, 0))),
        out_shape=jax.ShapeDtypeStruct((rows, cols), out_dtype),
        compiler_params=_params(("parallel",)),
    )(me, a2.reshape(2, rows, cols), recv.reshape(rows, cols))
    return out.reshape(shape)


_WEIGHTS = {
    "norm_g": ((DEPTH, 4, D_MODEL), 2, True),
    "attn_w_qkv": ((2, D_MODEL, 4608), 2, False),
    "attn_w_o": ((2, ATTN_WIDTH, D_MODEL), 2, False),
    "conv_w_in": ((1, D_MODEL, 3 * D_MODEL), 2, False),
    "conv_w_dw": ((1, 3, D_MODEL), 2, True),
    "conv_w_out": ((1, D_MODEL, D_MODEL), 1, False),
    "pool_w_in": ((1, D_MODEL, D_MODEL), 1, False),
    "pool_w_grp": ((1, 4, POOL_GROUP_DIM, POOL_GROUP_DIM), 2, False),
    "pool_scale": ((1, D_MODEL), 1, True),
    "pool_w_out": ((1, D_MODEL, D_MODEL), 1, False),
    "ffn_w_up": ((DEPTH, D_MODEL, 2 * D_FF), 2, False),
    "ffn_w_dw": ((DEPTH, 3, 2 * D_FF), 2, True),
    "ffn_w_down": ((DEPTH, D_FF, D_MODEL), 1, False),
}
_NAMES = tuple(_WEIGHTS)
_VECTORS = tuple(k for k in _NAMES if _WEIGHTS[k][2])
_ROUTE_A = ("ffn_w_up", "attn_w_o", "conv_w_out", "pool_w_in")
_ROUTE_B = tuple(k for k in _NAMES if k not in _ROUTE_A and not _WEIGHTS[k][2]) + ("vectors",)
_AXES_A = ("y", "x", "c")
_AXES_B = ("x", "y", "c")
VECTOR_ROWS = 96


def _shard_shape(name):
    shape, ax, _ = _WEIGHTS[name]
    return tuple(s // N_DEV if i == ax else s for i, s in enumerate(shape))


def _full_from_slots(slots, name, layers=None):
    shape, ax, _ = _WEIGHTS[name]
    if layers is not None:
        shape = (layers,) + shape[1:]
    return jnp.moveaxis(slots, 0, ax).reshape(shape)


def _slots_from_full(full, name):
    shape, ax, _ = _WEIGHTS[name]
    split = shape[:ax] + (N_DEV, shape[ax] // N_DEV) + shape[ax + 1:]
    return jnp.moveaxis(full.reshape(split), ax, 0)


def _pack_vectors(parts, lead):
    rows = [parts[k].reshape(lead + (-1, LANES)) for k in _VECTORS]
    used = sum(r.shape[-2] for r in rows)
    rows.append(jnp.zeros(lead + (VECTOR_ROWS - used, LANES), rows[0].dtype))
    return jnp.concatenate(rows, axis=len(lead))


def _unpack_vectors(buf, lead):
    out, r0 = {}, 0
    for k in _VECTORS:
        shard = _shard_shape(k)
        rows = math.prod(shard) // LANES
        out[k] = buf[..., r0:r0 + rows, :].reshape(lead + shard)
        r0 += rows
    return out


def _device_order(a, route_a):
    perm = (1, 2, 0) if route_a else (2, 1, 0)
    a = a.transpose(perm + tuple(range(3, a.ndim)))
    return a.reshape((N_DEV,) + a.shape[3:])


def _route_order(a, route_a):
    a = a.reshape((2, 2, 2) + a.shape[1:])
    perm = (2, 0, 1) if route_a else (2, 1, 0)
    return a.transpose(perm + tuple(range(3, a.ndim)))


def _all_gather_weights(shards):
    send = {k: shards[k].astype(BF16) for k in _NAMES if not _WEIGHTS[k][2]}
    send["vectors"] = _pack_vectors(shards, ())
    names = _ROUTE_A + _ROUTE_B
    arrays = [send[k] for k in names]
    for stage in range(3):
        axes = [_AXES_A[stage]] * len(_ROUTE_A) + [_AXES_B[stage]] * len(_ROUTE_B)
        arrays = _gather_stage(arrays, axes, f"gather_{stage + 1}")
    slots = {k: _device_order(a, k in _ROUTE_A) for k, a in zip(names, arrays)}
    full = {k: [_full_from_slots(slots[k][:, l:l + 1], k, layers=1)[0] for l in range(_WEIGHTS[k][0][0])]
            for k in names if k != "vectors"}
    vec = _unpack_vectors(slots["vectors"], (N_DEV,))
    full.update({k: _full_from_slots(vec[k], k) for k in _VECTORS})
    return full


def _reduce_scatter_grads(grads):
    coord = {a: lax.axis_index(a).astype(jnp.int32).reshape(1) for a in _AXES}
    slots = {k: _slots_from_full(grads[k], k) for k in _NAMES}
    send = {k: slots[k].astype(BF16) for k in _NAMES if not _WEIGHTS[k][2]}
    send["vectors"] = _pack_vectors(slots, (N_DEV,)).astype(BF16)
    names = _ROUTE_A + _ROUTE_B
    arrays = [_route_order(send[k], k in _ROUTE_A) for k in names]
    arrays = [a.reshape(2, -1, a.shape[-1]) for a in arrays]
    for stage in range(3):
        axes = [_AXES_A[2 - stage]] * len(_ROUTE_A) + [_AXES_B[2 - stage]] * len(_ROUTE_B)
        recv = _exchange([(a, ax, "half") for a, ax in zip(arrays, axes)], f"scatter_{stage + 1}")
        dt = F32 if stage == 2 else BF16
        arrays = [_add_half(a, r, coord[ax], dt, f"scatter_add_{stage + 1}_{k}")
                  for a, r, ax, k in zip(arrays, recv, axes, names)]
        if stage < 2:
            arrays = [a.reshape((2, a.shape[0] // 2) + a.shape[1:]) for a in arrays]
    out = {k: a.reshape(_shard_shape(k)) for k, a in zip(names, arrays) if k != "vectors"}
    out.update(_unpack_vectors(arrays[names.index("vectors")], ()))
    return out


def _adamw(w, g, m, v, name):
    shape = w.shape
    cols = shape[-1]
    rows = math.prod(shape[:-1])
    tr = _pick(rows, (512, 256, 128, 64, 32, 16, 8))

    def body(w_ref, g_ref, m_ref, v_ref, d_ref, nm_ref, nv_ref):
        gv = g_ref[...]
        nm = ADAM_B1 * m_ref[...] + (1.0 - ADAM_B1) * gv
        nv = ADAM_B2 * v_ref[...] + (1.0 - ADAM_B2) * jnp.square(gv)
        m_hat = nm / (1.0 - ADAM_B1 ** ADAM_STEP)
        v_hat = nv / (1.0 - ADAM_B2 ** ADAM_STEP)
        d_ref[...] = -ADAM_LR * (m_hat / (jnp.sqrt(v_hat) + ADAM_EPS) + ADAM_WD * w_ref[...])
        nm_ref[...] = nm
        nv_ref[...] = nv

    blk = pl.BlockSpec((tr, cols), lambda i: (i, 0))
    shp = jax.ShapeDtypeStruct((rows, cols), F32)
    outs = pl.pallas_call(
        body, name=name, grid=(rows // tr,), in_specs=[blk] * 4, out_specs=[blk] * 3, out_shape=[shp] * 3,
        compiler_params=_params(("parallel",)),
    )(*[t.reshape(rows, cols) for t in (w, g, m, v)])
    return [o.reshape(shape) for o in outs]


def kernel(x, norm_g, attn_w_qkv, attn_w_o, conv_w_in, conv_w_dw, conv_w_out, pool_w_in, pool_w_grp, pool_scale, pool_w_out, ffn_w_up, ffn_w_dw, ffn_w_down, loss_target, m_norm_g, m_attn_w_qkv, m_attn_w_o, m_conv_w_in, m_conv_w_dw, m_conv_w_out, m_pool_w_in, m_pool_w_grp, m_pool_scale, m_pool_w_out, m_ffn_w_up, m_ffn_w_dw, m_ffn_w_down, v_norm_g, v_attn_w_qkv, v_attn_w_o, v_conv_w_in, v_conv_w_dw, v_conv_w_out, v_pool_w_in, v_pool_w_grp, v_pool_scale, v_pool_w_out, v_ffn_w_up, v_ffn_w_dw, v_ffn_w_down):
    shards = dict(zip(_NAMES, (norm_g, attn_w_qkv, attn_w_o, conv_w_in, conv_w_dw, conv_w_out, pool_w_in,
                               pool_w_grp, pool_scale, pool_w_out, ffn_w_up, ffn_w_dw, ffn_w_down)))
    moms = dict(zip(_NAMES, (m_norm_g, m_attn_w_qkv, m_attn_w_o, m_conv_w_in, m_conv_w_dw, m_conv_w_out,
                             m_pool_w_in, m_pool_w_grp, m_pool_scale, m_pool_w_out, m_ffn_w_up, m_ffn_w_dw,
                             m_ffn_w_down)))
    vels = dict(zip(_NAMES, (v_norm_g, v_attn_w_qkv, v_attn_w_o, v_conv_w_in, v_conv_w_dw, v_conv_w_out,
                             v_pool_w_in, v_pool_w_grp, v_pool_scale, v_pool_w_out, v_ffn_w_up, v_ffn_w_dw,
                             v_ffn_w_down)))
    full = _all_gather_weights(shards)
    loss, grad_x, grads = _local_step(x[0], loss_target[0], full)
    loss = lax.psum(loss[0, 0], _AXES)
    gsh = _reduce_scatter_grads(grads)
    deltas, new_m, new_v = [], [], []
    for k in _NAMES:
        d, nm, nv = _adamw(shards[k], gsh[k], moms[k], vels[k], f"adamw_{k}")
        deltas.append(d)
        new_m.append(nm)
        new_v.append(nv)
    return (loss, grad_x[None], *[gsh[k] for k in _NAMES], *deltas, *new_m, *new_v)
```

```python
import functools
import math

import numpy as np
import jax
import jax.numpy as jnp
from jax import lax
from jax.experimental import pallas as pl
from jax.experimental.pallas import tpu as pltpu

F32, BF16 = jnp.float32, jnp.bfloat16

D_MODEL = 1024
SEQ = 2048
DEPTH = 4
DILATED_CFG = ((128, 1), (512, 4), (2048, 16))
N_GROUPS_A = 3
HEADS = 8
HEAD_DIM = 64
ATTN_WIDTH = HEADS * HEAD_DIM
N_HEADS_A = N_GROUPS_A * HEADS
BLOCK = 128
NEG_INF = -1e30
POOL_GROUP_DIM = 256
D_FF = 2816
RMS_EPS = 1e-6
ADAM_LR, ADAM_B1, ADAM_B2, ADAM_EPS, ADAM_WD, ADAM_STEP = 0.001, 0.9, 0.999, 1e-08, 0.01, 10

N_DEV = 8
LANES = 128
V7X_VMEM_BYTES = 64 * 2 ** 20
VMEM_LIMIT_BYTES = V7X_VMEM_BYTES - 8 * 2 ** 20
COL_TILE = 256
ROW_TILE = 256
MATMUL_TILES = (1024, 1408, 512, 256, 128)
TN_RESIDENT_K = 2048

NN = (((1,), (0,)), ((), ()))
NT = (((1,), (1,)), ((), ()))
TN = (((0,), (0,)), ((), ()))


def _dot(a, b, dims=NN):
    return lax.dot_general(a, b, dims, preferred_element_type=F32)


def _params(sem=None):
    return pltpu.CompilerParams(dimension_semantics=sem, vmem_limit_bytes=VMEM_LIMIT_BYTES)


def _pick(n, prefs):
    for p in prefs:
        if n % p == 0:
            return p
    return n


def _matmul(a, b, mode, out_dtype, name, a_parts=1, b_parts=1):
    if mode == "nn":
        m, k = a.shape[-2], a.shape[-1] * a_parts
        n = b.shape[-1] * b_parts
    elif mode == "nt":
        m, k = a.shape[-2], a.shape[-1] * a_parts
        n = b.shape[-2]
    else:
        k, m = a.shape[-2], a.shape[-1] * a_parts
        n = b.shape[-1] * b_parts
    tm = _pick(m, MATMUL_TILES)
    tn = _pick(n // b_parts if mode != "nt" else n, MATMUL_TILES)
    kk = k // a_parts if mode != "tn" else k
    tk = _pick(kk, MATMUL_TILES)
    if mode == "tn":
        tm = _pick(m // a_parts, MATMUL_TILES)
        if k <= TN_RESIDENT_K:
            tk = k
    gm, gn, gk = m // tm, n // tn, k // tk

    def a_idx(i, j, kq):
        if mode == "tn":
            r, c, per = kq, i, (m // a_parts) // tm
        else:
            r, c, per = i, kq, (k // a_parts) // tk
        return (r, c) if a_parts == 1 else (c // per, r, c % per)

    def b_idx(i, j, kq):
        if mode == "nt":
            return (j, kq)
        per = (n // b_parts) // tn
        return (kq, j) if b_parts == 1 else (j // per, kq, j % per)

    a_blk = (tk, tm) if mode == "tn" else (tm, tk)
    b_blk = (tn, tk) if mode == "nt" else (tk, tn)
    if a_parts > 1:
        a_blk = (None,) + a_blk
    if b_parts > 1:
        b_blk = (None,) + b_blk
    dims = {"nn": NN, "nt": NT, "tn": TN}[mode]

    def body_single(a_ref, b_ref, o_ref):
        o_ref[...] = _dot(a_ref[...], b_ref[...], dims).astype(o_ref.dtype)

    def body(a_ref, b_ref, o_ref, acc_ref):
        kq = pl.program_id(2)

        @pl.when(kq == 0)
        def _():
            acc_ref[...] = jnp.zeros_like(acc_ref)

        acc_ref[...] += _dot(a_ref[...], b_ref[...], dims)

        @pl.when(kq == gk - 1)
        def _():
            o_ref[...] = acc_ref[...].astype(o_ref.dtype)

    return pl.pallas_call(
        body_single if gk == 1 else body, name=name, grid=(gm, gn, gk),
        in_specs=[pl.BlockSpec(a_blk, a_idx), pl.BlockSpec(b_blk, b_idx)],
        out_specs=pl.BlockSpec((tm, tn), lambda i, j, kq: (i, j)),
        out_shape=jax.ShapeDtypeStruct((m, n), out_dtype),
        scratch_shapes=[] if gk == 1 else [pltpu.VMEM((tm, tn), F32)],
        compiler_params=_params(("parallel", "parallel", "arbitrary")),
    )(a, b)


def _rms_fwd(xin, g, res, out_dtype, name):
    s, d = xin.shape
    has_res = res is not None

    def body(*refs):
        x_ref, g_ref = refs[0], refs[1]
        o_ref = refs[-1]
        x = x_ref[...]
        r = lax.rsqrt(jnp.mean(x * x, axis=-1, keepdims=True) + RMS_EPS)
        y = x * r * g_ref[...]
        if has_res:
            y = refs[2][...] + y
        o_ref[...] = y.astype(o_ref.dtype)

    row = pl.BlockSpec((ROW_TILE, d), lambda i: (i, 0))
    vec = pl.BlockSpec((1, d), lambda i: (0, 0))
    ins = [xin, g] + ([res] if has_res else [])
    return pl.pallas_call(
        body, name=name, grid=(s // ROW_TILE,),
        in_specs=[row, vec] + ([row] if has_res else []),
        out_specs=row, out_shape=jax.ShapeDtypeStruct((s, d), out_dtype),
        compiler_params=_params(("parallel",)),
    )(*ins)


def _rms_bwd(xin, g, dy, dres, out_dtype, name):
    s, d = xin.shape
    has_res = dres is not None

    def body(*refs):
        x_ref, g_ref, dy_ref = refs[0], refs[1], refs[2]
        dx_ref, dg_ref = refs[-2], refs[-1]

        @pl.when(pl.program_id(0) == 0)
        def _():
            dg_ref[...] = jnp.zeros_like(dg_ref)

        x = x_ref[...]
        dyv = dy_ref[...].astype(F32)
        r = lax.rsqrt(jnp.mean(x * x, axis=-1, keepdims=True) + RMS_EPS)
        xhat = x * r
        u = dyv * g_ref[...]
        dx = r * (u - xhat * jnp.mean(u * xhat, axis=-1, keepdims=True))
        if has_res:
            dx = refs[3][...] + dx
        dx_ref[...] = dx.astype(dx_ref.dtype)
        dg_ref[...] += jnp.sum(dyv * xhat, axis=0, keepdims=True)

    row = pl.BlockSpec((ROW_TILE, d), lambda i: (i, 0))
    vec = pl.BlockSpec((1, d), lambda i: (0, 0))
    ins = [xin, g, dy] + ([dres] if has_res else [])
    return pl.pallas_call(
        body, name=name, grid=(s // ROW_TILE,),
        in_specs=[row, vec, row] + ([row] if has_res else []),
        out_specs=[row, vec],
        out_shape=[jax.ShapeDtypeStruct((s, d), out_dtype), jax.ShapeDtypeStruct((1, d), F32)],
        compiler_params=_params(("arbitrary",)),
    )(*ins)


def _rms(x):
    r = lax.rsqrt(jnp.mean(x * x, axis=-1, keepdims=True) + RMS_EPS)
    return r, x * r


def _rms_grad(r, xhat, dy, g):
    u = dy * g
    return r * (u - xhat * jnp.mean(u * xhat, axis=-1, keepdims=True))


def _rms_res_pre(sub, g_post, res, g_pre, name):
    s, d = sub.shape

    def body(sub_ref, gp_ref, res_ref, gn_ref, x_ref, n_ref):
        xnew = res_ref[...] + _rms(sub_ref[...])[1] * gp_ref[...]
        x_ref[...] = xnew
        n_ref[...] = (_rms(xnew)[1] * gn_ref[...]).astype(BF16)

    row = pl.BlockSpec((ROW_TILE, d), lambda i: (i, 0))
    vec = pl.BlockSpec((1, d), lambda i: (0, 0))
    return pl.pallas_call(
        body, name=name, grid=(s // ROW_TILE,),
        in_specs=[row, vec, row, vec], out_specs=[row, row],
        out_shape=[jax.ShapeDtypeStruct((s, d), F32), jax.ShapeDtypeStruct((s, d), BF16)],
        compiler_params=_params(("parallel",)),
    )(sub, g_post, res, g_pre)


def _rms_bwd_pair(xmid, g_pre, dn, dres, sub, g_post, name):
    s, d = xmid.shape

    def body(x_ref, gn_ref, dn_ref, dres_ref, sub_ref, gp_ref, dx_ref, dsub_ref, dgn_ref, dgp_ref):
        @pl.when(pl.program_id(0) == 0)
        def _():
            dgn_ref[...] = jnp.zeros_like(dgn_ref)
            dgp_ref[...] = jnp.zeros_like(dgp_ref)

        dnv = dn_ref[...].astype(F32)
        r, xhat = _rms(x_ref[...])
        dx = dres_ref[...] + _rms_grad(r, xhat, dnv, gn_ref[...])
        dx_ref[...] = dx
        dgn_ref[...] += jnp.sum(dnv * xhat, axis=0, keepdims=True)
        rs, shat = _rms(sub_ref[...])
        dsub_ref[...] = _rms_grad(rs, shat, dx, gp_ref[...]).astype(BF16)
        dgp_ref[...] += jnp.sum(dx * shat, axis=0, keepdims=True)

    row = pl.BlockSpec((ROW_TILE, d), lambda i: (i, 0))
    vec = pl.BlockSpec((1, d), lambda i: (0, 0))
    return pl.pallas_call(
        body, name=name, grid=(s // ROW_TILE,),
        in_specs=[row, vec, row, row, row, vec], out_specs=[row, row, vec, vec],
        out_shape=[jax.ShapeDtypeStruct((s, d), F32), jax.ShapeDtypeStruct((s, d), BF16),
                   jax.ShapeDtypeStruct((1, d), F32), jax.ShapeDtypeStruct((1, d), F32)],
        compiler_params=_params(("arbitrary",)),
    )(xmid, g_pre, dn, dres, sub, g_post)


def _loss_head(y, tgt, name):
    s, d = y.shape

    def body(y_ref, t_ref, l_ref, dy_ref):
        @pl.when(pl.program_id(0) == 0)
        def _():
            l_ref[...] = jnp.zeros_like(l_ref)

        e = y_ref[...] - t_ref[...]
        dy_ref[...] = e / d
        per_tok = jnp.mean(e * e, axis=-1, keepdims=True)
        l_ref[...] += 0.5 * jnp.sum(per_tok, axis=0, keepdims=True)

    row = pl.BlockSpec((ROW_TILE, d), lambda i: (i, 0))
    return pl.pallas_call(
        body, name=name, grid=(s // ROW_TILE,),
        in_specs=[row, row],
        out_specs=[pl.BlockSpec((1, 1), lambda i: (0, 0)), row],
        out_shape=[jax.ShapeDtypeStruct((1, 1), F32), jax.ShapeDtypeStruct((s, d), F32)],
        compiler_params=_params(("arbitrary",)),
    )(y, tgt)


def _shift_down(x, k):
    rows = lax.broadcasted_iota(jnp.int32, x.shape, 0)
    return jnp.where(rows >= k, pltpu.roll(x, k, axis=0), 0.0)


def _shift_up(x, k):
    t = x.shape[0]
    rows = lax.broadcasted_iota(jnp.int32, x.shape, 0)
    return jnp.where(rows < t - k, pltpu.roll(x, t - k, axis=0), 0.0)


def _conv3(h, w):
    return w[2:3] * h + w[1:2] * _shift_down(h, 1) + w[0:1] * _shift_down(h, 2)


def _conv3_bwd_x(dc, w):
    return w[2:3] * dc + w[1:2] * _shift_up(dc, 1) + w[0:1] * _shift_up(dc, 2)


def _conv3_bwd_w(dc, h, dw_ref, cols=slice(None)):
    dw_ref[0:1, cols] = jnp.sum(dc * _shift_down(h, 2), axis=0, keepdims=True)
    dw_ref[1:2, cols] = jnp.sum(dc * _shift_down(h, 1), axis=0, keepdims=True)
    dw_ref[2:3, cols] = jnp.sum(dc * h, axis=0, keepdims=True)


FFN_PAIRS = N_DEV // 2


def _lane_chunks(width):
    return [(c0, min(COL_TILE, width - c0)) for c0 in range(0, width, COL_TILE)]


def _ffn_up(n, wup, wdw, name):
    s, d = n.shape
    cw = wup.shape[-1]

    def body(n_ref, wg_ref, wu_ref, dg_ref, du_ref, h_ref, a_ref):
        x = n_ref[...]
        for c0, size in _lane_chunks(cw):
            cols = slice(c0, c0 + size)
            hg = _dot(x, wg_ref[:, cols])
            hu = _dot(x, wu_ref[:, cols])
            h_ref[0, :, cols] = hg.astype(BF16)
            h_ref[1, :, cols] = hu.astype(BF16)
            cg = _conv3(hg, dg_ref[:, cols])
            cu = _conv3(hu, du_ref[:, cols])
            a_ref[:, cols] = (cg * jax.nn.sigmoid(cg) * cu).astype(BF16)

    return pl.pallas_call(
        body, name=name, grid=(FFN_PAIRS,),
        in_specs=[pl.BlockSpec((s, d), lambda j: (0, 0)),
                  pl.BlockSpec((None, d, cw), lambda j: (j, 0, 0)),
                  pl.BlockSpec((None, d, cw), lambda j: (j + FFN_PAIRS, 0, 0)),
                  pl.BlockSpec((None, 3, cw), lambda j: (j, 0, 0)),
                  pl.BlockSpec((None, 3, cw), lambda j: (j + FFN_PAIRS, 0, 0))],
        out_specs=[pl.BlockSpec((None, 2, s, cw), lambda j: (j, 0, 0, 0)),
                   pl.BlockSpec((None, s, cw), lambda j: (j, 0, 0))],
        out_shape=[jax.ShapeDtypeStruct((FFN_PAIRS, 2, s, cw), BF16), jax.ShapeDtypeStruct((FFN_PAIRS, s, cw), BF16)],
        compiler_params=_params(("parallel",)),
    )(n, wup, wup, wdw, wdw)


def _ffn_mid_bwd(do, wdown, h, wdw, name):
    s, d = do.shape
    cw = wdown.shape[1]

    def body(do_ref, wd_ref, h_ref, wg_ref, wu_ref, dh_ref, dwg_ref, dwu_ref):
        dov = do_ref[...]
        for c0, size in _lane_chunks(cw):
            cols = slice(c0, c0 + size)
            da = _dot(dov, wd_ref[cols, :], NT)
            hg = h_ref[0, :, cols].astype(F32)
            hu = h_ref[1, :, cols].astype(F32)
            wg, wu = wg_ref[:, cols], wu_ref[:, cols]
            cg = _conv3(hg, wg)
            cu = _conv3(hu, wu)
            sg = jax.nn.sigmoid(cg)
            dcu = da * (cg * sg)
            dcg = da * cu * (sg * (1.0 + cg * (1.0 - sg)))
            dh_ref[0, :, cols] = _conv3_bwd_x(dcg, wg).astype(BF16)
            dh_ref[1, :, cols] = _conv3_bwd_x(dcu, wu).astype(BF16)
            _conv3_bwd_w(dcg, hg, dwg_ref, cols)
            _conv3_bwd_w(dcu, hu, dwu_ref, cols)

    vec = jax.ShapeDtypeStruct((FFN_PAIRS, 3, cw), F32)
    return pl.pallas_call(
        body, name=name, grid=(FFN_PAIRS,),
        in_specs=[pl.BlockSpec((s, d), lambda j: (0, 0)), pl.BlockSpec((None, cw, d), lambda j: (j, 0, 0)),
                  pl.BlockSpec((None, 2, s, cw), lambda j: (j, 0, 0, 0)),
                  pl.BlockSpec((None, 3, cw), lambda j: (j, 0, 0)),
                  pl.BlockSpec((None, 3, cw), lambda j: (j + FFN_PAIRS, 0, 0))],
        out_specs=[pl.BlockSpec((None, 2, s, cw), lambda j: (j, 0, 0, 0)),
                   pl.BlockSpec((None, 3, cw), lambda j: (j, 0, 0)), pl.BlockSpec((None, 3, cw), lambda j: (j, 0, 0))],
        out_shape=[jax.ShapeDtypeStruct((FFN_PAIRS, 2, s, cw), BF16), vec, vec],
        compiler_params=_params(("parallel",)),
    )(do, wdown, h, wdw, wdw)


def _ffn_dwup(n, dh, name):
    s, d = n.shape
    cw = dh.shape[-1]

    def body(n_ref, dh_ref, o_ref):
        o_ref[...] = _dot(n_ref[...], dh_ref[...], TN).astype(BF16)

    return pl.pallas_call(
        body, name=name, grid=(N_DEV,),
        in_specs=[pl.BlockSpec((s, d), lambda k: (0, 0)),
                  pl.BlockSpec((None, None, s, cw), lambda k: (k % FFN_PAIRS, k // FFN_PAIRS, 0, 0))],
        out_specs=pl.BlockSpec((None, d, cw), lambda k: (k, 0, 0)),
        out_shape=jax.ShapeDtypeStruct((N_DEV, d, cw), BF16),
        compiler_params=_params(("parallel",)),
    )(n, dh)


def _ffn_dn(dh, wup, name):
    s, cw = dh.shape[-2:]
    d = wup.shape[1]
    tm = _pick(s, MATMUL_TILES)

    def body(dh_ref, w_ref, o_ref, acc_ref):
        k = pl.program_id(1)

        @pl.when(k == 0)
        def _():
            acc_ref[...] = jnp.zeros_like(acc_ref)

        acc_ref[...] += _dot(dh_ref[...], w_ref[...], NT)

        @pl.when(k == N_DEV - 1)
        def _():
            o_ref[...] = acc_ref[...]

    return pl.pallas_call(
        body, name=name, grid=(s // tm, N_DEV),
        in_specs=[pl.BlockSpec((None, None, tm, cw), lambda i, k: (k % FFN_PAIRS, k // FFN_PAIRS, i, 0)),
                  pl.BlockSpec((None, d, cw), lambda i, k: (k, 0, 0))],
        out_specs=pl.BlockSpec((tm, d), lambda i, k: (i, 0)),
        out_shape=jax.ShapeDtypeStruct((s, d), F32),
        scratch_shapes=[pltpu.VMEM((tm, d), F32)],
        compiler_params=_params(("parallel", "arbitrary")),
    )(dh, wup)


def _sconv_fwd(n, win, wdw, name):
    s, d = n.shape
    tn = COL_TILE
    nj = d // tn

    def body(n_ref, wb_ref, wc_ref, wh_ref, dw_ref, z_ref, y_ref):
        x = n_ref[...]
        zb = _dot(x, wb_ref[...])
        zc = _dot(x, wc_ref[...])
        zh = _dot(x, wh_ref[...])
        z_ref[0] = zb.astype(BF16)
        z_ref[1] = zc.astype(BF16)
        z_ref[2] = zh.astype(BF16)
        y_ref[...] = (zb * _conv3(zc * zh, dw_ref[...])).astype(BF16)

    return pl.pallas_call(
        body, name=name, grid=(nj,),
        in_specs=[pl.BlockSpec((s, d), lambda j: (0, 0)),
                  pl.BlockSpec((d, tn), lambda j: (0, j)), pl.BlockSpec((d, tn), lambda j: (0, j + nj)),
                  pl.BlockSpec((d, tn), lambda j: (0, j + 2 * nj)), pl.BlockSpec((3, tn), lambda j: (0, j))],
        out_specs=[pl.BlockSpec((3, s, tn), lambda j: (0, 0, j)), pl.BlockSpec((s, tn), lambda j: (0, j))],
        out_shape=[jax.ShapeDtypeStruct((3, s, d), BF16), jax.ShapeDtypeStruct((s, d), BF16)],
        compiler_params=_params(("parallel",)),
    )(n, win, win, win, wdw)


def _sconv_mid_bwd(dm, wout, z, wdw, name):
    s, d = dm.shape
    tn = COL_TILE
    nj = d // tn

    def body(dm_ref, wo_ref, z_ref, w_ref, dz_ref, dw_ref):
        dy = _dot(dm_ref[...], wo_ref[...], NT)
        zb = z_ref[0].astype(F32)
        zc = z_ref[1].astype(F32)
        zh = z_ref[2].astype(F32)
        w = w_ref[...]
        p = zc * zh
        cp = _conv3(p, w)
        dz_ref[0] = (dy * cp).astype(BF16)
        dcp = dy * zb
        dp = _conv3_bwd_x(dcp, w)
        _conv3_bwd_w(dcp, p, dw_ref)
        dz_ref[1] = (dp * zh).astype(BF16)
        dz_ref[2] = (dp * zc).astype(BF16)

    return pl.pallas_call(
        body, name=name, grid=(nj,),
        in_specs=[pl.BlockSpec((s, d), lambda j: (0, 0)), pl.BlockSpec((tn, d), lambda j: (j, 0)),
                  pl.BlockSpec((3, s, tn), lambda j: (0, 0, j)), pl.BlockSpec((3, tn), lambda j: (0, j))],
        out_specs=[pl.BlockSpec((3, s, tn), lambda j: (0, 0, j)), pl.BlockSpec((3, tn), lambda j: (0, j))],
        out_shape=[jax.ShapeDtypeStruct((3, s, d), BF16), jax.ShapeDtypeStruct((3, d), F32)],
        compiler_params=_params(("parallel",)),
    )(dm, wout, z, wdw)


def _pool_select(g, c2, c4, c8, c16):
    return jnp.where(g == 0, c2, jnp.where(g == 1, c4, jnp.where(g == 2, c8, c16)))


def _pool_inv_count(g, shape):
    pos = lax.broadcasted_iota(jnp.int32, shape, 0).astype(F32) + 1.0
    win = (2 << g).astype(F32)
    return jnp.minimum(pos, win)


def _pool_fwd(n, win, wgrp, scale, name):
    s, d = n.shape
    tn = POOL_GROUP_DIM

    def body(n_ref, wi_ref, wg_ref, sc_ref, p_ref, y_ref):
        g = pl.program_id(0)
        u = _dot(n_ref[...], wi_ref[...])
        s2 = u + _shift_down(u, 1)
        s4 = s2 + _shift_down(s2, 2)
        s8 = s4 + _shift_down(s4, 4)
        s16 = s8 + _shift_down(s8, 8)
        tot = _pool_select(g, s2, s4, s8, s16)
        p = (tot / _pool_inv_count(g, u.shape) - u).astype(BF16)
        p_ref[...] = p
        y_ref[...] = (_dot(p, wg_ref[...]) * sc_ref[...]).astype(BF16)

    return pl.pallas_call(
        body, name=name, grid=(d // tn,),
        in_specs=[pl.BlockSpec((s, d), lambda g: (0, 0)), pl.BlockSpec((d, tn), lambda g: (0, g)),
                  pl.BlockSpec((None, tn, tn), lambda g: (g, 0, 0)), pl.BlockSpec((1, tn), lambda g: (0, g))],
        out_specs=[pl.BlockSpec((s, tn), lambda g: (0, g)), pl.BlockSpec((s, tn), lambda g: (0, g))],
        out_shape=[jax.ShapeDtypeStruct((s, d), BF16), jax.ShapeDtypeStruct((s, d), BF16)],
        compiler_params=_params(("parallel",)),
    )(n, win, wgrp, scale)


def _pool_mid_bwd(dm, wout, p, wgrp, scale, name):
    s, d = dm.shape
    tn = POOL_GROUP_DIM

    def body(dm_ref, wo_ref, p_ref, wg_ref, sc_ref, du_ref, dwg_ref, dsc_ref):
        g = pl.program_id(0)
        dy = _dot(dm_ref[...], wo_ref[...], NT)
        pv = p_ref[...]
        wg = wg_ref[...]
        ypre = _dot(pv, wg)
        dsc_ref[...] = jnp.sum(dy * ypre, axis=0, keepdims=True)
        dypre = (dy * sc_ref[...]).astype(BF16)
        dwg_ref[...] = _dot(pv, dypre, TN)
        dp = _dot(dypre, wg, NT)
        e = dp / _pool_inv_count(g, dp.shape)
        f2 = e + _shift_up(e, 1)
        f4 = f2 + _shift_up(f2, 2)
        f8 = f4 + _shift_up(f4, 4)
        f16 = f8 + _shift_up(f8, 8)
        du_ref[...] = (_pool_select(g, f2, f4, f8, f16) - dp).astype(BF16)

    return pl.pallas_call(
        body, name=name, grid=(d // tn,),
        in_specs=[pl.BlockSpec((s, d), lambda g: (0, 0)), pl.BlockSpec((tn, d), lambda g: (g, 0)),
                  pl.BlockSpec((s, tn), lambda g: (0, g)), pl.BlockSpec((None, tn, tn), lambda g: (g, 0, 0)),
                  pl.BlockSpec((1, tn), lambda g: (0, g))],
        out_specs=[pl.BlockSpec((s, tn), lambda g: (0, g)), pl.BlockSpec((None, tn, tn), lambda g: (g, 0, 0)),
                   pl.BlockSpec((1, tn), lambda g: (0, g))],
        out_shape=[jax.ShapeDtypeStruct((s, d), BF16), jax.ShapeDtypeStruct((4, tn, tn), F32),
                   jax.ShapeDtypeStruct((1, d), F32)],
        compiler_params=_params(("parallel",)),
    )(dm, wout, p, wgrp, scale)


def _alibi_slopes(g):
    all_slopes = 2.0 ** (-8.0 * np.arange(1, N_HEADS_A + 1) / N_HEADS_A)
    return jnp.asarray(all_slopes[g * HEADS:(g + 1) * HEADS], F32).reshape(HEADS, 1, 1)


BHQK = (((2,), (2,)), ((0,), (0,)))
BHQC = (((2,), (1,)), ((0,), (0,)))
BHKC = (((1,), (1,)), ((0,), (0,)))


def _attn_window(n, ln):
    if ln == BLOCK:
        return 0, BLOCK
    return pl.multiple_of(jnp.maximum(n - 1, 0) * BLOCK, BLOCK), 2 * BLOCK


def _attn_scores(q, keys, slope, n, k0):
    kw = keys.shape[1]
    qpos = n * BLOCK + lax.broadcasted_iota(jnp.int32, (BLOCK, kw), 0)
    kpos = k0 + lax.broadcasted_iota(jnp.int32, (BLOCK, kw), 1)
    dist = qpos - kpos
    valid = (dist >= 0) & (dist <= BLOCK)
    s = _dot(q, keys, BHQK) * (HEAD_DIM ** -0.5) - slope * dist.astype(F32)[None]
    return jnp.where(valid[None], s, NEG_INF)


def _attn_specs(hb, ln):
    def part(t):
        return pl.BlockSpec((None, hb, None, ln, HEAD_DIM), lambda h, r, n: (t, h, r, 0, 0))
    qblk = pl.BlockSpec((None, hb, None, BLOCK, HEAD_DIM), lambda h, r, n: (0, h, r, n, 0))
    row = pl.BlockSpec((hb, None, BLOCK, HEAD_DIM), lambda h, r, n: (h, r, n, 0))
    stat = pl.BlockSpec((hb, None, BLOCK, 1), lambda h, r, n: (h, r, n, 0))
    whole = pl.BlockSpec((hb, None, ln, HEAD_DIM), lambda h, r, n: (h, r, 0, 0))
    slope = pl.BlockSpec((hb, 1, 1), lambda h, r, n: (h, 0, 0))
    return part, qblk, row, stat, whole, slope


def _attn_fwd(qkv, slopes, dil, name):
    _, hh, _, ln, _ = qkv.shape
    nb = ln // BLOCK

    def body(sl_ref, q_ref, k_ref, v_ref, o_ref, lse_ref):
        n = pl.program_id(2)
        k0, kw = _attn_window(n, ln)
        s = _attn_scores(q_ref[...], k_ref[:, pl.ds(k0, kw), :], sl_ref[...] * float(dil), n, k0)
        m = jnp.max(s, axis=-1, keepdims=True)
        p = jnp.exp(s - m)
        den = jnp.sum(p, axis=-1, keepdims=True)
        o_ref[...] = _dot(p.astype(BF16), v_ref[:, pl.ds(k0, kw), :], BHQC) / den
        lse_ref[...] = m + jnp.log(den)

    part, qblk, row, stat, _, slope = _attn_specs(hh, ln)
    return pl.pallas_call(
        body, name=name, grid=(1, dil, nb),
        in_specs=[slope, qblk, part(1), part(2)],
        out_specs=[row, stat],
        out_shape=[jax.ShapeDtypeStruct((hh, dil, ln, HEAD_DIM), F32), jax.ShapeDtypeStruct((hh, dil, ln, 1), F32)],
        compiler_params=_params(("parallel", "parallel", "parallel")),
    )(slopes, qkv, qkv, qkv)


ATTN_BWD_HEADS = 4


def _attn_bwd(qkv, do, lse, dvec, slopes, dil, name):
    _, hh, _, ln, _ = qkv.shape
    nb = ln // BLOCK
    scale = HEAD_DIM ** -0.5

    def body(sl_ref, q_ref, k_ref, v_ref, do_ref, lse_ref, dd_ref, dq_ref, dk_ref, dv_ref):
        n = pl.program_id(2)

        @pl.when(n == 0)
        def _():
            dk_ref[...] = jnp.zeros_like(dk_ref)
            dv_ref[...] = jnp.zeros_like(dv_ref)

        k0, kw = _attn_window(n, ln)
        win = pl.ds(k0, kw)
        q, keys, dob = q_ref[...], k_ref[:, win, :], do_ref[...]
        s = _attn_scores(q, keys, sl_ref[...] * float(dil), n, k0)
        p = jnp.exp(s - lse_ref[...])
        ds = (p * (_dot(dob, v_ref[:, win, :], BHQK) - dd_ref[...])).astype(BF16)
        dq_ref[...] = scale * _dot(ds, keys, BHQC)
        dv_ref[:, win, :] += _dot(p.astype(BF16), dob, BHKC)
        dk_ref[:, win, :] += scale * _dot(ds, q, BHKC)

    part, qblk, row, stat, whole, slope = _attn_specs(ATTN_BWD_HEADS, ln)
    shp = jax.ShapeDtypeStruct((hh, dil, ln, HEAD_DIM), F32)
    return pl.pallas_call(
        body, name=name, grid=(hh // ATTN_BWD_HEADS, dil, nb),
        in_specs=[slope, qblk, part(1), part(2), row, stat, stat],
        out_specs=[row, whole, whole], out_shape=[shp, shp, shp],
        compiler_params=_params(("parallel", "parallel", "arbitrary")),
    )(slopes, qkv, qkv, qkv, do, lse, dvec)


def _attn_merge(o, lse, name):
    _, s, w = o.shape

    def body(o_ref, l_ref, m_ref, mb_ref, lse_ref):
        l = l_ref[...]
        mx = jnp.max(l, axis=0)
        e = jnp.exp(l - mx[None])
        z = jnp.sum(e, axis=0)
        lse_ref[...] = mx + jnp.log(z)
        wts = e / z[None]
        for h in range(HEADS):
            cols = slice(h * HEAD_DIM, (h + 1) * HEAD_DIM)
            acc = wts[0][:, h:h + 1] * o_ref[0, :, cols]
            for g in range(1, N_GROUPS_A):
                acc = acc + wts[g][:, h:h + 1] * o_ref[g, :, cols]
            m_ref[:, cols] = acc
            mb_ref[:, cols] = acc.astype(BF16)

    return pl.pallas_call(
        body, name=name, grid=(s // ROW_TILE,),
        in_specs=[pl.BlockSpec((N_GROUPS_A, ROW_TILE, w), lambda i: (0, i, 0)),
                  pl.BlockSpec((N_GROUPS_A, ROW_TILE, HEADS), lambda i: (0, i, 0))],
        out_specs=[pl.BlockSpec((ROW_TILE, w), lambda i: (i, 0)), pl.BlockSpec((ROW_TILE, w), lambda i: (i, 0)),
                   pl.BlockSpec((ROW_TILE, HEADS), lambda i: (i, 0))],
        out_shape=[jax.ShapeDtypeStruct((s, w), F32), jax.ShapeDtypeStruct((s, w), BF16),
                   jax.ShapeDtypeStruct((s, HEADS), F32)],
        compiler_params=_params(("parallel",)),
    )(o, lse)


def _attn_dvec(dmerged, merged, name):
    s, w = merged.shape

    def body(dm_ref, m_ref, dd_ref, dmb_ref):
        dmv = dm_ref[...]
        dmb_ref[...] = dmv.astype(BF16)
        prod = dmv * m_ref[...]
        for h in range(HEADS):
            dd_ref[:, h:h + 1] = jnp.sum(prod[:, h * HEAD_DIM:(h + 1) * HEAD_DIM], axis=-1, keepdims=True)

    row = pl.BlockSpec((ROW_TILE, w), lambda i: (i, 0))
    return pl.pallas_call(
        body, name=name, grid=(s // ROW_TILE,),
        in_specs=[row, row],
        out_specs=[pl.BlockSpec((ROW_TILE, HEADS), lambda i: (i, 0)), row],
        out_shape=[jax.ShapeDtypeStruct((s, HEADS), F32), jax.ShapeDtypeStruct((s, w), BF16)],
        compiler_params=_params(("parallel",)),
    )(dmerged, merged)


def _to_group(a, dil):
    s, w = a.shape
    c = w // HEADS
    return a.reshape(s // dil, dil, HEADS, c).transpose(2, 1, 0, 3)


def _from_group(a):
    hh, dil, ln, c = a.shape
    return a.transpose(2, 1, 0, 3).reshape(ln * dil, hh * c)


def _attention_fwd(n, wqkv, wo, tag):
    s = n.shape[0]
    qkv = _matmul(n, wqkv, "nn", BF16, f"{tag}_qkv")
    qkv6 = qkv.reshape(s, N_GROUPS_A, 3, HEADS, HEAD_DIM)
    groups, outs, lses = [], [], []
    for g, (_, dil) in enumerate(DILATED_CFG):
        qg = qkv6[:, g].reshape(s // dil, dil, 3, HEADS, HEAD_DIM).transpose(2, 3, 1, 0, 4)
        o, lse = _attn_fwd(qg, _alibi_slopes(g), dil, f"{tag}_fwd_g{g}")
        groups.append(qg)
        outs.append(_from_group(o))
        lses.append(_from_group(lse))
    merged, merged_bf, lse_all = _attn_merge(jnp.stack(outs), jnp.stack(lses), f"{tag}_merge")
    m = _matmul(merged_bf, wo, "nn", F32, f"{tag}_wo")
    return m, (groups, merged, merged_bf, lse_all)


def _attention_bwd(dm, n, wqkv, wo, saved, tag):
    groups, merged, merged_bf, lse_all = saved
    d_wo = _matmul(merged_bf, dm, "tn", BF16, f"{tag}_dwo")
    dmerged = _matmul(dm, wo, "nt", F32, f"{tag}_dmerged")
    dvec, dmerged_bf = _attn_dvec(dmerged, merged, f"{tag}_dvec")
    cols = []
    for g, (_, dil) in enumerate(DILATED_CFG):
        dq, dk, dv = _attn_bwd(groups[g], _to_group(dmerged_bf, dil), _to_group(lse_all, dil),
                               _to_group(dvec, dil), _alibi_slopes(g), dil, f"{tag}_bwd_g{g}")
        cols += [_from_group(dq), _from_group(dk), _from_group(dv)]
    dqkv = jnp.concatenate(cols, axis=1).astype(BF16)
    d_wqkv = _matmul(n, dqkv, "tn", BF16, f"{tag}_dwqkv")
    dn = _matmul(dqkv, wqkv, "nt", F32, f"{tag}_dn")
    return dn, d_wqkv, d_wo


def _local_step(x, tgt, w):
    ng = w["norm_g"]

    def gain(i, j):
        return ng[i, j][None, :]

    saved = []
    ia = ib = ic = 0
    n = _rms_fwd(x, gain(0, 0), None, BF16, "norm_first")
    for i in range(DEPTH):
        kind = i % 3
        if kind == 0:
            m, ms = _attention_fwd(n, w["attn_w_qkv"][ia], w["attn_w_o"][ia], "attn")
            mixer = ("a", ia, ms)
            ia += 1
        elif kind == 1:
            z, y = _sconv_fwd(n, w["conv_w_in"][ib], w["conv_w_dw"][ib], "sconv_fwd")
            m = _matmul(y, w["conv_w_out"][ib], "nn", F32, "sconv_out")
            mixer = ("b", ib, (z, y))
            ib += 1
        else:
            p, y = _pool_fwd(n, w["pool_w_in"][ic], w["pool_w_grp"][ic], w["pool_scale"][ic][None, :], "pool_fwd")
            m = _matmul(y, w["pool_w_out"][ic], "nn", F32, "pool_out")
            mixer = ("c", ic, (p, y))
            ic += 1
        x1, n2 = _rms_res_pre(m, gain(i, 1), x, gain(i, 2), "norm_res_pre")
        h, a = _ffn_up(n2, w["ffn_w_up"][i], w["ffn_w_dw"][i], "ffn_up")
        f = _matmul(a, w["ffn_w_down"][i].reshape(D_FF, D_MODEL), "nn", F32, "ffn_down", a_parts=FFN_PAIRS)
        saved.append((x, n, m, mixer, x1, n2, h, a, f))
        if i + 1 < DEPTH:
            x, n = _rms_res_pre(f, gain(i, 3), x1, gain(i + 1, 0), "norm_res_pre")
        else:
            x = _rms_fwd(f, gain(i, 3), x1, F32, "norm_res")

    loss, dx = _loss_head(x, tgt, "loss_head")

    g_norm = [[None] * 4 for _ in range(DEPTH)]
    grads = {k: [] for k in ("attn_w_qkv", "attn_w_o", "conv_w_in", "conv_w_dw", "conv_w_out", "pool_w_in",
                             "pool_w_grp", "pool_scale", "pool_w_out", "ffn_w_up", "ffn_w_dw", "ffn_w_down")}
    df, g_norm[DEPTH - 1][3] = _rms_bwd(saved[-1][-1], gain(DEPTH - 1, 3), dx, None, BF16, "norm_bwd_sub")
    for i in reversed(range(DEPTH)):
        xin, n, m, mixer, x1, n2, h, a, f = saved[i]
        d_wdown = _matmul(a, df, "tn", BF16, "ffn_dwdown", a_parts=FFN_PAIRS)
        grads["ffn_w_down"].append(d_wdown.reshape(N_DEV, D_FF // N_DEV, D_MODEL))
        dh, dwg, dwu = _ffn_mid_bwd(df, w["ffn_w_down"][i].reshape(FFN_PAIRS, -1, D_MODEL), h, w["ffn_w_dw"][i],
                                    "ffn_mid_bwd")
        grads["ffn_w_dw"].append(jnp.concatenate([dwg, dwu], axis=0))
        grads["ffn_w_up"].append(_ffn_dwup(n2, dh, "ffn_dwup"))
        dn2 = _ffn_dn(dh, w["ffn_w_up"][i], "ffn_dn")
        dx1, dm, g_norm[i][2], g_norm[i][1] = _rms_bwd_pair(x1, gain(i, 2), dn2, dx, m, gain(i, 1), "norm_bwd_pair")
        kind, idx, ms = mixer
        if kind == "a":
            dn, d_wqkv, d_wo = _attention_bwd(dm, n, w["attn_w_qkv"][idx], w["attn_w_o"][idx], ms, "attn")
            grads["attn_w_qkv"].append(d_wqkv)
            grads["attn_w_o"].append(d_wo)
        elif kind == "b":
            z, y = ms
            grads["conv_w_out"].append(_matmul(y, dm, "tn", BF16, "sconv_dwout"))
            dz, ddw = _sconv_mid_bwd(dm, w["conv_w_out"][idx], z, w["conv_w_dw"][idx], "sconv_mid_bwd")
            grads["conv_w_dw"].append(ddw)
            grads["conv_w_in"].append(_matmul(n, dz, "tn", BF16, "sconv_dwin", b_parts=3))
            dn = _matmul(dz, w["conv_w_in"][idx], "nt", F32, "sconv_dn", a_parts=3)
        else:
            p, y = ms
            grads["pool_w_out"].append(_matmul(y, dm, "tn", BF16, "pool_dwout"))
            du, dwgrp, dscale = _pool_mid_bwd(dm, w["pool_w_out"][idx], p, w["pool_w_grp"][idx],
                                              w["pool_scale"][idx][None, :], "pool_mid_bwd")
            grads["pool_w_grp"].append(dwgrp)
            grads["pool_scale"].append(dscale[0])
            grads["pool_w_in"].append(_matmul(n, du, "tn", BF16, "pool_dwin"))
            dn = _matmul(du, w["pool_w_in"][idx], "nt", F32, "pool_dn")
        if i > 0:
            dx, df, g_norm[i][0], g_norm[i - 1][3] = _rms_bwd_pair(xin, gain(i, 0), dn, dx1, saved[i - 1][-1],
                                                                   gain(i - 1, 3), "norm_bwd_pair")
        else:
            dx, g_norm[0][0] = _rms_bwd(xin, gain(0, 0), dn, dx1, F32, "norm_bwd_res")

    out = {k: v[::-1] if k.startswith("ffn_") else jnp.stack(v[::-1]) for k, v in grads.items()}
    out["norm_g"] = jnp.stack([jnp.concatenate(row, axis=0) for row in g_norm])
    return loss, dx, out


_AXES = ("x", "y", "c")
ROUTE_A = ("y", "x", "c")
ROUTE_B = ("x", "y", "c")
GATHER_COPIES = 7


def _dev_index(pos):
    return 4 * pos["x"] + 2 * pos["y"] + pos["c"]


def _gather(shards, routes, name):
    n = len(shards)

    def body(*refs):
        srcs, outs = refs[:n], refs[n:2 * n]
        send_sems, recv_sems = refs[2 * n:]
        pos = {a: lax.axis_index(a) for a in _AXES}

        def flipped(axes):
            return {a: 1 - pos[a] if a in axes else pos[a] for a in _AXES}

        def copy(i, k, holder, to):
            slot = outs[i].at[_dev_index(pos if holder is None else holder)]
            return pltpu.make_async_remote_copy(
                src_ref=srcs[i] if holder is None else slot, dst_ref=slot,
                send_sem=send_sems.at[GATHER_COPIES * i + k], recv_sem=recv_sems.at[GATHER_COPIES * i + k],
                device_id=tuple(to[a] for a in _AXES), device_id_type=pl.DeviceIdType.MESH)

        plans = []
        for i, (a1, a2, a3) in enumerate(routes):
            p1, p2, p12, p3 = flipped((a1,)), flipped((a2,)), flipped((a1, a2)), flipped((a3,))
            plans.append([copy(i, 0, None, p1), copy(i, 1, None, p2), copy(i, 2, p1, p2), copy(i, 3, None, p3),
                          copy(i, 4, p1, p3), copy(i, 5, p2, p3), copy(i, 6, p12, p3)])
        for c in plans:
            c[0].start()
            c[1].start()
            c[3].start()
        for c in plans:
            c[0].wait_recv()
            c[2].start()
            c[4].start()
        for c in plans:
            c[1].wait_recv()
            c[5].start()
        for c in plans:
            c[2].wait_recv()
            c[6].start()
        for c in plans:
            for k in range(3, GATHER_COPIES):
                c[k].wait_recv()
        for c in plans:
            for k in range(GATHER_COPIES):
                c[k].wait_send()

    hbm = pl.BlockSpec(memory_space=pl.ANY)
    return pl.pallas_call(
        body, name=name, in_specs=[hbm] * n, out_specs=[hbm] * n,
        out_shape=[jax.ShapeDtypeStruct((N_DEV,) + a.shape, a.dtype) for a in shards],
        scratch_shapes=[pltpu.SemaphoreType.DMA((GATHER_COPIES * n,)), pltpu.SemaphoreType.DMA((GATHER_COPIES * n,))],
    )(*shards)


def _exchange_half(arrays, axes, name):
    n = len(arrays)

    def body(*refs):
        srcs, outs = refs[:n], refs[n:2 * n]
        send_sems, recv_sems = refs[2 * n:]
        pos = {a: lax.axis_index(a) for a in _AXES}
        copies = []
        for i, axis in enumerate(axes):
            peer = tuple(1 - pos[a] if a == axis else pos[a] for a in _AXES)
            rc = pltpu.make_async_remote_copy(src_ref=srcs[i].at[:, 1 - pos[axis]], dst_ref=outs[i],
                                              send_sem=send_sems.at[i], recv_sem=recv_sems.at[i], device_id=peer,
                                              device_id_type=pl.DeviceIdType.MESH)
            rc.start()
            copies.append(rc)
        for rc in copies:
            rc.wait_send()
            rc.wait_recv()

    hbm = pl.BlockSpec(memory_space=pl.ANY)
    return pl.pallas_call(
        body, name=name, in_specs=[hbm] * n, out_specs=[hbm] * n,
        out_shape=[jax.ShapeDtypeStruct((a.shape[0],) + a.shape[2:], a.dtype) for a in arrays],
        scratch_shapes=[pltpu.SemaphoreType.DMA((n,)), pltpu.SemaphoreType.DMA((n,))],
    )(*arrays)


ADD_ROW_TILES = (1024, 704, 512, 352, 256, 128, 96, 64, 32, 16)


def _add_half(a, recv, me, out_dtype, name):
    p, q, cols = recv.shape
    tr = _pick(q, ADD_ROW_TILES)

    def body(me_ref, a_ref, b_ref, o_ref):
        o_ref[...] = (a_ref[...].astype(F32) + b_ref[...].astype(F32)).astype(o_ref.dtype)

    return pl.pallas_call(
        body, name=name,
        grid_spec=pltpu.PrefetchScalarGridSpec(
            num_scalar_prefetch=1, grid=(p, q // tr),
            in_specs=[pl.BlockSpec((None, None, tr, cols), lambda j, i, m: (j, m[0], i, 0)),
                      pl.BlockSpec((None, tr, cols), lambda j, i, m: (j, i, 0))],
            out_specs=pl.BlockSpec((None, tr, cols), lambda j, i, m: (j, i, 0))),
        out_shape=jax.ShapeDtypeStruct((p, q, cols), out_dtype),
        compiler_params=_params(("parallel", "parallel")),
    )(me, a, recv)


def _reduce_scatter(slots, routes, tags):
    coord = {a: lax.axis_index(a).astype(jnp.int32).reshape(1) for a in _AXES}
    shapes = [a.shape[1:] for a in slots]
    rows = [math.prod(s[:-1]) for s in shapes]
    arrays = [a.reshape(4, 2, n, s[-1]) for a, n, s in zip(slots, rows, shapes)]
    for stage in range(3):
        axes = [r[2 - stage] for r in routes]
        recv = _exchange_half(arrays, axes, f"scatter_{stage + 1}")
        dt = F32 if stage == 2 else BF16
        arrays = [_add_half(a, r, coord[ax], dt, f"scatter_add_{stage + 1}_{t}")
                  for a, r, ax, t in zip(arrays, recv, axes, tags)]
        if stage == 0:
            arrays = [a.reshape((1, 2, 2 * n, s[-1]) if r[1] == "x" else (2, 2, n, s[-1]))
                      for a, n, s, r in zip(arrays, rows, shapes, routes)]
        elif stage == 1:
            arrays = [a.reshape(1, 2, n, s[-1]) for a, n, s in zip(arrays, rows, shapes)]
    return [a.reshape(s) for a, s in zip(arrays, shapes)]


_WEIGHTS = {
    "norm_g": ((DEPTH, 4, D_MODEL), 2, True),
    "attn_w_qkv": ((2, D_MODEL, 4608), 2, False),
    "attn_w_o": ((2, ATTN_WIDTH, D_MODEL), 2, False),
    "conv_w_in": ((1, D_MODEL, 3 * D_MODEL), 2, False),
    "conv_w_dw": ((1, 3, D_MODEL), 2, True),
    "conv_w_out": ((1, D_MODEL, D_MODEL), 1, False),
    "pool_w_in": ((1, D_MODEL, D_MODEL), 1, False),
    "pool_w_grp": ((1, 4, POOL_GROUP_DIM, POOL_GROUP_DIM), 2, False),
    "pool_scale": ((1, D_MODEL), 1, True),
    "pool_w_out": ((1, D_MODEL, D_MODEL), 1, False),
    "ffn_w_up": ((DEPTH, D_MODEL, 2 * D_FF), 2, False),
    "ffn_w_dw": ((DEPTH, 3, 2 * D_FF), 2, True),
    "ffn_w_down": ((DEPTH, D_FF, D_MODEL), 1, False),
}
_NAMES = tuple(_WEIGHTS)
_VECTORS = tuple(k for k in _NAMES if _WEIGHTS[k][2])
_MATRICES = tuple(k for k in _NAMES if not _WEIGHTS[k][2])
_FFN = ("ffn_w_up", "ffn_w_down")
_ON_ROUTE_A = ("ffn_w_up", "attn_w_o", "conv_w_out", "pool_w_in")
PACK_ROWS = 16


def _route(name):
    return ROUTE_A if name in _ON_ROUTE_A else ROUTE_B


def _shard_shape(name):
    shape, ax, _ = _WEIGHTS[name]
    return tuple(s // N_DEV if i == ax else s for i, s in enumerate(shape))


def _full_from_slots(slots, name, layers=None):
    shape, ax, _ = _WEIGHTS[name]
    if layers is not None:
        shape = (layers,) + shape[1:]
    return jnp.moveaxis(slots, 0, ax).reshape(shape)


def _slots_from_full(full, name):
    shape, ax, _ = _WEIGHTS[name]
    split = shape[:ax] + (N_DEV, shape[ax] // N_DEV) + shape[ax + 1:]
    return jnp.moveaxis(full.reshape(split), ax, 0)


def _pack_vectors(parts, lead):
    rows = []
    for k in _VECTORS:
        r = parts[k].reshape(lead + (-1, LANES))
        pad = -r.shape[-2] % PACK_ROWS
        rows.append(jnp.pad(r, [(0, 0)] * len(lead) + [(0, pad), (0, 0)]))
    return jnp.concatenate(rows, axis=len(lead))


def _unpack_vectors(buf, lead):
    out, r0 = {}, 0
    for k in _VECTORS:
        shard = _shard_shape(k)
        rows = math.prod(shard) // LANES
        out[k] = buf[..., r0:r0 + rows, :].reshape(lead + shard)
        r0 += rows + (-rows % PACK_ROWS)
    return out


def _comm_items():
    items = []
    for k in _MATRICES:
        items += [(k, l) for l in range(DEPTH)] if k in _FFN else [(k, None)]
    return items + [("vectors", None)]


def _all_gather_weights(shards):
    items = _comm_items()
    cast = {k: shards[k].astype(BF16) for k in _MATRICES}
    send = [_pack_vectors(shards, ()) if k == "vectors" else cast[k] if l is None else cast[k][l] for k, l in items]
    outs = _gather(send, [_route(k) for k, _ in items], "gather")
    me = _dev_index({a: lax.axis_index(a) for a in _AXES})
    outs = [lax.dynamic_update_index_in_dim(o, s, me, 0) for o, s in zip(outs, send)]
    full = {k: [None] * DEPTH for k in _FFN}
    for (k, l), o in zip(items, outs):
        if k in _FFN:
            full[k][l] = o
        elif k == "vectors":
            vec = _unpack_vectors(o, (N_DEV,))
        else:
            full[k] = [_full_from_slots(o[:, j:j + 1], k, layers=1)[0] for j in range(_WEIGHTS[k][0][0])]
    full.update({k: _full_from_slots(vec[k], k) for k in _VECTORS})
    full["ffn_w_dw"] = [vec["ffn_w_dw"][:, l] for l in range(DEPTH)]
    return full


def _reduce_scatter_grads(grads):
    items = _comm_items()
    vec_slots = {k: _slots_from_full(grads[k], k) for k in _VECTORS if k != "ffn_w_dw"}
    vec_slots["ffn_w_dw"] = jnp.stack(grads["ffn_w_dw"], axis=1)
    slots = []
    for k, l in items:
        if k == "vectors":
            slots.append(_pack_vectors(vec_slots, (N_DEV,)).astype(BF16))
        elif k in _FFN:
            slots.append(grads[k][l])
        else:
            slots.append(_slots_from_full(grads[k], k).astype(BF16))
    sums = _reduce_scatter(slots, [_route(k) for k, _ in items], [k if l is None else f"{k}{l}" for k, l in items])
    out = {k: [None] * DEPTH for k in _FFN}
    for (k, l), s in zip(items, sums):
        if k in _FFN:
            out[k][l] = s
        elif k == "vectors":
            out.update(_unpack_vectors(s, ()))
        else:
            out[k] = s
    out.update({k: jnp.stack(out[k]) for k in _FFN})
    return out


def _adamw(w, g, m, v, name):
    shape = w.shape
    cols = shape[-1]
    rows = math.prod(shape[:-1])
    tr = _pick(rows, (512, 256, 128, 64, 32, 16, 8))

    def body(w_ref, g_ref, m_ref, v_ref, d_ref, nm_ref, nv_ref):
        gv = g_ref[...]
        nm = ADAM_B1 * m_ref[...] + (1.0 - ADAM_B1) * gv
        nv = ADAM_B2 * v_ref[...] + (1.0 - ADAM_B2) * jnp.square(gv)
        m_hat = nm / (1.0 - ADAM_B1 ** ADAM_STEP)
        v_hat = nv / (1.0 - ADAM_B2 ** ADAM_STEP)
        d_ref[...] = -ADAM_LR * (m_hat / (jnp.sqrt(v_hat) + ADAM_EPS) + ADAM_WD * w_ref[...])
        nm_ref[...] = nm
        nv_ref[...] = nv

    blk = pl.BlockSpec((tr, cols), lambda i: (i, 0))
    shp = jax.ShapeDtypeStruct((rows, cols), F32)
    outs = pl.pallas_call(
        body, name=name, grid=(rows // tr,), in_specs=[blk] * 4, out_specs=[blk] * 3, out_shape=[shp] * 3,
        compiler_params=_params(("parallel",)),
    )(*[t.reshape(rows, cols) for t in (w, g, m, v)])
    return [o.reshape(shape) for o in outs]


def kernel(x, norm_g, attn_w_qkv, attn_w_o, conv_w_in, conv_w_dw, conv_w_out, pool_w_in, pool_w_grp, pool_scale, pool_w_out, ffn_w_up, ffn_w_dw, ffn_w_down, loss_target, m_norm_g, m_attn_w_qkv, m_attn_w_o, m_conv_w_in, m_conv_w_dw, m_conv_w_out, m_pool_w_in, m_pool_w_grp, m_pool_scale, m_pool_w_out, m_ffn_w_up, m_ffn_w_dw, m_ffn_w_down, v_norm_g, v_attn_w_qkv, v_attn_w_o, v_conv_w_in, v_conv_w_dw, v_conv_w_out, v_pool_w_in, v_pool_w_grp, v_pool_scale, v_pool_w_out, v_ffn_w_up, v_ffn_w_dw, v_ffn_w_down):
    shards = dict(zip(_NAMES, (norm_g, attn_w_qkv, attn_w_o, conv_w_in, conv_w_dw, conv_w_out, pool_w_in,
                               pool_w_grp, pool_scale, pool_w_out, ffn_w_up, ffn_w_dw, ffn_w_down)))
    moms = dict(zip(_NAMES, (m_norm_g, m_attn_w_qkv, m_attn_w_o, m_conv_w_in, m_conv_w_dw, m_conv_w_out,
                             m_pool_w_in, m_pool_w_grp, m_pool_scale, m_pool_w_out, m_ffn_w_up, m_ffn_w_dw,
                             m_ffn_w_down)))
    vels = dict(zip(_NAMES, (v_norm_g, v_attn_w_qkv, v_attn_w_o, v_conv_w_in, v_conv_w_dw, v_conv_w_out,
                             v_pool_w_in, v_pool_w_grp, v_pool_scale, v_pool_w_out, v_ffn_w_up, v_ffn_w_dw,
                             v_ffn_w_down)))
    full = _all_gather_weights(shards)
    loss, grad_x, grads = _local_step(x[0], loss_target[0], full)
    loss = lax.psum(loss[0, 0], _AXES)
    gsh = _reduce_scatter_grads(grads)
    deltas, new_m, new_v = [], [], []
    for k in _NAMES:
        d, nm, nv = _adamw(shards[k], gsh[k], moms[k], vels[k], f"adamw_{k}")
        deltas.append(d)
        new_m.append(nm)
        new_v.append(nv)
    return (loss, grad_x[None], *[gsh[k] for k in _NAMES], *deltas, *new_m, *new_v)
```

```python
import functools
import math

import numpy as np
import jax
import jax.numpy as jnp
from jax import lax
from jax.experimental import pallas as pl
from jax.experimental.pallas import tpu as pltpu

F32, BF16 = jnp.float32, jnp.bfloat16

D_MODEL = 1024
SEQ = 2048
DEPTH = 4
DILATED_CFG = ((128, 1), (512, 4), (2048, 16))
N_GROUPS_A = 3
HEADS = 8
HEAD_DIM = 64
ATTN_WIDTH = HEADS * HEAD_DIM
N_HEADS_A = N_GROUPS_A * HEADS
BLOCK = 128
NEG_INF = -1e30
POOL_GROUP_DIM = 256
D_FF = 2816
RMS_EPS = 1e-6
ADAM_LR, ADAM_B1, ADAM_B2, ADAM_EPS, ADAM_WD, ADAM_STEP = 0.001, 0.9, 0.999, 1e-08, 0.01, 10

N_DEV = 8
LANES = 128
V7X_VMEM_BYTES = 64 * 2 ** 20
VMEM_LIMIT_BYTES = V7X_VMEM_BYTES - 8 * 2 ** 20
COL_TILE = 256
ROW_TILE = 256
MATMUL_TILES = (1024, 1408, 512, 256, 128)
TN_RESIDENT_K = 2048

NN = (((1,), (0,)), ((), ()))
NT = (((1,), (1,)), ((), ()))
TN = (((0,), (0,)), ((), ()))


def _dot(a, b, dims=NN):
    return lax.dot_general(a, b, dims, preferred_element_type=F32)


def _params(sem=None):
    return pltpu.CompilerParams(dimension_semantics=sem, vmem_limit_bytes=VMEM_LIMIT_BYTES)


def _pick(n, prefs):
    for p in prefs:
        if n % p == 0:
            return p
    return n


def _matmul(a, b, mode, out_dtype, name, a_parts=1, b_parts=1):
    if mode == "nn":
        m, k = a.shape[-2], a.shape[-1] * a_parts
        n = b.shape[-1] * b_parts
    elif mode == "nt":
        m, k = a.shape[-2], a.shape[-1] * a_parts
        n = b.shape[-2]
    else:
        k, m = a.shape[-2], a.shape[-1] * a_parts
        n = b.shape[-1] * b_parts
    tm = _pick(m, MATMUL_TILES)
    tn = _pick(n // b_parts if mode != "nt" else n, MATMUL_TILES)
    kk = k // a_parts if mode != "tn" else k
    tk = _pick(kk, MATMUL_TILES)
    if mode == "tn":
        tm = _pick(m // a_parts, MATMUL_TILES)
        if k <= TN_RESIDENT_K:
            tk = k
    gm, gn, gk = m // tm, n // tn, k // tk

    def a_idx(i, j, kq):
        if mode == "tn":
            r, c, per = kq, i, (m // a_parts) // tm
        else:
            r, c, per = i, kq, (k // a_parts) // tk
        return (r, c) if a_parts == 1 else (c // per, r, c % per)

    def b_idx(i, j, kq):
        if mode == "nt":
            return (j, kq)
        per = (n // b_parts) // tn
        return (kq, j) if b_parts == 1 else (j // per, kq, j % per)

    a_blk = (tk, tm) if mode == "tn" else (tm, tk)
    b_blk = (tn, tk) if mode == "nt" else (tk, tn)
    if a_parts > 1:
        a_blk = (None,) + a_blk
    if b_parts > 1:
        b_blk = (None,) + b_blk
    dims = {"nn": NN, "nt": NT, "tn": TN}[mode]

    def body_single(a_ref, b_ref, o_ref):
        o_ref[...] = _dot(a_ref[...], b_ref[...], dims).astype(o_ref.dtype)

    def body(a_ref, b_ref, o_ref, acc_ref):
        kq = pl.program_id(2)

        @pl.when(kq == 0)
        def _():
            acc_ref[...] = jnp.zeros_like(acc_ref)

        acc_ref[...] += _dot(a_ref[...], b_ref[...], dims)

        @pl.when(kq == gk - 1)
        def _():
            o_ref[...] = acc_ref[...].astype(o_ref.dtype)

    return pl.pallas_call(
        body_single if gk == 1 else body, name=name, grid=(gm, gn, gk),
        in_specs=[pl.BlockSpec(a_blk, a_idx), pl.BlockSpec(b_blk, b_idx)],
        out_specs=pl.BlockSpec((tm, tn), lambda i, j, kq: (i, j)),
        out_shape=jax.ShapeDtypeStruct((m, n), out_dtype),
        scratch_shapes=[] if gk == 1 else [pltpu.VMEM((tm, tn), F32)],
        compiler_params=_params(("parallel", "parallel", "arbitrary")),
    )(a, b)


def _rms_fwd(xin, g, res, out_dtype, name):
    s, d = xin.shape
    has_res = res is not None

    def body(*refs):
        x_ref, g_ref = refs[0], refs[1]
        o_ref = refs[-1]
        x = x_ref[...]
        r = lax.rsqrt(jnp.mean(x * x, axis=-1, keepdims=True) + RMS_EPS)
        y = x * r * g_ref[...]
        if has_res:
            y = refs[2][...] + y
        o_ref[...] = y.astype(o_ref.dtype)

    row = pl.BlockSpec((ROW_TILE, d), lambda i: (i, 0))
    vec = pl.BlockSpec((1, d), lambda i: (0, 0))
    ins = [xin, g] + ([res] if has_res else [])
    return pl.pallas_call(
        body, name=name, grid=(s // ROW_TILE,),
        in_specs=[row, vec] + ([row] if has_res else []),
        out_specs=row, out_shape=jax.ShapeDtypeStruct((s, d), out_dtype),
        compiler_params=_params(("parallel",)),
    )(*ins)


def _rms_bwd(xin, g, dy, dres, out_dtype, name):
    s, d = xin.shape
    has_res = dres is not None

    def body(*refs):
        x_ref, g_ref, dy_ref = refs[0], refs[1], refs[2]
        dx_ref, dg_ref = refs[-2], refs[-1]

        @pl.when(pl.program_id(0) == 0)
        def _():
            dg_ref[...] = jnp.zeros_like(dg_ref)

        x = x_ref[...]
        dyv = dy_ref[...].astype(F32)
        r = lax.rsqrt(jnp.mean(x * x, axis=-1, keepdims=True) + RMS_EPS)
        xhat = x * r
        u = dyv * g_ref[...]
        dx = r * (u - xhat * jnp.mean(u * xhat, axis=-1, keepdims=True))
        if has_res:
            dx = refs[3][...] + dx
        dx_ref[...] = dx.astype(dx_ref.dtype)
        dg_ref[...] += jnp.sum(dyv * xhat, axis=0, keepdims=True)

    row = pl.BlockSpec((ROW_TILE, d), lambda i: (i, 0))
    vec = pl.BlockSpec((1, d), lambda i: (0, 0))
    ins = [xin, g, dy] + ([dres] if has_res else [])
    return pl.pallas_call(
        body, name=name, grid=(s // ROW_TILE,),
        in_specs=[row, vec, row] + ([row] if has_res else []),
        out_specs=[row, vec],
        out_shape=[jax.ShapeDtypeStruct((s, d), out_dtype), jax.ShapeDtypeStruct((1, d), F32)],
        compiler_params=_params(("arbitrary",)),
    )(*ins)


def _rms(x):
    r = lax.rsqrt(jnp.mean(x * x, axis=-1, keepdims=True) + RMS_EPS)
    return r, x * r


def _rms_grad(r, xhat, dy, g):
    u = dy * g
    return r * (u - xhat * jnp.mean(u * xhat, axis=-1, keepdims=True))


def _rms_res_pre(sub, g_post, res, g_pre, name):
    s, d = sub.shape

    def body(sub_ref, gp_ref, res_ref, gn_ref, x_ref, n_ref):
        xnew = res_ref[...] + _rms(sub_ref[...])[1] * gp_ref[...]
        x_ref[...] = xnew
        n_ref[...] = (_rms(xnew)[1] * gn_ref[...]).astype(BF16)

    row = pl.BlockSpec((ROW_TILE, d), lambda i: (i, 0))
    vec = pl.BlockSpec((1, d), lambda i: (0, 0))
    return pl.pallas_call(
        body, name=name, grid=(s // ROW_TILE,),
        in_specs=[row, vec, row, vec], out_specs=[row, row],
        out_shape=[jax.ShapeDtypeStruct((s, d), F32), jax.ShapeDtypeStruct((s, d), BF16)],
        compiler_params=_params(("parallel",)),
    )(sub, g_post, res, g_pre)


def _rms_bwd_pair(xmid, g_pre, dn, dres, sub, g_post, name):
    s, d = xmid.shape

    def body(x_ref, gn_ref, dn_ref, dres_ref, sub_ref, gp_ref, dx_ref, dsub_ref, dgn_ref, dgp_ref):
        @pl.when(pl.program_id(0) == 0)
        def _():
            dgn_ref[...] = jnp.zeros_like(dgn_ref)
            dgp_ref[...] = jnp.zeros_like(dgp_ref)

        dnv = dn_ref[...].astype(F32)
        r, xhat = _rms(x_ref[...])
        dx = dres_ref[...] + _rms_grad(r, xhat, dnv, gn_ref[...])
        dx_ref[...] = dx
        dgn_ref[...] += jnp.sum(dnv * xhat, axis=0, keepdims=True)
        rs, shat = _rms(sub_ref[...])
        dsub_ref[...] = _rms_grad(rs, shat, dx, gp_ref[...]).astype(BF16)
        dgp_ref[...] += jnp.sum(dx * shat, axis=0, keepdims=True)

    row = pl.BlockSpec((ROW_TILE, d), lambda i: (i, 0))
    vec = pl.BlockSpec((1, d), lambda i: (0, 0))
    return pl.pallas_call(
        body, name=name, grid=(s // ROW_TILE,),
        in_specs=[row, vec, row, row, row, vec], out_specs=[row, row, vec, vec],
        out_shape=[jax.ShapeDtypeStruct((s, d), F32), jax.ShapeDtypeStruct((s, d), BF16),
                   jax.ShapeDtypeStruct((1, d), F32), jax.ShapeDtypeStruct((1, d), F32)],
        compiler_params=_params(("arbitrary",)),
    )(xmid, g_pre, dn, dres, sub, g_post)


def _loss_head(y, tgt, name):
    s, d = y.shape

    def body(y_ref, t_ref, l_ref, dy_ref):
        @pl.when(pl.program_id(0) == 0)
        def _():
            l_ref[...] = jnp.zeros_like(l_ref)

        e = y_ref[...] - t_ref[...]
        dy_ref[...] = e / d
        per_tok = jnp.mean(e * e, axis=-1, keepdims=True)
        l_ref[...] += 0.5 * jnp.sum(per_tok, axis=0, keepdims=True)

    row = pl.BlockSpec((ROW_TILE, d), lambda i: (i, 0))
    return pl.pallas_call(
        body, name=name, grid=(s // ROW_TILE,),
        in_specs=[row, row],
        out_specs=[pl.BlockSpec((1, 1), lambda i: (0, 0)), row],
        out_shape=[jax.ShapeDtypeStruct((1, 1), F32), jax.ShapeDtypeStruct((s, d), F32)],
        compiler_params=_params(("arbitrary",)),
    )(y, tgt)


def _shift_down(x, k):
    rows = lax.broadcasted_iota(jnp.int32, x.shape, 0)
    return jnp.where(rows >= k, pltpu.roll(x, k, axis=0), 0.0)


def _shift_up(x, k):
    t = x.shape[0]
    rows = lax.broadcasted_iota(jnp.int32, x.shape, 0)
    return jnp.where(rows < t - k, pltpu.roll(x, t - k, axis=0), 0.0)


def _conv3(h, w):
    return w[2:3] * h + w[1:2] * _shift_down(h, 1) + w[0:1] * _shift_down(h, 2)


def _conv3_bwd_x(dc, w):
    return w[2:3] * dc + w[1:2] * _shift_up(dc, 1) + w[0:1] * _shift_up(dc, 2)


def _conv3_bwd_w(dc, h, dw_ref, cols=slice(None)):
    dw_ref[0:1, cols] = jnp.sum(dc * _shift_down(h, 2), axis=0, keepdims=True)
    dw_ref[1:2, cols] = jnp.sum(dc * _shift_down(h, 1), axis=0, keepdims=True)
    dw_ref[2:3, cols] = jnp.sum(dc * h, axis=0, keepdims=True)


FFN_PAIRS = N_DEV // 2


def _lane_chunks(width):
    return [(c0, min(COL_TILE, width - c0)) for c0 in range(0, width, COL_TILE)]


def _ffn_up(n, wup, wdw, name):
    s, d = n.shape
    cw = wup.shape[-1]

    def body(n_ref, wg_ref, wu_ref, dg_ref, du_ref, h_ref, a_ref):
        x = n_ref[...]
        for c0, size in _lane_chunks(cw):
            cols = slice(c0, c0 + size)
            hg = _dot(x, wg_ref[:, cols])
            hu = _dot(x, wu_ref[:, cols])
            h_ref[0, :, cols] = hg.astype(BF16)
            h_ref[1, :, cols] = hu.astype(BF16)
            cg = _conv3(hg, dg_ref[:, cols])
            cu = _conv3(hu, du_ref[:, cols])
            a_ref[:, cols] = (cg * jax.nn.sigmoid(cg) * cu).astype(BF16)

    return pl.pallas_call(
        body, name=name, grid=(FFN_PAIRS,),
        in_specs=[pl.BlockSpec((s, d), lambda j: (0, 0)),
                  pl.BlockSpec((None, d, cw), lambda j: (j, 0, 0)),
                  pl.BlockSpec((None, d, cw), lambda j: (j + FFN_PAIRS, 0, 0)),
                  pl.BlockSpec((None, 3, cw), lambda j: (j, 0, 0)),
                  pl.BlockSpec((None, 3, cw), lambda j: (j + FFN_PAIRS, 0, 0))],
        out_specs=[pl.BlockSpec((None, 2, s, cw), lambda j: (j, 0, 0, 0)),
                   pl.BlockSpec((None, s, cw), lambda j: (j, 0, 0))],
        out_shape=[jax.ShapeDtypeStruct((FFN_PAIRS, 2, s, cw), BF16), jax.ShapeDtypeStruct((FFN_PAIRS, s, cw), BF16)],
        compiler_params=_params(("parallel",)),
    )(n, wup, wup, wdw, wdw)


def _ffn_mid_bwd(do, wdown, h, wdw, name):
    s, d = do.shape
    cw = wdown.shape[1]

    def body(do_ref, wd_ref, h_ref, wg_ref, wu_ref, dh_ref, dwg_ref, dwu_ref):
        dov = do_ref[...]
        for c0, size in _lane_chunks(cw):
            cols = slice(c0, c0 + size)
            da = _dot(dov, wd_ref[cols, :], NT)
            hg = h_ref[0, :, cols].astype(F32)
            hu = h_ref[1, :, cols].astype(F32)
            wg, wu = wg_ref[:, cols], wu_ref[:, cols]
            cg = _conv3(hg, wg)
            cu = _conv3(hu, wu)
            sg = jax.nn.sigmoid(cg)
            dcu = da * (cg * sg)
            dcg = da * cu * (sg * (1.0 + cg * (1.0 - sg)))
            dh_ref[0, :, cols] = _conv3_bwd_x(dcg, wg).astype(BF16)
            dh_ref[1, :, cols] = _conv3_bwd_x(dcu, wu).astype(BF16)
            _conv3_bwd_w(dcg, hg, dwg_ref, cols)
            _conv3_bwd_w(dcu, hu, dwu_ref, cols)

    vec = jax.ShapeDtypeStruct((FFN_PAIRS, 3, cw), F32)
    return pl.pallas_call(
        body, name=name, grid=(FFN_PAIRS,),
        in_specs=[pl.BlockSpec((s, d), lambda j: (0, 0)), pl.BlockSpec((None, cw, d), lambda j: (j, 0, 0)),
                  pl.BlockSpec((None, 2, s, cw), lambda j: (j, 0, 0, 0)),
                  pl.BlockSpec((None, 3, cw), lambda j: (j, 0, 0)),
                  pl.BlockSpec((None, 3, cw), lambda j: (j + FFN_PAIRS, 0, 0))],
        out_specs=[pl.BlockSpec((None, 2, s, cw), lambda j: (j, 0, 0, 0)),
                   pl.BlockSpec((None, 3, cw), lambda j: (j, 0, 0)), pl.BlockSpec((None, 3, cw), lambda j: (j, 0, 0))],
        out_shape=[jax.ShapeDtypeStruct((FFN_PAIRS, 2, s, cw), BF16), vec, vec],
        compiler_params=_params(("parallel",)),
    )(do, wdown, h, wdw, wdw)


def _ffn_dwup(n, dh, name):
    s, d = n.shape
    cw = dh.shape[-1]

    def body(n_ref, dh_ref, o_ref):
        o_ref[...] = _dot(n_ref[...], dh_ref[...], TN).astype(BF16)

    return pl.pallas_call(
        body, name=name, grid=(N_DEV,),
        in_specs=[pl.BlockSpec((s, d), lambda k: (0, 0)),
                  pl.BlockSpec((None, None, s, cw), lambda k: (k % FFN_PAIRS, k // FFN_PAIRS, 0, 0))],
        out_specs=pl.BlockSpec((None, d, cw), lambda k: (k, 0, 0)),
        out_shape=jax.ShapeDtypeStruct((N_DEV, d, cw), BF16),
        compiler_params=_params(("parallel",)),
    )(n, dh)


def _ffn_dn(dh, wup, name):
    s, cw = dh.shape[-2:]
    d = wup.shape[1]
    tm = _pick(s, MATMUL_TILES)

    def body(dh_ref, w_ref, o_ref, acc_ref):
        k = pl.program_id(1)

        @pl.when(k == 0)
        def _():
            acc_ref[...] = jnp.zeros_like(acc_ref)

        acc_ref[...] += _dot(dh_ref[...], w_ref[...], NT)

        @pl.when(k == N_DEV - 1)
        def _():
            o_ref[...] = acc_ref[...]

    return pl.pallas_call(
        body, name=name, grid=(s // tm, N_DEV),
        in_specs=[pl.BlockSpec((None, None, tm, cw), lambda i, k: (k % FFN_PAIRS, k // FFN_PAIRS, i, 0)),
                  pl.BlockSpec((None, d, cw), lambda i, k: (k, 0, 0))],
        out_specs=pl.BlockSpec((tm, d), lambda i, k: (i, 0)),
        out_shape=jax.ShapeDtypeStruct((s, d), F32),
        scratch_shapes=[pltpu.VMEM((tm, d), F32)],
        compiler_params=_params(("parallel", "arbitrary")),
    )(dh, wup)


def _sconv_fwd(n, win, wdw, name):
    s, d = n.shape
    tn = COL_TILE
    nj = d // tn

    def body(n_ref, wb_ref, wc_ref, wh_ref, dw_ref, z_ref, y_ref):
        x = n_ref[...]
        zb = _dot(x, wb_ref[...])
        zc = _dot(x, wc_ref[...])
        zh = _dot(x, wh_ref[...])
        z_ref[0] = zb.astype(BF16)
        z_ref[1] = zc.astype(BF16)
        z_ref[2] = zh.astype(BF16)
        y_ref[...] = (zb * _conv3(zc * zh, dw_ref[...])).astype(BF16)

    return pl.pallas_call(
        body, name=name, grid=(nj,),
        in_specs=[pl.BlockSpec((s, d), lambda j: (0, 0)),
                  pl.BlockSpec((d, tn), lambda j: (0, j)), pl.BlockSpec((d, tn), lambda j: (0, j + nj)),
                  pl.BlockSpec((d, tn), lambda j: (0, j + 2 * nj)), pl.BlockSpec((3, tn), lambda j: (0, j))],
        out_specs=[pl.BlockSpec((3, s, tn), lambda j: (0, 0, j)), pl.BlockSpec((s, tn), lambda j: (0, j))],
        out_shape=[jax.ShapeDtypeStruct((3, s, d), BF16), jax.ShapeDtypeStruct((s, d), BF16)],
        compiler_params=_params(("parallel",)),
    )(n, win, win, win, wdw)


def _sconv_mid_bwd(dm, wout, z, wdw, name):
    s, d = dm.shape
    tn = COL_TILE
    nj = d // tn

    def body(dm_ref, wo_ref, z_ref, w_ref, dz_ref, dw_ref):
        dy = _dot(dm_ref[...], wo_ref[...], NT)
        zb = z_ref[0].astype(F32)
        zc = z_ref[1].astype(F32)
        zh = z_ref[2].astype(F32)
        w = w_ref[...]
        p = zc * zh
        cp = _conv3(p, w)
        dz_ref[0] = (dy * cp).astype(BF16)
        dcp = dy * zb
        dp = _conv3_bwd_x(dcp, w)
        _conv3_bwd_w(dcp, p, dw_ref)
        dz_ref[1] = (dp * zh).astype(BF16)
        dz_ref[2] = (dp * zc).astype(BF16)

    return pl.pallas_call(
        body, name=name, grid=(nj,),
        in_specs=[pl.BlockSpec((s, d), lambda j: (0, 0)), pl.BlockSpec((tn, d), lambda j: (j, 0)),
                  pl.BlockSpec((3, s, tn), lambda j: (0, 0, j)), pl.BlockSpec((3, tn), lambda j: (0, j))],
        out_specs=[pl.BlockSpec((3, s, tn), lambda j: (0, 0, j)), pl.BlockSpec((3, tn), lambda j: (0, j))],
        out_shape=[jax.ShapeDtypeStruct((3, s, d), BF16), jax.ShapeDtypeStruct((3, d), F32)],
        compiler_params=_params(("parallel",)),
    )(dm, wout, z, wdw)


def _pool_select(g, c2, c4, c8, c16):
    return jnp.where(g == 0, c2, jnp.where(g == 1, c4, jnp.where(g == 2, c8, c16)))


def _pool_inv_count(g, shape):
    pos = lax.broadcasted_iota(jnp.int32, shape, 0).astype(F32) + 1.0
    win = (2 << g).astype(F32)
    return jnp.minimum(pos, win)


def _pool_fwd(n, win, wgrp, scale, name):
    s, d = n.shape
    tn = POOL_GROUP_DIM

    def body(n_ref, wi_ref, wg_ref, sc_ref, p_ref, y_ref):
        g = pl.program_id(0)
        u = _dot(n_ref[...], wi_ref[...])
        s2 = u + _shift_down(u, 1)
        s4 = s2 + _shift_down(s2, 2)
        s8 = s4 + _shift_down(s4, 4)
        s16 = s8 + _shift_down(s8, 8)
        tot = _pool_select(g, s2, s4, s8, s16)
        p = (tot / _pool_inv_count(g, u.shape) - u).astype(BF16)
        p_ref[...] = p
        y_ref[...] = (_dot(p, wg_ref[...]) * sc_ref[...]).astype(BF16)

    return pl.pallas_call(
        body, name=name, grid=(d // tn,),
        in_specs=[pl.BlockSpec((s, d), lambda g: (0, 0)), pl.BlockSpec((d, tn), lambda g: (0, g)),
                  pl.BlockSpec((None, tn, tn), lambda g: (g, 0, 0)), pl.BlockSpec((1, tn), lambda g: (0, g))],
        out_specs=[pl.BlockSpec((s, tn), lambda g: (0, g)), pl.BlockSpec((s, tn), lambda g: (0, g))],
        out_shape=[jax.ShapeDtypeStruct((s, d), BF16), jax.ShapeDtypeStruct((s, d), BF16)],
        compiler_params=_params(("parallel",)),
    )(n, win, wgrp, scale)


def _pool_mid_bwd(dm, wout, p, wgrp, scale, name):
    s, d = dm.shape
    tn = POOL_GROUP_DIM

    def body(dm_ref, wo_ref, p_ref, wg_ref, sc_ref, du_ref, dwg_ref, dsc_ref):
        g = pl.program_id(0)
        dy = _dot(dm_ref[...], wo_ref[...], NT)
        pv = p_ref[...]
        wg = wg_ref[...]
        ypre = _dot(pv, wg)
        dsc_ref[...] = jnp.sum(dy * ypre, axis=0, keepdims=True)
        dypre = (dy * sc_ref[...]).astype(BF16)
        dwg_ref[...] = _dot(pv, dypre, TN)
        dp = _dot(dypre, wg, NT)
        e = dp / _pool_inv_count(g, dp.shape)
        f2 = e + _shift_up(e, 1)
        f4 = f2 + _shift_up(f2, 2)
        f8 = f4 + _shift_up(f4, 4)
        f16 = f8 + _shift_up(f8, 8)
        du_ref[...] = (_pool_select(g, f2, f4, f8, f16) - dp).astype(BF16)

    return pl.pallas_call(
        body, name=name, grid=(d // tn,),
        in_specs=[pl.BlockSpec((s, d), lambda g: (0, 0)), pl.BlockSpec((tn, d), lambda g: (g, 0)),
                  pl.BlockSpec((s, tn), lambda g: (0, g)), pl.BlockSpec((None, tn, tn), lambda g: (g, 0, 0)),
                  pl.BlockSpec((1, tn), lambda g: (0, g))],
        out_specs=[pl.BlockSpec((s, tn), lambda g: (0, g)), pl.BlockSpec((None, tn, tn), lambda g: (g, 0, 0)),
                   pl.BlockSpec((1, tn), lambda g: (0, g))],
        out_shape=[jax.ShapeDtypeStruct((s, d), BF16), jax.ShapeDtypeStruct((4, tn, tn), F32),
                   jax.ShapeDtypeStruct((1, d), F32)],
        compiler_params=_params(("parallel",)),
    )(dm, wout, p, wgrp, scale)


PANEL = LANES
ATTN_EXT = ATTN_WIDTH + PANEL
DVEC_LANE = HEADS


def _alibi_slopes(g, dil):
    all_slopes = 2.0 ** (-8.0 * np.arange(1, N_HEADS_A + 1) / N_HEADS_A)
    return [float(np.float32(sl) * np.float32(dil)) for sl in all_slopes[g * HEADS:(g + 1) * HEADS]]


def _residue_order(a, dil, name):
    s, w = a.shape
    per = ROW_TILE // dil
    panels = w // PANEL

    def body(a_ref, o_ref, *tiles):
        for c in range(panels):
            cols = slice(c * PANEL, (c + 1) * PANEL)
            tiles[c][...] = a_ref[:, cols].astype(F32)
            for r in range(dil):
                o_ref[r, :, cols] = tiles[c][pl.ds(r, per, stride=dil), :].astype(o_ref.dtype)

    out = pl.pallas_call(
        body, name=name, grid=(s // ROW_TILE,),
        in_specs=[pl.BlockSpec((ROW_TILE, w), lambda i: (i, 0))],
        out_specs=pl.BlockSpec((dil, per, w), lambda i: (0, i, 0)),
        out_shape=jax.ShapeDtypeStruct((dil, s // dil, w), a.dtype),
        scratch_shapes=[pltpu.VMEM((ROW_TILE, PANEL), F32)] * panels,
        compiler_params=_params(("parallel",)),
    )(a)
    return out.reshape(s, w)


def _token_order(a, dil, acc, name):
    s, w = a.shape
    per = ROW_TILE // dil
    panels = w // PANEL
    has_acc = acc is not None

    def body(*refs):
        a_ref = refs[0]
        o_ref = refs[2] if has_acc else refs[1]
        tiles = refs[3:] if has_acc else refs[2:]
        for c in range(panels):
            cols = slice(c * PANEL, (c + 1) * PANEL)
            for r in range(dil):
                tiles[c][pl.ds(r, per, stride=dil), :] = a_ref[r, :, cols]
            v = tiles[c][...]
            if has_acc:
                v = v + refs[1][:, cols]
            o_ref[:, cols] = v

    row = pl.BlockSpec((ROW_TILE, w), lambda i: (i, 0))
    return pl.pallas_call(
        body, name=name, grid=(s // ROW_TILE,),
        in_specs=[pl.BlockSpec((dil, per, w), lambda i: (0, i, 0))] + ([row] if has_acc else []),
        out_specs=row, out_shape=jax.ShapeDtypeStruct((s, w), F32),
        scratch_shapes=[pltpu.VMEM((ROW_TILE, PANEL), F32)] * panels,
        compiler_params=_params(("parallel",)),
    )(*([a.reshape(dil, s // dil, w)] + ([acc] if has_acc else [])))


def _qkv_proj(n, wqkv, g, name):
    s, d = n.shape
    tm = _pick(s, MATMUL_TILES)

    def body(a_ref, b_ref, o_ref):
        o_ref[...] = _dot(a_ref[...], b_ref[...]).astype(BF16)

    return pl.pallas_call(
        body, name=name, grid=(s // tm, 3),
        in_specs=[pl.BlockSpec((tm, d), lambda i, t: (i, 0)),
                  pl.BlockSpec((d, ATTN_WIDTH), lambda i, t: (0, 3 * g + t))],
        out_specs=pl.BlockSpec((None, tm, ATTN_WIDTH), lambda i, t: (t, i, 0)),
        out_shape=jax.ShapeDtypeStruct((3, s, ATTN_WIDTH), BF16),
        compiler_params=_params(("parallel", "parallel")),
    )(n, wqkv)


def _attn_window(n, ln):
    if ln == BLOCK:
        return 0, BLOCK
    return pl.multiple_of(jnp.maximum(n - 1, 0) * BLOCK, BLOCK), 2 * BLOCK


def _attn_mask(n, k0, kw):
    qpos = n * BLOCK + lax.broadcasted_iota(jnp.int32, (BLOCK, kw), 0)
    kpos = k0 + lax.broadcasted_iota(jnp.int32, (BLOCK, kw), 1)
    dist = qpos - kpos
    return dist.astype(F32), (dist >= 0) & (dist <= BLOCK)


def _attn_scores(q, keys, slope, dist, valid):
    s = _dot(q, keys, NT) * (HEAD_DIM ** -0.5) - slope * dist
    return jnp.where(valid, s, NEG_INF)


def _attn_fwd(qkv, g, name):
    _, s, w = qkv.shape
    dil = DILATED_CFG[g][1]
    ln = s // dil
    nb = ln // BLOCK
    slopes = _alibi_slopes(g, dil)

    def body(qkv_ref, o_ref):
        n = pl.program_id(1)
        k0, kw = _attn_window(n, ln)
        cur, win = pl.ds(pl.multiple_of(n * BLOCK, BLOCK), BLOCK), pl.ds(k0, kw)
        dist, valid = _attn_mask(n, k0, kw)
        o_ref[:, w:] = jnp.zeros((BLOCK, PANEL), F32)
        for h in range(HEADS):
            cols = slice(h * HEAD_DIM, (h + 1) * HEAD_DIM)
            sc = _attn_scores(qkv_ref[0, cur, cols], qkv_ref[1, win, cols], slopes[h], dist, valid)
            m = jnp.max(sc, axis=-1, keepdims=True)
            p = jnp.exp(sc - m)
            den = jnp.sum(p, axis=-1, keepdims=True)
            o_ref[:, cols] = _dot(p.astype(BF16), qkv_ref[2, win, cols]) / den
            o_ref[:, w + h:w + h + 1] = m + jnp.log(den)

    return pl.pallas_call(
        body, name=name, grid=(dil, nb),
        in_specs=[pl.BlockSpec((3, ln, w), lambda r, n: (0, r, 0))],
        out_specs=pl.BlockSpec((BLOCK, ATTN_EXT), lambda r, n: (r * nb + n, 0)),
        out_shape=jax.ShapeDtypeStruct((s, ATTN_EXT), F32),
        compiler_params=_params(("parallel", "parallel")),
    )(qkv)


def _attn_bwd(qkv, dext, g, name):
    _, s, w = qkv.shape
    dil = DILATED_CFG[g][1]
    ln = s // dil
    nb = ln // BLOCK
    slopes = _alibi_slopes(g, dil)
    scale = HEAD_DIM ** -0.5

    def body(qkv_ref, de_ref, d_ref, dk_ref, dv_ref):
        n = pl.program_id(1)

        @pl.when(n == 0)
        def _():
            dk_ref[...] = jnp.zeros_like(dk_ref)
            dv_ref[...] = jnp.zeros_like(dv_ref)

        k0, kw = _attn_window(n, ln)
        cur, win = pl.ds(pl.multiple_of(n * BLOCK, BLOCK), BLOCK), pl.ds(k0, kw)
        dist, valid = _attn_mask(n, k0, kw)
        for h in range(HEADS):
            cols = slice(h * HEAD_DIM, (h + 1) * HEAD_DIM)
            q, keys = qkv_ref[0, cur, cols], qkv_ref[1, win, cols]
            dob = de_ref[:, cols].astype(BF16)
            p = jnp.exp(_attn_scores(q, keys, slopes[h], dist, valid) - de_ref[:, w + h:w + h + 1])
            dd = de_ref[:, w + DVEC_LANE + h:w + DVEC_LANE + h + 1]
            ds = (p * (_dot(dob, qkv_ref[2, win, cols], NT) - dd)).astype(BF16)
            d_ref[0, cur, cols] = (scale * _dot(ds, keys)).astype(BF16)
            dv_ref[win, cols] += _dot(p.astype(BF16), dob, TN)
            dk_ref[win, cols] += scale * _dot(ds, q, TN)

        @pl.when(n == nb - 1)
        def _():
            d_ref[1] = dk_ref[...].astype(BF16)
            d_ref[2] = dv_ref[...].astype(BF16)

    return pl.pallas_call(
        body, name=name, grid=(dil, nb),
        in_specs=[pl.BlockSpec((3, ln, w), lambda r, n: (0, r, 0)),
                  pl.BlockSpec((BLOCK, ATTN_EXT), lambda r, n: (r * nb + n, 0))],
        out_specs=pl.BlockSpec((3, ln, w), lambda r, n: (0, r, 0)),
        out_shape=jax.ShapeDtypeStruct((3, s, w), BF16),
        scratch_shapes=[pltpu.VMEM((ln, w), F32), pltpu.VMEM((ln, w), F32)],
        compiler_params=_params(("parallel", "arbitrary")),
    )(qkv, dext)


def _attn_merge(e0, e1, e2, name):
    s = e0.shape[0]
    w = ATTN_WIDTH

    def body(e0_ref, e1_ref, e2_ref, m_ref, mb_ref, lse_ref):
        refs = (e0_ref, e1_ref, e2_ref)
        l = [r[:, w:w + HEADS] for r in refs]
        mx = jnp.maximum(jnp.maximum(l[0], l[1]), l[2])
        e = [jnp.exp(v - mx) for v in l]
        z = e[0] + e[1] + e[2]
        lse_ref[...] = mx + jnp.log(z)
        wts = [v / z for v in e]
        for h in range(HEADS):
            cols = slice(h * HEAD_DIM, (h + 1) * HEAD_DIM)
            acc = wts[0][:, h:h + 1] * refs[0][:, cols]
            for g in range(1, N_GROUPS_A):
                acc = acc + wts[g][:, h:h + 1] * refs[g][:, cols]
            m_ref[:, cols] = acc
            mb_ref[:, cols] = acc.astype(BF16)

    ext = pl.BlockSpec((ROW_TILE, ATTN_EXT), lambda i: (i, 0))
    row = pl.BlockSpec((ROW_TILE, w), lambda i: (i, 0))
    return pl.pallas_call(
        body, name=name, grid=(s // ROW_TILE,),
        in_specs=[ext, ext, ext],
        out_specs=[row, row, pl.BlockSpec((ROW_TILE, HEADS), lambda i: (i, 0))],
        out_shape=[jax.ShapeDtypeStruct((s, w), F32), jax.ShapeDtypeStruct((s, w), BF16),
                   jax.ShapeDtypeStruct((s, HEADS), F32)],
        compiler_params=_params(("parallel",)),
    )(e0, e1, e2)


def _attn_dvec(dmerged, merged, lse_all, name):
    s, w = merged.shape

    def body(dm_ref, m_ref, lse_ref, de_ref):
        dmv = dm_ref[...]
        de_ref[:, :w] = dmv
        de_ref[:, w:] = jnp.zeros((ROW_TILE, PANEL), F32)
        de_ref[:, w:w + HEADS] = lse_ref[...]
        prod = dmv * m_ref[...]
        for h in range(HEADS):
            lane = w + DVEC_LANE + h
            de_ref[:, lane:lane + 1] = jnp.sum(prod[:, h * HEAD_DIM:(h + 1) * HEAD_DIM], axis=-1, keepdims=True)

    row = pl.BlockSpec((ROW_TILE, w), lambda i: (i, 0))
    return pl.pallas_call(
        body, name=name, grid=(s // ROW_TILE,),
        in_specs=[row, row, pl.BlockSpec((ROW_TILE, HEADS), lambda i: (i, 0))],
        out_specs=pl.BlockSpec((ROW_TILE, ATTN_EXT), lambda i: (i, 0)),
        out_shape=jax.ShapeDtypeStruct((s, ATTN_EXT), F32),
        compiler_params=_params(("parallel",)),
    )(dmerged, merged, lse_all)


def _attention_fwd(n, wqkv, wo, tag):
    ns, qkvs, exts = [], [], []
    for g, (_, dil) in enumerate(DILATED_CFG):
        ng = n if dil == 1 else _residue_order(n, dil, f"{tag}_order_g{g}")
        qkv = _qkv_proj(ng, wqkv, g, f"{tag}_qkv_g{g}")
        ext = _attn_fwd(qkv, g, f"{tag}_fwd_g{g}")
        ns.append(ng)
        qkvs.append(qkv)
        exts.append(ext if dil == 1 else _token_order(ext, dil, None, f"{tag}_unorder_g{g}"))
    merged, merged_bf, lse_all = _attn_merge(*exts, f"{tag}_merge")
    m = _matmul(merged_bf, wo, "nn", F32, f"{tag}_wo")
    return m, (ns, qkvs, merged, merged_bf, lse_all)


def _attention_bwd(dm, wqkv, wo, saved, tag):
    ns, qkvs, merged, merged_bf, lse_all = saved
    d_wo = _matmul(merged_bf, dm, "tn", BF16, f"{tag}_dwo")
    dmerged = _matmul(dm, wo, "nt", F32, f"{tag}_dmerged")
    dext = _attn_dvec(dmerged, merged, lse_all, f"{tag}_dvec")
    width = 3 * ATTN_WIDTH
    d_wqkv, dn = [], None
    for g, (_, dil) in enumerate(DILATED_CFG):
        dext_g = dext if dil == 1 else _residue_order(dext, dil, f"{tag}_dorder_g{g}")
        dqkv = _attn_bwd(qkvs[g], dext_g, g, f"{tag}_bwd_g{g}")
        d_wqkv.append(_matmul(ns[g], dqkv, "tn", BF16, f"{tag}_dwqkv_g{g}", b_parts=3))
        dn_g = _matmul(dqkv, wqkv[:, g * width:(g + 1) * width], "nt", F32, f"{tag}_dn_g{g}", a_parts=3)
        dn = dn_g if dil == 1 else _token_order(dn_g, dil, dn, f"{tag}_dn_sum_g{g}")
    return dn, jnp.concatenate(d_wqkv, axis=1), d_wo


def _local_step(x, tgt, w):
    ng = w["norm_g"]

    def gain(i, j):
        return ng[i, j][None, :]

    saved = []
    ia = ib = ic = 0
    n = _rms_fwd(x, gain(0, 0), None, BF16, "norm_first")
    for i in range(DEPTH):
        kind = i % 3
        if kind == 0:
            m, ms = _attention_fwd(n, w["attn_w_qkv"][ia], w["attn_w_o"][ia], "attn")
            mixer = ("a", ia, ms)
            ia += 1
        elif kind == 1:
            z, y = _sconv_fwd(n, w["conv_w_in"][ib], w["conv_w_dw"][ib], "sconv_fwd")
            m = _matmul(y, w["conv_w_out"][ib], "nn", F32, "sconv_out")
            mixer = ("b", ib, (z, y))
            ib += 1
        else:
            p, y = _pool_fwd(n, w["pool_w_in"][ic], w["pool_w_grp"][ic], w["pool_scale"][ic][None, :], "pool_fwd")
            m = _matmul(y, w["pool_w_out"][ic], "nn", F32, "pool_out")
            mixer = ("c", ic, (p, y))
            ic += 1
        x1, n2 = _rms_res_pre(m, gain(i, 1), x, gain(i, 2), "norm_res_pre")
        h, a = _ffn_up(n2, w["ffn_w_up"][i], w["ffn_w_dw"][i], "ffn_up")
        f = _matmul(a, w["ffn_w_down"][i].reshape(D_FF, D_MODEL), "nn", F32, "ffn_down", a_parts=FFN_PAIRS)
        saved.append((x, n, m, mixer, x1, n2, h, a, f))
        if i + 1 < DEPTH:
            x, n = _rms_res_pre(f, gain(i, 3), x1, gain(i + 1, 0), "norm_res_pre")
        else:
            x = _rms_fwd(f, gain(i, 3), x1, F32, "norm_res")

    loss, dx = _loss_head(x, tgt, "loss_head")

    g_norm = [[None] * 4 for _ in range(DEPTH)]
    grads = {k: [] for k in ("attn_w_qkv", "attn_w_o", "conv_w_in", "conv_w_dw", "conv_w_out", "pool_w_in",
                             "pool_w_grp", "pool_scale", "pool_w_out", "ffn_w_up", "ffn_w_dw", "ffn_w_down")}
    df, g_norm[DEPTH - 1][3] = _rms_bwd(saved[-1][-1], gain(DEPTH - 1, 3), dx, None, BF16, "norm_bwd_sub")
    for i in reversed(range(DEPTH)):
        xin, n, m, mixer, x1, n2, h, a, f = saved[i]
        d_wdown = _matmul(a, df, "tn", BF16, "ffn_dwdown", a_parts=FFN_PAIRS)
        grads["ffn_w_down"].append(d_wdown.reshape(N_DEV, D_FF // N_DEV, D_MODEL))
        dh, dwg, dwu = _ffn_mid_bwd(df, w["ffn_w_down"][i].reshape(FFN_PAIRS, -1, D_MODEL), h, w["ffn_w_dw"][i],
                                    "ffn_mid_bwd")
        grads["ffn_w_dw"].append(jnp.concatenate([dwg, dwu], axis=0))
        grads["ffn_w_up"].append(_ffn_dwup(n2, dh, "ffn_dwup"))
        dn2 = _ffn_dn(dh, w["ffn_w_up"][i], "ffn_dn")
        dx1, dm, g_norm[i][2], g_norm[i][1] = _rms_bwd_pair(x1, gain(i, 2), dn2, dx, m, gain(i, 1), "norm_bwd_pair")
        kind, idx, ms = mixer
        if kind == "a":
            dn, d_wqkv, d_wo = _attention_bwd(dm, w["attn_w_qkv"][idx], w["attn_w_o"][idx], ms, "attn")
            grads["attn_w_qkv"].append(d_wqkv)
            grads["attn_w_o"].append(d_wo)
        elif kind == "b":
            z, y = ms
            grads["conv_w_out"].append(_matmul(y, dm, "tn", BF16, "sconv_dwout"))
            dz, ddw = _sconv_mid_bwd(dm, w["conv_w_out"][idx], z, w["conv_w_dw"][idx], "sconv_mid_bwd")
            grads["conv_w_dw"].append(ddw)
            grads["conv_w_in"].append(_matmul(n, dz, "tn", BF16, "sconv_dwin", b_parts=3))
            dn = _matmul(dz, w["conv_w_in"][idx], "nt", F32, "sconv_dn", a_parts=3)
        else:
            p, y = ms
            grads["pool_w_out"].append(_matmul(y, dm, "tn", BF16, "pool_dwout"))
            du, dwgrp, dscale = _pool_mid_bwd(dm, w["pool_w_out"][idx], p, w["pool_w_grp"][idx],
                                              w["pool_scale"][idx][None, :], "pool_mid_bwd")
            grads["pool_w_grp"].append(dwgrp)
            grads["pool_scale"].append(dscale[0])
            grads["pool_w_in"].append(_matmul(n, du, "tn", BF16, "pool_dwin"))
            dn = _matmul(du, w["pool_w_in"][idx], "nt", F32, "pool_dn")
        if i > 0:
            dx, df, g_norm[i][0], g_norm[i - 1][3] = _rms_bwd_pair(xin, gain(i, 0), dn, dx1, saved[i - 1][-1],
                                                                   gain(i - 1, 3), "norm_bwd_pair")
        else:
            dx, g_norm[0][0] = _rms_bwd(xin, gain(0, 0), dn, dx1, F32, "norm_bwd_res")

    out = {k: v[::-1] if k.startswith("ffn_") else jnp.stack(v[::-1]) for k, v in grads.items()}
    out["norm_g"] = jnp.stack([jnp.concatenate(row, axis=0) for row in g_norm])
    return loss, dx, out


_AXES = ("x", "y", "c")
ROUTE_A = ("y", "x", "c")
ROUTE_B = ("x", "y", "c")
GATHER_COPIES = 7


def _dev_index(pos):
    return 4 * pos["x"] + 2 * pos["y"] + pos["c"]


def _gather(shards, routes, name):
    n = len(shards)

    def body(*refs):
        srcs, outs = refs[:n], refs[n:2 * n]
        send_sems, recv_sems = refs[2 * n:]
        pos = {a: lax.axis_index(a) for a in _AXES}

        def flipped(axes):
            return {a: 1 - pos[a] if a in axes else pos[a] for a in _AXES}

        def copy(i, k, holder, to):
            slot = outs[i].at[_dev_index(pos if holder is None else holder)]
            return pltpu.make_async_remote_copy(
                src_ref=srcs[i] if holder is None else slot, dst_ref=slot,
                send_sem=send_sems.at[GATHER_COPIES * i + k], recv_sem=recv_sems.at[GATHER_COPIES * i + k],
                device_id=tuple(to[a] for a in _AXES), device_id_type=pl.DeviceIdType.MESH)

        plans = []
        for i, (a1, a2, a3) in enumerate(routes):
            p1, p2, p12, p3 = flipped((a1,)), flipped((a2,)), flipped((a1, a2)), flipped((a3,))
            plans.append([copy(i, 0, None, p1), copy(i, 1, None, p2), copy(i, 2, p1, p2), copy(i, 3, None, p3),
                          copy(i, 4, p1, p3), copy(i, 5, p2, p3), copy(i, 6, p12, p3)])
        for c in plans:
            c[0].start()
            c[1].start()
            c[3].start()
        for c in plans:
            c[0].wait_recv()
            c[2].start()
            c[4].start()
        for c in plans:
            c[1].wait_recv()
            c[5].start()
        for c in plans:
            c[2].wait_recv()
            c[6].start()
        for c in plans:
            for k in range(3, GATHER_COPIES):
                c[k].wait_recv()
        for c in plans:
            for k in range(GATHER_COPIES):
                c[k].wait_send()

    hbm = pl.BlockSpec(memory_space=pl.ANY)
    return pl.pallas_call(
        body, name=name, in_specs=[hbm] * n, out_specs=[hbm] * n,
        out_shape=[jax.ShapeDtypeStruct((N_DEV,) + a.shape, a.dtype) for a in shards],
        scratch_shapes=[pltpu.SemaphoreType.DMA((GATHER_COPIES * n,)), pltpu.SemaphoreType.DMA((GATHER_COPIES * n,))],
    )(*shards)


def _exchange_half(arrays, axes, name):
    n = len(arrays)

    def body(*refs):
        srcs, outs = refs[:n], refs[n:2 * n]
        send_sems, recv_sems = refs[2 * n:]
        pos = {a: lax.axis_index(a) for a in _AXES}
        copies = []
        for i, axis in enumerate(axes):
            peer = tuple(1 - pos[a] if a == axis else pos[a] for a in _AXES)
            rc = pltpu.make_async_remote_copy(src_ref=srcs[i].at[:, 1 - pos[axis]], dst_ref=outs[i],
                                              send_sem=send_sems.at[i], recv_sem=recv_sems.at[i], device_id=peer,
                                              device_id_type=pl.DeviceIdType.MESH)
            rc.start()
            copies.append(rc)
        for rc in copies:
            rc.wait_send()
            rc.wait_recv()

    hbm = pl.BlockSpec(memory_space=pl.ANY)
    return pl.pallas_call(
        body, name=name, in_specs=[hbm] * n, out_specs=[hbm] * n,
        out_shape=[jax.ShapeDtypeStruct((a.shape[0],) + a.shape[2:], a.dtype) for a in arrays],
        scratch_shapes=[pltpu.SemaphoreType.DMA((n,)), pltpu.SemaphoreType.DMA((n,))],
    )(*arrays)


ADD_ROW_TILES = (1024, 704, 512, 352, 256, 128, 96, 64, 32, 16)


def _add_half(a, recv, me, out_dtype, name):
    p, q, cols = recv.shape
    tr = _pick(q, ADD_ROW_TILES)

    def body(me_ref, a_ref, b_ref, o_ref):
        o_ref[...] = (a_ref[...].astype(F32) + b_ref[...].astype(F32)).astype(o_ref.dtype)

    return pl.pallas_call(
        body, name=name,
        grid_spec=pltpu.PrefetchScalarGridSpec(
            num_scalar_prefetch=1, grid=(p, q // tr),
            in_specs=[pl.BlockSpec((None, None, tr, cols), lambda j, i, m: (j, m[0], i, 0)),
                      pl.BlockSpec((None, tr, cols), lambda j, i, m: (j, i, 0))],
            out_specs=pl.BlockSpec((None, tr, cols), lambda j, i, m: (j, i, 0))),
        out_shape=jax.ShapeDtypeStruct((p, q, cols), out_dtype),
        compiler_params=_params(("parallel", "parallel")),
    )(me, a, recv)


def _reduce_scatter(slots, routes, tags):
    coord = {a: lax.axis_index(a).astype(jnp.int32).reshape(1) for a in _AXES}
    shapes = [a.shape[1:] for a in slots]
    rows = [math.prod(s[:-1]) for s in shapes]
    arrays = [a.reshape(4, 2, n, s[-1]) for a, n, s in zip(slots, rows, shapes)]
    for stage in range(3):
        axes = [r[2 - stage] for r in routes]
        recv = _exchange_half(arrays, axes, f"scatter_{stage + 1}")
        dt = F32 if stage == 2 else BF16
        arrays = [_add_half(a, r, coord[ax], dt, f"scatter_add_{stage + 1}_{t}")
                  for a, r, ax, t in zip(arrays, recv, axes, tags)]
        if stage == 0:
            arrays = [a.reshape((1, 2, 2 * n, s[-1]) if r[1] == "x" else (2, 2, n, s[-1]))
                      for a, n, s, r in zip(arrays, rows, shapes, routes)]
        elif stage == 1:
            arrays = [a.reshape(1, 2, n, s[-1]) for a, n, s in zip(arrays, rows, shapes)]
    return [a.reshape(s) for a, s in zip(arrays, shapes)]


_WEIGHTS = {
    "norm_g": ((DEPTH, 4, D_MODEL), 2, True),
    "attn_w_qkv": ((2, D_MODEL, 4608), 2, False),
    "attn_w_o": ((2, ATTN_WIDTH, D_MODEL), 2, False),
    "conv_w_in": ((1, D_MODEL, 3 * D_MODEL), 2, False),
    "conv_w_dw": ((1, 3, D_MODEL), 2, True),
    "conv_w_out": ((1, D_MODEL, D_MODEL), 1, False),
    "pool_w_in": ((1, D_MODEL, D_MODEL), 1, False),
    "pool_w_grp": ((1, 4, POOL_GROUP_DIM, POOL_GROUP_DIM), 2, False),
    "pool_scale": ((1, D_MODEL), 1, True),
    "pool_w_out": ((1, D_MODEL, D_MODEL), 1, False),
    "ffn_w_up": ((DEPTH, D_MODEL, 2 * D_FF), 2, False),
    "ffn_w_dw": ((DEPTH, 3, 2 * D_FF), 2, True),
    "ffn_w_down": ((DEPTH, D_FF, D_MODEL), 1, False),
}
_NAMES = tuple(_WEIGHTS)
_VECTORS = tuple(k for k in _NAMES if _WEIGHTS[k][2])
_MATRICES = tuple(k for k in _NAMES if not _WEIGHTS[k][2])
_FFN = ("ffn_w_up", "ffn_w_down")
_ON_ROUTE_A = ("ffn_w_up", "attn_w_o", "conv_w_out", "pool_w_in")
PACK_ROWS = 16


def _route(name):
    return ROUTE_A if name in _ON_ROUTE_A else ROUTE_B


def _shard_shape(name):
    shape, ax, _ = _WEIGHTS[name]
    return tuple(s // N_DEV if i == ax else s for i, s in enumerate(shape))


def _full_from_slots(slots, name, layers=None):
    shape, ax, _ = _WEIGHTS[name]
    if layers is not None:
        shape = (layers,) + shape[1:]
    return jnp.moveaxis(slots, 0, ax).reshape(shape)


def _slots_from_full(full, name):
    shape, ax, _ = _WEIGHTS[name]
    split = shape[:ax] + (N_DEV, shape[ax] // N_DEV) + shape[ax + 1:]
    return jnp.moveaxis(full.reshape(split), ax, 0)


def _pack_vectors(parts, lead):
    rows = []
    for k in _VECTORS:
        r = parts[k].reshape(lead + (-1, LANES))
        pad = -r.shape[-2] % PACK_ROWS
        rows.append(jnp.pad(r, [(0, 0)] * len(lead) + [(0, pad), (0, 0)]))
    return jnp.concatenate(rows, axis=len(lead))


def _unpack_vectors(buf, lead):
    out, r0 = {}, 0
    for k in _VECTORS:
        shard = _shard_shape(k)
        rows = math.prod(shard) // LANES
        out[k] = buf[..., r0:r0 + rows, :].reshape(lead + shard)
        r0 += rows + (-rows % PACK_ROWS)
    return out


def _comm_items():
    items = []
    for k in _MATRICES:
        items += [(k, l) for l in range(DEPTH)] if k in _FFN else [(k, None)]
    return items + [("vectors", None)]


def _all_gather_weights(shards):
    items = _comm_items()
    cast = {k: shards[k].astype(BF16) for k in _MATRICES}
    send = [_pack_vectors(shards, ()) if k == "vectors" else cast[k] if l is None else cast[k][l] for k, l in items]
    outs = _gather(send, [_route(k) for k, _ in items], "gather")
    me = _dev_index({a: lax.axis_index(a) for a in _AXES})
    outs = [lax.dynamic_update_index_in_dim(o, s, me, 0) for o, s in zip(outs, send)]
    full = {k: [None] * DEPTH for k in _FFN}
    for (k, l), o in zip(items, outs):
        if k in _FFN:
            full[k][l] = o
        elif k == "vectors":
            vec = _unpack_vectors(o, (N_DEV,))
        else:
            full[k] = [_full_from_slots(o[:, j:j + 1], k, layers=1)[0] for j in range(_WEIGHTS[k][0][0])]
    full.update({k: _full_from_slots(vec[k], k) for k in _VECTORS})
    full["ffn_w_dw"] = [vec["ffn_w_dw"][:, l] for l in range(DEPTH)]
    return full


def _reduce_scatter_grads(grads):
    items = _comm_items()
    vec_slots = {k: _slots_from_full(grads[k], k) for k in _VECTORS if k != "ffn_w_dw"}
    vec_slots["ffn_w_dw"] = jnp.stack(grads["ffn_w_dw"], axis=1)
    slots = []
    for k, l in items:
        if k == "vectors":
            slots.append(_pack_vectors(vec_slots, (N_DEV,)).astype(BF16))
        elif k in _FFN:
            slots.append(grads[k][l])
        else:
            slots.append(_slots_from_full(grads[k], k).astype(BF16))
    sums = _reduce_scatter(slots, [_route(k) for k, _ in items], [k if l is None else f"{k}{l}" for k, l in items])
    out = {k: [None] * DEPTH for k in _FFN}
    for (k, l), s in zip(items, sums):
        if k in _FFN:
            out[k][l] = s
        elif k == "vectors":
            out.update(_unpack_vectors(s, ()))
        else:
            out[k] = s
    out.update({k: jnp.stack(out[k]) for k in _FFN})
    return out


def _adamw(w, g, m, v, name):
    shape = w.shape
    cols = shape[-1]
    view = shape if len(shape) == 3 else (1, math.prod(shape[:-1]), cols)
    layers, rows, _ = view
    tr = _pick(rows, (512, 256, 128, 64, 32, 16, 8))

    def body(w_ref, g_ref, m_ref, v_ref, d_ref, nm_ref, nv_ref):
        gv = g_ref[...]
        nm = ADAM_B1 * m_ref[...] + (1.0 - ADAM_B1) * gv
        nv = ADAM_B2 * v_ref[...] + (1.0 - ADAM_B2) * jnp.square(gv)
        m_hat = nm / (1.0 - ADAM_B1 ** ADAM_STEP)
        v_hat = nv / (1.0 - ADAM_B2 ** ADAM_STEP)
        d_ref[...] = -ADAM_LR * (m_hat / (jnp.sqrt(v_hat) + ADAM_EPS) + ADAM_WD * w_ref[...])
        nm_ref[...] = nm
        nv_ref[...] = nv

    blk = pl.BlockSpec((None, tr, cols), lambda l, i: (l, i, 0))
    shp = jax.ShapeDtypeStruct(view, F32)
    outs = pl.pallas_call(
        body, name=name, grid=(layers, rows // tr), in_specs=[blk] * 4, out_specs=[blk] * 3, out_shape=[shp] * 3,
        compiler_params=_params(("parallel", "parallel")),
    )(*[t.reshape(view) for t in (w, g, m, v)])
    return [o.reshape(shape) for o in outs]


def kernel(x, norm_g, attn_w_qkv, attn_w_o, conv_w_in, conv_w_dw, conv_w_out, pool_w_in, pool_w_grp, pool_scale, pool_w_out, ffn_w_up, ffn_w_dw, ffn_w_down, loss_target, m_norm_g, m_attn_w_qkv, m_attn_w_o, m_conv_w_in, m_conv_w_dw, m_conv_w_out, m_pool_w_in, m_pool_w_grp, m_pool_scale, m_pool_w_out, m_ffn_w_up, m_ffn_w_dw, m_ffn_w_down, v_norm_g, v_attn_w_qkv, v_attn_w_o, v_conv_w_in, v_conv_w_dw, v_conv_w_out, v_pool_w_in, v_pool_w_grp, v_pool_scale, v_pool_w_out, v_ffn_w_up, v_ffn_w_dw, v_ffn_w_down):
    shards = dict(zip(_NAMES, (norm_g, attn_w_qkv, attn_w_o, conv_w_in, conv_w_dw, conv_w_out, pool_w_in,
                               pool_w_grp, pool_scale, pool_w_out, ffn_w_up, ffn_w_dw, ffn_w_down)))
    moms = dict(zip(_NAMES, (m_norm_g, m_attn_w_qkv, m_attn_w_o, m_conv_w_in, m_conv_w_dw, m_conv_w_out,
                             m_pool_w_in, m_pool_w_grp, m_pool_scale, m_pool_w_out, m_ffn_w_up, m_ffn_w_dw,
                             m_ffn_w_down)))
    vels = dict(zip(_NAMES, (v_norm_g, v_attn_w_qkv, v_attn_w_o, v_conv_w_in, v_conv_w_dw, v_conv_w_out,
                             v_pool_w_in, v_pool_w_grp, v_pool_scale, v_pool_w_out, v_ffn_w_up, v_ffn_w_dw,
                             v_ffn_w_down)))
    full = _all_gather_weights(shards)
    loss, grad_x, grads = _local_step(x[0], loss_target[0], full)
    loss = lax.psum(loss[0, 0], _AXES)
    gsh = _reduce_scatter_grads(grads)
    deltas, new_m, new_v = [], [], []
    for k in _NAMES:
        d, nm, nv = _adamw(shards[k], gsh[k], moms[k], vels[k], f"adamw_{k}")
        deltas.append(d)
        new_m.append(nm)
        new_v.append(nv)
    return (loss, grad_x[None], *[gsh[k] for k in _NAMES], *deltas, *new_m, *new_v)
```

```python
import functools
import math

import numpy as np
import jax
import jax.numpy as jnp
from jax import lax
from jax.experimental import pallas as pl
from jax.experimental.pallas import tpu as pltpu

F32, BF16 = jnp.float32, jnp.bfloat16

D_MODEL = 1024
SEQ = 2048
DEPTH = 4
DILATED_CFG = ((128, 1), (512, 4), (2048, 16))
N_GROUPS_A = 3
HEADS = 8
HEAD_DIM = 64
ATTN_WIDTH = HEADS * HEAD_DIM
N_HEADS_A = N_GROUPS_A * HEADS
BLOCK = 128
NEG_INF = -1e30
POOL_GROUP_DIM = 256
D_FF = 2816
RMS_EPS = 1e-6
ADAM_LR, ADAM_B1, ADAM_B2, ADAM_EPS, ADAM_WD, ADAM_STEP = 0.001, 0.9, 0.999, 1e-08, 0.01, 10

N_DEV = 8
LANES = 128
V7X_VMEM_BYTES = 64 * 2 ** 20
VMEM_LIMIT_BYTES = V7X_VMEM_BYTES - 8 * 2 ** 20
COL_TILE = 256
ROW_TILE = 256
MATMUL_TILES = (1024, 1408, 512, 256, 128)
TN_RESIDENT_K = 2048

NN = (((1,), (0,)), ((), ()))
NT = (((1,), (1,)), ((), ()))
TN = (((0,), (0,)), ((), ()))


def _dot(a, b, dims=NN):
    return lax.dot_general(a, b, dims, preferred_element_type=F32)


def _params(sem=None):
    return pltpu.CompilerParams(dimension_semantics=sem, vmem_limit_bytes=VMEM_LIMIT_BYTES)


def _pick(n, prefs):
    for p in prefs:
        if n % p == 0:
            return p
    return n


def _matmul(a, b, mode, out_dtype, name, a_parts=1, b_parts=1):
    if mode == "nn":
        m, k = a.shape[-2], a.shape[-1] * a_parts
        n = b.shape[-1] * b_parts
    elif mode == "nt":
        m, k = a.shape[-2], a.shape[-1] * a_parts
        n = b.shape[-2]
    else:
        k, m = a.shape[-2], a.shape[-1] * a_parts
        n = b.shape[-1] * b_parts
    tm = _pick(m, MATMUL_TILES)
    tn = _pick(n // b_parts if mode != "nt" else n, MATMUL_TILES)
    kk = k // a_parts if mode != "tn" else k
    tk = _pick(kk, MATMUL_TILES)
    if mode == "tn":
        tm = _pick(m // a_parts, MATMUL_TILES)
        if k <= TN_RESIDENT_K:
            tk = k
    gm, gn, gk = m // tm, n // tn, k // tk

    def a_idx(i, j, kq):
        if mode == "tn":
            r, c, per = kq, i, (m // a_parts) // tm
        else:
            r, c, per = i, kq, (k // a_parts) // tk
        return (r, c) if a_parts == 1 else (c // per, r, c % per)

    def b_idx(i, j, kq):
        if mode == "nt":
            return (j, kq)
        per = (n // b_parts) // tn
        return (kq, j) if b_parts == 1 else (j // per, kq, j % per)

    a_blk = (tk, tm) if mode == "tn" else (tm, tk)
    b_blk = (tn, tk) if mode == "nt" else (tk, tn)
    if a_parts > 1:
        a_blk = (None,) + a_blk
    if b_parts > 1:
        b_blk = (None,) + b_blk
    dims = {"nn": NN, "nt": NT, "tn": TN}[mode]

    def body_single(a_ref, b_ref, o_ref):
        o_ref[...] = _dot(a_ref[...], b_ref[...], dims).astype(o_ref.dtype)

    def body(a_ref, b_ref, o_ref, acc_ref):
        kq = pl.program_id(2)

        @pl.when(kq == 0)
        def _():
            acc_ref[...] = jnp.zeros_like(acc_ref)

        acc_ref[...] += _dot(a_ref[...], b_ref[...], dims)

        @pl.when(kq == gk - 1)
        def _():
            o_ref[...] = acc_ref[...].astype(o_ref.dtype)

    return pl.pallas_call(
        body_single if gk == 1 else body, name=name, grid=(gm, gn, gk),
        in_specs=[pl.BlockSpec(a_blk, a_idx), pl.BlockSpec(b_blk, b_idx)],
        out_specs=pl.BlockSpec((tm, tn), lambda i, j, kq: (i, j)),
        out_shape=jax.ShapeDtypeStruct((m, n), out_dtype),
        scratch_shapes=[] if gk == 1 else [pltpu.VMEM((tm, tn), F32)],
        compiler_params=_params(("parallel", "parallel", "arbitrary")),
    )(a, b)


def _rms_fwd(xin, g, res, out_dtype, name):
    s, d = xin.shape
    has_res = res is not None

    def body(*refs):
        x_ref, g_ref = refs[0], refs[1]
        o_ref = refs[-1]
        x = x_ref[...]
        r = lax.rsqrt(jnp.mean(x * x, axis=-1, keepdims=True) + RMS_EPS)
        y = x * r * g_ref[...]
        if has_res:
            y = refs[2][...] + y
        o_ref[...] = y.astype(o_ref.dtype)

    row = pl.BlockSpec((ROW_TILE, d), lambda i: (i, 0))
    vec = pl.BlockSpec((1, d), lambda i: (0, 0))
    ins = [xin, g] + ([res] if has_res else [])
    return pl.pallas_call(
        body, name=name, grid=(s // ROW_TILE,),
        in_specs=[row, vec] + ([row] if has_res else []),
        out_specs=row, out_shape=jax.ShapeDtypeStruct((s, d), out_dtype),
        compiler_params=_params(("parallel",)),
    )(*ins)


def _rms_bwd(xin, g, dy, dres, out_dtype, name):
    s, d = xin.shape
    has_res = dres is not None

    def body(*refs):
        x_ref, g_ref, dy_ref = refs[0], refs[1], refs[2]
        dx_ref, dg_ref = refs[-2], refs[-1]

        @pl.when(pl.program_id(0) == 0)
        def _():
            dg_ref[...] = jnp.zeros_like(dg_ref)

        x = x_ref[...]
        dyv = dy_ref[...].astype(F32)
        r = lax.rsqrt(jnp.mean(x * x, axis=-1, keepdims=True) + RMS_EPS)
        xhat = x * r
        u = dyv * g_ref[...]
        dx = r * (u - xhat * jnp.mean(u * xhat, axis=-1, keepdims=True))
        if has_res:
            dx = refs[3][...] + dx
        dx_ref[...] = dx.astype(dx_ref.dtype)
        dg_ref[...] += jnp.sum(dyv * xhat, axis=0, keepdims=True)

    row = pl.BlockSpec((ROW_TILE, d), lambda i: (i, 0))
    vec = pl.BlockSpec((1, d), lambda i: (0, 0))
    ins = [xin, g, dy] + ([dres] if has_res else [])
    return pl.pallas_call(
        body, name=name, grid=(s // ROW_TILE,),
        in_specs=[row, vec, row] + ([row] if has_res else []),
        out_specs=[row, vec],
        out_shape=[jax.ShapeDtypeStruct((s, d), out_dtype), jax.ShapeDtypeStruct((1, d), F32)],
        compiler_params=_params(("arbitrary",)),
    )(*ins)


def _rms(x):
    r = lax.rsqrt(jnp.mean(x * x, axis=-1, keepdims=True) + RMS_EPS)
    return r, x * r


def _rms_grad(r, xhat, dy, g):
    u = dy * g
    return r * (u - xhat * jnp.mean(u * xhat, axis=-1, keepdims=True))


def _rms_res_pre(sub, g_post, res, g_pre, name):
    s, d = sub.shape

    def body(sub_ref, gp_ref, res_ref, gn_ref, x_ref, n_ref):
        xnew = res_ref[...] + _rms(sub_ref[...])[1] * gp_ref[...]
        x_ref[...] = xnew
        n_ref[...] = (_rms(xnew)[1] * gn_ref[...]).astype(BF16)

    row = pl.BlockSpec((ROW_TILE, d), lambda i: (i, 0))
    vec = pl.BlockSpec((1, d), lambda i: (0, 0))
    return pl.pallas_call(
        body, name=name, grid=(s // ROW_TILE,),
        in_specs=[row, vec, row, vec], out_specs=[row, row],
        out_shape=[jax.ShapeDtypeStruct((s, d), F32), jax.ShapeDtypeStruct((s, d), BF16)],
        compiler_params=_params(("parallel",)),
    )(sub, g_post, res, g_pre)


def _rms_bwd_pair(xmid, g_pre, dn, dres, sub, g_post, name):
    s, d = xmid.shape

    def body(x_ref, gn_ref, dn_ref, dres_ref, sub_ref, gp_ref, dx_ref, dsub_ref, dgn_ref, dgp_ref):
        @pl.when(pl.program_id(0) == 0)
        def _():
            dgn_ref[...] = jnp.zeros_like(dgn_ref)
            dgp_ref[...] = jnp.zeros_like(dgp_ref)

        dnv = dn_ref[...].astype(F32)
        r, xhat = _rms(x_ref[...])
        dx = dres_ref[...] + _rms_grad(r, xhat, dnv, gn_ref[...])
        dx_ref[...] = dx
        dgn_ref[...] += jnp.sum(dnv * xhat, axis=0, keepdims=True)
        rs, shat = _rms(sub_ref[...])
        dsub_ref[...] = _rms_grad(rs, shat, dx, gp_ref[...]).astype(BF16)
        dgp_ref[...] += jnp.sum(dx * shat, axis=0, keepdims=True)

    row = pl.BlockSpec((ROW_TILE, d), lambda i: (i, 0))
    vec = pl.BlockSpec((1, d), lambda i: (0, 0))
    return pl.pallas_call(
        body, name=name, grid=(s // ROW_TILE,),
        in_specs=[row, vec, row, row, row, vec], out_specs=[row, row, vec, vec],
        out_shape=[jax.ShapeDtypeStruct((s, d), F32), jax.ShapeDtypeStruct((s, d), BF16),
                   jax.ShapeDtypeStruct((1, d), F32), jax.ShapeDtypeStruct((1, d), F32)],
        compiler_params=_params(("arbitrary",)),
    )(xmid, g_pre, dn, dres, sub, g_post)


def _loss_head(y, tgt, name):
    s, d = y.shape

    def body(y_ref, t_ref, l_ref, dy_ref):
        @pl.when(pl.program_id(0) == 0)
        def _():
            l_ref[...] = jnp.zeros_like(l_ref)

        e = y_ref[...] - t_ref[...]
        dy_ref[...] = e / d
        per_tok = jnp.mean(e * e, axis=-1, keepdims=True)
        l_ref[...] += 0.5 * jnp.sum(per_tok, axis=0, keepdims=True)

    row = pl.BlockSpec((ROW_TILE, d), lambda i: (i, 0))
    return pl.pallas_call(
        body, name=name, grid=(s // ROW_TILE,),
        in_specs=[row, row],
        out_specs=[pl.BlockSpec((1, 1), lambda i: (0, 0)), row],
        out_shape=[jax.ShapeDtypeStruct((1, 1), F32), jax.ShapeDtypeStruct((s, d), F32)],
        compiler_params=_params(("arbitrary",)),
    )(y, tgt)


def _shift_down(x, k):
    rows = lax.broadcasted_iota(jnp.int32, x.shape, 0)
    return jnp.where(rows >= k, pltpu.roll(x, k, axis=0), 0.0)


def _shift_up(x, k):
    t = x.shape[0]
    rows = lax.broadcasted_iota(jnp.int32, x.shape, 0)
    return jnp.where(rows < t - k, pltpu.roll(x, t - k, axis=0), 0.0)


def _conv3(h, w):
    return w[2:3] * h + w[1:2] * _shift_down(h, 1) + w[0:1] * _shift_down(h, 2)


def _conv3_bwd_x(dc, w):
    return w[2:3] * dc + w[1:2] * _shift_up(dc, 1) + w[0:1] * _shift_up(dc, 2)


def _conv3_bwd_w(dc, h, dw_ref, cols=slice(None)):
    dw_ref[0:1, cols] = jnp.sum(dc * _shift_down(h, 2), axis=0, keepdims=True)
    dw_ref[1:2, cols] = jnp.sum(dc * _shift_down(h, 1), axis=0, keepdims=True)
    dw_ref[2:3, cols] = jnp.sum(dc * h, axis=0, keepdims=True)


FFN_PAIRS = N_DEV // 2


def _lane_chunks(width):
    return [(c0, min(COL_TILE, width - c0)) for c0 in range(0, width, COL_TILE)]


def _ffn_up(n, wup, wdw, name):
    s, d = n.shape
    cw = wup.shape[-1]

    def body(n_ref, wg_ref, wu_ref, dg_ref, du_ref, h_ref, a_ref):
        x = n_ref[...]
        for c0, size in _lane_chunks(cw):
            cols = slice(c0, c0 + size)
            hg = _dot(x, wg_ref[:, cols])
            hu = _dot(x, wu_ref[:, cols])
            h_ref[0, :, cols] = hg.astype(BF16)
            h_ref[1, :, cols] = hu.astype(BF16)
            cg = _conv3(hg, dg_ref[:, cols])
            cu = _conv3(hu, du_ref[:, cols])
            a_ref[:, cols] = (cg * jax.nn.sigmoid(cg) * cu).astype(BF16)

    return pl.pallas_call(
        body, name=name, grid=(FFN_PAIRS,),
        in_specs=[pl.BlockSpec((s, d), lambda j: (0, 0)),
                  pl.BlockSpec((None, d, cw), lambda j: (j, 0, 0)),
                  pl.BlockSpec((None, d, cw), lambda j: (j + FFN_PAIRS, 0, 0)),
                  pl.BlockSpec((None, 3, cw), lambda j: (j, 0, 0)),
                  pl.BlockSpec((None, 3, cw), lambda j: (j + FFN_PAIRS, 0, 0))],
        out_specs=[pl.BlockSpec((None, 2, s, cw), lambda j: (j, 0, 0, 0)),
                   pl.BlockSpec((None, s, cw), lambda j: (j, 0, 0))],
        out_shape=[jax.ShapeDtypeStruct((FFN_PAIRS, 2, s, cw), BF16), jax.ShapeDtypeStruct((FFN_PAIRS, s, cw), BF16)],
        compiler_params=_params(("parallel",)),
    )(n, wup, wup, wdw, wdw)


def _ffn_mid_bwd(do, wdown, h, wdw, name):
    s, d = do.shape
    cw = wdown.shape[1]

    def body(do_ref, wd_ref, h_ref, wg_ref, wu_ref, dh_ref, dwg_ref, dwu_ref):
        dov = do_ref[...]
        for c0, size in _lane_chunks(cw):
            cols = slice(c0, c0 + size)
            da = _dot(dov, wd_ref[cols, :], NT)
            hg = h_ref[0, :, cols].astype(F32)
            hu = h_ref[1, :, cols].astype(F32)
            wg, wu = wg_ref[:, cols], wu_ref[:, cols]
            cg = _conv3(hg, wg)
            cu = _conv3(hu, wu)
            sg = jax.nn.sigmoid(cg)
            dcu = da * (cg * sg)
            dcg = da * cu * (sg * (1.0 + cg * (1.0 - sg)))
            dh_ref[0, :, cols] = _conv3_bwd_x(dcg, wg).astype(BF16)
            dh_ref[1, :, cols] = _conv3_bwd_x(dcu, wu).astype(BF16)
            _conv3_bwd_w(dcg, hg, dwg_ref, cols)
            _conv3_bwd_w(dcu, hu, dwu_ref, cols)

    vec = jax.ShapeDtypeStruct((FFN_PAIRS, 3, cw), F32)
    return pl.pallas_call(
        body, name=name, grid=(FFN_PAIRS,),
        in_specs=[pl.BlockSpec((s, d), lambda j: (0, 0)), pl.BlockSpec((None, cw, d), lambda j: (j, 0, 0)),
                  pl.BlockSpec((None, 2, s, cw), lambda j: (j, 0, 0, 0)),
                  pl.BlockSpec((None, 3, cw), lambda j: (j, 0, 0)),
                  pl.BlockSpec((None, 3, cw), lambda j: (j + FFN_PAIRS, 0, 0))],
        out_specs=[pl.BlockSpec((None, 2, s, cw), lambda j: (j, 0, 0, 0)),
                   pl.BlockSpec((None, 3, cw), lambda j: (j, 0, 0)), pl.BlockSpec((None, 3, cw), lambda j: (j, 0, 0))],
        out_shape=[jax.ShapeDtypeStruct((FFN_PAIRS, 2, s, cw), BF16), vec, vec],
        compiler_params=_params(("parallel",)),
    )(do, wdown, h, wdw, wdw)


def _ffn_dwup(n, dh, name):
    s, d = n.shape
    cw = dh.shape[-1]

    def body(n_ref, dh_ref, o_ref):
        o_ref[...] = _dot(n_ref[...], dh_ref[...], TN).astype(BF16)

    return pl.pallas_call(
        body, name=name, grid=(N_DEV,),
        in_specs=[pl.BlockSpec((s, d), lambda k: (0, 0)),
                  pl.BlockSpec((None, None, s, cw), lambda k: (k % FFN_PAIRS, k // FFN_PAIRS, 0, 0))],
        out_specs=pl.BlockSpec((None, d, cw), lambda k: (k, 0, 0)),
        out_shape=jax.ShapeDtypeStruct((N_DEV, d, cw), BF16),
        compiler_params=_params(("parallel",)),
    )(n, dh)


def _ffn_dn(dh, wup, name):
    s, cw = dh.shape[-2:]
    d = wup.shape[1]
    tm = _pick(s, MATMUL_TILES)

    def body(dh_ref, w_ref, o_ref, acc_ref):
        k = pl.program_id(1)

        @pl.when(k == 0)
        def _():
            acc_ref[...] = jnp.zeros_like(acc_ref)

        acc_ref[...] += _dot(dh_ref[...], w_ref[...], NT)

        @pl.when(k == N_DEV - 1)
        def _():
            o_ref[...] = acc_ref[...]

    return pl.pallas_call(
        body, name=name, grid=(s // tm, N_DEV),
        in_specs=[pl.BlockSpec((None, None, tm, cw), lambda i, k: (k % FFN_PAIRS, k // FFN_PAIRS, i, 0)),
                  pl.BlockSpec((None, d, cw), lambda i, k: (k, 0, 0))],
        out_specs=pl.BlockSpec((tm, d), lambda i, k: (i, 0)),
        out_shape=jax.ShapeDtypeStruct((s, d), F32),
        scratch_shapes=[pltpu.VMEM((tm, d), F32)],
        compiler_params=_params(("parallel", "arbitrary")),
    )(dh, wup)


def _sconv_fwd(n, win, wdw, name):
    s, d = n.shape
    tn = COL_TILE
    nj = d // tn

    def body(n_ref, wb_ref, wc_ref, wh_ref, dw_ref, z_ref, y_ref):
        x = n_ref[...]
        zb = _dot(x, wb_ref[...])
        zc = _dot(x, wc_ref[...])
        zh = _dot(x, wh_ref[...])
        z_ref[0] = zb.astype(BF16)
        z_ref[1] = zc.astype(BF16)
        z_ref[2] = zh.astype(BF16)
        y_ref[...] = (zb * _conv3(zc * zh, dw_ref[...])).astype(BF16)

    return pl.pallas_call(
        body, name=name, grid=(nj,),
        in_specs=[pl.BlockSpec((s, d), lambda j: (0, 0)),
                  pl.BlockSpec((d, tn), lambda j: (0, j)), pl.BlockSpec((d, tn), lambda j: (0, j + nj)),
                  pl.BlockSpec((d, tn), lambda j: (0, j + 2 * nj)), pl.BlockSpec((3, tn), lambda j: (0, j))],
        out_specs=[pl.BlockSpec((3, s, tn), lambda j: (0, 0, j)), pl.BlockSpec((s, tn), lambda j: (0, j))],
        out_shape=[jax.ShapeDtypeStruct((3, s, d), BF16), jax.ShapeDtypeStruct((s, d), BF16)],
        compiler_params=_params(("parallel",)),
    )(n, win, win, win, wdw)


def _sconv_mid_bwd(dm, wout, z, wdw, name):
    s, d = dm.shape
    tn = COL_TILE
    nj = d // tn

    def body(dm_ref, wo_ref, z_ref, w_ref, dz_ref, dw_ref):
        dy = _dot(dm_ref[...], wo_ref[...], NT)
        zb = z_ref[0].astype(F32)
        zc = z_ref[1].astype(F32)
        zh = z_ref[2].astype(F32)
        w = w_ref[...]
        p = zc * zh
        cp = _conv3(p, w)
        dz_ref[0] = (dy * cp).astype(BF16)
        dcp = dy * zb
        dp = _conv3_bwd_x(dcp, w)
        _conv3_bwd_w(dcp, p, dw_ref)
        dz_ref[1] = (dp * zh).astype(BF16)
        dz_ref[2] = (dp * zc).astype(BF16)

    return pl.pallas_call(
        body, name=name, grid=(nj,),
        in_specs=[pl.BlockSpec((s, d), lambda j: (0, 0)), pl.BlockSpec((tn, d), lambda j: (j, 0)),
                  pl.BlockSpec((3, s, tn), lambda j: (0, 0, j)), pl.BlockSpec((3, tn), lambda j: (0, j))],
        out_specs=[pl.BlockSpec((3, s, tn), lambda j: (0, 0, j)), pl.BlockSpec((3, tn), lambda j: (0, j))],
        out_shape=[jax.ShapeDtypeStruct((3, s, d), BF16), jax.ShapeDtypeStruct((3, d), F32)],
        compiler_params=_params(("parallel",)),
    )(dm, wout, z, wdw)


def _pool_select(g, c2, c4, c8, c16):
    return jnp.where(g == 0, c2, jnp.where(g == 1, c4, jnp.where(g == 2, c8, c16)))


def _pool_inv_count(g, shape):
    pos = lax.broadcasted_iota(jnp.int32, shape, 0).astype(F32) + 1.0
    win = (2 << g).astype(F32)
    return jnp.minimum(pos, win)


def _pool_fwd(n, win, wgrp, scale, name):
    s, d = n.shape
    tn = POOL_GROUP_DIM

    def body(n_ref, wi_ref, wg_ref, sc_ref, p_ref, y_ref):
        g = pl.program_id(0)
        u = _dot(n_ref[...], wi_ref[...])
        s2 = u + _shift_down(u, 1)
        s4 = s2 + _shift_down(s2, 2)
        s8 = s4 + _shift_down(s4, 4)
        s16 = s8 + _shift_down(s8, 8)
        tot = _pool_select(g, s2, s4, s8, s16)
        p = (tot / _pool_inv_count(g, u.shape) - u).astype(BF16)
        p_ref[...] = p
        y_ref[...] = (_dot(p, wg_ref[...]) * sc_ref[...]).astype(BF16)

    return pl.pallas_call(
        body, name=name, grid=(d // tn,),
        in_specs=[pl.BlockSpec((s, d), lambda g: (0, 0)), pl.BlockSpec((d, tn), lambda g: (0, g)),
                  pl.BlockSpec((None, tn, tn), lambda g: (g, 0, 0)), pl.BlockSpec((1, tn), lambda g: (0, g))],
        out_specs=[pl.BlockSpec((s, tn), lambda g: (0, g)), pl.BlockSpec((s, tn), lambda g: (0, g))],
        out_shape=[jax.ShapeDtypeStruct((s, d), BF16), jax.ShapeDtypeStruct((s, d), BF16)],
        compiler_params=_params(("parallel",)),
    )(n, win, wgrp, scale)


def _pool_mid_bwd(dm, wout, p, wgrp, scale, name):
    s, d = dm.shape
    tn = POOL_GROUP_DIM

    def body(dm_ref, wo_ref, p_ref, wg_ref, sc_ref, du_ref, dwg_ref, dsc_ref):
        g = pl.program_id(0)
        dy = _dot(dm_ref[...], wo_ref[...], NT)
        pv = p_ref[...]
        wg = wg_ref[...]
        ypre = _dot(pv, wg)
        dsc_ref[...] = jnp.sum(dy * ypre, axis=0, keepdims=True)
        dypre = (dy * sc_ref[...]).astype(BF16)
        dwg_ref[...] = _dot(pv, dypre, TN)
        dp = _dot(dypre, wg, NT)
        e = dp / _pool_inv_count(g, dp.shape)
        f2 = e + _shift_up(e, 1)
        f4 = f2 + _shift_up(f2, 2)
        f8 = f4 + _shift_up(f4, 4)
        f16 = f8 + _shift_up(f8, 8)
        du_ref[...] = (_pool_select(g, f2, f4, f8, f16) - dp).astype(BF16)

    return pl.pallas_call(
        body, name=name, grid=(d // tn,),
        in_specs=[pl.BlockSpec((s, d), lambda g: (0, 0)), pl.BlockSpec((tn, d), lambda g: (g, 0)),
                  pl.BlockSpec((s, tn), lambda g: (0, g)), pl.BlockSpec((None, tn, tn), lambda g: (g, 0, 0)),
                  pl.BlockSpec((1, tn), lambda g: (0, g))],
        out_specs=[pl.BlockSpec((s, tn), lambda g: (0, g)), pl.BlockSpec((None, tn, tn), lambda g: (g, 0, 0)),
                   pl.BlockSpec((1, tn), lambda g: (0, g))],
        out_shape=[jax.ShapeDtypeStruct((s, d), BF16), jax.ShapeDtypeStruct((4, tn, tn), F32),
                   jax.ShapeDtypeStruct((1, d), F32)],
        compiler_params=_params(("parallel",)),
    )(dm, wout, p, wgrp, scale)


PANEL = LANES
ATTN_EXT = ATTN_WIDTH + PANEL
DVEC_LANE = HEADS


def _alibi_slopes(g, dil):
    all_slopes = 2.0 ** (-8.0 * np.arange(1, N_HEADS_A + 1) / N_HEADS_A)
    return [float(np.float32(sl) * np.float32(dil)) for sl in all_slopes[g * HEADS:(g + 1) * HEADS]]


def _residue_order(a, dil, name):
    s, w = a.shape
    per = ROW_TILE // dil
    panels = w // PANEL

    def body(a_ref, o_ref, *tiles):
        for c in range(panels):
            cols = slice(c * PANEL, (c + 1) * PANEL)
            tiles[c][...] = a_ref[:, cols].astype(F32)
            for r in range(dil):
                o_ref[r, :, cols] = tiles[c][pl.ds(r, per, stride=dil), :].astype(o_ref.dtype)

    out = pl.pallas_call(
        body, name=name, grid=(s // ROW_TILE,),
        in_specs=[pl.BlockSpec((ROW_TILE, w), lambda i: (i, 0))],
        out_specs=pl.BlockSpec((dil, per, w), lambda i: (0, i, 0)),
        out_shape=jax.ShapeDtypeStruct((dil, s // dil, w), a.dtype),
        scratch_shapes=[pltpu.VMEM((ROW_TILE, PANEL), F32)] * panels,
        compiler_params=_params(("parallel",)),
    )(a)
    return out.reshape(s, w)


def _token_order(a, dil, acc, name):
    s, w = a.shape
    per = ROW_TILE // dil
    panels = w // PANEL
    has_acc = acc is not None

    def body(*refs):
        a_ref = refs[0]
        o_ref = refs[2] if has_acc else refs[1]
        tiles = refs[3:] if has_acc else refs[2:]
        for c in range(panels):
            cols = slice(c * PANEL, (c + 1) * PANEL)
            for r in range(dil):
                tiles[c][pl.ds(r, per, stride=dil), :] = a_ref[r, :, cols]
            v = tiles[c][...]
            if has_acc:
                v = v + refs[1][:, cols]
            o_ref[:, cols] = v

    row = pl.BlockSpec((ROW_TILE, w), lambda i: (i, 0))
    return pl.pallas_call(
        body, name=name, grid=(s // ROW_TILE,),
        in_specs=[pl.BlockSpec((dil, per, w), lambda i: (0, i, 0))] + ([row] if has_acc else []),
        out_specs=row, out_shape=jax.ShapeDtypeStruct((s, w), F32),
        scratch_shapes=[pltpu.VMEM((ROW_TILE, PANEL), F32)] * panels,
        compiler_params=_params(("parallel",)),
    )(*([a.reshape(dil, s // dil, w)] + ([acc] if has_acc else [])))


def _qkv_proj(n, wqkv, g, name):
    s, d = n.shape
    tm = _pick(s, MATMUL_TILES)

    def body(a_ref, b_ref, o_ref):
        o_ref[...] = _dot(a_ref[...], b_ref[...]).astype(BF16)

    return pl.pallas_call(
        body, name=name, grid=(s // tm, 3),
        in_specs=[pl.BlockSpec((tm, d), lambda i, t: (i, 0)),
                  pl.BlockSpec((d, ATTN_WIDTH), lambda i, t: (0, 3 * g + t))],
        out_specs=pl.BlockSpec((None, tm, ATTN_WIDTH), lambda i, t: (t, i, 0)),
        out_shape=jax.ShapeDtypeStruct((3, s, ATTN_WIDTH), BF16),
        compiler_params=_params(("parallel", "parallel")),
    )(n, wqkv)


def _attn_window(n, ln):
    if ln == BLOCK:
        return 0, BLOCK
    return pl.multiple_of(jnp.maximum(n - 1, 0) * BLOCK, BLOCK), 2 * BLOCK


def _attn_mask(n, k0, kw):
    qpos = n * BLOCK + lax.broadcasted_iota(jnp.int32, (BLOCK, kw), 0)
    kpos = k0 + lax.broadcasted_iota(jnp.int32, (BLOCK, kw), 1)
    dist = qpos - kpos
    return dist.astype(F32), (dist >= 0) & (dist <= BLOCK)


def _attn_scores(q, keys, slope, dist, valid):
    s = _dot(q, keys, NT) * (HEAD_DIM ** -0.5) - slope * dist
    return jnp.where(valid, s, NEG_INF)


def _attn_fwd(qkv, g, name):
    _, s, w = qkv.shape
    dil = DILATED_CFG[g][1]
    ln = s // dil
    nb = ln // BLOCK
    slopes = _alibi_slopes(g, dil)

    def body(qkv_ref, o_ref):
        n = pl.program_id(1)
        k0, kw = _attn_window(n, ln)
        cur, win = pl.ds(pl.multiple_of(n * BLOCK, BLOCK), BLOCK), pl.ds(k0, kw)
        dist, valid = _attn_mask(n, k0, kw)
        o_ref[:, w:] = jnp.zeros((BLOCK, PANEL), F32)
        for h in range(HEADS):
            cols = slice(h * HEAD_DIM, (h + 1) * HEAD_DIM)
            sc = _attn_scores(qkv_ref[0, cur, cols], qkv_ref[1, win, cols], slopes[h], dist, valid)
            m = jnp.max(sc, axis=-1, keepdims=True)
            p = jnp.exp(sc - m)
            den = jnp.sum(p, axis=-1, keepdims=True)
            o_ref[:, cols] = _dot(p.astype(BF16), qkv_ref[2, win, cols]) / den
            o_ref[:, w + h:w + h + 1] = m + jnp.log(den)

    return pl.pallas_call(
        body, name=name, grid=(dil, nb),
        in_specs=[pl.BlockSpec((3, ln, w), lambda r, n: (0, r, 0))],
        out_specs=pl.BlockSpec((BLOCK, ATTN_EXT), lambda r, n: (r * nb + n, 0)),
        out_shape=jax.ShapeDtypeStruct((s, ATTN_EXT), F32),
        compiler_params=_params(("parallel", "parallel")),
    )(qkv)


def _attn_bwd(qkv, dext, g, name):
    _, s, w = qkv.shape
    dil = DILATED_CFG[g][1]
    ln = s // dil
    nb = ln // BLOCK
    slopes = _alibi_slopes(g, dil)
    scale = HEAD_DIM ** -0.5

    def body(qkv_ref, de_ref, d_ref, dk_ref, dv_ref):
        n = pl.program_id(1)

        @pl.when(n == 0)
        def _():
            dk_ref[...] = jnp.zeros_like(dk_ref)
            dv_ref[...] = jnp.zeros_like(dv_ref)

        k0, kw = _attn_window(n, ln)
        cur, win = pl.ds(pl.multiple_of(n * BLOCK, BLOCK), BLOCK), pl.ds(k0, kw)
        dist, valid = _attn_mask(n, k0, kw)
        for h in range(HEADS):
            cols = slice(h * HEAD_DIM, (h + 1) * HEAD_DIM)
            q, keys = qkv_ref[0, cur, cols], qkv_ref[1, win, cols]
            dob = de_ref[:, cols].astype(BF16)
            p = jnp.exp(_attn_scores(q, keys, slopes[h], dist, valid) - de_ref[:, w + h:w + h + 1])
            dd = de_ref[:, w + DVEC_LANE + h:w + DVEC_LANE + h + 1]
            ds = (p * (_dot(dob, qkv_ref[2, win, cols], NT) - dd)).astype(BF16)
            d_ref[0, cur, cols] = (scale * _dot(ds, keys)).astype(BF16)
            dv_ref[win, cols] += _dot(p.astype(BF16), dob, TN)
            dk_ref[win, cols] += scale * _dot(ds, q, TN)

        @pl.when(n == nb - 1)
        def _():
            d_ref[1] = dk_ref[...].astype(BF16)
            d_ref[2] = dv_ref[...].astype(BF16)

    return pl.pallas_call(
        body, name=name, grid=(dil, nb),
        in_specs=[pl.BlockSpec((3, ln, w), lambda r, n: (0, r, 0)),
                  pl.BlockSpec((BLOCK, ATTN_EXT), lambda r, n: (r * nb + n, 0))],
        out_specs=pl.BlockSpec((3, ln, w), lambda r, n: (0, r, 0)),
        out_shape=jax.ShapeDtypeStruct((3, s, w), BF16),
        scratch_shapes=[pltpu.VMEM((ln, w), F32), pltpu.VMEM((ln, w), F32)],
        compiler_params=_params(("parallel", "arbitrary")),
    )(qkv, dext)


def _attn_merge(e0, e1, e2, name):
    s = e0.shape[0]
    w = ATTN_WIDTH

    def body(e0_ref, e1_ref, e2_ref, m_ref, mb_ref, lse_ref):
        refs = (e0_ref, e1_ref, e2_ref)
        l = [r[:, w:w + HEADS] for r in refs]
        mx = jnp.maximum(jnp.maximum(l[0], l[1]), l[2])
        e = [jnp.exp(v - mx) for v in l]
        z = e[0] + e[1] + e[2]
        lse_ref[...] = mx + jnp.log(z)
        wts = [v / z for v in e]
        for h in range(HEADS):
            cols = slice(h * HEAD_DIM, (h + 1) * HEAD_DIM)
            acc = wts[0][:, h:h + 1] * refs[0][:, cols]
            for g in range(1, N_GROUPS_A):
                acc = acc + wts[g][:, h:h + 1] * refs[g][:, cols]
            m_ref[:, cols] = acc
            mb_ref[:, cols] = acc.astype(BF16)

    ext = pl.BlockSpec((ROW_TILE, ATTN_EXT), lambda i: (i, 0))
    row = pl.BlockSpec((ROW_TILE, w), lambda i: (i, 0))
    return pl.pallas_call(
        body, name=name, grid=(s // ROW_TILE,),
        in_specs=[ext, ext, ext],
        out_specs=[row, row, pl.BlockSpec((ROW_TILE, HEADS), lambda i: (i, 0))],
        out_shape=[jax.ShapeDtypeStruct((s, w), F32), jax.ShapeDtypeStruct((s, w), BF16),
                   jax.ShapeDtypeStruct((s, HEADS), F32)],
        compiler_params=_params(("parallel",)),
    )(e0, e1, e2)


def _attn_dvec(dmerged, merged, lse_all, name):
    s, w = merged.shape

    def body(dm_ref, m_ref, lse_ref, de_ref):
        dmv = dm_ref[...]
        de_ref[:, :w] = dmv
        de_ref[:, w:] = jnp.zeros((ROW_TILE, PANEL), F32)
        de_ref[:, w:w + HEADS] = lse_ref[...]
        prod = dmv * m_ref[...]
        for h in range(HEADS):
            lane = w + DVEC_LANE + h
            de_ref[:, lane:lane + 1] = jnp.sum(prod[:, h * HEAD_DIM:(h + 1) * HEAD_DIM], axis=-1, keepdims=True)

    row = pl.BlockSpec((ROW_TILE, w), lambda i: (i, 0))
    return pl.pallas_call(
        body, name=name, grid=(s // ROW_TILE,),
        in_specs=[row, row, pl.BlockSpec((ROW_TILE, HEADS), lambda i: (i, 0))],
        out_specs=pl.BlockSpec((ROW_TILE, ATTN_EXT), lambda i: (i, 0)),
        out_shape=jax.ShapeDtypeStruct((s, ATTN_EXT), F32),
        compiler_params=_params(("parallel",)),
    )(dmerged, merged, lse_all)


def _attention_fwd(n, wqkv, wo, tag):
    ns, qkvs, exts = [], [], []
    for g, (_, dil) in enumerate(DILATED_CFG):
        ng = n if dil == 1 else _residue_order(n, dil, f"{tag}_order_g{g}")
        qkv = _qkv_proj(ng, wqkv, g, f"{tag}_qkv_g{g}")
        ext = _attn_fwd(qkv, g, f"{tag}_fwd_g{g}")
        ns.append(ng)
        qkvs.append(qkv)
        exts.append(ext if dil == 1 else _token_order(ext, dil, None, f"{tag}_unorder_g{g}"))
    merged, merged_bf, lse_all = _attn_merge(*exts, f"{tag}_merge")
    m = _matmul(merged_bf, wo, "nn", F32, f"{tag}_wo")
    return m, (ns, qkvs, merged, merged_bf, lse_all)


def _attention_bwd(dm, wqkv, wo, saved, tag):
    ns, qkvs, merged, merged_bf, lse_all = saved
    d_wo = _matmul(merged_bf, dm, "tn", BF16, f"{tag}_dwo")
    dmerged = _matmul(dm, wo, "nt", F32, f"{tag}_dmerged")
    dext = _attn_dvec(dmerged, merged, lse_all, f"{tag}_dvec")
    width = 3 * ATTN_WIDTH
    d_wqkv, dn = [], None
    for g, (_, dil) in enumerate(DILATED_CFG):
        dext_g = dext if dil == 1 else _residue_order(dext, dil, f"{tag}_dorder_g{g}")
        dqkv = _attn_bwd(qkvs[g], dext_g, g, f"{tag}_bwd_g{g}")
        d_wqkv.append(_matmul(ns[g], dqkv, "tn", BF16, f"{tag}_dwqkv_g{g}", b_parts=3))
        dn_g = _matmul(dqkv, wqkv[:, g * width:(g + 1) * width], "nt", F32, f"{tag}_dn_g{g}", a_parts=3)
        dn = dn_g if dil == 1 else _token_order(dn_g, dil, dn, f"{tag}_dn_sum_g{g}")
    return dn, jnp.concatenate(d_wqkv, axis=1), d_wo


def _layer_matrices(i):
    mixer = (("attn_w_qkv", "attn_w_o"), ("conv_w_in", "conv_w_out"), ("pool_w_in", "pool_w_grp", "pool_w_out"))[i % 3]
    return [(k, i // 3) for k in mixer] + [("ffn_w_up", i), ("ffn_w_down", i)]


def _local_step(x, tgt, vec, weights):
    ng = vec["norm_g"]

    def gain(i, j, token=None):
        g = ng[i, j][None, :]
        return g if token is None else g + token

    saved = []
    n = _rms_fwd(x, gain(0, 0), None, BF16, "norm_first")
    for i in range(DEPTH):
        wl = weights.layer(i)
        t0 = weights.hook(i, 0, n)
        kind, idx = i % 3, i // 3
        if kind == 0:
            m, ms = _attention_fwd(n, wl["attn_w_qkv"], wl["attn_w_o"], "attn")
        elif kind == 1:
            taps = vec["conv_w_dw"][idx] if t0 is None else vec["conv_w_dw"][idx] + t0
            z, y = _sconv_fwd(n, wl["conv_w_in"], taps, "sconv_fwd")
            m = _matmul(y, wl["conv_w_out"], "nn", F32, "sconv_out")
            ms = (z, y)
        else:
            scale = vec["pool_scale"][idx][None, :] if t0 is None else vec["pool_scale"][idx][None, :] + t0
            p, y = _pool_fwd(n, wl["pool_w_in"], wl["pool_w_grp"], scale, "pool_fwd")
            m = _matmul(y, wl["pool_w_out"], "nn", F32, "pool_out")
            ms = (p, y)
        t1 = weights.hook(i, 1, m)
        x1, n2 = _rms_res_pre(m, gain(i, 1, t0), x, gain(i, 2, t1), "norm_res_pre")
        h, a = _ffn_up(n2, wl["ffn_w_up"], vec["ffn_w_dw"][i], "ffn_up")
        t2 = weights.hook(i, 2, a)
        f = _matmul(a, wl["ffn_w_down"].reshape(D_FF, D_MODEL), "nn", F32, "ffn_down", a_parts=FFN_PAIRS)
        saved.append((x, n, m, ms, x1, n2, h, a, f, wl))
        if i + 1 < DEPTH:
            x, n = _rms_res_pre(f, gain(i, 3, t2), x1, gain(i + 1, 0), "norm_res_pre")
        else:
            x = _rms_fwd(f, gain(i, 3), x1, F32, "norm_res")
        weights.hook(i, 3, x)

    loss, dx = _loss_head(x, tgt, "loss_head")

    g_norm = [[None] * 4 for _ in range(DEPTH)]
    grads = {k: [] for k in ("attn_w_qkv", "attn_w_o", "conv_w_in", "conv_w_dw", "conv_w_out", "pool_w_in",
                             "pool_w_grp", "pool_scale", "pool_w_out", "ffn_w_up", "ffn_w_dw", "ffn_w_down")}
    df, g_norm[DEPTH - 1][3] = _rms_bwd(saved[-1][8], gain(DEPTH - 1, 3), dx, None, BF16, "norm_bwd_sub")
    for i in reversed(range(DEPTH)):
        xin, n, m, ms, x1, n2, h, a, f, wl = saved[i]
        kind, idx = i % 3, i // 3
        d_wdown = _matmul(a, df, "tn", BF16, "ffn_dwdown", a_parts=FFN_PAIRS)
        grads["ffn_w_down"].append(d_wdown.reshape(N_DEV, D_FF // N_DEV, D_MODEL))
        dh, dwg, dwu = _ffn_mid_bwd(df, wl["ffn_w_down"].reshape(FFN_PAIRS, -1, D_MODEL), h, vec["ffn_w_dw"][i],
                                    "ffn_mid_bwd")
        grads["ffn_w_dw"].append(jnp.concatenate([dwg, dwu], axis=0))
        grads["ffn_w_up"].append(_ffn_dwup(n2, dh, "ffn_dwup"))
        dn2 = _ffn_dn(dh, wl["ffn_w_up"], "ffn_dn")
        dx1, dm, g_norm[i][2], g_norm[i][1] = _rms_bwd_pair(x1, gain(i, 2), dn2, dx, m, gain(i, 1), "norm_bwd_pair")
        if kind == 0:
            dn, d_wqkv, d_wo = _attention_bwd(dm, wl["attn_w_qkv"], wl["attn_w_o"], ms, "attn")
            grads["attn_w_qkv"].append(d_wqkv)
            grads["attn_w_o"].append(d_wo)
        elif kind == 1:
            z, y = ms
            grads["conv_w_out"].append(_matmul(y, dm, "tn", BF16, "sconv_dwout"))
            dz, ddw = _sconv_mid_bwd(dm, wl["conv_w_out"], z, vec["conv_w_dw"][idx], "sconv_mid_bwd")
            grads["conv_w_dw"].append(ddw)
            grads["conv_w_in"].append(_matmul(n, dz, "tn", BF16, "sconv_dwin", b_parts=3))
            dn = _matmul(dz, wl["conv_w_in"], "nt", F32, "sconv_dn", a_parts=3)
        else:
            p, y = ms
            grads["pool_w_out"].append(_matmul(y, dm, "tn", BF16, "pool_dwout"))
            du, dwgrp, dscale = _pool_mid_bwd(dm, wl["pool_w_out"], p, wl["pool_w_grp"],
                                              vec["pool_scale"][idx][None, :], "pool_mid_bwd")
            grads["pool_w_grp"].append(dwgrp)
            grads["pool_scale"].append(dscale[0])
            grads["pool_w_in"].append(_matmul(n, du, "tn", BF16, "pool_dwin"))
            dn = _matmul(du, wl["pool_w_in"], "nt", F32, "pool_dn")
        if i > 0:
            dx, df, g_norm[i][0], g_norm[i - 1][3] = _rms_bwd_pair(xin, gain(i, 0), dn, dx1, saved[i - 1][8],
                                                                   gain(i - 1, 3), "norm_bwd_pair")
        else:
            dx, g_norm[0][0] = _rms_bwd(xin, gain(0, 0), dn, dx1, F32, "norm_bwd_res")

    out = {k: v[::-1] if k.startswith("ffn_") else jnp.stack(v[::-1]) for k, v in grads.items()}
    out["norm_g"] = jnp.stack([jnp.concatenate(row, axis=0) for row in g_norm])
    return loss, dx, out


_AXES = ("x", "y", "c")
ROUTE_A = ("y", "x", "c")
ROUTE_B = ("x", "y", "c")
def _dev_index(pos):
    return 4 * pos["x"] + 2 * pos["y"] + pos["c"]


_HBM = pl.BlockSpec(memory_space=pltpu.HBM)
_SEM = pl.BlockSpec(memory_space=pltpu.SEMAPHORE)
_ANY = pl.BlockSpec(memory_space=pl.ANY)
_EFFECT = pltpu.SideEffectType.DATAFLOW_SIDE_EFFECTING


def _copies_start(describe, arrays, n_copies, name, after):
    n = len(arrays)
    deps = [] if after is None else [after]

    def body(*refs):
        send_sems, recv_sems = refs[n + len(deps)], refs[n + len(deps) + 1]
        for c in describe(refs[:n], send_sems, recv_sems):
            c.start()
        refs[-1][...] = jnp.zeros_like(refs[-1])

    outs = pl.pallas_call(
        body, name=f"{name}_start",
        out_shape=(pltpu.SemaphoreType.DMA((n_copies,)), pltpu.SemaphoreType.DMA((n_copies,)),
                   *[pltpu.HBM(a.shape, a.dtype) for a in arrays], jax.ShapeDtypeStruct((1, D_MODEL), F32)),
        in_specs=[_HBM] * n + [_ANY] * len(deps),
        out_specs=(_SEM, _SEM, *([_HBM] * n), pl.BlockSpec(memory_space=pltpu.VMEM)),
        input_output_aliases={i: 2 + i for i in range(n)},
        compiler_params=pltpu.CompilerParams(has_side_effects=_EFFECT),
    )(*[pltpu.with_memory_space_constraint(a, pltpu.HBM) for a in arrays], *deps)
    return (outs[0], outs[1], list(outs[2:2 + n])), outs[-1]


def _copies_wait(describe, handle, name, after):
    send_sems, recv_sems, arrays = handle
    n = len(arrays)
    deps = [] if after is None else [after]

    def body(*refs):
        for c in describe(refs[:n], refs[n], refs[n + 1]):
            c.wait_send()
            c.wait_recv()

    outs = pl.pallas_call(
        body, name=f"{name}_wait",
        out_shape=tuple(pltpu.HBM(a.shape, a.dtype) for a in arrays),
        in_specs=[_HBM] * n + [_SEM, _SEM] + [_ANY] * len(deps), out_specs=tuple([_HBM] * n),
        input_output_aliases={i: i for i in range(n)},
        compiler_params=pltpu.CompilerParams(has_side_effects=_EFFECT),
    )(*arrays, send_sems, recv_sems, *deps)
    return list(outs)


GATHER_STAGE_COPIES = (3, 3, 1)


def _gather_copies(stage, routes):
    n = len(routes)

    def describe(refs, send_sems, recv_sems):
        pos = {a: lax.axis_index(a) for a in _AXES}

        def flipped(axes):
            return {a: 1 - pos[a] if a in axes else pos[a] for a in _AXES}

        copies = []
        for i, (a1, a2, a3) in enumerate(routes):
            land = refs[n + i] if stage == 1 else refs[i]
            p1, p2, p12, p3 = flipped((a1,)), flipped((a2,)), flipped((a1, a2)), flipped((a3,))
            plan = {1: [(None, p1), (None, p2), (None, p3)], 2: [(p1, p2), (p1, p3), (p2, p3)], 3: [(p12, p3)]}[stage]
            for holder, to in plan:
                slot = land.at[_dev_index(pos if holder is None else holder)]
                k = len(copies)
                copies.append(pltpu.make_async_remote_copy(
                    src_ref=refs[i] if holder is None else slot, dst_ref=slot,
                    send_sem=send_sems.at[k], recv_sem=recv_sems.at[k],
                    device_id=tuple(to[a] for a in _AXES), device_id_type=pl.DeviceIdType.MESH))
        return copies

    return describe


def _gather_begin(shards, routes, name, after):
    n = len(shards)
    lands = [lax.empty((N_DEV,) + a.shape, a.dtype) for a in shards]
    handle, token = _copies_start(_gather_copies(1, routes), list(shards) + lands, GATHER_STAGE_COPIES[0] * n,
                                  f"{name}_1", after)
    return {"stage": 1, "handle": handle, "routes": routes, "name": name, "n": n}, token


def _gather_next(state, after):
    stage, routes, name, n = state["stage"], state["routes"], state["name"], state["n"]
    arrays = _copies_wait(_gather_copies(stage, routes), state["handle"], f"{name}_{stage}", after)
    if stage == 1:
        state = dict(state, shards=arrays[:n])
        arrays = arrays[n:]
    if stage == 3:
        me = _dev_index({a: lax.axis_index(a) for a in _AXES})
        return [lax.dynamic_update_index_in_dim(o, s, me, 0) for o, s in zip(arrays, state["shards"])], None
    handle, token = _copies_start(_gather_copies(stage + 1, routes), arrays, GATHER_STAGE_COPIES[stage] * n,
                                  f"{name}_{stage + 1}", None)
    return dict(state, stage=stage + 1, handle=handle), token


def _exchange_half(arrays, axes, name):
    n = len(arrays)

    def body(*refs):
        srcs, outs = refs[:n], refs[n:2 * n]
        send_sems, recv_sems = refs[2 * n:]
        pos = {a: lax.axis_index(a) for a in _AXES}
        copies = []
        for i, axis in enumerate(axes):
            peer = tuple(1 - pos[a] if a == axis else pos[a] for a in _AXES)
            rc = pltpu.make_async_remote_copy(src_ref=srcs[i].at[:, 1 - pos[axis]], dst_ref=outs[i],
                                              send_sem=send_sems.at[i], recv_sem=recv_sems.at[i], device_id=peer,
                                              device_id_type=pl.DeviceIdType.MESH)
            rc.start()
            copies.append(rc)
        for rc in copies:
            rc.wait_send()
            rc.wait_recv()

    hbm = pl.BlockSpec(memory_space=pl.ANY)
    return pl.pallas_call(
        body, name=name, in_specs=[hbm] * n, out_specs=[hbm] * n,
        out_shape=[jax.ShapeDtypeStruct((a.shape[0],) + a.shape[2:], a.dtype) for a in arrays],
        scratch_shapes=[pltpu.SemaphoreType.DMA((n,)), pltpu.SemaphoreType.DMA((n,))],
    )(*arrays)


ADD_ROW_TILES = (1024, 704, 512, 352, 256, 128, 96, 64, 32, 16)


def _add_half(a, recv, me, out_dtype, name):
    p, q, cols = recv.shape
    tr = _pick(q, ADD_ROW_TILES)

    def body(me_ref, a_ref, b_ref, o_ref):
        o_ref[...] = (a_ref[...].astype(F32) + b_ref[...].astype(F32)).astype(o_ref.dtype)

    return pl.pallas_call(
        body, name=name,
        grid_spec=pltpu.PrefetchScalarGridSpec(
            num_scalar_prefetch=1, grid=(p, q // tr),
            in_specs=[pl.BlockSpec((None, None, tr, cols), lambda j, i, m: (j, m[0], i, 0)),
                      pl.BlockSpec((None, tr, cols), lambda j, i, m: (j, i, 0))],
            out_specs=pl.BlockSpec((None, tr, cols), lambda j, i, m: (j, i, 0))),
        out_shape=jax.ShapeDtypeStruct((p, q, cols), out_dtype),
        compiler_params=_params(("parallel", "parallel")),
    )(me, a, recv)


def _reduce_scatter(slots, routes, tags):
    coord = {a: lax.axis_index(a).astype(jnp.int32).reshape(1) for a in _AXES}
    shapes = [a.shape[1:] for a in slots]
    rows = [math.prod(s[:-1]) for s in shapes]
    arrays = [a.reshape(4, 2, n, s[-1]) for a, n, s in zip(slots, rows, shapes)]
    for stage in range(3):
        axes = [r[2 - stage] for r in routes]
        recv = _exchange_half(arrays, axes, f"scatter_{stage + 1}")
        dt = F32 if stage == 2 else BF16
        arrays = [_add_half(a, r, coord[ax], dt, f"scatter_add_{stage + 1}_{t}")
                  for a, r, ax, t in zip(arrays, recv, axes, tags)]
        if stage == 0:
            arrays = [a.reshape((1, 2, 2 * n, s[-1]) if r[1] == "x" else (2, 2, n, s[-1]))
                      for a, n, s, r in zip(arrays, rows, shapes, routes)]
        elif stage == 1:
            arrays = [a.reshape(1, 2, n, s[-1]) for a, n, s in zip(arrays, rows, shapes)]
    return [a.reshape(s) for a, s in zip(arrays, shapes)]


_WEIGHTS = {
    "norm_g": ((DEPTH, 4, D_MODEL), 2, True),
    "attn_w_qkv": ((2, D_MODEL, 4608), 2, False),
    "attn_w_o": ((2, ATTN_WIDTH, D_MODEL), 2, False),
    "conv_w_in": ((1, D_MODEL, 3 * D_MODEL), 2, False),
    "conv_w_dw": ((1, 3, D_MODEL), 2, True),
    "conv_w_out": ((1, D_MODEL, D_MODEL), 1, False),
    "pool_w_in": ((1, D_MODEL, D_MODEL), 1, False),
    "pool_w_grp": ((1, 4, POOL_GROUP_DIM, POOL_GROUP_DIM), 2, False),
    "pool_scale": ((1, D_MODEL), 1, True),
    "pool_w_out": ((1, D_MODEL, D_MODEL), 1, False),
    "ffn_w_up": ((DEPTH, D_MODEL, 2 * D_FF), 2, False),
    "ffn_w_dw": ((DEPTH, 3, 2 * D_FF), 2, True),
    "ffn_w_down": ((DEPTH, D_FF, D_MODEL), 1, False),
}
_NAMES = tuple(_WEIGHTS)
_VECTORS = tuple(k for k in _NAMES if _WEIGHTS[k][2])
_MATRICES = tuple(k for k in _NAMES if not _WEIGHTS[k][2])
_FFN = ("ffn_w_up", "ffn_w_down")
_ON_ROUTE_A = ("ffn_w_up", "attn_w_o", "conv_w_out", "pool_w_in")
PACK_ROWS = 16


def _route(name):
    return ROUTE_A if name in _ON_ROUTE_A else ROUTE_B


def _shard_shape(name):
    shape, ax, _ = _WEIGHTS[name]
    return tuple(s // N_DEV if i == ax else s for i, s in enumerate(shape))


def _full_from_slots(slots, name, layers=None):
    shape, ax, _ = _WEIGHTS[name]
    if layers is not None:
        shape = (layers,) + shape[1:]
    return jnp.moveaxis(slots, 0, ax).reshape(shape)


def _slots_from_full(full, name):
    shape, ax, _ = _WEIGHTS[name]
    split = shape[:ax] + (N_DEV, shape[ax] // N_DEV) + shape[ax + 1:]
    return jnp.moveaxis(full.reshape(split), ax, 0)


def _pack_vectors(parts, lead):
    rows = []
    for k in _VECTORS:
        r = parts[k].reshape(lead + (-1, LANES))
        pad = -r.shape[-2] % PACK_ROWS
        rows.append(jnp.pad(r, [(0, 0)] * len(lead) + [(0, pad), (0, 0)]))
    return jnp.concatenate(rows, axis=len(lead))


def _unpack_vectors(buf, lead):
    out, r0 = {}, 0
    for k in _VECTORS:
        shard = _shard_shape(k)
        rows = math.prod(shard) // LANES
        out[k] = buf[..., r0:r0 + rows, :].reshape(lead + shard)
        r0 += rows + (-rows % PACK_ROWS)
    return out


def _comm_items():
    items = []
    for k in _MATRICES:
        items += [(k, l) for l in range(DEPTH)] if k in _FFN else [(k, None)]
    return items + [("vectors", None)]


class _LayerWeights:
    def __init__(self, shards):
        self.cast = {k: shards[k].astype(BF16) for k in _MATRICES}
        items = _layer_matrices(0)
        state, _ = _gather_begin(self._send(items) + [_pack_vectors(shards, ())],
                                 [_route(k) for k, _ in items] + [ROUTE_B], "gather0", None)
        for _ in range(2):
            state, _ = _gather_next(state, None)
        outs, _ = _gather_next(state, None)
        vec = _unpack_vectors(outs[-1], (N_DEV,))
        self.vec = {k: _full_from_slots(vec[k], k) for k in _VECTORS}
        self.vec["ffn_w_dw"] = [vec["ffn_w_dw"][:, l] for l in range(DEPTH)]
        self.ready = {0: self._unpack(items, outs[:-1])}
        self.state = None

    def _send(self, items):
        return [self.cast[k][j] for k, j in items]

    @staticmethod
    def _unpack(items, outs):
        return {k: o if k in _FFN else _full_from_slots(o[:, None], k, layers=1)[0] for (k, _), o in zip(items, outs)}

    def layer(self, i):
        return self.ready[i]

    def hook(self, i, point, after):
        if i + 1 >= DEPTH:
            return None
        items = _layer_matrices(i + 1)
        if point == 0:
            self.state, token = _gather_begin(self._send(items), [_route(k) for k, _ in items], f"gather{i + 1}", after)
        else:
            self.state, token = _gather_next(self.state, after)
            if point == 3:
                self.ready[i + 1] = self._unpack(items, self.state)
        return token


def _reduce_scatter_grads(grads):
    items = _comm_items()
    vec_slots = {k: _slots_from_full(grads[k], k) for k in _VECTORS if k != "ffn_w_dw"}
    vec_slots["ffn_w_dw"] = jnp.stack(grads["ffn_w_dw"], axis=1)
    slots = []
    for k, l in items:
        if k == "vectors":
            slots.append(_pack_vectors(vec_slots, (N_DEV,)).astype(BF16))
        elif k in _FFN:
            slots.append(grads[k][l])
        else:
            slots.append(_slots_from_full(grads[k], k).astype(BF16))
    sums = _reduce_scatter(slots, [_route(k) for k, _ in items], [k if l is None else f"{k}{l}" for k, l in items])
    out = {k: [None] * DEPTH for k in _FFN}
    for (k, l), s in zip(items, sums):
        if k in _FFN:
            out[k][l] = s
        elif k == "vectors":
            out.update(_unpack_vectors(s, ()))
        else:
            out[k] = s
    out.update({k: jnp.stack(out[k]) for k in _FFN})
    return out


def _adamw(w, g, m, v, name):
    shape = w.shape
    cols = shape[-1]
    view = shape if len(shape) == 3 else (1, math.prod(shape[:-1]), cols)
    layers, rows, _ = view
    tr = _pick(rows, (512, 256, 128, 64, 32, 16, 8))

    def body(w_ref, g_ref, m_ref, v_ref, d_ref, nm_ref, nv_ref):
        gv = g_ref[...]
        nm = ADAM_B1 * m_ref[...] + (1.0 - ADAM_B1) * gv
        nv = ADAM_B2 * v_ref[...] + (1.0 - ADAM_B2) * jnp.square(gv)
        m_hat = nm / (1.0 - ADAM_B1 ** ADAM_STEP)
        v_hat = nv / (1.0 - ADAM_B2 ** ADAM_STEP)
        d_ref[...] = -ADAM_LR * (m_hat / (jnp.sqrt(v_hat) + ADAM_EPS) + ADAM_WD * w_ref[...])
        nm_ref[...] = nm
        nv_ref[...] = nv

    blk = pl.BlockSpec((None, tr, cols), lambda l, i: (l, i, 0))
    shp = jax.ShapeDtypeStruct(view, F32)
    outs = pl.pallas_call(
        body, name=name, grid=(layers, rows // tr), in_specs=[blk] * 4, out_specs=[blk] * 3, out_shape=[shp] * 3,
        compiler_params=_params(("parallel", "parallel")),
    )(*[t.reshape(view) for t in (w, g, m, v)])
    return [o.reshape(shape) for o in outs]


def kernel(x, norm_g, attn_w_qkv, attn_w_o, conv_w_in, conv_w_dw, conv_w_out, pool_w_in, pool_w_grp, pool_scale, pool_w_out, ffn_w_up, ffn_w_dw, ffn_w_down, loss_target, m_norm_g, m_attn_w_qkv, m_attn_w_o, m_conv_w_in, m_conv_w_dw, m_conv_w_out, m_pool_w_in, m_pool_w_grp, m_pool_scale, m_pool_w_out, m_ffn_w_up, m_ffn_w_dw, m_ffn_w_down, v_norm_g, v_attn_w_qkv, v_attn_w_o, v_conv_w_in, v_conv_w_dw, v_conv_w_out, v_pool_w_in, v_pool_w_grp, v_pool_scale, v_pool_w_out, v_ffn_w_up, v_ffn_w_dw, v_ffn_w_down):
    shards = dict(zip(_NAMES, (norm_g, attn_w_qkv, attn_w_o, conv_w_in, conv_w_dw, conv_w_out, pool_w_in,
                               pool_w_grp, pool_scale, pool_w_out, ffn_w_up, ffn_w_dw, ffn_w_down)))
    moms = dict(zip(_NAMES, (m_norm_g, m_attn_w_qkv, m_attn_w_o, m_conv_w_in, m_conv_w_dw, m_conv_w_out,
                             m_pool_w_in, m_pool_w_grp, m_pool_scale, m_pool_w_out, m_ffn_w_up, m_ffn_w_dw,
                             m_ffn_w_down)))
    vels = dict(zip(_NAMES, (v_norm_g, v_attn_w_qkv, v_attn_w_o, v_conv_w_in, v_conv_w_dw, v_conv_w_out,
                             v_pool_w_in, v_pool_w_grp, v_pool_scale, v_pool_w_out, v_ffn_w_up, v_ffn_w_dw,
                             v_ffn_w_down)))
    weights = _LayerWeights(shards)
    loss, grad_x, grads = _local_step(x[0], loss_target[0], weights.vec, weights)
    loss = lax.psum(loss[0, 0], _AXES)
    gsh = _reduce_scatter_grads(grads)
    deltas, new_m, new_v = [], [], []
    for k in _NAMES:
        d, nm, nv = _adamw(shards[k], gsh[k], moms[k], vels[k], f"adamw_{k}")
        deltas.append(d)
        new_m.append(nm)
        new_v.append(nv)
    return (loss, grad_x[None], *[gsh[k] for k in _NAMES], *deltas, *new_m, *new_v)
```

```python
import functools
import math

import numpy as np
import jax
import jax.numpy as jnp
from jax import lax
from jax.experimental import pallas as pl
from jax.experimental.pallas import tpu as pltpu

F32, BF16 = jnp.float32, jnp.bfloat16

D_MODEL = 1024
SEQ = 2048
DEPTH = 4
DILATED_CFG = ((128, 1), (512, 4), (2048, 16))
N_GROUPS_A = 3
HEADS = 8
HEAD_DIM = 64
ATTN_WIDTH = HEADS * HEAD_DIM
N_HEADS_A = N_GROUPS_A * HEADS
BLOCK = 128
NEG_INF = -1e30
POOL_GROUP_DIM = 256
D_FF = 2816
RMS_EPS = 1e-6
ADAM_LR, ADAM_B1, ADAM_B2, ADAM_EPS, ADAM_WD, ADAM_STEP = 0.001, 0.9, 0.999, 1e-08, 0.01, 10

N_DEV = 8
LANES = 128
V7X_VMEM_BYTES = 64 * 2 ** 20
VMEM_LIMIT_BYTES = V7X_VMEM_BYTES - 8 * 2 ** 20
COL_TILE = 256
ROW_TILE = 256
MATMUL_TILES = (1024, 1408, 512, 256, 128)
TN_RESIDENT_K = 2048

NN = (((1,), (0,)), ((), ()))
NT = (((1,), (1,)), ((), ()))
TN = (((0,), (0,)), ((), ()))


def _dot(a, b, dims=NN):
    return lax.dot_general(a, b, dims, preferred_element_type=F32)


def _params(sem=None):
    return pltpu.CompilerParams(dimension_semantics=sem, vmem_limit_bytes=VMEM_LIMIT_BYTES)


def _pick(n, prefs):
    for p in prefs:
        if n % p == 0:
            return p
    return n


def _matmul(a, b, mode, out_dtype, name, a_parts=1, b_parts=1):
    if mode == "nn":
        m, k = a.shape[-2], a.shape[-1] * a_parts
        n = b.shape[-1] * b_parts
    elif mode == "nt":
        m, k = a.shape[-2], a.shape[-1] * a_parts
        n = b.shape[-2]
    else:
        k, m = a.shape[-2], a.shape[-1] * a_parts
        n = b.shape[-1] * b_parts
    tm = _pick(m, MATMUL_TILES)
    tn = _pick(n // b_parts if mode != "nt" else n, MATMUL_TILES)
    kk = k // a_parts if mode != "tn" else k
    tk = _pick(kk, MATMUL_TILES)
    if mode == "tn":
        tm = _pick(m // a_parts, MATMUL_TILES)
        if k <= TN_RESIDENT_K:
            tk = k
    gm, gn, gk = m // tm, n // tn, k // tk

    def a_idx(i, j, kq):
        if mode == "tn":
            r, c, per = kq, i, (m // a_parts) // tm
        else:
            r, c, per = i, kq, (k // a_parts) // tk
        return (r, c) if a_parts == 1 else (c // per, r, c % per)

    def b_idx(i, j, kq):
        if mode == "nt":
            return (j, kq)
        per = (n // b_parts) // tn
        return (kq, j) if b_parts == 1 else (j // per, kq, j % per)

    a_blk = (tk, tm) if mode == "tn" else (tm, tk)
    b_blk = (tn, tk) if mode == "nt" else (tk, tn)
    if a_parts > 1:
        a_blk = (None,) + a_blk
    if b_parts > 1:
        b_blk = (None,) + b_blk
    dims = {"nn": NN, "nt": NT, "tn": TN}[mode]

    def body_single(a_ref, b_ref, o_ref):
        o_ref[...] = _dot(a_ref[...], b_ref[...], dims).astype(o_ref.dtype)

    def body(a_ref, b_ref, o_ref, acc_ref):
        kq = pl.program_id(2)

        @pl.when(kq == 0)
        def _():
            acc_ref[...] = jnp.zeros_like(acc_ref)

        acc_ref[...] += _dot(a_ref[...], b_ref[...], dims)

        @pl.when(kq == gk - 1)
        def _():
            o_ref[...] = acc_ref[...].astype(o_ref.dtype)

    return pl.pallas_call(
        body_single if gk == 1 else body, name=name, grid=(gm, gn, gk),
        in_specs=[pl.BlockSpec(a_blk, a_idx), pl.BlockSpec(b_blk, b_idx)],
        out_specs=pl.BlockSpec((tm, tn), lambda i, j, kq: (i, j)),
        out_shape=jax.ShapeDtypeStruct((m, n), out_dtype),
        scratch_shapes=[] if gk == 1 else [pltpu.VMEM((tm, tn), F32)],
        compiler_params=_params(("parallel", "parallel", "arbitrary")),
    )(a, b)


def _rms_fwd(xin, g, res, out_dtype, name):
    s, d = xin.shape
    has_res = res is not None

    def body(*refs):
        x_ref, g_ref = refs[0], refs[1]
        o_ref = refs[-1]
        x = x_ref[...]
        r = lax.rsqrt(jnp.mean(x * x, axis=-1, keepdims=True) + RMS_EPS)
        y = x * r * g_ref[...]
        if has_res:
            y = refs[2][...] + y
        o_ref[...] = y.astype(o_ref.dtype)

    row = pl.BlockSpec((ROW_TILE, d), lambda i: (i, 0))
    vec = pl.BlockSpec((1, d), lambda i: (0, 0))
    ins = [xin, g] + ([res] if has_res else [])
    return pl.pallas_call(
        body, name=name, grid=(s // ROW_TILE,),
        in_specs=[row, vec] + ([row] if has_res else []),
        out_specs=row, out_shape=jax.ShapeDtypeStruct((s, d), out_dtype),
        compiler_params=_params(("parallel",)),
    )(*ins)


def _rms_bwd(xin, g, dy, dres, out_dtype, name):
    s, d = xin.shape
    has_res = dres is not None

    def body(*refs):
        x_ref, g_ref, dy_ref = refs[0], refs[1], refs[2]
        dx_ref, dg_ref = refs[-2], refs[-1]

        @pl.when(pl.program_id(0) == 0)
        def _():
            dg_ref[...] = jnp.zeros_like(dg_ref)

        x = x_ref[...]
        dyv = dy_ref[...].astype(F32)
        r = lax.rsqrt(jnp.mean(x * x, axis=-1, keepdims=True) + RMS_EPS)
        xhat = x * r
        u = dyv * g_ref[...]
        dx = r * (u - xhat * jnp.mean(u * xhat, axis=-1, keepdims=True))
        if has_res:
            dx = refs[3][...] + dx
        dx_ref[...] = dx.astype(dx_ref.dtype)
        dg_ref[...] += jnp.sum(dyv * xhat, axis=0, keepdims=True)

    row = pl.BlockSpec((ROW_TILE, d), lambda i: (i, 0))
    vec = pl.BlockSpec((1, d), lambda i: (0, 0))
    ins = [xin, g, dy] + ([dres] if has_res else [])
    return pl.pallas_call(
        body, name=name, grid=(s // ROW_TILE,),
        in_specs=[row, vec, row] + ([row] if has_res else []),
        out_specs=[row, vec],
        out_shape=[jax.ShapeDtypeStruct((s, d), out_dtype), jax.ShapeDtypeStruct((1, d), F32)],
        compiler_params=_params(("arbitrary",)),
    )(*ins)


def _rms(x):
    r = lax.rsqrt(jnp.mean(x * x, axis=-1, keepdims=True) + RMS_EPS)
    return r, x * r


def _rms_grad(r, xhat, dy, g):
    u = dy * g
    return r * (u - xhat * jnp.mean(u * xhat, axis=-1, keepdims=True))


def _rms_res_pre(sub, g_post, res, g_pre, name):
    s, d = sub.shape

    def body(sub_ref, gp_ref, res_ref, gn_ref, x_ref, n_ref):
        xnew = res_ref[...] + _rms(sub_ref[...])[1] * gp_ref[...]
        x_ref[...] = xnew
        n_ref[...] = (_rms(xnew)[1] * gn_ref[...]).astype(BF16)

    row = pl.BlockSpec((ROW_TILE, d), lambda i: (i, 0))
    vec = pl.BlockSpec((1, d), lambda i: (0, 0))
    return pl.pallas_call(
        body, name=name, grid=(s // ROW_TILE,),
        in_specs=[row, vec, row, vec], out_specs=[row, row],
        out_shape=[jax.ShapeDtypeStruct((s, d), F32), jax.ShapeDtypeStruct((s, d), BF16)],
        compiler_params=_params(("parallel",)),
    )(sub, g_post, res, g_pre)


def _rms_bwd_pair(xmid, g_pre, dn, dres, sub, g_post, name):
    s, d = xmid.shape

    def body(x_ref, gn_ref, dn_ref, dres_ref, sub_ref, gp_ref, dx_ref, dsub_ref, dgn_ref, dgp_ref):
        @pl.when(pl.program_id(0) == 0)
        def _():
            dgn_ref[...] = jnp.zeros_like(dgn_ref)
            dgp_ref[...] = jnp.zeros_like(dgp_ref)

        dnv = dn_ref[...].astype(F32)
        r, xhat = _rms(x_ref[...])
        dx = dres_ref[...] + _rms_grad(r, xhat, dnv, gn_ref[...])
        dx_ref[...] = dx
        dgn_ref[...] += jnp.sum(dnv * xhat, axis=0, keepdims=True)
        rs, shat = _rms(sub_ref[...])
        dsub_ref[...] = _rms_grad(rs, shat, dx, gp_ref[...]).astype(BF16)
        dgp_ref[...] += jnp.sum(dx * shat, axis=0, keepdims=True)

    row = pl.BlockSpec((ROW_TILE, d), lambda i: (i, 0))
    vec = pl.BlockSpec((1, d), lambda i: (0, 0))
    return pl.pallas_call(
        body, name=name, grid=(s // ROW_TILE,),
        in_specs=[row, vec, row, row, row, vec], out_specs=[row, row, vec, vec],
        out_shape=[jax.ShapeDtypeStruct((s, d), F32), jax.ShapeDtypeStruct((s, d), BF16),
                   jax.ShapeDtypeStruct((1, d), F32), jax.ShapeDtypeStruct((1, d), F32)],
        compiler_params=_params(("arbitrary",)),
    )(xmid, g_pre, dn, dres, sub, g_post)


def _loss_head(y, tgt, name):
    s, d = y.shape

    def body(y_ref, t_ref, l_ref, dy_ref):
        @pl.when(pl.program_id(0) == 0)
        def _():
            l_ref[...] = jnp.zeros_like(l_ref)

        e = y_ref[...] - t_ref[...]
        dy_ref[...] = e / d
        per_tok = jnp.mean(e * e, axis=-1, keepdims=True)
        l_ref[...] += 0.5 * jnp.sum(per_tok, axis=0, keepdims=True)

    row = pl.BlockSpec((ROW_TILE, d), lambda i: (i, 0))
    return pl.pallas_call(
        body, name=name, grid=(s // ROW_TILE,),
        in_specs=[row, row],
        out_specs=[pl.BlockSpec((1, 1), lambda i: (0, 0)), row],
        out_shape=[jax.ShapeDtypeStruct((1, 1), F32), jax.ShapeDtypeStruct((s, d), F32)],
        compiler_params=_params(("arbitrary",)),
    )(y, tgt)


def _shift_down(x, k):
    rows = lax.broadcasted_iota(jnp.int32, x.shape, 0)
    return jnp.where(rows >= k, pltpu.roll(x, k, axis=0), 0.0)


def _shift_up(x, k):
    t = x.shape[0]
    rows = lax.broadcasted_iota(jnp.int32, x.shape, 0)
    return jnp.where(rows < t - k, pltpu.roll(x, t - k, axis=0), 0.0)


def _conv3(h, w):
    return w[2:3] * h + w[1:2] * _shift_down(h, 1) + w[0:1] * _shift_down(h, 2)


def _conv3_bwd_x(dc, w):
    return w[2:3] * dc + w[1:2] * _shift_up(dc, 1) + w[0:1] * _shift_up(dc, 2)


def _conv3_bwd_w(dc, h, dw_ref, cols=slice(None)):
    dw_ref[0:1, cols] = jnp.sum(dc * _shift_down(h, 2), axis=0, keepdims=True)
    dw_ref[1:2, cols] = jnp.sum(dc * _shift_down(h, 1), axis=0, keepdims=True)
    dw_ref[2:3, cols] = jnp.sum(dc * h, axis=0, keepdims=True)


FFN_PAIRS = N_DEV // 2


def _lane_chunks(width):
    return [(c0, min(COL_TILE, width - c0)) for c0 in range(0, width, COL_TILE)]


def _ffn_up(n, wup, wdw, name):
    s, d = n.shape
    cw = wup.shape[-1]

    def body(n_ref, wg_ref, wu_ref, dg_ref, du_ref, h_ref, a_ref):
        x = n_ref[...]
        for c0, size in _lane_chunks(cw):
            cols = slice(c0, c0 + size)
            hg = _dot(x, wg_ref[:, cols])
            hu = _dot(x, wu_ref[:, cols])
            h_ref[0, :, cols] = hg.astype(BF16)
            h_ref[1, :, cols] = hu.astype(BF16)
            cg = _conv3(hg, dg_ref[:, cols])
            cu = _conv3(hu, du_ref[:, cols])
            a_ref[:, cols] = (cg * jax.nn.sigmoid(cg) * cu).astype(BF16)

    return pl.pallas_call(
        body, name=name, grid=(FFN_PAIRS,),
        in_specs=[pl.BlockSpec((s, d), lambda j: (0, 0)),
                  pl.BlockSpec((None, d, cw), lambda j: (j, 0, 0)),
                  pl.BlockSpec((None, d, cw), lambda j: (j + FFN_PAIRS, 0, 0)),
                  pl.BlockSpec((None, 3, cw), lambda j: (j, 0, 0)),
                  pl.BlockSpec((None, 3, cw), lambda j: (j + FFN_PAIRS, 0, 0))],
        out_specs=[pl.BlockSpec((None, 2, s, cw), lambda j: (j, 0, 0, 0)),
                   pl.BlockSpec((None, s, cw), lambda j: (j, 0, 0))],
        out_shape=[jax.ShapeDtypeStruct((FFN_PAIRS, 2, s, cw), BF16), jax.ShapeDtypeStruct((FFN_PAIRS, s, cw), BF16)],
        compiler_params=_params(("parallel",)),
    )(n, wup, wup, wdw, wdw)


def _ffn_mid_bwd(do, wdown, h, wdw, name):
    s, d = do.shape
    cw = wdown.shape[1]

    def body(do_ref, wd_ref, h_ref, wg_ref, wu_ref, dh_ref, dwg_ref, dwu_ref):
        dov = do_ref[...]
        for c0, size in _lane_chunks(cw):
            cols = slice(c0, c0 + size)
            da = _dot(dov, wd_ref[cols, :], NT)
            hg = h_ref[0, :, cols].astype(F32)
            hu = h_ref[1, :, cols].astype(F32)
            wg, wu = wg_ref[:, cols], wu_ref[:, cols]
            cg = _conv3(hg, wg)
            cu = _conv3(hu, wu)
            sg = jax.nn.sigmoid(cg)
            dcu = da * (cg * sg)
            dcg = da * cu * (sg * (1.0 + cg * (1.0 - sg)))
            dh_ref[0, :, cols] = _conv3_bwd_x(dcg, wg).astype(BF16)
            dh_ref[1, :, cols] = _conv3_bwd_x(dcu, wu).astype(BF16)
            _conv3_bwd_w(dcg, hg, dwg_ref, cols)
            _conv3_bwd_w(dcu, hu, dwu_ref, cols)

    vec = jax.ShapeDtypeStruct((FFN_PAIRS, 3, cw), F32)
    return pl.pallas_call(
        body, name=name, grid=(FFN_PAIRS,),
        in_specs=[pl.BlockSpec((s, d), lambda j: (0, 0)), pl.BlockSpec((None, cw, d), lambda j: (j, 0, 0)),
                  pl.BlockSpec((None, 2, s, cw), lambda j: (j, 0, 0, 0)),
                  pl.BlockSpec((None, 3, cw), lambda j: (j, 0, 0)),
                  pl.BlockSpec((None, 3, cw), lambda j: (j + FFN_PAIRS, 0, 0))],
        out_specs=[pl.BlockSpec((None, 2, s, cw), lambda j: (j, 0, 0, 0)),
                   pl.BlockSpec((None, 3, cw), lambda j: (j, 0, 0)), pl.BlockSpec((None, 3, cw), lambda j: (j, 0, 0))],
        out_shape=[jax.ShapeDtypeStruct((FFN_PAIRS, 2, s, cw), BF16), vec, vec],
        compiler_params=_params(("parallel",)),
    )(do, wdown, h, wdw, wdw)


def _ffn_dwup(n, dh, name):
    s, d = n.shape
    cw = dh.shape[-1]

    def body(n_ref, dh_ref, o_ref):
        o_ref[...] = _dot(n_ref[...], dh_ref[...], TN).astype(BF16)

    return pl.pallas_call(
        body, name=name, grid=(N_DEV,),
        in_specs=[pl.BlockSpec((s, d), lambda k: (0, 0)),
                  pl.BlockSpec((None, None, s, cw), lambda k: (k % FFN_PAIRS, k // FFN_PAIRS, 0, 0))],
        out_specs=pl.BlockSpec((None, d, cw), lambda k: (k, 0, 0)),
        out_shape=jax.ShapeDtypeStruct((N_DEV, d, cw), BF16),
        compiler_params=_params(("parallel",)),
    )(n, dh)


def _ffn_dn(dh, wup, name):
    s, cw = dh.shape[-2:]
    d = wup.shape[1]
    tm = _pick(s, MATMUL_TILES)

    def body(dh_ref, w_ref, o_ref, acc_ref):
        k = pl.program_id(1)

        @pl.when(k == 0)
        def _():
            acc_ref[...] = jnp.zeros_like(acc_ref)

        acc_ref[...] += _dot(dh_ref[...], w_ref[...], NT)

        @pl.when(k == N_DEV - 1)
        def _():
            o_ref[...] = acc_ref[...]

    return pl.pallas_call(
        body, name=name, grid=(s // tm, N_DEV),
        in_specs=[pl.BlockSpec((None, None, tm, cw), lambda i, k: (k % FFN_PAIRS, k // FFN_PAIRS, i, 0)),
                  pl.BlockSpec((None, d, cw), lambda i, k: (k, 0, 0))],
        out_specs=pl.BlockSpec((tm, d), lambda i, k: (i, 0)),
        out_shape=jax.ShapeDtypeStruct((s, d), F32),
        scratch_shapes=[pltpu.VMEM((tm, d), F32)],
        compiler_params=_params(("parallel", "arbitrary")),
    )(dh, wup)


def _sconv_fwd(n, win, wdw, name):
    s, d = n.shape
    tn = COL_TILE
    nj = d // tn

    def body(n_ref, wb_ref, wc_ref, wh_ref, dw_ref, z_ref, y_ref):
        x = n_ref[...]
        zb = _dot(x, wb_ref[...])
        zc = _dot(x, wc_ref[...])
        zh = _dot(x, wh_ref[...])
        z_ref[0] = zb.astype(BF16)
        z_ref[1] = zc.astype(BF16)
        z_ref[2] = zh.astype(BF16)
        y_ref[...] = (zb * _conv3(zc * zh, dw_ref[...])).astype(BF16)

    return pl.pallas_call(
        body, name=name, grid=(nj,),
        in_specs=[pl.BlockSpec((s, d), lambda j: (0, 0)),
                  pl.BlockSpec((d, tn), lambda j: (0, j)), pl.BlockSpec((d, tn), lambda j: (0, j + nj)),
                  pl.BlockSpec((d, tn), lambda j: (0, j + 2 * nj)), pl.BlockSpec((3, tn), lambda j: (0, j))],
        out_specs=[pl.BlockSpec((3, s, tn), lambda j: (0, 0, j)), pl.BlockSpec((s, tn), lambda j: (0, j))],
        out_shape=[jax.ShapeDtypeStruct((3, s, d), BF16), jax.ShapeDtypeStruct((s, d), BF16)],
        compiler_params=_params(("parallel",)),
    )(n, win, win, win, wdw)


def _sconv_mid_bwd(dm, wout, z, wdw, name):
    s, d = dm.shape
    tn = COL_TILE
    nj = d // tn

    def body(dm_ref, wo_ref, z_ref, w_ref, dz_ref, dw_ref):
        dy = _dot(dm_ref[...], wo_ref[...], NT)
        zb = z_ref[0].astype(F32)
        zc = z_ref[1].astype(F32)
        zh = z_ref[2].astype(F32)
        w = w_ref[...]
        p = zc * zh
        cp = _conv3(p, w)
        dz_ref[0] = (dy * cp).astype(BF16)
        dcp = dy * zb
        dp = _conv3_bwd_x(dcp, w)
        _conv3_bwd_w(dcp, p, dw_ref)
        dz_ref[1] = (dp * zh).astype(BF16)
        dz_ref[2] = (dp * zc).astype(BF16)

    return pl.pallas_call(
        body, name=name, grid=(nj,),
        in_specs=[pl.BlockSpec((s, d), lambda j: (0, 0)), pl.BlockSpec((tn, d), lambda j: (j, 0)),
                  pl.BlockSpec((3, s, tn), lambda j: (0, 0, j)), pl.BlockSpec((3, tn), lambda j: (0, j))],
        out_specs=[pl.BlockSpec((3, s, tn), lambda j: (0, 0, j)), pl.BlockSpec((3, tn), lambda j: (0, j))],
        out_shape=[jax.ShapeDtypeStruct((3, s, d), BF16), jax.ShapeDtypeStruct((3, d), F32)],
        compiler_params=_params(("parallel",)),
    )(dm, wout, z, wdw)


def _pool_select(g, c2, c4, c8, c16):
    return jnp.where(g == 0, c2, jnp.where(g == 1, c4, jnp.where(g == 2, c8, c16)))


def _pool_inv_count(g, shape):
    pos = lax.broadcasted_iota(jnp.int32, shape, 0).astype(F32) + 1.0
    win = (2 << g).astype(F32)
    return jnp.minimum(pos, win)


def _pool_fwd(n, win, wgrp, scale, name):
    s, d = n.shape
    tn = POOL_GROUP_DIM

    def body(n_ref, wi_ref, wg_ref, sc_ref, p_ref, y_ref):
        g = pl.program_id(0)
        u = _dot(n_ref[...], wi_ref[...])
        s2 = u + _shift_down(u, 1)
        s4 = s2 + _shift_down(s2, 2)
        s8 = s4 + _shift_down(s4, 4)
        s16 = s8 + _shift_down(s8, 8)
        tot = _pool_select(g, s2, s4, s8, s16)
        p = (tot / _pool_inv_count(g, u.shape) - u).astype(BF16)
        p_ref[...] = p
        y_ref[...] = (_dot(p, wg_ref[...]) * sc_ref[...]).astype(BF16)

    return pl.pallas_call(
        body, name=name, grid=(d // tn,),
        in_specs=[pl.BlockSpec((s, d), lambda g: (0, 0)), pl.BlockSpec((d, tn), lambda g: (0, g)),
                  pl.BlockSpec((None, tn, tn), lambda g: (g, 0, 0)), pl.BlockSpec((1, tn), lambda g: (0, g))],
        out_specs=[pl.BlockSpec((s, tn), lambda g: (0, g)), pl.BlockSpec((s, tn), lambda g: (0, g))],
        out_shape=[jax.ShapeDtypeStruct((s, d), BF16), jax.ShapeDtypeStruct((s, d), BF16)],
        compiler_params=_params(("parallel",)),
    )(n, win, wgrp, scale)


def _pool_mid_bwd(dm, wout, p, wgrp, scale, name):
    s, d = dm.shape
    tn = POOL_GROUP_DIM

    def body(dm_ref, wo_ref, p_ref, wg_ref, sc_ref, du_ref, dwg_ref, dsc_ref):
        g = pl.program_id(0)
        dy = _dot(dm_ref[...], wo_ref[...], NT)
        pv = p_ref[...]
        wg = wg_ref[...]
        ypre = _dot(pv, wg)
        dsc_ref[...] = jnp.sum(dy * ypre, axis=0, keepdims=True)
        dypre = (dy * sc_ref[...]).astype(BF16)
        dwg_ref[...] = _dot(pv, dypre, TN)
        dp = _dot(dypre, wg, NT)
        e = dp / _pool_inv_count(g, dp.shape)
        f2 = e + _shift_up(e, 1)
        f4 = f2 + _shift_up(f2, 2)
        f8 = f4 + _shift_up(f4, 4)
        f16 = f8 + _shift_up(f8, 8)
        du_ref[...] = (_pool_select(g, f2, f4, f8, f16) - dp).astype(BF16)

    return pl.pallas_call(
        body, name=name, grid=(d // tn,),
        in_specs=[pl.BlockSpec((s, d), lambda g: (0, 0)), pl.BlockSpec((tn, d), lambda g: (g, 0)),
                  pl.BlockSpec((s, tn), lambda g: (0, g)), pl.BlockSpec((None, tn, tn), lambda g: (g, 0, 0)),
                  pl.BlockSpec((1, tn), lambda g: (0, g))],
        out_specs=[pl.BlockSpec((s, tn), lambda g: (0, g)), pl.BlockSpec((None, tn, tn), lambda g: (g, 0, 0)),
                   pl.BlockSpec((1, tn), lambda g: (0, g))],
        out_shape=[jax.ShapeDtypeStruct((s, d), BF16), jax.ShapeDtypeStruct((4, tn, tn), F32),
                   jax.ShapeDtypeStruct((1, d), F32)],
        compiler_params=_params(("parallel",)),
    )(dm, wout, p, wgrp, scale)


PANEL = LANES
ATTN_EXT = ATTN_WIDTH + PANEL
DVEC_LANE = HEADS


def _alibi_slopes(g, dil):
    all_slopes = 2.0 ** (-8.0 * np.arange(1, N_HEADS_A + 1) / N_HEADS_A)
    return [float(np.float32(sl) * np.float32(dil)) for sl in all_slopes[g * HEADS:(g + 1) * HEADS]]


def _residue_order(a, dil, name):
    s, w = a.shape
    per = ROW_TILE // dil
    panels = w // PANEL

    def body(a_ref, o_ref, *tiles):
        for c in range(panels):
            cols = slice(c * PANEL, (c + 1) * PANEL)
            tiles[c][...] = a_ref[:, cols].astype(F32)
            for r in range(dil):
                o_ref[r, :, cols] = tiles[c][pl.ds(r, per, stride=dil), :].astype(o_ref.dtype)

    out = pl.pallas_call(
        body, name=name, grid=(s // ROW_TILE,),
        in_specs=[pl.BlockSpec((ROW_TILE, w), lambda i: (i, 0))],
        out_specs=pl.BlockSpec((dil, per, w), lambda i: (0, i, 0)),
        out_shape=jax.ShapeDtypeStruct((dil, s // dil, w), a.dtype),
        scratch_shapes=[pltpu.VMEM((ROW_TILE, PANEL), F32)] * panels,
        compiler_params=_params(("parallel",)),
    )(a)
    return out.reshape(s, w)


def _token_order(a, dil, acc, name):
    s, w = a.shape
    per = ROW_TILE // dil
    panels = w // PANEL
    has_acc = acc is not None

    def body(*refs):
        a_ref = refs[0]
        o_ref = refs[2] if has_acc else refs[1]
        tiles = refs[3:] if has_acc else refs[2:]
        for c in range(panels):
            cols = slice(c * PANEL, (c + 1) * PANEL)
            for r in range(dil):
                tiles[c][pl.ds(r, per, stride=dil), :] = a_ref[r, :, cols]
            v = tiles[c][...]
            if has_acc:
                v = v + refs[1][:, cols]
            o_ref[:, cols] = v

    row = pl.BlockSpec((ROW_TILE, w), lambda i: (i, 0))
    return pl.pallas_call(
        body, name=name, grid=(s // ROW_TILE,),
        in_specs=[pl.BlockSpec((dil, per, w), lambda i: (0, i, 0))] + ([row] if has_acc else []),
        out_specs=row, out_shape=jax.ShapeDtypeStruct((s, w), F32),
        scratch_shapes=[pltpu.VMEM((ROW_TILE, PANEL), F32)] * panels,
        compiler_params=_params(("parallel",)),
    )(*([a.reshape(dil, s // dil, w)] + ([acc] if has_acc else [])))


def _qkv_proj(n, wqkv, g, name):
    s, d = n.shape
    tm = _pick(s, MATMUL_TILES)

    def body(a_ref, b_ref, o_ref):
        o_ref[...] = _dot(a_ref[...], b_ref[...]).astype(BF16)

    return pl.pallas_call(
        body, name=name, grid=(s // tm, 3),
        in_specs=[pl.BlockSpec((tm, d), lambda i, t: (i, 0)),
                  pl.BlockSpec((d, ATTN_WIDTH), lambda i, t: (0, 3 * g + t))],
        out_specs=pl.BlockSpec((None, tm, ATTN_WIDTH), lambda i, t: (t, i, 0)),
        out_shape=jax.ShapeDtypeStruct((3, s, ATTN_WIDTH), BF16),
        compiler_params=_params(("parallel", "parallel")),
    )(n, wqkv)


def _attn_window(n, ln):
    if ln == BLOCK:
        return 0, BLOCK
    return pl.multiple_of(jnp.maximum(n - 1, 0) * BLOCK, BLOCK), 2 * BLOCK


def _attn_mask(n, k0, kw):
    qpos = n * BLOCK + lax.broadcasted_iota(jnp.int32, (BLOCK, kw), 0)
    kpos = k0 + lax.broadcasted_iota(jnp.int32, (BLOCK, kw), 1)
    dist = qpos - kpos
    return dist.astype(F32), (dist >= 0) & (dist <= BLOCK)


def _attn_scores(q, keys, slope, dist, valid):
    s = _dot(q, keys, NT) * (HEAD_DIM ** -0.5) - slope * dist
    return jnp.where(valid, s, NEG_INF)


def _attn_fwd(qkv, g, name):
    _, s, w = qkv.shape
    dil = DILATED_CFG[g][1]
    ln = s // dil
    nb = ln // BLOCK
    slopes = _alibi_slopes(g, dil)

    def body(qkv_ref, o_ref):
        n = pl.program_id(1)
        k0, kw = _attn_window(n, ln)
        cur, win = pl.ds(pl.multiple_of(n * BLOCK, BLOCK), BLOCK), pl.ds(k0, kw)
        dist, valid = _attn_mask(n, k0, kw)
        o_ref[:, w:] = jnp.zeros((BLOCK, PANEL), F32)
        for h in range(HEADS):
            cols = slice(h * HEAD_DIM, (h + 1) * HEAD_DIM)
            sc = _attn_scores(qkv_ref[0, cur, cols], qkv_ref[1, win, cols], slopes[h], dist, valid)
            m = jnp.max(sc, axis=-1, keepdims=True)
            p = jnp.exp(sc - m)
            den = jnp.sum(p, axis=-1, keepdims=True)
            o_ref[:, cols] = _dot(p.astype(BF16), qkv_ref[2, win, cols]) / den
            o_ref[:, w + h:w + h + 1] = m + jnp.log(den)

    return pl.pallas_call(
        body, name=name, grid=(dil, nb),
        in_specs=[pl.BlockSpec((3, ln, w), lambda r, n: (0, r, 0))],
        out_specs=pl.BlockSpec((BLOCK, ATTN_EXT), lambda r, n: (r * nb + n, 0)),
        out_shape=jax.ShapeDtypeStruct((s, ATTN_EXT), F32),
        compiler_params=_params(("parallel", "parallel")),
    )(qkv)


def _attn_bwd(qkv, dext, g, name):
    _, s, w = qkv.shape
    dil = DILATED_CFG[g][1]
    ln = s // dil
    nb = ln // BLOCK
    slopes = _alibi_slopes(g, dil)
    scale = HEAD_DIM ** -0.5

    def body(qkv_ref, de_ref, d_ref, dk_ref, dv_ref):
        n = pl.program_id(1)

        @pl.when(n == 0)
        def _():
            dk_ref[...] = jnp.zeros_like(dk_ref)
            dv_ref[...] = jnp.zeros_like(dv_ref)

        k0, kw = _attn_window(n, ln)
        cur, win = pl.ds(pl.multiple_of(n * BLOCK, BLOCK), BLOCK), pl.ds(k0, kw)
        dist, valid = _attn_mask(n, k0, kw)
        for h in range(HEADS):
            cols = slice(h * HEAD_DIM, (h + 1) * HEAD_DIM)
            q, keys = qkv_ref[0, cur, cols], qkv_ref[1, win, cols]
            dob = de_ref[:, cols].astype(BF16)
            p = jnp.exp(_attn_scores(q, keys, slopes[h], dist, valid) - de_ref[:, w + h:w + h + 1])
            dd = de_ref[:, w + DVEC_LANE + h:w + DVEC_LANE + h + 1]
            ds = (p * (_dot(dob, qkv_ref[2, win, cols], NT) - dd)).astype(BF16)
            d_ref[0, cur, cols] = (scale * _dot(ds, keys)).astype(BF16)
            dv_ref[win, cols] += _dot(p.astype(BF16), dob, TN)
            dk_ref[win, cols] += scale * _dot(ds, q, TN)

        @pl.when(n == nb - 1)
        def _():
            d_ref[1] = dk_ref[...].astype(BF16)
            d_ref[2] = dv_ref[...].astype(BF16)

    return pl.pallas_call(
        body, name=name, grid=(dil, nb),
        in_specs=[pl.BlockSpec((3, ln, w), lambda r, n: (0, r, 0)),
                  pl.BlockSpec((BLOCK, ATTN_EXT), lambda r, n: (r * nb + n, 0))],
        out_specs=pl.BlockSpec((3, ln, w), lambda r, n: (0, r, 0)),
        out_shape=jax.ShapeDtypeStruct((3, s, w), BF16),
        scratch_shapes=[pltpu.VMEM((ln, w), F32), pltpu.VMEM((ln, w), F32)],
        compiler_params=_params(("parallel", "arbitrary")),
    )(qkv, dext)


def _attn_merge(e0, e1, e2, name):
    s = e0.shape[0]
    w = ATTN_WIDTH

    def body(e0_ref, e1_ref, e2_ref, m_ref, mb_ref, lse_ref):
        refs = (e0_ref, e1_ref, e2_ref)
        l = [r[:, w:w + HEADS] for r in refs]
        mx = jnp.maximum(jnp.maximum(l[0], l[1]), l[2])
        e = [jnp.exp(v - mx) for v in l]
        z = e[0] + e[1] + e[2]
        lse_ref[...] = mx + jnp.log(z)
        wts = [v / z for v in e]
        for h in range(HEADS):
            cols = slice(h * HEAD_DIM, (h + 1) * HEAD_DIM)
            acc = wts[0][:, h:h + 1] * refs[0][:, cols]
            for g in range(1, N_GROUPS_A):
                acc = acc + wts[g][:, h:h + 1] * refs[g][:, cols]
            m_ref[:, cols] = acc
            mb_ref[:, cols] = acc.astype(BF16)

    ext = pl.BlockSpec((ROW_TILE, ATTN_EXT), lambda i: (i, 0))
    row = pl.BlockSpec((ROW_TILE, w), lambda i: (i, 0))
    return pl.pallas_call(
        body, name=name, grid=(s // ROW_TILE,),
        in_specs=[ext, ext, ext],
        out_specs=[row, row, pl.BlockSpec((ROW_TILE, HEADS), lambda i: (i, 0))],
        out_shape=[jax.ShapeDtypeStruct((s, w), F32), jax.ShapeDtypeStruct((s, w), BF16),
                   jax.ShapeDtypeStruct((s, HEADS), F32)],
        compiler_params=_params(("parallel",)),
    )(e0, e1, e2)


def _attn_dvec(dmerged, merged, lse_all, name):
    s, w = merged.shape

    def body(dm_ref, m_ref, lse_ref, de_ref):
        dmv = dm_ref[...]
        de_ref[:, :w] = dmv
        de_ref[:, w:] = jnp.zeros((ROW_TILE, PANEL), F32)
        de_ref[:, w:w + HEADS] = lse_ref[...]
        prod = dmv * m_ref[...]
        for h in range(HEADS):
            lane = w + DVEC_LANE + h
            de_ref[:, lane:lane + 1] = jnp.sum(prod[:, h * HEAD_DIM:(h + 1) * HEAD_DIM], axis=-1, keepdims=True)

    row = pl.BlockSpec((ROW_TILE, w), lambda i: (i, 0))
    return pl.pallas_call(
        body, name=name, grid=(s // ROW_TILE,),
        in_specs=[row, row, pl.BlockSpec((ROW_TILE, HEADS), lambda i: (i, 0))],
        out_specs=pl.BlockSpec((ROW_TILE, ATTN_EXT), lambda i: (i, 0)),
        out_shape=jax.ShapeDtypeStruct((s, ATTN_EXT), F32),
        compiler_params=_params(("parallel",)),
    )(dmerged, merged, lse_all)


def _attention_fwd(n, wqkv, wo, tag):
    ns, qkvs, exts = [], [], []
    for g, (_, dil) in enumerate(DILATED_CFG):
        ng = n if dil == 1 else _residue_order(n, dil, f"{tag}_order_g{g}")
        qkv = _qkv_proj(ng, wqkv, g, f"{tag}_qkv_g{g}")
        ext = _attn_fwd(qkv, g, f"{tag}_fwd_g{g}")
        ns.append(ng)
        qkvs.append(qkv)
        exts.append(ext if dil == 1 else _token_order(ext, dil, None, f"{tag}_unorder_g{g}"))
    merged, merged_bf, lse_all = _attn_merge(*exts, f"{tag}_merge")
    m = _matmul(merged_bf, wo, "nn", F32, f"{tag}_wo")
    return m, (ns, qkvs, merged, merged_bf, lse_all)


def _attention_bwd(dm, wqkv, wo, saved, tag):
    ns, qkvs, merged, merged_bf, lse_all = saved
    d_wo = _matmul(merged_bf, dm, "tn", BF16, f"{tag}_dwo")
    dmerged = _matmul(dm, wo, "nt", F32, f"{tag}_dmerged")
    dext = _attn_dvec(dmerged, merged, lse_all, f"{tag}_dvec")
    width = 3 * ATTN_WIDTH
    d_wqkv, dn = [], None
    for g, (_, dil) in enumerate(DILATED_CFG):
        dext_g = dext if dil == 1 else _residue_order(dext, dil, f"{tag}_dorder_g{g}")
        dqkv = _attn_bwd(qkvs[g], dext_g, g, f"{tag}_bwd_g{g}")
        d_wqkv.append(_matmul(ns[g], dqkv, "tn", BF16, f"{tag}_dwqkv_g{g}", b_parts=3))
        dn_g = _matmul(dqkv, wqkv[:, g * width:(g + 1) * width], "nt", F32, f"{tag}_dn_g{g}", a_parts=3)
        dn = dn_g if dil == 1 else _token_order(dn_g, dil, dn, f"{tag}_dn_sum_g{g}")
    return dn, jnp.concatenate(d_wqkv, axis=1), d_wo


def _layer_matrices(i):
    mixer = (("attn_w_qkv", "attn_w_o"), ("conv_w_in", "conv_w_out"), ("pool_w_in", "pool_w_grp", "pool_w_out"))[i % 3]
    return [(k, i // 3) for k in mixer] + [("ffn_w_up", i), ("ffn_w_down", i)]


def _local_step(x, tgt, vec, weights, sink):
    ng = vec["norm_g"]

    def gain(i, j, token=None):
        g = ng[i, j][None, :]
        return g if token is None else g + token

    saved = []
    n = _rms_fwd(x, gain(0, 0), None, BF16, "norm_first")
    for i in range(DEPTH):
        wl = weights.layer(i)
        t0 = weights.hook(i, 0, n)
        kind, idx = i % 3, i // 3
        if kind == 0:
            m, ms = _attention_fwd(n, wl["attn_w_qkv"], wl["attn_w_o"], "attn")
        elif kind == 1:
            taps = vec["conv_w_dw"][idx] if t0 is None else vec["conv_w_dw"][idx] + t0
            z, y = _sconv_fwd(n, wl["conv_w_in"], taps, "sconv_fwd")
            m = _matmul(y, wl["conv_w_out"], "nn", F32, "sconv_out")
            ms = (z, y)
        else:
            scale = vec["pool_scale"][idx][None, :] if t0 is None else vec["pool_scale"][idx][None, :] + t0
            p, y = _pool_fwd(n, wl["pool_w_in"], wl["pool_w_grp"], scale, "pool_fwd")
            m = _matmul(y, wl["pool_w_out"], "nn", F32, "pool_out")
            ms = (p, y)
        t1 = weights.hook(i, 1, m)
        x1, n2 = _rms_res_pre(m, gain(i, 1, t0), x, gain(i, 2, t1), "norm_res_pre")
        h, a = _ffn_up(n2, wl["ffn_w_up"], vec["ffn_w_dw"][i], "ffn_up")
        t2 = weights.hook(i, 2, a)
        f = _matmul(a, wl["ffn_w_down"].reshape(D_FF, D_MODEL), "nn", F32, "ffn_down", a_parts=FFN_PAIRS)
        saved.append((x, n, m, ms, x1, n2, h, a, f, wl))
        if i + 1 < DEPTH:
            x, n = _rms_res_pre(f, gain(i, 3, t2), x1, gain(i + 1, 0), "norm_res_pre")
        else:
            x = _rms_fwd(f, gain(i, 3), x1, F32, "norm_res")
        weights.hook(i, 3, x)

    loss, dx = _loss_head(x, tgt, "loss_head")

    g_norm = [[None] * 4 for _ in range(DEPTH)]
    g_taps, g_scale, g_ffn_dw = [], [], [None] * DEPTH
    df, g_norm[DEPTH - 1][3] = _rms_bwd(saved[-1][8], gain(DEPTH - 1, 3), dx, None, BF16, "norm_bwd_sub")
    t0 = None
    for i in reversed(range(DEPTH)):
        xin, n, m, ms, x1, n2, h, a, f, wl = saved[i]
        kind, idx = i % 3, i // 3
        gl = {}
        d_wdown = _matmul(a, df, "tn", BF16, "ffn_dwdown", a_parts=FFN_PAIRS)
        gl["ffn_w_down"] = d_wdown.reshape(N_DEV, D_FF // N_DEV, D_MODEL)
        dh, dwg, dwu = _ffn_mid_bwd(df, wl["ffn_w_down"].reshape(FFN_PAIRS, -1, D_MODEL), h, vec["ffn_w_dw"][i],
                                    "ffn_mid_bwd")
        g_ffn_dw[i] = jnp.concatenate([dwg, dwu], axis=0)
        t1 = sink.hook(i, 1, dh)
        gl["ffn_w_up"] = _ffn_dwup(n2, dh, "ffn_dwup")
        dn2 = _ffn_dn(dh, wl["ffn_w_up"], "ffn_dn")
        dx1, dm, g_norm[i][2], g_norm[i][1] = _rms_bwd_pair(x1, gain(i, 2, t0), dn2, dx, m, gain(i, 1, t1),
                                                            "norm_bwd_pair")
        t2 = sink.hook(i, 2, dm)
        if kind == 0:
            dn, gl["attn_w_qkv"], gl["attn_w_o"] = _attention_bwd(dm, wl["attn_w_qkv"], wl["attn_w_o"], ms, "attn")
        elif kind == 1:
            z, y = ms
            gl["conv_w_out"] = _matmul(y, dm, "tn", BF16, "sconv_dwout")
            taps = vec["conv_w_dw"][idx] if t2 is None else vec["conv_w_dw"][idx] + t2
            dz, ddw = _sconv_mid_bwd(dm, wl["conv_w_out"], z, taps, "sconv_mid_bwd")
            g_taps.append(ddw)
            gl["conv_w_in"] = _matmul(n, dz, "tn", BF16, "sconv_dwin", b_parts=3)
            dn = _matmul(dz, wl["conv_w_in"], "nt", F32, "sconv_dn", a_parts=3)
        else:
            p, y = ms
            gl["pool_w_out"] = _matmul(y, dm, "tn", BF16, "pool_dwout")
            scale = vec["pool_scale"][idx][None, :] if t2 is None else vec["pool_scale"][idx][None, :] + t2
            du, gl["pool_w_grp"], dscale = _pool_mid_bwd(dm, wl["pool_w_out"], p, wl["pool_w_grp"], scale, "pool_mid_bwd")
            g_scale.append(dscale[0])
            gl["pool_w_in"] = _matmul(n, du, "tn", BF16, "pool_dwin")
            dn = _matmul(du, wl["pool_w_in"], "nt", F32, "pool_dn")
        sink.hook(i, 3, dn)
        if i > 0:
            dx, df, g_norm[i][0], g_norm[i - 1][3] = _rms_bwd_pair(xin, gain(i, 0, t2), dn, dx1, saved[i - 1][8],
                                                                   gain(i - 1, 3), "norm_bwd_pair")
        else:
            dx, g_norm[0][0] = _rms_bwd(xin, gain(0, 0), dn, dx1, F32, "norm_bwd_res")
        t0 = sink.layer_done(i, gl)

    vec_grads = {"norm_g": jnp.stack([jnp.concatenate(row, axis=0) for row in g_norm]), "conv_w_dw": jnp.stack(g_taps),
                 "pool_scale": jnp.stack(g_scale), "ffn_w_dw": g_ffn_dw}
    return loss, dx, vec_grads


_AXES = ("x", "y", "c")
ROUTE_A = ("y", "x", "c")
ROUTE_B = ("x", "y", "c")
def _dev_index(pos):
    return 4 * pos["x"] + 2 * pos["y"] + pos["c"]


_HBM = pl.BlockSpec(memory_space=pltpu.HBM)
_SEM = pl.BlockSpec(memory_space=pltpu.SEMAPHORE)
_ANY = pl.BlockSpec(memory_space=pl.ANY)
_EFFECT = pltpu.SideEffectType.DATAFLOW_SIDE_EFFECTING


def _copies_start(describe, arrays, n_copies, name, after):
    n = len(arrays)
    deps = [] if after is None else [after]

    def body(*refs):
        send_sems, recv_sems = refs[n + len(deps)], refs[n + len(deps) + 1]
        for c in describe(refs[:n], send_sems, recv_sems):
            c.start()
        refs[-1][...] = jnp.zeros_like(refs[-1])

    outs = pl.pallas_call(
        body, name=f"{name}_start",
        out_shape=(pltpu.SemaphoreType.DMA((n_copies,)), pltpu.SemaphoreType.DMA((n_copies,)),
                   *[pltpu.HBM(a.shape, a.dtype) for a in arrays], jax.ShapeDtypeStruct((1, D_MODEL), F32)),
        in_specs=[_HBM] * n + [_ANY] * len(deps),
        out_specs=(_SEM, _SEM, *([_HBM] * n), pl.BlockSpec(memory_space=pltpu.VMEM)),
        input_output_aliases={i: 2 + i for i in range(n)},
        compiler_params=pltpu.CompilerParams(has_side_effects=_EFFECT),
    )(*[pltpu.with_memory_space_constraint(a, pltpu.HBM) for a in arrays], *deps)
    return (outs[0], outs[1], list(outs[2:2 + n])), outs[-1]


def _copies_wait(describe, handle, name, after):
    send_sems, recv_sems, arrays = handle
    n = len(arrays)
    deps = [] if after is None else [after]

    def body(*refs):
        for c in describe(refs[:n], refs[n], refs[n + 1]):
            c.wait_send()
            c.wait_recv()

    outs = pl.pallas_call(
        body, name=f"{name}_wait",
        out_shape=tuple(pltpu.HBM(a.shape, a.dtype) for a in arrays),
        in_specs=[_HBM] * n + [_SEM, _SEM] + [_ANY] * len(deps), out_specs=tuple([_HBM] * n),
        input_output_aliases={i: i for i in range(n)},
        compiler_params=pltpu.CompilerParams(has_side_effects=_EFFECT),
    )(*arrays, send_sems, recv_sems, *deps)
    return list(outs)


GATHER_STAGE_COPIES = (3, 3, 1)


def _gather_copies(stage, routes):
    n = len(routes)

    def describe(refs, send_sems, recv_sems):
        pos = {a: lax.axis_index(a) for a in _AXES}

        def flipped(axes):
            return {a: 1 - pos[a] if a in axes else pos[a] for a in _AXES}

        copies = []
        for i, (a1, a2, a3) in enumerate(routes):
            land = refs[n + i] if stage == 1 else refs[i]
            p1, p2, p12, p3 = flipped((a1,)), flipped((a2,)), flipped((a1, a2)), flipped((a3,))
            plan = {1: [(None, p1), (None, p2), (None, p3)], 2: [(p1, p2), (p1, p3), (p2, p3)], 3: [(p12, p3)]}[stage]
            for holder, to in plan:
                slot = land.at[_dev_index(pos if holder is None else holder)]
                k = len(copies)
                copies.append(pltpu.make_async_remote_copy(
                    src_ref=refs[i] if holder is None else slot, dst_ref=slot,
                    send_sem=send_sems.at[k], recv_sem=recv_sems.at[k],
                    device_id=tuple(to[a] for a in _AXES), device_id_type=pl.DeviceIdType.MESH))
        return copies

    return describe


def _gather_begin(shards, routes, name, after):
    n = len(shards)
    lands = [lax.empty((N_DEV,) + a.shape, a.dtype) for a in shards]
    handle, token = _copies_start(_gather_copies(1, routes), list(shards) + lands, GATHER_STAGE_COPIES[0] * n,
                                  f"{name}_1", after)
    return {"stage": 1, "handle": handle, "routes": routes, "name": name, "n": n}, token


def _gather_next(state, after):
    stage, routes, name, n = state["stage"], state["routes"], state["name"], state["n"]
    arrays = _copies_wait(_gather_copies(stage, routes), state["handle"], f"{name}_{stage}", after)
    if stage == 1:
        state = dict(state, shards=arrays[:n])
        arrays = arrays[n:]
    if stage == 3:
        me = _dev_index({a: lax.axis_index(a) for a in _AXES})
        return [lax.dynamic_update_index_in_dim(o, s, me, 0) for o, s in zip(arrays, state["shards"])], None
    handle, token = _copies_start(_gather_copies(stage + 1, routes), arrays, GATHER_STAGE_COPIES[stage] * n,
                                  f"{name}_{stage + 1}", None)
    return dict(state, stage=stage + 1, handle=handle), token


ADD_ROW_TILES = (1024, 704, 512, 352, 256, 128, 96, 64, 32, 16)


def _add_half(a, recv, me, out_dtype, name):
    p, q, cols = recv.shape
    tr = _pick(q, ADD_ROW_TILES)

    def body(me_ref, a_ref, b_ref, o_ref):
        o_ref[...] = (a_ref[...].astype(F32) + b_ref[...].astype(F32)).astype(o_ref.dtype)

    return pl.pallas_call(
        body, name=name,
        grid_spec=pltpu.PrefetchScalarGridSpec(
            num_scalar_prefetch=1, grid=(p, q // tr),
            in_specs=[pl.BlockSpec((None, None, tr, cols), lambda j, i, m: (j, m[0], i, 0)),
                      pl.BlockSpec((None, tr, cols), lambda j, i, m: (j, i, 0))],
            out_specs=pl.BlockSpec((None, tr, cols), lambda j, i, m: (j, i, 0))),
        out_shape=jax.ShapeDtypeStruct((p, q, cols), out_dtype),
        compiler_params=_params(("parallel", "parallel")),
    )(me, a, recv)


def _half_copies(axes):
    n = len(axes)

    def describe(refs, send_sems, recv_sems):
        pos = {a: lax.axis_index(a) for a in _AXES}
        copies = []
        for i, axis in enumerate(axes):
            peer = tuple(1 - pos[a] if a == axis else pos[a] for a in _AXES)
            copies.append(pltpu.make_async_remote_copy(
                src_ref=refs[i].at[:, 1 - pos[axis]], dst_ref=refs[n + i], send_sem=send_sems.at[i],
                recv_sem=recv_sems.at[i], device_id=peer, device_id_type=pl.DeviceIdType.MESH))
        return copies

    return describe


def _scatter_begin(slots, routes, tags, name):
    shapes = [a.shape[1:] for a in slots]
    rows = [math.prod(s[:-1]) for s in shapes]
    arrays = [a.reshape(4, 2, n, s[-1]) for a, n, s in zip(slots, rows, shapes)]
    return _scatter_start({"stage": 0, "arrays": arrays, "routes": routes, "tags": tags, "name": name,
                           "shapes": shapes, "rows": rows})


def _scatter_start(state):
    stage, arrays = state["stage"], state["arrays"]
    axes = [r[2 - stage] for r in state["routes"]]
    lands = [lax.empty((a.shape[0],) + a.shape[2:], a.dtype) for a in arrays]
    handle, token = _copies_start(_half_copies(axes), arrays + lands, len(arrays), f"{state['name']}_{stage + 1}", None)
    return dict(state, handle=handle, axes=axes), token


def _scatter_next(state, after):
    stage, axes, n = state["stage"], state["axes"], len(state["arrays"])
    both = _copies_wait(_half_copies(axes), state["handle"], f"{state['name']}_{stage + 1}", after)
    coord = {a: lax.axis_index(a).astype(jnp.int32).reshape(1) for a in _AXES}
    sums = [_add_half(a, r, coord[ax], F32 if stage == 2 else BF16, f"scatter_add_{stage + 1}_{t}")
            for a, r, ax, t in zip(both[:n], both[n:], axes, state["tags"])]
    if stage == 2:
        return [a.reshape(s) for a, s in zip(sums, state["shapes"])], None
    if stage == 0:
        views = [(1, 2, 2 * r, s[-1]) if route[1] == "x" else (2, 2, r, s[-1])
                 for r, s, route in zip(state["rows"], state["shapes"], state["routes"])]
    else:
        views = [(1, 2, r, s[-1]) for r, s in zip(state["rows"], state["shapes"])]
    return _scatter_start(dict(state, stage=stage + 1, arrays=[a.reshape(v) for a, v in zip(sums, views)]))


_WEIGHTS = {
    "norm_g": ((DEPTH, 4, D_MODEL), 2, True),
    "attn_w_qkv": ((2, D_MODEL, 4608), 2, False),
    "attn_w_o": ((2, ATTN_WIDTH, D_MODEL), 2, False),
    "conv_w_in": ((1, D_MODEL, 3 * D_MODEL), 2, False),
    "conv_w_dw": ((1, 3, D_MODEL), 2, True),
    "conv_w_out": ((1, D_MODEL, D_MODEL), 1, False),
    "pool_w_in": ((1, D_MODEL, D_MODEL), 1, False),
    "pool_w_grp": ((1, 4, POOL_GROUP_DIM, POOL_GROUP_DIM), 2, False),
    "pool_scale": ((1, D_MODEL), 1, True),
    "pool_w_out": ((1, D_MODEL, D_MODEL), 1, False),
    "ffn_w_up": ((DEPTH, D_MODEL, 2 * D_FF), 2, False),
    "ffn_w_dw": ((DEPTH, 3, 2 * D_FF), 2, True),
    "ffn_w_down": ((DEPTH, D_FF, D_MODEL), 1, False),
}
_NAMES = tuple(_WEIGHTS)
_VECTORS = tuple(k for k in _NAMES if _WEIGHTS[k][2])
_MATRICES = tuple(k for k in _NAMES if not _WEIGHTS[k][2])
_FFN = ("ffn_w_up", "ffn_w_down")
_ON_ROUTE_A = ("ffn_w_up", "attn_w_o", "conv_w_out", "pool_w_in")
PACK_ROWS = 16


def _route(name):
    return ROUTE_A if name in _ON_ROUTE_A else ROUTE_B


def _shard_shape(name):
    shape, ax, _ = _WEIGHTS[name]
    return tuple(s // N_DEV if i == ax else s for i, s in enumerate(shape))


def _full_from_slots(slots, name, layers=None):
    shape, ax, _ = _WEIGHTS[name]
    if layers is not None:
        shape = (layers,) + shape[1:]
    return jnp.moveaxis(slots, 0, ax).reshape(shape)


def _slots_from_full(full, name):
    shape, ax, _ = _WEIGHTS[name]
    split = shape[:ax] + (N_DEV, shape[ax] // N_DEV) + shape[ax + 1:]
    return jnp.moveaxis(full.reshape(split), ax, 0)


def _pack_vectors(parts, lead):
    rows = []
    for k in _VECTORS:
        r = parts[k].reshape(lead + (-1, LANES))
        pad = -r.shape[-2] % PACK_ROWS
        rows.append(jnp.pad(r, [(0, 0)] * len(lead) + [(0, pad), (0, 0)]))
    return jnp.concatenate(rows, axis=len(lead))


def _unpack_vectors(buf, lead):
    out, r0 = {}, 0
    for k in _VECTORS:
        shard = _shard_shape(k)
        rows = math.prod(shard) // LANES
        out[k] = buf[..., r0:r0 + rows, :].reshape(lead + shard)
        r0 += rows + (-rows % PACK_ROWS)
    return out


class _LayerWeights:
    def __init__(self, shards):
        self.cast = {k: shards[k].astype(BF16) for k in _MATRICES}
        items = _layer_matrices(0)
        state, _ = _gather_begin(self._send(items) + [_pack_vectors(shards, ())],
                                 [_route(k) for k, _ in items] + [ROUTE_B], "gather0", None)
        for _ in range(2):
            state, _ = _gather_next(state, None)
        outs, _ = _gather_next(state, None)
        vec = _unpack_vectors(outs[-1], (N_DEV,))
        self.vec = {k: _full_from_slots(vec[k], k) for k in _VECTORS}
        self.vec["ffn_w_dw"] = [vec["ffn_w_dw"][:, l] for l in range(DEPTH)]
        self.ready = {0: self._unpack(items, outs[:-1])}
        self.state = None

    def _send(self, items):
        return [self.cast[k][j] for k, j in items]

    @staticmethod
    def _unpack(items, outs):
        return {k: o if k in _FFN else _full_from_slots(o[:, None], k, layers=1)[0] for (k, _), o in zip(items, outs)}

    def layer(self, i):
        return self.ready[i]

    def hook(self, i, point, after):
        if i + 1 >= DEPTH:
            return None
        items = _layer_matrices(i + 1)
        if point == 0:
            self.state, token = _gather_begin(self._send(items), [_route(k) for k, _ in items], f"gather{i + 1}", after)
        else:
            self.state, token = _gather_next(self.state, after)
            if point == 3:
                self.ready[i + 1] = self._unpack(items, self.state)
        return token


def _layer_slots(g, name):
    shape, ax, _ = _WEIGHTS[name]
    shape, ax = shape[1:], ax - 1
    split = shape[:ax] + (N_DEV, shape[ax] // N_DEV) + shape[ax + 1:]
    return jnp.moveaxis(g.reshape(split), ax, 0).astype(BF16)


class _GradSink:
    def __init__(self):
        self.state = None
        self.sums = {}
        self.last = None

    def layer_done(self, i, grads):
        items = _layer_matrices(i)
        slots = [grads[k] if k in _FFN else _layer_slots(grads[k], k) for k, _ in items]
        if i == 0:
            self.last = (items, slots)
            return None
        self.items = items
        self.state, token = _scatter_begin(slots, [_route(k) for k, _ in items], [f"{k}{j}" for k, j in items],
                                           f"scatter{i}")
        return token

    def hook(self, i, point, after):
        if self.state is None:
            return None
        self.state, token = _scatter_next(self.state, after)
        if point == 3:
            self.sums.update(dict(zip(self.items, self.state)))
            self.state = None
        return token


def _adamw(w, g, m, v, name, layer=None, prev=None):
    shape = w.shape
    cols = shape[-1]
    view = shape if len(shape) == 3 else (1, math.prod(shape[:-1]), cols)
    layers, rows, _ = view
    tr = _pick(rows, (512, 256, 128, 64, 32, 16, 8))
    n_prev = 0 if prev is None else 3

    def body(*refs):
        w_ref, g_ref, m_ref, v_ref = refs[n_prev:n_prev + 4]
        d_ref, nm_ref, nv_ref = refs[n_prev + 4:]
        gv = g_ref[...]
        nm = ADAM_B1 * m_ref[...] + (1.0 - ADAM_B1) * gv
        nv = ADAM_B2 * v_ref[...] + (1.0 - ADAM_B2) * jnp.square(gv)
        m_hat = nm / (1.0 - ADAM_B1 ** ADAM_STEP)
        v_hat = nv / (1.0 - ADAM_B2 ** ADAM_STEP)
        d_ref[...] = -ADAM_LR * (m_hat / (jnp.sqrt(v_hat) + ADAM_EPS) + ADAM_WD * w_ref[...])
        nm_ref[...] = nm
        nv_ref[...] = nv

    if layer is None:
        grid = (layers, rows // tr)
        blk = gblk = pl.BlockSpec((None, tr, cols), lambda l, i: (l, i, 0))
        gview = view
    else:
        grid = (rows // tr,)
        blk = pl.BlockSpec((None, tr, cols), lambda i: (layer, i, 0))
        gblk = pl.BlockSpec((tr, cols), lambda i: (i, 0))
        gview = (rows, cols)
    shp = jax.ShapeDtypeStruct(view, F32)
    outs = pl.pallas_call(
        body, name=name, grid=grid, in_specs=[_ANY] * n_prev + [blk, gblk, blk, blk], out_specs=[blk] * 3,
        out_shape=[shp] * 3, input_output_aliases={i: i for i in range(n_prev)},
        compiler_params=_params(("parallel",) * len(grid)),
    )(*([] if prev is None else [p.reshape(view) for p in prev]), w.reshape(view), g.reshape(gview), m.reshape(view),
      v.reshape(view))
    return [o.reshape(shape) for o in outs]


def kernel(x, norm_g, attn_w_qkv, attn_w_o, conv_w_in, conv_w_dw, conv_w_out, pool_w_in, pool_w_grp, pool_scale, pool_w_out, ffn_w_up, ffn_w_dw, ffn_w_down, loss_target, m_norm_g, m_attn_w_qkv, m_attn_w_o, m_conv_w_in, m_conv_w_dw, m_conv_w_out, m_pool_w_in, m_pool_w_grp, m_pool_scale, m_pool_w_out, m_ffn_w_up, m_ffn_w_dw, m_ffn_w_down, v_norm_g, v_attn_w_qkv, v_attn_w_o, v_conv_w_in, v_conv_w_dw, v_conv_w_out, v_pool_w_in, v_pool_w_grp, v_pool_scale, v_pool_w_out, v_ffn_w_up, v_ffn_w_dw, v_ffn_w_down):
    shards = dict(zip(_NAMES, (norm_g, attn_w_qkv, attn_w_o, conv_w_in, conv_w_dw, conv_w_out, pool_w_in,
                               pool_w_grp, pool_scale, pool_w_out, ffn_w_up, ffn_w_dw, ffn_w_down)))
    moms = dict(zip(_NAMES, (m_norm_g, m_attn_w_qkv, m_attn_w_o, m_conv_w_in, m_conv_w_dw, m_conv_w_out,
                             m_pool_w_in, m_pool_w_grp, m_pool_scale, m_pool_w_out, m_ffn_w_up, m_ffn_w_dw,
                             m_ffn_w_down)))
    vels = dict(zip(_NAMES, (v_norm_g, v_attn_w_qkv, v_attn_w_o, v_conv_w_in, v_conv_w_dw, v_conv_w_out,
                             v_pool_w_in, v_pool_w_grp, v_pool_scale, v_pool_w_out, v_ffn_w_up, v_ffn_w_dw,
                             v_ffn_w_down)))
    weights = _LayerWeights(shards)
    sink = _GradSink()
    loss, grad_x, vec_grads = _local_step(x[0], loss_target[0], weights.vec, weights, sink)
    loss = lax.psum(loss[0, 0], _AXES)

    items, slots = sink.last
    vec_slots = {k: _slots_from_full(vec_grads[k], k) for k in _VECTORS if k != "ffn_w_dw"}
    vec_slots["ffn_w_dw"] = jnp.stack(vec_grads["ffn_w_dw"], axis=1)
    state, _ = _scatter_begin(slots + [_pack_vectors(vec_slots, (N_DEV,)).astype(BF16)],
                              [_route(k) for k, _ in items] + [ROUTE_B], [f"{k}{j}" for k, j in items] + ["vectors"],
                              "scatter0")
    results = {}

    def step_layer(i):
        last = None
        for k, j in _layer_matrices(i):
            g = sink.sums[(k, j)]
            if _WEIGHTS[k][0][0] == 1:
                results[k] = (g[None], _adamw(shards[k], g[None], moms[k], vels[k], f"adamw_{k}"))
            else:
                gs, prev = results.get(k, ({}, None))
                gs[j] = g
                results[k] = (gs, _adamw(shards[k], g, moms[k], vels[k], f"adamw_{k}{j}", layer=j, prev=prev))
            last = results[k][1][0]
        return last

    for i in (3, 2):
        state, _ = _scatter_next(state, step_layer(i))
    sums, _ = _scatter_next(state, step_layer(1))
    sink.sums.update(dict(zip(items, sums[:-1])))
    step_layer(0)
    vec_sums = _unpack_vectors(sums[-1], ())
    for k in _VECTORS:
        results[k] = (vec_sums[k], _adamw(shards[k], vec_sums[k], moms[k], vels[k], f"adamw_{k}"))
    grads_out = {k: g if not isinstance(g, dict) else jnp.stack([g[j] for j in range(len(g))])
                 for k, (g, _) in results.items()}
    return (loss, grad_x[None], *[grads_out[k] for k in _NAMES], *[results[k][1][0] for k in _NAMES],
            *[results[k][1][1] for k in _NAMES], *[results[k][1][2] for k in _NAMES])
```

```python
import functools
import math

import numpy as np
import jax
import jax.numpy as jnp
from jax import lax
from jax.experimental import pallas as pl
from jax.experimental.pallas import tpu as pltpu

F32, BF16 = jnp.float32, jnp.bfloat16

D_MODEL = 1024
SEQ = 2048
DEPTH = 4
DILATED_CFG = ((128, 1), (512, 4), (2048, 16))
N_GROUPS_A = 3
HEADS = 8
HEAD_DIM = 64
ATTN_WIDTH = HEADS * HEAD_DIM
N_HEADS_A = N_GROUPS_A * HEADS
BLOCK = 128
NEG_INF = -1e30
POOL_GROUP_DIM = 256
D_FF = 2816
RMS_EPS = 1e-6
ADAM_LR, ADAM_B1, ADAM_B2, ADAM_EPS, ADAM_WD, ADAM_STEP = 0.001, 0.9, 0.999, 1e-08, 0.01, 10

N_DEV = 8
LANES = 128
V7X_VMEM_BYTES = 64 * 2 ** 20
VMEM_LIMIT_BYTES = V7X_VMEM_BYTES - 8 * 2 ** 20
COL_TILE = 256
ROW_TILE = 256
MATMUL_TILES = (1024, 1408, 512, 256, 128)
TN_RESIDENT_K = 2048

NN = (((1,), (0,)), ((), ()))
NT = (((1,), (1,)), ((), ()))
TN = (((0,), (0,)), ((), ()))


def _dot(a, b, dims=NN):
    return lax.dot_general(a, b, dims, preferred_element_type=F32)


def _params(sem=None):
    return pltpu.CompilerParams(dimension_semantics=sem, vmem_limit_bytes=VMEM_LIMIT_BYTES)


def _pick(n, prefs):
    for p in prefs:
        if n % p == 0:
            return p
    return n


def _matmul(a, b, mode, out_dtype, name, a_parts=1, b_parts=1):
    if mode == "nn":
        m, k = a.shape[-2], a.shape[-1] * a_parts
        n = b.shape[-1] * b_parts
    elif mode == "nt":
        m, k = a.shape[-2], a.shape[-1] * a_parts
        n = b.shape[-2]
    else:
        k, m = a.shape[-2], a.shape[-1] * a_parts
        n = b.shape[-1] * b_parts
    tm = _pick(m, MATMUL_TILES)
    tn = _pick(n // b_parts if mode != "nt" else n, MATMUL_TILES)
    kk = k // a_parts if mode != "tn" else k
    tk = _pick(kk, MATMUL_TILES)
    if mode == "tn":
        tm = _pick(m // a_parts, MATMUL_TILES)
        if k <= TN_RESIDENT_K:
            tk = k
    gm, gn, gk = m // tm, n // tn, k // tk

    def a_idx(i, j, kq):
        if mode == "tn":
            r, c, per = kq, i, (m // a_parts) // tm
        else:
            r, c, per = i, kq, (k // a_parts) // tk
        return (r, c) if a_parts == 1 else (c // per, r, c % per)

    def b_idx(i, j, kq):
        if mode == "nt":
            return (j, kq)
        per = (n // b_parts) // tn
        return (kq, j) if b_parts == 1 else (j // per, kq, j % per)

    a_blk = (tk, tm) if mode == "tn" else (tm, tk)
    b_blk = (tn, tk) if mode == "nt" else (tk, tn)
    if a_parts > 1:
        a_blk = (None,) + a_blk
    if b_parts > 1:
        b_blk = (None,) + b_blk
    dims = {"nn": NN, "nt": NT, "tn": TN}[mode]

    def body_single(a_ref, b_ref, o_ref):
        o_ref[...] = _dot(a_ref[...], b_ref[...], dims).astype(o_ref.dtype)

    def body(a_ref, b_ref, o_ref, acc_ref):
        kq = pl.program_id(2)

        @pl.when(kq == 0)
        def _():
            acc_ref[...] = jnp.zeros_like(acc_ref)

        acc_ref[...] += _dot(a_ref[...], b_ref[...], dims)

        @pl.when(kq == gk - 1)
        def _():
            o_ref[...] = acc_ref[...].astype(o_ref.dtype)

    return pl.pallas_call(
        body_single if gk == 1 else body, name=name, grid=(gm, gn, gk),
        in_specs=[pl.BlockSpec(a_blk, a_idx), pl.BlockSpec(b_blk, b_idx)],
        out_specs=pl.BlockSpec((tm, tn), lambda i, j, kq: (i, j)),
        out_shape=jax.ShapeDtypeStruct((m, n), out_dtype),
        scratch_shapes=[] if gk == 1 else [pltpu.VMEM((tm, tn), F32)],
        compiler_params=_params(("parallel", "parallel", "arbitrary")),
    )(a, b)


def _rms_fwd(xin, g, res, out_dtype, name):
    s, d = xin.shape
    has_res = res is not None

    def body(*refs):
        x_ref, g_ref = refs[0], refs[1]
        o_ref = refs[-1]
        x = x_ref[...]
        r = lax.rsqrt(jnp.mean(x * x, axis=-1, keepdims=True) + RMS_EPS)
        y = x * r * g_ref[...]
        if has_res:
            y = refs[2][...] + y
        o_ref[...] = y.astype(o_ref.dtype)

    row = pl.BlockSpec((ROW_TILE, d), lambda i: (i, 0))
    vec = pl.BlockSpec((1, d), lambda i: (0, 0))
    ins = [xin, g] + ([res] if has_res else [])
    return pl.pallas_call(
        body, name=name, grid=(s // ROW_TILE,),
        in_specs=[row, vec] + ([row] if has_res else []),
        out_specs=row, out_shape=jax.ShapeDtypeStruct((s, d), out_dtype),
        compiler_params=_params(("parallel",)),
    )(*ins)


def _rms_bwd(xin, g, dy, dres, out_dtype, name):
    s, d = xin.shape
    has_res = dres is not None

    def body(*refs):
        x_ref, g_ref, dy_ref = refs[0], refs[1], refs[2]
        dx_ref, dg_ref = refs[-2], refs[-1]

        @pl.when(pl.program_id(0) == 0)
        def _():
            dg_ref[...] = jnp.zeros_like(dg_ref)

        x = x_ref[...]
        dyv = dy_ref[...].astype(F32)
        r = lax.rsqrt(jnp.mean(x * x, axis=-1, keepdims=True) + RMS_EPS)
        xhat = x * r
        u = dyv * g_ref[...]
        dx = r * (u - xhat * jnp.mean(u * xhat, axis=-1, keepdims=True))
        if has_res:
            dx = refs[3][...] + dx
        dx_ref[...] = dx.astype(dx_ref.dtype)
        dg_ref[...] += jnp.sum(dyv * xhat, axis=0, keepdims=True)

    row = pl.BlockSpec((ROW_TILE, d), lambda i: (i, 0))
    vec = pl.BlockSpec((1, d), lambda i: (0, 0))
    ins = [xin, g, dy] + ([dres] if has_res else [])
    return pl.pallas_call(
        body, name=name, grid=(s // ROW_TILE,),
        in_specs=[row, vec, row] + ([row] if has_res else []),
        out_specs=[row, vec],
        out_shape=[jax.ShapeDtypeStruct((s, d), out_dtype), jax.ShapeDtypeStruct((1, d), F32)],
        compiler_params=_params(("arbitrary",)),
    )(*ins)


def _rms(x):
    r = lax.rsqrt(jnp.mean(x * x, axis=-1, keepdims=True) + RMS_EPS)
    return r, x * r


def _rms_grad(r, xhat, dy, g):
    u = dy * g
    return r * (u - xhat * jnp.mean(u * xhat, axis=-1, keepdims=True))


def _rms_res_pre(sub, g_post, res, g_pre, name):
    s, d = sub.shape

    def body(sub_ref, gp_ref, res_ref, gn_ref, x_ref, n_ref):
        xnew = res_ref[...] + _rms(sub_ref[...])[1] * gp_ref[...]
        x_ref[...] = xnew
        n_ref[...] = (_rms(xnew)[1] * gn_ref[...]).astype(BF16)

    row = pl.BlockSpec((ROW_TILE, d), lambda i: (i, 0))
    vec = pl.BlockSpec((1, d), lambda i: (0, 0))
    return pl.pallas_call(
        body, name=name, grid=(s // ROW_TILE,),
        in_specs=[row, vec, row, vec], out_specs=[row, row],
        out_shape=[jax.ShapeDtypeStruct((s, d), F32), jax.ShapeDtypeStruct((s, d), BF16)],
        compiler_params=_params(("parallel",)),
    )(sub, g_post, res, g_pre)


def _rms_bwd_pair(xmid, g_pre, dn, dres, sub, g_post, name):
    s, d = xmid.shape

    def body(x_ref, gn_ref, dn_ref, dres_ref, sub_ref, gp_ref, dx_ref, dsub_ref, dgn_ref, dgp_ref):
        @pl.when(pl.program_id(0) == 0)
        def _():
            dgn_ref[...] = jnp.zeros_like(dgn_ref)
            dgp_ref[...] = jnp.zeros_like(dgp_ref)

        dnv = dn_ref[...].astype(F32)
        r, xhat = _rms(x_ref[...])
        dx = dres_ref[...] + _rms_grad(r, xhat, dnv, gn_ref[...])
        dx_ref[...] = dx
        dgn_ref[...] += jnp.sum(dnv * xhat, axis=0, keepdims=True)
        rs, shat = _rms(sub_ref[...])
        dsub_ref[...] = _rms_grad(rs, shat, dx, gp_ref[...]).astype(BF16)
        dgp_ref[...] += jnp.sum(dx * shat, axis=0, keepdims=True)

    row = pl.BlockSpec((ROW_TILE, d), lambda i: (i, 0))
    vec = pl.BlockSpec((1, d), lambda i: (0, 0))
    return pl.pallas_call(
        body, name=name, grid=(s // ROW_TILE,),
        in_specs=[row, vec, row, row, row, vec], out_specs=[row, row, vec, vec],
        out_shape=[jax.ShapeDtypeStruct((s, d), F32), jax.ShapeDtypeStruct((s, d), BF16),
                   jax.ShapeDtypeStruct((1, d), F32), jax.ShapeDtypeStruct((1, d), F32)],
        compiler_params=_params(("arbitrary",)),
    )(xmid, g_pre, dn, dres, sub, g_post)


def _loss_head(y, tgt, name):
    s, d = y.shape

    def body(y_ref, t_ref, l_ref, dy_ref):
        @pl.when(pl.program_id(0) == 0)
        def _():
            l_ref[...] = jnp.zeros_like(l_ref)

        e = y_ref[...] - t_ref[...]
        dy_ref[...] = e / d
        per_tok = jnp.mean(e * e, axis=-1, keepdims=True)
        l_ref[...] += 0.5 * jnp.sum(per_tok, axis=0, keepdims=True)

    row = pl.BlockSpec((ROW_TILE, d), lambda i: (i, 0))
    return pl.pallas_call(
        body, name=name, grid=(s // ROW_TILE,),
        in_specs=[row, row],
        out_specs=[pl.BlockSpec((1, 1), lambda i: (0, 0)), row],
        out_shape=[jax.ShapeDtypeStruct((1, 1), F32), jax.ShapeDtypeStruct((s, d), F32)],
        compiler_params=_params(("arbitrary",)),
    )(y, tgt)


def _shift_down(x, k):
    rows = lax.broadcasted_iota(jnp.int32, x.shape, 0)
    return jnp.where(rows >= k, pltpu.roll(x, k, axis=0), 0.0)


def _shift_up(x, k):
    t = x.shape[0]
    rows = lax.broadcasted_iota(jnp.int32, x.shape, 0)
    return jnp.where(rows < t - k, pltpu.roll(x, t - k, axis=0), 0.0)


def _conv3(h, w):
    return w[2:3] * h + w[1:2] * _shift_down(h, 1) + w[0:1] * _shift_down(h, 2)


def _conv3_bwd_x(dc, w):
    return w[2:3] * dc + w[1:2] * _shift_up(dc, 1) + w[0:1] * _shift_up(dc, 2)


def _conv3_bwd_w(dc, h, dw_ref, cols=slice(None)):
    dw_ref[0:1, cols] = jnp.sum(dc * _shift_down(h, 2), axis=0, keepdims=True)
    dw_ref[1:2, cols] = jnp.sum(dc * _shift_down(h, 1), axis=0, keepdims=True)
    dw_ref[2:3, cols] = jnp.sum(dc * h, axis=0, keepdims=True)


FFN_PAIRS = N_DEV // 2


def _lane_chunks(width):
    return [(c0, min(COL_TILE, width - c0)) for c0 in range(0, width, COL_TILE)]


def _ffn_up(n, wup, wdw, name):
    s, d = n.shape
    cw = wup.shape[-1]

    def body(n_ref, wg_ref, wu_ref, dg_ref, du_ref, h_ref, a_ref):
        x = n_ref[...]
        for c0, size in _lane_chunks(cw):
            cols = slice(c0, c0 + size)
            hg = _dot(x, wg_ref[:, cols])
            hu = _dot(x, wu_ref[:, cols])
            h_ref[0, :, cols] = hg.astype(BF16)
            h_ref[1, :, cols] = hu.astype(BF16)
            cg = _conv3(hg, dg_ref[:, cols])
            cu = _conv3(hu, du_ref[:, cols])
            a_ref[:, cols] = (cg * jax.nn.sigmoid(cg) * cu).astype(BF16)

    return pl.pallas_call(
        body, name=name, grid=(FFN_PAIRS,),
        in_specs=[pl.BlockSpec((s, d), lambda j: (0, 0)),
                  pl.BlockSpec((None, d, cw), lambda j: (j, 0, 0)),
                  pl.BlockSpec((None, d, cw), lambda j: (j + FFN_PAIRS, 0, 0)),
                  pl.BlockSpec((None, 3, cw), lambda j: (j, 0, 0)),
                  pl.BlockSpec((None, 3, cw), lambda j: (j + FFN_PAIRS, 0, 0))],
        out_specs=[pl.BlockSpec((None, 2, s, cw), lambda j: (j, 0, 0, 0)),
                   pl.BlockSpec((None, s, cw), lambda j: (j, 0, 0))],
        out_shape=[jax.ShapeDtypeStruct((FFN_PAIRS, 2, s, cw), BF16), jax.ShapeDtypeStruct((FFN_PAIRS, s, cw), BF16)],
        compiler_params=_params(("parallel",)),
    )(n, wup, wup, wdw, wdw)


def _ffn_mid_bwd(do, wdown, h, wdw, name):
    s, d = do.shape
    cw = wdown.shape[1]

    def body(do_ref, wd_ref, h_ref, wg_ref, wu_ref, dh_ref, dwg_ref, dwu_ref):
        dov = do_ref[...]
        for c0, size in _lane_chunks(cw):
            cols = slice(c0, c0 + size)
            da = _dot(dov, wd_ref[cols, :], NT)
            hg = h_ref[0, :, cols].astype(F32)
            hu = h_ref[1, :, cols].astype(F32)
            wg, wu = wg_ref[:, cols], wu_ref[:, cols]
            cg = _conv3(hg, wg)
            cu = _conv3(hu, wu)
            sg = jax.nn.sigmoid(cg)
            dcu = da * (cg * sg)
            dcg = da * cu * (sg * (1.0 + cg * (1.0 - sg)))
            dh_ref[0, :, cols] = _conv3_bwd_x(dcg, wg).astype(BF16)
            dh_ref[1, :, cols] = _conv3_bwd_x(dcu, wu).astype(BF16)
            _conv3_bwd_w(dcg, hg, dwg_ref, cols)
            _conv3_bwd_w(dcu, hu, dwu_ref, cols)

    vec = jax.ShapeDtypeStruct((FFN_PAIRS, 3, cw), F32)
    return pl.pallas_call(
        body, name=name, grid=(FFN_PAIRS,),
        in_specs=[pl.BlockSpec((s, d), lambda j: (0, 0)), pl.BlockSpec((None, cw, d), lambda j: (j, 0, 0)),
                  pl.BlockSpec((None, 2, s, cw), lambda j: (j, 0, 0, 0)),
                  pl.BlockSpec((None, 3, cw), lambda j: (j, 0, 0)),
                  pl.BlockSpec((None, 3, cw), lambda j: (j + FFN_PAIRS, 0, 0))],
        out_specs=[pl.BlockSpec((None, 2, s, cw), lambda j: (j, 0, 0, 0)),
                   pl.BlockSpec((None, 3, cw), lambda j: (j, 0, 0)), pl.BlockSpec((None, 3, cw), lambda j: (j, 0, 0))],
        out_shape=[jax.ShapeDtypeStruct((FFN_PAIRS, 2, s, cw), BF16), vec, vec],
        compiler_params=_params(("parallel",)),
    )(do, wdown, h, wdw, wdw)


def _ffn_dwup(n, dh, name):
    s, d = n.shape
    cw = dh.shape[-1]

    def body(n_ref, dh_ref, o_ref):
        o_ref[...] = _dot(n_ref[...], dh_ref[...], TN).astype(BF16)

    return pl.pallas_call(
        body, name=name, grid=(N_DEV,),
        in_specs=[pl.BlockSpec((s, d), lambda k: (0, 0)),
                  pl.BlockSpec((None, None, s, cw), lambda k: (k % FFN_PAIRS, k // FFN_PAIRS, 0, 0))],
        out_specs=pl.BlockSpec((None, d, cw), lambda k: (k, 0, 0)),
        out_shape=jax.ShapeDtypeStruct((N_DEV, d, cw), BF16),
        compiler_params=_params(("parallel",)),
    )(n, dh)


def _ffn_dn(dh, wup, name, dep=None):
    s, cw = dh.shape[-2:]
    d = wup.shape[1]
    tm = _pick(s, MATMUL_TILES)
    deps = [] if dep is None else [dep]

    def body(dh_ref, w_ref, *rest):
        o_ref, acc_ref = rest[-2:]
        k = pl.program_id(1)

        @pl.when(k == 0)
        def _():
            acc_ref[...] = jnp.zeros_like(acc_ref)

        acc_ref[...] += _dot(dh_ref[...], w_ref[...], NT)

        @pl.when(k == N_DEV - 1)
        def _():
            o_ref[...] = acc_ref[...]

    return pl.pallas_call(
        body, name=name, grid=(s // tm, N_DEV),
        in_specs=[pl.BlockSpec((None, None, tm, cw), lambda i, k: (k % FFN_PAIRS, k // FFN_PAIRS, i, 0)),
                  pl.BlockSpec((None, d, cw), lambda i, k: (k, 0, 0))] + [_ANY] * len(deps),
        out_specs=pl.BlockSpec((tm, d), lambda i, k: (i, 0)),
        out_shape=jax.ShapeDtypeStruct((s, d), F32),
        scratch_shapes=[pltpu.VMEM((tm, d), F32)],
        compiler_params=_params(("parallel", "arbitrary")),
    )(dh, wup, *deps)


def _sconv_fwd(n, win, wdw, name):
    s, d = n.shape
    tn = COL_TILE
    nj = d // tn

    def body(n_ref, wb_ref, wc_ref, wh_ref, dw_ref, z_ref, y_ref):
        x = n_ref[...]
        zb = _dot(x, wb_ref[...])
        zc = _dot(x, wc_ref[...])
        zh = _dot(x, wh_ref[...])
        z_ref[0] = zb.astype(BF16)
        z_ref[1] = zc.astype(BF16)
        z_ref[2] = zh.astype(BF16)
        y_ref[...] = (zb * _conv3(zc * zh, dw_ref[...])).astype(BF16)

    return pl.pallas_call(
        body, name=name, grid=(nj,),
        in_specs=[pl.BlockSpec((s, d), lambda j: (0, 0)),
                  pl.BlockSpec((d, tn), lambda j: (0, j)), pl.BlockSpec((d, tn), lambda j: (0, j + nj)),
                  pl.BlockSpec((d, tn), lambda j: (0, j + 2 * nj)), pl.BlockSpec((3, tn), lambda j: (0, j))],
        out_specs=[pl.BlockSpec((3, s, tn), lambda j: (0, 0, j)), pl.BlockSpec((s, tn), lambda j: (0, j))],
        out_shape=[jax.ShapeDtypeStruct((3, s, d), BF16), jax.ShapeDtypeStruct((s, d), BF16)],
        compiler_params=_params(("parallel",)),
    )(n, win, win, win, wdw)


def _sconv_mid_bwd(dm, wout, z, wdw, name):
    s, d = dm.shape
    tn = COL_TILE
    nj = d // tn

    def body(dm_ref, wo_ref, z_ref, w_ref, dz_ref, dw_ref):
        dy = _dot(dm_ref[...], wo_ref[...], NT)
        zb = z_ref[0].astype(F32)
        zc = z_ref[1].astype(F32)
        zh = z_ref[2].astype(F32)
        w = w_ref[...]
        p = zc * zh
        cp = _conv3(p, w)
        dz_ref[0] = (dy * cp).astype(BF16)
        dcp = dy * zb
        dp = _conv3_bwd_x(dcp, w)
        _conv3_bwd_w(dcp, p, dw_ref)
        dz_ref[1] = (dp * zh).astype(BF16)
        dz_ref[2] = (dp * zc).astype(BF16)

    return pl.pallas_call(
        body, name=name, grid=(nj,),
        in_specs=[pl.BlockSpec((s, d), lambda j: (0, 0)), pl.BlockSpec((tn, d), lambda j: (j, 0)),
                  pl.BlockSpec((3, s, tn), lambda j: (0, 0, j)), pl.BlockSpec((3, tn), lambda j: (0, j))],
        out_specs=[pl.BlockSpec((3, s, tn), lambda j: (0, 0, j)), pl.BlockSpec((3, tn), lambda j: (0, j))],
        out_shape=[jax.ShapeDtypeStruct((3, s, d), BF16), jax.ShapeDtypeStruct((3, d), F32)],
        compiler_params=_params(("parallel",)),
    )(dm, wout, z, wdw)


def _pool_select(g, c2, c4, c8, c16):
    return jnp.where(g == 0, c2, jnp.where(g == 1, c4, jnp.where(g == 2, c8, c16)))


def _pool_inv_count(g, shape):
    pos = lax.broadcasted_iota(jnp.int32, shape, 0).astype(F32) + 1.0
    win = (2 << g).astype(F32)
    return jnp.minimum(pos, win)


def _pool_fwd(n, win, wgrp, scale, name):
    s, d = n.shape
    tn = POOL_GROUP_DIM

    def body(n_ref, wi_ref, wg_ref, sc_ref, p_ref, y_ref):
        g = pl.program_id(0)
        u = _dot(n_ref[...], wi_ref[...])
        s2 = u + _shift_down(u, 1)
        s4 = s2 + _shift_down(s2, 2)
        s8 = s4 + _shift_down(s4, 4)
        s16 = s8 + _shift_down(s8, 8)
        tot = _pool_select(g, s2, s4, s8, s16)
        p = (tot / _pool_inv_count(g, u.shape) - u).astype(BF16)
        p_ref[...] = p
        y_ref[...] = (_dot(p, wg_ref[...]) * sc_ref[...]).astype(BF16)

    return pl.pallas_call(
        body, name=name, grid=(d // tn,),
        in_specs=[pl.BlockSpec((s, d), lambda g: (0, 0)), pl.BlockSpec((d, tn), lambda g: (0, g)),
                  pl.BlockSpec((None, tn, tn), lambda g: (g, 0, 0)), pl.BlockSpec((1, tn), lambda g: (0, g))],
        out_specs=[pl.BlockSpec((s, tn), lambda g: (0, g)), pl.BlockSpec((s, tn), lambda g: (0, g))],
        out_shape=[jax.ShapeDtypeStruct((s, d), BF16), jax.ShapeDtypeStruct((s, d), BF16)],
        compiler_params=_params(("parallel",)),
    )(n, win, wgrp, scale)


def _pool_mid_bwd(dm, wout, p, wgrp, scale, name):
    s, d = dm.shape
    tn = POOL_GROUP_DIM

    def body(dm_ref, wo_ref, p_ref, wg_ref, sc_ref, du_ref, dwg_ref, dsc_ref):
        g = pl.program_id(0)
        dy = _dot(dm_ref[...], wo_ref[...], NT)
        pv = p_ref[...]
        wg = wg_ref[...]
        ypre = _dot(pv, wg)
        dsc_ref[...] = jnp.sum(dy * ypre, axis=0, keepdims=True)
        dypre = (dy * sc_ref[...]).astype(BF16)
        dwg_ref[...] = _dot(pv, dypre, TN)
        dp = _dot(dypre, wg, NT)
        e = dp / _pool_inv_count(g, dp.shape)
        f2 = e + _shift_up(e, 1)
        f4 = f2 + _shift_up(f2, 2)
        f8 = f4 + _shift_up(f4, 4)
        f16 = f8 + _shift_up(f8, 8)
        du_ref[...] = (_pool_select(g, f2, f4, f8, f16) - dp).astype(BF16)

    return pl.pallas_call(
        body, name=name, grid=(d // tn,),
        in_specs=[pl.BlockSpec((s, d), lambda g: (0, 0)), pl.BlockSpec((tn, d), lambda g: (g, 0)),
                  pl.BlockSpec((s, tn), lambda g: (0, g)), pl.BlockSpec((None, tn, tn), lambda g: (g, 0, 0)),
                  pl.BlockSpec((1, tn), lambda g: (0, g))],
        out_specs=[pl.BlockSpec((s, tn), lambda g: (0, g)), pl.BlockSpec((None, tn, tn), lambda g: (g, 0, 0)),
                   pl.BlockSpec((1, tn), lambda g: (0, g))],
        out_shape=[jax.ShapeDtypeStruct((s, d), BF16), jax.ShapeDtypeStruct((4, tn, tn), F32),
                   jax.ShapeDtypeStruct((1, d), F32)],
        compiler_params=_params(("parallel",)),
    )(dm, wout, p, wgrp, scale)


PANEL = LANES
ATTN_EXT = ATTN_WIDTH + PANEL
DVEC_LANE = HEADS


def _alibi_slopes(g, dil):
    all_slopes = 2.0 ** (-8.0 * np.arange(1, N_HEADS_A + 1) / N_HEADS_A)
    return [float(np.float32(sl) * np.float32(dil)) for sl in all_slopes[g * HEADS:(g + 1) * HEADS]]


def _residue_order(a, dil, name):
    s, w = a.shape
    per = ROW_TILE // dil
    panels = w // PANEL

    def body(a_ref, o_ref, *tiles):
        for c in range(panels):
            cols = slice(c * PANEL, (c + 1) * PANEL)
            tiles[c][...] = a_ref[:, cols].astype(F32)
            for r in range(dil):
                o_ref[r, :, cols] = tiles[c][pl.ds(r, per, stride=dil), :].astype(o_ref.dtype)

    out = pl.pallas_call(
        body, name=name, grid=(s // ROW_TILE,),
        in_specs=[pl.BlockSpec((ROW_TILE, w), lambda i: (i, 0))],
        out_specs=pl.BlockSpec((dil, per, w), lambda i: (0, i, 0)),
        out_shape=jax.ShapeDtypeStruct((dil, s // dil, w), a.dtype),
        scratch_shapes=[pltpu.VMEM((ROW_TILE, PANEL), F32)] * panels,
        compiler_params=_params(("parallel",)),
    )(a)
    return out.reshape(s, w)


def _token_order(a, dil, acc, name):
    s, w = a.shape
    per = ROW_TILE // dil
    panels = w // PANEL
    has_acc = acc is not None

    def body(*refs):
        a_ref = refs[0]
        o_ref = refs[2] if has_acc else refs[1]
        tiles = refs[3:] if has_acc else refs[2:]
        for c in range(panels):
            cols = slice(c * PANEL, (c + 1) * PANEL)
            for r in range(dil):
                tiles[c][pl.ds(r, per, stride=dil), :] = a_ref[r, :, cols]
            v = tiles[c][...]
            if has_acc:
                v = v + refs[1][:, cols]
            o_ref[:, cols] = v

    row = pl.BlockSpec((ROW_TILE, w), lambda i: (i, 0))
    return pl.pallas_call(
        body, name=name, grid=(s // ROW_TILE,),
        in_specs=[pl.BlockSpec((dil, per, w), lambda i: (0, i, 0))] + ([row] if has_acc else []),
        out_specs=row, out_shape=jax.ShapeDtypeStruct((s, w), F32),
        scratch_shapes=[pltpu.VMEM((ROW_TILE, PANEL), F32)] * panels,
        compiler_params=_params(("parallel",)),
    )(*([a.reshape(dil, s // dil, w)] + ([acc] if has_acc else [])))


def _qkv_proj(n, wqkv, g, name):
    s, d = n.shape
    tm = _pick(s, MATMUL_TILES)

    def body(a_ref, b_ref, o_ref):
        o_ref[...] = _dot(a_ref[...], b_ref[...]).astype(BF16)

    return pl.pallas_call(
        body, name=name, grid=(s // tm, 3),
        in_specs=[pl.BlockSpec((tm, d), lambda i, t: (i, 0)),
                  pl.BlockSpec((d, ATTN_WIDTH), lambda i, t: (0, 3 * g + t))],
        out_specs=pl.BlockSpec((None, tm, ATTN_WIDTH), lambda i, t: (t, i, 0)),
        out_shape=jax.ShapeDtypeStruct((3, s, ATTN_WIDTH), BF16),
        compiler_params=_params(("parallel", "parallel")),
    )(n, wqkv)


def _attn_window(n, ln):
    if ln == BLOCK:
        return 0, BLOCK
    return pl.multiple_of(jnp.maximum(n - 1, 0) * BLOCK, BLOCK), 2 * BLOCK


def _attn_mask(n, k0, kw):
    qpos = n * BLOCK + lax.broadcasted_iota(jnp.int32, (BLOCK, kw), 0)
    kpos = k0 + lax.broadcasted_iota(jnp.int32, (BLOCK, kw), 1)
    dist = qpos - kpos
    return dist.astype(F32), (dist >= 0) & (dist <= BLOCK)


def _attn_scores(q, keys, slope, dist, valid):
    s = _dot(q, keys, NT) * (HEAD_DIM ** -0.5) - slope * dist
    return jnp.where(valid, s, NEG_INF)


def _attn_fwd(qkv, g, name):
    _, s, w = qkv.shape
    dil = DILATED_CFG[g][1]
    ln = s // dil
    nb = ln // BLOCK
    slopes = _alibi_slopes(g, dil)

    def body(qkv_ref, o_ref):
        n = pl.program_id(1)
        k0, kw = _attn_window(n, ln)
        cur, win = pl.ds(pl.multiple_of(n * BLOCK, BLOCK), BLOCK), pl.ds(k0, kw)
        dist, valid = _attn_mask(n, k0, kw)
        o_ref[:, w:] = jnp.zeros((BLOCK, PANEL), F32)
        for h in range(HEADS):
            cols = slice(h * HEAD_DIM, (h + 1) * HEAD_DIM)
            sc = _attn_scores(qkv_ref[0, cur, cols], qkv_ref[1, win, cols], slopes[h], dist, valid)
            m = jnp.max(sc, axis=-1, keepdims=True)
            p = jnp.exp(sc - m)
            den = jnp.sum(p, axis=-1, keepdims=True)
            o_ref[:, cols] = _dot(p.astype(BF16), qkv_ref[2, win, cols]) / den
            o_ref[:, w + h:w + h + 1] = m + jnp.log(den)

    return pl.pallas_call(
        body, name=name, grid=(dil, nb),
        in_specs=[pl.BlockSpec((3, ln, w), lambda r, n: (0, r, 0))],
        out_specs=pl.BlockSpec((BLOCK, ATTN_EXT), lambda r, n: (r * nb + n, 0)),
        out_shape=jax.ShapeDtypeStruct((s, ATTN_EXT), F32),
        compiler_params=_params(("parallel", "parallel")),
    )(qkv)


def _attn_bwd(qkv, dext, g, name):
    _, s, w = qkv.shape
    dil = DILATED_CFG[g][1]
    ln = s // dil
    nb = ln // BLOCK
    slopes = _alibi_slopes(g, dil)
    scale = HEAD_DIM ** -0.5

    def body(qkv_ref, de_ref, d_ref, dk_ref, dv_ref):
        n = pl.program_id(1)

        @pl.when(n == 0)
        def _():
            dk_ref[...] = jnp.zeros_like(dk_ref)
            dv_ref[...] = jnp.zeros_like(dv_ref)

        k0, kw = _attn_window(n, ln)
        cur, win = pl.ds(pl.multiple_of(n * BLOCK, BLOCK), BLOCK), pl.ds(k0, kw)
        dist, valid = _attn_mask(n, k0, kw)
        for h in range(HEADS):
            cols = slice(h * HEAD_DIM, (h + 1) * HEAD_DIM)
            q, keys = qkv_ref[0, cur, cols], qkv_ref[1, win, cols]
            dob = de_ref[:, cols].astype(BF16)
            p = jnp.exp(_attn_scores(q, keys, slopes[h], dist, valid) - de_ref[:, w + h:w + h + 1])
            dd = de_ref[:, w + DVEC_LANE + h:w + DVEC_LANE + h + 1]
            ds = (p * (_dot(dob, qkv_ref[2, win, cols], NT) - dd)).astype(BF16)
            d_ref[0, cur, cols] = (scale * _dot(ds, keys)).astype(BF16)
            dv_ref[win, cols] += _dot(p.astype(BF16), dob, TN)
            dk_ref[win, cols] += scale * _dot(ds, q, TN)

        @pl.when(n == nb - 1)
        def _():
            d_ref[1] = dk_ref[...].astype(BF16)
            d_ref[2] = dv_ref[...].astype(BF16)

    return pl.pallas_call(
        body, name=name, grid=(dil, nb),
        in_specs=[pl.BlockSpec((3, ln, w), lambda r, n: (0, r, 0)),
                  pl.BlockSpec((BLOCK, ATTN_EXT), lambda r, n: (r * nb + n, 0))],
        out_specs=pl.BlockSpec((3, ln, w), lambda r, n: (0, r, 0)),
        out_shape=jax.ShapeDtypeStruct((3, s, w), BF16),
        scratch_shapes=[pltpu.VMEM((ln, w), F32), pltpu.VMEM((ln, w), F32)],
        compiler_params=_params(("parallel", "arbitrary")),
    )(qkv, dext)


def _attn_merge(e0, e1, e2, name):
    s = e0.shape[0]
    w = ATTN_WIDTH

    def body(e0_ref, e1_ref, e2_ref, m_ref, mb_ref, lse_ref):
        refs = (e0_ref, e1_ref, e2_ref)
        l = [r[:, w:w + HEADS] for r in refs]
        mx = jnp.maximum(jnp.maximum(l[0], l[1]), l[2])
        e = [jnp.exp(v - mx) for v in l]
        z = e[0] + e[1] + e[2]
        lse_ref[...] = mx + jnp.log(z)
        wts = [v / z for v in e]
        for h in range(HEADS):
            cols = slice(h * HEAD_DIM, (h + 1) * HEAD_DIM)
            acc = wts[0][:, h:h + 1] * refs[0][:, cols]
            for g in range(1, N_GROUPS_A):
                acc = acc + wts[g][:, h:h + 1] * refs[g][:, cols]
            m_ref[:, cols] = acc
            mb_ref[:, cols] = acc.astype(BF16)

    ext = pl.BlockSpec((ROW_TILE, ATTN_EXT), lambda i: (i, 0))
    row = pl.BlockSpec((ROW_TILE, w), lambda i: (i, 0))
    return pl.pallas_call(
        body, name=name, grid=(s // ROW_TILE,),
        in_specs=[ext, ext, ext],
        out_specs=[row, row, pl.BlockSpec((ROW_TILE, HEADS), lambda i: (i, 0))],
        out_shape=[jax.ShapeDtypeStruct((s, w), F32), jax.ShapeDtypeStruct((s, w), BF16),
                   jax.ShapeDtypeStruct((s, HEADS), F32)],
        compiler_params=_params(("parallel",)),
    )(e0, e1, e2)


def _attn_dvec(dmerged, merged, lse_all, name, dep=None):
    s, w = merged.shape
    deps = [] if dep is None else [dep]

    def body(dm_ref, m_ref, lse_ref, *rest):
        de_ref = rest[-1]
        dmv = dm_ref[...]
        de_ref[:, :w] = dmv
        de_ref[:, w:] = jnp.zeros((ROW_TILE, PANEL), F32)
        de_ref[:, w:w + HEADS] = lse_ref[...]
        prod = dmv * m_ref[...]
        for h in range(HEADS):
            lane = w + DVEC_LANE + h
            de_ref[:, lane:lane + 1] = jnp.sum(prod[:, h * HEAD_DIM:(h + 1) * HEAD_DIM], axis=-1, keepdims=True)

    row = pl.BlockSpec((ROW_TILE, w), lambda i: (i, 0))
    return pl.pallas_call(
        body, name=name, grid=(s // ROW_TILE,),
        in_specs=[row, row, pl.BlockSpec((ROW_TILE, HEADS), lambda i: (i, 0))] + [_ANY] * len(deps),
        out_specs=pl.BlockSpec((ROW_TILE, ATTN_EXT), lambda i: (i, 0)),
        out_shape=jax.ShapeDtypeStruct((s, ATTN_EXT), F32),
        compiler_params=_params(("parallel",)),
    )(dmerged, merged, lse_all, *deps)


def _attention_fwd(n, wqkv, wo, tag):
    ns, qkvs, exts = [], [], []
    for g, (_, dil) in enumerate(DILATED_CFG):
        ng = n if dil == 1 else _residue_order(n, dil, f"{tag}_order_g{g}")
        qkv = _qkv_proj(ng, wqkv, g, f"{tag}_qkv_g{g}")
        ext = _attn_fwd(qkv, g, f"{tag}_fwd_g{g}")
        ns.append(ng)
        qkvs.append(qkv)
        exts.append(ext if dil == 1 else _token_order(ext, dil, None, f"{tag}_unorder_g{g}"))
    merged, merged_bf, lse_all = _attn_merge(*exts, f"{tag}_merge")
    m = _matmul(merged_bf, wo, "nn", F32, f"{tag}_wo")
    return m, (ns, qkvs, merged, merged_bf, lse_all)


def _attention_bwd(dm, wqkv, wo, saved, tag, dep=None):
    ns, qkvs, merged, merged_bf, lse_all = saved
    d_wo = _matmul(merged_bf, dm, "tn", BF16, f"{tag}_dwo")
    dmerged = _matmul(dm, wo, "nt", F32, f"{tag}_dmerged")
    dext = _attn_dvec(dmerged, merged, lse_all, f"{tag}_dvec", dep)
    width = 3 * ATTN_WIDTH
    d_wqkv, dn = [], None
    for g, (_, dil) in enumerate(DILATED_CFG):
        dext_g = dext if dil == 1 else _residue_order(dext, dil, f"{tag}_dorder_g{g}")
        dqkv = _attn_bwd(qkvs[g], dext_g, g, f"{tag}_bwd_g{g}")
        d_wqkv.append(_matmul(ns[g], dqkv, "tn", BF16, f"{tag}_dwqkv_g{g}", b_parts=3))
        dn_g = _matmul(dqkv, wqkv[:, g * width:(g + 1) * width], "nt", F32, f"{tag}_dn_g{g}", a_parts=3)
        dn = dn_g if dil == 1 else _token_order(dn_g, dil, dn, f"{tag}_dn_sum_g{g}")
    return dn, jnp.concatenate(d_wqkv, axis=1), d_wo


def _layer_matrices(i):
    mixer = (("attn_w_qkv", "attn_w_o"), ("conv_w_in", "conv_w_out"), ("pool_w_in", "pool_w_grp", "pool_w_out"))[i % 3]
    return [(k, i // 3) for k in mixer] + [("ffn_w_up", i), ("ffn_w_down", i)]


def _local_step(x, tgt, vec, weights, sink):
    ng = vec["norm_g"]

    def gain(i, j, token=None):
        g = ng[i, j][None, :]
        return g if token is None else g + token

    saved = []
    n = _rms_fwd(x, gain(0, 0), None, BF16, "norm_first")
    for i in range(DEPTH):
        wl = weights.layer(i)
        t0 = weights.hook(i, 0, n)
        kind, idx = i % 3, i // 3
        if kind == 0:
            m, ms = _attention_fwd(n, wl["attn_w_qkv"], wl["attn_w_o"], "attn")
        elif kind == 1:
            taps = vec["conv_w_dw"][idx] if t0 is None else vec["conv_w_dw"][idx] + t0
            z, y = _sconv_fwd(n, wl["conv_w_in"], taps, "sconv_fwd")
            m = _matmul(y, wl["conv_w_out"], "nn", F32, "sconv_out")
            ms = (z, y)
        else:
            scale = vec["pool_scale"][idx][None, :] if t0 is None else vec["pool_scale"][idx][None, :] + t0
            p, y = _pool_fwd(n, wl["pool_w_in"], wl["pool_w_grp"], scale, "pool_fwd")
            m = _matmul(y, wl["pool_w_out"], "nn", F32, "pool_out")
            ms = (p, y)
        t1 = weights.hook(i, 1, m)
        x1, n2 = _rms_res_pre(m, gain(i, 1, t0), x, gain(i, 2, t1), "norm_res_pre")
        h, a = _ffn_up(n2, wl["ffn_w_up"], vec["ffn_w_dw"][i], "ffn_up")
        t2 = weights.hook(i, 2, a)
        f = _matmul(a, wl["ffn_w_down"].reshape(D_FF, D_MODEL), "nn", F32, "ffn_down", a_parts=FFN_PAIRS)
        saved.append((x, n, m, ms, x1, n2, h, a, f, wl))
        if i + 1 < DEPTH:
            x, n = _rms_res_pre(f, gain(i, 3, t2), x1, gain(i + 1, 0), "norm_res_pre")
        else:
            x = _rms_fwd(f, gain(i, 3), x1, F32, "norm_res")
        weights.hook(i, 3, x)

    loss, dx = _loss_head(x, tgt, "loss_head")

    g_norm = [[None] * 4 for _ in range(DEPTH)]
    g_taps, g_scale, g_ffn_dw = [], [], [None] * DEPTH
    df, g_norm[DEPTH - 1][3] = _rms_bwd(saved[-1][8], gain(DEPTH - 1, 3), dx, None, BF16, "norm_bwd_sub")
    t0 = None
    for i in reversed(range(DEPTH)):
        xin, n, m, ms, x1, n2, h, a, f, wl = saved[i]
        kind, idx = i % 3, i // 3
        gl = {}
        d_wdown = _matmul(a, df, "tn", BF16, "ffn_dwdown", a_parts=FFN_PAIRS)
        gl["ffn_w_down"] = d_wdown.reshape(N_DEV, D_FF // N_DEV, D_MODEL)
        ffn_taps = vec["ffn_w_dw"][i] if t0 is None else vec["ffn_w_dw"][i] + t0
        dh, dwg, dwu = _ffn_mid_bwd(df, wl["ffn_w_down"].reshape(FFN_PAIRS, -1, D_MODEL), h, ffn_taps, "ffn_mid_bwd")
        g_ffn_dw[i] = jnp.concatenate([dwg, dwu], axis=0)
        t1 = sink.hook(i, 1, dh)
        gl["ffn_w_up"] = _ffn_dwup(n2, dh, "ffn_dwup")
        dn2 = _ffn_dn(dh, wl["ffn_w_up"], "ffn_dn", t1)
        dx1, dm, g_norm[i][2], g_norm[i][1] = _rms_bwd_pair(x1, gain(i, 2), dn2, dx, m, gain(i, 1), "norm_bwd_pair")
        t2 = sink.hook(i, 2, dm)
        if kind == 0:
            dn, gl["attn_w_qkv"], gl["attn_w_o"] = _attention_bwd(dm, wl["attn_w_qkv"], wl["attn_w_o"], ms, "attn", t2)
        elif kind == 1:
            z, y = ms
            gl["conv_w_out"] = _matmul(y, dm, "tn", BF16, "sconv_dwout")
            taps = vec["conv_w_dw"][idx] if t2 is None else vec["conv_w_dw"][idx] + t2
            dz, ddw = _sconv_mid_bwd(dm, wl["conv_w_out"], z, taps, "sconv_mid_bwd")
            g_taps.append(ddw)
            gl["conv_w_in"] = _matmul(n, dz, "tn", BF16, "sconv_dwin", b_parts=3)
            dn = _matmul(dz, wl["conv_w_in"], "nt", F32, "sconv_dn", a_parts=3)
        else:
            p, y = ms
            gl["pool_w_out"] = _matmul(y, dm, "tn", BF16, "pool_dwout")
            scale = vec["pool_scale"][idx][None, :] if t2 is None else vec["pool_scale"][idx][None, :] + t2
            du, gl["pool_w_grp"], dscale = _pool_mid_bwd(dm, wl["pool_w_out"], p, wl["pool_w_grp"], scale, "pool_mid_bwd")
            g_scale.append(dscale[0])
            gl["pool_w_in"] = _matmul(n, du, "tn", BF16, "pool_dwin")
            dn = _matmul(du, wl["pool_w_in"], "nt", F32, "pool_dn")
        sink.hook(i, 3, dn)
        if i > 0:
            dx, df, g_norm[i][0], g_norm[i - 1][3] = _rms_bwd_pair(xin, gain(i, 0, t2), dn, dx1, saved[i - 1][8],
                                                                   gain(i - 1, 3), "norm_bwd_pair")
        else:
            dx, g_norm[0][0] = _rms_bwd(xin, gain(0, 0), dn, dx1, F32, "norm_bwd_res")
        t0 = sink.layer_done(i, gl)

    vec_grads = {"norm_g": jnp.stack([jnp.concatenate(row, axis=0) for row in g_norm]), "conv_w_dw": jnp.stack(g_taps),
                 "pool_scale": jnp.stack(g_scale), "ffn_w_dw": g_ffn_dw}
    return loss, dx, vec_grads


_AXES = ("x", "y", "c")
ROUTE_A = ("y", "x", "c")
ROUTE_B = ("x", "y", "c")
def _dev_index(pos):
    return 4 * pos["x"] + 2 * pos["y"] + pos["c"]


_HBM = pl.BlockSpec(memory_space=pltpu.HBM)
_SEM = pl.BlockSpec(memory_space=pltpu.SEMAPHORE)
_ANY = pl.BlockSpec(memory_space=pl.ANY)
_EFFECT = pltpu.SideEffectType.DATAFLOW_SIDE_EFFECTING


TOKEN_SHAPE = (1, D_MODEL)


def _copies_start(describe, arrays, n_copies, name, after, token_shape=TOKEN_SHAPE):
    n = len(arrays)
    deps = [] if after is None else [after]

    def body(*refs):
        send_sems, recv_sems = refs[n + len(deps)], refs[n + len(deps) + 1]
        for c in describe(refs[:n], send_sems, recv_sems):
            c.start()
        refs[-1][...] = jnp.zeros_like(refs[-1])

    outs = pl.pallas_call(
        body, name=f"{name}_start",
        out_shape=(pltpu.SemaphoreType.DMA((n_copies,)), pltpu.SemaphoreType.DMA((n_copies,)),
                   *[pltpu.HBM(a.shape, a.dtype) for a in arrays], jax.ShapeDtypeStruct(token_shape, F32)),
        in_specs=[_HBM] * n + [_ANY] * len(deps),
        out_specs=(_SEM, _SEM, *([_HBM] * n), pl.BlockSpec(memory_space=pltpu.VMEM)),
        input_output_aliases={i: 2 + i for i in range(n)},
        compiler_params=pltpu.CompilerParams(has_side_effects=_EFFECT),
    )(*[pltpu.with_memory_space_constraint(a, pltpu.HBM) for a in arrays], *deps)
    return (outs[0], outs[1], list(outs[2:2 + n])), outs[-1]


def _copies_wait(describe, handle, name, after):
    send_sems, recv_sems, arrays = handle
    n = len(arrays)
    deps = [] if after is None else [after]

    def body(*refs):
        for c in describe(refs[:n], refs[n], refs[n + 1]):
            c.wait_send()
            c.wait_recv()

    outs = pl.pallas_call(
        body, name=f"{name}_wait",
        out_shape=tuple(pltpu.HBM(a.shape, a.dtype) for a in arrays),
        in_specs=[_HBM] * n + [_SEM, _SEM] + [_ANY] * len(deps), out_specs=tuple([_HBM] * n),
        input_output_aliases={i: i for i in range(n)},
        compiler_params=pltpu.CompilerParams(has_side_effects=_EFFECT),
    )(*arrays, send_sems, recv_sems, *deps)
    return list(outs)


GATHER_STAGE_COPIES = (3, 3, 1)


def _gather_copies(stage, routes):
    n = len(routes)

    def describe(refs, send_sems, recv_sems):
        pos = {a: lax.axis_index(a) for a in _AXES}

        def flipped(axes):
            return {a: 1 - pos[a] if a in axes else pos[a] for a in _AXES}

        copies = []
        for i, (a1, a2, a3) in enumerate(routes):
            land = refs[n + i] if stage == 1 else refs[i]
            p1, p2, p12, p3 = flipped((a1,)), flipped((a2,)), flipped((a1, a2)), flipped((a3,))
            plan = {1: [(None, p1), (None, p2), (None, p3)], 2: [(p1, p2), (p1, p3), (p2, p3)], 3: [(p12, p3)]}[stage]
            for holder, to in plan:
                slot = land.at[_dev_index(pos if holder is None else holder)]
                k = len(copies)
                copies.append(pltpu.make_async_remote_copy(
                    src_ref=refs[i] if holder is None else slot, dst_ref=slot,
                    send_sem=send_sems.at[k], recv_sem=recv_sems.at[k],
                    device_id=tuple(to[a] for a in _AXES), device_id_type=pl.DeviceIdType.MESH))
        return copies

    return describe


def _gather_begin(shards, routes, name, after):
    n = len(shards)
    lands = [lax.empty((N_DEV,) + a.shape, a.dtype) for a in shards]
    handle, token = _copies_start(_gather_copies(1, routes), list(shards) + lands, GATHER_STAGE_COPIES[0] * n,
                                  f"{name}_1", after)
    return {"stage": 1, "handle": handle, "routes": routes, "name": name, "n": n}, token


def _gather_next(state, after):
    stage, routes, name, n = state["stage"], state["routes"], state["name"], state["n"]
    arrays = _copies_wait(_gather_copies(stage, routes), state["handle"], f"{name}_{stage}", after)
    if stage == 1:
        state = dict(state, shards=arrays[:n])
        arrays = arrays[n:]
    if stage == 3:
        me = _dev_index({a: lax.axis_index(a) for a in _AXES})
        return [lax.dynamic_update_index_in_dim(o, s, me, 0) for o, s in zip(arrays, state["shards"])], None
    handle, token = _copies_start(_gather_copies(stage + 1, routes), arrays, GATHER_STAGE_COPIES[stage] * n,
                                  f"{name}_{stage + 1}", None)
    return dict(state, stage=stage + 1, handle=handle), token


ADD_ROW_TILES = (1024, 704, 512, 352, 256, 128, 96, 64, 32, 16)


def _add_half(a, recv, me, out_dtype, name):
    p, q, cols = recv.shape
    tr = _pick(q, ADD_ROW_TILES)

    def body(me_ref, a_ref, b_ref, o_ref):
        o_ref[...] = (a_ref[...].astype(F32) + b_ref[...].astype(F32)).astype(o_ref.dtype)

    return pl.pallas_call(
        body, name=name,
        grid_spec=pltpu.PrefetchScalarGridSpec(
            num_scalar_prefetch=1, grid=(p, q // tr),
            in_specs=[pl.BlockSpec((None, None, tr, cols), lambda j, i, m: (j, m[0], i, 0)),
                      pl.BlockSpec((None, tr, cols), lambda j, i, m: (j, i, 0))],
            out_specs=pl.BlockSpec((None, tr, cols), lambda j, i, m: (j, i, 0))),
        out_shape=jax.ShapeDtypeStruct((p, q, cols), out_dtype),
        compiler_params=_params(("parallel", "parallel")),
    )(me, a, recv)


def _half_copies(axes):
    n = len(axes)

    def describe(refs, send_sems, recv_sems):
        pos = {a: lax.axis_index(a) for a in _AXES}
        copies = []
        for i, axis in enumerate(axes):
            peer = tuple(1 - pos[a] if a == axis else pos[a] for a in _AXES)
            copies.append(pltpu.make_async_remote_copy(
                src_ref=refs[i].at[:, 1 - pos[axis]], dst_ref=refs[n + i], send_sem=send_sems.at[i],
                recv_sem=recv_sems.at[i], device_id=peer, device_id_type=pl.DeviceIdType.MESH))
        return copies

    return describe


def _scatter_begin(slots, routes, tags, name, token_shape=TOKEN_SHAPE):
    shapes = [a.shape[1:] for a in slots]
    rows = [math.prod(s[:-1]) for s in shapes]
    arrays = [a.reshape(4, 2, n, s[-1]) for a, n, s in zip(slots, rows, shapes)]
    return _scatter_start({"stage": 0, "arrays": arrays, "routes": routes, "tags": tags, "name": name,
                           "shapes": shapes, "rows": rows}, token_shape)


def _scatter_start(state, token_shape=TOKEN_SHAPE):
    stage, arrays = state["stage"], state["arrays"]
    axes = [r[2 - stage] for r in state["routes"]]
    lands = [lax.empty((a.shape[0],) + a.shape[2:], a.dtype) for a in arrays]
    handle, token = _copies_start(_half_copies(axes), arrays + lands, len(arrays), f"{state['name']}_{stage + 1}", None,
                                  token_shape)
    return dict(state, handle=handle, axes=axes), token


def _scatter_next(state, after):
    stage, axes, n = state["stage"], state["axes"], len(state["arrays"])
    both = _copies_wait(_half_copies(axes), state["handle"], f"{state['name']}_{stage + 1}", after)
    coord = {a: lax.axis_index(a).astype(jnp.int32).reshape(1) for a in _AXES}
    sums = [_add_half(a, r, coord[ax], F32 if stage == 2 else BF16, f"scatter_add_{stage + 1}_{t}")
            for a, r, ax, t in zip(both[:n], both[n:], axes, state["tags"])]
    if stage == 2:
        return [a.reshape(s) for a, s in zip(sums, state["shapes"])], None
    if stage == 0:
        views = [(1, 2, 2 * r, s[-1]) if route[1] == "x" else (2, 2, r, s[-1])
                 for r, s, route in zip(state["rows"], state["shapes"], state["routes"])]
    else:
        views = [(1, 2, r, s[-1]) for r, s in zip(state["rows"], state["shapes"])]
    return _scatter_start(dict(state, stage=stage + 1, arrays=[a.reshape(v) for a, v in zip(sums, views)]))


_WEIGHTS = {
    "norm_g": ((DEPTH, 4, D_MODEL), 2, True),
    "attn_w_qkv": ((2, D_MODEL, 4608), 2, False),
    "attn_w_o": ((2, ATTN_WIDTH, D_MODEL), 2, False),
    "conv_w_in": ((1, D_MODEL, 3 * D_MODEL), 2, False),
    "conv_w_dw": ((1, 3, D_MODEL), 2, True),
    "conv_w_out": ((1, D_MODEL, D_MODEL), 1, False),
    "pool_w_in": ((1, D_MODEL, D_MODEL), 1, False),
    "pool_w_grp": ((1, 4, POOL_GROUP_DIM, POOL_GROUP_DIM), 2, False),
    "pool_scale": ((1, D_MODEL), 1, True),
    "pool_w_out": ((1, D_MODEL, D_MODEL), 1, False),
    "ffn_w_up": ((DEPTH, D_MODEL, 2 * D_FF), 2, False),
    "ffn_w_dw": ((DEPTH, 3, 2 * D_FF), 2, True),
    "ffn_w_down": ((DEPTH, D_FF, D_MODEL), 1, False),
}
_NAMES = tuple(_WEIGHTS)
_VECTORS = tuple(k for k in _NAMES if _WEIGHTS[k][2])
_MATRICES = tuple(k for k in _NAMES if not _WEIGHTS[k][2])
_FFN = ("ffn_w_up", "ffn_w_down")
_ON_ROUTE_A = ("ffn_w_up", "attn_w_o", "conv_w_out", "pool_w_in")
PACK_ROWS = 16


def _route(name):
    return ROUTE_A if name in _ON_ROUTE_A else ROUTE_B


def _shard_shape(name):
    shape, ax, _ = _WEIGHTS[name]
    return tuple(s // N_DEV if i == ax else s for i, s in enumerate(shape))


def _full_from_slots(slots, name, layers=None):
    shape, ax, _ = _WEIGHTS[name]
    if layers is not None:
        shape = (layers,) + shape[1:]
    return jnp.moveaxis(slots, 0, ax).reshape(shape)


def _slots_from_full(full, name):
    shape, ax, _ = _WEIGHTS[name]
    split = shape[:ax] + (N_DEV, shape[ax] // N_DEV) + shape[ax + 1:]
    return jnp.moveaxis(full.reshape(split), ax, 0)


def _pack_vectors(parts, lead):
    rows = []
    for k in _VECTORS:
        r = parts[k].reshape(lead + (-1, LANES))
        pad = -r.shape[-2] % PACK_ROWS
        rows.append(jnp.pad(r, [(0, 0)] * len(lead) + [(0, pad), (0, 0)]))
    return jnp.concatenate(rows, axis=len(lead))


def _unpack_vectors(buf, lead):
    out, r0 = {}, 0
    for k in _VECTORS:
        shard = _shard_shape(k)
        rows = math.prod(shard) // LANES
        out[k] = buf[..., r0:r0 + rows, :].reshape(lead + shard)
        r0 += rows + (-rows % PACK_ROWS)
    return out


class _LayerWeights:
    def __init__(self, shards):
        self.cast = {k: shards[k].astype(BF16) for k in _MATRICES}
        first, ffn0 = _layer_matrices(0)[:-2], _layer_matrices(0)[-2:]
        state, _ = _gather_begin(self._send(first) + [_pack_vectors(shards, ())],
                                 [_route(k) for k, _ in first] + [ROUTE_B], "gather0", None)
        for _ in range(2):
            state, _ = _gather_next(state, None)
        outs, _ = _gather_next(state, None)
        vec = _unpack_vectors(outs[-1], (N_DEV,))
        self.vec = {k: _full_from_slots(vec[k], k) for k in _VECTORS}
        self.vec["ffn_w_dw"] = [vec["ffn_w_dw"][:, l] for l in range(DEPTH)]
        self.ready = {0: self._unpack(first, outs[:-1])}
        self.chains = {}
        tokens = []
        self._begin("ffn0", ffn0, "gather0f", outs[0], tokens)
        self._begin(1, _layer_matrices(1), "gather1", outs[0], tokens)
        self.vec["norm_g"] = self.vec["norm_g"] + (tokens[0] + tokens[1])

    def _send(self, items):
        return [self.cast[k][j] for k, j in items]

    @staticmethod
    def _unpack(items, outs):
        return {k: o if k in _FFN else _full_from_slots(o[:, None], k, layers=1)[0] for (k, _), o in zip(items, outs)}

    def _begin(self, key, items, name, after, tokens):
        state, token = _gather_begin(self._send(items), [_route(k) for k, _ in items], name, after)
        self.chains[key] = (items, state)
        tokens.append(token)

    def _advance(self, key, after, tokens):
        items, state = self.chains.pop(key)
        state, token = _gather_next(state, after)
        if token is None:
            self.ready.setdefault(0 if key == "ffn0" else key, {}).update(self._unpack(items, state))
        else:
            self.chains[key] = (items, state)
            tokens.append(token)

    def layer(self, i):
        return self.ready[i]

    def hook(self, i, point, after):
        tokens = []
        if i == 0 and point == 0:
            self._advance("ffn0", after, tokens)
        if i == 0 and point == 1:
            self._advance("ffn0", after, tokens)
            self._advance("ffn0", None, tokens)
        if point >= 1 and i + 1 in self.chains:
            self._advance(i + 1, after, tokens)
        if point == 2 and i + 2 < DEPTH:
            self._begin(i + 2, _layer_matrices(i + 2), f"gather{i + 2}", after, tokens)
        return functools.reduce(lambda a, b: a + b, tokens) if tokens else None


def _layer_slots(g, name):
    shape, ax, _ = _WEIGHTS[name]
    shape, ax = shape[1:], ax - 1
    split = shape[:ax] + (N_DEV, shape[ax] // N_DEV) + shape[ax + 1:]
    return jnp.moveaxis(g.reshape(split), ax, 0).astype(BF16)


class _GradSink:
    def __init__(self):
        self.state = None
        self.sums = {}
        self.last = None

    def layer_done(self, i, grads):
        items = _layer_matrices(i)
        slots = [grads[k] if k in _FFN else _layer_slots(grads[k], k) for k, _ in items]
        if i == 0:
            self.last = (items, slots)
            return None
        self.items = items
        self.state, token = _scatter_begin(slots, [_route(k) for k, _ in items], [f"{k}{j}" for k, j in items],
                                           f"scatter{i}", (N_DEV, 3, 2 * D_FF // N_DEV))
        return token

    def hook(self, i, point, after):
        if self.state is None:
            return None
        self.state, token = _scatter_next(self.state, after)
        if point == 3:
            self.sums.update(dict(zip(self.items, self.state)))
            self.state = None
        return token


def _adamw(w, g, m, v, name, layer=None, prev=None):
    shape = w.shape
    cols = shape[-1]
    view = shape if len(shape) == 3 else (1, math.prod(shape[:-1]), cols)
    layers, rows, _ = view
    tr = _pick(rows, (512, 256, 128, 64, 32, 16, 8))
    n_prev = 0 if prev is None else 3

    def body(*refs):
        w_ref, g_ref, m_ref, v_ref = refs[n_prev:n_prev + 4]
        d_ref, nm_ref, nv_ref = refs[n_prev + 4:]
        gv = g_ref[...]
        nm = ADAM_B1 * m_ref[...] + (1.0 - ADAM_B1) * gv
        nv = ADAM_B2 * v_ref[...] + (1.0 - ADAM_B2) * jnp.square(gv)
        m_hat = nm / (1.0 - ADAM_B1 ** ADAM_STEP)
        v_hat = nv / (1.0 - ADAM_B2 ** ADAM_STEP)
        d_ref[...] = -ADAM_LR * (m_hat / (jnp.sqrt(v_hat) + ADAM_EPS) + ADAM_WD * w_ref[...])
        nm_ref[...] = nm
        nv_ref[...] = nv

    if layer is None:
        grid = (layers, rows // tr)
        blk = gblk = pl.BlockSpec((None, tr, cols), lambda l, i: (l, i, 0))
        gview = view
    else:
        grid = (rows // tr,)
        blk = pl.BlockSpec((None, tr, cols), lambda i: (layer, i, 0))
        gblk = pl.BlockSpec((tr, cols), lambda i: (i, 0))
        gview = (rows, cols)
    shp = jax.ShapeDtypeStruct(view, F32)
    outs = pl.pallas_call(
        body, name=name, grid=grid, in_specs=[_ANY] * n_prev + [blk, gblk, blk, blk], out_specs=[blk] * 3,
        out_shape=[shp] * 3, input_output_aliases={i: i for i in range(n_prev)},
        compiler_params=_params(("parallel",) * len(grid)),
    )(*([] if prev is None else [p.reshape(view) for p in prev]), w.reshape(view), g.reshape(gview), m.reshape(view),
      v.reshape(view))
    return [o.reshape(shape) for o in outs]


def kernel(x, norm_g, attn_w_qkv, attn_w_o, conv_w_in, conv_w_dw, conv_w_out, pool_w_in, pool_w_grp, pool_scale, pool_w_out, ffn_w_up, ffn_w_dw, ffn_w_down, loss_target, m_norm_g, m_attn_w_qkv, m_attn_w_o, m_conv_w_in, m_conv_w_dw, m_conv_w_out, m_pool_w_in, m_pool_w_grp, m_pool_scale, m_pool_w_out, m_ffn_w_up, m_ffn_w_dw, m_ffn_w_down, v_norm_g, v_attn_w_qkv, v_attn_w_o, v_conv_w_in, v_conv_w_dw, v_conv_w_out, v_pool_w_in, v_pool_w_grp, v_pool_scale, v_pool_w_out, v_ffn_w_up, v_ffn_w_dw, v_ffn_w_down):
    shards = dict(zip(_NAMES, (norm_g, attn_w_qkv, attn_w_o, conv_w_in, conv_w_dw, conv_w_out, pool_w_in,
                               pool_w_grp, pool_scale, pool_w_out, ffn_w_up, ffn_w_dw, ffn_w_down)))
    moms = dict(zip(_NAMES, (m_norm_g, m_attn_w_qkv, m_attn_w_o, m_conv_w_in, m_conv_w_dw, m_conv_w_out,
                             m_pool_w_in, m_pool_w_grp, m_pool_scale, m_pool_w_out, m_ffn_w_up, m_ffn_w_dw,
                             m_ffn_w_down)))
    vels = dict(zip(_NAMES, (v_norm_g, v_attn_w_qkv, v_attn_w_o, v_conv_w_in, v_conv_w_dw, v_conv_w_out,
                             v_pool_w_in, v_pool_w_grp, v_pool_scale, v_pool_w_out, v_ffn_w_up, v_ffn_w_dw,
                             v_ffn_w_down)))
    weights = _LayerWeights(shards)
    sink = _GradSink()
    loss, grad_x, vec_grads = _local_step(x[0], loss_target[0], weights.vec, weights, sink)
    loss = lax.psum(loss[0, 0], _AXES)

    items, slots = sink.last
    vec_slots = {k: _slots_from_full(vec_grads[k], k) for k in _VECTORS if k != "ffn_w_dw"}
    vec_slots["ffn_w_dw"] = jnp.stack(vec_grads["ffn_w_dw"], axis=1)
    state, _ = _scatter_begin(slots + [_pack_vectors(vec_slots, (N_DEV,)).astype(BF16)],
                              [_route(k) for k, _ in items] + [ROUTE_B], [f"{k}{j}" for k, j in items] + ["vectors"],
                              "scatter0")
    results = {}

    def step_layer(i):
        last = None
        for k, j in _layer_matrices(i):
            g = sink.sums[(k, j)]
            if _WEIGHTS[k][0][0] == 1:
                results[k] = (g[None], _adamw(shards[k], g[None], moms[k], vels[k], f"adamw_{k}"))
            else:
                gs, prev = results.get(k, ({}, None))
                gs[j] = g
                results[k] = (gs, _adamw(shards[k], g, moms[k], vels[k], f"adamw_{k}{j}", layer=j, prev=prev))
            last = results[k][1][0]
        return last

    for i in (3, 2):
        state, _ = _scatter_next(state, step_layer(i))
    sums, _ = _scatter_next(state, step_layer(1))
    sink.sums.update(dict(zip(items, sums[:-1])))
    step_layer(0)
    vec_sums = _unpack_vectors(sums[-1], ())
    for k in _VECTORS:
        results[k] = (vec_sums[k], _adamw(shards[k], vec_sums[k], moms[k], vels[k], f"adamw_{k}"))
    grads_out = {k: g if not isinstance(g, dict) else jnp.stack([g[j] for j in range(len(g))])
                 for k, (g, _) in results.items()}
    return (loss, grad_x[None], *[grads_out[k] for k in _NAMES], *[results[k][1][0] for k in _NAMES],
            *[results[k][1][1] for k in _NAMES], *[results[k][1][2] for k in _NAMES])
```

```python
import functools
import math

import numpy as np
import jax
import jax.numpy as jnp
from jax import lax
from jax.experimental import pallas as pl
from jax.experimental.pallas import tpu as pltpu

F32, BF16 = jnp.float32, jnp.bfloat16

D_MODEL = 1024
SEQ = 2048
DEPTH = 4
DILATED_CFG = ((128, 1), (512, 4), (2048, 16))
N_GROUPS_A = 3
HEADS = 8
HEAD_DIM = 64
ATTN_WIDTH = HEADS * HEAD_DIM
N_HEADS_A = N_GROUPS_A * HEADS
BLOCK = 128
NEG_INF = -1e30
POOL_GROUP_DIM = 256
D_FF = 2816
RMS_EPS = 1e-6
ADAM_LR, ADAM_B1, ADAM_B2, ADAM_EPS, ADAM_WD, ADAM_STEP = 0.001, 0.9, 0.999, 1e-08, 0.01, 10

N_DEV = 8
LANES = 128
V7X_VMEM_BYTES = 64 * 2 ** 20
VMEM_LIMIT_BYTES = V7X_VMEM_BYTES - 8 * 2 ** 20
COL_TILE = 256
ROW_TILE = 256
MATMUL_TILES = (1024, 1408, 512, 256, 128)
TN_RESIDENT_K = 2048

NN = (((1,), (0,)), ((), ()))
NT = (((1,), (1,)), ((), ()))
TN = (((0,), (0,)), ((), ()))


def _dot(a, b, dims=NN):
    return lax.dot_general(a, b, dims, preferred_element_type=F32)


def _params(sem=None):
    return pltpu.CompilerParams(dimension_semantics=sem, vmem_limit_bytes=VMEM_LIMIT_BYTES)


def _pick(n, prefs):
    for p in prefs:
        if n % p == 0:
            return p
    return n


def _matmul(a, b, mode, out_dtype, name, a_parts=1, b_parts=1):
    if mode == "nn":
        m, k = a.shape[-2], a.shape[-1] * a_parts
        n = b.shape[-1] * b_parts
    elif mode == "nt":
        m, k = a.shape[-2], a.shape[-1] * a_parts
        n = b.shape[-2]
    else:
        k, m = a.shape[-2], a.shape[-1] * a_parts
        n = b.shape[-1] * b_parts
    tm = _pick(m, MATMUL_TILES)
    tn = _pick(n // b_parts if mode != "nt" else n, MATMUL_TILES)
    kk = k // a_parts if mode != "tn" else k
    tk = _pick(kk, MATMUL_TILES)
    if mode == "tn":
        tm = _pick(m // a_parts, MATMUL_TILES)
        if k <= TN_RESIDENT_K:
            tk = k
    gm, gn, gk = m // tm, n // tn, k // tk

    def a_idx(i, j, kq):
        if mode == "tn":
            r, c, per = kq, i, (m // a_parts) // tm
        else:
            r, c, per = i, kq, (k // a_parts) // tk
        return (r, c) if a_parts == 1 else (c // per, r, c % per)

    def b_idx(i, j, kq):
        if mode == "nt":
            return (j, kq)
        per = (n // b_parts) // tn
        return (kq, j) if b_parts == 1 else (j // per, kq, j % per)

    a_blk = (tk, tm) if mode == "tn" else (tm, tk)
    b_blk = (tn, tk) if mode == "nt" else (tk, tn)
    if a_parts > 1:
        a_blk = (None,) + a_blk
    if b_parts > 1:
        b_blk = (None,) + b_blk
    dims = {"nn": NN, "nt": NT, "tn": TN}[mode]

    def body_single(a_ref, b_ref, o_ref):
        o_ref[...] = _dot(a_ref[...], b_ref[...], dims).astype(o_ref.dtype)

    def body(a_ref, b_ref, o_ref, acc_ref):
        kq = pl.program_id(2)

        @pl.when(kq == 0)
        def _():
            acc_ref[...] = jnp.zeros_like(acc_ref)

        acc_ref[...] += _dot(a_ref[...], b_ref[...], dims)

        @pl.when(kq == gk - 1)
        def _():
            o_ref[...] = acc_ref[...].astype(o_ref.dtype)

    return pl.pallas_call(
        body_single if gk == 1 else body, name=name, grid=(gm, gn, gk),
        in_specs=[pl.BlockSpec(a_blk, a_idx), pl.BlockSpec(b_blk, b_idx)],
        out_specs=pl.BlockSpec((tm, tn), lambda i, j, kq: (i, j)),
        out_shape=jax.ShapeDtypeStruct((m, n), out_dtype),
        scratch_shapes=[] if gk == 1 else [pltpu.VMEM((tm, tn), F32)],
        compiler_params=_params(("parallel", "parallel", "arbitrary")),
    )(a, b)


def _rms_fwd(xin, g, res, out_dtype, name):
    s, d = xin.shape
    has_res = res is not None

    def body(*refs):
        x_ref, g_ref = refs[0], refs[1]
        o_ref = refs[-1]
        x = x_ref[...]
        r = lax.rsqrt(jnp.mean(x * x, axis=-1, keepdims=True) + RMS_EPS)
        y = x * r * g_ref[...]
        if has_res:
            y = refs[2][...] + y
        o_ref[...] = y.astype(o_ref.dtype)

    row = pl.BlockSpec((ROW_TILE, d), lambda i: (i, 0))
    vec = pl.BlockSpec((1, d), lambda i: (0, 0))
    ins = [xin, g] + ([res] if has_res else [])
    return pl.pallas_call(
        body, name=name, grid=(s // ROW_TILE,),
        in_specs=[row, vec] + ([row] if has_res else []),
        out_specs=row, out_shape=jax.ShapeDtypeStruct((s, d), out_dtype),
        compiler_params=_params(("parallel",)),
    )(*ins)


def _rms_bwd(xin, g, dy, dres, out_dtype, name):
    s, d = xin.shape
    has_res = dres is not None

    def body(*refs):
        x_ref, g_ref, dy_ref = refs[0], refs[1], refs[2]
        dx_ref, dg_ref = refs[-2], refs[-1]

        @pl.when(pl.program_id(0) == 0)
        def _():
            dg_ref[...] = jnp.zeros_like(dg_ref)

        x = x_ref[...]
        dyv = dy_ref[...].astype(F32)
        r = lax.rsqrt(jnp.mean(x * x, axis=-1, keepdims=True) + RMS_EPS)
        xhat = x * r
        u = dyv * g_ref[...]
        dx = r * (u - xhat * jnp.mean(u * xhat, axis=-1, keepdims=True))
        if has_res:
            dx = refs[3][...] + dx
        dx_ref[...] = dx.astype(dx_ref.dtype)
        dg_ref[...] += jnp.sum(dyv * xhat, axis=0, keepdims=True)

    row = pl.BlockSpec((ROW_TILE, d), lambda i: (i, 0))
    vec = pl.BlockSpec((1, d), lambda i: (0, 0))
    ins = [xin, g, dy] + ([dres] if has_res else [])
    return pl.pallas_call(
        body, name=name, grid=(s // ROW_TILE,),
        in_specs=[row, vec, row] + ([row] if has_res else []),
        out_specs=[row, vec],
        out_shape=[jax.ShapeDtypeStruct((s, d), out_dtype), jax.ShapeDtypeStruct((1, d), F32)],
        compiler_params=_params(("arbitrary",)),
    )(*ins)


def _rms(x):
    r = lax.rsqrt(jnp.mean(x * x, axis=-1, keepdims=True) + RMS_EPS)
    return r, x * r


def _rms_grad(r, xhat, dy, g):
    u = dy * g
    return r * (u - xhat * jnp.mean(u * xhat, axis=-1, keepdims=True))


def _rms_res_pre(sub, g_post, res, g_pre, name):
    s, d = sub.shape

    def body(sub_ref, gp_ref, res_ref, gn_ref, x_ref, n_ref):
        xnew = res_ref[...] + _rms(sub_ref[...])[1] * gp_ref[...]
        x_ref[...] = xnew
        n_ref[...] = (_rms(xnew)[1] * gn_ref[...]).astype(BF16)

    row = pl.BlockSpec((ROW_TILE, d), lambda i: (i, 0))
    vec = pl.BlockSpec((1, d), lambda i: (0, 0))
    return pl.pallas_call(
        body, name=name, grid=(s // ROW_TILE,),
        in_specs=[row, vec, row, vec], out_specs=[row, row],
        out_shape=[jax.ShapeDtypeStruct((s, d), F32), jax.ShapeDtypeStruct((s, d), BF16)],
        compiler_params=_params(("parallel",)),
    )(sub, g_post, res, g_pre)


def _rms_bwd_pair(xmid, g_pre, dn, dres, sub, g_post, name):
    s, d = xmid.shape

    def body(x_ref, gn_ref, dn_ref, dres_ref, sub_ref, gp_ref, dx_ref, dsub_ref, dgn_ref, dgp_ref):
        @pl.when(pl.program_id(0) == 0)
        def _():
            dgn_ref[...] = jnp.zeros_like(dgn_ref)
            dgp_ref[...] = jnp.zeros_like(dgp_ref)

        dnv = dn_ref[...].astype(F32)
        r, xhat = _rms(x_ref[...])
        dx = dres_ref[...] + _rms_grad(r, xhat, dnv, gn_ref[...])
        dx_ref[...] = dx
        dgn_ref[...] += jnp.sum(dnv * xhat, axis=0, keepdims=True)
        rs, shat = _rms(sub_ref[...])
        dsub_ref[...] = _rms_grad(rs, shat, dx, gp_ref[...]).astype(BF16)
        dgp_ref[...] += jnp.sum(dx * shat, axis=0, keepdims=True)

    row = pl.BlockSpec((ROW_TILE, d), lambda i: (i, 0))
    vec = pl.BlockSpec((1, d), lambda i: (0, 0))
    return pl.pallas_call(
        body, name=name, grid=(s // ROW_TILE,),
        in_specs=[row, vec, row, row, row, vec], out_specs=[row, row, vec, vec],
        out_shape=[jax.ShapeDtypeStruct((s, d), F32), jax.ShapeDtypeStruct((s, d), BF16),
                   jax.ShapeDtypeStruct((1, d), F32), jax.ShapeDtypeStruct((1, d), F32)],
        compiler_params=_params(("arbitrary",)),
    )(xmid, g_pre, dn, dres, sub, g_post)


def _loss_head(y, tgt, name):
    s, d = y.shape

    def body(y_ref, t_ref, l_ref, dy_ref):
        @pl.when(pl.program_id(0) == 0)
        def _():
            l_ref[...] = jnp.zeros_like(l_ref)

        e = y_ref[...] - t_ref[...]
        dy_ref[...] = e / d
        per_tok = jnp.mean(e * e, axis=-1, keepdims=True)
        l_ref[...] += 0.5 * jnp.sum(per_tok, axis=0, keepdims=True)

    row = pl.BlockSpec((ROW_TILE, d), lambda i: (i, 0))
    return pl.pallas_call(
        body, name=name, grid=(s // ROW_TILE,),
        in_specs=[row, row],
        out_specs=[pl.BlockSpec((1, 1), lambda i: (0, 0)), row],
        out_shape=[jax.ShapeDtypeStruct((1, 1), F32), jax.ShapeDtypeStruct((s, d), F32)],
        compiler_params=_params(("arbitrary",)),
    )(y, tgt)


SUBLANES = 8


def _shift_down(x, k):
    y = pltpu.roll(x, k, axis=0)
    rows = lax.broadcasted_iota(jnp.int32, (SUBLANES, x.shape[1]), 0)
    return jnp.concatenate([jnp.where(rows >= k, y[:SUBLANES], 0.0), y[SUBLANES:]], axis=0)


def _shift_up(x, k):
    t = x.shape[0]
    y = pltpu.roll(x, t - k, axis=0)
    rows = lax.broadcasted_iota(jnp.int32, (SUBLANES, x.shape[1]), 0)
    return jnp.concatenate([y[:t - SUBLANES], jnp.where(rows < SUBLANES - k, y[t - SUBLANES:], 0.0)], axis=0)


def _conv3(h, w):
    return w[2:3] * h + w[1:2] * _shift_down(h, 1) + w[0:1] * _shift_down(h, 2)


def _conv3_bwd(dc, h, w, dw_ref, cols=slice(None)):
    u1, u2 = _shift_up(dc, 1), _shift_up(dc, 2)
    dw_ref[0:1, cols] = jnp.sum(u2 * h, axis=0, keepdims=True)
    dw_ref[1:2, cols] = jnp.sum(u1 * h, axis=0, keepdims=True)
    dw_ref[2:3, cols] = jnp.sum(dc * h, axis=0, keepdims=True)
    return w[2:3] * dc + w[1:2] * u1 + w[0:1] * u2


FFN_PAIRS = N_DEV // 2


def _lane_chunks(width):
    return [(c0, min(COL_TILE, width - c0)) for c0 in range(0, width, COL_TILE)]


def _ffn_up(n, wup, wdw, name):
    s, d = n.shape
    cw = wup.shape[-1]

    def body(n_ref, wg_ref, wu_ref, dg_ref, du_ref, h_ref, a_ref):
        x = n_ref[...]
        for c0, size in _lane_chunks(cw):
            cols = slice(c0, c0 + size)
            hg = _dot(x, wg_ref[:, cols])
            hu = _dot(x, wu_ref[:, cols])
            h_ref[0, :, cols] = hg.astype(BF16)
            h_ref[1, :, cols] = hu.astype(BF16)
            cg = _conv3(hg, dg_ref[:, cols])
            cu = _conv3(hu, du_ref[:, cols])
            a_ref[:, cols] = (cg * jax.nn.sigmoid(cg) * cu).astype(BF16)

    return pl.pallas_call(
        body, name=name, grid=(FFN_PAIRS,),
        in_specs=[pl.BlockSpec((s, d), lambda j: (0, 0)),
                  pl.BlockSpec((None, d, cw), lambda j: (j, 0, 0)),
                  pl.BlockSpec((None, d, cw), lambda j: (j + FFN_PAIRS, 0, 0)),
                  pl.BlockSpec((None, 3, cw), lambda j: (j, 0, 0)),
                  pl.BlockSpec((None, 3, cw), lambda j: (j + FFN_PAIRS, 0, 0))],
        out_specs=[pl.BlockSpec((None, 2, s, cw), lambda j: (j, 0, 0, 0)),
                   pl.BlockSpec((None, s, cw), lambda j: (j, 0, 0))],
        out_shape=[jax.ShapeDtypeStruct((FFN_PAIRS, 2, s, cw), BF16), jax.ShapeDtypeStruct((FFN_PAIRS, s, cw), BF16)],
        compiler_params=_params(("parallel",)),
    )(n, wup, wup, wdw, wdw)


def _ffn_mid_bwd(do, wdown, h, wdw, name):
    s, d = do.shape
    cw = wdown.shape[1]

    def body(do_ref, wd_ref, h_ref, wg_ref, wu_ref, dh_ref, dwg_ref, dwu_ref):
        dov = do_ref[...]
        for c0, size in _lane_chunks(cw):
            cols = slice(c0, c0 + size)
            da = _dot(dov, wd_ref[cols, :], NT)
            hg = h_ref[0, :, cols].astype(F32)
            hu = h_ref[1, :, cols].astype(F32)
            wg, wu = wg_ref[:, cols], wu_ref[:, cols]
            cg = _conv3(hg, wg)
            cu = _conv3(hu, wu)
            sg = jax.nn.sigmoid(cg)
            dcu = da * (cg * sg)
            dcg = da * cu * (sg * (1.0 + cg * (1.0 - sg)))
            dh_ref[0, :, cols] = _conv3_bwd(dcg, hg, wg, dwg_ref, cols).astype(BF16)
            dh_ref[1, :, cols] = _conv3_bwd(dcu, hu, wu, dwu_ref, cols).astype(BF16)

    vec = jax.ShapeDtypeStruct((FFN_PAIRS, 3, cw), F32)
    return pl.pallas_call(
        body, name=name, grid=(FFN_PAIRS,),
        in_specs=[pl.BlockSpec((s, d), lambda j: (0, 0)), pl.BlockSpec((None, cw, d), lambda j: (j, 0, 0)),
                  pl.BlockSpec((None, 2, s, cw), lambda j: (j, 0, 0, 0)),
                  pl.BlockSpec((None, 3, cw), lambda j: (j, 0, 0)),
                  pl.BlockSpec((None, 3, cw), lambda j: (j + FFN_PAIRS, 0, 0))],
        out_specs=[pl.BlockSpec((None, 2, s, cw), lambda j: (j, 0, 0, 0)),
                   pl.BlockSpec((None, 3, cw), lambda j: (j, 0, 0)), pl.BlockSpec((None, 3, cw), lambda j: (j, 0, 0))],
        out_shape=[jax.ShapeDtypeStruct((FFN_PAIRS, 2, s, cw), BF16), vec, vec],
        compiler_params=_params(("parallel",)),
    )(do, wdown, h, wdw, wdw)


def _ffn_dwup(n, dh, name):
    s, d = n.shape
    cw = dh.shape[-1]

    def body(n_ref, dh_ref, o_ref):
        o_ref[...] = _dot(n_ref[...], dh_ref[...], TN).astype(BF16)

    return pl.pallas_call(
        body, name=name, grid=(N_DEV,),
        in_specs=[pl.BlockSpec((s, d), lambda k: (0, 0)),
                  pl.BlockSpec((None, None, s, cw), lambda k: (k % FFN_PAIRS, k // FFN_PAIRS, 0, 0))],
        out_specs=pl.BlockSpec((None, d, cw), lambda k: (k, 0, 0)),
        out_shape=jax.ShapeDtypeStruct((N_DEV, d, cw), BF16),
        compiler_params=_params(("parallel",)),
    )(n, dh)


def _ffn_dn(dh, wup, name, dep=None):
    s, cw = dh.shape[-2:]
    d = wup.shape[1]
    tm = _pick(s, MATMUL_TILES)
    deps = [] if dep is None else [dep]

    def body(dh_ref, w_ref, *rest):
        o_ref, acc_ref = rest[-2:]
        k = pl.program_id(1)

        @pl.when(k == 0)
        def _():
            acc_ref[...] = jnp.zeros_like(acc_ref)

        acc_ref[...] += _dot(dh_ref[...], w_ref[...], NT)

        @pl.when(k == N_DEV - 1)
        def _():
            o_ref[...] = acc_ref[...]

    return pl.pallas_call(
        body, name=name, grid=(s // tm, N_DEV),
        in_specs=[pl.BlockSpec((None, None, tm, cw), lambda i, k: (k % FFN_PAIRS, k // FFN_PAIRS, i, 0)),
                  pl.BlockSpec((None, d, cw), lambda i, k: (k, 0, 0))] + [_ANY] * len(deps),
        out_specs=pl.BlockSpec((tm, d), lambda i, k: (i, 0)),
        out_shape=jax.ShapeDtypeStruct((s, d), F32),
        scratch_shapes=[pltpu.VMEM((tm, d), F32)],
        compiler_params=_params(("parallel", "arbitrary")),
    )(dh, wup, *deps)


def _sconv_fwd(n, win, wdw, name):
    s, d = n.shape
    tn = COL_TILE
    nj = d // tn

    def body(n_ref, wb_ref, wc_ref, wh_ref, dw_ref, z_ref, y_ref):
        x = n_ref[...]
        zb = _dot(x, wb_ref[...])
        zc = _dot(x, wc_ref[...])
        zh = _dot(x, wh_ref[...])
        z_ref[0] = zb.astype(BF16)
        z_ref[1] = zc.astype(BF16)
        z_ref[2] = zh.astype(BF16)
        y_ref[...] = (zb * _conv3(zc * zh, dw_ref[...])).astype(BF16)

    return pl.pallas_call(
        body, name=name, grid=(nj,),
        in_specs=[pl.BlockSpec((s, d), lambda j: (0, 0)),
                  pl.BlockSpec((d, tn), lambda j: (0, j)), pl.BlockSpec((d, tn), lambda j: (0, j + nj)),
                  pl.BlockSpec((d, tn), lambda j: (0, j + 2 * nj)), pl.BlockSpec((3, tn), lambda j: (0, j))],
        out_specs=[pl.BlockSpec((3, s, tn), lambda j: (0, 0, j)), pl.BlockSpec((s, tn), lambda j: (0, j))],
        out_shape=[jax.ShapeDtypeStruct((3, s, d), BF16), jax.ShapeDtypeStruct((s, d), BF16)],
        compiler_params=_params(("parallel",)),
    )(n, win, win, win, wdw)


def _sconv_mid_bwd(dm, wout, z, wdw, name):
    s, d = dm.shape
    tn = COL_TILE
    nj = d // tn

    def body(dm_ref, wo_ref, z_ref, w_ref, dz_ref, dw_ref):
        dy = _dot(dm_ref[...], wo_ref[...], NT)
        zb = z_ref[0].astype(F32)
        zc = z_ref[1].astype(F32)
        zh = z_ref[2].astype(F32)
        w = w_ref[...]
        p = zc * zh
        cp = _conv3(p, w)
        dz_ref[0] = (dy * cp).astype(BF16)
        dcp = dy * zb
        dp = _conv3_bwd(dcp, p, w, dw_ref)
        dz_ref[1] = (dp * zh).astype(BF16)
        dz_ref[2] = (dp * zc).astype(BF16)

    return pl.pallas_call(
        body, name=name, grid=(nj,),
        in_specs=[pl.BlockSpec((s, d), lambda j: (0, 0)), pl.BlockSpec((tn, d), lambda j: (j, 0)),
                  pl.BlockSpec((3, s, tn), lambda j: (0, 0, j)), pl.BlockSpec((3, tn), lambda j: (0, j))],
        out_specs=[pl.BlockSpec((3, s, tn), lambda j: (0, 0, j)), pl.BlockSpec((3, tn), lambda j: (0, j))],
        out_shape=[jax.ShapeDtypeStruct((3, s, d), BF16), jax.ShapeDtypeStruct((3, d), F32)],
        compiler_params=_params(("parallel",)),
    )(dm, wout, z, wdw)


def _pool_select(g, c2, c4, c8, c16):
    return jnp.where(g == 0, c2, jnp.where(g == 1, c4, jnp.where(g == 2, c8, c16)))


def _pool_inv_count(g, shape):
    pos = lax.broadcasted_iota(jnp.int32, shape, 0).astype(F32) + 1.0
    win = (2 << g).astype(F32)
    return jnp.minimum(pos, win)


def _pool_fwd(n, win, wgrp, scale, name):
    s, d = n.shape
    tn = POOL_GROUP_DIM

    def body(n_ref, wi_ref, wg_ref, sc_ref, p_ref, y_ref):
        g = pl.program_id(0)
        u = _dot(n_ref[...], wi_ref[...])
        s2 = u + _shift_down(u, 1)
        s4 = s2 + _shift_down(s2, 2)
        s8 = s4 + _shift_down(s4, 4)
        s16 = s8 + _shift_down(s8, 8)
        tot = _pool_select(g, s2, s4, s8, s16)
        p = (tot / _pool_inv_count(g, u.shape) - u).astype(BF16)
        p_ref[...] = p
        y_ref[...] = (_dot(p, wg_ref[...]) * sc_ref[...]).astype(BF16)

    return pl.pallas_call(
        body, name=name, grid=(d // tn,),
        in_specs=[pl.BlockSpec((s, d), lambda g: (0, 0)), pl.BlockSpec((d, tn), lambda g: (0, g)),
                  pl.BlockSpec((None, tn, tn), lambda g: (g, 0, 0)), pl.BlockSpec((1, tn), lambda g: (0, g))],
        out_specs=[pl.BlockSpec((s, tn), lambda g: (0, g)), pl.BlockSpec((s, tn), lambda g: (0, g))],
        out_shape=[jax.ShapeDtypeStruct((s, d), BF16), jax.ShapeDtypeStruct((s, d), BF16)],
        compiler_params=_params(("parallel",)),
    )(n, win, wgrp, scale)


def _pool_mid_bwd(dm, wout, p, wgrp, scale, name):
    s, d = dm.shape
    tn = POOL_GROUP_DIM

    def body(dm_ref, wo_ref, p_ref, wg_ref, sc_ref, du_ref, dwg_ref, dsc_ref):
        g = pl.program_id(0)
        dy = _dot(dm_ref[...], wo_ref[...], NT)
        pv = p_ref[...]
        wg = wg_ref[...]
        ypre = _dot(pv, wg)
        dsc_ref[...] = jnp.sum(dy * ypre, axis=0, keepdims=True)
        dypre = (dy * sc_ref[...]).astype(BF16)
        dwg_ref[...] = _dot(pv, dypre, TN)
        dp = _dot(dypre, wg, NT)
        e = dp / _pool_inv_count(g, dp.shape)
        f2 = e + _shift_up(e, 1)
        f4 = f2 + _shift_up(f2, 2)
        f8 = f4 + _shift_up(f4, 4)
        f16 = f8 + _shift_up(f8, 8)
        du_ref[...] = (_pool_select(g, f2, f4, f8, f16) - dp).astype(BF16)

    return pl.pallas_call(
        body, name=name, grid=(d // tn,),
        in_specs=[pl.BlockSpec((s, d), lambda g: (0, 0)), pl.BlockSpec((tn, d), lambda g: (g, 0)),
                  pl.BlockSpec((s, tn), lambda g: (0, g)), pl.BlockSpec((None, tn, tn), lambda g: (g, 0, 0)),
                  pl.BlockSpec((1, tn), lambda g: (0, g))],
        out_specs=[pl.BlockSpec((s, tn), lambda g: (0, g)), pl.BlockSpec((None, tn, tn), lambda g: (g, 0, 0)),
                   pl.BlockSpec((1, tn), lambda g: (0, g))],
        out_shape=[jax.ShapeDtypeStruct((s, d), BF16), jax.ShapeDtypeStruct((4, tn, tn), F32),
                   jax.ShapeDtypeStruct((1, d), F32)],
        compiler_params=_params(("parallel",)),
    )(dm, wout, p, wgrp, scale)


PANEL = LANES
ATTN_EXT = ATTN_WIDTH + PANEL
DVEC_LANE = HEADS


def _alibi_slopes(g, dil):
    all_slopes = 2.0 ** (-8.0 * np.arange(1, N_HEADS_A + 1) / N_HEADS_A)
    return [float(np.float32(sl) * np.float32(dil)) for sl in all_slopes[g * HEADS:(g + 1) * HEADS]]


def _residue_order(a, dil, name):
    s, w = a.shape
    per = ROW_TILE // dil
    panels = w // PANEL

    def body(a_ref, o_ref, *tiles):
        for c in range(panels):
            cols = slice(c * PANEL, (c + 1) * PANEL)
            tiles[c][...] = a_ref[:, cols].astype(F32)
            for r in range(dil):
                o_ref[r, :, cols] = tiles[c][pl.ds(r, per, stride=dil), :].astype(o_ref.dtype)

    out = pl.pallas_call(
        body, name=name, grid=(s // ROW_TILE,),
        in_specs=[pl.BlockSpec((ROW_TILE, w), lambda i: (i, 0))],
        out_specs=pl.BlockSpec((dil, per, w), lambda i: (0, i, 0)),
        out_shape=jax.ShapeDtypeStruct((dil, s // dil, w), a.dtype),
        scratch_shapes=[pltpu.VMEM((ROW_TILE, PANEL), F32)] * panels,
        compiler_params=_params(("parallel",)),
    )(a)
    return out.reshape(s, w)


def _token_order(a, dil, acc, name):
    s, w = a.shape
    per = ROW_TILE // dil
    panels = w // PANEL
    has_acc = acc is not None

    def body(*refs):
        a_ref = refs[0]
        o_ref = refs[2] if has_acc else refs[1]
        tiles = refs[3:] if has_acc else refs[2:]
        for c in range(panels):
            cols = slice(c * PANEL, (c + 1) * PANEL)
            for r in range(dil):
                tiles[c][pl.ds(r, per, stride=dil), :] = a_ref[r, :, cols]
            v = tiles[c][...]
            if has_acc:
                v = v + refs[1][:, cols]
            o_ref[:, cols] = v

    row = pl.BlockSpec((ROW_TILE, w), lambda i: (i, 0))
    return pl.pallas_call(
        body, name=name, grid=(s // ROW_TILE,),
        in_specs=[pl.BlockSpec((dil, per, w), lambda i: (0, i, 0))] + ([row] if has_acc else []),
        out_specs=row, out_shape=jax.ShapeDtypeStruct((s, w), F32),
        scratch_shapes=[pltpu.VMEM((ROW_TILE, PANEL), F32)] * panels,
        compiler_params=_params(("parallel",)),
    )(*([a.reshape(dil, s // dil, w)] + ([acc] if has_acc else [])))


def _qkv_proj(n, wqkv, g, name):
    s, d = n.shape
    tm = _pick(s, MATMUL_TILES)

    def body(a_ref, b_ref, o_ref):
        o_ref[...] = _dot(a_ref[...], b_ref[...]).astype(BF16)

    return pl.pallas_call(
        body, name=name, grid=(s // tm, 3),
        in_specs=[pl.BlockSpec((tm, d), lambda i, t: (i, 0)),
                  pl.BlockSpec((d, ATTN_WIDTH), lambda i, t: (0, 3 * g + t))],
        out_specs=pl.BlockSpec((None, tm, ATTN_WIDTH), lambda i, t: (t, i, 0)),
        out_shape=jax.ShapeDtypeStruct((3, s, ATTN_WIDTH), BF16),
        compiler_params=_params(("parallel", "parallel")),
    )(n, wqkv)


def _attn_window(n, ln):
    if ln == BLOCK:
        return 0, BLOCK
    return pl.multiple_of(jnp.maximum(n - 1, 0) * BLOCK, BLOCK), 2 * BLOCK


def _attn_mask(n, k0, kw):
    qpos = n * BLOCK + lax.broadcasted_iota(jnp.int32, (BLOCK, kw), 0)
    kpos = k0 + lax.broadcasted_iota(jnp.int32, (BLOCK, kw), 1)
    dist = qpos - kpos
    return dist.astype(F32), (dist >= 0) & (dist <= BLOCK)


def _attn_scores(q, keys, slope, dist, valid):
    s = _dot(q, keys, NT) * (HEAD_DIM ** -0.5) - slope * dist
    return jnp.where(valid, s, NEG_INF)


def _attn_fwd(qkv, g, name):
    _, s, w = qkv.shape
    dil = DILATED_CFG[g][1]
    ln = s // dil
    nb = ln // BLOCK
    slopes = _alibi_slopes(g, dil)

    def body(qkv_ref, o_ref):
        n = pl.program_id(1)
        k0, kw = _attn_window(n, ln)
        cur, win = pl.ds(pl.multiple_of(n * BLOCK, BLOCK), BLOCK), pl.ds(k0, kw)
        dist, valid = _attn_mask(n, k0, kw)
        o_ref[:, w:] = jnp.zeros((BLOCK, PANEL), F32)
        for h in range(HEADS):
            cols = slice(h * HEAD_DIM, (h + 1) * HEAD_DIM)
            sc = _attn_scores(qkv_ref[0, cur, cols], qkv_ref[1, win, cols], slopes[h], dist, valid)
            m = jnp.max(sc, axis=-1, keepdims=True)
            p = jnp.exp(sc - m)
            den = jnp.sum(p, axis=-1, keepdims=True)
            o_ref[:, cols] = _dot(p.astype(BF16), qkv_ref[2, win, cols]) / den
            o_ref[:, w + h:w + h + 1] = m + jnp.log(den)

    return pl.pallas_call(
        body, name=name, grid=(dil, nb),
        in_specs=[pl.BlockSpec((3, ln, w), lambda r, n: (0, r, 0))],
        out_specs=pl.BlockSpec((BLOCK, ATTN_EXT), lambda r, n: (r * nb + n, 0)),
        out_shape=jax.ShapeDtypeStruct((s, ATTN_EXT), F32),
        compiler_params=_params(("parallel", "parallel")),
    )(qkv)


def _attn_bwd(qkv, dext, g, name, dep=None):
    _, s, w = qkv.shape
    dil = DILATED_CFG[g][1]
    ln = s // dil
    nb = ln // BLOCK
    slopes = _alibi_slopes(g, dil)
    scale = HEAD_DIM ** -0.5
    deps = [] if dep is None else [dep]

    def body(qkv_ref, de_ref, *rest):
        d_ref, dk_ref, dv_ref = rest[-3:]
        n = pl.program_id(1)

        @pl.when(n == 0)
        def _():
            dk_ref[...] = jnp.zeros_like(dk_ref)
            dv_ref[...] = jnp.zeros_like(dv_ref)

        k0, kw = _attn_window(n, ln)
        cur, win = pl.ds(pl.multiple_of(n * BLOCK, BLOCK), BLOCK), pl.ds(k0, kw)
        dist, valid = _attn_mask(n, k0, kw)
        for h in range(HEADS):
            cols = slice(h * HEAD_DIM, (h + 1) * HEAD_DIM)
            q, keys = qkv_ref[0, cur, cols], qkv_ref[1, win, cols]
            dob = de_ref[:, cols].astype(BF16)
            p = jnp.exp(_attn_scores(q, keys, slopes[h], dist, valid) - de_ref[:, w + h:w + h + 1])
            dd = de_ref[:, w + DVEC_LANE + h:w + DVEC_LANE + h + 1]
            ds = (p * (_dot(dob, qkv_ref[2, win, cols], NT) - dd)).astype(BF16)
            d_ref[0, cur, cols] = (scale * _dot(ds, keys)).astype(BF16)
            dv_ref[win, cols] += _dot(p.astype(BF16), dob, TN)
            dk_ref[win, cols] += scale * _dot(ds, q, TN)

        @pl.when(n == nb - 1)
        def _():
            d_ref[1] = dk_ref[...].astype(BF16)
            d_ref[2] = dv_ref[...].astype(BF16)

    return pl.pallas_call(
        body, name=name, grid=(dil, nb),
        in_specs=[pl.BlockSpec((3, ln, w), lambda r, n: (0, r, 0)),
                  pl.BlockSpec((BLOCK, ATTN_EXT), lambda r, n: (r * nb + n, 0))] + [_ANY] * len(deps),
        out_specs=pl.BlockSpec((3, ln, w), lambda r, n: (0, r, 0)),
        out_shape=jax.ShapeDtypeStruct((3, s, w), BF16),
        scratch_shapes=[pltpu.VMEM((ln, w), F32), pltpu.VMEM((ln, w), F32)],
        compiler_params=_params(("parallel", "arbitrary")),
    )(qkv, dext, *deps)


def _attn_merge(e0, e1, e2, name):
    s = e0.shape[0]
    w = ATTN_WIDTH

    def body(e0_ref, e1_ref, e2_ref, m_ref, mb_ref, lse_ref):
        refs = (e0_ref, e1_ref, e2_ref)
        l = [r[:, w:w + HEADS] for r in refs]
        mx = jnp.maximum(jnp.maximum(l[0], l[1]), l[2])
        e = [jnp.exp(v - mx) for v in l]
        z = e[0] + e[1] + e[2]
        lse_ref[...] = mx + jnp.log(z)
        wts = [v / z for v in e]
        for h in range(HEADS):
            cols = slice(h * HEAD_DIM, (h + 1) * HEAD_DIM)
            acc = wts[0][:, h:h + 1] * refs[0][:, cols]
            for g in range(1, N_GROUPS_A):
                acc = acc + wts[g][:, h:h + 1] * refs[g][:, cols]
            m_ref[:, cols] = acc
            mb_ref[:, cols] = acc.astype(BF16)

    ext = pl.BlockSpec((ROW_TILE, ATTN_EXT), lambda i: (i, 0))
    row = pl.BlockSpec((ROW_TILE, w), lambda i: (i, 0))
    return pl.pallas_call(
        body, name=name, grid=(s // ROW_TILE,),
        in_specs=[ext, ext, ext],
        out_specs=[row, row, pl.BlockSpec((ROW_TILE, HEADS), lambda i: (i, 0))],
        out_shape=[jax.ShapeDtypeStruct((s, w), F32), jax.ShapeDtypeStruct((s, w), BF16),
                   jax.ShapeDtypeStruct((s, HEADS), F32)],
        compiler_params=_params(("parallel",)),
    )(e0, e1, e2)


def _attn_dvec(dmerged, merged, lse_all, name, dep=None):
    s, w = merged.shape
    deps = [] if dep is None else [dep]

    def body(dm_ref, m_ref, lse_ref, *rest):
        de_ref = rest[-1]
        dmv = dm_ref[...]
        de_ref[:, :w] = dmv
        de_ref[:, w:] = jnp.zeros((ROW_TILE, PANEL), F32)
        de_ref[:, w:w + HEADS] = lse_ref[...]
        prod = dmv * m_ref[...]
        for h in range(HEADS):
            lane = w + DVEC_LANE + h
            de_ref[:, lane:lane + 1] = jnp.sum(prod[:, h * HEAD_DIM:(h + 1) * HEAD_DIM], axis=-1, keepdims=True)

    row = pl.BlockSpec((ROW_TILE, w), lambda i: (i, 0))
    return pl.pallas_call(
        body, name=name, grid=(s // ROW_TILE,),
        in_specs=[row, row, pl.BlockSpec((ROW_TILE, HEADS), lambda i: (i, 0))] + [_ANY] * len(deps),
        out_specs=pl.BlockSpec((ROW_TILE, ATTN_EXT), lambda i: (i, 0)),
        out_shape=jax.ShapeDtypeStruct((s, ATTN_EXT), F32),
        compiler_params=_params(("parallel",)),
    )(dmerged, merged, lse_all, *deps)


def _attention_fwd(n, wqkv, wo, tag):
    ns, qkvs, exts = [], [], []
    for g, (_, dil) in enumerate(DILATED_CFG):
        ng = n if dil == 1 else _residue_order(n, dil, f"{tag}_order_g{g}")
        qkv = _qkv_proj(ng, wqkv, g, f"{tag}_qkv_g{g}")
        ext = _attn_fwd(qkv, g, f"{tag}_fwd_g{g}")
        ns.append(ng)
        qkvs.append(qkv)
        exts.append(ext if dil == 1 else _token_order(ext, dil, None, f"{tag}_unorder_g{g}"))
    merged, merged_bf, lse_all = _attn_merge(*exts, f"{tag}_merge")
    m = _matmul(merged_bf, wo, "nn", F32, f"{tag}_wo")
    return m, (ns, qkvs, merged, merged_bf, lse_all)


def _attention_bwd(dm, wqkv, wo, saved, tag, dep=None, hook=None):
    ns, qkvs, merged, merged_bf, lse_all = saved
    d_wo = _matmul(merged_bf, dm, "tn", BF16, f"{tag}_dwo")
    dmerged = _matmul(dm, wo, "nt", F32, f"{tag}_dmerged")
    dext = _attn_dvec(dmerged, merged, lse_all, f"{tag}_dvec", dep)
    width = 3 * ATTN_WIDTH
    d_wqkv, dn, dep = [], None, None
    for g, (_, dil) in enumerate(DILATED_CFG):
        dext_g = dext if dil == 1 else _residue_order(dext, dil, f"{tag}_dorder_g{g}")
        dqkv = _attn_bwd(qkvs[g], dext_g, g, f"{tag}_bwd_g{g}", dep)
        dep = hook(g, dqkv) if hook is not None and g + 1 < N_GROUPS_A else None
        d_wqkv.append(_matmul(ns[g], dqkv, "tn", BF16, f"{tag}_dwqkv_g{g}", b_parts=3))
        dn_g = _matmul(dqkv, wqkv[:, g * width:(g + 1) * width], "nt", F32, f"{tag}_dn_g{g}", a_parts=3)
        dn = dn_g if dil == 1 else _token_order(dn_g, dil, dn, f"{tag}_dn_sum_g{g}")
    return dn, jnp.concatenate(d_wqkv, axis=1), d_wo


def _layer_matrices(i):
    mixer = (("attn_w_qkv", "attn_w_o"), ("conv_w_in", "conv_w_out"), ("pool_w_in", "pool_w_grp", "pool_w_out"))[i % 3]
    return [(k, i // 3) for k in mixer] + [("ffn_w_up", i), ("ffn_w_down", i)]


def _local_step(x, tgt, vec, weights, sink):
    ng = vec["norm_g"]

    def gain(i, j, token=None):
        g = ng[i, j][None, :]
        return g if token is None else g + token

    saved = []
    n = _rms_fwd(x, gain(0, 0), None, BF16, "norm_first")
    for i in range(DEPTH):
        wl = weights.layer(i)
        t0 = weights.hook(i, 0, n)
        kind, idx = i % 3, i // 3
        if kind == 0:
            m, ms = _attention_fwd(n, wl["attn_w_qkv"], wl["attn_w_o"], "attn")
        elif kind == 1:
            taps = vec["conv_w_dw"][idx] if t0 is None else vec["conv_w_dw"][idx] + t0
            z, y = _sconv_fwd(n, wl["conv_w_in"], taps, "sconv_fwd")
            m = _matmul(y, wl["conv_w_out"], "nn", F32, "sconv_out")
            ms = (z, y)
        else:
            scale = vec["pool_scale"][idx][None, :] if t0 is None else vec["pool_scale"][idx][None, :] + t0
            p, y = _pool_fwd(n, wl["pool_w_in"], wl["pool_w_grp"], scale, "pool_fwd")
            m = _matmul(y, wl["pool_w_out"], "nn", F32, "pool_out")
            ms = (p, y)
        t1 = weights.hook(i, 1, m)
        x1, n2 = _rms_res_pre(m, gain(i, 1, t0), x, gain(i, 2, t1), "norm_res_pre")
        h, a = _ffn_up(n2, wl["ffn_w_up"], vec["ffn_w_dw"][i], "ffn_up")
        t2 = weights.hook(i, 2, a)
        f = _matmul(a, wl["ffn_w_down"].reshape(D_FF, D_MODEL), "nn", F32, "ffn_down", a_parts=FFN_PAIRS)
        saved.append((x, n, m, ms, x1, n2, h, a, f, wl))
        if i + 1 < DEPTH:
            x, n = _rms_res_pre(f, gain(i, 3, t2), x1, gain(i + 1, 0), "norm_res_pre")
        else:
            x = _rms_fwd(f, gain(i, 3), x1, F32, "norm_res")
        weights.hook(i, 3, x)

    loss, dx = _loss_head(x, tgt, "loss_head")

    g_norm = [[None] * 4 for _ in range(DEPTH)]
    g_taps, g_scale, g_ffn_dw = [], [], [None] * DEPTH
    df, g_norm[DEPTH - 1][3] = _rms_bwd(saved[-1][8], gain(DEPTH - 1, 3), dx, None, BF16, "norm_bwd_sub")
    t0 = None
    for i in reversed(range(DEPTH)):
        xin, n, m, ms, x1, n2, h, a, f, wl = saved[i]
        kind, idx = i % 3, i // 3
        gl = {}
        d_wdown = _matmul(a, df, "tn", BF16, "ffn_dwdown", a_parts=FFN_PAIRS)
        gl["ffn_w_down"] = d_wdown.reshape(N_DEV, D_FF // N_DEV, D_MODEL)
        ffn_taps = vec["ffn_w_dw"][i] if t0 is None else vec["ffn_w_dw"][i] + t0
        dh, dwg, dwu = _ffn_mid_bwd(df, wl["ffn_w_down"].reshape(FFN_PAIRS, -1, D_MODEL), h, ffn_taps, "ffn_mid_bwd")
        g_ffn_dw[i] = jnp.concatenate([dwg, dwu], axis=0)
        t1 = sink.hook(i, 1, dh)
        gl["ffn_w_up"] = _ffn_dwup(n2, dh, "ffn_dwup")
        tf = sink.ffn_done(i, gl)
        dn2 = _ffn_dn(dh, wl["ffn_w_up"], "ffn_dn", t1)
        dx1, dm, g_norm[i][2], g_norm[i][1] = _rms_bwd_pair(x1, gain(i, 2, tf), dn2, dx, m, gain(i, 1), "norm_bwd_pair")
        t2 = sink.hook(i, 2, dm)
        if kind == 0:
            dn, gl["attn_w_qkv"], gl["attn_w_o"] = _attention_bwd(
                dm, wl["attn_w_qkv"], wl["attn_w_o"], ms, "attn", t2, lambda g, after, i=i: sink.hook(i, ("a", "b")[g], after))
        elif kind == 1:
            z, y = ms
            gl["conv_w_out"] = _matmul(y, dm, "tn", BF16, "sconv_dwout")
            taps = vec["conv_w_dw"][idx] if t2 is None else vec["conv_w_dw"][idx] + t2
            dz, ddw = _sconv_mid_bwd(dm, wl["conv_w_out"], z, taps, "sconv_mid_bwd")
            g_taps.append(ddw)
            gl["conv_w_in"] = _matmul(n, dz, "tn", BF16, "sconv_dwin", b_parts=3)
            dn = _matmul(dz, wl["conv_w_in"], "nt", F32, "sconv_dn", a_parts=3)
        else:
            p, y = ms
            gl["pool_w_out"] = _matmul(y, dm, "tn", BF16, "pool_dwout")
            scale = vec["pool_scale"][idx][None, :] if t2 is None else vec["pool_scale"][idx][None, :] + t2
            du, gl["pool_w_grp"], dscale = _pool_mid_bwd(dm, wl["pool_w_out"], p, wl["pool_w_grp"], scale, "pool_mid_bwd")
            g_scale.append(dscale[0])
            gl["pool_w_in"] = _matmul(n, du, "tn", BF16, "pool_dwin")
            dn = _matmul(du, wl["pool_w_in"], "nt", F32, "pool_dn")
        sink.hook(i, 3, dn)
        if i > 0:
            dx, df, g_norm[i][0], g_norm[i - 1][3] = _rms_bwd_pair(xin, gain(i, 0, t2), dn, dx1, saved[i - 1][8],
                                                                   gain(i - 1, 3), "norm_bwd_pair")
        else:
            dx, g_norm[0][0] = _rms_bwd(xin, gain(0, 0), dn, dx1, F32, "norm_bwd_res")
        t0 = sink.layer_done(i, gl)

    vec_grads = {"norm_g": jnp.stack([jnp.concatenate(row, axis=0) for row in g_norm]), "conv_w_dw": jnp.stack(g_taps),
                 "pool_scale": jnp.stack(g_scale), "ffn_w_dw": g_ffn_dw}
    return loss, dx, vec_grads


_AXES = ("x", "y", "c")
ROUTE_A = ("y", "x", "c")
ROUTE_B = ("x", "y", "c")
def _dev_index(pos):
    return 4 * pos["x"] + 2 * pos["y"] + pos["c"]


_HBM = pl.BlockSpec(memory_space=pltpu.HBM)
_SEM = pl.BlockSpec(memory_space=pltpu.SEMAPHORE)
_ANY = pl.BlockSpec(memory_space=pl.ANY)
_EFFECT = pltpu.SideEffectType.DATAFLOW_SIDE_EFFECTING


TOKEN_SHAPE = (1, D_MODEL)


def _copies_start(describe, arrays, n_copies, name, after, token_shape=TOKEN_SHAPE):
    n = len(arrays)
    deps = [] if after is None else [after]

    def body(*refs):
        send_sems, recv_sems = refs[n + len(deps)], refs[n + len(deps) + 1]
        for c in describe(refs[:n], send_sems, recv_sems):
            c.start()
        refs[-1][...] = jnp.zeros_like(refs[-1])

    outs = pl.pallas_call(
        body, name=f"{name}_start",
        out_shape=(pltpu.SemaphoreType.DMA((n_copies,)), pltpu.SemaphoreType.DMA((n_copies,)),
                   *[pltpu.HBM(a.shape, a.dtype) for a in arrays], jax.ShapeDtypeStruct(token_shape, F32)),
        in_specs=[_HBM] * n + [_ANY] * len(deps),
        out_specs=(_SEM, _SEM, *([_HBM] * n), pl.BlockSpec(memory_space=pltpu.VMEM)),
        input_output_aliases={i: 2 + i for i in range(n)},
        compiler_params=pltpu.CompilerParams(has_side_effects=_EFFECT),
    )(*[pltpu.with_memory_space_constraint(a, pltpu.HBM) for a in arrays], *deps)
    return (outs[0], outs[1], list(outs[2:2 + n])), outs[-1]


def _copies_wait(describe, handle, name, after):
    send_sems, recv_sems, arrays = handle
    n = len(arrays)
    deps = [] if after is None else [after]

    def body(*refs):
        for c in describe(refs[:n], refs[n], refs[n + 1]):
            c.wait_send()
            c.wait_recv()

    outs = pl.pallas_call(
        body, name=f"{name}_wait",
        out_shape=tuple(pltpu.HBM(a.shape, a.dtype) for a in arrays),
        in_specs=[_HBM] * n + [_SEM, _SEM] + [_ANY] * len(deps), out_specs=tuple([_HBM] * n),
        input_output_aliases={i: i for i in range(n)},
        compiler_params=pltpu.CompilerParams(has_side_effects=_EFFECT),
    )(*arrays, send_sems, recv_sems, *deps)
    return list(outs)


GATHER_STAGE_COPIES = (3, 3, 1)


def _gather_copies(stage, routes):
    n = len(routes)

    def describe(refs, send_sems, recv_sems):
        pos = {a: lax.axis_index(a) for a in _AXES}

        def flipped(axes):
            return {a: 1 - pos[a] if a in axes else pos[a] for a in _AXES}

        copies = []
        for i, (a1, a2, a3) in enumerate(routes):
            land = refs[n + i] if stage == 1 else refs[i]
            p1, p2, p12, p3 = flipped((a1,)), flipped((a2,)), flipped((a1, a2)), flipped((a3,))
            plan = {1: [(None, p1), (None, p2), (None, p3)], 2: [(p1, p2), (p1, p3), (p2, p3)], 3: [(p12, p3)]}[stage]
            for holder, to in plan:
                slot = land.at[_dev_index(pos if holder is None else holder)]
                k = len(copies)
                copies.append(pltpu.make_async_remote_copy(
                    src_ref=refs[i] if holder is None else slot, dst_ref=slot,
                    send_sem=send_sems.at[k], recv_sem=recv_sems.at[k],
                    device_id=tuple(to[a] for a in _AXES), device_id_type=pl.DeviceIdType.MESH))
        return copies

    return describe


def _gather_begin(shards, routes, name, after):
    n = len(shards)
    lands = [lax.empty((N_DEV,) + a.shape, a.dtype) for a in shards]
    handle, token = _copies_start(_gather_copies(1, routes), list(shards) + lands, GATHER_STAGE_COPIES[0] * n,
                                  f"{name}_1", after)
    return {"stage": 1, "handle": handle, "routes": routes, "name": name, "n": n}, token


def _gather_next(state, after):
    stage, routes, name, n = state["stage"], state["routes"], state["name"], state["n"]
    arrays = _copies_wait(_gather_copies(stage, routes), state["handle"], f"{name}_{stage}", after)
    if stage == 1:
        state = dict(state, shards=arrays[:n])
        arrays = arrays[n:]
    if stage == 3:
        me = _dev_index({a: lax.axis_index(a) for a in _AXES})
        return [lax.dynamic_update_index_in_dim(o, s, me, 0) for o, s in zip(arrays, state["shards"])], None
    handle, token = _copies_start(_gather_copies(stage + 1, routes), arrays, GATHER_STAGE_COPIES[stage] * n,
                                  f"{name}_{stage + 1}", None)
    return dict(state, stage=stage + 1, handle=handle), token


ADD_ROW_TILES = (1024, 704, 512, 352, 256, 128, 96, 64, 32, 16)


def _add_half(a, recv, me, out_dtype, name):
    p, q, cols = recv.shape
    tr = _pick(q, ADD_ROW_TILES)

    def body(me_ref, a_ref, b_ref, o_ref):
        o_ref[...] = (a_ref[...].astype(F32) + b_ref[...].astype(F32)).astype(o_ref.dtype)

    return pl.pallas_call(
        body, name=name,
        grid_spec=pltpu.PrefetchScalarGridSpec(
            num_scalar_prefetch=1, grid=(p, q // tr),
            in_specs=[pl.BlockSpec((None, None, tr, cols), lambda j, i, m: (j, m[0], i, 0)),
                      pl.BlockSpec((None, tr, cols), lambda j, i, m: (j, i, 0))],
            out_specs=pl.BlockSpec((None, tr, cols), lambda j, i, m: (j, i, 0))),
        out_shape=jax.ShapeDtypeStruct((p, q, cols), out_dtype),
        compiler_params=_params(("parallel", "parallel")),
    )(me, a, recv)


def _half_copies(axes):
    n = len(axes)

    def describe(refs, send_sems, recv_sems):
        pos = {a: lax.axis_index(a) for a in _AXES}
        copies = []
        for i, axis in enumerate(axes):
            peer = tuple(1 - pos[a] if a == axis else pos[a] for a in _AXES)
            copies.append(pltpu.make_async_remote_copy(
                src_ref=refs[i].at[:, 1 - pos[axis]], dst_ref=refs[n + i], send_sem=send_sems.at[i],
                recv_sem=recv_sems.at[i], device_id=peer, device_id_type=pl.DeviceIdType.MESH))
        return copies

    return describe


def _scatter_begin(slots, routes, tags, name, token_shape=TOKEN_SHAPE):
    shapes = [a.shape[1:] for a in slots]
    rows = [math.prod(s[:-1]) for s in shapes]
    arrays = [a.reshape(4, 2, n, s[-1]) for a, n, s in zip(slots, rows, shapes)]
    return _scatter_start({"stage": 0, "arrays": arrays, "routes": routes, "tags": tags, "name": name,
                           "shapes": shapes, "rows": rows}, token_shape)


def _scatter_start(state, token_shape=TOKEN_SHAPE):
    stage, arrays = state["stage"], state["arrays"]
    axes = [r[2 - stage] for r in state["routes"]]
    lands = [lax.empty((a.shape[0],) + a.shape[2:], a.dtype) for a in arrays]
    handle, token = _copies_start(_half_copies(axes), arrays + lands, len(arrays), f"{state['name']}_{stage + 1}", None,
                                  token_shape)
    return dict(state, handle=handle, axes=axes), token


def _scatter_next(state, after):
    stage, axes, n = state["stage"], state["axes"], len(state["arrays"])
    both = _copies_wait(_half_copies(axes), state["handle"], f"{state['name']}_{stage + 1}", after)
    coord = {a: lax.axis_index(a).astype(jnp.int32).reshape(1) for a in _AXES}
    sums = [_add_half(a, r, coord[ax], F32 if stage == 2 else BF16, f"scatter_add_{stage + 1}_{t}")
            for a, r, ax, t in zip(both[:n], both[n:], axes, state["tags"])]
    if stage == 2:
        return [a.reshape(s) for a, s in zip(sums, state["shapes"])], None
    if stage == 0:
        views = [(1, 2, 2 * r, s[-1]) if route[1] == "x" else (2, 2, r, s[-1])
                 for r, s, route in zip(state["rows"], state["shapes"], state["routes"])]
    else:
        views = [(1, 2, r, s[-1]) for r, s in zip(state["rows"], state["shapes"])]
    return _scatter_start(dict(state, stage=stage + 1, arrays=[a.reshape(v) for a, v in zip(sums, views)]))


_WEIGHTS = {
    "norm_g": ((DEPTH, 4, D_MODEL), 2, True),
    "attn_w_qkv": ((2, D_MODEL, 4608), 2, False),
    "attn_w_o": ((2, ATTN_WIDTH, D_MODEL), 2, False),
    "conv_w_in": ((1, D_MODEL, 3 * D_MODEL), 2, False),
    "conv_w_dw": ((1, 3, D_MODEL), 2, True),
    "conv_w_out": ((1, D_MODEL, D_MODEL), 1, False),
    "pool_w_in": ((1, D_MODEL, D_MODEL), 1, False),
    "pool_w_grp": ((1, 4, POOL_GROUP_DIM, POOL_GROUP_DIM), 2, False),
    "pool_scale": ((1, D_MODEL), 1, True),
    "pool_w_out": ((1, D_MODEL, D_MODEL), 1, False),
    "ffn_w_up": ((DEPTH, D_MODEL, 2 * D_FF), 2, False),
    "ffn_w_dw": ((DEPTH, 3, 2 * D_FF), 2, True),
    "ffn_w_down": ((DEPTH, D_FF, D_MODEL), 1, False),
}
_NAMES = tuple(_WEIGHTS)
_VECTORS = tuple(k for k in _NAMES if _WEIGHTS[k][2])
_MATRICES = tuple(k for k in _NAMES if not _WEIGHTS[k][2])
_FFN = ("ffn_w_up", "ffn_w_down")
_ON_ROUTE_A = ("ffn_w_up", "attn_w_o", "conv_w_out", "pool_w_in")
PACK_ROWS = 16


def _route(name):
    return ROUTE_A if name in _ON_ROUTE_A else ROUTE_B


def _shard_shape(name):
    shape, ax, _ = _WEIGHTS[name]
    return tuple(s // N_DEV if i == ax else s for i, s in enumerate(shape))


def _full_from_slots(slots, name, layers=None):
    shape, ax, _ = _WEIGHTS[name]
    if layers is not None:
        shape = (layers,) + shape[1:]
    return jnp.moveaxis(slots, 0, ax).reshape(shape)


def _slots_from_full(full, name):
    shape, ax, _ = _WEIGHTS[name]
    split = shape[:ax] + (N_DEV, shape[ax] // N_DEV) + shape[ax + 1:]
    return jnp.moveaxis(full.reshape(split), ax, 0)


def _pack_vectors(parts, lead):
    rows = []
    for k in _VECTORS:
        r = parts[k].reshape(lead + (-1, LANES))
        pad = -r.shape[-2] % PACK_ROWS
        rows.append(jnp.pad(r, [(0, 0)] * len(lead) + [(0, pad), (0, 0)]))
    return jnp.concatenate(rows, axis=len(lead))


def _unpack_vectors(buf, lead):
    out, r0 = {}, 0
    for k in _VECTORS:
        shard = _shard_shape(k)
        rows = math.prod(shard) // LANES
        out[k] = buf[..., r0:r0 + rows, :].reshape(lead + shard)
        r0 += rows + (-rows % PACK_ROWS)
    return out


class _LayerWeights:
    def __init__(self, shards):
        self.cast = {k: shards[k].astype(BF16) for k in _MATRICES}
        first, ffn0 = _layer_matrices(0)[:-2], _layer_matrices(0)[-2:]
        state, _ = _gather_begin(self._send(first) + [_pack_vectors(shards, ())],
                                 [_route(k) for k, _ in first] + [ROUTE_B], "gather0", None)
        for _ in range(2):
            state, _ = _gather_next(state, None)
        outs, _ = _gather_next(state, None)
        vec = _unpack_vectors(outs[-1], (N_DEV,))
        self.vec = {k: _full_from_slots(vec[k], k) for k in _VECTORS}
        self.vec["ffn_w_dw"] = [vec["ffn_w_dw"][:, l] for l in range(DEPTH)]
        self.ready = {0: self._unpack(first, outs[:-1])}
        self.chains = {}
        tokens = []
        self._begin("ffn0", ffn0, "gather0f", outs[0], tokens)
        self._begin(1, _layer_matrices(1), "gather1", outs[0], tokens)
        self.vec["norm_g"] = self.vec["norm_g"] + (tokens[0] + tokens[1])

    def _send(self, items):
        return [self.cast[k][j] for k, j in items]

    @staticmethod
    def _unpack(items, outs):
        return {k: o if k in _FFN else _full_from_slots(o[:, None], k, layers=1)[0] for (k, _), o in zip(items, outs)}

    def _begin(self, key, items, name, after, tokens):
        state, token = _gather_begin(self._send(items), [_route(k) for k, _ in items], name, after)
        self.chains[key] = (items, state)
        tokens.append(token)

    def _advance(self, key, after, tokens):
        items, state = self.chains.pop(key)
        state, token = _gather_next(state, after)
        if token is None:
            self.ready.setdefault(0 if key == "ffn0" else key, {}).update(self._unpack(items, state))
        else:
            self.chains[key] = (items, state)
            tokens.append(token)

    def layer(self, i):
        return self.ready[i]

    def hook(self, i, point, after):
        tokens = []
        if i == 0 and point == 0:
            self._advance("ffn0", after, tokens)
        if i == 0 and point == 1:
            self._advance("ffn0", after, tokens)
            self._advance("ffn0", None, tokens)
        if point >= 1 and i + 1 in self.chains:
            self._advance(i + 1, after, tokens)
        if point == 1 and i + 2 < DEPTH:
            self._begin(i + 2, _layer_matrices(i + 2), f"gather{i + 2}", after, tokens)
        return functools.reduce(lambda a, b: a + b, tokens) if tokens else None


def _layer_slots(g, name):
    shape, ax, _ = _WEIGHTS[name]
    shape, ax = shape[1:], ax - 1
    split = shape[:ax] + (N_DEV, shape[ax] // N_DEV) + shape[ax + 1:]
    return jnp.moveaxis(g.reshape(split), ax, 0).astype(BF16)


class _GradSink:
    def __init__(self):
        self.state = None
        self.ffn_state = None
        self.sums = {}
        self.last = None

    def ffn_done(self, i, grads):
        if i != 0:
            return None
        self.ffn_items = _layer_matrices(0)[-2:]
        self.ffn_state, token = _scatter_begin([grads[k] for k, _ in self.ffn_items],
                                               [_route(k) for k, _ in self.ffn_items],
                                               [f"{k}{j}" for k, j in self.ffn_items], "scatter0f")
        return token

    def layer_done(self, i, grads):
        items = _layer_matrices(i)
        if i == 0:
            items = items[:-2]
            self.last = (items, [_layer_slots(grads[k], k) for k, _ in items])
            return None
        slots = [grads[k] if k in _FFN else _layer_slots(grads[k], k) for k, _ in items]
        self.items = items
        self.state, token = _scatter_begin(slots, [_route(k) for k, _ in items], [f"{k}{j}" for k, j in items],
                                           f"scatter{i}", (N_DEV, 3, 2 * D_FF // N_DEV))
        return token

    def hook(self, i, point, after):
        tokens = []
        if self.state is not None and point in (1, 2, 3):
            self.state, token = _scatter_next(self.state, after)
            if point == 3:
                self.sums.update(dict(zip(self.items, self.state)))
                self.state = None
            tokens.append(token)
        if self.ffn_state is not None and point in ("a", "b", 3):
            self.ffn_state, token = _scatter_next(self.ffn_state, after)
            if point == 3:
                self.sums.update(dict(zip(self.ffn_items, self.ffn_state)))
                self.ffn_state = None
            tokens.append(token)
        tokens = [t for t in tokens if t is not None]
        return functools.reduce(lambda a, b: a + b, tokens) if tokens else None


def _adamw(w, g, m, v, name, layer=None, prev=None):
    shape = w.shape
    cols = shape[-1]
    view = shape if len(shape) == 3 else (1, math.prod(shape[:-1]), cols)
    layers, rows, _ = view
    tr = _pick(rows, (512, 256, 128, 64, 32, 16, 8))
    n_prev = 0 if prev is None else 3

    def body(*refs):
        w_ref, g_ref, m_ref, v_ref = refs[n_prev:n_prev + 4]
        d_ref, nm_ref, nv_ref = refs[n_prev + 4:]
        gv = g_ref[...]
        nm = ADAM_B1 * m_ref[...] + (1.0 - ADAM_B1) * gv
        nv = ADAM_B2 * v_ref[...] + (1.0 - ADAM_B2) * jnp.square(gv)
        m_hat = nm / (1.0 - ADAM_B1 ** ADAM_STEP)
        v_hat = nv / (1.0 - ADAM_B2 ** ADAM_STEP)
        d_ref[...] = -ADAM_LR * (m_hat / (jnp.sqrt(v_hat) + ADAM_EPS) + ADAM_WD * w_ref[...])
        nm_ref[...] = nm
        nv_ref[...] = nv

    if layer is None:
        grid = (layers, rows // tr)
        blk = gblk = pl.BlockSpec((None, tr, cols), lambda l, i: (l, i, 0))
        gview = view
    else:
        grid = (rows // tr,)
        blk = pl.BlockSpec((None, tr, cols), lambda i: (layer, i, 0))
        gblk = pl.BlockSpec((tr, cols), lambda i: (i, 0))
        gview = (rows, cols)
    shp = jax.ShapeDtypeStruct(view, F32)
    outs = pl.pallas_call(
        body, name=name, grid=grid, in_specs=[_ANY] * n_prev + [blk, gblk, blk, blk], out_specs=[blk] * 3,
        out_shape=[shp] * 3, input_output_aliases={i: i for i in range(n_prev)},
        compiler_params=_params(("parallel",) * len(grid)),
    )(*([] if prev is None else [p.reshape(view) for p in prev]), w.reshape(view), g.reshape(gview), m.reshape(view),
      v.reshape(view))
    return [o.reshape(shape) for o in outs]


def kernel(x, norm_g, attn_w_qkv, attn_w_o, conv_w_in, conv_w_dw, conv_w_out, pool_w_in, pool_w_grp, pool_scale, pool_w_out, ffn_w_up, ffn_w_dw, ffn_w_down, loss_target, m_norm_g, m_attn_w_qkv, m_attn_w_o, m_conv_w_in, m_conv_w_dw, m_conv_w_out, m_pool_w_in, m_pool_w_grp, m_pool_scale, m_pool_w_out, m_ffn_w_up, m_ffn_w_dw, m_ffn_w_down, v_norm_g, v_attn_w_qkv, v_attn_w_o, v_conv_w_in, v_conv_w_dw, v_conv_w_out, v_pool_w_in, v_pool_w_grp, v_pool_scale, v_pool_w_out, v_ffn_w_up, v_ffn_w_dw, v_ffn_w_down):
    shards = dict(zip(_NAMES, (norm_g, attn_w_qkv, attn_w_o, conv_w_in, conv_w_dw, conv_w_out, pool_w_in,
                               pool_w_grp, pool_scale, pool_w_out, ffn_w_up, ffn_w_dw, ffn_w_down)))
    moms = dict(zip(_NAMES, (m_norm_g, m_attn_w_qkv, m_attn_w_o, m_conv_w_in, m_conv_w_dw, m_conv_w_out,
                             m_pool_w_in, m_pool_w_grp, m_pool_scale, m_pool_w_out, m_ffn_w_up, m_ffn_w_dw,
                             m_ffn_w_down)))
    vels = dict(zip(_NAMES, (v_norm_g, v_attn_w_qkv, v_attn_w_o, v_conv_w_in, v_conv_w_dw, v_conv_w_out,
                             v_pool_w_in, v_pool_w_grp, v_pool_scale, v_pool_w_out, v_ffn_w_up, v_ffn_w_dw,
                             v_ffn_w_down)))
    weights = _LayerWeights(shards)
    sink = _GradSink()
    loss, grad_x, vec_grads = _local_step(x[0], loss_target[0], weights.vec, weights, sink)
    loss = lax.psum(loss[0, 0], _AXES)

    items, slots = sink.last
    vec_slots = {k: _slots_from_full(vec_grads[k], k) for k in _VECTORS if k != "ffn_w_dw"}
    vec_slots["ffn_w_dw"] = jnp.stack(vec_grads["ffn_w_dw"], axis=1)
    state, _ = _scatter_begin(slots + [_pack_vectors(vec_slots, (N_DEV,)).astype(BF16)],
                              [_route(k) for k, _ in items] + [ROUTE_B], [f"{k}{j}" for k, j in items] + ["vectors"],
                              "scatter0")
    results = {}

    def step_layer(i):
        last = None
        for k, j in _layer_matrices(i):
            g = sink.sums[(k, j)]
            if _WEIGHTS[k][0][0] == 1:
                results[k] = (g[None], _adamw(shards[k], g[None], moms[k], vels[k], f"adamw_{k}"))
            else:
                gs, prev = results.get(k, ({}, None))
                gs[j] = g
                results[k] = (gs, _adamw(shards[k], g, moms[k], vels[k], f"adamw_{k}{j}", layer=j, prev=prev))
            last = results[k][1][0]
        return last

    for i in (3, 2):
        state, _ = _scatter_next(state, step_layer(i))
    sums, _ = _scatter_next(state, step_layer(1))
    sink.sums.update(dict(zip(items, sums[:-1])))
    step_layer(0)
    vec_sums = _unpack_vectors(sums[-1], ())
    for k in _VECTORS:
        results[k] = (vec_sums[k], _adamw(shards[k], vec_sums[k], moms[k], vels[k], f"adamw_{k}"))
    grads_out = {k: g if not isinstance(g, dict) else jnp.stack([g[j] for j in range(len(g))])
                 for k, (g, _) in results.items()}
    return (loss, grad_x[None], *[grads_out[k] for k in _NAMES], *[results[k][1][0] for k in _NAMES],
            *[results[k][1][1] for k in _NAMES], *[results[k][1][2] for k in _NAMES])
```

```python
import functools
import math

import numpy as np
import jax
import jax.numpy as jnp
from jax import lax
from jax.experimental import pallas as pl
from jax.experimental.pallas import tpu as pltpu

F32, BF16 = jnp.float32, jnp.bfloat16

D_MODEL = 1024
SEQ = 2048
DEPTH = 4
DILATED_CFG = ((128, 1), (512, 4), (2048, 16))
N_GROUPS_A = 3
HEADS = 8
HEAD_DIM = 64
ATTN_WIDTH = HEADS * HEAD_DIM
N_HEADS_A = N_GROUPS_A * HEADS
BLOCK = 128
NEG_INF = -1e30
POOL_GROUP_DIM = 256
D_FF = 2816
RMS_EPS = 1e-6
ADAM_LR, ADAM_B1, ADAM_B2, ADAM_EPS, ADAM_WD, ADAM_STEP = 0.001, 0.9, 0.999, 1e-08, 0.01, 10

N_DEV = 8
LANES = 128
V7X_VMEM_BYTES = 64 * 2 ** 20
VMEM_LIMIT_BYTES = V7X_VMEM_BYTES - 8 * 2 ** 20
COL_TILE = 256
ROW_TILE = 256
MATMUL_TILES = (1024, 1408, 512, 256, 128)
TN_RESIDENT_K = 2048

NN = (((1,), (0,)), ((), ()))
NT = (((1,), (1,)), ((), ()))
TN = (((0,), (0,)), ((), ()))


def _dot(a, b, dims=NN):
    return lax.dot_general(a, b, dims, preferred_element_type=F32)


def _params(sem=None):
    return pltpu.CompilerParams(dimension_semantics=sem, vmem_limit_bytes=VMEM_LIMIT_BYTES)


def _pick(n, prefs):
    for p in prefs:
        if n % p == 0:
            return p
    return n


def _matmul(a, b, mode, out_dtype, name, a_parts=1, b_parts=1):
    if mode == "nn":
        m, k = a.shape[-2], a.shape[-1] * a_parts
        n = b.shape[-1] * b_parts
    elif mode == "nt":
        m, k = a.shape[-2], a.shape[-1] * a_parts
        n = b.shape[-2]
    else:
        k, m = a.shape[-2], a.shape[-1] * a_parts
        n = b.shape[-1] * b_parts
    tm = _pick(m, MATMUL_TILES)
    tn = _pick(n // b_parts if mode != "nt" else n, MATMUL_TILES)
    kk = k // a_parts if mode != "tn" else k
    tk = _pick(kk, MATMUL_TILES)
    if mode == "tn":
        tm = _pick(m // a_parts, MATMUL_TILES)
        if k <= TN_RESIDENT_K:
            tk = k
    gm, gn, gk = m // tm, n // tn, k // tk

    def a_idx(i, j, kq):
        if mode == "tn":
            r, c, per = kq, i, (m // a_parts) // tm
        else:
            r, c, per = i, kq, (k // a_parts) // tk
        return (r, c) if a_parts == 1 else (c // per, r, c % per)

    def b_idx(i, j, kq):
        if mode == "nt":
            return (j, kq)
        per = (n // b_parts) // tn
        return (kq, j) if b_parts == 1 else (j // per, kq, j % per)

    a_blk = (tk, tm) if mode == "tn" else (tm, tk)
    b_blk = (tn, tk) if mode == "nt" else (tk, tn)
    if a_parts > 1:
        a_blk = (None,) + a_blk
    if b_parts > 1:
        b_blk = (None,) + b_blk
    dims = {"nn": NN, "nt": NT, "tn": TN}[mode]

    def body_single(a_ref, b_ref, o_ref):
        o_ref[...] = _dot(a_ref[...], b_ref[...], dims).astype(o_ref.dtype)

    def body(a_ref, b_ref, o_ref, acc_ref):
        kq = pl.program_id(2)

        @pl.when(kq == 0)
        def _():
            acc_ref[...] = jnp.zeros_like(acc_ref)

        acc_ref[...] += _dot(a_ref[...], b_ref[...], dims)

        @pl.when(kq == gk - 1)
        def _():
            o_ref[...] = acc_ref[...].astype(o_ref.dtype)

    return pl.pallas_call(
        body_single if gk == 1 else body, name=name, grid=(gm, gn, gk),
        in_specs=[pl.BlockSpec(a_blk, a_idx), pl.BlockSpec(b_blk, b_idx)],
        out_specs=pl.BlockSpec((tm, tn), lambda i, j, kq: (i, j)),
        out_shape=jax.ShapeDtypeStruct((m, n), out_dtype),
        scratch_shapes=[] if gk == 1 else [pltpu.VMEM((tm, tn), F32)],
        compiler_params=_params(("parallel", "parallel", "arbitrary")),
    )(a, b)


def _rms_fwd(xin, g, res, out_dtype, name):
    s, d = xin.shape
    has_res = res is not None

    def body(*refs):
        x_ref, g_ref = refs[0], refs[1]
        o_ref = refs[-1]
        x = x_ref[...]
        r = lax.rsqrt(jnp.mean(x * x, axis=-1, keepdims=True) + RMS_EPS)
        y = x * r * g_ref[...]
        if has_res:
            y = refs[2][...] + y
        o_ref[...] = y.astype(o_ref.dtype)

    row = pl.BlockSpec((ROW_TILE, d), lambda i: (i, 0))
    vec = pl.BlockSpec((1, d), lambda i: (0, 0))
    ins = [xin, g] + ([res] if has_res else [])
    return pl.pallas_call(
        body, name=name, grid=(s // ROW_TILE,),
        in_specs=[row, vec] + ([row] if has_res else []),
        out_specs=row, out_shape=jax.ShapeDtypeStruct((s, d), out_dtype),
        compiler_params=_params(("parallel",)),
    )(*ins)


def _rms_bwd(xin, g, dy, dres, out_dtype, name):
    s, d = xin.shape
    has_res = dres is not None

    def body(*refs):
        x_ref, g_ref, dy_ref = refs[0], refs[1], refs[2]
        dx_ref, dg_ref = refs[-2], refs[-1]

        @pl.when(pl.program_id(0) == 0)
        def _():
            dg_ref[...] = jnp.zeros_like(dg_ref)

        x = x_ref[...]
        dyv = dy_ref[...].astype(F32)
        r = lax.rsqrt(jnp.mean(x * x, axis=-1, keepdims=True) + RMS_EPS)
        xhat = x * r
        u = dyv * g_ref[...]
        dx = r * (u - xhat * jnp.mean(u * xhat, axis=-1, keepdims=True))
        if has_res:
            dx = refs[3][...] + dx
        dx_ref[...] = dx.astype(dx_ref.dtype)
        dg_ref[...] += jnp.sum(dyv * xhat, axis=0, keepdims=True)

    row = pl.BlockSpec((ROW_TILE, d), lambda i: (i, 0))
    vec = pl.BlockSpec((1, d), lambda i: (0, 0))
    ins = [xin, g, dy] + ([dres] if has_res else [])
    return pl.pallas_call(
        body, name=name, grid=(s // ROW_TILE,),
        in_specs=[row, vec, row] + ([row] if has_res else []),
        out_specs=[row, vec],
        out_shape=[jax.ShapeDtypeStruct((s, d), out_dtype), jax.ShapeDtypeStruct((1, d), F32)],
        compiler_params=_params(("arbitrary",)),
    )(*ins)


def _rms(x):
    r = lax.rsqrt(jnp.mean(x * x, axis=-1, keepdims=True) + RMS_EPS)
    return r, x * r


def _rms_grad(r, xhat, dy, g):
    u = dy * g
    return r * (u - xhat * jnp.mean(u * xhat, axis=-1, keepdims=True))


def _rms_res_pre(sub, g_post, res, g_pre, name):
    s, d = sub.shape

    def body(sub_ref, gp_ref, res_ref, gn_ref, x_ref, n_ref):
        xnew = res_ref[...] + _rms(sub_ref[...])[1] * gp_ref[...]
        x_ref[...] = xnew
        n_ref[...] = (_rms(xnew)[1] * gn_ref[...]).astype(BF16)

    row = pl.BlockSpec((ROW_TILE, d), lambda i: (i, 0))
    vec = pl.BlockSpec((1, d), lambda i: (0, 0))
    return pl.pallas_call(
        body, name=name, grid=(s // ROW_TILE,),
        in_specs=[row, vec, row, vec], out_specs=[row, row],
        out_shape=[jax.ShapeDtypeStruct((s, d), F32), jax.ShapeDtypeStruct((s, d), BF16)],
        compiler_params=_params(("parallel",)),
    )(sub, g_post, res, g_pre)


def _rms_bwd_pair(xmid, g_pre, dn, dres, sub, g_post, name):
    s, d = xmid.shape

    def body(x_ref, gn_ref, dn_ref, dres_ref, sub_ref, gp_ref, dx_ref, dsub_ref, dgn_ref, dgp_ref):
        @pl.when(pl.program_id(0) == 0)
        def _():
            dgn_ref[...] = jnp.zeros_like(dgn_ref)
            dgp_ref[...] = jnp.zeros_like(dgp_ref)

        dnv = dn_ref[...].astype(F32)
        r, xhat = _rms(x_ref[...])
        dx = dres_ref[...] + _rms_grad(r, xhat, dnv, gn_ref[...])
        dx_ref[...] = dx
        dgn_ref[...] += jnp.sum(dnv * xhat, axis=0, keepdims=True)
        rs, shat = _rms(sub_ref[...])
        dsub_ref[...] = _rms_grad(rs, shat, dx, gp_ref[...]).astype(BF16)
        dgp_ref[...] += jnp.sum(dx * shat, axis=0, keepdims=True)

    row = pl.BlockSpec((ROW_TILE, d), lambda i: (i, 0))
    vec = pl.BlockSpec((1, d), lambda i: (0, 0))
    return pl.pallas_call(
        body, name=name, grid=(s // ROW_TILE,),
        in_specs=[row, vec, row, row, row, vec], out_specs=[row, row, vec, vec],
        out_shape=[jax.ShapeDtypeStruct((s, d), F32), jax.ShapeDtypeStruct((s, d), BF16),
                   jax.ShapeDtypeStruct((1, d), F32), jax.ShapeDtypeStruct((1, d), F32)],
        compiler_params=_params(("arbitrary",)),
    )(xmid, g_pre, dn, dres, sub, g_post)


def _loss_head(y, tgt, name):
    s, d = y.shape

    def body(y_ref, t_ref, l_ref, dy_ref):
        @pl.when(pl.program_id(0) == 0)
        def _():
            l_ref[...] = jnp.zeros_like(l_ref)

        e = y_ref[...] - t_ref[...]
        dy_ref[...] = e / d
        per_tok = jnp.mean(e * e, axis=-1, keepdims=True)
        l_ref[...] += 0.5 * jnp.sum(per_tok, axis=0, keepdims=True)

    row = pl.BlockSpec((ROW_TILE, d), lambda i: (i, 0))
    return pl.pallas_call(
        body, name=name, grid=(s // ROW_TILE,),
        in_specs=[row, row],
        out_specs=[pl.BlockSpec((1, 1), lambda i: (0, 0)), row],
        out_shape=[jax.ShapeDtypeStruct((1, 1), F32), jax.ShapeDtypeStruct((s, d), F32)],
        compiler_params=_params(("arbitrary",)),
    )(y, tgt)


SUBLANES = 8


def _shift_down(x, k):
    t, c = x.shape
    r = pltpu.roll(x.reshape(t // SUBLANES, SUBLANES, c), k, axis=1)
    above = jnp.concatenate([jnp.zeros((1, SUBLANES, c), x.dtype), r[:-1]], axis=0)
    rows = lax.broadcasted_iota(jnp.int32, (1, SUBLANES, c), 1)
    return jnp.where(rows >= k, r, above).reshape(t, c)


def _shift_up(x, k):
    t, c = x.shape
    r = pltpu.roll(x.reshape(t // SUBLANES, SUBLANES, c), SUBLANES - k, axis=1)
    below = jnp.concatenate([r[1:], jnp.zeros((1, SUBLANES, c), x.dtype)], axis=0)
    rows = lax.broadcasted_iota(jnp.int32, (1, SUBLANES, c), 1)
    return jnp.where(rows < SUBLANES - k, r, below).reshape(t, c)


def _conv3(h, w):
    return w[2:3] * h + w[1:2] * _shift_down(h, 1) + w[0:1] * _shift_down(h, 2)


def _conv3_bwd(dc, h, w, dw_ref, cols=slice(None)):
    u1, u2 = _shift_up(dc, 1), _shift_up(dc, 2)
    dw_ref[0:1, cols] = jnp.sum(u2 * h, axis=0, keepdims=True)
    dw_ref[1:2, cols] = jnp.sum(u1 * h, axis=0, keepdims=True)
    dw_ref[2:3, cols] = jnp.sum(dc * h, axis=0, keepdims=True)
    return w[2:3] * dc + w[1:2] * u1 + w[0:1] * u2


FFN_PAIRS = N_DEV // 2


def _lane_chunks(width):
    return [(c0, min(COL_TILE, width - c0)) for c0 in range(0, width, COL_TILE)]


def _ffn_up(n, wup, wdw, name):
    s, d = n.shape
    cw = wup.shape[-1]

    def body(n_ref, wg_ref, wu_ref, dg_ref, du_ref, h_ref, a_ref):
        x = n_ref[...]
        for c0, size in _lane_chunks(cw):
            cols = slice(c0, c0 + size)
            hg = _dot(x, wg_ref[:, cols])
            hu = _dot(x, wu_ref[:, cols])
            h_ref[0, :, cols] = hg.astype(BF16)
            h_ref[1, :, cols] = hu.astype(BF16)
            cg = _conv3(hg, dg_ref[:, cols])
            cu = _conv3(hu, du_ref[:, cols])
            a_ref[:, cols] = (cg * jax.nn.sigmoid(cg) * cu).astype(BF16)

    return pl.pallas_call(
        body, name=name, grid=(FFN_PAIRS,),
        in_specs=[pl.BlockSpec((s, d), lambda j: (0, 0)),
                  pl.BlockSpec((None, d, cw), lambda j: (j, 0, 0)),
                  pl.BlockSpec((None, d, cw), lambda j: (j + FFN_PAIRS, 0, 0)),
                  pl.BlockSpec((None, 3, cw), lambda j: (j, 0, 0)),
                  pl.BlockSpec((None, 3, cw), lambda j: (j + FFN_PAIRS, 0, 0))],
        out_specs=[pl.BlockSpec((None, 2, s, cw), lambda j: (j, 0, 0, 0)),
                   pl.BlockSpec((None, s, cw), lambda j: (j, 0, 0))],
        out_shape=[jax.ShapeDtypeStruct((FFN_PAIRS, 2, s, cw), BF16), jax.ShapeDtypeStruct((FFN_PAIRS, s, cw), BF16)],
        compiler_params=_params(("parallel",)),
    )(n, wup, wup, wdw, wdw)


def _ffn_mid_bwd(do, wdown, h, wdw, name):
    s, d = do.shape
    cw = wdown.shape[1]

    def body(do_ref, wd_ref, h_ref, wg_ref, wu_ref, dh_ref, dwg_ref, dwu_ref):
        dov = do_ref[...]
        for c0, size in _lane_chunks(cw):
            cols = slice(c0, c0 + size)
            da = _dot(dov, wd_ref[cols, :], NT)
            hg = h_ref[0, :, cols].astype(F32)
            hu = h_ref[1, :, cols].astype(F32)
            wg, wu = wg_ref[:, cols], wu_ref[:, cols]
            cg = _conv3(hg, wg)
            cu = _conv3(hu, wu)
            sg = jax.nn.sigmoid(cg)
            dcu = da * (cg * sg)
            dcg = da * cu * (sg * (1.0 + cg * (1.0 - sg)))
            dh_ref[0, :, cols] = _conv3_bwd(dcg, hg, wg, dwg_ref, cols).astype(BF16)
            dh_ref[1, :, cols] = _conv3_bwd(dcu, hu, wu, dwu_ref, cols).astype(BF16)

    vec = jax.ShapeDtypeStruct((FFN_PAIRS, 3, cw), F32)
    return pl.pallas_call(
        body, name=name, grid=(FFN_PAIRS,),
        in_specs=[pl.BlockSpec((s, d), lambda j: (0, 0)), pl.BlockSpec((None, cw, d), lambda j: (j, 0, 0)),
                  pl.BlockSpec((None, 2, s, cw), lambda j: (j, 0, 0, 0)),
                  pl.BlockSpec((None, 3, cw), lambda j: (j, 0, 0)),
                  pl.BlockSpec((None, 3, cw), lambda j: (j + FFN_PAIRS, 0, 0))],
        out_specs=[pl.BlockSpec((None, 2, s, cw), lambda j: (j, 0, 0, 0)),
                   pl.BlockSpec((None, 3, cw), lambda j: (j, 0, 0)), pl.BlockSpec((None, 3, cw), lambda j: (j, 0, 0))],
        out_shape=[jax.ShapeDtypeStruct((FFN_PAIRS, 2, s, cw), BF16), vec, vec],
        compiler_params=_params(("parallel",)),
    )(do, wdown, h, wdw, wdw)


def _ffn_dwup(n, dh, name):
    s, d = n.shape
    cw = dh.shape[-1]

    def body(n_ref, dh_ref, o_ref):
        o_ref[...] = _dot(n_ref[...], dh_ref[...], TN).astype(BF16)

    return pl.pallas_call(
        body, name=name, grid=(N_DEV,),
        in_specs=[pl.BlockSpec((s, d), lambda k: (0, 0)),
                  pl.BlockSpec((None, None, s, cw), lambda k: (k % FFN_PAIRS, k // FFN_PAIRS, 0, 0))],
        out_specs=pl.BlockSpec((None, d, cw), lambda k: (k, 0, 0)),
        out_shape=jax.ShapeDtypeStruct((N_DEV, d, cw), BF16),
        compiler_params=_params(("parallel",)),
    )(n, dh)


def _ffn_dn(dh, wup, name, dep=None):
    s, cw = dh.shape[-2:]
    d = wup.shape[1]
    tm = _pick(s, MATMUL_TILES)
    deps = [] if dep is None else [dep]

    def body(dh_ref, w_ref, *rest):
        o_ref, acc_ref = rest[-2:]
        k = pl.program_id(1)

        @pl.when(k == 0)
        def _():
            acc_ref[...] = jnp.zeros_like(acc_ref)

        acc_ref[...] += _dot(dh_ref[...], w_ref[...], NT)

        @pl.when(k == N_DEV - 1)
        def _():
            o_ref[...] = acc_ref[...]

    return pl.pallas_call(
        body, name=name, grid=(s // tm, N_DEV),
        in_specs=[pl.BlockSpec((None, None, tm, cw), lambda i, k: (k % FFN_PAIRS, k // FFN_PAIRS, i, 0)),
                  pl.BlockSpec((None, d, cw), lambda i, k: (k, 0, 0))] + [_ANY] * len(deps),
        out_specs=pl.BlockSpec((tm, d), lambda i, k: (i, 0)),
        out_shape=jax.ShapeDtypeStruct((s, d), F32),
        scratch_shapes=[pltpu.VMEM((tm, d), F32)],
        compiler_params=_params(("parallel", "arbitrary")),
    )(dh, wup, *deps)


def _sconv_fwd(n, win, wdw, name):
    s, d = n.shape
    tn = COL_TILE
    nj = d // tn

    def body(n_ref, wb_ref, wc_ref, wh_ref, dw_ref, z_ref, y_ref):
        x = n_ref[...]
        zb = _dot(x, wb_ref[...])
        zc = _dot(x, wc_ref[...])
        zh = _dot(x, wh_ref[...])
        z_ref[0] = zb.astype(BF16)
        z_ref[1] = zc.astype(BF16)
        z_ref[2] = zh.astype(BF16)
        y_ref[...] = (zb * _conv3(zc * zh, dw_ref[...])).astype(BF16)

    return pl.pallas_call(
        body, name=name, grid=(nj,),
        in_specs=[pl.BlockSpec((s, d), lambda j: (0, 0)),
                  pl.BlockSpec((d, tn), lambda j: (0, j)), pl.BlockSpec((d, tn), lambda j: (0, j + nj)),
                  pl.BlockSpec((d, tn), lambda j: (0, j + 2 * nj)), pl.BlockSpec((3, tn), lambda j: (0, j))],
        out_specs=[pl.BlockSpec((3, s, tn), lambda j: (0, 0, j)), pl.BlockSpec((s, tn), lambda j: (0, j))],
        out_shape=[jax.ShapeDtypeStruct((3, s, d), BF16), jax.ShapeDtypeStruct((s, d), BF16)],
        compiler_params=_params(("parallel",)),
    )(n, win, win, win, wdw)


def _sconv_mid_bwd(dm, wout, z, wdw, name):
    s, d = dm.shape
    tn = COL_TILE
    nj = d // tn

    def body(dm_ref, wo_ref, z_ref, w_ref, dz_ref, dw_ref):
        dy = _dot(dm_ref[...], wo_ref[...], NT)
        zb = z_ref[0].astype(F32)
        zc = z_ref[1].astype(F32)
        zh = z_ref[2].astype(F32)
        w = w_ref[...]
        p = zc * zh
        cp = _conv3(p, w)
        dz_ref[0] = (dy * cp).astype(BF16)
        dcp = dy * zb
        dp = _conv3_bwd(dcp, p, w, dw_ref)
        dz_ref[1] = (dp * zh).astype(BF16)
        dz_ref[2] = (dp * zc).astype(BF16)

    return pl.pallas_call(
        body, name=name, grid=(nj,),
        in_specs=[pl.BlockSpec((s, d), lambda j: (0, 0)), pl.BlockSpec((tn, d), lambda j: (j, 0)),
                  pl.BlockSpec((3, s, tn), lambda j: (0, 0, j)), pl.BlockSpec((3, tn), lambda j: (0, j))],
        out_specs=[pl.BlockSpec((3, s, tn), lambda j: (0, 0, j)), pl.BlockSpec((3, tn), lambda j: (0, j))],
        out_shape=[jax.ShapeDtypeStruct((3, s, d), BF16), jax.ShapeDtypeStruct((3, d), F32)],
        compiler_params=_params(("parallel",)),
    )(dm, wout, z, wdw)


def _pool_select(g, c2, c4, c8, c16):
    return jnp.where(g == 0, c2, jnp.where(g == 1, c4, jnp.where(g == 2, c8, c16)))


def _pool_inv_count(g, shape):
    pos = lax.broadcasted_iota(jnp.int32, shape, 0).astype(F32) + 1.0
    win = (2 << g).astype(F32)
    return jnp.minimum(pos, win)


def _pool_fwd(n, win, wgrp, scale, name):
    s, d = n.shape
    tn = POOL_GROUP_DIM

    def body(n_ref, wi_ref, wg_ref, sc_ref, p_ref, y_ref):
        g = pl.program_id(0)
        u = _dot(n_ref[...], wi_ref[...])
        s2 = u + _shift_down(u, 1)
        s4 = s2 + _shift_down(s2, 2)
        s8 = s4 + _shift_down(s4, 4)
        s16 = s8 + _shift_down(s8, 8)
        tot = _pool_select(g, s2, s4, s8, s16)
        p = (tot / _pool_inv_count(g, u.shape) - u).astype(BF16)
        p_ref[...] = p
        y_ref[...] = (_dot(p, wg_ref[...]) * sc_ref[...]).astype(BF16)

    return pl.pallas_call(
        body, name=name, grid=(d // tn,),
        in_specs=[pl.BlockSpec((s, d), lambda g: (0, 0)), pl.BlockSpec((d, tn), lambda g: (0, g)),
                  pl.BlockSpec((None, tn, tn), lambda g: (g, 0, 0)), pl.BlockSpec((1, tn), lambda g: (0, g))],
        out_specs=[pl.BlockSpec((s, tn), lambda g: (0, g)), pl.BlockSpec((s, tn), lambda g: (0, g))],
        out_shape=[jax.ShapeDtypeStruct((s, d), BF16), jax.ShapeDtypeStruct((s, d), BF16)],
        compiler_params=_params(("parallel",)),
    )(n, win, wgrp, scale)


def _pool_mid_bwd(dm, wout, p, wgrp, scale, name):
    s, d = dm.shape
    tn = POOL_GROUP_DIM

    def body(dm_ref, wo_ref, p_ref, wg_ref, sc_ref, du_ref, dwg_ref, dsc_ref):
        g = pl.program_id(0)
        dy = _dot(dm_ref[...], wo_ref[...], NT)
        pv = p_ref[...]
        wg = wg_ref[...]
        ypre = _dot(pv, wg)
        dsc_ref[...] = jnp.sum(dy * ypre, axis=0, keepdims=True)
        dypre = (dy * sc_ref[...]).astype(BF16)
        dwg_ref[...] = _dot(pv, dypre, TN)
        dp = _dot(dypre, wg, NT)
        e = dp / _pool_inv_count(g, dp.shape)
        f2 = e + _shift_up(e, 1)
        f4 = f2 + _shift_up(f2, 2)
        f8 = f4 + _shift_up(f4, 4)
        f16 = f8 + _shift_up(f8, 8)
        du_ref[...] = (_pool_select(g, f2, f4, f8, f16) - dp).astype(BF16)

    return pl.pallas_call(
        body, name=name, grid=(d // tn,),
        in_specs=[pl.BlockSpec((s, d), lambda g: (0, 0)), pl.BlockSpec((tn, d), lambda g: (g, 0)),
                  pl.BlockSpec((s, tn), lambda g: (0, g)), pl.BlockSpec((None, tn, tn), lambda g: (g, 0, 0)),
                  pl.BlockSpec((1, tn), lambda g: (0, g))],
        out_specs=[pl.BlockSpec((s, tn), lambda g: (0, g)), pl.BlockSpec((None, tn, tn), lambda g: (g, 0, 0)),
                   pl.BlockSpec((1, tn), lambda g: (0, g))],
        out_shape=[jax.ShapeDtypeStruct((s, d), BF16), jax.ShapeDtypeStruct((4, tn, tn), F32),
                   jax.ShapeDtypeStruct((1, d), F32)],
        compiler_params=_params(("parallel",)),
    )(dm, wout, p, wgrp, scale)


PANEL = LANES
ATTN_EXT = ATTN_WIDTH + PANEL
DVEC_LANE = HEADS


def _alibi_slopes(g, dil):
    all_slopes = 2.0 ** (-8.0 * np.arange(1, N_HEADS_A + 1) / N_HEADS_A)
    return [float(np.float32(sl) * np.float32(dil)) for sl in all_slopes[g * HEADS:(g + 1) * HEADS]]


def _residue_order(a, dil, name):
    s, w = a.shape
    per = ROW_TILE // dil
    panels = w // PANEL

    def body(a_ref, o_ref, *tiles):
        for c in range(panels):
            cols = slice(c * PANEL, (c + 1) * PANEL)
            tiles[c][...] = a_ref[:, cols].astype(F32)
            for r in range(dil):
                o_ref[r, :, cols] = tiles[c][pl.ds(r, per, stride=dil), :].astype(o_ref.dtype)

    out = pl.pallas_call(
        body, name=name, grid=(s // ROW_TILE,),
        in_specs=[pl.BlockSpec((ROW_TILE, w), lambda i: (i, 0))],
        out_specs=pl.BlockSpec((dil, per, w), lambda i: (0, i, 0)),
        out_shape=jax.ShapeDtypeStruct((dil, s // dil, w), a.dtype),
        scratch_shapes=[pltpu.VMEM((ROW_TILE, PANEL), F32)] * panels,
        compiler_params=_params(("parallel",)),
    )(a)
    return out.reshape(s, w)


def _token_order(a, dil, acc, name):
    s, w = a.shape
    per = ROW_TILE // dil
    panels = w // PANEL
    has_acc = acc is not None

    def body(*refs):
        a_ref = refs[0]
        o_ref = refs[2] if has_acc else refs[1]
        tiles = refs[3:] if has_acc else refs[2:]
        for c in range(panels):
            cols = slice(c * PANEL, (c + 1) * PANEL)
            for r in range(dil):
                tiles[c][pl.ds(r, per, stride=dil), :] = a_ref[r, :, cols]
            v = tiles[c][...]
            if has_acc:
                v = v + refs[1][:, cols]
            o_ref[:, cols] = v

    row = pl.BlockSpec((ROW_TILE, w), lambda i: (i, 0))
    return pl.pallas_call(
        body, name=name, grid=(s // ROW_TILE,),
        in_specs=[pl.BlockSpec((dil, per, w), lambda i: (0, i, 0))] + ([row] if has_acc else []),
        out_specs=row, out_shape=jax.ShapeDtypeStruct((s, w), F32),
        scratch_shapes=[pltpu.VMEM((ROW_TILE, PANEL), F32)] * panels,
        compiler_params=_params(("parallel",)),
    )(*([a.reshape(dil, s // dil, w)] + ([acc] if has_acc else [])))


def _qkv_proj(n, wqkv, g, name):
    s, d = n.shape
    tm = _pick(s, MATMUL_TILES)

    def body(a_ref, b_ref, o_ref):
        o_ref[...] = _dot(a_ref[...], b_ref[...]).astype(BF16)

    return pl.pallas_call(
        body, name=name, grid=(s // tm, 3),
        in_specs=[pl.BlockSpec((tm, d), lambda i, t: (i, 0)),
                  pl.BlockSpec((d, ATTN_WIDTH), lambda i, t: (0, 3 * g + t))],
        out_specs=pl.BlockSpec((None, tm, ATTN_WIDTH), lambda i, t: (t, i, 0)),
        out_shape=jax.ShapeDtypeStruct((3, s, ATTN_WIDTH), BF16),
        compiler_params=_params(("parallel", "parallel")),
    )(n, wqkv)


def _attn_window(n, ln):
    if ln == BLOCK:
        return 0, BLOCK
    return pl.multiple_of(jnp.maximum(n - 1, 0) * BLOCK, BLOCK), 2 * BLOCK


def _attn_mask(n, k0, kw):
    qpos = n * BLOCK + lax.broadcasted_iota(jnp.int32, (BLOCK, kw), 0)
    kpos = k0 + lax.broadcasted_iota(jnp.int32, (BLOCK, kw), 1)
    dist = qpos - kpos
    return dist.astype(F32), (dist >= 0) & (dist <= BLOCK)


def _attn_scores(q, keys, slope, dist, valid):
    s = _dot(q, keys, NT) * (HEAD_DIM ** -0.5) - slope * dist
    return jnp.where(valid, s, NEG_INF)


ATTN_STEP_BLOCKS = 1
ATTN_BWD_STEP_BLOCKS = 4


def _attn_block(gb, ln):
    nb = ln // BLOCK
    n, base = (0, gb * ln) if nb == 1 else (gb % nb, (gb // nb) * ln)
    k0, kw = _attn_window(n, ln)
    cur = pl.ds(pl.multiple_of(gb * BLOCK, BLOCK), BLOCK)
    win = pl.ds(pl.multiple_of(base + k0, BLOCK), kw)
    return cur, win, n, k0, kw


def _attn_fwd(qkv, g, name):
    _, s, w = qkv.shape
    dil = DILATED_CFG[g][1]
    ln = s // dil
    slopes = _alibi_slopes(g, dil)
    rows = ATTN_STEP_BLOCKS * BLOCK

    def body(qkv_ref, o_ref):
        o_ref[:, w:] = jnp.zeros((rows, PANEL), F32)
        for b in range(ATTN_STEP_BLOCKS):
            cur, win, n, k0, kw = _attn_block(pl.program_id(0) * ATTN_STEP_BLOCKS + b, ln)
            dist, valid = _attn_mask(n, k0, kw)
            out = slice(b * BLOCK, (b + 1) * BLOCK)
            for h in range(HEADS):
                cols = slice(h * HEAD_DIM, (h + 1) * HEAD_DIM)
                sc = _attn_scores(qkv_ref[0, cur, cols], qkv_ref[1, win, cols], slopes[h], dist, valid)
                m = jnp.max(sc, axis=-1, keepdims=True)
                p = jnp.exp(sc - m)
                den = jnp.sum(p, axis=-1, keepdims=True)
                o_ref[out, cols] = _dot(p.astype(BF16), qkv_ref[2, win, cols]) / den
                o_ref[out, w + h:w + h + 1] = m + jnp.log(den)

    return pl.pallas_call(
        body, name=name, grid=(s // rows,),
        in_specs=[pl.BlockSpec((3, s, w), lambda i: (0, 0, 0))],
        out_specs=pl.BlockSpec((rows, ATTN_EXT), lambda i: (i, 0)),
        out_shape=jax.ShapeDtypeStruct((s, ATTN_EXT), F32),
        compiler_params=_params(("parallel",)),
    )(qkv)


def _attn_bwd(qkv, dext, g, name, dep=None):
    _, s, w = qkv.shape
    dil = DILATED_CFG[g][1]
    ln = s // dil
    slopes = _alibi_slopes(g, dil)
    scale = HEAD_DIM ** -0.5
    rows = ATTN_BWD_STEP_BLOCKS * BLOCK
    steps = s // rows
    deps = [] if dep is None else [dep]

    def body(qkv_ref, de_ref, *rest):
        d_ref, dk_ref, dv_ref = rest[-3:]

        @pl.when(pl.program_id(0) == 0)
        def _():
            dk_ref[...] = jnp.zeros_like(dk_ref)
            dv_ref[...] = jnp.zeros_like(dv_ref)

        for b in range(ATTN_BWD_STEP_BLOCKS):
            cur, win, n, k0, kw = _attn_block(pl.program_id(0) * ATTN_BWD_STEP_BLOCKS + b, ln)
            dist, valid = _attn_mask(n, k0, kw)
            blk = slice(b * BLOCK, (b + 1) * BLOCK)
            for h in range(HEADS):
                cols = slice(h * HEAD_DIM, (h + 1) * HEAD_DIM)
                q, keys = qkv_ref[0, cur, cols], qkv_ref[1, win, cols]
                dob = de_ref[blk, cols].astype(BF16)
                p = jnp.exp(_attn_scores(q, keys, slopes[h], dist, valid) - de_ref[blk, w + h:w + h + 1])
                dd = de_ref[blk, w + DVEC_LANE + h:w + DVEC_LANE + h + 1]
                ds = (p * (_dot(dob, qkv_ref[2, win, cols], NT) - dd)).astype(BF16)
                d_ref[0, cur, cols] = (scale * _dot(ds, keys)).astype(BF16)
                dv_ref[win, cols] += _dot(p.astype(BF16), dob, TN)
                dk_ref[win, cols] += scale * _dot(ds, q, TN)

        @pl.when(pl.program_id(0) == steps - 1)
        def _():
            d_ref[1] = dk_ref[...].astype(BF16)
            d_ref[2] = dv_ref[...].astype(BF16)

    whole = pl.BlockSpec((3, s, w), lambda i: (0, 0, 0))
    return pl.pallas_call(
        body, name=name, grid=(steps,),
        in_specs=[whole, pl.BlockSpec((rows, ATTN_EXT), lambda i: (i, 0))] + [_ANY] * len(deps),
        out_specs=whole, out_shape=jax.ShapeDtypeStruct((3, s, w), BF16),
        scratch_shapes=[pltpu.VMEM((s, w), F32), pltpu.VMEM((s, w), F32)],
        compiler_params=_params(("arbitrary",)),
    )(qkv, dext, *deps)


def _attn_merge(e0, e1, e2, name):
    s = e0.shape[0]
    w = ATTN_WIDTH

    def body(e0_ref, e1_ref, e2_ref, m_ref, mb_ref, lse_ref):
        refs = (e0_ref, e1_ref, e2_ref)
        l = [r[:, w:w + HEADS] for r in refs]
        mx = jnp.maximum(jnp.maximum(l[0], l[1]), l[2])
        e = [jnp.exp(v - mx) for v in l]
        z = e[0] + e[1] + e[2]
        lse_ref[...] = mx + jnp.log(z)
        wts = [v / z for v in e]
        for h in range(HEADS):
            cols = slice(h * HEAD_DIM, (h + 1) * HEAD_DIM)
            acc = wts[0][:, h:h + 1] * refs[0][:, cols]
            for g in range(1, N_GROUPS_A):
                acc = acc + wts[g][:, h:h + 1] * refs[g][:, cols]
            m_ref[:, cols] = acc
            mb_ref[:, cols] = acc.astype(BF16)

    ext = pl.BlockSpec((ROW_TILE, ATTN_EXT), lambda i: (i, 0))
    row = pl.BlockSpec((ROW_TILE, w), lambda i: (i, 0))
    return pl.pallas_call(
        body, name=name, grid=(s // ROW_TILE,),
        in_specs=[ext, ext, ext],
        out_specs=[row, row, pl.BlockSpec((ROW_TILE, HEADS), lambda i: (i, 0))],
        out_shape=[jax.ShapeDtypeStruct((s, w), F32), jax.ShapeDtypeStruct((s, w), BF16),
                   jax.ShapeDtypeStruct((s, HEADS), F32)],
        compiler_params=_params(("parallel",)),
    )(e0, e1, e2)


def _attn_dvec(dmerged, merged, lse_all, name, dep=None):
    s, w = merged.shape
    deps = [] if dep is None else [dep]

    def body(dm_ref, m_ref, lse_ref, *rest):
        de_ref = rest[-1]
        dmv = dm_ref[...]
        de_ref[:, :w] = dmv
        de_ref[:, w:] = jnp.zeros((ROW_TILE, PANEL), F32)
        de_ref[:, w:w + HEADS] = lse_ref[...]
        prod = dmv * m_ref[...]
        for h in range(HEADS):
            lane = w + DVEC_LANE + h
            de_ref[:, lane:lane + 1] = jnp.sum(prod[:, h * HEAD_DIM:(h + 1) * HEAD_DIM], axis=-1, keepdims=True)

    row = pl.BlockSpec((ROW_TILE, w), lambda i: (i, 0))
    return pl.pallas_call(
        body, name=name, grid=(s // ROW_TILE,),
        in_specs=[row, row, pl.BlockSpec((ROW_TILE, HEADS), lambda i: (i, 0))] + [_ANY] * len(deps),
        out_specs=pl.BlockSpec((ROW_TILE, ATTN_EXT), lambda i: (i, 0)),
        out_shape=jax.ShapeDtypeStruct((s, ATTN_EXT), F32),
        compiler_params=_params(("parallel",)),
    )(dmerged, merged, lse_all, *deps)


def _attention_fwd(n, wqkv, wo, tag):
    ns, qkvs, exts = [], [], []
    for g, (_, dil) in enumerate(DILATED_CFG):
        ng = n if dil == 1 else _residue_order(n, dil, f"{tag}_order_g{g}")
        qkv = _qkv_proj(ng, wqkv, g, f"{tag}_qkv_g{g}")
        ext = _attn_fwd(qkv, g, f"{tag}_fwd_g{g}")
        ns.append(ng)
        qkvs.append(qkv)
        exts.append(ext if dil == 1 else _token_order(ext, dil, None, f"{tag}_unorder_g{g}"))
    merged, merged_bf, lse_all = _attn_merge(*exts, f"{tag}_merge")
    m = _matmul(merged_bf, wo, "nn", F32, f"{tag}_wo")
    return m, (ns, qkvs, merged, merged_bf, lse_all)


def _attention_bwd(dm, wqkv, wo, saved, tag, dep=None, hook=None):
    ns, qkvs, merged, merged_bf, lse_all = saved
    d_wo = _matmul(merged_bf, dm, "tn", BF16, f"{tag}_dwo")
    dmerged = _matmul(dm, wo, "nt", F32, f"{tag}_dmerged")
    dext = _attn_dvec(dmerged, merged, lse_all, f"{tag}_dvec", dep)
    width = 3 * ATTN_WIDTH
    d_wqkv, dn, dep = [], None, None
    for g, (_, dil) in enumerate(DILATED_CFG):
        dext_g = dext if dil == 1 else _residue_order(dext, dil, f"{tag}_dorder_g{g}")
        dqkv = _attn_bwd(qkvs[g], dext_g, g, f"{tag}_bwd_g{g}", dep)
        dep = hook(g, dqkv) if hook is not None and g + 1 < N_GROUPS_A else None
        d_wqkv.append(_matmul(ns[g], dqkv, "tn", BF16, f"{tag}_dwqkv_g{g}", b_parts=3))
        dn_g = _matmul(dqkv, wqkv[:, g * width:(g + 1) * width], "nt", F32, f"{tag}_dn_g{g}", a_parts=3)
        dn = dn_g if dil == 1 else _token_order(dn_g, dil, dn, f"{tag}_dn_sum_g{g}")
    return dn, jnp.concatenate(d_wqkv, axis=1), d_wo


def _layer_matrices(i):
    mixer = (("attn_w_qkv", "attn_w_o"), ("conv_w_in", "conv_w_out"), ("pool_w_in", "pool_w_grp", "pool_w_out"))[i % 3]
    return [(k, i // 3) for k in mixer] + [("ffn_w_up", i), ("ffn_w_down", i)]


def _local_step(x, tgt, vec, weights, sink):
    ng = vec["norm_g"]

    def gain(i, j, token=None):
        g = ng[i, j][None, :]
        return g if token is None else g + token

    saved = []
    n = _rms_fwd(x, gain(0, 0), None, BF16, "norm_first")
    for i in range(DEPTH):
        wl = weights.layer(i)
        t0 = weights.hook(i, 0, n)
        kind, idx = i % 3, i // 3
        if kind == 0:
            m, ms = _attention_fwd(n, wl["attn_w_qkv"], wl["attn_w_o"], "attn")
        elif kind == 1:
            taps = vec["conv_w_dw"][idx] if t0 is None else vec["conv_w_dw"][idx] + t0
            z, y = _sconv_fwd(n, wl["conv_w_in"], taps, "sconv_fwd")
            m = _matmul(y, wl["conv_w_out"], "nn", F32, "sconv_out")
            ms = (z, y)
        else:
            scale = vec["pool_scale"][idx][None, :] if t0 is None else vec["pool_scale"][idx][None, :] + t0
            p, y = _pool_fwd(n, wl["pool_w_in"], wl["pool_w_grp"], scale, "pool_fwd")
            m = _matmul(y, wl["pool_w_out"], "nn", F32, "pool_out")
            ms = (p, y)
        t1 = weights.hook(i, 1, m)
        x1, n2 = _rms_res_pre(m, gain(i, 1, t0), x, gain(i, 2, t1), "norm_res_pre")
        h, a = _ffn_up(n2, wl["ffn_w_up"], vec["ffn_w_dw"][i], "ffn_up")
        t2 = weights.hook(i, 2, a)
        f = _matmul(a, wl["ffn_w_down"].reshape(D_FF, D_MODEL), "nn", F32, "ffn_down", a_parts=FFN_PAIRS)
        saved.append((x, n, m, ms, x1, n2, h, a, f, wl))
        if i + 1 < DEPTH:
            x, n = _rms_res_pre(f, gain(i, 3, t2), x1, gain(i + 1, 0), "norm_res_pre")
        else:
            x = _rms_fwd(f, gain(i, 3), x1, F32, "norm_res")
        weights.hook(i, 3, x)

    loss, dx = _loss_head(x, tgt, "loss_head")

    g_norm = [[None] * 4 for _ in range(DEPTH)]
    g_taps, g_scale, g_ffn_dw = [], [], [None] * DEPTH
    df, g_norm[DEPTH - 1][3] = _rms_bwd(saved[-1][8], gain(DEPTH - 1, 3), dx, None, BF16, "norm_bwd_sub")
    t0 = None
    for i in reversed(range(DEPTH)):
        xin, n, m, ms, x1, n2, h, a, f, wl = saved[i]
        kind, idx = i % 3, i // 3
        gl = {}
        d_wdown = _matmul(a, df, "tn", BF16, "ffn_dwdown", a_parts=FFN_PAIRS)
        gl["ffn_w_down"] = d_wdown.reshape(N_DEV, D_FF // N_DEV, D_MODEL)
        ffn_taps = vec["ffn_w_dw"][i] if t0 is None else vec["ffn_w_dw"][i] + t0
        dh, dwg, dwu = _ffn_mid_bwd(df, wl["ffn_w_down"].reshape(FFN_PAIRS, -1, D_MODEL), h, ffn_taps, "ffn_mid_bwd")
        g_ffn_dw[i] = jnp.concatenate([dwg, dwu], axis=0)
        t1 = sink.hook(i, 1, dh)
        gl["ffn_w_up"] = _ffn_dwup(n2, dh, "ffn_dwup")
        tf = sink.ffn_done(i, gl)
        dn2 = _ffn_dn(dh, wl["ffn_w_up"], "ffn_dn", t1)
        dx1, dm, g_norm[i][2], g_norm[i][1] = _rms_bwd_pair(x1, gain(i, 2, tf), dn2, dx, m, gain(i, 1), "norm_bwd_pair")
        t2 = sink.hook(i, 2, dm)
        if kind == 0:
            dn, gl["attn_w_qkv"], gl["attn_w_o"] = _attention_bwd(
                dm, wl["attn_w_qkv"], wl["attn_w_o"], ms, "attn", t2, lambda g, after, i=i: sink.hook(i, ("a", "b")[g], after))
        elif kind == 1:
            z, y = ms
            gl["conv_w_out"] = _matmul(y, dm, "tn", BF16, "sconv_dwout")
            taps = vec["conv_w_dw"][idx] if t2 is None else vec["conv_w_dw"][idx] + t2
            dz, ddw = _sconv_mid_bwd(dm, wl["conv_w_out"], z, taps, "sconv_mid_bwd")
            g_taps.append(ddw)
            gl["conv_w_in"] = _matmul(n, dz, "tn", BF16, "sconv_dwin", b_parts=3)
            dn = _matmul(dz, wl["conv_w_in"], "nt", F32, "sconv_dn", a_parts=3)
        else:
            p, y = ms
            gl["pool_w_out"] = _matmul(y, dm, "tn", BF16, "pool_dwout")
            scale = vec["pool_scale"][idx][None, :] if t2 is None else vec["pool_scale"][idx][None, :] + t2
            du, gl["pool_w_grp"], dscale = _pool_mid_bwd(dm, wl["pool_w_out"], p, wl["pool_w_grp"], scale, "pool_mid_bwd")
            g_scale.append(dscale[0])
            gl["pool_w_in"] = _matmul(n, du, "tn", BF16, "pool_dwin")
            dn = _matmul(du, wl["pool_w_in"], "nt", F32, "pool_dn")
        sink.hook(i, 3, dn)
        if i > 0:
            dx, df, g_norm[i][0], g_norm[i - 1][3] = _rms_bwd_pair(xin, gain(i, 0, t2), dn, dx1, saved[i - 1][8],
                                                                   gain(i - 1, 3), "norm_bwd_pair")
        else:
            dx, g_norm[0][0] = _rms_bwd(xin, gain(0, 0), dn, dx1, F32, "norm_bwd_res")
        t0 = sink.layer_done(i, gl)

    vec_grads = {"norm_g": jnp.stack([jnp.concatenate(row, axis=0) for row in g_norm]), "conv_w_dw": jnp.stack(g_taps),
                 "pool_scale": jnp.stack(g_scale), "ffn_w_dw": g_ffn_dw}
    return loss, dx, vec_grads


_AXES = ("x", "y", "c")
ROUTE_A = ("y", "x", "c")
ROUTE_B = ("x", "y", "c")
def _dev_index(pos):
    return 4 * pos["x"] + 2 * pos["y"] + pos["c"]


_HBM = pl.BlockSpec(memory_space=pltpu.HBM)
_SEM = pl.BlockSpec(memory_space=pltpu.SEMAPHORE)
_ANY = pl.BlockSpec(memory_space=pl.ANY)
_EFFECT = pltpu.SideEffectType.DATAFLOW_SIDE_EFFECTING


TOKEN_SHAPE = (1, D_MODEL)


def _copies_start(describe, arrays, n_copies, name, after, token_shape=TOKEN_SHAPE):
    n = len(arrays)
    deps = [] if after is None else [after]

    def body(*refs):
        send_sems, recv_sems = refs[n + len(deps)], refs[n + len(deps) + 1]
        for c in describe(refs[:n], send_sems, recv_sems):
            c.start()
        refs[-1][...] = jnp.zeros_like(refs[-1])

    outs = pl.pallas_call(
        body, name=f"{name}_start",
        out_shape=(pltpu.SemaphoreType.DMA((n_copies,)), pltpu.SemaphoreType.DMA((n_copies,)),
                   *[pltpu.HBM(a.shape, a.dtype) for a in arrays], jax.ShapeDtypeStruct(token_shape, F32)),
        in_specs=[_HBM] * n + [_ANY] * len(deps),
        out_specs=(_SEM, _SEM, *([_HBM] * n), pl.BlockSpec(memory_space=pltpu.VMEM)),
        input_output_aliases={i: 2 + i for i in range(n)},
        compiler_params=pltpu.CompilerParams(has_side_effects=_EFFECT),
    )(*[pltpu.with_memory_space_constraint(a, pltpu.HBM) for a in arrays], *deps)
    return (outs[0], outs[1], list(outs[2:2 + n])), outs[-1]


def _copies_wait(describe, handle, name, after):
    send_sems, recv_sems, arrays = handle
    n = len(arrays)
    deps = [] if after is None else [after]

    def body(*refs):
        for c in describe(refs[:n], refs[n], refs[n + 1]):
            c.wait_send()
            c.wait_recv()

    outs = pl.pallas_call(
        body, name=f"{name}_wait",
        out_shape=tuple(pltpu.HBM(a.shape, a.dtype) for a in arrays),
        in_specs=[_HBM] * n + [_SEM, _SEM] + [_ANY] * len(deps), out_specs=tuple([_HBM] * n),
        input_output_aliases={i: i for i in range(n)},
        compiler_params=pltpu.CompilerParams(has_side_effects=_EFFECT),
    )(*arrays, send_sems, recv_sems, *deps)
    return list(outs)


GATHER_STAGE_COPIES = (3, 3, 1)


def _gather_copies(stage, routes):
    n = len(routes)

    def describe(refs, send_sems, recv_sems):
        pos = {a: lax.axis_index(a) for a in _AXES}

        def flipped(axes):
            return {a: 1 - pos[a] if a in axes else pos[a] for a in _AXES}

        copies = []
        for i, (a1, a2, a3) in enumerate(routes):
            land = refs[n + i] if stage == 1 else refs[i]
            p1, p2, p12, p3 = flipped((a1,)), flipped((a2,)), flipped((a1, a2)), flipped((a3,))
            plan = {1: [(None, p1), (None, p2), (None, p3)], 2: [(p1, p2), (p1, p3), (p2, p3)], 3: [(p12, p3)]}[stage]
            for holder, to in plan:
                slot = land.at[_dev_index(pos if holder is None else holder)]
                k = len(copies)
                copies.append(pltpu.make_async_remote_copy(
                    src_ref=refs[i] if holder is None else slot, dst_ref=slot,
                    send_sem=send_sems.at[k], recv_sem=recv_sems.at[k],
                    device_id=tuple(to[a] for a in _AXES), device_id_type=pl.DeviceIdType.MESH))
        return copies

    return describe


def _gather_begin(shards, routes, name, after):
    n = len(shards)
    lands = [lax.empty((N_DEV,) + a.shape, a.dtype) for a in shards]
    handle, token = _copies_start(_gather_copies(1, routes), list(shards) + lands, GATHER_STAGE_COPIES[0] * n,
                                  f"{name}_1", after)
    return {"stage": 1, "handle": handle, "routes": routes, "name": name, "n": n}, token


def _gather_next(state, after):
    stage, routes, name, n = state["stage"], state["routes"], state["name"], state["n"]
    arrays = _copies_wait(_gather_copies(stage, routes), state["handle"], f"{name}_{stage}", after)
    if stage == 1:
        state = dict(state, shards=arrays[:n])
        arrays = arrays[n:]
    if stage == 3:
        me = _dev_index({a: lax.axis_index(a) for a in _AXES})
        return [lax.dynamic_update_index_in_dim(o, s, me, 0) for o, s in zip(arrays, state["shards"])], None
    handle, token = _copies_start(_gather_copies(stage + 1, routes), arrays, GATHER_STAGE_COPIES[stage] * n,
                                  f"{name}_{stage + 1}", None)
    return dict(state, stage=stage + 1, handle=handle), token


ADD_ROW_TILES = (1024, 704, 512, 352, 256, 128, 96, 64, 32, 16)


def _add_half(a, recv, me, out_dtype, name):
    p, q, cols = recv.shape
    tr = _pick(q, ADD_ROW_TILES)

    def body(me_ref, a_ref, b_ref, o_ref):
        o_ref[...] = (a_ref[...].astype(F32) + b_ref[...].astype(F32)).astype(o_ref.dtype)

    return pl.pallas_call(
        body, name=name,
        grid_spec=pltpu.PrefetchScalarGridSpec(
            num_scalar_prefetch=1, grid=(p, q // tr),
            in_specs=[pl.BlockSpec((None, None, tr, cols), lambda j, i, m: (j, m[0], i, 0)),
                      pl.BlockSpec((None, tr, cols), lambda j, i, m: (j, i, 0))],
            out_specs=pl.BlockSpec((None, tr, cols), lambda j, i, m: (j, i, 0))),
        out_shape=jax.ShapeDtypeStruct((p, q, cols), out_dtype),
        compiler_params=_params(("parallel", "parallel")),
    )(me, a, recv)


def _half_copies(axes):
    n = len(axes)

    def describe(refs, send_sems, recv_sems):
        pos = {a: lax.axis_index(a) for a in _AXES}
        copies = []
        for i, axis in enumerate(axes):
            peer = tuple(1 - pos[a] if a == axis else pos[a] for a in _AXES)
            copies.append(pltpu.make_async_remote_copy(
                src_ref=refs[i].at[:, 1 - pos[axis]], dst_ref=refs[n + i], send_sem=send_sems.at[i],
                recv_sem=recv_sems.at[i], device_id=peer, device_id_type=pl.DeviceIdType.MESH))
        return copies

    return describe


def _scatter_begin(slots, routes, tags, name, token_shape=TOKEN_SHAPE):
    shapes = [a.shape[1:] for a in slots]
    rows = [math.prod(s[:-1]) for s in shapes]
    arrays = [a.reshape(4, 2, n, s[-1]) for a, n, s in zip(slots, rows, shapes)]
    return _scatter_start({"stage": 0, "arrays": arrays, "routes": routes, "tags": tags, "name": name,
                           "shapes": shapes, "rows": rows}, token_shape)


def _scatter_start(state, token_shape=TOKEN_SHAPE):
    stage, arrays = state["stage"], state["arrays"]
    axes = [r[2 - stage] for r in state["routes"]]
    lands = [lax.empty((a.shape[0],) + a.shape[2:], a.dtype) for a in arrays]
    handle, token = _copies_start(_half_copies(axes), arrays + lands, len(arrays), f"{state['name']}_{stage + 1}", None,
                                  token_shape)
    return dict(state, handle=handle, axes=axes), token


def _scatter_next(state, after):
    stage, axes, n = state["stage"], state["axes"], len(state["arrays"])
    both = _copies_wait(_half_copies(axes), state["handle"], f"{state['name']}_{stage + 1}", after)
    coord = {a: lax.axis_index(a).astype(jnp.int32).reshape(1) for a in _AXES}
    sums = [_add_half(a, r, coord[ax], F32 if stage == 2 else BF16, f"scatter_add_{stage + 1}_{t}")
            for a, r, ax, t in zip(both[:n], both[n:], axes, state["tags"])]
    if stage == 2:
        return [a.reshape(s) for a, s in zip(sums, state["shapes"])], None
    if stage == 0:
        views = [(1, 2, 2 * r, s[-1]) if route[1] == "x" else (2, 2, r, s[-1])
                 for r, s, route in zip(state["rows"], state["shapes"], state["routes"])]
    else:
        views = [(1, 2, r, s[-1]) for r, s in zip(state["rows"], state["shapes"])]
    return _scatter_start(dict(state, stage=stage + 1, arrays=[a.reshape(v) for a, v in zip(sums, views)]))


_WEIGHTS = {
    "norm_g": ((DEPTH, 4, D_MODEL), 2, True),
    "attn_w_qkv": ((2, D_MODEL, 4608), 2, False),
    "attn_w_o": ((2, ATTN_WIDTH, D_MODEL), 2, False),
    "conv_w_in": ((1, D_MODEL, 3 * D_MODEL), 2, False),
    "conv_w_dw": ((1, 3, D_MODEL), 2, True),
    "conv_w_out": ((1, D_MODEL, D_MODEL), 1, False),
    "pool_w_in": ((1, D_MODEL, D_MODEL), 1, False),
    "pool_w_grp": ((1, 4, POOL_GROUP_DIM, POOL_GROUP_DIM), 2, False),
    "pool_scale": ((1, D_MODEL), 1, True),
    "pool_w_out": ((1, D_MODEL, D_MODEL), 1, False),
    "ffn_w_up": ((DEPTH, D_MODEL, 2 * D_FF), 2, False),
    "ffn_w_dw": ((DEPTH, 3, 2 * D_FF), 2, True),
    "ffn_w_down": ((DEPTH, D_FF, D_MODEL), 1, False),
}
_NAMES = tuple(_WEIGHTS)
_VECTORS = tuple(k for k in _NAMES if _WEIGHTS[k][2])
_MATRICES = tuple(k for k in _NAMES if not _WEIGHTS[k][2])
_FFN = ("ffn_w_up", "ffn_w_down")
_ON_ROUTE_A = ("ffn_w_up", "attn_w_o", "conv_w_out", "pool_w_in")
PACK_ROWS = 16


def _route(name):
    return ROUTE_A if name in _ON_ROUTE_A else ROUTE_B


def _shard_shape(name):
    shape, ax, _ = _WEIGHTS[name]
    return tuple(s // N_DEV if i == ax else s for i, s in enumerate(shape))


def _full_from_slots(slots, name, layers=None):
    shape, ax, _ = _WEIGHTS[name]
    if layers is not None:
        shape = (layers,) + shape[1:]
    return jnp.moveaxis(slots, 0, ax).reshape(shape)


def _slots_from_full(full, name):
    shape, ax, _ = _WEIGHTS[name]
    split = shape[:ax] + (N_DEV, shape[ax] // N_DEV) + shape[ax + 1:]
    return jnp.moveaxis(full.reshape(split), ax, 0)


def _pack_vectors(parts, lead):
    rows = []
    for k in _VECTORS:
        r = parts[k].reshape(lead + (-1, LANES))
        pad = -r.shape[-2] % PACK_ROWS
        rows.append(jnp.pad(r, [(0, 0)] * len(lead) + [(0, pad), (0, 0)]))
    return jnp.concatenate(rows, axis=len(lead))


def _unpack_vectors(buf, lead):
    out, r0 = {}, 0
    for k in _VECTORS:
        shard = _shard_shape(k)
        rows = math.prod(shard) // LANES
        out[k] = buf[..., r0:r0 + rows, :].reshape(lead + shard)
        r0 += rows + (-rows % PACK_ROWS)
    return out


class _LayerWeights:
    def __init__(self, shards):
        self.cast = {k: shards[k].astype(BF16) for k in _MATRICES}
        first, ffn0 = _layer_matrices(0)[:-2], _layer_matrices(0)[-2:]
        state, _ = _gather_begin(self._send(first) + [_pack_vectors(shards, ())],
                                 [_route(k) for k, _ in first] + [ROUTE_B], "gather0", None)
        for _ in range(2):
            state, _ = _gather_next(state, None)
        outs, _ = _gather_next(state, None)
        vec = _unpack_vectors(outs[-1], (N_DEV,))
        self.vec = {k: _full_from_slots(vec[k], k) for k in _VECTORS}
        self.vec["ffn_w_dw"] = [vec["ffn_w_dw"][:, l] for l in range(DEPTH)]
        self.ready = {0: self._unpack(first, outs[:-1])}
        self.chains = {}
        tokens = []
        self._begin("ffn0", ffn0, "gather0f", outs[0], tokens)
        self._begin(1, _layer_matrices(1), "gather1", outs[0], tokens)
        self.vec["norm_g"] = self.vec["norm_g"] + (tokens[0] + tokens[1])

    def _send(self, items):
        return [self.cast[k][j] for k, j in items]

    @staticmethod
    def _unpack(items, outs):
        return {k: o if k in _FFN else _full_from_slots(o[:, None], k, layers=1)[0] for (k, _), o in zip(items, outs)}

    def _begin(self, key, items, name, after, tokens):
        state, token = _gather_begin(self._send(items), [_route(k) for k, _ in items], name, after)
        self.chains[key] = (items, state)
        tokens.append(token)

    def _advance(self, key, after, tokens):
        items, state = self.chains.pop(key)
        state, token = _gather_next(state, after)
        if token is None:
            self.ready.setdefault(0 if key == "ffn0" else key, {}).update(self._unpack(items, state))
        else:
            self.chains[key] = (items, state)
            tokens.append(token)

    def layer(self, i):
        return self.ready[i]

    def hook(self, i, point, after):
        tokens = []
        if i == 0 and point == 0:
            self._advance("ffn0", after, tokens)
        if i == 0 and point == 1:
            self._advance("ffn0", after, tokens)
            self._advance("ffn0", None, tokens)
        if point >= 1 and i + 1 in self.chains:
            self._advance(i + 1, after, tokens)
        if point == 1 and i + 2 < DEPTH:
            self._begin(i + 2, _layer_matrices(i + 2), f"gather{i + 2}", after, tokens)
        return functools.reduce(lambda a, b: a + b, tokens) if tokens else None


def _layer_slots(g, name):
    shape, ax, _ = _WEIGHTS[name]
    shape, ax = shape[1:], ax - 1
    split = shape[:ax] + (N_DEV, shape[ax] // N_DEV) + shape[ax + 1:]
    return jnp.moveaxis(g.reshape(split), ax, 0).astype(BF16)


class _GradSink:
    def __init__(self):
        self.state = None
        self.ffn_state = None
        self.sums = {}
        self.last = None

    def ffn_done(self, i, grads):
        if i != 0:
            return None
        self.ffn_items = _layer_matrices(0)[-2:]
        self.ffn_state, token = _scatter_begin([grads[k] for k, _ in self.ffn_items],
                                               [_route(k) for k, _ in self.ffn_items],
                                               [f"{k}{j}" for k, j in self.ffn_items], "scatter0f")
        return token

    def layer_done(self, i, grads):
        items = _layer_matrices(i)
        if i == 0:
            items = items[:-2]
            self.last = (items, [_layer_slots(grads[k], k) for k, _ in items])
            return None
        slots = [grads[k] if k in _FFN else _layer_slots(grads[k], k) for k, _ in items]
        self.items = items
        self.state, token = _scatter_begin(slots, [_route(k) for k, _ in items], [f"{k}{j}" for k, j in items],
                                           f"scatter{i}", (N_DEV, 3, 2 * D_FF // N_DEV))
        return token

    def hook(self, i, point, after):
        tokens = []
        if self.state is not None and point in (1, 2, 3):
            self.state, token = _scatter_next(self.state, after)
            if point == 3:
                self.sums.update(dict(zip(self.items, self.state)))
                self.state = None
            tokens.append(token)
        if self.ffn_state is not None and point in ("a", "b", 3):
            self.ffn_state, token = _scatter_next(self.ffn_state, after)
            if point == 3:
                self.sums.update(dict(zip(self.ffn_items, self.ffn_state)))
                self.ffn_state = None
            tokens.append(token)
        tokens = [t for t in tokens if t is not None]
        return functools.reduce(lambda a, b: a + b, tokens) if tokens else None


def _adamw(w, g, m, v, name, layer=None, prev=None):
    shape = w.shape
    cols = shape[-1]
    view = shape if len(shape) == 3 else (1, math.prod(shape[:-1]), cols)
    layers, rows, _ = view
    tr = _pick(rows, (512, 256, 128, 64, 32, 16, 8))
    n_prev = 0 if prev is None else 3

    def body(*refs):
        w_ref, g_ref, m_ref, v_ref = refs[n_prev:n_prev + 4]
        d_ref, nm_ref, nv_ref = refs[n_prev + 4:]
        gv = g_ref[...]
        nm = ADAM_B1 * m_ref[...] + (1.0 - ADAM_B1) * gv
        nv = ADAM_B2 * v_ref[...] + (1.0 - ADAM_B2) * jnp.square(gv)
        m_hat = nm / (1.0 - ADAM_B1 ** ADAM_STEP)
        v_hat = nv / (1.0 - ADAM_B2 ** ADAM_STEP)
        d_ref[...] = -ADAM_LR * (m_hat / (jnp.sqrt(v_hat) + ADAM_EPS) + ADAM_WD * w_ref[...])
        nm_ref[...] = nm
        nv_ref[...] = nv

    if layer is None:
        grid = (layers, rows // tr)
        blk = gblk = pl.BlockSpec((None, tr, cols), lambda l, i: (l, i, 0))
        gview = view
    else:
        grid = (rows // tr,)
        blk = pl.BlockSpec((None, tr, cols), lambda i: (layer, i, 0))
        gblk = pl.BlockSpec((tr, cols), lambda i: (i, 0))
        gview = (rows, cols)
    shp = jax.ShapeDtypeStruct(view, F32)
    outs = pl.pallas_call(
        body, name=name, grid=grid, in_specs=[_ANY] * n_prev + [blk, gblk, blk, blk], out_specs=[blk] * 3,
        out_shape=[shp] * 3, input_output_aliases={i: i for i in range(n_prev)},
        compiler_params=_params(("parallel",) * len(grid)),
    )(*([] if prev is None else [p.reshape(view) for p in prev]), w.reshape(view), g.reshape(gview), m.reshape(view),
      v.reshape(view))
    return [o.reshape(shape) for o in outs]


def kernel(x, norm_g, attn_w_qkv, attn_w_o, conv_w_in, conv_w_dw, conv_w_out, pool_w_in, pool_w_grp, pool_scale, pool_w_out, ffn_w_up, ffn_w_dw, ffn_w_down, loss_target, m_norm_g, m_attn_w_qkv, m_attn_w_o, m_conv_w_in, m_conv_w_dw, m_conv_w_out, m_pool_w_in, m_pool_w_grp, m_pool_scale, m_pool_w_out, m_ffn_w_up, m_ffn_w_dw, m_ffn_w_down, v_norm_g, v_attn_w_qkv, v_attn_w_o, v_conv_w_in, v_conv_w_dw, v_conv_w_out, v_pool_w_in, v_pool_w_grp, v_pool_scale, v_pool_w_out, v_ffn_w_up, v_ffn_w_dw, v_ffn_w_down):
    shards = dict(zip(_NAMES, (norm_g, attn_w_qkv, attn_w_o, conv_w_in, conv_w_dw, conv_w_out, pool_w_in,
                               pool_w_grp, pool_scale, pool_w_out, ffn_w_up, ffn_w_dw, ffn_w_down)))
    moms = dict(zip(_NAMES, (m_norm_g, m_attn_w_qkv, m_attn_w_o, m_conv_w_in, m_conv_w_dw, m_conv_w_out,
                             m_pool_w_in, m_pool_w_grp, m_pool_scale, m_pool_w_out, m_ffn_w_up, m_ffn_w_dw,
                             m_ffn_w_down)))
    vels = dict(zip(_NAMES, (v_norm_g, v_attn_w_qkv, v_attn_w_o, v_conv_w_in, v_conv_w_dw, v_conv_w_out,
                             v_pool_w_in, v_pool_w_grp, v_pool_scale, v_pool_w_out, v_ffn_w_up, v_ffn_w_dw,
                             v_ffn_w_down)))
    weights = _LayerWeights(shards)
    sink = _GradSink()
    loss, grad_x, vec_grads = _local_step(x[0], loss_target[0], weights.vec, weights, sink)
    loss = lax.psum(loss[0, 0], _AXES)

    items, slots = sink.last
    vec_slots = {k: _slots_from_full(vec_grads[k], k) for k in _VECTORS if k != "ffn_w_dw"}
    vec_slots["ffn_w_dw"] = jnp.stack(vec_grads["ffn_w_dw"], axis=1)
    state, _ = _scatter_begin(slots + [_pack_vectors(vec_slots, (N_DEV,)).astype(BF16)],
                              [_route(k) for k, _ in items] + [ROUTE_B], [f"{k}{j}" for k, j in items] + ["vectors"],
                              "scatter0")
    results = {}

    def step_layer(i):
        last = None
        for k, j in _layer_matrices(i):
            g = sink.sums[(k, j)]
            if _WEIGHTS[k][0][0] == 1:
                results[k] = (g[None], _adamw(shards[k], g[None], moms[k], vels[k], f"adamw_{k}"))
            else:
                gs, prev = results.get(k, ({}, None))
                gs[j] = g
                results[k] = (gs, _adamw(shards[k], g, moms[k], vels[k], f"adamw_{k}{j}", layer=j, prev=prev))
            last = results[k][1][0]
        return last

    for i in (3, 2):
        state, _ = _scatter_next(state, step_layer(i))
    sums, _ = _scatter_next(state, step_layer(1))
    sink.sums.update(dict(zip(items, sums[:-1])))
    step_layer(0)
    vec_sums = _unpack_vectors(sums[-1], ())
    for k in _VECTORS:
        results[k] = (vec_sums[k], _adamw(shards[k], vec_sums[k], moms[k], vels[k], f"adamw_{k}"))
    grads_out = {k: g if not isinstance(g, dict) else jnp.stack([g[j] for j in range(len(g))])
                 for k, (g, _) in results.items()}
    return (loss, grad_x[None], *[grads_out[k] for k in _NAMES], *[results[k][1][0] for k in _NAMES],
            *[results[k][1][1] for k in _NAMES], *[results[k][1][2] for k in _NAMES])
```

```python
import functools
import math

import numpy as np
import jax
import jax.numpy as jnp
from jax import lax
from jax.experimental import pallas as pl
from jax.experimental.pallas import tpu as pltpu

F32, BF16 = jnp.float32, jnp.bfloat16

D_MODEL = 1024
SEQ = 2048
DEPTH = 4
DILATED_CFG = ((128, 1), (512, 4), (2048, 16))
N_GROUPS_A = 3
HEADS = 8
HEAD_DIM = 64
ATTN_WIDTH = HEADS * HEAD_DIM
N_HEADS_A = N_GROUPS_A * HEADS
BLOCK = 128
NEG_INF = -1e30
POOL_GROUP_DIM = 256
D_FF = 2816
RMS_EPS = 1e-6
ADAM_LR, ADAM_B1, ADAM_B2, ADAM_EPS, ADAM_WD, ADAM_STEP = 0.001, 0.9, 0.999, 1e-08, 0.01, 10

N_DEV = 8
LANES = 128
V7X_VMEM_BYTES = 64 * 2 ** 20
VMEM_LIMIT_BYTES = V7X_VMEM_BYTES - 8 * 2 ** 20
COL_TILE = 256
ROW_TILE = 256
MATMUL_TILES = (1024, 1408, 512, 256, 128)
TN_RESIDENT_K = 2048

NN = (((1,), (0,)), ((), ()))
NT = (((1,), (1,)), ((), ()))
TN = (((0,), (0,)), ((), ()))


def _dot(a, b, dims=NN):
    return lax.dot_general(a, b, dims, preferred_element_type=F32)


def _params(sem=None):
    return pltpu.CompilerParams(dimension_semantics=sem, vmem_limit_bytes=VMEM_LIMIT_BYTES)


def _pick(n, prefs):
    for p in prefs:
        if n % p == 0:
            return p
    return n


def _matmul(a, b, mode, out_dtype, name, a_parts=1, b_parts=1):
    if mode == "nn":
        m, k = a.shape[-2], a.shape[-1] * a_parts
        n = b.shape[-1] * b_parts
    elif mode == "nt":
        m, k = a.shape[-2], a.shape[-1] * a_parts
        n = b.shape[-2]
    else:
        k, m = a.shape[-2], a.shape[-1] * a_parts
        n = b.shape[-1] * b_parts
    tm = _pick(m, MATMUL_TILES)
    tn = _pick(n // b_parts if mode != "nt" else n, MATMUL_TILES)
    kk = k // a_parts if mode != "tn" else k
    tk = _pick(kk, MATMUL_TILES)
    if mode == "tn":
        tm = _pick(m // a_parts, MATMUL_TILES)
        if k <= TN_RESIDENT_K:
            tk = k
    gm, gn, gk = m // tm, n // tn, k // tk

    def a_idx(i, j, kq):
        if mode == "tn":
            r, c, per = kq, i, (m // a_parts) // tm
        else:
            r, c, per = i, kq, (k // a_parts) // tk
        return (r, c) if a_parts == 1 else (c // per, r, c % per)

    def b_idx(i, j, kq):
        if mode == "nt":
            return (j, kq)
        per = (n // b_parts) // tn
        return (kq, j) if b_parts == 1 else (j // per, kq, j % per)

    a_blk = (tk, tm) if mode == "tn" else (tm, tk)
    b_blk = (tn, tk) if mode == "nt" else (tk, tn)
    if a_parts > 1:
        a_blk = (None,) + a_blk
    if b_parts > 1:
        b_blk = (None,) + b_blk
    dims = {"nn": NN, "nt": NT, "tn": TN}[mode]

    def body_single(a_ref, b_ref, o_ref):
        o_ref[...] = _dot(a_ref[...], b_ref[...], dims).astype(o_ref.dtype)

    def body(a_ref, b_ref, o_ref, acc_ref):
        kq = pl.program_id(2)

        @pl.when(kq == 0)
        def _():
            acc_ref[...] = jnp.zeros_like(acc_ref)

        acc_ref[...] += _dot(a_ref[...], b_ref[...], dims)

        @pl.when(kq == gk - 1)
        def _():
            o_ref[...] = acc_ref[...].astype(o_ref.dtype)

    return pl.pallas_call(
        body_single if gk == 1 else body, name=name, grid=(gm, gn, gk),
        in_specs=[pl.BlockSpec(a_blk, a_idx), pl.BlockSpec(b_blk, b_idx)],
        out_specs=pl.BlockSpec((tm, tn), lambda i, j, kq: (i, j)),
        out_shape=jax.ShapeDtypeStruct((m, n), out_dtype),
        scratch_shapes=[] if gk == 1 else [pltpu.VMEM((tm, tn), F32)],
        compiler_params=_params(("parallel", "parallel", "arbitrary")),
    )(a, b)


def _rms_fwd(xin, g, res, out_dtype, name):
    s, d = xin.shape
    has_res = res is not None

    def body(*refs):
        x_ref, g_ref = refs[0], refs[1]
        o_ref = refs[-1]
        x = x_ref[...]
        r = lax.rsqrt(jnp.mean(x * x, axis=-1, keepdims=True) + RMS_EPS)
        y = x * r * g_ref[...]
        if has_res:
            y = refs[2][...] + y
        o_ref[...] = y.astype(o_ref.dtype)

    row = pl.BlockSpec((ROW_TILE, d), lambda i: (i, 0))
    vec = pl.BlockSpec((1, d), lambda i: (0, 0))
    ins = [xin, g] + ([res] if has_res else [])
    return pl.pallas_call(
        body, name=name, grid=(s // ROW_TILE,),
        in_specs=[row, vec] + ([row] if has_res else []),
        out_specs=row, out_shape=jax.ShapeDtypeStruct((s, d), out_dtype),
        compiler_params=_params(("parallel",)),
    )(*ins)


def _rms_bwd(xin, g, dy, dres, out_dtype, name):
    s, d = xin.shape
    has_res = dres is not None

    def body(*refs):
        x_ref, g_ref, dy_ref = refs[0], refs[1], refs[2]
        dx_ref, dg_ref = refs[-2], refs[-1]

        @pl.when(pl.program_id(0) == 0)
        def _():
            dg_ref[...] = jnp.zeros_like(dg_ref)

        x = x_ref[...]
        dyv = dy_ref[...].astype(F32)
        r = lax.rsqrt(jnp.mean(x * x, axis=-1, keepdims=True) + RMS_EPS)
        xhat = x * r
        u = dyv * g_ref[...]
        dx = r * (u - xhat * jnp.mean(u * xhat, axis=-1, keepdims=True))
        if has_res:
            dx = refs[3][...] + dx
        dx_ref[...] = dx.astype(dx_ref.dtype)
        dg_ref[...] += jnp.sum(dyv * xhat, axis=0, keepdims=True)

    row = pl.BlockSpec((ROW_TILE, d), lambda i: (i, 0))
    vec = pl.BlockSpec((1, d), lambda i: (0, 0))
    ins = [xin, g, dy] + ([dres] if has_res else [])
    return pl.pallas_call(
        body, name=name, grid=(s // ROW_TILE,),
        in_specs=[row, vec, row] + ([row] if has_res else []),
        out_specs=[row, vec],
        out_shape=[jax.ShapeDtypeStruct((s, d), out_dtype), jax.ShapeDtypeStruct((1, d), F32)],
        compiler_params=_params(("arbitrary",)),
    )(*ins)


def _rms(x):
    r = lax.rsqrt(jnp.mean(x * x, axis=-1, keepdims=True) + RMS_EPS)
    return r, x * r


def _rms_grad(r, xhat, dy, g):
    u = dy * g
    return r * (u - xhat * jnp.mean(u * xhat, axis=-1, keepdims=True))


def _rms_res_pre(sub, g_post, res, g_pre, name):
    s, d = sub.shape

    def body(sub_ref, gp_ref, res_ref, gn_ref, x_ref, n_ref):
        xnew = res_ref[...] + _rms(sub_ref[...])[1] * gp_ref[...]
        x_ref[...] = xnew
        n_ref[...] = (_rms(xnew)[1] * gn_ref[...]).astype(BF16)

    row = pl.BlockSpec((ROW_TILE, d), lambda i: (i, 0))
    vec = pl.BlockSpec((1, d), lambda i: (0, 0))
    return pl.pallas_call(
        body, name=name, grid=(s // ROW_TILE,),
        in_specs=[row, vec, row, vec], out_specs=[row, row],
        out_shape=[jax.ShapeDtypeStruct((s, d), F32), jax.ShapeDtypeStruct((s, d), BF16)],
        compiler_params=_params(("parallel",)),
    )(sub, g_post, res, g_pre)


def _rms_bwd_pair(xmid, g_pre, dn, dres, sub, g_post, name):
    s, d = xmid.shape

    def body(x_ref, gn_ref, dn_ref, dres_ref, sub_ref, gp_ref, dx_ref, dsub_ref, dgn_ref, dgp_ref):
        @pl.when(pl.program_id(0) == 0)
        def _():
            dgn_ref[...] = jnp.zeros_like(dgn_ref)
            dgp_ref[...] = jnp.zeros_like(dgp_ref)

        dnv = dn_ref[...].astype(F32)
        r, xhat = _rms(x_ref[...])
        dx = dres_ref[...] + _rms_grad(r, xhat, dnv, gn_ref[...])
        dx_ref[...] = dx
        dgn_ref[...] += jnp.sum(dnv * xhat, axis=0, keepdims=True)
        rs, shat = _rms(sub_ref[...])
        dsub_ref[...] = _rms_grad(rs, shat, dx, gp_ref[...]).astype(BF16)
        dgp_ref[...] += jnp.sum(dx * shat, axis=0, keepdims=True)

    row = pl.BlockSpec((ROW_TILE, d), lambda i: (i, 0))
    vec = pl.BlockSpec((1, d), lambda i: (0, 0))
    return pl.pallas_call(
        body, name=name, grid=(s // ROW_TILE,),
        in_specs=[row, vec, row, row, row, vec], out_specs=[row, row, vec, vec],
        out_shape=[jax.ShapeDtypeStruct((s, d), F32), jax.ShapeDtypeStruct((s, d), BF16),
                   jax.ShapeDtypeStruct((1, d), F32), jax.ShapeDtypeStruct((1, d), F32)],
        compiler_params=_params(("arbitrary",)),
    )(xmid, g_pre, dn, dres, sub, g_post)


def _loss_head(y, tgt, name):
    s, d = y.shape

    def body(y_ref, t_ref, l_ref, dy_ref):
        @pl.when(pl.program_id(0) == 0)
        def _():
            l_ref[...] = jnp.zeros_like(l_ref)

        e = y_ref[...] - t_ref[...]
        dy_ref[...] = e / d
        per_tok = jnp.mean(e * e, axis=-1, keepdims=True)
        l_ref[...] += 0.5 * jnp.sum(per_tok, axis=0, keepdims=True)

    row = pl.BlockSpec((ROW_TILE, d), lambda i: (i, 0))
    return pl.pallas_call(
        body, name=name, grid=(s // ROW_TILE,),
        in_specs=[row, row],
        out_specs=[pl.BlockSpec((1, 1), lambda i: (0, 0)), row],
        out_shape=[jax.ShapeDtypeStruct((1, 1), F32), jax.ShapeDtypeStruct((s, d), F32)],
        compiler_params=_params(("arbitrary",)),
    )(y, tgt)


SUBLANES = 8


def _shift_down(x, k):
    t, c = x.shape
    r = pltpu.roll(x.reshape(t // SUBLANES, SUBLANES, c), k, axis=1)
    above = jnp.concatenate([jnp.zeros((1, SUBLANES, c), x.dtype), r[:-1]], axis=0)
    rows = lax.broadcasted_iota(jnp.int32, (1, SUBLANES, c), 1)
    return jnp.where(rows >= k, r, above).reshape(t, c)


def _shift_up(x, k):
    t, c = x.shape
    r = pltpu.roll(x.reshape(t // SUBLANES, SUBLANES, c), SUBLANES - k, axis=1)
    below = jnp.concatenate([r[1:], jnp.zeros((1, SUBLANES, c), x.dtype)], axis=0)
    rows = lax.broadcasted_iota(jnp.int32, (1, SUBLANES, c), 1)
    return jnp.where(rows < SUBLANES - k, r, below).reshape(t, c)


def _conv3(h, w):
    return w[2:3] * h + w[1:2] * _shift_down(h, 1) + w[0:1] * _shift_down(h, 2)


def _conv3_bwd(dc, h, w, dw_ref, cols=slice(None)):
    u1, u2 = _shift_up(dc, 1), _shift_up(dc, 2)
    dw_ref[0:1, cols] = jnp.sum(u2 * h, axis=0, keepdims=True)
    dw_ref[1:2, cols] = jnp.sum(u1 * h, axis=0, keepdims=True)
    dw_ref[2:3, cols] = jnp.sum(dc * h, axis=0, keepdims=True)
    return w[2:3] * dc + w[1:2] * u1 + w[0:1] * u2


FFN_PAIRS = N_DEV // 2


def _lane_chunks(width):
    return [(c0, min(COL_TILE, width - c0)) for c0 in range(0, width, COL_TILE)]


def _ffn_up(n, wup, wdw, name):
    s, d = n.shape
    cw = wup.shape[-1]

    def body(n_ref, wg_ref, wu_ref, dg_ref, du_ref, h_ref, c_ref, a_ref):
        x = n_ref[...]
        for c0, size in _lane_chunks(cw):
            cols = slice(c0, c0 + size)
            hg = _dot(x, wg_ref[:, cols])
            hu = _dot(x, wu_ref[:, cols])
            h_ref[0, :, cols] = hg.astype(BF16)
            h_ref[1, :, cols] = hu.astype(BF16)
            cg = _conv3(hg, dg_ref[:, cols])
            cu = _conv3(hu, du_ref[:, cols])
            c_ref[0, :, cols] = cg.astype(BF16)
            c_ref[1, :, cols] = cu.astype(BF16)
            a_ref[:, cols] = (cg * jax.nn.sigmoid(cg) * cu).astype(BF16)

    return pl.pallas_call(
        body, name=name, grid=(FFN_PAIRS,),
        in_specs=[pl.BlockSpec((s, d), lambda j: (0, 0)),
                  pl.BlockSpec((None, d, cw), lambda j: (j, 0, 0)),
                  pl.BlockSpec((None, d, cw), lambda j: (j + FFN_PAIRS, 0, 0)),
                  pl.BlockSpec((None, 3, cw), lambda j: (j, 0, 0)),
                  pl.BlockSpec((None, 3, cw), lambda j: (j + FFN_PAIRS, 0, 0))],
        out_specs=[pl.BlockSpec((None, 2, s, cw), lambda j: (j, 0, 0, 0)),
                   pl.BlockSpec((None, 2, s, cw), lambda j: (j, 0, 0, 0)),
                   pl.BlockSpec((None, s, cw), lambda j: (j, 0, 0))],
        out_shape=[jax.ShapeDtypeStruct((FFN_PAIRS, 2, s, cw), BF16), jax.ShapeDtypeStruct((FFN_PAIRS, 2, s, cw), BF16),
                   jax.ShapeDtypeStruct((FFN_PAIRS, s, cw), BF16)],
        compiler_params=_params(("parallel",)),
    )(n, wup, wup, wdw, wdw)


def _ffn_mid_bwd(do, wdown, h, c, wdw, name):
    s, d = do.shape
    cw = wdown.shape[1]

    def body(do_ref, wd_ref, h_ref, c_ref, wg_ref, wu_ref, dh_ref, dwg_ref, dwu_ref):
        dov = do_ref[...]
        for c0, size in _lane_chunks(cw):
            cols = slice(c0, c0 + size)
            da = _dot(dov, wd_ref[cols, :], NT)
            hg = h_ref[0, :, cols].astype(F32)
            hu = h_ref[1, :, cols].astype(F32)
            wg, wu = wg_ref[:, cols], wu_ref[:, cols]
            cg = c_ref[0, :, cols].astype(F32)
            cu = c_ref[1, :, cols].astype(F32)
            sg = jax.nn.sigmoid(cg)
            dcu = da * (cg * sg)
            dcg = da * cu * (sg * (1.0 + cg * (1.0 - sg)))
            dh_ref[0, :, cols] = _conv3_bwd(dcg, hg, wg, dwg_ref, cols).astype(BF16)
            dh_ref[1, :, cols] = _conv3_bwd(dcu, hu, wu, dwu_ref, cols).astype(BF16)

    vec = jax.ShapeDtypeStruct((FFN_PAIRS, 3, cw), F32)
    return pl.pallas_call(
        body, name=name, grid=(FFN_PAIRS,),
        in_specs=[pl.BlockSpec((s, d), lambda j: (0, 0)), pl.BlockSpec((None, cw, d), lambda j: (j, 0, 0)),
                  pl.BlockSpec((None, 2, s, cw), lambda j: (j, 0, 0, 0)),
                  pl.BlockSpec((None, 2, s, cw), lambda j: (j, 0, 0, 0)),
                  pl.BlockSpec((None, 3, cw), lambda j: (j, 0, 0)),
                  pl.BlockSpec((None, 3, cw), lambda j: (j + FFN_PAIRS, 0, 0))],
        out_specs=[pl.BlockSpec((None, 2, s, cw), lambda j: (j, 0, 0, 0)),
                   pl.BlockSpec((None, 3, cw), lambda j: (j, 0, 0)), pl.BlockSpec((None, 3, cw), lambda j: (j, 0, 0))],
        out_shape=[jax.ShapeDtypeStruct((FFN_PAIRS, 2, s, cw), BF16), vec, vec],
        compiler_params=_params(("parallel",)),
    )(do, wdown, h, c, wdw, wdw)


def _ffn_dwup(n, dh, name):
    s, d = n.shape
    cw = dh.shape[-1]

    def body(n_ref, dh_ref, o_ref):
        o_ref[...] = _dot(n_ref[...], dh_ref[...], TN).astype(BF16)

    return pl.pallas_call(
        body, name=name, grid=(N_DEV,),
        in_specs=[pl.BlockSpec((s, d), lambda k: (0, 0)),
                  pl.BlockSpec((None, None, s, cw), lambda k: (k % FFN_PAIRS, k // FFN_PAIRS, 0, 0))],
        out_specs=pl.BlockSpec((None, d, cw), lambda k: (k, 0, 0)),
        out_shape=jax.ShapeDtypeStruct((N_DEV, d, cw), BF16),
        compiler_params=_params(("parallel",)),
    )(n, dh)


def _ffn_dn(dh, wup, name, dep=None):
    s, cw = dh.shape[-2:]
    d = wup.shape[1]
    tm = _pick(s, MATMUL_TILES)
    deps = [] if dep is None else [dep]

    def body(dh_ref, w_ref, *rest):
        o_ref, acc_ref = rest[-2:]
        k = pl.program_id(1)

        @pl.when(k == 0)
        def _():
            acc_ref[...] = jnp.zeros_like(acc_ref)

        acc_ref[...] += _dot(dh_ref[...], w_ref[...], NT)

        @pl.when(k == N_DEV - 1)
        def _():
            o_ref[...] = acc_ref[...]

    return pl.pallas_call(
        body, name=name, grid=(s // tm, N_DEV),
        in_specs=[pl.BlockSpec((None, None, tm, cw), lambda i, k: (k % FFN_PAIRS, k // FFN_PAIRS, i, 0)),
                  pl.BlockSpec((None, d, cw), lambda i, k: (k, 0, 0))] + [_ANY] * len(deps),
        out_specs=pl.BlockSpec((tm, d), lambda i, k: (i, 0)),
        out_shape=jax.ShapeDtypeStruct((s, d), F32),
        scratch_shapes=[pltpu.VMEM((tm, d), F32)],
        compiler_params=_params(("parallel", "arbitrary")),
    )(dh, wup, *deps)


def _sconv_fwd(n, win, wdw, name):
    s, d = n.shape
    tn = COL_TILE
    nj = d // tn

    def body(n_ref, wb_ref, wc_ref, wh_ref, dw_ref, z_ref, y_ref):
        x = n_ref[...]
        zb = _dot(x, wb_ref[...])
        zc = _dot(x, wc_ref[...])
        zh = _dot(x, wh_ref[...])
        z_ref[0] = zb.astype(BF16)
        z_ref[1] = zc.astype(BF16)
        z_ref[2] = zh.astype(BF16)
        y_ref[...] = (zb * _conv3(zc * zh, dw_ref[...])).astype(BF16)

    return pl.pallas_call(
        body, name=name, grid=(nj,),
        in_specs=[pl.BlockSpec((s, d), lambda j: (0, 0)),
                  pl.BlockSpec((d, tn), lambda j: (0, j)), pl.BlockSpec((d, tn), lambda j: (0, j + nj)),
                  pl.BlockSpec((d, tn), lambda j: (0, j + 2 * nj)), pl.BlockSpec((3, tn), lambda j: (0, j))],
        out_specs=[pl.BlockSpec((3, s, tn), lambda j: (0, 0, j)), pl.BlockSpec((s, tn), lambda j: (0, j))],
        out_shape=[jax.ShapeDtypeStruct((3, s, d), BF16), jax.ShapeDtypeStruct((s, d), BF16)],
        compiler_params=_params(("parallel",)),
    )(n, win, win, win, wdw)


def _sconv_mid_bwd(dm, wout, z, wdw, name):
    s, d = dm.shape
    tn = COL_TILE
    nj = d // tn

    def body(dm_ref, wo_ref, z_ref, w_ref, dz_ref, dw_ref):
        dy = _dot(dm_ref[...], wo_ref[...], NT)
        zb = z_ref[0].astype(F32)
        zc = z_ref[1].astype(F32)
        zh = z_ref[2].astype(F32)
        w = w_ref[...]
        p = zc * zh
        cp = _conv3(p, w)
        dz_ref[0] = (dy * cp).astype(BF16)
        dcp = dy * zb
        dp = _conv3_bwd(dcp, p, w, dw_ref)
        dz_ref[1] = (dp * zh).astype(BF16)
        dz_ref[2] = (dp * zc).astype(BF16)

    return pl.pallas_call(
        body, name=name, grid=(nj,),
        in_specs=[pl.BlockSpec((s, d), lambda j: (0, 0)), pl.BlockSpec((tn, d), lambda j: (j, 0)),
                  pl.BlockSpec((3, s, tn), lambda j: (0, 0, j)), pl.BlockSpec((3, tn), lambda j: (0, j))],
        out_specs=[pl.BlockSpec((3, s, tn), lambda j: (0, 0, j)), pl.BlockSpec((3, tn), lambda j: (0, j))],
        out_shape=[jax.ShapeDtypeStruct((3, s, d), BF16), jax.ShapeDtypeStruct((3, d), F32)],
        compiler_params=_params(("parallel",)),
    )(dm, wout, z, wdw)


def _pool_select(g, c2, c4, c8, c16):
    return jnp.where(g == 0, c2, jnp.where(g == 1, c4, jnp.where(g == 2, c8, c16)))


def _pool_inv_count(g, shape):
    pos = lax.broadcasted_iota(jnp.int32, shape, 0).astype(F32) + 1.0
    win = (2 << g).astype(F32)
    return jnp.minimum(pos, win)


def _pool_fwd(n, win, wgrp, scale, name):
    s, d = n.shape
    tn = POOL_GROUP_DIM

    def body(n_ref, wi_ref, wg_ref, sc_ref, p_ref, y_ref):
        g = pl.program_id(0)
        u = _dot(n_ref[...], wi_ref[...])
        s2 = u + _shift_down(u, 1)
        s4 = s2 + _shift_down(s2, 2)
        s8 = s4 + _shift_down(s4, 4)
        s16 = s8 + _shift_down(s8, 8)
        tot = _pool_select(g, s2, s4, s8, s16)
        p = (tot / _pool_inv_count(g, u.shape) - u).astype(BF16)
        p_ref[...] = p
        y_ref[...] = (_dot(p, wg_ref[...]) * sc_ref[...]).astype(BF16)

    return pl.pallas_call(
        body, name=name, grid=(d // tn,),
        in_specs=[pl.BlockSpec((s, d), lambda g: (0, 0)), pl.BlockSpec((d, tn), lambda g: (0, g)),
                  pl.BlockSpec((None, tn, tn), lambda g: (g, 0, 0)), pl.BlockSpec((1, tn), lambda g: (0, g))],
        out_specs=[pl.BlockSpec((s, tn), lambda g: (0, g)), pl.BlockSpec((s, tn), lambda g: (0, g))],
        out_shape=[jax.ShapeDtypeStruct((s, d), BF16), jax.ShapeDtypeStruct((s, d), BF16)],
        compiler_params=_params(("parallel",)),
    )(n, win, wgrp, scale)


def _pool_mid_bwd(dm, wout, p, wgrp, scale, name):
    s, d = dm.shape
    tn = POOL_GROUP_DIM

    def body(dm_ref, wo_ref, p_ref, wg_ref, sc_ref, du_ref, dwg_ref, dsc_ref):
        g = pl.program_id(0)
        dy = _dot(dm_ref[...], wo_ref[...], NT)
        pv = p_ref[...]
        wg = wg_ref[...]
        ypre = _dot(pv, wg)
        dsc_ref[...] = jnp.sum(dy * ypre, axis=0, keepdims=True)
        dypre = (dy * sc_ref[...]).astype(BF16)
        dwg_ref[...] = _dot(pv, dypre, TN)
        dp = _dot(dypre, wg, NT)
        e = dp / _pool_inv_count(g, dp.shape)
        f2 = e + _shift_up(e, 1)
        f4 = f2 + _shift_up(f2, 2)
        f8 = f4 + _shift_up(f4, 4)
        f16 = f8 + _shift_up(f8, 8)
        du_ref[...] = (_pool_select(g, f2, f4, f8, f16) - dp).astype(BF16)

    return pl.pallas_call(
        body, name=name, grid=(d // tn,),
        in_specs=[pl.BlockSpec((s, d), lambda g: (0, 0)), pl.BlockSpec((tn, d), lambda g: (g, 0)),
                  pl.BlockSpec((s, tn), lambda g: (0, g)), pl.BlockSpec((None, tn, tn), lambda g: (g, 0, 0)),
                  pl.BlockSpec((1, tn), lambda g: (0, g))],
        out_specs=[pl.BlockSpec((s, tn), lambda g: (0, g)), pl.BlockSpec((None, tn, tn), lambda g: (g, 0, 0)),
                   pl.BlockSpec((1, tn), lambda g: (0, g))],
        out_shape=[jax.ShapeDtypeStruct((s, d), BF16), jax.ShapeDtypeStruct((4, tn, tn), F32),
                   jax.ShapeDtypeStruct((1, d), F32)],
        compiler_params=_params(("parallel",)),
    )(dm, wout, p, wgrp, scale)


PANEL = LANES
ATTN_EXT = ATTN_WIDTH + PANEL
DVEC_LANE = HEADS


def _alibi_slopes(g, dil):
    all_slopes = 2.0 ** (-8.0 * np.arange(1, N_HEADS_A + 1) / N_HEADS_A)
    return [float(np.float32(sl) * np.float32(dil)) for sl in all_slopes[g * HEADS:(g + 1) * HEADS]]


def _residue_order(a, dil, name):
    s, w = a.shape
    per = ROW_TILE // dil
    panels = w // PANEL

    def body(a_ref, o_ref, *tiles):
        for c in range(panels):
            cols = slice(c * PANEL, (c + 1) * PANEL)
            tiles[c][...] = a_ref[:, cols].astype(F32)
            for r in range(dil):
                o_ref[r, :, cols] = tiles[c][pl.ds(r, per, stride=dil), :].astype(o_ref.dtype)

    out = pl.pallas_call(
        body, name=name, grid=(s // ROW_TILE,),
        in_specs=[pl.BlockSpec((ROW_TILE, w), lambda i: (i, 0))],
        out_specs=pl.BlockSpec((dil, per, w), lambda i: (0, i, 0)),
        out_shape=jax.ShapeDtypeStruct((dil, s // dil, w), a.dtype),
        scratch_shapes=[pltpu.VMEM((ROW_TILE, PANEL), F32)] * panels,
        compiler_params=_params(("parallel",)),
    )(a)
    return out.reshape(s, w)


def _token_order(a, dil, acc, name):
    s, w = a.shape
    per = ROW_TILE // dil
    panels = w // PANEL
    has_acc = acc is not None

    def body(*refs):
        a_ref = refs[0]
        o_ref = refs[2] if has_acc else refs[1]
        tiles = refs[3:] if has_acc else refs[2:]
        for c in range(panels):
            cols = slice(c * PANEL, (c + 1) * PANEL)
            for r in range(dil):
                tiles[c][pl.ds(r, per, stride=dil), :] = a_ref[r, :, cols]
            v = tiles[c][...]
            if has_acc:
                v = v + refs[1][:, cols]
            o_ref[:, cols] = v

    row = pl.BlockSpec((ROW_TILE, w), lambda i: (i, 0))
    return pl.pallas_call(
        body, name=name, grid=(s // ROW_TILE,),
        in_specs=[pl.BlockSpec((dil, per, w), lambda i: (0, i, 0))] + ([row] if has_acc else []),
        out_specs=row, out_shape=jax.ShapeDtypeStruct((s, w), F32),
        scratch_shapes=[pltpu.VMEM((ROW_TILE, PANEL), F32)] * panels,
        compiler_params=_params(("parallel",)),
    )(*([a.reshape(dil, s // dil, w)] + ([acc] if has_acc else [])))


def _qkv_proj(n, wqkv, g, name):
    s, d = n.shape
    tm = _pick(s, MATMUL_TILES)

    def body(a_ref, b_ref, o_ref):
        o_ref[...] = _dot(a_ref[...], b_ref[...]).astype(BF16)

    return pl.pallas_call(
        body, name=name, grid=(s // tm, 3),
        in_specs=[pl.BlockSpec((tm, d), lambda i, t: (i, 0)),
                  pl.BlockSpec((d, ATTN_WIDTH), lambda i, t: (0, 3 * g + t))],
        out_specs=pl.BlockSpec((None, tm, ATTN_WIDTH), lambda i, t: (t, i, 0)),
        out_shape=jax.ShapeDtypeStruct((3, s, ATTN_WIDTH), BF16),
        compiler_params=_params(("parallel", "parallel")),
    )(n, wqkv)


def _attn_window(n, ln):
    if ln == BLOCK:
        return 0, BLOCK
    return pl.multiple_of(jnp.maximum(n - 1, 0) * BLOCK, BLOCK), 2 * BLOCK


def _attn_mask(n, k0, kw):
    qpos = n * BLOCK + lax.broadcasted_iota(jnp.int32, (BLOCK, kw), 0)
    kpos = k0 + lax.broadcasted_iota(jnp.int32, (BLOCK, kw), 1)
    dist = qpos - kpos
    return dist.astype(F32), (dist >= 0) & (dist <= BLOCK)


def _attn_scores(q, keys, slope, dist, valid):
    s = _dot(q, keys, NT) * (HEAD_DIM ** -0.5) - slope * dist
    return jnp.where(valid, s, NEG_INF)


ATTN_STEP_BLOCKS = 1
ATTN_BWD_STEP_BLOCKS = 4


def _attn_block(gb, ln):
    nb = ln // BLOCK
    n, base = (0, gb * ln) if nb == 1 else (gb % nb, (gb // nb) * ln)
    k0, kw = _attn_window(n, ln)
    cur = pl.ds(pl.multiple_of(gb * BLOCK, BLOCK), BLOCK)
    win = pl.ds(pl.multiple_of(base + k0, BLOCK), kw)
    return cur, win, n, k0, kw


def _attn_fwd(qkv, g, name):
    _, s, w = qkv.shape
    dil = DILATED_CFG[g][1]
    ln = s // dil
    slopes = _alibi_slopes(g, dil)
    rows = ATTN_STEP_BLOCKS * BLOCK

    def body(qkv_ref, o_ref):
        o_ref[:, w:] = jnp.zeros((rows, PANEL), F32)
        for b in range(ATTN_STEP_BLOCKS):
            cur, win, n, k0, kw = _attn_block(pl.program_id(0) * ATTN_STEP_BLOCKS + b, ln)
            dist, valid = _attn_mask(n, k0, kw)
            out = slice(b * BLOCK, (b + 1) * BLOCK)
            for h in range(HEADS):
                cols = slice(h * HEAD_DIM, (h + 1) * HEAD_DIM)
                sc = _attn_scores(qkv_ref[0, cur, cols], qkv_ref[1, win, cols], slopes[h], dist, valid)
                m = jnp.max(sc, axis=-1, keepdims=True)
                p = jnp.exp(sc - m)
                den = jnp.sum(p, axis=-1, keepdims=True)
                o_ref[out, cols] = _dot(p.astype(BF16), qkv_ref[2, win, cols]) / den
                o_ref[out, w + h:w + h + 1] = m + jnp.log(den)

    return pl.pallas_call(
        body, name=name, grid=(s // rows,),
        in_specs=[pl.BlockSpec((3, s, w), lambda i: (0, 0, 0))],
        out_specs=pl.BlockSpec((rows, ATTN_EXT), lambda i: (i, 0)),
        out_shape=jax.ShapeDtypeStruct((s, ATTN_EXT), F32),
        compiler_params=_params(("parallel",)),
    )(qkv)


def _attn_bwd(qkv, dext, g, name, dep=None):
    _, s, w = qkv.shape
    dil = DILATED_CFG[g][1]
    ln = s // dil
    slopes = _alibi_slopes(g, dil)
    scale = HEAD_DIM ** -0.5
    rows = ATTN_BWD_STEP_BLOCKS * BLOCK
    steps = s // rows
    deps = [] if dep is None else [dep]

    def body(qkv_ref, de_ref, *rest):
        d_ref, dk_ref, dv_ref = rest[-3:]

        @pl.when(pl.program_id(0) == 0)
        def _():
            dk_ref[...] = jnp.zeros_like(dk_ref)
            dv_ref[...] = jnp.zeros_like(dv_ref)

        for b in range(ATTN_BWD_STEP_BLOCKS):
            cur, win, n, k0, kw = _attn_block(pl.program_id(0) * ATTN_BWD_STEP_BLOCKS + b, ln)
            dist, valid = _attn_mask(n, k0, kw)
            blk = slice(b * BLOCK, (b + 1) * BLOCK)
            for h in range(HEADS):
                cols = slice(h * HEAD_DIM, (h + 1) * HEAD_DIM)
                q, keys = qkv_ref[0, cur, cols], qkv_ref[1, win, cols]
                dob = de_ref[blk, cols].astype(BF16)
                p = jnp.exp(_attn_scores(q, keys, slopes[h], dist, valid) - de_ref[blk, w + h:w + h + 1])
                dd = de_ref[blk, w + DVEC_LANE + h:w + DVEC_LANE + h + 1]
                ds = (p * (_dot(dob, qkv_ref[2, win, cols], NT) - dd)).astype(BF16)
                d_ref[0, cur, cols] = (scale * _dot(ds, keys)).astype(BF16)
                dv_ref[win, cols] += _dot(p.astype(BF16), dob, TN)
                dk_ref[win, cols] += scale * _dot(ds, q, TN)

        @pl.when(pl.program_id(0) == steps - 1)
        def _():
            d_ref[1] = dk_ref[...].astype(BF16)
            d_ref[2] = dv_ref[...].astype(BF16)

    whole = pl.BlockSpec((3, s, w), lambda i: (0, 0, 0))
    return pl.pallas_call(
        body, name=name, grid=(steps,),
        in_specs=[whole, pl.BlockSpec((rows, ATTN_EXT), lambda i: (i, 0))] + [_ANY] * len(deps),
        out_specs=whole, out_shape=jax.ShapeDtypeStruct((3, s, w), BF16),
        scratch_shapes=[pltpu.VMEM((s, w), F32), pltpu.VMEM((s, w), F32)],
        compiler_params=_params(("arbitrary",)),
    )(qkv, dext, *deps)


def _attn_merge(e0, e1, e2, name):
    s = e0.shape[0]
    w = ATTN_WIDTH

    def body(e0_ref, e1_ref, e2_ref, m_ref, mb_ref, lse_ref):
        refs = (e0_ref, e1_ref, e2_ref)
        l = [r[:, w:w + HEADS] for r in refs]
        mx = jnp.maximum(jnp.maximum(l[0], l[1]), l[2])
        e = [jnp.exp(v - mx) for v in l]
        z = e[0] + e[1] + e[2]
        lse_ref[...] = mx + jnp.log(z)
        wts = [v / z for v in e]
        for h in range(HEADS):
            cols = slice(h * HEAD_DIM, (h + 1) * HEAD_DIM)
            acc = wts[0][:, h:h + 1] * refs[0][:, cols]
            for g in range(1, N_GROUPS_A):
                acc = acc + wts[g][:, h:h + 1] * refs[g][:, cols]
            m_ref[:, cols] = acc
            mb_ref[:, cols] = acc.astype(BF16)

    ext = pl.BlockSpec((ROW_TILE, ATTN_EXT), lambda i: (i, 0))
    row = pl.BlockSpec((ROW_TILE, w), lambda i: (i, 0))
    return pl.pallas_call(
        body, name=name, grid=(s // ROW_TILE,),
        in_specs=[ext, ext, ext],
        out_specs=[row, row, pl.BlockSpec((ROW_TILE, HEADS), lambda i: (i, 0))],
        out_shape=[jax.ShapeDtypeStruct((s, w), F32), jax.ShapeDtypeStruct((s, w), BF16),
                   jax.ShapeDtypeStruct((s, HEADS), F32)],
        compiler_params=_params(("parallel",)),
    )(e0, e1, e2)


def _attn_dvec(dmerged, merged, lse_all, name, dep=None):
    s, w = merged.shape
    deps = [] if dep is None else [dep]

    def body(dm_ref, m_ref, lse_ref, *rest):
        de_ref = rest[-1]
        dmv = dm_ref[...]
        de_ref[:, :w] = dmv
        de_ref[:, w:] = jnp.zeros((ROW_TILE, PANEL), F32)
        de_ref[:, w:w + HEADS] = lse_ref[...]
        prod = dmv * m_ref[...]
        for h in range(HEADS):
            lane = w + DVEC_LANE + h
            de_ref[:, lane:lane + 1] = jnp.sum(prod[:, h * HEAD_DIM:(h + 1) * HEAD_DIM], axis=-1, keepdims=True)

    row = pl.BlockSpec((ROW_TILE, w), lambda i: (i, 0))
    return pl.pallas_call(
        body, name=name, grid=(s // ROW_TILE,),
        in_specs=[row, row, pl.BlockSpec((ROW_TILE, HEADS), lambda i: (i, 0))] + [_ANY] * len(deps),
        out_specs=pl.BlockSpec((ROW_TILE, ATTN_EXT), lambda i: (i, 0)),
        out_shape=jax.ShapeDtypeStruct((s, ATTN_EXT), F32),
        compiler_params=_params(("parallel",)),
    )(dmerged, merged, lse_all, *deps)


def _attention_fwd(n, wqkv, wo, tag):
    ns, qkvs, exts = [], [], []
    for g, (_, dil) in enumerate(DILATED_CFG):
        ng = n if dil == 1 else _residue_order(n, dil, f"{tag}_order_g{g}")
        qkv = _qkv_proj(ng, wqkv, g, f"{tag}_qkv_g{g}")
        ext = _attn_fwd(qkv, g, f"{tag}_fwd_g{g}")
        ns.append(ng)
        qkvs.append(qkv)
        exts.append(ext if dil == 1 else _token_order(ext, dil, None, f"{tag}_unorder_g{g}"))
    merged, merged_bf, lse_all = _attn_merge(*exts, f"{tag}_merge")
    m = _matmul(merged_bf, wo, "nn", F32, f"{tag}_wo")
    return m, (ns, qkvs, merged, merged_bf, lse_all)


def _attention_bwd(dm, wqkv, wo, saved, tag, dep=None, hook=None):
    ns, qkvs, merged, merged_bf, lse_all = saved
    d_wo = _matmul(merged_bf, dm, "tn", BF16, f"{tag}_dwo")
    dmerged = _matmul(dm, wo, "nt", F32, f"{tag}_dmerged")
    dext = _attn_dvec(dmerged, merged, lse_all, f"{tag}_dvec", dep)
    width = 3 * ATTN_WIDTH
    d_wqkv, dn, dep = [], None, None
    for g, (_, dil) in enumerate(DILATED_CFG):
        dext_g = dext if dil == 1 else _residue_order(dext, dil, f"{tag}_dorder_g{g}")
        dqkv = _attn_bwd(qkvs[g], dext_g, g, f"{tag}_bwd_g{g}", dep)
        dep = hook(g, dqkv) if hook is not None and g + 1 < N_GROUPS_A else None
        d_wqkv.append(_matmul(ns[g], dqkv, "tn", BF16, f"{tag}_dwqkv_g{g}", b_parts=3))
        dn_g = _matmul(dqkv, wqkv[:, g * width:(g + 1) * width], "nt", F32, f"{tag}_dn_g{g}", a_parts=3)
        dn = dn_g if dil == 1 else _token_order(dn_g, dil, dn, f"{tag}_dn_sum_g{g}")
    return dn, jnp.concatenate(d_wqkv, axis=1), d_wo


def _layer_matrices(i):
    mixer = (("attn_w_qkv", "attn_w_o"), ("conv_w_in", "conv_w_out"), ("pool_w_in", "pool_w_grp", "pool_w_out"))[i % 3]
    return [(k, i // 3) for k in mixer] + [("ffn_w_up", i), ("ffn_w_down", i)]


def _local_step(x, tgt, vec, weights, sink):
    ng = vec["norm_g"]

    def gain(i, j, token=None):
        g = ng[i, j][None, :]
        return g if token is None else g + token

    saved = []
    n = _rms_fwd(x, gain(0, 0), None, BF16, "norm_first")
    for i in range(DEPTH):
        wl = weights.layer(i)
        t0 = weights.hook(i, 0, n)
        kind, idx = i % 3, i // 3
        if kind == 0:
            m, ms = _attention_fwd(n, wl["attn_w_qkv"], wl["attn_w_o"], "attn")
        elif kind == 1:
            taps = vec["conv_w_dw"][idx] if t0 is None else vec["conv_w_dw"][idx] + t0
            z, y = _sconv_fwd(n, wl["conv_w_in"], taps, "sconv_fwd")
            m = _matmul(y, wl["conv_w_out"], "nn", F32, "sconv_out")
            ms = (z, y)
        else:
            scale = vec["pool_scale"][idx][None, :] if t0 is None else vec["pool_scale"][idx][None, :] + t0
            p, y = _pool_fwd(n, wl["pool_w_in"], wl["pool_w_grp"], scale, "pool_fwd")
            m = _matmul(y, wl["pool_w_out"], "nn", F32, "pool_out")
            ms = (p, y)
        t1 = weights.hook(i, 1, m)
        x1, n2 = _rms_res_pre(m, gain(i, 1, t0), x, gain(i, 2, t1), "norm_res_pre")
        h, c, a = _ffn_up(n2, wl["ffn_w_up"], vec["ffn_w_dw"][i], "ffn_up")
        t2 = weights.hook(i, 2, a)
        f = _matmul(a, wl["ffn_w_down"].reshape(D_FF, D_MODEL), "nn", F32, "ffn_down", a_parts=FFN_PAIRS)
        saved.append((x, n, m, ms, x1, n2, h, a, f, wl, c))
        if i + 1 < DEPTH:
            x, n = _rms_res_pre(f, gain(i, 3, t2), x1, gain(i + 1, 0), "norm_res_pre")
        else:
            x = _rms_fwd(f, gain(i, 3), x1, F32, "norm_res")
        weights.hook(i, 3, x)

    loss, dx = _loss_head(x, tgt, "loss_head")

    g_norm = [[None] * 4 for _ in range(DEPTH)]
    g_taps, g_scale, g_ffn_dw = [], [], [None] * DEPTH
    df, g_norm[DEPTH - 1][3] = _rms_bwd(saved[-1][8], gain(DEPTH - 1, 3), dx, None, BF16, "norm_bwd_sub")
    t0 = None
    for i in reversed(range(DEPTH)):
        xin, n, m, ms, x1, n2, h, a, f, wl, c = saved[i]
        kind, idx = i % 3, i // 3
        gl = {}
        d_wdown = _matmul(a, df, "tn", BF16, "ffn_dwdown", a_parts=FFN_PAIRS)
        gl["ffn_w_down"] = d_wdown.reshape(N_DEV, D_FF // N_DEV, D_MODEL)
        ffn_taps = vec["ffn_w_dw"][i] if t0 is None else vec["ffn_w_dw"][i] + t0
        dh, dwg, dwu = _ffn_mid_bwd(df, wl["ffn_w_down"].reshape(FFN_PAIRS, -1, D_MODEL), h, c, ffn_taps, "ffn_mid_bwd")
        g_ffn_dw[i] = jnp.concatenate([dwg, dwu], axis=0)
        t1 = sink.hook(i, 1, dh)
        gl["ffn_w_up"] = _ffn_dwup(n2, dh, "ffn_dwup")
        tf = sink.ffn_done(i, gl)
        dn2 = _ffn_dn(dh, wl["ffn_w_up"], "ffn_dn", t1)
        dx1, dm, g_norm[i][2], g_norm[i][1] = _rms_bwd_pair(x1, gain(i, 2, tf), dn2, dx, m, gain(i, 1), "norm_bwd_pair")
        t2 = sink.hook(i, 2, dm)
        if kind == 0:
            dn, gl["attn_w_qkv"], gl["attn_w_o"] = _attention_bwd(
                dm, wl["attn_w_qkv"], wl["attn_w_o"], ms, "attn", t2, lambda g, after, i=i: sink.hook(i, ("a", "b")[g], after))
        elif kind == 1:
            z, y = ms
            gl["conv_w_out"] = _matmul(y, dm, "tn", BF16, "sconv_dwout")
            taps = vec["conv_w_dw"][idx] if t2 is None else vec["conv_w_dw"][idx] + t2
            dz, ddw = _sconv_mid_bwd(dm, wl["conv_w_out"], z, taps, "sconv_mid_bwd")
            g_taps.append(ddw)
            gl["conv_w_in"] = _matmul(n, dz, "tn", BF16, "sconv_dwin", b_parts=3)
            dn = _matmul(dz, wl["conv_w_in"], "nt", F32, "sconv_dn", a_parts=3)
        else:
            p, y = ms
            gl["pool_w_out"] = _matmul(y, dm, "tn", BF16, "pool_dwout")
            scale = vec["pool_scale"][idx][None, :] if t2 is None else vec["pool_scale"][idx][None, :] + t2
            du, gl["pool_w_grp"], dscale = _pool_mid_bwd(dm, wl["pool_w_out"], p, wl["pool_w_grp"], scale, "pool_mid_bwd")
            g_scale.append(dscale[0])
            gl["pool_w_in"] = _matmul(n, du, "tn", BF16, "pool_dwin")
            dn = _matmul(du, wl["pool_w_in"], "nt", F32, "pool_dn")
        sink.hook(i, 3, dn)
        if i > 0:
            dx, df, g_norm[i][0], g_norm[i - 1][3] = _rms_bwd_pair(xin, gain(i, 0, t2), dn, dx1, saved[i - 1][8],
                                                                   gain(i - 1, 3), "norm_bwd_pair")
        else:
            dx, g_norm[0][0] = _rms_bwd(xin, gain(0, 0), dn, dx1, F32, "norm_bwd_res")
        t0 = sink.layer_done(i, gl)

    vec_grads = {"norm_g": jnp.stack([jnp.concatenate(row, axis=0) for row in g_norm]), "conv_w_dw": jnp.stack(g_taps),
                 "pool_scale": jnp.stack(g_scale), "ffn_w_dw": g_ffn_dw}
    return loss, dx, vec_grads


_AXES = ("x", "y", "c")
ROUTE_A = ("y", "x", "c")
ROUTE_B = ("x", "y", "c")
def _dev_index(pos):
    return 4 * pos["x"] + 2 * pos["y"] + pos["c"]


_HBM = pl.BlockSpec(memory_space=pltpu.HBM)
_SEM = pl.BlockSpec(memory_space=pltpu.SEMAPHORE)
_ANY = pl.BlockSpec(memory_space=pl.ANY)
_EFFECT = pltpu.SideEffectType.DATAFLOW_SIDE_EFFECTING


TOKEN_SHAPE = (1, D_MODEL)


def _copies_start(describe, arrays, n_copies, name, after, token_shape=TOKEN_SHAPE):
    n = len(arrays)
    deps = [] if after is None else [after]

    def body(*refs):
        send_sems, recv_sems = refs[n + len(deps)], refs[n + len(deps) + 1]
        for c in describe(refs[:n], send_sems, recv_sems):
            c.start()
        refs[-1][...] = jnp.zeros_like(refs[-1])

    outs = pl.pallas_call(
        body, name=f"{name}_start",
        out_shape=(pltpu.SemaphoreType.DMA((n_copies,)), pltpu.SemaphoreType.DMA((n_copies,)),
                   *[pltpu.HBM(a.shape, a.dtype) for a in arrays], jax.ShapeDtypeStruct(token_shape, F32)),
        in_specs=[_HBM] * n + [_ANY] * len(deps),
        out_specs=(_SEM, _SEM, *([_HBM] * n), pl.BlockSpec(memory_space=pltpu.VMEM)),
        input_output_aliases={i: 2 + i for i in range(n)},
        compiler_params=pltpu.CompilerParams(has_side_effects=_EFFECT),
    )(*[pltpu.with_memory_space_constraint(a, pltpu.HBM) for a in arrays], *deps)
    return (outs[0], outs[1], list(outs[2:2 + n])), outs[-1]


def _copies_wait(describe, handle, name, after):
    send_sems, recv_sems, arrays = handle
    n = len(arrays)
    deps = [] if after is None else [after]

    def body(*refs):
        for c in describe(refs[:n], refs[n], refs[n + 1]):
            c.wait_send()
            c.wait_recv()

    outs = pl.pallas_call(
        body, name=f"{name}_wait",
        out_shape=tuple(pltpu.HBM(a.shape, a.dtype) for a in arrays),
        in_specs=[_HBM] * n + [_SEM, _SEM] + [_ANY] * len(deps), out_specs=tuple([_HBM] * n),
        input_output_aliases={i: i for i in range(n)},
        compiler_params=pltpu.CompilerParams(has_side_effects=_EFFECT),
    )(*arrays, send_sems, recv_sems, *deps)
    return list(outs)


GATHER_STAGE_COPIES = (3, 3, 1)


def _gather_copies(stage, routes):
    n = len(routes)

    def describe(refs, send_sems, recv_sems):
        pos = {a: lax.axis_index(a) for a in _AXES}

        def flipped(axes):
            return {a: 1 - pos[a] if a in axes else pos[a] for a in _AXES}

        copies = []
        for i, (a1, a2, a3) in enumerate(routes):
            land = refs[n + i] if stage == 1 else refs[i]
            p1, p2, p12, p3 = flipped((a1,)), flipped((a2,)), flipped((a1, a2)), flipped((a3,))
            plan = {1: [(None, p1), (None, p2), (None, p3)], 2: [(p1, p2), (p1, p3), (p2, p3)], 3: [(p12, p3)]}[stage]
            for holder, to in plan:
                slot = land.at[_dev_index(pos if holder is None else holder)]
                k = len(copies)
                copies.append(pltpu.make_async_remote_copy(
                    src_ref=refs[i] if holder is None else slot, dst_ref=slot,
                    send_sem=send_sems.at[k], recv_sem=recv_sems.at[k],
                    device_id=tuple(to[a] for a in _AXES), device_id_type=pl.DeviceIdType.MESH))
        return copies

    return describe


def _gather_begin(shards, routes, name, after):
    n = len(shards)
    lands = [lax.empty((N_DEV,) + a.shape, a.dtype) for a in shards]
    handle, token = _copies_start(_gather_copies(1, routes), list(shards) + lands, GATHER_STAGE_COPIES[0] * n,
                                  f"{name}_1", after)
    return {"stage": 1, "handle": handle, "routes": routes, "name": name, "n": n}, token


def _gather_next(state, after):
    stage, routes, name, n = state["stage"], state["routes"], state["name"], state["n"]
    arrays = _copies_wait(_gather_copies(stage, routes), state["handle"], f"{name}_{stage}", after)
    if stage == 1:
        state = dict(state, shards=arrays[:n])
        arrays = arrays[n:]
    if stage == 3:
        me = _dev_index({a: lax.axis_index(a) for a in _AXES})
        return [lax.dynamic_update_index_in_dim(o, s, me, 0) for o, s in zip(arrays, state["shards"])], None
    handle, token = _copies_start(_gather_copies(stage + 1, routes), arrays, GATHER_STAGE_COPIES[stage] * n,
                                  f"{name}_{stage + 1}", None)
    return dict(state, stage=stage + 1, handle=handle), token


ADD_ROW_TILES = (1024, 704, 512, 352, 256, 128, 96, 64, 32, 16)


def _add_half(a, recv, me, out_dtype, name):
    p, q, cols = recv.shape
    tr = _pick(q, ADD_ROW_TILES)

    def body(me_ref, a_ref, b_ref, o_ref):
        o_ref[...] = (a_ref[...].astype(F32) + b_ref[...].astype(F32)).astype(o_ref.dtype)

    return pl.pallas_call(
        body, name=name,
        grid_spec=pltpu.PrefetchScalarGridSpec(
            num_scalar_prefetch=1, grid=(p, q // tr),
            in_specs=[pl.BlockSpec((None, None, tr, cols), lambda j, i, m: (j, m[0], i, 0)),
                      pl.BlockSpec((None, tr, cols), lambda j, i, m: (j, i, 0))],
            out_specs=pl.BlockSpec((None, tr, cols), lambda j, i, m: (j, i, 0))),
        out_shape=jax.ShapeDtypeStruct((p, q, cols), out_dtype),
        compiler_params=_params(("parallel", "parallel")),
    )(me, a, recv)


def _half_copies(axes):
    n = len(axes)

    def describe(refs, send_sems, recv_sems):
        pos = {a: lax.axis_index(a) for a in _AXES}
        copies = []
        for i, axis in enumerate(axes):
            peer = tuple(1 - pos[a] if a == axis else pos[a] for a in _AXES)
            copies.append(pltpu.make_async_remote_copy(
                src_ref=refs[i].at[:, 1 - pos[axis]], dst_ref=refs[n + i], send_sem=send_sems.at[i],
                recv_sem=recv_sems.at[i], device_id=peer, device_id_type=pl.DeviceIdType.MESH))
        return copies

    return describe


def _scatter_begin(slots, routes, tags, name, token_shape=TOKEN_SHAPE):
    shapes = [a.shape[1:] for a in slots]
    rows = [math.prod(s[:-1]) for s in shapes]
    arrays = [a.reshape(4, 2, n, s[-1]) for a, n, s in zip(slots, rows, shapes)]
    return _scatter_start({"stage": 0, "arrays": arrays, "routes": routes, "tags": tags, "name": name,
                           "shapes": shapes, "rows": rows}, token_shape)


def _scatter_start(state, token_shape=TOKEN_SHAPE):
    stage, arrays = state["stage"], state["arrays"]
    axes = [r[2 - stage] for r in state["routes"]]
    lands = [lax.empty((a.shape[0],) + a.shape[2:], a.dtype) for a in arrays]
    handle, token = _copies_start(_half_copies(axes), arrays + lands, len(arrays), f"{state['name']}_{stage + 1}", None,
                                  token_shape)
    return dict(state, handle=handle, axes=axes), token


def _scatter_next(state, after):
    stage, axes, n = state["stage"], state["axes"], len(state["arrays"])
    both = _copies_wait(_half_copies(axes), state["handle"], f"{state['name']}_{stage + 1}", after)
    coord = {a: lax.axis_index(a).astype(jnp.int32).reshape(1) for a in _AXES}
    sums = [_add_half(a, r, coord[ax], F32 if stage == 2 else BF16, f"scatter_add_{stage + 1}_{t}")
            for a, r, ax, t in zip(both[:n], both[n:], axes, state["tags"])]
    if stage == 2:
        return [a.reshape(s) for a, s in zip(sums, state["shapes"])], None
    if stage == 0:
        views = [(1, 2, 2 * r, s[-1]) if route[1] == "x" else (2, 2, r, s[-1])
                 for r, s, route in zip(state["rows"], state["shapes"], state["routes"])]
    else:
        views = [(1, 2, r, s[-1]) for r, s in zip(state["rows"], state["shapes"])]
    return _scatter_start(dict(state, stage=stage + 1, arrays=[a.reshape(v) for a, v in zip(sums, views)]))


_WEIGHTS = {
    "norm_g": ((DEPTH, 4, D_MODEL), 2, True),
    "attn_w_qkv": ((2, D_MODEL, 4608), 2, False),
    "attn_w_o": ((2, ATTN_WIDTH, D_MODEL), 2, False),
    "conv_w_in": ((1, D_MODEL, 3 * D_MODEL), 2, False),
    "conv_w_dw": ((1, 3, D_MODEL), 2, True),
    "conv_w_out": ((1, D_MODEL, D_MODEL), 1, False),
    "pool_w_in": ((1, D_MODEL, D_MODEL), 1, False),
    "pool_w_grp": ((1, 4, POOL_GROUP_DIM, POOL_GROUP_DIM), 2, False),
    "pool_scale": ((1, D_MODEL), 1, True),
    "pool_w_out": ((1, D_MODEL, D_MODEL), 1, False),
    "ffn_w_up": ((DEPTH, D_MODEL, 2 * D_FF), 2, False),
    "ffn_w_dw": ((DEPTH, 3, 2 * D_FF), 2, True),
    "ffn_w_down": ((DEPTH, D_FF, D_MODEL), 1, False),
}
_NAMES = tuple(_WEIGHTS)
_VECTORS = tuple(k for k in _NAMES if _WEIGHTS[k][2])
_MATRICES = tuple(k for k in _NAMES if not _WEIGHTS[k][2])
_FFN = ("ffn_w_up", "ffn_w_down")
_ON_ROUTE_A = ("ffn_w_up", "attn_w_o", "conv_w_out", "pool_w_in")
PACK_ROWS = 16


def _route(name):
    return ROUTE_A if name in _ON_ROUTE_A else ROUTE_B


def _shard_shape(name):
    shape, ax, _ = _WEIGHTS[name]
    return tuple(s // N_DEV if i == ax else s for i, s in enumerate(shape))


def _full_from_slots(slots, name, layers=None):
    shape, ax, _ = _WEIGHTS[name]
    if layers is not None:
        shape = (layers,) + shape[1:]
    return jnp.moveaxis(slots, 0, ax).reshape(shape)


def _slots_from_full(full, name):
    shape, ax, _ = _WEIGHTS[name]
    split = shape[:ax] + (N_DEV, shape[ax] // N_DEV) + shape[ax + 1:]
    return jnp.moveaxis(full.reshape(split), ax, 0)


def _pack_vectors(parts, lead):
    rows = []
    for k in _VECTORS:
        r = parts[k].reshape(lead + (-1, LANES))
        pad = -r.shape[-2] % PACK_ROWS
        rows.append(jnp.pad(r, [(0, 0)] * len(lead) + [(0, pad), (0, 0)]))
    return jnp.concatenate(rows, axis=len(lead))


def _unpack_vectors(buf, lead):
    out, r0 = {}, 0
    for k in _VECTORS:
        shard = _shard_shape(k)
        rows = math.prod(shard) // LANES
        out[k] = buf[..., r0:r0 + rows, :].reshape(lead + shard)
        r0 += rows + (-rows % PACK_ROWS)
    return out


class _LayerWeights:
    def __init__(self, shards):
        self.cast = {k: shards[k].astype(BF16) for k in _MATRICES}
        first, ffn0 = _layer_matrices(0)[:-2], _layer_matrices(0)[-2:]
        state, _ = _gather_begin(self._send(first) + [_pack_vectors(shards, ())],
                                 [_route(k) for k, _ in first] + [ROUTE_B], "gather0", None)
        for _ in range(2):
            state, _ = _gather_next(state, None)
        outs, _ = _gather_next(state, None)
        vec = _unpack_vectors(outs[-1], (N_DEV,))
        self.vec = {k: _full_from_slots(vec[k], k) for k in _VECTORS}
        self.vec["ffn_w_dw"] = [vec["ffn_w_dw"][:, l] for l in range(DEPTH)]
        self.ready = {0: self._unpack(first, outs[:-1])}
        self.chains = {}
        tokens = []
        self._begin("ffn0", ffn0, "gather0f", outs[0], tokens)
        self._begin(1, _layer_matrices(1), "gather1", outs[0], tokens)
        self.vec["norm_g"] = self.vec["norm_g"] + (tokens[0] + tokens[1])

    def _send(self, items):
        return [self.cast[k][j] for k, j in items]

    @staticmethod
    def _unpack(items, outs):
        return {k: o if k in _FFN else _full_from_slots(o[:, None], k, layers=1)[0] for (k, _), o in zip(items, outs)}

    def _begin(self, key, items, name, after, tokens):
        state, token = _gather_begin(self._send(items), [_route(k) for k, _ in items], name, after)
        self.chains[key] = (items, state)
        tokens.append(token)

    def _advance(self, key, after, tokens):
        items, state = self.chains.pop(key)
        state, token = _gather_next(state, after)
        if token is None:
            self.ready.setdefault(0 if key == "ffn0" else key, {}).update(self._unpack(items, state))
        else:
            self.chains[key] = (items, state)
            tokens.append(token)

    def layer(self, i):
        return self.ready[i]

    def hook(self, i, point, after):
        tokens = []
        if i == 0 and point == 0:
            self._advance("ffn0", after, tokens)
        if i == 0 and point == 1:
            self._advance("ffn0", after, tokens)
            self._advance("ffn0", None, tokens)
        if point >= 1 and i + 1 in self.chains:
            self._advance(i + 1, after, tokens)
        if point == 1 and i + 2 < DEPTH:
            self._begin(i + 2, _layer_matrices(i + 2), f"gather{i + 2}", after, tokens)
        return functools.reduce(lambda a, b: a + b, tokens) if tokens else None


def _layer_slots(g, name):
    shape, ax, _ = _WEIGHTS[name]
    shape, ax = shape[1:], ax - 1
    split = shape[:ax] + (N_DEV, shape[ax] // N_DEV) + shape[ax + 1:]
    return jnp.moveaxis(g.reshape(split), ax, 0).astype(BF16)


class _GradSink:
    def __init__(self):
        self.state = None
        self.ffn_state = None
        self.sums = {}
        self.last = None

    def ffn_done(self, i, grads):
        if i != 0:
            return None
        self.ffn_items = _layer_matrices(0)[-2:]
        self.ffn_state, token = _scatter_begin([grads[k] for k, _ in self.ffn_items],
                                               [_route(k) for k, _ in self.ffn_items],
                                               [f"{k}{j}" for k, j in self.ffn_items], "scatter0f")
        return token

    def layer_done(self, i, grads):
        items = _layer_matrices(i)
        if i == 0:
            items = items[:-2]
            self.last = (items, [_layer_slots(grads[k], k) for k, _ in items])
            return None
        slots = [grads[k] if k in _FFN else _layer_slots(grads[k], k) for k, _ in items]
        self.items = items
        self.state, token = _scatter_begin(slots, [_route(k) for k, _ in items], [f"{k}{j}" for k, j in items],
                                           f"scatter{i}", (N_DEV, 3, 2 * D_FF // N_DEV))
        return token

    def hook(self, i, point, after):
        tokens = []
        if self.state is not None and point in (1, 2, 3):
            self.state, token = _scatter_next(self.state, after)
            if point == 3:
                self.sums.update(dict(zip(self.items, self.state)))
                self.state = None
            tokens.append(token)
        if self.ffn_state is not None and point in ("a", "b", 3):
            self.ffn_state, token = _scatter_next(self.ffn_state, after)
            if point == 3:
                self.sums.update(dict(zip(self.ffn_items, self.ffn_state)))
                self.ffn_state = None
            tokens.append(token)
        tokens = [t for t in tokens if t is not None]
        return functools.reduce(lambda a, b: a + b, tokens) if tokens else None


def _adamw(w, g, m, v, name, layer=None, prev=None):
    shape = w.shape
    cols = shape[-1]
    view = shape if len(shape) == 3 else (1, math.prod(shape[:-1]), cols)
    layers, rows, _ = view
    tr = _pick(rows, (512, 256, 128, 64, 32, 16, 8))
    n_prev = 0 if prev is None else 3

    def body(*refs):
        w_ref, g_ref, m_ref, v_ref = refs[n_prev:n_prev + 4]
        d_ref, nm_ref, nv_ref = refs[n_prev + 4:]
        gv = g_ref[...]
        nm = ADAM_B1 * m_ref[...] + (1.0 - ADAM_B1) * gv
        nv = ADAM_B2 * v_ref[...] + (1.0 - ADAM_B2) * jnp.square(gv)
        m_hat = nm / (1.0 - ADAM_B1 ** ADAM_STEP)
        v_hat = nv / (1.0 - ADAM_B2 ** ADAM_STEP)
        d_ref[...] = -ADAM_LR * (m_hat / (jnp.sqrt(v_hat) + ADAM_EPS) + ADAM_WD * w_ref[...])
        nm_ref[...] = nm
        nv_ref[...] = nv

    if layer is None:
        grid = (layers, rows // tr)
        blk = gblk = pl.BlockSpec((None, tr, cols), lambda l, i: (l, i, 0))
        gview = view
    else:
        grid = (rows // tr,)
        blk = pl.BlockSpec((None, tr, cols), lambda i: (layer, i, 0))
        gblk = pl.BlockSpec((tr, cols), lambda i: (i, 0))
        gview = (rows, cols)
    shp = jax.ShapeDtypeStruct(view, F32)
    outs = pl.pallas_call(
        body, name=name, grid=grid, in_specs=[_ANY] * n_prev + [blk, gblk, blk, blk], out_specs=[blk] * 3,
        out_shape=[shp] * 3, input_output_aliases={i: i for i in range(n_prev)},
        compiler_params=_params(("parallel",) * len(grid)),
    )(*([] if prev is None else [p.reshape(view) for p in prev]), w.reshape(view), g.reshape(gview), m.reshape(view),
      v.reshape(view))
    return [o.reshape(shape) for o in outs]


def kernel(x, norm_g, attn_w_qkv, attn_w_o, conv_w_in, conv_w_dw, conv_w_out, pool_w_in, pool_w_grp, pool_scale, pool_w_out, ffn_w_up, ffn_w_dw, ffn_w_down, loss_target, m_norm_g, m_attn_w_qkv, m_attn_w_o, m_conv_w_in, m_conv_w_dw, m_conv_w_out, m_pool_w_in, m_pool_w_grp, m_pool_scale, m_pool_w_out, m_ffn_w_up, m_ffn_w_dw, m_ffn_w_down, v_norm_g, v_attn_w_qkv, v_attn_w_o, v_conv_w_in, v_conv_w_dw, v_conv_w_out, v_pool_w_in, v_pool_w_grp, v_pool_scale, v_pool_w_out, v_ffn_w_up, v_ffn_w_dw, v_ffn_w_down):
    shards = dict(zip(_NAMES, (norm_g, attn_w_qkv, attn_w_o, conv_w_in, conv_w_dw, conv_w_out, pool_w_in,
                               pool_w_grp, pool_scale, pool_w_out, ffn_w_up, ffn_w_dw, ffn_w_down)))
    moms = dict(zip(_NAMES, (m_norm_g, m_attn_w_qkv, m_attn_w_o, m_conv_w_in, m_conv_w_dw, m_conv_w_out,
                             m_pool_w_in, m_pool_w_grp, m_pool_scale, m_pool_w_out, m_ffn_w_up, m_ffn_w_dw,
                             m_ffn_w_down)))
    vels = dict(zip(_NAMES, (v_norm_g, v_attn_w_qkv, v_attn_w_o, v_conv_w_in, v_conv_w_dw, v_conv_w_out,
                             v_pool_w_in, v_pool_w_grp, v_pool_scale, v_pool_w_out, v_ffn_w_up, v_ffn_w_dw,
                             v_ffn_w_down)))
    weights = _LayerWeights(shards)
    sink = _GradSink()
    loss, grad_x, vec_grads = _local_step(x[0], loss_target[0], weights.vec, weights, sink)
    loss = lax.psum(loss[0, 0], _AXES)

    items, slots = sink.last
    vec_slots = {k: _slots_from_full(vec_grads[k], k) for k in _VECTORS if k != "ffn_w_dw"}
    vec_slots["ffn_w_dw"] = jnp.stack(vec_grads["ffn_w_dw"], axis=1)
    state, _ = _scatter_begin(slots + [_pack_vectors(vec_slots, (N_DEV,)).astype(BF16)],
                              [_route(k) for k, _ in items] + [ROUTE_B], [f"{k}{j}" for k, j in items] + ["vectors"],
                              "scatter0")
    results = {}

    def step_layer(i):
        last = None
        for k, j in _layer_matrices(i):
            g = sink.sums[(k, j)]
            if _WEIGHTS[k][0][0] == 1:
                results[k] = (g[None], _adamw(shards[k], g[None], moms[k], vels[k], f"adamw_{k}"))
            else:
                gs, prev = results.get(k, ({}, None))
                gs[j] = g
                results[k] = (gs, _adamw(shards[k], g, moms[k], vels[k], f"adamw_{k}{j}", layer=j, prev=prev))
            last = results[k][1][0]
        return last

    for i in (3, 2):
        state, _ = _scatter_next(state, step_layer(i))
    sums, _ = _scatter_next(state, step_layer(1))
    sink.sums.update(dict(zip(items, sums[:-1])))
    step_layer(0)
    vec_sums = _unpack_vectors(sums[-1], ())
    for k in _VECTORS:
        results[k] = (vec_sums[k], _adamw(shards[k], vec_sums[k], moms[k], vels[k], f"adamw_{k}"))
    grads_out = {k: g if not isinstance(g, dict) else jnp.stack([g[j] for j in range(len(g))])
                 for k, (g, _) in results.items()}
    return (loss, grad_x[None], *[grads_out[k] for k in _NAMES], *[results[k][1][0] for k in _NAMES],
            *[results[k][1][1] for k in _NAMES], *[results[k][1][2] for k in _NAMES])
```

```python
import functools
import math

import numpy as np
import jax
import jax.numpy as jnp
from jax import lax
from jax.experimental import pallas as pl
from jax.experimental.pallas import tpu as pltpu

F32, BF16 = jnp.float32, jnp.bfloat16

D_MODEL = 1024
SEQ = 2048
DEPTH = 4
DILATED_CFG = ((128, 1), (512, 4), (2048, 16))
N_GROUPS_A = 3
HEADS = 8
HEAD_DIM = 64
ATTN_WIDTH = HEADS * HEAD_DIM
N_HEADS_A = N_GROUPS_A * HEADS
BLOCK = 128
NEG_INF = -1e30
POOL_GROUP_DIM = 256
D_FF = 2816
RMS_EPS = 1e-6
ADAM_LR, ADAM_B1, ADAM_B2, ADAM_EPS, ADAM_WD, ADAM_STEP = 0.001, 0.9, 0.999, 1e-08, 0.01, 10

N_DEV = 8
LANES = 128
V7X_VMEM_BYTES = 64 * 2 ** 20
VMEM_LIMIT_BYTES = V7X_VMEM_BYTES - 8 * 2 ** 20
COL_TILE = 256
ROW_TILE = 256
MATMUL_TILES = (1024, 1408, 512, 256, 128)
TN_RESIDENT_K = 2048

NN = (((1,), (0,)), ((), ()))
NT = (((1,), (1,)), ((), ()))
TN = (((0,), (0,)), ((), ()))


def _dot(a, b, dims=NN):
    return lax.dot_general(a, b, dims, preferred_element_type=F32)


def _params(sem=None):
    return pltpu.CompilerParams(dimension_semantics=sem, vmem_limit_bytes=VMEM_LIMIT_BYTES)


def _pick(n, prefs):
    for p in prefs:
        if n % p == 0:
            return p
    return n


def _matmul(a, b, mode, out_dtype, name, a_parts=1, b_parts=1):
    if mode == "nn":
        m, k = a.shape[-2], a.shape[-1] * a_parts
        n = b.shape[-1] * b_parts
    elif mode == "nt":
        m, k = a.shape[-2], a.shape[-1] * a_parts
        n = b.shape[-2]
    else:
        k, m = a.shape[-2], a.shape[-1] * a_parts
        n = b.shape[-1] * b_parts
    tm = _pick(m, MATMUL_TILES)
    tn = _pick(n // b_parts if mode != "nt" else n, MATMUL_TILES)
    kk = k // a_parts if mode != "tn" else k
    tk = _pick(kk, MATMUL_TILES)
    if mode == "tn":
        tm = _pick(m // a_parts, MATMUL_TILES)
        if k <= TN_RESIDENT_K:
            tk = k
    gm, gn, gk = m // tm, n // tn, k // tk

    def a_idx(i, j, kq):
        if mode == "tn":
            r, c, per = kq, i, (m // a_parts) // tm
        else:
            r, c, per = i, kq, (k // a_parts) // tk
        return (r, c) if a_parts == 1 else (c // per, r, c % per)

    def b_idx(i, j, kq):
        if mode == "nt":
            return (j, kq)
        per = (n // b_parts) // tn
        return (kq, j) if b_parts == 1 else (j // per, kq, j % per)

    a_blk = (tk, tm) if mode == "tn" else (tm, tk)
    b_blk = (tn, tk) if mode == "nt" else (tk, tn)
    if a_parts > 1:
        a_blk = (None,) + a_blk
    if b_parts > 1:
        b_blk = (None,) + b_blk
    dims = {"nn": NN, "nt": NT, "tn": TN}[mode]

    def body_single(a_ref, b_ref, o_ref):
        o_ref[...] = _dot(a_ref[...], b_ref[...], dims).astype(o_ref.dtype)

    def body(a_ref, b_ref, o_ref, acc_ref):
        kq = pl.program_id(2)

        @pl.when(kq == 0)
        def _():
            acc_ref[...] = jnp.zeros_like(acc_ref)

        acc_ref[...] += _dot(a_ref[...], b_ref[...], dims)

        @pl.when(kq == gk - 1)
        def _():
            o_ref[...] = acc_ref[...].astype(o_ref.dtype)

    return pl.pallas_call(
        body_single if gk == 1 else body, name=name, grid=(gm, gn, gk),
        in_specs=[pl.BlockSpec(a_blk, a_idx), pl.BlockSpec(b_blk, b_idx)],
        out_specs=pl.BlockSpec((tm, tn), lambda i, j, kq: (i, j)),
        out_shape=jax.ShapeDtypeStruct((m, n), out_dtype),
        scratch_shapes=[] if gk == 1 else [pltpu.VMEM((tm, tn), F32)],
        compiler_params=_params(("parallel", "parallel", "arbitrary")),
    )(a, b)


def _rms_fwd(xin, g, res, out_dtype, name):
    s, d = xin.shape
    has_res = res is not None

    def body(*refs):
        x_ref, g_ref = refs[0], refs[1]
        o_ref = refs[-1]
        x = x_ref[...]
        r = lax.rsqrt(jnp.mean(x * x, axis=-1, keepdims=True) + RMS_EPS)
        y = x * r * g_ref[...]
        if has_res:
            y = refs[2][...] + y
        o_ref[...] = y.astype(o_ref.dtype)

    row = pl.BlockSpec((ROW_TILE, d), lambda i: (i, 0))
    vec = pl.BlockSpec((1, d), lambda i: (0, 0))
    ins = [xin, g] + ([res] if has_res else [])
    return pl.pallas_call(
        body, name=name, grid=(s // ROW_TILE,),
        in_specs=[row, vec] + ([row] if has_res else []),
        out_specs=row, out_shape=jax.ShapeDtypeStruct((s, d), out_dtype),
        compiler_params=_params(("parallel",)),
    )(*ins)


def _rms_bwd(xin, g, dy, dres, out_dtype, name):
    s, d = xin.shape
    has_res = dres is not None

    def body(*refs):
        x_ref, g_ref, dy_ref = refs[0], refs[1], refs[2]
        dx_ref, dg_ref = refs[-2], refs[-1]

        @pl.when(pl.program_id(0) == 0)
        def _():
            dg_ref[...] = jnp.zeros_like(dg_ref)

        x = x_ref[...]
        dyv = dy_ref[...].astype(F32)
        r = lax.rsqrt(jnp.mean(x * x, axis=-1, keepdims=True) + RMS_EPS)
        xhat = x * r
        u = dyv * g_ref[...]
        dx = r * (u - xhat * jnp.mean(u * xhat, axis=-1, keepdims=True))
        if has_res:
            dx = refs[3][...] + dx
        dx_ref[...] = dx.astype(dx_ref.dtype)
        dg_ref[...] += jnp.sum(dyv * xhat, axis=0, keepdims=True)

    row = pl.BlockSpec((ROW_TILE, d), lambda i: (i, 0))
    vec = pl.BlockSpec((1, d), lambda i: (0, 0))
    ins = [xin, g, dy] + ([dres] if has_res else [])
    return pl.pallas_call(
        body, name=name, grid=(s // ROW_TILE,),
        in_specs=[row, vec, row] + ([row] if has_res else []),
        out_specs=[row, vec],
        out_shape=[jax.ShapeDtypeStruct((s, d), out_dtype), jax.ShapeDtypeStruct((1, d), F32)],
        compiler_params=_params(("arbitrary",)),
    )(*ins)


def _rms(x):
    r = lax.rsqrt(jnp.mean(x * x, axis=-1, keepdims=True) + RMS_EPS)
    return r, x * r


def _rms_grad(r, xhat, dy, g):
    u = dy * g
    return r * (u - xhat * jnp.mean(u * xhat, axis=-1, keepdims=True))


def _rms_res_pre(sub, g_post, res, g_pre, name):
    s, d = sub.shape

    def body(sub_ref, gp_ref, res_ref, gn_ref, x_ref, n_ref):
        xnew = res_ref[...] + _rms(sub_ref[...])[1] * gp_ref[...]
        x_ref[...] = xnew
        n_ref[...] = (_rms(xnew)[1] * gn_ref[...]).astype(BF16)

    row = pl.BlockSpec((ROW_TILE, d), lambda i: (i, 0))
    vec = pl.BlockSpec((1, d), lambda i: (0, 0))
    return pl.pallas_call(
        body, name=name, grid=(s // ROW_TILE,),
        in_specs=[row, vec, row, vec], out_specs=[row, row],
        out_shape=[jax.ShapeDtypeStruct((s, d), F32), jax.ShapeDtypeStruct((s, d), BF16)],
        compiler_params=_params(("parallel",)),
    )(sub, g_post, res, g_pre)


def _rms_bwd_pair(xmid, g_pre, dn, dres, sub, g_post, name):
    s, d = xmid.shape

    def body(x_ref, gn_ref, dn_ref, dres_ref, sub_ref, gp_ref, dx_ref, dsub_ref, dgn_ref, dgp_ref):
        @pl.when(pl.program_id(0) == 0)
        def _():
            dgn_ref[...] = jnp.zeros_like(dgn_ref)
            dgp_ref[...] = jnp.zeros_like(dgp_ref)

        dnv = dn_ref[...].astype(F32)
        r, xhat = _rms(x_ref[...])
        dx = dres_ref[...] + _rms_grad(r, xhat, dnv, gn_ref[...])
        dx_ref[...] = dx
        dgn_ref[...] += jnp.sum(dnv * xhat, axis=0, keepdims=True)
        rs, shat = _rms(sub_ref[...])
        dsub_ref[...] = _rms_grad(rs, shat, dx, gp_ref[...]).astype(BF16)
        dgp_ref[...] += jnp.sum(dx * shat, axis=0, keepdims=True)

    row = pl.BlockSpec((ROW_TILE, d), lambda i: (i, 0))
    vec = pl.BlockSpec((1, d), lambda i: (0, 0))
    return pl.pallas_call(
        body, name=name, grid=(s // ROW_TILE,),
        in_specs=[row, vec, row, row, row, vec], out_specs=[row, row, vec, vec],
        out_shape=[jax.ShapeDtypeStruct((s, d), F32), jax.ShapeDtypeStruct((s, d), BF16),
                   jax.ShapeDtypeStruct((1, d), F32), jax.ShapeDtypeStruct((1, d), F32)],
        compiler_params=_params(("arbitrary",)),
    )(xmid, g_pre, dn, dres, sub, g_post)


def _loss_head(y, tgt, name):
    s, d = y.shape

    def body(y_ref, t_ref, l_ref, dy_ref):
        @pl.when(pl.program_id(0) == 0)
        def _():
            l_ref[...] = jnp.zeros_like(l_ref)

        e = y_ref[...] - t_ref[...]
        dy_ref[...] = e / d
        per_tok = jnp.mean(e * e, axis=-1, keepdims=True)
        l_ref[...] += 0.5 * jnp.sum(per_tok, axis=0, keepdims=True)

    row = pl.BlockSpec((ROW_TILE, d), lambda i: (i, 0))
    return pl.pallas_call(
        body, name=name, grid=(s // ROW_TILE,),
        in_specs=[row, row],
        out_specs=[pl.BlockSpec((1, 1), lambda i: (0, 0)), row],
        out_shape=[jax.ShapeDtypeStruct((1, 1), F32), jax.ShapeDtypeStruct((s, d), F32)],
        compiler_params=_params(("arbitrary",)),
    )(y, tgt)


SUBLANES = 8


def _shift_down(x, k):
    t, c = x.shape
    r = pltpu.roll(x.reshape(t // SUBLANES, SUBLANES, c), k, axis=1)
    above = jnp.concatenate([jnp.zeros((1, SUBLANES, c), x.dtype), r[:-1]], axis=0)
    rows = lax.broadcasted_iota(jnp.int32, (1, SUBLANES, c), 1)
    return jnp.where(rows >= k, r, above).reshape(t, c)


def _shift_up(x, k):
    t, c = x.shape
    r = pltpu.roll(x.reshape(t // SUBLANES, SUBLANES, c), SUBLANES - k, axis=1)
    below = jnp.concatenate([r[1:], jnp.zeros((1, SUBLANES, c), x.dtype)], axis=0)
    rows = lax.broadcasted_iota(jnp.int32, (1, SUBLANES, c), 1)
    return jnp.where(rows < SUBLANES - k, r, below).reshape(t, c)


def _conv3(h, w):
    return w[2:3] * h + w[1:2] * _shift_down(h, 1) + w[0:1] * _shift_down(h, 2)


def _conv3_bwd(dc, h, w, dw_ref, cols=slice(None)):
    u1, u2 = _shift_up(dc, 1), _shift_up(dc, 2)
    dw_ref[0:1, cols] = jnp.sum(u2 * h, axis=0, keepdims=True)
    dw_ref[1:2, cols] = jnp.sum(u1 * h, axis=0, keepdims=True)
    dw_ref[2:3, cols] = jnp.sum(dc * h, axis=0, keepdims=True)
    return w[2:3] * dc + w[1:2] * u1 + w[0:1] * u2


FFN_PAIRS = N_DEV // 2


def _lane_chunks(width):
    return [(c0, min(COL_TILE, width - c0)) for c0 in range(0, width, COL_TILE)]


def _ffn_up(n, wup, wdw, name):
    s, d = n.shape
    cw = wup.shape[-1]

    def body(n_ref, wg_ref, wu_ref, dg_ref, du_ref, h_ref, c_ref, a_ref):
        x = n_ref[...]
        for c0, size in _lane_chunks(cw):
            cols = slice(c0, c0 + size)
            hg = _dot(x, wg_ref[:, cols])
            hu = _dot(x, wu_ref[:, cols])
            h_ref[0, :, cols] = hg.astype(BF16)
            h_ref[1, :, cols] = hu.astype(BF16)
            cg = _conv3(hg, dg_ref[:, cols])
            cu = _conv3(hu, du_ref[:, cols])
            c_ref[0, :, cols] = cg.astype(BF16)
            c_ref[1, :, cols] = cu.astype(BF16)
            a_ref[:, cols] = (cg * jax.nn.sigmoid(cg) * cu).astype(BF16)

    return pl.pallas_call(
        body, name=name, grid=(FFN_PAIRS,),
        in_specs=[pl.BlockSpec((s, d), lambda j: (0, 0)),
                  pl.BlockSpec((None, d, cw), lambda j: (j, 0, 0)),
                  pl.BlockSpec((None, d, cw), lambda j: (j + FFN_PAIRS, 0, 0)),
                  pl.BlockSpec((None, 3, cw), lambda j: (j, 0, 0)),
                  pl.BlockSpec((None, 3, cw), lambda j: (j + FFN_PAIRS, 0, 0))],
        out_specs=[pl.BlockSpec((None, 2, s, cw), lambda j: (j, 0, 0, 0)),
                   pl.BlockSpec((None, 2, s, cw), lambda j: (j, 0, 0, 0)),
                   pl.BlockSpec((None, s, cw), lambda j: (j, 0, 0))],
        out_shape=[jax.ShapeDtypeStruct((FFN_PAIRS, 2, s, cw), BF16), jax.ShapeDtypeStruct((FFN_PAIRS, 2, s, cw), BF16),
                   jax.ShapeDtypeStruct((FFN_PAIRS, s, cw), BF16)],
        compiler_params=_params(("parallel",)),
    )(n, wup, wup, wdw, wdw)


def _ffn_mid_bwd(do, wdown, h, c, wdw, name):
    s, d = do.shape
    cw = wdown.shape[1]

    def body(do_ref, wd_ref, h_ref, c_ref, wg_ref, wu_ref, dh_ref, dwg_ref, dwu_ref):
        dov = do_ref[...]
        for c0, size in _lane_chunks(cw):
            cols = slice(c0, c0 + size)
            da = _dot(dov, wd_ref[cols, :], NT)
            hg = h_ref[0, :, cols].astype(F32)
            hu = h_ref[1, :, cols].astype(F32)
            wg, wu = wg_ref[:, cols], wu_ref[:, cols]
            cg = c_ref[0, :, cols].astype(F32)
            cu = c_ref[1, :, cols].astype(F32)
            sg = jax.nn.sigmoid(cg)
            dcu = da * (cg * sg)
            dcg = da * cu * (sg * (1.0 + cg * (1.0 - sg)))
            dh_ref[0, :, cols] = _conv3_bwd(dcg, hg, wg, dwg_ref, cols).astype(BF16)
            dh_ref[1, :, cols] = _conv3_bwd(dcu, hu, wu, dwu_ref, cols).astype(BF16)

    vec = jax.ShapeDtypeStruct((FFN_PAIRS, 3, cw), F32)
    return pl.pallas_call(
        body, name=name, grid=(FFN_PAIRS,),
        in_specs=[pl.BlockSpec((s, d), lambda j: (0, 0)), pl.BlockSpec((None, cw, d), lambda j: (j, 0, 0)),
                  pl.BlockSpec((None, 2, s, cw), lambda j: (j, 0, 0, 0)),
                  pl.BlockSpec((None, 2, s, cw), lambda j: (j, 0, 0, 0)),
                  pl.BlockSpec((None, 3, cw), lambda j: (j, 0, 0)),
                  pl.BlockSpec((None, 3, cw), lambda j: (j + FFN_PAIRS, 0, 0))],
        out_specs=[pl.BlockSpec((None, 2, s, cw), lambda j: (j, 0, 0, 0)),
                   pl.BlockSpec((None, 3, cw), lambda j: (j, 0, 0)), pl.BlockSpec((None, 3, cw), lambda j: (j, 0, 0))],
        out_shape=[jax.ShapeDtypeStruct((FFN_PAIRS, 2, s, cw), BF16), vec, vec],
        compiler_params=_params(("parallel",)),
    )(do, wdown, h, c, wdw, wdw)


def _ffn_dwup(n, dh, name):
    s, d = n.shape
    cw = dh.shape[-1]

    def body(n_ref, dh_ref, o_ref):
        o_ref[...] = _dot(n_ref[...], dh_ref[...], TN).astype(BF16)

    return pl.pallas_call(
        body, name=name, grid=(N_DEV,),
        in_specs=[pl.BlockSpec((s, d), lambda k: (0, 0)),
                  pl.BlockSpec((None, None, s, cw), lambda k: (k % FFN_PAIRS, k // FFN_PAIRS, 0, 0))],
        out_specs=pl.BlockSpec((None, d, cw), lambda k: (k, 0, 0)),
        out_shape=jax.ShapeDtypeStruct((N_DEV, d, cw), BF16),
        compiler_params=_params(("parallel",)),
    )(n, dh)


def _ffn_dn(dh, wup, name, dep=None):
    s, cw = dh.shape[-2:]
    d = wup.shape[1]
    tm = _pick(s, MATMUL_TILES)
    deps = [] if dep is None else [dep]

    def body(dh_ref, w_ref, *rest):
        o_ref, acc_ref = rest[-2:]
        k = pl.program_id(1)

        @pl.when(k == 0)
        def _():
            acc_ref[...] = jnp.zeros_like(acc_ref)

        acc_ref[...] += _dot(dh_ref[...], w_ref[...], NT)

        @pl.when(k == N_DEV - 1)
        def _():
            o_ref[...] = acc_ref[...]

    return pl.pallas_call(
        body, name=name, grid=(s // tm, N_DEV),
        in_specs=[pl.BlockSpec((None, None, tm, cw), lambda i, k: (k % FFN_PAIRS, k // FFN_PAIRS, i, 0)),
                  pl.BlockSpec((None, d, cw), lambda i, k: (k, 0, 0))] + [_ANY] * len(deps),
        out_specs=pl.BlockSpec((tm, d), lambda i, k: (i, 0)),
        out_shape=jax.ShapeDtypeStruct((s, d), F32),
        scratch_shapes=[pltpu.VMEM((tm, d), F32)],
        compiler_params=_params(("parallel", "arbitrary")),
    )(dh, wup, *deps)


def _sconv_fwd(n, win, wdw, name):
    s, d = n.shape
    tn = COL_TILE
    nj = d // tn

    def body(n_ref, wb_ref, wc_ref, wh_ref, dw_ref, z_ref, y_ref):
        x = n_ref[...]
        zb = _dot(x, wb_ref[...])
        zc = _dot(x, wc_ref[...])
        zh = _dot(x, wh_ref[...])
        z_ref[0] = zb.astype(BF16)
        z_ref[1] = zc.astype(BF16)
        z_ref[2] = zh.astype(BF16)
        y_ref[...] = (zb * _conv3(zc * zh, dw_ref[...])).astype(BF16)

    return pl.pallas_call(
        body, name=name, grid=(nj,),
        in_specs=[pl.BlockSpec((s, d), lambda j: (0, 0)),
                  pl.BlockSpec((d, tn), lambda j: (0, j)), pl.BlockSpec((d, tn), lambda j: (0, j + nj)),
                  pl.BlockSpec((d, tn), lambda j: (0, j + 2 * nj)), pl.BlockSpec((3, tn), lambda j: (0, j))],
        out_specs=[pl.BlockSpec((3, s, tn), lambda j: (0, 0, j)), pl.BlockSpec((s, tn), lambda j: (0, j))],
        out_shape=[jax.ShapeDtypeStruct((3, s, d), BF16), jax.ShapeDtypeStruct((s, d), BF16)],
        compiler_params=_params(("parallel",)),
    )(n, win, win, win, wdw)


def _sconv_mid_bwd(dm, wout, z, wdw, name):
    s, d = dm.shape
    tn = COL_TILE
    nj = d // tn

    def body(dm_ref, wo_ref, z_ref, w_ref, dz_ref, dw_ref):
        dy = _dot(dm_ref[...], wo_ref[...], NT)
        zb = z_ref[0].astype(F32)
        zc = z_ref[1].astype(F32)
        zh = z_ref[2].astype(F32)
        w = w_ref[...]
        p = zc * zh
        cp = _conv3(p, w)
        dz_ref[0] = (dy * cp).astype(BF16)
        dcp = dy * zb
        dp = _conv3_bwd(dcp, p, w, dw_ref)
        dz_ref[1] = (dp * zh).astype(BF16)
        dz_ref[2] = (dp * zc).astype(BF16)

    return pl.pallas_call(
        body, name=name, grid=(nj,),
        in_specs=[pl.BlockSpec((s, d), lambda j: (0, 0)), pl.BlockSpec((tn, d), lambda j: (j, 0)),
                  pl.BlockSpec((3, s, tn), lambda j: (0, 0, j)), pl.BlockSpec((3, tn), lambda j: (0, j))],
        out_specs=[pl.BlockSpec((3, s, tn), lambda j: (0, 0, j)), pl.BlockSpec((3, tn), lambda j: (0, j))],
        out_shape=[jax.ShapeDtypeStruct((3, s, d), BF16), jax.ShapeDtypeStruct((3, d), F32)],
        compiler_params=_params(("parallel",)),
    )(dm, wout, z, wdw)


def _pool_select(g, c2, c4, c8, c16):
    return jnp.where(g == 0, c2, jnp.where(g == 1, c4, jnp.where(g == 2, c8, c16)))


def _pool_inv_count(g, shape):
    pos = lax.broadcasted_iota(jnp.int32, shape, 0).astype(F32) + 1.0
    win = (2 << g).astype(F32)
    return jnp.minimum(pos, win)


def _pool_fwd(n, win, wgrp, scale, name):
    s, d = n.shape
    tn = POOL_GROUP_DIM

    def body(n_ref, wi_ref, wg_ref, sc_ref, p_ref, y_ref):
        g = pl.program_id(0)
        u = _dot(n_ref[...], wi_ref[...])
        s2 = u + _shift_down(u, 1)
        s4 = s2 + _shift_down(s2, 2)
        s8 = s4 + _shift_down(s4, 4)
        s16 = s8 + _shift_down(s8, 8)
        tot = _pool_select(g, s2, s4, s8, s16)
        p = (tot / _pool_inv_count(g, u.shape) - u).astype(BF16)
        p_ref[...] = p
        y_ref[...] = (_dot(p, wg_ref[...]) * sc_ref[...]).astype(BF16)

    return pl.pallas_call(
        body, name=name, grid=(d // tn,),
        in_specs=[pl.BlockSpec((s, d), lambda g: (0, 0)), pl.BlockSpec((d, tn), lambda g: (0, g)),
                  pl.BlockSpec((None, tn, tn), lambda g: (g, 0, 0)), pl.BlockSpec((1, tn), lambda g: (0, g))],
        out_specs=[pl.BlockSpec((s, tn), lambda g: (0, g)), pl.BlockSpec((s, tn), lambda g: (0, g))],
        out_shape=[jax.ShapeDtypeStruct((s, d), BF16), jax.ShapeDtypeStruct((s, d), BF16)],
        compiler_params=_params(("parallel",)),
    )(n, win, wgrp, scale)


def _pool_mid_bwd(dm, wout, p, wgrp, scale, name):
    s, d = dm.shape
    tn = POOL_GROUP_DIM

    def body(dm_ref, wo_ref, p_ref, wg_ref, sc_ref, du_ref, dwg_ref, dsc_ref):
        g = pl.program_id(0)
        dy = _dot(dm_ref[...], wo_ref[...], NT)
        pv = p_ref[...]
        wg = wg_ref[...]
        ypre = _dot(pv, wg)
        dsc_ref[...] = jnp.sum(dy * ypre, axis=0, keepdims=True)
        dypre = (dy * sc_ref[...]).astype(BF16)
        dwg_ref[...] = _dot(pv, dypre, TN)
        dp = _dot(dypre, wg, NT)
        e = dp / _pool_inv_count(g, dp.shape)
        f2 = e + _shift_up(e, 1)
        f4 = f2 + _shift_up(f2, 2)
        f8 = f4 + _shift_up(f4, 4)
        f16 = f8 + _shift_up(f8, 8)
        du_ref[...] = (_pool_select(g, f2, f4, f8, f16) - dp).astype(BF16)

    return pl.pallas_call(
        body, name=name, grid=(d // tn,),
        in_specs=[pl.BlockSpec((s, d), lambda g: (0, 0)), pl.BlockSpec((tn, d), lambda g: (g, 0)),
                  pl.BlockSpec((s, tn), lambda g: (0, g)), pl.BlockSpec((None, tn, tn), lambda g: (g, 0, 0)),
                  pl.BlockSpec((1, tn), lambda g: (0, g))],
        out_specs=[pl.BlockSpec((s, tn), lambda g: (0, g)), pl.BlockSpec((None, tn, tn), lambda g: (g, 0, 0)),
                   pl.BlockSpec((1, tn), lambda g: (0, g))],
        out_shape=[jax.ShapeDtypeStruct((s, d), BF16), jax.ShapeDtypeStruct((4, tn, tn), F32),
                   jax.ShapeDtypeStruct((1, d), F32)],
        compiler_params=_params(("parallel",)),
    )(dm, wout, p, wgrp, scale)


PANEL = LANES
ATTN_EXT = ATTN_WIDTH + PANEL
DVEC_LANE = HEADS


def _alibi_slopes(g, dil):
    all_slopes = 2.0 ** (-8.0 * np.arange(1, N_HEADS_A + 1) / N_HEADS_A)
    return [float(np.float32(sl) * np.float32(dil)) for sl in all_slopes[g * HEADS:(g + 1) * HEADS]]


def _residue_order(a, dil, name):
    s, w = a.shape
    per = ROW_TILE // dil
    panels = w // PANEL

    def body(a_ref, o_ref, *tiles):
        for c in range(panels):
            cols = slice(c * PANEL, (c + 1) * PANEL)
            tiles[c][...] = a_ref[:, cols].astype(F32)
            for r in range(dil):
                o_ref[r, :, cols] = tiles[c][pl.ds(r, per, stride=dil), :].astype(o_ref.dtype)

    out = pl.pallas_call(
        body, name=name, grid=(s // ROW_TILE,),
        in_specs=[pl.BlockSpec((ROW_TILE, w), lambda i: (i, 0))],
        out_specs=pl.BlockSpec((dil, per, w), lambda i: (0, i, 0)),
        out_shape=jax.ShapeDtypeStruct((dil, s // dil, w), a.dtype),
        scratch_shapes=[pltpu.VMEM((ROW_TILE, PANEL), F32)] * panels,
        compiler_params=_params(("parallel",)),
    )(a)
    return out.reshape(s, w)


def _token_order(a, dil, acc, name):
    s, w = a.shape
    per = ROW_TILE // dil
    panels = w // PANEL
    has_acc = acc is not None

    def body(*refs):
        a_ref = refs[0]
        o_ref = refs[2] if has_acc else refs[1]
        tiles = refs[3:] if has_acc else refs[2:]
        for c in range(panels):
            cols = slice(c * PANEL, (c + 1) * PANEL)
            for r in range(dil):
                tiles[c][pl.ds(r, per, stride=dil), :] = a_ref[r, :, cols]
            v = tiles[c][...]
            if has_acc:
                v = v + refs[1][:, cols]
            o_ref[:, cols] = v

    row = pl.BlockSpec((ROW_TILE, w), lambda i: (i, 0))
    return pl.pallas_call(
        body, name=name, grid=(s // ROW_TILE,),
        in_specs=[pl.BlockSpec((dil, per, w), lambda i: (0, i, 0))] + ([row] if has_acc else []),
        out_specs=row, out_shape=jax.ShapeDtypeStruct((s, w), F32),
        scratch_shapes=[pltpu.VMEM((ROW_TILE, PANEL), F32)] * panels,
        compiler_params=_params(("parallel",)),
    )(*([a.reshape(dil, s // dil, w)] + ([acc] if has_acc else [])))


def _qkv_proj(n, wqkv, g, name):
    s, d = n.shape
    tm = _pick(s, MATMUL_TILES)

    def body(a_ref, b_ref, o_ref):
        o_ref[...] = _dot(a_ref[...], b_ref[...]).astype(BF16)

    return pl.pallas_call(
        body, name=name, grid=(s // tm, 3),
        in_specs=[pl.BlockSpec((tm, d), lambda i, t: (i, 0)),
                  pl.BlockSpec((d, ATTN_WIDTH), lambda i, t: (0, 3 * g + t))],
        out_specs=pl.BlockSpec((None, tm, ATTN_WIDTH), lambda i, t: (t, i, 0)),
        out_shape=jax.ShapeDtypeStruct((3, s, ATTN_WIDTH), BF16),
        compiler_params=_params(("parallel", "parallel")),
    )(n, wqkv)


def _attn_window(n, ln):
    if ln == BLOCK:
        return 0, BLOCK
    return pl.multiple_of(jnp.maximum(n - 1, 0) * BLOCK, BLOCK), 2 * BLOCK


def _attn_mask(n, k0, kw):
    qpos = n * BLOCK + lax.broadcasted_iota(jnp.int32, (BLOCK, kw), 0)
    kpos = k0 + lax.broadcasted_iota(jnp.int32, (BLOCK, kw), 1)
    dist = qpos - kpos
    return dist.astype(F32), (dist >= 0) & (dist <= BLOCK)


def _attn_scores(q, keys, slope, dist, valid):
    s = _dot(q, keys, NT) * (HEAD_DIM ** -0.5) - slope * dist
    return jnp.where(valid, s, NEG_INF)


ATTN_STEP_BLOCKS = 1
ATTN_BWD_STEP_BLOCKS = 4


def _attn_block(gb, ln):
    nb = ln // BLOCK
    n, base = (0, gb * ln) if nb == 1 else (gb % nb, (gb // nb) * ln)
    k0, kw = _attn_window(n, ln)
    cur = pl.ds(pl.multiple_of(gb * BLOCK, BLOCK), BLOCK)
    win = pl.ds(pl.multiple_of(base + k0, BLOCK), kw)
    return cur, win, n, k0, kw


def _attn_fwd(qkv, g, name):
    _, s, w = qkv.shape
    dil = DILATED_CFG[g][1]
    ln = s // dil
    slopes = _alibi_slopes(g, dil)
    rows = ATTN_STEP_BLOCKS * BLOCK

    def body(qkv_ref, o_ref):
        o_ref[:, w:] = jnp.zeros((rows, PANEL), F32)
        for b in range(ATTN_STEP_BLOCKS):
            cur, win, n, k0, kw = _attn_block(pl.program_id(0) * ATTN_STEP_BLOCKS + b, ln)
            dist, valid = _attn_mask(n, k0, kw)
            out = slice(b * BLOCK, (b + 1) * BLOCK)
            for h in range(HEADS):
                cols = slice(h * HEAD_DIM, (h + 1) * HEAD_DIM)
                sc = _attn_scores(qkv_ref[0, cur, cols], qkv_ref[1, win, cols], slopes[h], dist, valid)
                m = jnp.max(sc, axis=-1, keepdims=True)
                p = jnp.exp(sc - m)
                den = jnp.sum(p, axis=-1, keepdims=True)
                o_ref[out, cols] = _dot(p.astype(BF16), qkv_ref[2, win, cols]) / den
                o_ref[out, w + h:w + h + 1] = m + jnp.log(den)

    return pl.pallas_call(
        body, name=name, grid=(s // rows,),
        in_specs=[pl.BlockSpec((3, s, w), lambda i: (0, 0, 0))],
        out_specs=pl.BlockSpec((rows, ATTN_EXT), lambda i: (i, 0)),
        out_shape=jax.ShapeDtypeStruct((s, ATTN_EXT), F32),
        compiler_params=_params(("parallel",)),
    )(qkv)


def _attn_bwd(qkv, dext, g, name, dep=None):
    _, s, w = qkv.shape
    dil = DILATED_CFG[g][1]
    ln = s // dil
    slopes = _alibi_slopes(g, dil)
    scale = HEAD_DIM ** -0.5
    rows = ATTN_BWD_STEP_BLOCKS * BLOCK
    steps = s // rows
    deps = [] if dep is None else [dep]

    def body(qkv_ref, de_ref, *rest):
        d_ref, dk_ref, dv_ref = rest[-3:]

        @pl.when(pl.program_id(0) == 0)
        def _():
            dk_ref[...] = jnp.zeros_like(dk_ref)
            dv_ref[...] = jnp.zeros_like(dv_ref)

        for b in range(ATTN_BWD_STEP_BLOCKS):
            cur, win, n, k0, kw = _attn_block(pl.program_id(0) * ATTN_BWD_STEP_BLOCKS + b, ln)
            dist, valid = _attn_mask(n, k0, kw)
            blk = slice(b * BLOCK, (b + 1) * BLOCK)
            for h in range(HEADS):
                cols = slice(h * HEAD_DIM, (h + 1) * HEAD_DIM)
                q, keys = qkv_ref[0, cur, cols], qkv_ref[1, win, cols]
                dob = de_ref[blk, cols].astype(BF16)
                p = jnp.exp(_attn_scores(q, keys, slopes[h], dist, valid) - de_ref[blk, w + h:w + h + 1])
                dd = de_ref[blk, w + DVEC_LANE + h:w + DVEC_LANE + h + 1]
                ds = (p * (_dot(dob, qkv_ref[2, win, cols], NT) - dd)).astype(BF16)
                d_ref[0, cur, cols] = (scale * _dot(ds, keys)).astype(BF16)
                dv_ref[win, cols] += _dot(p.astype(BF16), dob, TN)
                dk_ref[win, cols] += scale * _dot(ds, q, TN)

        @pl.when(pl.program_id(0) == steps - 1)
        def _():
            d_ref[1] = dk_ref[...].astype(BF16)
            d_ref[2] = dv_ref[...].astype(BF16)

    whole = pl.BlockSpec((3, s, w), lambda i: (0, 0, 0))
    return pl.pallas_call(
        body, name=name, grid=(steps,),
        in_specs=[whole, pl.BlockSpec((rows, ATTN_EXT), lambda i: (i, 0))] + [_ANY] * len(deps),
        out_specs=whole, out_shape=jax.ShapeDtypeStruct((3, s, w), BF16),
        scratch_shapes=[pltpu.VMEM((s, w), F32), pltpu.VMEM((s, w), F32)],
        compiler_params=_params(("arbitrary",)),
    )(qkv, dext, *deps)


def _attn_merge(e0, e1, e2, name):
    s = e0.shape[0]
    w = ATTN_WIDTH

    def body(e0_ref, e1_ref, e2_ref, m_ref, mb_ref, lse_ref):
        refs = (e0_ref, e1_ref, e2_ref)
        l = [r[:, w:w + HEADS] for r in refs]
        mx = jnp.maximum(jnp.maximum(l[0], l[1]), l[2])
        e = [jnp.exp(v - mx) for v in l]
        z = e[0] + e[1] + e[2]
        lse_ref[...] = mx + jnp.log(z)
        wts = [v / z for v in e]
        for h in range(HEADS):
            cols = slice(h * HEAD_DIM, (h + 1) * HEAD_DIM)
            acc = wts[0][:, h:h + 1] * refs[0][:, cols]
            for g in range(1, N_GROUPS_A):
                acc = acc + wts[g][:, h:h + 1] * refs[g][:, cols]
            m_ref[:, cols] = acc
            mb_ref[:, cols] = acc.astype(BF16)

    ext = pl.BlockSpec((ROW_TILE, ATTN_EXT), lambda i: (i, 0))
    row = pl.BlockSpec((ROW_TILE, w), lambda i: (i, 0))
    return pl.pallas_call(
        body, name=name, grid=(s // ROW_TILE,),
        in_specs=[ext, ext, ext],
        out_specs=[row, row, pl.BlockSpec((ROW_TILE, HEADS), lambda i: (i, 0))],
        out_shape=[jax.ShapeDtypeStruct((s, w), F32), jax.ShapeDtypeStruct((s, w), BF16),
                   jax.ShapeDtypeStruct((s, HEADS), F32)],
        compiler_params=_params(("parallel",)),
    )(e0, e1, e2)


def _attn_dvec(dmerged, merged, lse_all, name, dep=None):
    s, w = merged.shape
    deps = [] if dep is None else [dep]

    def body(dm_ref, m_ref, lse_ref, *rest):
        de_ref = rest[-1]
        dmv = dm_ref[...]
        de_ref[:, :w] = dmv
        de_ref[:, w:] = jnp.zeros((ROW_TILE, PANEL), F32)
        de_ref[:, w:w + HEADS] = lse_ref[...]
        prod = dmv * m_ref[...]
        for h in range(HEADS):
            lane = w + DVEC_LANE + h
            de_ref[:, lane:lane + 1] = jnp.sum(prod[:, h * HEAD_DIM:(h + 1) * HEAD_DIM], axis=-1, keepdims=True)

    row = pl.BlockSpec((ROW_TILE, w), lambda i: (i, 0))
    return pl.pallas_call(
        body, name=name, grid=(s // ROW_TILE,),
        in_specs=[row, row, pl.BlockSpec((ROW_TILE, HEADS), lambda i: (i, 0))] + [_ANY] * len(deps),
        out_specs=pl.BlockSpec((ROW_TILE, ATTN_EXT), lambda i: (i, 0)),
        out_shape=jax.ShapeDtypeStruct((s, ATTN_EXT), F32),
        compiler_params=_params(("parallel",)),
    )(dmerged, merged, lse_all, *deps)


def _attention_fwd(n, wqkv, wo, tag):
    ns, qkvs, exts = [], [], []
    for g, (_, dil) in enumerate(DILATED_CFG):
        ng = n if dil == 1 else _residue_order(n, dil, f"{tag}_order_g{g}")
        qkv = _qkv_proj(ng, wqkv, g, f"{tag}_qkv_g{g}")
        ext = _attn_fwd(qkv, g, f"{tag}_fwd_g{g}")
        ns.append(ng)
        qkvs.append(qkv)
        exts.append(ext if dil == 1 else _token_order(ext, dil, None, f"{tag}_unorder_g{g}"))
    merged, merged_bf, lse_all = _attn_merge(*exts, f"{tag}_merge")
    m = _matmul(merged_bf, wo, "nn", F32, f"{tag}_wo")
    return m, (ns, qkvs, merged, merged_bf, lse_all)


def _attention_bwd(dm, wqkv, wo, saved, tag, dep=None, hook=None):
    ns, qkvs, merged, merged_bf, lse_all = saved
    d_wo = _matmul(merged_bf, dm, "tn", BF16, f"{tag}_dwo")
    dmerged = _matmul(dm, wo, "nt", F32, f"{tag}_dmerged")
    dext = _attn_dvec(dmerged, merged, lse_all, f"{tag}_dvec", dep)
    width = 3 * ATTN_WIDTH
    d_wqkv, dn, dep = [], None, None
    for g, (_, dil) in enumerate(DILATED_CFG):
        dext_g = dext if dil == 1 else _residue_order(dext, dil, f"{tag}_dorder_g{g}")
        dqkv = _attn_bwd(qkvs[g], dext_g, g, f"{tag}_bwd_g{g}", dep)
        dep = hook(g, dqkv) if hook is not None and g + 1 < N_GROUPS_A else None
        d_wqkv.append(_matmul(ns[g], dqkv, "tn", BF16, f"{tag}_dwqkv_g{g}", b_parts=3))
        dn_g = _matmul(dqkv, wqkv[:, g * width:(g + 1) * width], "nt", F32, f"{tag}_dn_g{g}", a_parts=3)
        dn = dn_g if dil == 1 else _token_order(dn_g, dil, dn, f"{tag}_dn_sum_g{g}")
    return dn, jnp.concatenate(d_wqkv, axis=1), d_wo


def _layer_matrices(i):
    mixer = (("attn_w_qkv", "attn_w_o"), ("conv_w_in", "conv_w_out"), ("pool_w_in", "pool_w_grp", "pool_w_out"))[i % 3]
    return [(k, i // 3) for k in mixer] + [("ffn_w_up", i), ("ffn_w_down", i)]


def _local_step(x, tgt, vec, weights, sink):
    ng = vec["norm_g"]

    def gain(i, j, token=None):
        g = ng[i, j][None, :]
        return g if token is None else g + token

    saved = []
    n = _rms_fwd(x, gain(0, 0), None, BF16, "norm_first")
    for i in range(DEPTH):
        wl = weights.layer(i)
        t0 = weights.hook(i, 0, n)
        kind, idx = i % 3, i // 3
        if kind == 0:
            m, ms = _attention_fwd(n, wl["attn_w_qkv"], wl["attn_w_o"], "attn")
        elif kind == 1:
            taps = vec["conv_w_dw"][idx] if t0 is None else vec["conv_w_dw"][idx] + t0
            z, y = _sconv_fwd(n, wl["conv_w_in"], taps, "sconv_fwd")
            m = _matmul(y, wl["conv_w_out"], "nn", F32, "sconv_out")
            ms = (z, y)
        else:
            scale = vec["pool_scale"][idx][None, :] if t0 is None else vec["pool_scale"][idx][None, :] + t0
            p, y = _pool_fwd(n, wl["pool_w_in"], wl["pool_w_grp"], scale, "pool_fwd")
            m = _matmul(y, wl["pool_w_out"], "nn", F32, "pool_out")
            ms = (p, y)
        t1 = weights.hook(i, 1, m)
        x1, n2 = _rms_res_pre(m, gain(i, 1, t0), x, gain(i, 2, t1), "norm_res_pre")
        h, c, a = _ffn_up(n2, wl["ffn_w_up"], vec["ffn_w_dw"][i], "ffn_up")
        t2 = weights.hook(i, 2, a)
        f = _matmul(a, wl["ffn_w_down"].reshape(D_FF, D_MODEL), "nn", F32, "ffn_down", a_parts=FFN_PAIRS)
        saved.append((x, n, m, ms, x1, n2, h, a, f, wl, c))
        if i + 1 < DEPTH:
            x, n = _rms_res_pre(f, gain(i, 3, t2), x1, gain(i + 1, 0), "norm_res_pre")
        else:
            x = _rms_fwd(f, gain(i, 3), x1, F32, "norm_res")
        weights.hook(i, 3, x)

    loss, dx = _loss_head(x, tgt, "loss_head")

    g_norm = [[None] * 4 for _ in range(DEPTH)]
    g_taps, g_scale, g_ffn_dw = [], [], [None] * DEPTH
    df, g_norm[DEPTH - 1][3] = _rms_bwd(saved[-1][8], gain(DEPTH - 1, 3), dx, None, BF16, "norm_bwd_sub")
    t0 = None
    for i in reversed(range(DEPTH)):
        xin, n, m, ms, x1, n2, h, a, f, wl, c = saved[i]
        kind, idx = i % 3, i // 3
        gl = {}
        d_wdown = _matmul(a, df, "tn", BF16, "ffn_dwdown", a_parts=FFN_PAIRS)
        gl["ffn_w_down"] = d_wdown.reshape(N_DEV, D_FF // N_DEV, D_MODEL)
        ffn_taps = vec["ffn_w_dw"][i] if t0 is None else vec["ffn_w_dw"][i] + t0
        dh, dwg, dwu = _ffn_mid_bwd(df, wl["ffn_w_down"].reshape(FFN_PAIRS, -1, D_MODEL), h, c, ffn_taps, "ffn_mid_bwd")
        g_ffn_dw[i] = jnp.concatenate([dwg, dwu], axis=0)
        t1 = sink.hook(i, 1, dh)
        gl["ffn_w_up"] = _ffn_dwup(n2, dh, "ffn_dwup")
        tf = sink.ffn_done(i, gl)
        dn2 = _ffn_dn(dh, wl["ffn_w_up"], "ffn_dn", t1)
        dx1, dm, g_norm[i][2], g_norm[i][1] = _rms_bwd_pair(x1, gain(i, 2, tf), dn2, dx, m, gain(i, 1), "norm_bwd_pair")
        t2 = sink.hook(i, 2, dm)
        if kind == 0:
            dn, gl["attn_w_qkv"], gl["attn_w_o"] = _attention_bwd(
                dm, wl["attn_w_qkv"], wl["attn_w_o"], ms, "attn", t2, lambda g, after, i=i: sink.hook(i, ("a", "b")[g], after))
        elif kind == 1:
            z, y = ms
            gl["conv_w_out"] = _matmul(y, dm, "tn", BF16, "sconv_dwout")
            taps = vec["conv_w_dw"][idx] if t2 is None else vec["conv_w_dw"][idx] + t2
            dz, ddw = _sconv_mid_bwd(dm, wl["conv_w_out"], z, taps, "sconv_mid_bwd")
            g_taps.append(ddw)
            gl["conv_w_in"] = _matmul(n, dz, "tn", BF16, "sconv_dwin", b_parts=3)
            dn = _matmul(dz, wl["conv_w_in"], "nt", F32, "sconv_dn", a_parts=3)
        else:
            p, y = ms
            gl["pool_w_out"] = _matmul(y, dm, "tn", BF16, "pool_dwout")
            scale = vec["pool_scale"][idx][None, :] if t2 is None else vec["pool_scale"][idx][None, :] + t2
            du, gl["pool_w_grp"], dscale = _pool_mid_bwd(dm, wl["pool_w_out"], p, wl["pool_w_grp"], scale, "pool_mid_bwd")
            g_scale.append(dscale[0])
            gl["pool_w_in"] = _matmul(n, du, "tn", BF16, "pool_dwin")
            dn = _matmul(du, wl["pool_w_in"], "nt", F32, "pool_dn")
        sink.hook(i, 3, dn)
        if i > 0:
            dx, df, g_norm[i][0], g_norm[i - 1][3] = _rms_bwd_pair(xin, gain(i, 0, t2), dn, dx1, saved[i - 1][8],
                                                                   gain(i - 1, 3), "norm_bwd_pair")
        else:
            dx, g_norm[0][0] = _rms_bwd(xin, gain(0, 0), dn, dx1, F32, "norm_bwd_res")
        t0 = sink.layer_done(i, gl)

    vec_grads = {"norm_g": jnp.stack([jnp.concatenate(row, axis=0) for row in g_norm]), "conv_w_dw": jnp.stack(g_taps),
                 "pool_scale": jnp.stack(g_scale), "ffn_w_dw": g_ffn_dw}
    return loss, dx, vec_grads


_AXES = ("x", "y", "c")
ROUTE_A = ("y", "x", "c")
ROUTE_B = ("x", "y", "c")
def _dev_index(pos):
    return 4 * pos["x"] + 2 * pos["y"] + pos["c"]


_HBM = pl.BlockSpec(memory_space=pltpu.HBM)
_SEM = pl.BlockSpec(memory_space=pltpu.SEMAPHORE)
_ANY = pl.BlockSpec(memory_space=pl.ANY)
_EFFECT = pltpu.SideEffectType.DATAFLOW_SIDE_EFFECTING


TOKEN_SHAPE = (1, D_MODEL)


def _copies_start(describe, arrays, n_copies, name, after, token_shape=TOKEN_SHAPE):
    n = len(arrays)
    deps = [] if after is None else [after]

    def body(*refs):
        send_sems, recv_sems = refs[n + len(deps)], refs[n + len(deps) + 1]
        for c in describe(refs[:n], send_sems, recv_sems):
            c.start()
        refs[-1][...] = jnp.zeros_like(refs[-1])

    outs = pl.pallas_call(
        body, name=f"{name}_start",
        out_shape=(pltpu.SemaphoreType.DMA((n_copies,)), pltpu.SemaphoreType.DMA((n_copies,)),
                   *[pltpu.HBM(a.shape, a.dtype) for a in arrays], jax.ShapeDtypeStruct(token_shape, F32)),
        in_specs=[_HBM] * n + [_ANY] * len(deps),
        out_specs=(_SEM, _SEM, *([_HBM] * n), pl.BlockSpec(memory_space=pltpu.VMEM)),
        input_output_aliases={i: 2 + i for i in range(n)},
        compiler_params=pltpu.CompilerParams(has_side_effects=_EFFECT),
    )(*[pltpu.with_memory_space_constraint(a, pltpu.HBM) for a in arrays], *deps)
    return (outs[0], outs[1], list(outs[2:2 + n])), outs[-1]


def _copies_wait(describe, handle, name, after):
    send_sems, recv_sems, arrays = handle
    n = len(arrays)
    deps = [] if after is None else [after]

    def body(*refs):
        for c in describe(refs[:n], refs[n], refs[n + 1]):
            c.wait_send()
            c.wait_recv()

    outs = pl.pallas_call(
        body, name=f"{name}_wait",
        out_shape=tuple(pltpu.HBM(a.shape, a.dtype) for a in arrays),
        in_specs=[_HBM] * n + [_SEM, _SEM] + [_ANY] * len(deps), out_specs=tuple([_HBM] * n),
        input_output_aliases={i: i for i in range(n)},
        compiler_params=pltpu.CompilerParams(has_side_effects=_EFFECT),
    )(*arrays, send_sems, recv_sems, *deps)
    return list(outs)


GATHER_STAGE_COPIES = (3, 3, 1)
_PEER_FLIPS = (("x",), ("y",), ("c",), ("x", "y"), ("x", "c"), ("y", "c"), ("x", "y", "c"))


def _direct_copies(n, scatter):
    def describe(refs, send_sems, recv_sems):
        pos = {a: lax.axis_index(a) for a in _AXES}
        me = _dev_index(pos)
        copies = []
        for i in range(n):
            for flip in _PEER_FLIPS:
                to = {a: 1 - pos[a] if a in flip else pos[a] for a in _AXES}
                k = len(copies)
                copies.append(pltpu.make_async_remote_copy(
                    src_ref=refs[i].at[_dev_index(to)] if scatter else refs[i], dst_ref=refs[n + i].at[me],
                    send_sem=send_sems.at[k], recv_sem=recv_sems.at[k],
                    device_id=tuple(to[a] for a in _AXES), device_id_type=pl.DeviceIdType.MESH))
        return copies

    return describe


def _direct_begin(arrays, scatter, name, after):
    n = len(arrays)
    lands = [lax.empty(a.shape if scatter else (N_DEV,) + a.shape, a.dtype) for a in arrays]
    handle, token = _copies_start(_direct_copies(n, scatter), list(arrays) + lands, len(_PEER_FLIPS) * n, name, after)
    return (handle, n, scatter, name), token


def _direct_end(state, after):
    handle, n, scatter, name = state
    both = _copies_wait(_direct_copies(n, scatter), handle, name, after)
    me = _dev_index({a: lax.axis_index(a) for a in _AXES})
    own = [lax.dynamic_index_in_dim(a, me, 0, keepdims=False) if scatter else a for a in both[:n]]
    return [lax.dynamic_update_index_in_dim(land, o, me, 0) for land, o in zip(both[n:], own)]


def _sum_slots(a, name):
    shape = a.shape[1:]
    cols = shape[-1]
    rows = math.prod(shape[:-1])
    tr = _pick(rows, ADD_ROW_TILES)

    def body(a_ref, o_ref):
        acc = a_ref[0].astype(F32)
        for d in range(1, N_DEV):
            acc = acc + a_ref[d].astype(F32)
        o_ref[...] = acc

    out = pl.pallas_call(
        body, name=name, grid=(rows // tr,),
        in_specs=[pl.BlockSpec((N_DEV, tr, cols), lambda i: (0, i, 0))],
        out_specs=pl.BlockSpec((tr, cols), lambda i: (i, 0)),
        out_shape=jax.ShapeDtypeStruct((rows, cols), F32),
        compiler_params=_params(("parallel",)),
    )(a.reshape(N_DEV, rows, cols))
    return out.reshape(shape)


def _gather_copies(stage, routes):
    n = len(routes)

    def describe(refs, send_sems, recv_sems):
        pos = {a: lax.axis_index(a) for a in _AXES}

        def flipped(axes):
            return {a: 1 - pos[a] if a in axes else pos[a] for a in _AXES}

        copies = []
        for i, (a1, a2, a3) in enumerate(routes):
            land = refs[n + i] if stage == 1 else refs[i]
            p1, p2, p12, p3 = flipped((a1,)), flipped((a2,)), flipped((a1, a2)), flipped((a3,))
            plan = {1: [(None, p1), (None, p2), (None, p3)], 2: [(p1, p2), (p1, p3), (p2, p3)], 3: [(p12, p3)]}[stage]
            for holder, to in plan:
                slot = land.at[_dev_index(pos if holder is None else holder)]
                k = len(copies)
                copies.append(pltpu.make_async_remote_copy(
                    src_ref=refs[i] if holder is None else slot, dst_ref=slot,
                    send_sem=send_sems.at[k], recv_sem=recv_sems.at[k],
                    device_id=tuple(to[a] for a in _AXES), device_id_type=pl.DeviceIdType.MESH))
        return copies

    return describe


def _gather_begin(shards, routes, name, after):
    n = len(shards)
    lands = [lax.empty((N_DEV,) + a.shape, a.dtype) for a in shards]
    handle, token = _copies_start(_gather_copies(1, routes), list(shards) + lands, GATHER_STAGE_COPIES[0] * n,
                                  f"{name}_1", after)
    return {"stage": 1, "handle": handle, "routes": routes, "name": name, "n": n}, token


def _gather_next(state, after):
    stage, routes, name, n = state["stage"], state["routes"], state["name"], state["n"]
    arrays = _copies_wait(_gather_copies(stage, routes), state["handle"], f"{name}_{stage}", after)
    if stage == 1:
        state = dict(state, shards=arrays[:n])
        arrays = arrays[n:]
    if stage == 3:
        me = _dev_index({a: lax.axis_index(a) for a in _AXES})
        return [lax.dynamic_update_index_in_dim(o, s, me, 0) for o, s in zip(arrays, state["shards"])], None
    handle, token = _copies_start(_gather_copies(stage + 1, routes), arrays, GATHER_STAGE_COPIES[stage] * n,
                                  f"{name}_{stage + 1}", None)
    return dict(state, stage=stage + 1, handle=handle), token


ADD_ROW_TILES = (1024, 704, 512, 352, 256, 128, 96, 64, 32, 16)


def _add_half(a, recv, me, out_dtype, name):
    p, q, cols = recv.shape
    tr = _pick(q, ADD_ROW_TILES)

    def body(me_ref, a_ref, b_ref, o_ref):
        o_ref[...] = (a_ref[...].astype(F32) + b_ref[...].astype(F32)).astype(o_ref.dtype)

    return pl.pallas_call(
        body, name=name,
        grid_spec=pltpu.PrefetchScalarGridSpec(
            num_scalar_prefetch=1, grid=(p, q // tr),
            in_specs=[pl.BlockSpec((None, None, tr, cols), lambda j, i, m: (j, m[0], i, 0)),
                      pl.BlockSpec((None, tr, cols), lambda j, i, m: (j, i, 0))],
            out_specs=pl.BlockSpec((None, tr, cols), lambda j, i, m: (j, i, 0))),
        out_shape=jax.ShapeDtypeStruct((p, q, cols), out_dtype),
        compiler_params=_params(("parallel", "parallel")),
    )(me, a, recv)


def _half_copies(axes):
    n = len(axes)

    def describe(refs, send_sems, recv_sems):
        pos = {a: lax.axis_index(a) for a in _AXES}
        copies = []
        for i, axis in enumerate(axes):
            peer = tuple(1 - pos[a] if a == axis else pos[a] for a in _AXES)
            copies.append(pltpu.make_async_remote_copy(
                src_ref=refs[i].at[:, 1 - pos[axis]], dst_ref=refs[n + i], send_sem=send_sems.at[i],
                recv_sem=recv_sems.at[i], device_id=peer, device_id_type=pl.DeviceIdType.MESH))
        return copies

    return describe


def _scatter_begin(slots, routes, tags, name, token_shape=TOKEN_SHAPE):
    shapes = [a.shape[1:] for a in slots]
    rows = [math.prod(s[:-1]) for s in shapes]
    arrays = [a.reshape(4, 2, n, s[-1]) for a, n, s in zip(slots, rows, shapes)]
    return _scatter_start({"stage": 0, "arrays": arrays, "routes": routes, "tags": tags, "name": name,
                           "shapes": shapes, "rows": rows}, token_shape)


def _scatter_start(state, token_shape=TOKEN_SHAPE):
    stage, arrays = state["stage"], state["arrays"]
    axes = [r[2 - stage] for r in state["routes"]]
    lands = [lax.empty((a.shape[0],) + a.shape[2:], a.dtype) for a in arrays]
    handle, token = _copies_start(_half_copies(axes), arrays + lands, len(arrays), f"{state['name']}_{stage + 1}", None,
                                  token_shape)
    return dict(state, handle=handle, axes=axes), token


def _scatter_next(state, after):
    stage, axes, n = state["stage"], state["axes"], len(state["arrays"])
    both = _copies_wait(_half_copies(axes), state["handle"], f"{state['name']}_{stage + 1}", after)
    coord = {a: lax.axis_index(a).astype(jnp.int32).reshape(1) for a in _AXES}
    sums = [_add_half(a, r, coord[ax], F32 if stage == 2 else BF16, f"scatter_add_{stage + 1}_{t}")
            for a, r, ax, t in zip(both[:n], both[n:], axes, state["tags"])]
    if stage == 2:
        return [a.reshape(s) for a, s in zip(sums, state["shapes"])], None
    if stage == 0:
        views = [(1, 2, 2 * r, s[-1]) if route[1] == "x" else (2, 2, r, s[-1])
                 for r, s, route in zip(state["rows"], state["shapes"], state["routes"])]
    else:
        views = [(1, 2, r, s[-1]) for r, s in zip(state["rows"], state["shapes"])]
    return _scatter_start(dict(state, stage=stage + 1, arrays=[a.reshape(v) for a, v in zip(sums, views)]))


_WEIGHTS = {
    "norm_g": ((DEPTH, 4, D_MODEL), 2, True),
    "attn_w_qkv": ((2, D_MODEL, 4608), 2, False),
    "attn_w_o": ((2, ATTN_WIDTH, D_MODEL), 2, False),
    "conv_w_in": ((1, D_MODEL, 3 * D_MODEL), 2, False),
    "conv_w_dw": ((1, 3, D_MODEL), 2, True),
    "conv_w_out": ((1, D_MODEL, D_MODEL), 1, False),
    "pool_w_in": ((1, D_MODEL, D_MODEL), 1, False),
    "pool_w_grp": ((1, 4, POOL_GROUP_DIM, POOL_GROUP_DIM), 2, False),
    "pool_scale": ((1, D_MODEL), 1, True),
    "pool_w_out": ((1, D_MODEL, D_MODEL), 1, False),
    "ffn_w_up": ((DEPTH, D_MODEL, 2 * D_FF), 2, False),
    "ffn_w_dw": ((DEPTH, 3, 2 * D_FF), 2, True),
    "ffn_w_down": ((DEPTH, D_FF, D_MODEL), 1, False),
}
_NAMES = tuple(_WEIGHTS)
_VECTORS = tuple(k for k in _NAMES if _WEIGHTS[k][2])
_MATRICES = tuple(k for k in _NAMES if not _WEIGHTS[k][2])
_FFN = ("ffn_w_up", "ffn_w_down")
_ON_ROUTE_A = ("ffn_w_up", "attn_w_o", "conv_w_out", "pool_w_in")
PACK_ROWS = 16


def _route(name):
    return ROUTE_A if name in _ON_ROUTE_A else ROUTE_B


def _shard_shape(name):
    shape, ax, _ = _WEIGHTS[name]
    return tuple(s // N_DEV if i == ax else s for i, s in enumerate(shape))


def _full_from_slots(slots, name, layers=None):
    shape, ax, _ = _WEIGHTS[name]
    if layers is not None:
        shape = (layers,) + shape[1:]
    return jnp.moveaxis(slots, 0, ax).reshape(shape)


def _slots_from_full(full, name):
    shape, ax, _ = _WEIGHTS[name]
    split = shape[:ax] + (N_DEV, shape[ax] // N_DEV) + shape[ax + 1:]
    return jnp.moveaxis(full.reshape(split), ax, 0)


def _pack_vectors(parts, lead):
    rows = []
    for k in _VECTORS:
        r = parts[k].reshape(lead + (-1, LANES))
        pad = -r.shape[-2] % PACK_ROWS
        rows.append(jnp.pad(r, [(0, 0)] * len(lead) + [(0, pad), (0, 0)]))
    return jnp.concatenate(rows, axis=len(lead))


def _unpack_vectors(buf, lead):
    out, r0 = {}, 0
    for k in _VECTORS:
        shard = _shard_shape(k)
        rows = math.prod(shard) // LANES
        out[k] = buf[..., r0:r0 + rows, :].reshape(lead + shard)
        r0 += rows + (-rows % PACK_ROWS)
    return out


class _LayerWeights:
    def __init__(self, shards):
        self.cast = {k: shards[k].astype(BF16) for k in _MATRICES}
        first, ffn0 = _layer_matrices(0)[:-2], _layer_matrices(0)[-2:]
        state, _ = _direct_begin(self._send(first) + [_pack_vectors(shards, ())], False, "gather0", None)
        outs = _direct_end(state, None)
        vec = _unpack_vectors(outs[-1], (N_DEV,))
        self.vec = {k: _full_from_slots(vec[k], k) for k in _VECTORS}
        self.vec["ffn_w_dw"] = [vec["ffn_w_dw"][:, l] for l in range(DEPTH)]
        self.ready = {0: self._unpack(first, outs[:-1])}
        self.chains = {}
        tokens = []
        self._begin("ffn0", ffn0, "gather0f", outs[0], tokens)
        self._begin(1, _layer_matrices(1), "gather1", outs[0], tokens)
        self.vec["norm_g"] = self.vec["norm_g"] + (tokens[0] + tokens[1])

    def _send(self, items):
        return [self.cast[k][j] for k, j in items]

    @staticmethod
    def _unpack(items, outs):
        return {k: o if k in _FFN else _full_from_slots(o[:, None], k, layers=1)[0] for (k, _), o in zip(items, outs)}

    def _begin(self, key, items, name, after, tokens):
        state, token = _gather_begin(self._send(items), [_route(k) for k, _ in items], name, after)
        self.chains[key] = (items, state)
        tokens.append(token)

    def _advance(self, key, after, tokens):
        items, state = self.chains.pop(key)
        state, token = _gather_next(state, after)
        if token is None:
            self.ready.setdefault(0 if key == "ffn0" else key, {}).update(self._unpack(items, state))
        else:
            self.chains[key] = (items, state)
            tokens.append(token)

    def layer(self, i):
        return self.ready[i]

    def hook(self, i, point, after):
        tokens = []
        if i == 0 and point == 0:
            self._advance("ffn0", after, tokens)
        if i == 0 and point == 1:
            self._advance("ffn0", after, tokens)
            self._advance("ffn0", None, tokens)
        if point >= 1 and i + 1 in self.chains:
            self._advance(i + 1, after, tokens)
        if point == 1 and i + 2 < DEPTH:
            self._begin(i + 2, _layer_matrices(i + 2), f"gather{i + 2}", after, tokens)
        return functools.reduce(lambda a, b: a + b, tokens) if tokens else None


def _layer_slots(g, name):
    shape, ax, _ = _WEIGHTS[name]
    shape, ax = shape[1:], ax - 1
    split = shape[:ax] + (N_DEV, shape[ax] // N_DEV) + shape[ax + 1:]
    return jnp.moveaxis(g.reshape(split), ax, 0).astype(BF16)


class _GradSink:
    def __init__(self):
        self.state = None
        self.ffn_state = None
        self.sums = {}
        self.last = None

    def ffn_done(self, i, grads):
        if i != 0:
            return None
        self.ffn_items = _layer_matrices(0)[-2:]
        self.ffn_state, token = _scatter_begin([grads[k] for k, _ in self.ffn_items],
                                               [_route(k) for k, _ in self.ffn_items],
                                               [f"{k}{j}" for k, j in self.ffn_items], "scatter0f")
        return token

    def layer_done(self, i, grads):
        items = _layer_matrices(i)
        if i == 0:
            items = items[:-2]
            self.last = (items, [_layer_slots(grads[k], k) for k, _ in items])
            return None
        slots = [grads[k] if k in _FFN else _layer_slots(grads[k], k) for k, _ in items]
        self.items = items
        self.state, token = _scatter_begin(slots, [_route(k) for k, _ in items], [f"{k}{j}" for k, j in items],
                                           f"scatter{i}", (N_DEV, 3, 2 * D_FF // N_DEV))
        return token

    def hook(self, i, point, after):
        tokens = []
        if self.state is not None and point in (1, 2, 3):
            self.state, token = _scatter_next(self.state, after)
            if point == 3:
                self.sums.update(dict(zip(self.items, self.state)))
                self.state = None
            tokens.append(token)
        if self.ffn_state is not None and point in ("a", "b", 3):
            self.ffn_state, token = _scatter_next(self.ffn_state, after)
            if point == 3:
                self.sums.update(dict(zip(self.ffn_items, self.ffn_state)))
                self.ffn_state = None
            tokens.append(token)
        tokens = [t for t in tokens if t is not None]
        return functools.reduce(lambda a, b: a + b, tokens) if tokens else None


def _adamw(w, g, m, v, name, layer=None, prev=None):
    shape = w.shape
    cols = shape[-1]
    view = shape if len(shape) == 3 else (1, math.prod(shape[:-1]), cols)
    layers, rows, _ = view
    tr = _pick(rows, (512, 256, 128, 64, 32, 16, 8))
    n_prev = 0 if prev is None else 3

    def body(*refs):
        w_ref, g_ref, m_ref, v_ref = refs[n_prev:n_prev + 4]
        d_ref, nm_ref, nv_ref = refs[n_prev + 4:]
        gv = g_ref[...]
        nm = ADAM_B1 * m_ref[...] + (1.0 - ADAM_B1) * gv
        nv = ADAM_B2 * v_ref[...] + (1.0 - ADAM_B2) * jnp.square(gv)
        m_hat = nm / (1.0 - ADAM_B1 ** ADAM_STEP)
        v_hat = nv / (1.0 - ADAM_B2 ** ADAM_STEP)
        d_ref[...] = -ADAM_LR * (m_hat / (jnp.sqrt(v_hat) + ADAM_EPS) + ADAM_WD * w_ref[...])
        nm_ref[...] = nm
        nv_ref[...] = nv

    if layer is None:
        grid = (layers, rows // tr)
        blk = gblk = pl.BlockSpec((None, tr, cols), lambda l, i: (l, i, 0))
        gview = view
    else:
        grid = (rows // tr,)
        blk = pl.BlockSpec((None, tr, cols), lambda i: (layer, i, 0))
        gblk = pl.BlockSpec((tr, cols), lambda i: (i, 0))
        gview = (rows, cols)
    shp = jax.ShapeDtypeStruct(view, F32)
    outs = pl.pallas_call(
        body, name=name, grid=grid, in_specs=[_ANY] * n_prev + [blk, gblk, blk, blk], out_specs=[blk] * 3,
        out_shape=[shp] * 3, input_output_aliases={i: i for i in range(n_prev)},
        compiler_params=_params(("parallel",) * len(grid)),
    )(*([] if prev is None else [p.reshape(view) for p in prev]), w.reshape(view), g.reshape(gview), m.reshape(view),
      v.reshape(view))
    return [o.reshape(shape) for o in outs]


def kernel(x, norm_g, attn_w_qkv, attn_w_o, conv_w_in, conv_w_dw, conv_w_out, pool_w_in, pool_w_grp, pool_scale, pool_w_out, ffn_w_up, ffn_w_dw, ffn_w_down, loss_target, m_norm_g, m_attn_w_qkv, m_attn_w_o, m_conv_w_in, m_conv_w_dw, m_conv_w_out, m_pool_w_in, m_pool_w_grp, m_pool_scale, m_pool_w_out, m_ffn_w_up, m_ffn_w_dw, m_ffn_w_down, v_norm_g, v_attn_w_qkv, v_attn_w_o, v_conv_w_in, v_conv_w_dw, v_conv_w_out, v_pool_w_in, v_pool_w_grp, v_pool_scale, v_pool_w_out, v_ffn_w_up, v_ffn_w_dw, v_ffn_w_down):
    shards = dict(zip(_NAMES, (norm_g, attn_w_qkv, attn_w_o, conv_w_in, conv_w_dw, conv_w_out, pool_w_in,
                               pool_w_grp, pool_scale, pool_w_out, ffn_w_up, ffn_w_dw, ffn_w_down)))
    moms = dict(zip(_NAMES, (m_norm_g, m_attn_w_qkv, m_attn_w_o, m_conv_w_in, m_conv_w_dw, m_conv_w_out,
                             m_pool_w_in, m_pool_w_grp, m_pool_scale, m_pool_w_out, m_ffn_w_up, m_ffn_w_dw,
                             m_ffn_w_down)))
    vels = dict(zip(_NAMES, (v_norm_g, v_attn_w_qkv, v_attn_w_o, v_conv_w_in, v_conv_w_dw, v_conv_w_out,
                             v_pool_w_in, v_pool_w_grp, v_pool_scale, v_pool_w_out, v_ffn_w_up, v_ffn_w_dw,
                             v_ffn_w_down)))
    weights = _LayerWeights(shards)
    sink = _GradSink()
    loss, grad_x, vec_grads = _local_step(x[0], loss_target[0], weights.vec, weights, sink)
    loss = lax.psum(loss[0, 0], _AXES)

    items, slots = sink.last
    vec_slots = {k: _slots_from_full(vec_grads[k], k) for k in _VECTORS if k != "ffn_w_dw"}
    vec_slots["ffn_w_dw"] = jnp.stack(vec_grads["ffn_w_dw"], axis=1)
    state, _ = _direct_begin(slots + [_pack_vectors(vec_slots, (N_DEV,)).astype(BF16)], True, "scatter0", None)
    results = {}

    def step_layer(i):
        last = None
        for k, j in _layer_matrices(i):
            g = sink.sums[(k, j)]
            if _WEIGHTS[k][0][0] == 1:
                results[k] = (g[None], _adamw(shards[k], g[None], moms[k], vels[k], f"adamw_{k}"))
            else:
                gs, prev = results.get(k, ({}, None))
                gs[j] = g
                results[k] = (gs, _adamw(shards[k], g, moms[k], vels[k], f"adamw_{k}{j}", layer=j, prev=prev))
            last = results[k][1][0]
        return last

    for i in (3, 2, 1):
        last = step_layer(i)
    lands = _direct_end(state, last)
    sums = [_sum_slots(a, f"scatter_sum_{t}") for a, t in zip(lands, [f"{k}{j}" for k, j in items] + ["vectors"])]
    sink.sums.update(dict(zip(items, sums[:-1])))
    step_layer(0)
    vec_sums = _unpack_vectors(sums[-1], ())
    for k in _VECTORS:
        results[k] = (vec_sums[k], _adamw(shards[k], vec_sums[k], moms[k], vels[k], f"adamw_{k}"))
    grads_out = {k: g if not isinstance(g, dict) else jnp.stack([g[j] for j in range(len(g))])
                 for k, (g, _) in results.items()}
    return (loss, grad_x[None], *[grads_out[k] for k in _NAMES], *[results[k][1][0] for k in _NAMES],
            *[results[k][1][1] for k in _NAMES], *[results[k][1][2] for k in _NAMES])
```

```python
import functools
import math

import numpy as np
import jax
import jax.numpy as jnp
from jax import lax
from jax.experimental import pallas as pl
from jax.experimental.pallas import tpu as pltpu

F32, BF16 = jnp.float32, jnp.bfloat16

D_MODEL = 1024
SEQ = 2048
DEPTH = 4
DILATED_CFG = ((128, 1), (512, 4), (2048, 16))
N_GROUPS_A = 3
HEADS = 8
HEAD_DIM = 64
ATTN_WIDTH = HEADS * HEAD_DIM
N_HEADS_A = N_GROUPS_A * HEADS
BLOCK = 128
NEG_INF = -1e30
POOL_GROUP_DIM = 256
D_FF = 2816
RMS_EPS = 1e-6
ADAM_LR, ADAM_B1, ADAM_B2, ADAM_EPS, ADAM_WD, ADAM_STEP = 0.001, 0.9, 0.999, 1e-08, 0.01, 10

N_DEV = 8
LANES = 128
V7X_VMEM_BYTES = 64 * 2 ** 20
VMEM_LIMIT_BYTES = V7X_VMEM_BYTES - 8 * 2 ** 20
COL_TILE = 256
ROW_TILE = 256
MATMUL_TILES = (1024, 1408, 512, 256, 128)
TN_RESIDENT_K = 2048

NN = (((1,), (0,)), ((), ()))
NT = (((1,), (1,)), ((), ()))
TN = (((0,), (0,)), ((), ()))


def _dot(a, b, dims=NN):
    return lax.dot_general(a, b, dims, preferred_element_type=F32)


def _params(sem=None):
    return pltpu.CompilerParams(dimension_semantics=sem, vmem_limit_bytes=VMEM_LIMIT_BYTES)


def _pick(n, prefs):
    for p in prefs:
        if n % p == 0:
            return p
    return n


def _matmul(a, b, mode, out_dtype, name, a_parts=1, b_parts=1):
    if mode == "nn":
        m, k = a.shape[-2], a.shape[-1] * a_parts
        n = b.shape[-1] * b_parts
    elif mode == "nt":
        m, k = a.shape[-2], a.shape[-1] * a_parts
        n = b.shape[-2]
    else:
        k, m = a.shape[-2], a.shape[-1] * a_parts
        n = b.shape[-1] * b_parts
    tm = _pick(m, MATMUL_TILES)
    tn = _pick(n // b_parts if mode != "nt" else n, MATMUL_TILES)
    kk = k // a_parts if mode != "tn" else k
    tk = _pick(kk, MATMUL_TILES)
    if mode == "tn":
        tm = _pick(m // a_parts, MATMUL_TILES)
        if k <= TN_RESIDENT_K:
            tk = k
    gm, gn, gk = m // tm, n // tn, k // tk

    def a_idx(i, j, kq):
        if mode == "tn":
            r, c, per = kq, i, (m // a_parts) // tm
        else:
            r, c, per = i, kq, (k // a_parts) // tk
        return (r, c) if a_parts == 1 else (c // per, r, c % per)

    def b_idx(i, j, kq):
        if mode == "nt":
            return (j, kq)
        per = (n // b_parts) // tn
        return (kq, j) if b_parts == 1 else (j // per, kq, j % per)

    a_blk = (tk, tm) if mode == "tn" else (tm, tk)
    b_blk = (tn, tk) if mode == "nt" else (tk, tn)
    if a_parts > 1:
        a_blk = (None,) + a_blk
    if b_parts > 1:
        b_blk = (None,) + b_blk
    dims = {"nn": NN, "nt": NT, "tn": TN}[mode]

    def body_single(a_ref, b_ref, o_ref):
        o_ref[...] = _dot(a_ref[...], b_ref[...], dims).astype(o_ref.dtype)

    def body(a_ref, b_ref, o_ref, acc_ref):
        kq = pl.program_id(2)

        @pl.when(kq == 0)
        def _():
            acc_ref[...] = jnp.zeros_like(acc_ref)

        acc_ref[...] += _dot(a_ref[...], b_ref[...], dims)

        @pl.when(kq == gk - 1)
        def _():
            o_ref[...] = acc_ref[...].astype(o_ref.dtype)

    return pl.pallas_call(
        body_single if gk == 1 else body, name=name, grid=(gm, gn, gk),
        in_specs=[pl.BlockSpec(a_blk, a_idx), pl.BlockSpec(b_blk, b_idx)],
        out_specs=pl.BlockSpec((tm, tn), lambda i, j, kq: (i, j)),
        out_shape=jax.ShapeDtypeStruct((m, n), out_dtype),
        scratch_shapes=[] if gk == 1 else [pltpu.VMEM((tm, tn), F32)],
        compiler_params=_params(("parallel", "parallel", "arbitrary")),
    )(a, b)


def _rms_fwd(xin, g, res, out_dtype, name):
    s, d = xin.shape
    has_res = res is not None

    def body(*refs):
        x_ref, g_ref = refs[0], refs[1]
        o_ref = refs[-1]
        x = x_ref[...]
        r = lax.rsqrt(jnp.mean(x * x, axis=-1, keepdims=True) + RMS_EPS)
        y = x * r * g_ref[...]
        if has_res:
            y = refs[2][...] + y
        o_ref[...] = y.astype(o_ref.dtype)

    row = pl.BlockSpec((ROW_TILE, d), lambda i: (i, 0))
    vec = pl.BlockSpec((1, d), lambda i: (0, 0))
    ins = [xin, g] + ([res] if has_res else [])
    return pl.pallas_call(
        body, name=name, grid=(s // ROW_TILE,),
        in_specs=[row, vec] + ([row] if has_res else []),
        out_specs=row, out_shape=jax.ShapeDtypeStruct((s, d), out_dtype),
        compiler_params=_params(("parallel",)),
    )(*ins)


def _rms_bwd(xin, g, dy, dres, out_dtype, name):
    s, d = xin.shape
    has_res = dres is not None

    def body(*refs):
        x_ref, g_ref, dy_ref = refs[0], refs[1], refs[2]
        dx_ref, dg_ref = refs[-2], refs[-1]

        @pl.when(pl.program_id(0) == 0)
        def _():
            dg_ref[...] = jnp.zeros_like(dg_ref)

        x = x_ref[...]
        dyv = dy_ref[...].astype(F32)
        r = lax.rsqrt(jnp.mean(x * x, axis=-1, keepdims=True) + RMS_EPS)
        xhat = x * r
        u = dyv * g_ref[...]
        dx = r * (u - xhat * jnp.mean(u * xhat, axis=-1, keepdims=True))
        if has_res:
            dx = refs[3][...] + dx
        dx_ref[...] = dx.astype(dx_ref.dtype)
        dg_ref[...] += jnp.sum(dyv * xhat, axis=0, keepdims=True)

    row = pl.BlockSpec((ROW_TILE, d), lambda i: (i, 0))
    vec = pl.BlockSpec((1, d), lambda i: (0, 0))
    ins = [xin, g, dy] + ([dres] if has_res else [])
    return pl.pallas_call(
        body, name=name, grid=(s // ROW_TILE,),
        in_specs=[row, vec, row] + ([row] if has_res else []),
        out_specs=[row, vec],
        out_shape=[jax.ShapeDtypeStruct((s, d), out_dtype), jax.ShapeDtypeStruct((1, d), F32)],
        compiler_params=_params(("arbitrary",)),
    )(*ins)


def _rms(x):
    r = lax.rsqrt(jnp.mean(x * x, axis=-1, keepdims=True) + RMS_EPS)
    return r, x * r


def _rms_grad(r, xhat, dy, g):
    u = dy * g
    return r * (u - xhat * jnp.mean(u * xhat, axis=-1, keepdims=True))


def _rms_res_pre(sub, g_post, res, g_pre, name):
    s, d = sub.shape

    def body(sub_ref, gp_ref, res_ref, gn_ref, x_ref, n_ref):
        xnew = res_ref[...] + _rms(sub_ref[...])[1] * gp_ref[...]
        x_ref[...] = xnew
        n_ref[...] = (_rms(xnew)[1] * gn_ref[...]).astype(BF16)

    row = pl.BlockSpec((ROW_TILE, d), lambda i: (i, 0))
    vec = pl.BlockSpec((1, d), lambda i: (0, 0))
    return pl.pallas_call(
        body, name=name, grid=(s // ROW_TILE,),
        in_specs=[row, vec, row, vec], out_specs=[row, row],
        out_shape=[jax.ShapeDtypeStruct((s, d), F32), jax.ShapeDtypeStruct((s, d), BF16)],
        compiler_params=_params(("parallel",)),
    )(sub, g_post, res, g_pre)


def _rms_bwd_pair(xmid, g_pre, dn, dres, sub, g_post, name):
    s, d = xmid.shape

    def body(x_ref, gn_ref, dn_ref, dres_ref, sub_ref, gp_ref, dx_ref, dsub_ref, dgn_ref, dgp_ref):
        @pl.when(pl.program_id(0) == 0)
        def _():
            dgn_ref[...] = jnp.zeros_like(dgn_ref)
            dgp_ref[...] = jnp.zeros_like(dgp_ref)

        dnv = dn_ref[...].astype(F32)
        r, xhat = _rms(x_ref[...])
        dx = dres_ref[...] + _rms_grad(r, xhat, dnv, gn_ref[...])
        dx_ref[...] = dx
        dgn_ref[...] += jnp.sum(dnv * xhat, axis=0, keepdims=True)
        rs, shat = _rms(sub_ref[...])
        dsub_ref[...] = _rms_grad(rs, shat, dx, gp_ref[...]).astype(BF16)
        dgp_ref[...] += jnp.sum(dx * shat, axis=0, keepdims=True)

    row = pl.BlockSpec((ROW_TILE, d), lambda i: (i, 0))
    vec = pl.BlockSpec((1, d), lambda i: (0, 0))
    return pl.pallas_call(
        body, name=name, grid=(s // ROW_TILE,),
        in_specs=[row, vec, row, row, row, vec], out_specs=[row, row, vec, vec],
        out_shape=[jax.ShapeDtypeStruct((s, d), F32), jax.ShapeDtypeStruct((s, d), BF16),
                   jax.ShapeDtypeStruct((1, d), F32), jax.ShapeDtypeStruct((1, d), F32)],
        compiler_params=_params(("arbitrary",)),
    )(xmid, g_pre, dn, dres, sub, g_post)


def _loss_head(y, tgt, name):
    s, d = y.shape

    def body(y_ref, t_ref, l_ref, dy_ref):
        @pl.when(pl.program_id(0) == 0)
        def _():
            l_ref[...] = jnp.zeros_like(l_ref)

        e = y_ref[...] - t_ref[...]
        dy_ref[...] = e / d
        per_tok = jnp.mean(e * e, axis=-1, keepdims=True)
        l_ref[...] += 0.5 * jnp.sum(per_tok, axis=0, keepdims=True)

    row = pl.BlockSpec((ROW_TILE, d), lambda i: (i, 0))
    return pl.pallas_call(
        body, name=name, grid=(s // ROW_TILE,),
        in_specs=[row, row],
        out_specs=[pl.BlockSpec((1, 1), lambda i: (0, 0)), row],
        out_shape=[jax.ShapeDtypeStruct((1, 1), F32), jax.ShapeDtypeStruct((s, d), F32)],
        compiler_params=_params(("arbitrary",)),
    )(y, tgt)


SUBLANES = 8


def _shift_down(x, k):
    t, c = x.shape
    r = pltpu.roll(x.reshape(t // SUBLANES, SUBLANES, c), k, axis=1)
    above = jnp.concatenate([jnp.zeros((1, SUBLANES, c), x.dtype), r[:-1]], axis=0)
    rows = lax.broadcasted_iota(jnp.int32, (1, SUBLANES, c), 1)
    return jnp.where(rows >= k, r, above).reshape(t, c)


def _shift_up(x, k):
    t, c = x.shape
    r = pltpu.roll(x.reshape(t // SUBLANES, SUBLANES, c), SUBLANES - k, axis=1)
    below = jnp.concatenate([r[1:], jnp.zeros((1, SUBLANES, c), x.dtype)], axis=0)
    rows = lax.broadcasted_iota(jnp.int32, (1, SUBLANES, c), 1)
    return jnp.where(rows < SUBLANES - k, r, below).reshape(t, c)


def _conv3(h, w):
    return w[2:3] * h + w[1:2] * _shift_down(h, 1) + w[0:1] * _shift_down(h, 2)


def _conv3_bwd(dc, h, w, dw_ref, cols=slice(None)):
    u1, u2 = _shift_up(dc, 1), _shift_up(dc, 2)
    dw_ref[0:1, cols] = jnp.sum(u2 * h, axis=0, keepdims=True)
    dw_ref[1:2, cols] = jnp.sum(u1 * h, axis=0, keepdims=True)
    dw_ref[2:3, cols] = jnp.sum(dc * h, axis=0, keepdims=True)
    return w[2:3] * dc + w[1:2] * u1 + w[0:1] * u2


FFN_PAIRS = N_DEV // 2


def _lane_chunks(width):
    return [(c0, min(COL_TILE, width - c0)) for c0 in range(0, width, COL_TILE)]


def _ffn_up(n, wup, wdw, name):
    s, d = n.shape
    cw = wup.shape[-1]

    def body(n_ref, wg_ref, wu_ref, dg_ref, du_ref, h_ref, c_ref, a_ref):
        x = n_ref[...]
        for c0, size in _lane_chunks(cw):
            cols = slice(c0, c0 + size)
            hg = _dot(x, wg_ref[:, cols])
            hu = _dot(x, wu_ref[:, cols])
            h_ref[0, :, cols] = hg.astype(BF16)
            h_ref[1, :, cols] = hu.astype(BF16)
            cg = _conv3(hg, dg_ref[:, cols])
            cu = _conv3(hu, du_ref[:, cols])
            c_ref[0, :, cols] = cg.astype(BF16)
            c_ref[1, :, cols] = cu.astype(BF16)
            a_ref[:, cols] = (cg * jax.nn.sigmoid(cg) * cu).astype(BF16)

    return pl.pallas_call(
        body, name=name, grid=(FFN_PAIRS,),
        in_specs=[pl.BlockSpec((s, d), lambda j: (0, 0)),
                  pl.BlockSpec((None, d, cw), lambda j: (j, 0, 0)),
                  pl.BlockSpec((None, d, cw), lambda j: (j + FFN_PAIRS, 0, 0)),
                  pl.BlockSpec((None, 3, cw), lambda j: (j, 0, 0)),
                  pl.BlockSpec((None, 3, cw), lambda j: (j + FFN_PAIRS, 0, 0))],
        out_specs=[pl.BlockSpec((None, 2, s, cw), lambda j: (j, 0, 0, 0)),
                   pl.BlockSpec((None, 2, s, cw), lambda j: (j, 0, 0, 0)),
                   pl.BlockSpec((None, s, cw), lambda j: (j, 0, 0))],
        out_shape=[jax.ShapeDtypeStruct((FFN_PAIRS, 2, s, cw), BF16), jax.ShapeDtypeStruct((FFN_PAIRS, 2, s, cw), BF16),
                   jax.ShapeDtypeStruct((FFN_PAIRS, s, cw), BF16)],
        compiler_params=_params(("parallel",)),
    )(n, wup, wup, wdw, wdw)


def _ffn_mid_bwd(do, wdown, h, c, wdw, name):
    s, d = do.shape
    cw = wdown.shape[1]

    def body(do_ref, wd_ref, h_ref, c_ref, wg_ref, wu_ref, dh_ref, dwg_ref, dwu_ref):
        dov = do_ref[...]
        for c0, size in _lane_chunks(cw):
            cols = slice(c0, c0 + size)
            da = _dot(dov, wd_ref[cols, :], NT)
            hg = h_ref[0, :, cols].astype(F32)
            hu = h_ref[1, :, cols].astype(F32)
            wg, wu = wg_ref[:, cols], wu_ref[:, cols]
            cg = c_ref[0, :, cols].astype(F32)
            cu = c_ref[1, :, cols].astype(F32)
            sg = jax.nn.sigmoid(cg)
            dcu = da * (cg * sg)
            dcg = da * cu * (sg * (1.0 + cg * (1.0 - sg)))
            dh_ref[0, :, cols] = _conv3_bwd(dcg, hg, wg, dwg_ref, cols).astype(BF16)
            dh_ref[1, :, cols] = _conv3_bwd(dcu, hu, wu, dwu_ref, cols).astype(BF16)

    vec = jax.ShapeDtypeStruct((FFN_PAIRS, 3, cw), F32)
    return pl.pallas_call(
        body, name=name, grid=(FFN_PAIRS,),
        in_specs=[pl.BlockSpec((s, d), lambda j: (0, 0)), pl.BlockSpec((None, cw, d), lambda j: (j, 0, 0)),
                  pl.BlockSpec((None, 2, s, cw), lambda j: (j, 0, 0, 0)),
                  pl.BlockSpec((None, 2, s, cw), lambda j: (j, 0, 0, 0)),
                  pl.BlockSpec((None, 3, cw), lambda j: (j, 0, 0)),
                  pl.BlockSpec((None, 3, cw), lambda j: (j + FFN_PAIRS, 0, 0))],
        out_specs=[pl.BlockSpec((None, 2, s, cw), lambda j: (j, 0, 0, 0)),
                   pl.BlockSpec((None, 3, cw), lambda j: (j, 0, 0)), pl.BlockSpec((None, 3, cw), lambda j: (j, 0, 0))],
        out_shape=[jax.ShapeDtypeStruct((FFN_PAIRS, 2, s, cw), BF16), vec, vec],
        compiler_params=_params(("parallel",)),
    )(do, wdown, h, c, wdw, wdw)


def _ffn_dwup(n, dh, name):
    s, d = n.shape
    cw = dh.shape[-1]

    def body(n_ref, dh_ref, o_ref):
        o_ref[...] = _dot(n_ref[...], dh_ref[...], TN).astype(BF16)

    return pl.pallas_call(
        body, name=name, grid=(N_DEV,),
        in_specs=[pl.BlockSpec((s, d), lambda k: (0, 0)),
                  pl.BlockSpec((None, None, s, cw), lambda k: (k % FFN_PAIRS, k // FFN_PAIRS, 0, 0))],
        out_specs=pl.BlockSpec((None, d, cw), lambda k: (k, 0, 0)),
        out_shape=jax.ShapeDtypeStruct((N_DEV, d, cw), BF16),
        compiler_params=_params(("parallel",)),
    )(n, dh)


def _ffn_dn(dh, wup, name, dep=None):
    s, cw = dh.shape[-2:]
    d = wup.shape[1]
    tm = _pick(s, MATMUL_TILES)
    deps = [] if dep is None else [dep]

    def body(dh_ref, w_ref, *rest):
        o_ref, acc_ref = rest[-2:]
        k = pl.program_id(1)

        @pl.when(k == 0)
        def _():
            acc_ref[...] = jnp.zeros_like(acc_ref)

        acc_ref[...] += _dot(dh_ref[...], w_ref[...], NT)

        @pl.when(k == N_DEV - 1)
        def _():
            o_ref[...] = acc_ref[...]

    return pl.pallas_call(
        body, name=name, grid=(s // tm, N_DEV),
        in_specs=[pl.BlockSpec((None, None, tm, cw), lambda i, k: (k % FFN_PAIRS, k // FFN_PAIRS, i, 0)),
                  pl.BlockSpec((None, d, cw), lambda i, k: (k, 0, 0))] + [_ANY] * len(deps),
        out_specs=pl.BlockSpec((tm, d), lambda i, k: (i, 0)),
        out_shape=jax.ShapeDtypeStruct((s, d), F32),
        scratch_shapes=[pltpu.VMEM((tm, d), F32)],
        compiler_params=_params(("parallel", "arbitrary")),
    )(dh, wup, *deps)


def _sconv_fwd(n, win, wdw, name):
    s, d = n.shape
    tn = COL_TILE
    nj = d // tn

    def body(n_ref, wb_ref, wc_ref, wh_ref, dw_ref, z_ref, y_ref):
        x = n_ref[...]
        zb = _dot(x, wb_ref[...])
        zc = _dot(x, wc_ref[...])
        zh = _dot(x, wh_ref[...])
        z_ref[0] = zb.astype(BF16)
        z_ref[1] = zc.astype(BF16)
        z_ref[2] = zh.astype(BF16)
        y_ref[...] = (zb * _conv3(zc * zh, dw_ref[...])).astype(BF16)

    return pl.pallas_call(
        body, name=name, grid=(nj,),
        in_specs=[pl.BlockSpec((s, d), lambda j: (0, 0)),
                  pl.BlockSpec((d, tn), lambda j: (0, j)), pl.BlockSpec((d, tn), lambda j: (0, j + nj)),
                  pl.BlockSpec((d, tn), lambda j: (0, j + 2 * nj)), pl.BlockSpec((3, tn), lambda j: (0, j))],
        out_specs=[pl.BlockSpec((3, s, tn), lambda j: (0, 0, j)), pl.BlockSpec((s, tn), lambda j: (0, j))],
        out_shape=[jax.ShapeDtypeStruct((3, s, d), BF16), jax.ShapeDtypeStruct((s, d), BF16)],
        compiler_params=_params(("parallel",)),
    )(n, win, win, win, wdw)


def _sconv_mid_bwd(dm, wout, z, wdw, name):
    s, d = dm.shape
    tn = COL_TILE
    nj = d // tn

    def body(dm_ref, wo_ref, z_ref, w_ref, dz_ref, dw_ref):
        dy = _dot(dm_ref[...], wo_ref[...], NT)
        zb = z_ref[0].astype(F32)
        zc = z_ref[1].astype(F32)
        zh = z_ref[2].astype(F32)
        w = w_ref[...]
        p = zc * zh
        cp = _conv3(p, w)
        dz_ref[0] = (dy * cp).astype(BF16)
        dcp = dy * zb
        dp = _conv3_bwd(dcp, p, w, dw_ref)
        dz_ref[1] = (dp * zh).astype(BF16)
        dz_ref[2] = (dp * zc).astype(BF16)

    return pl.pallas_call(
        body, name=name, grid=(nj,),
        in_specs=[pl.BlockSpec((s, d), lambda j: (0, 0)), pl.BlockSpec((tn, d), lambda j: (j, 0)),
                  pl.BlockSpec((3, s, tn), lambda j: (0, 0, j)), pl.BlockSpec((3, tn), lambda j: (0, j))],
        out_specs=[pl.BlockSpec((3, s, tn), lambda j: (0, 0, j)), pl.BlockSpec((3, tn), lambda j: (0, j))],
        out_shape=[jax.ShapeDtypeStruct((3, s, d), BF16), jax.ShapeDtypeStruct((3, d), F32)],
        compiler_params=_params(("parallel",)),
    )(dm, wout, z, wdw)


def _pool_select(g, c2, c4, c8, c16):
    return jnp.where(g == 0, c2, jnp.where(g == 1, c4, jnp.where(g == 2, c8, c16)))


def _pool_inv_count(g, shape):
    pos = lax.broadcasted_iota(jnp.int32, shape, 0).astype(F32) + 1.0
    win = (2 << g).astype(F32)
    return jnp.minimum(pos, win)


def _pool_fwd(n, win, wgrp, scale, name):
    s, d = n.shape
    tn = POOL_GROUP_DIM

    def body(n_ref, wi_ref, wg_ref, sc_ref, p_ref, y_ref):
        g = pl.program_id(0)
        u = _dot(n_ref[...], wi_ref[...])
        s2 = u + _shift_down(u, 1)
        s4 = s2 + _shift_down(s2, 2)
        s8 = s4 + _shift_down(s4, 4)
        s16 = s8 + _shift_down(s8, 8)
        tot = _pool_select(g, s2, s4, s8, s16)
        p = (tot / _pool_inv_count(g, u.shape) - u).astype(BF16)
        p_ref[...] = p
        y_ref[...] = (_dot(p, wg_ref[...]) * sc_ref[...]).astype(BF16)

    return pl.pallas_call(
        body, name=name, grid=(d // tn,),
        in_specs=[pl.BlockSpec((s, d), lambda g: (0, 0)), pl.BlockSpec((d, tn), lambda g: (0, g)),
                  pl.BlockSpec((None, tn, tn), lambda g: (g, 0, 0)), pl.BlockSpec((1, tn), lambda g: (0, g))],
        out_specs=[pl.BlockSpec((s, tn), lambda g: (0, g)), pl.BlockSpec((s, tn), lambda g: (0, g))],
        out_shape=[jax.ShapeDtypeStruct((s, d), BF16), jax.ShapeDtypeStruct((s, d), BF16)],
        compiler_params=_params(("parallel",)),
    )(n, win, wgrp, scale)


def _pool_mid_bwd(dm, wout, p, wgrp, scale, name):
    s, d = dm.shape
    tn = POOL_GROUP_DIM

    def body(dm_ref, wo_ref, p_ref, wg_ref, sc_ref, du_ref, dwg_ref, dsc_ref):
        g = pl.program_id(0)
        dy = _dot(dm_ref[...], wo_ref[...], NT)
        pv = p_ref[...]
        wg = wg_ref[...]
        ypre = _dot(pv, wg)
        dsc_ref[...] = jnp.sum(dy * ypre, axis=0, keepdims=True)
        dypre = (dy * sc_ref[...]).astype(BF16)
        dwg_ref[...] = _dot(pv, dypre, TN)
        dp = _dot(dypre, wg, NT)
        e = dp / _pool_inv_count(g, dp.shape)
        f2 = e + _shift_up(e, 1)
        f4 = f2 + _shift_up(f2, 2)
        f8 = f4 + _shift_up(f4, 4)
        f16 = f8 + _shift_up(f8, 8)
        du_ref[...] = (_pool_select(g, f2, f4, f8, f16) - dp).astype(BF16)

    return pl.pallas_call(
        body, name=name, grid=(d // tn,),
        in_specs=[pl.BlockSpec((s, d), lambda g: (0, 0)), pl.BlockSpec((tn, d), lambda g: (g, 0)),
                  pl.BlockSpec((s, tn), lambda g: (0, g)), pl.BlockSpec((None, tn, tn), lambda g: (g, 0, 0)),
                  pl.BlockSpec((1, tn), lambda g: (0, g))],
        out_specs=[pl.BlockSpec((s, tn), lambda g: (0, g)), pl.BlockSpec((None, tn, tn), lambda g: (g, 0, 0)),
                   pl.BlockSpec((1, tn), lambda g: (0, g))],
        out_shape=[jax.ShapeDtypeStruct((s, d), BF16), jax.ShapeDtypeStruct((4, tn, tn), F32),
                   jax.ShapeDtypeStruct((1, d), F32)],
        compiler_params=_params(("parallel",)),
    )(dm, wout, p, wgrp, scale)


PANEL = LANES
ATTN_EXT = ATTN_WIDTH + PANEL
DVEC_LANE = HEADS


def _alibi_slopes(g, dil):
    all_slopes = 2.0 ** (-8.0 * np.arange(1, N_HEADS_A + 1) / N_HEADS_A)
    return [float(np.float32(sl) * np.float32(dil)) for sl in all_slopes[g * HEADS:(g + 1) * HEADS]]


def _residue_order(a, dil, name):
    s, w = a.shape
    per = ROW_TILE // dil
    panels = w // PANEL

    def body(a_ref, o_ref, *tiles):
        for c in range(panels):
            cols = slice(c * PANEL, (c + 1) * PANEL)
            tiles[c][...] = a_ref[:, cols].astype(F32)
            for r in range(dil):
                o_ref[r, :, cols] = tiles[c][pl.ds(r, per, stride=dil), :].astype(o_ref.dtype)

    out = pl.pallas_call(
        body, name=name, grid=(s // ROW_TILE,),
        in_specs=[pl.BlockSpec((ROW_TILE, w), lambda i: (i, 0))],
        out_specs=pl.BlockSpec((dil, per, w), lambda i: (0, i, 0)),
        out_shape=jax.ShapeDtypeStruct((dil, s // dil, w), a.dtype),
        scratch_shapes=[pltpu.VMEM((ROW_TILE, PANEL), F32)] * panels,
        compiler_params=_params(("parallel",)),
    )(a)
    return out.reshape(s, w)


def _token_order(a, dil, acc, name):
    s, w = a.shape
    per = ROW_TILE // dil
    panels = w // PANEL
    has_acc = acc is not None

    def body(*refs):
        a_ref = refs[0]
        o_ref = refs[2] if has_acc else refs[1]
        tiles = refs[3:] if has_acc else refs[2:]
        for c in range(panels):
            cols = slice(c * PANEL, (c + 1) * PANEL)
            for r in range(dil):
                tiles[c][pl.ds(r, per, stride=dil), :] = a_ref[r, :, cols]
            v = tiles[c][...]
            if has_acc:
                v = v + refs[1][:, cols]
            o_ref[:, cols] = v

    row = pl.BlockSpec((ROW_TILE, w), lambda i: (i, 0))
    return pl.pallas_call(
        body, name=name, grid=(s // ROW_TILE,),
        in_specs=[pl.BlockSpec((dil, per, w), lambda i: (0, i, 0))] + ([row] if has_acc else []),
        out_specs=row, out_shape=jax.ShapeDtypeStruct((s, w), F32),
        scratch_shapes=[pltpu.VMEM((ROW_TILE, PANEL), F32)] * panels,
        compiler_params=_params(("parallel",)),
    )(*([a.reshape(dil, s // dil, w)] + ([acc] if has_acc else [])))


def _qkv_proj(n, wqkv, g, name):
    s, d = n.shape
    tm = _pick(s, MATMUL_TILES)

    def body(a_ref, b_ref, o_ref):
        o_ref[...] = _dot(a_ref[...], b_ref[...]).astype(BF16)

    return pl.pallas_call(
        body, name=name, grid=(s // tm, 3),
        in_specs=[pl.BlockSpec((tm, d), lambda i, t: (i, 0)),
                  pl.BlockSpec((d, ATTN_WIDTH), lambda i, t: (0, 3 * g + t))],
        out_specs=pl.BlockSpec((None, tm, ATTN_WIDTH), lambda i, t: (t, i, 0)),
        out_shape=jax.ShapeDtypeStruct((3, s, ATTN_WIDTH), BF16),
        compiler_params=_params(("parallel", "parallel")),
    )(n, wqkv)


def _attn_window(n, ln):
    if ln == BLOCK:
        return 0, BLOCK
    return pl.multiple_of(jnp.maximum(n - 1, 0) * BLOCK, BLOCK), 2 * BLOCK


def _attn_mask(n, k0, kw):
    qpos = n * BLOCK + lax.broadcasted_iota(jnp.int32, (BLOCK, kw), 0)
    kpos = k0 + lax.broadcasted_iota(jnp.int32, (BLOCK, kw), 1)
    dist = qpos - kpos
    return dist.astype(F32), (dist >= 0) & (dist <= BLOCK)


def _attn_scores(q, keys, slope, dist, valid):
    s = _dot(q, keys, NT) * (HEAD_DIM ** -0.5) - slope * dist
    return jnp.where(valid, s, NEG_INF)


ATTN_STEP_BLOCKS = 1
ATTN_BWD_STEP_BLOCKS = 4


def _attn_block(gb, ln):
    nb = ln // BLOCK
    n, base = (0, gb * ln) if nb == 1 else (gb % nb, (gb // nb) * ln)
    k0, kw = _attn_window(n, ln)
    cur = pl.ds(pl.multiple_of(gb * BLOCK, BLOCK), BLOCK)
    win = pl.ds(pl.multiple_of(base + k0, BLOCK), kw)
    return cur, win, n, k0, kw


def _attn_fwd(qkv, g, name):
    _, s, w = qkv.shape
    dil = DILATED_CFG[g][1]
    ln = s // dil
    slopes = _alibi_slopes(g, dil)
    rows = ATTN_STEP_BLOCKS * BLOCK

    def body(qkv_ref, o_ref):
        o_ref[:, w:] = jnp.zeros((rows, PANEL), F32)
        for b in range(ATTN_STEP_BLOCKS):
            cur, win, n, k0, kw = _attn_block(pl.program_id(0) * ATTN_STEP_BLOCKS + b, ln)
            dist, valid = _attn_mask(n, k0, kw)
            out = slice(b * BLOCK, (b + 1) * BLOCK)
            for h in range(HEADS):
                cols = slice(h * HEAD_DIM, (h + 1) * HEAD_DIM)
                sc = _attn_scores(qkv_ref[0, cur, cols], qkv_ref[1, win, cols], slopes[h], dist, valid)
                m = jnp.max(sc, axis=-1, keepdims=True)
                p = jnp.exp(sc - m)
                den = jnp.sum(p, axis=-1, keepdims=True)
                o_ref[out, cols] = _dot(p.astype(BF16), qkv_ref[2, win, cols]) / den
                o_ref[out, w + h:w + h + 1] = m + jnp.log(den)

    return pl.pallas_call(
        body, name=name, grid=(s // rows,),
        in_specs=[pl.BlockSpec((3, s, w), lambda i: (0, 0, 0))],
        out_specs=pl.BlockSpec((rows, ATTN_EXT), lambda i: (i, 0)),
        out_shape=jax.ShapeDtypeStruct((s, ATTN_EXT), F32),
        compiler_params=_params(("parallel",)),
    )(qkv)


def _attn_bwd(qkv, dext, g, name, dep=None):
    _, s, w = qkv.shape
    dil = DILATED_CFG[g][1]
    ln = s // dil
    slopes = _alibi_slopes(g, dil)
    scale = HEAD_DIM ** -0.5
    rows = ATTN_BWD_STEP_BLOCKS * BLOCK
    steps = s // rows
    deps = [] if dep is None else [dep]

    def body(qkv_ref, de_ref, *rest):
        d_ref, dk_ref, dv_ref = rest[-3:]

        @pl.when(pl.program_id(0) == 0)
        def _():
            dk_ref[...] = jnp.zeros_like(dk_ref)
            dv_ref[...] = jnp.zeros_like(dv_ref)

        for b in range(ATTN_BWD_STEP_BLOCKS):
            cur, win, n, k0, kw = _attn_block(pl.program_id(0) * ATTN_BWD_STEP_BLOCKS + b, ln)
            dist, valid = _attn_mask(n, k0, kw)
            blk = slice(b * BLOCK, (b + 1) * BLOCK)
            for h in range(HEADS):
                cols = slice(h * HEAD_DIM, (h + 1) * HEAD_DIM)
                q, keys = qkv_ref[0, cur, cols], qkv_ref[1, win, cols]
                dob = de_ref[blk, cols].astype(BF16)
                p = jnp.exp(_attn_scores(q, keys, slopes[h], dist, valid) - de_ref[blk, w + h:w + h + 1])
                dd = de_ref[blk, w + DVEC_LANE + h:w + DVEC_LANE + h + 1]
                ds = (p * (_dot(dob, qkv_ref[2, win, cols], NT) - dd)).astype(BF16)
                d_ref[0, cur, cols] = (scale * _dot(ds, keys)).astype(BF16)
                dv_ref[win, cols] += _dot(p.astype(BF16), dob, TN)
                dk_ref[win, cols] += scale * _dot(ds, q, TN)

        @pl.when(pl.program_id(0) == steps - 1)
        def _():
            d_ref[1] = dk_ref[...].astype(BF16)
            d_ref[2] = dv_ref[...].astype(BF16)

    whole = pl.BlockSpec((3, s, w), lambda i: (0, 0, 0))
    return pl.pallas_call(
        body, name=name, grid=(steps,),
        in_specs=[whole, pl.BlockSpec((rows, ATTN_EXT), lambda i: (i, 0))] + [_ANY] * len(deps),
        out_specs=whole, out_shape=jax.ShapeDtypeStruct((3, s, w), BF16),
        scratch_shapes=[pltpu.VMEM((s, w), F32), pltpu.VMEM((s, w), F32)],
        compiler_params=_params(("arbitrary",)),
    )(qkv, dext, *deps)


def _attn_merge(e0, e1, e2, name):
    s = e0.shape[0]
    w = ATTN_WIDTH

    def body(e0_ref, e1_ref, e2_ref, m_ref, mb_ref, lse_ref):
        refs = (e0_ref, e1_ref, e2_ref)
        l = [r[:, w:w + HEADS] for r in refs]
        mx = jnp.maximum(jnp.maximum(l[0], l[1]), l[2])
        e = [jnp.exp(v - mx) for v in l]
        z = e[0] + e[1] + e[2]
        lse_ref[...] = mx + jnp.log(z)
        wts = [v / z for v in e]
        for h in range(HEADS):
            cols = slice(h * HEAD_DIM, (h + 1) * HEAD_DIM)
            acc = wts[0][:, h:h + 1] * refs[0][:, cols]
            for g in range(1, N_GROUPS_A):
                acc = acc + wts[g][:, h:h + 1] * refs[g][:, cols]
            m_ref[:, cols] = acc
            mb_ref[:, cols] = acc.astype(BF16)

    ext = pl.BlockSpec((ROW_TILE, ATTN_EXT), lambda i: (i, 0))
    row = pl.BlockSpec((ROW_TILE, w), lambda i: (i, 0))
    return pl.pallas_call(
        body, name=name, grid=(s // ROW_TILE,),
        in_specs=[ext, ext, ext],
        out_specs=[row, row, pl.BlockSpec((ROW_TILE, HEADS), lambda i: (i, 0))],
        out_shape=[jax.ShapeDtypeStruct((s, w), F32), jax.ShapeDtypeStruct((s, w), BF16),
                   jax.ShapeDtypeStruct((s, HEADS), F32)],
        compiler_params=_params(("parallel",)),
    )(e0, e1, e2)


def _attn_dvec(dmerged, merged, lse_all, name, dep=None):
    s, w = merged.shape
    deps = [] if dep is None else [dep]

    def body(dm_ref, m_ref, lse_ref, *rest):
        de_ref = rest[-1]
        dmv = dm_ref[...]
        de_ref[:, :w] = dmv
        de_ref[:, w:] = jnp.zeros((ROW_TILE, PANEL), F32)
        de_ref[:, w:w + HEADS] = lse_ref[...]
        prod = dmv * m_ref[...]
        for h in range(HEADS):
            lane = w + DVEC_LANE + h
            de_ref[:, lane:lane + 1] = jnp.sum(prod[:, h * HEAD_DIM:(h + 1) * HEAD_DIM], axis=-1, keepdims=True)

    row = pl.BlockSpec((ROW_TILE, w), lambda i: (i, 0))
    return pl.pallas_call(
        body, name=name, grid=(s // ROW_TILE,),
        in_specs=[row, row, pl.BlockSpec((ROW_TILE, HEADS), lambda i: (i, 0))] + [_ANY] * len(deps),
        out_specs=pl.BlockSpec((ROW_TILE, ATTN_EXT), lambda i: (i, 0)),
        out_shape=jax.ShapeDtypeStruct((s, ATTN_EXT), F32),
        compiler_params=_params(("parallel",)),
    )(dmerged, merged, lse_all, *deps)


def _attention_fwd(n, wqkv, wo, tag):
    ns, qkvs, exts = [], [], []
    for g, (_, dil) in enumerate(DILATED_CFG):
        ng = n if dil == 1 else _residue_order(n, dil, f"{tag}_order_g{g}")
        qkv = _qkv_proj(ng, wqkv, g, f"{tag}_qkv_g{g}")
        ext = _attn_fwd(qkv, g, f"{tag}_fwd_g{g}")
        ns.append(ng)
        qkvs.append(qkv)
        exts.append(ext if dil == 1 else _token_order(ext, dil, None, f"{tag}_unorder_g{g}"))
    merged, merged_bf, lse_all = _attn_merge(*exts, f"{tag}_merge")
    m = _matmul(merged_bf, wo, "nn", F32, f"{tag}_wo")
    return m, (ns, qkvs, merged, merged_bf, lse_all)


def _attention_bwd(dm, wqkv, wo, saved, tag, dep=None, hook=None):
    ns, qkvs, merged, merged_bf, lse_all = saved
    d_wo = _matmul(merged_bf, dm, "tn", BF16, f"{tag}_dwo")
    dmerged = _matmul(dm, wo, "nt", F32, f"{tag}_dmerged")
    dext = _attn_dvec(dmerged, merged, lse_all, f"{tag}_dvec", dep)
    width = 3 * ATTN_WIDTH
    d_wqkv, dn, dep = [], None, None
    for g, (_, dil) in enumerate(DILATED_CFG):
        dext_g = dext if dil == 1 else _residue_order(dext, dil, f"{tag}_dorder_g{g}")
        dqkv = _attn_bwd(qkvs[g], dext_g, g, f"{tag}_bwd_g{g}", dep)
        dep = hook(g, dqkv) if hook is not None and g + 1 < N_GROUPS_A else None
        d_wqkv.append(_matmul(ns[g], dqkv, "tn", BF16, f"{tag}_dwqkv_g{g}", b_parts=3))
        dn_g = _matmul(dqkv, wqkv[:, g * width:(g + 1) * width], "nt", F32, f"{tag}_dn_g{g}", a_parts=3)
        dn = dn_g if dil == 1 else _token_order(dn_g, dil, dn, f"{tag}_dn_sum_g{g}")
    return dn, jnp.concatenate(d_wqkv, axis=1), d_wo


def _layer_matrices(i):
    mixer = (("attn_w_qkv", "attn_w_o"), ("conv_w_in", "conv_w_out"), ("pool_w_in", "pool_w_grp", "pool_w_out"))[i % 3]
    return [(k, i // 3) for k in mixer] + [("ffn_w_up", i), ("ffn_w_down", i)]


def _local_step(x, tgt, vec, weights, sink):
    ng = vec["norm_g"]

    def gain(i, j, token=None):
        g = ng[i, j][None, :]
        return g if token is None else g + token

    saved = []
    n = _rms_fwd(x, gain(0, 0), None, BF16, "norm_first")
    for i in range(DEPTH):
        wl = weights.layer(i)
        t0 = weights.hook(i, 0, n)
        kind, idx = i % 3, i // 3
        if kind == 0:
            m, ms = _attention_fwd(n, wl["attn_w_qkv"], wl["attn_w_o"], "attn")
        elif kind == 1:
            taps = vec["conv_w_dw"][idx] if t0 is None else vec["conv_w_dw"][idx] + t0
            z, y = _sconv_fwd(n, wl["conv_w_in"], taps, "sconv_fwd")
            m = _matmul(y, wl["conv_w_out"], "nn", F32, "sconv_out")
            ms = (z, y)
        else:
            scale = vec["pool_scale"][idx][None, :] if t0 is None else vec["pool_scale"][idx][None, :] + t0
            p, y = _pool_fwd(n, wl["pool_w_in"], wl["pool_w_grp"], scale, "pool_fwd")
            m = _matmul(y, wl["pool_w_out"], "nn", F32, "pool_out")
            ms = (p, y)
        t1 = weights.hook(i, 1, m)
        x1, n2 = _rms_res_pre(m, gain(i, 1, t0), x, gain(i, 2, t1), "norm_res_pre")
        h, c, a = _ffn_up(n2, wl["ffn_w_up"], vec["ffn_w_dw"][i], "ffn_up")
        t2 = weights.hook(i, 2, a)
        f = _matmul(a, wl["ffn_w_down"].reshape(D_FF, D_MODEL), "nn", F32, "ffn_down", a_parts=FFN_PAIRS)
        saved.append((x, n, m, ms, x1, n2, h, a, f, wl, c))
        if i + 1 < DEPTH:
            x, n = _rms_res_pre(f, gain(i, 3, t2), x1, gain(i + 1, 0), "norm_res_pre")
        else:
            x = _rms_fwd(f, gain(i, 3), x1, F32, "norm_res")
        weights.hook(i, 3, x)

    loss, dx = _loss_head(x, tgt, "loss_head")

    g_norm = [[None] * 4 for _ in range(DEPTH)]
    g_taps, g_scale, g_ffn_dw = [], [], [None] * DEPTH
    df, g_norm[DEPTH - 1][3] = _rms_bwd(saved[-1][8], gain(DEPTH - 1, 3), dx, None, BF16, "norm_bwd_sub")
    t0 = None
    for i in reversed(range(DEPTH)):
        xin, n, m, ms, x1, n2, h, a, f, wl, c = saved[i]
        kind, idx = i % 3, i // 3
        gl = {}
        d_wdown = _matmul(a, df, "tn", BF16, "ffn_dwdown", a_parts=FFN_PAIRS)
        gl["ffn_w_down"] = d_wdown.reshape(N_DEV, D_FF // N_DEV, D_MODEL)
        ffn_taps = vec["ffn_w_dw"][i] if t0 is None else vec["ffn_w_dw"][i] + t0
        dh, dwg, dwu = _ffn_mid_bwd(df, wl["ffn_w_down"].reshape(FFN_PAIRS, -1, D_MODEL), h, c, ffn_taps, "ffn_mid_bwd")
        g_ffn_dw[i] = jnp.concatenate([dwg, dwu], axis=0)
        t1 = sink.hook(i, 1, dh)
        gl["ffn_w_up"] = _ffn_dwup(n2, dh, "ffn_dwup")
        tf = sink.ffn_done(i, gl)
        dn2 = _ffn_dn(dh, wl["ffn_w_up"], "ffn_dn", t1)
        dx1, dm, g_norm[i][2], g_norm[i][1] = _rms_bwd_pair(x1, gain(i, 2, tf), dn2, dx, m, gain(i, 1), "norm_bwd_pair")
        t2 = sink.hook(i, 2, dm)
        if kind == 0:
            dn, gl["attn_w_qkv"], gl["attn_w_o"] = _attention_bwd(
                dm, wl["attn_w_qkv"], wl["attn_w_o"], ms, "attn", t2, lambda g, after, i=i: sink.hook(i, ("a", "b")[g], after))
        elif kind == 1:
            z, y = ms
            gl["conv_w_out"] = _matmul(y, dm, "tn", BF16, "sconv_dwout")
            taps = vec["conv_w_dw"][idx] if t2 is None else vec["conv_w_dw"][idx] + t2
            dz, ddw = _sconv_mid_bwd(dm, wl["conv_w_out"], z, taps, "sconv_mid_bwd")
            g_taps.append(ddw)
            gl["conv_w_in"] = _matmul(n, dz, "tn", BF16, "sconv_dwin", b_parts=3)
            dn = _matmul(dz, wl["conv_w_in"], "nt", F32, "sconv_dn", a_parts=3)
        else:
            p, y = ms
            gl["pool_w_out"] = _matmul(y, dm, "tn", BF16, "pool_dwout")
            scale = vec["pool_scale"][idx][None, :] if t2 is None else vec["pool_scale"][idx][None, :] + t2
            du, gl["pool_w_grp"], dscale = _pool_mid_bwd(dm, wl["pool_w_out"], p, wl["pool_w_grp"], scale, "pool_mid_bwd")
            g_scale.append(dscale[0])
            gl["pool_w_in"] = _matmul(n, du, "tn", BF16, "pool_dwin")
            dn = _matmul(du, wl["pool_w_in"], "nt", F32, "pool_dn")
        sink.hook(i, 3, dn)
        if i > 0:
            dx, df, g_norm[i][0], g_norm[i - 1][3] = _rms_bwd_pair(xin, gain(i, 0, t2), dn, dx1, saved[i - 1][8],
                                                                   gain(i - 1, 3), "norm_bwd_pair")
        else:
            dx, g_norm[0][0] = _rms_bwd(xin, gain(0, 0), dn, dx1, F32, "norm_bwd_res")
        t0 = sink.layer_done(i, gl)

    vec_grads = {"norm_g": jnp.stack([jnp.concatenate(row, axis=0) for row in g_norm]), "conv_w_dw": jnp.stack(g_taps),
                 "pool_scale": jnp.stack(g_scale), "ffn_w_dw": g_ffn_dw}
    return loss, dx, vec_grads


_AXES = ("x", "y", "c")
ROUTE_A = ("y", "x", "c")
ROUTE_B = ("x", "y", "c")
def _dev_index(pos):
    return 4 * pos["x"] + 2 * pos["y"] + pos["c"]


_HBM = pl.BlockSpec(memory_space=pltpu.HBM)
_SEM = pl.BlockSpec(memory_space=pltpu.SEMAPHORE)
_ANY = pl.BlockSpec(memory_space=pl.ANY)
_EFFECT = pltpu.SideEffectType.DATAFLOW_SIDE_EFFECTING


TOKEN_SHAPE = (1, D_MODEL)


def _copies_start(describe, arrays, n_copies, name, after, token_shape=TOKEN_SHAPE):
    n = len(arrays)
    deps = [] if after is None else [after]

    def body(*refs):
        send_sems, recv_sems = refs[n + len(deps)], refs[n + len(deps) + 1]
        for c in describe(refs[:n], send_sems, recv_sems):
            c.start()
        refs[-1][...] = jnp.zeros_like(refs[-1])

    outs = pl.pallas_call(
        body, name=f"{name}_start",
        out_shape=(pltpu.SemaphoreType.DMA((n_copies,)), pltpu.SemaphoreType.DMA((n_copies,)),
                   *[pltpu.HBM(a.shape, a.dtype) for a in arrays], jax.ShapeDtypeStruct(token_shape, F32)),
        in_specs=[_HBM] * n + [_ANY] * len(deps),
        out_specs=(_SEM, _SEM, *([_HBM] * n), pl.BlockSpec(memory_space=pltpu.VMEM)),
        input_output_aliases={i: 2 + i for i in range(n)},
        compiler_params=pltpu.CompilerParams(has_side_effects=_EFFECT),
    )(*[pltpu.with_memory_space_constraint(a, pltpu.HBM) for a in arrays], *deps)
    return (outs[0], outs[1], list(outs[2:2 + n])), outs[-1]


def _copies_wait(describe, handle, name, after):
    send_sems, recv_sems, arrays = handle
    n = len(arrays)
    deps = [] if after is None else [after]

    def body(*refs):
        for c in describe(refs[:n], refs[n], refs[n + 1]):
            c.wait_send()
            c.wait_recv()

    outs = pl.pallas_call(
        body, name=f"{name}_wait",
        out_shape=tuple(pltpu.HBM(a.shape, a.dtype) for a in arrays),
        in_specs=[_HBM] * n + [_SEM, _SEM] + [_ANY] * len(deps), out_specs=tuple([_HBM] * n),
        input_output_aliases={i: i for i in range(n)},
        compiler_params=pltpu.CompilerParams(has_side_effects=_EFFECT),
    )(*arrays, send_sems, recv_sems, *deps)
    return list(outs)


GATHER_STAGE_COPIES = (3, 3, 1)


def _gather_copies(stage, routes):
    n = len(routes)

    def describe(refs, send_sems, recv_sems):
        pos = {a: lax.axis_index(a) for a in _AXES}

        def flipped(axes):
            return {a: 1 - pos[a] if a in axes else pos[a] for a in _AXES}

        copies = []
        for i, (a1, a2, a3) in enumerate(routes):
            land = refs[n + i] if stage == 1 else refs[i]
            p1, p2, p12, p3 = flipped((a1,)), flipped((a2,)), flipped((a1, a2)), flipped((a3,))
            plan = {1: [(None, p1), (None, p2), (None, p3)], 2: [(p1, p2), (p1, p3), (p2, p3)], 3: [(p12, p3)]}[stage]
            for holder, to in plan:
                slot = land.at[_dev_index(pos if holder is None else holder)]
                k = len(copies)
                copies.append(pltpu.make_async_remote_copy(
                    src_ref=refs[i] if holder is None else slot, dst_ref=slot,
                    send_sem=send_sems.at[k], recv_sem=recv_sems.at[k],
                    device_id=tuple(to[a] for a in _AXES), device_id_type=pl.DeviceIdType.MESH))
        return copies

    return describe


def _gather_begin(shards, routes, name, after):
    n = len(shards)
    lands = [lax.empty((N_DEV,) + a.shape, a.dtype) for a in shards]
    handle, token = _copies_start(_gather_copies(1, routes), list(shards) + lands, GATHER_STAGE_COPIES[0] * n,
                                  f"{name}_1", after)
    return {"stage": 1, "handle": handle, "routes": routes, "name": name, "n": n}, token


def _gather_next(state, after):
    stage, routes, name, n = state["stage"], state["routes"], state["name"], state["n"]
    arrays = _copies_wait(_gather_copies(stage, routes), state["handle"], f"{name}_{stage}", after)
    if stage == 1:
        state = dict(state, shards=arrays[:n])
        arrays = arrays[n:]
    if stage == 3:
        me = _dev_index({a: lax.axis_index(a) for a in _AXES})
        return [lax.dynamic_update_index_in_dim(o, s, me, 0) for o, s in zip(arrays, state["shards"])], None
    handle, token = _copies_start(_gather_copies(stage + 1, routes), arrays, GATHER_STAGE_COPIES[stage] * n,
                                  f"{name}_{stage + 1}", None)
    return dict(state, stage=stage + 1, handle=handle), token


ADD_ROW_TILES = (1024, 704, 512, 352, 256, 128, 96, 64, 32, 16)


def _add_half(a, recv, me, out_dtype, name):
    p, q, cols = recv.shape
    tr = _pick(q, ADD_ROW_TILES)

    def body(me_ref, a_ref, b_ref, o_ref):
        o_ref[...] = (a_ref[...].astype(F32) + b_ref[...].astype(F32)).astype(o_ref.dtype)

    return pl.pallas_call(
        body, name=name,
        grid_spec=pltpu.PrefetchScalarGridSpec(
            num_scalar_prefetch=1, grid=(p, q // tr),
            in_specs=[pl.BlockSpec((None, None, tr, cols), lambda j, i, m: (j, m[0], i, 0)),
                      pl.BlockSpec((None, tr, cols), lambda j, i, m: (j, i, 0))],
            out_specs=pl.BlockSpec((None, tr, cols), lambda j, i, m: (j, i, 0))),
        out_shape=jax.ShapeDtypeStruct((p, q, cols), out_dtype),
        compiler_params=_params(("parallel", "parallel")),
    )(me, a, recv)


def _half_copies(axes):
    n = len(axes)

    def describe(refs, send_sems, recv_sems):
        pos = {a: lax.axis_index(a) for a in _AXES}
        copies = []
        for i, axis in enumerate(axes):
            peer = tuple(1 - pos[a] if a == axis else pos[a] for a in _AXES)
            copies.append(pltpu.make_async_remote_copy(
                src_ref=refs[i].at[:, 1 - pos[axis]], dst_ref=refs[n + i], send_sem=send_sems.at[i],
                recv_sem=recv_sems.at[i], device_id=peer, device_id_type=pl.DeviceIdType.MESH))
        return copies

    return describe


def _scatter_begin(slots, routes, tags, name, token_shape=TOKEN_SHAPE):
    shapes = [a.shape[1:] for a in slots]
    rows = [math.prod(s[:-1]) for s in shapes]
    arrays = [a.reshape(4, 2, n, s[-1]) for a, n, s in zip(slots, rows, shapes)]
    return _scatter_start({"stage": 0, "arrays": arrays, "routes": routes, "tags": tags, "name": name,
                           "shapes": shapes, "rows": rows}, token_shape)


def _scatter_start(state, token_shape=TOKEN_SHAPE):
    stage, arrays = state["stage"], state["arrays"]
    axes = [r[2 - stage] for r in state["routes"]]
    lands = [lax.empty((a.shape[0],) + a.shape[2:], a.dtype) for a in arrays]
    handle, token = _copies_start(_half_copies(axes), arrays + lands, len(arrays), f"{state['name']}_{stage + 1}", None,
                                  token_shape)
    return dict(state, handle=handle, axes=axes), token


def _scatter_next(state, after):
    stage, axes, n = state["stage"], state["axes"], len(state["arrays"])
    both = _copies_wait(_half_copies(axes), state["handle"], f"{state['name']}_{stage + 1}", after)
    coord = {a: lax.axis_index(a).astype(jnp.int32).reshape(1) for a in _AXES}
    sums = [_add_half(a, r, coord[ax], F32 if stage == 2 else BF16, f"scatter_add_{stage + 1}_{t}")
            for a, r, ax, t in zip(both[:n], both[n:], axes, state["tags"])]
    if stage == 2:
        return [a.reshape(s) for a, s in zip(sums, state["shapes"])], None
    if stage == 0:
        views = [(1, 2, 2 * r, s[-1]) if route[1] == "x" else (2, 2, r, s[-1])
                 for r, s, route in zip(state["rows"], state["shapes"], state["routes"])]
    else:
        views = [(1, 2, r, s[-1]) for r, s in zip(state["rows"], state["shapes"])]
    return _scatter_start(dict(state, stage=stage + 1, arrays=[a.reshape(v) for a, v in zip(sums, views)]))


_WEIGHTS = {
    "norm_g": ((DEPTH, 4, D_MODEL), 2, True),
    "attn_w_qkv": ((2, D_MODEL, 4608), 2, False),
    "attn_w_o": ((2, ATTN_WIDTH, D_MODEL), 2, False),
    "conv_w_in": ((1, D_MODEL, 3 * D_MODEL), 2, False),
    "conv_w_dw": ((1, 3, D_MODEL), 2, True),
    "conv_w_out": ((1, D_MODEL, D_MODEL), 1, False),
    "pool_w_in": ((1, D_MODEL, D_MODEL), 1, False),
    "pool_w_grp": ((1, 4, POOL_GROUP_DIM, POOL_GROUP_DIM), 2, False),
    "pool_scale": ((1, D_MODEL), 1, True),
    "pool_w_out": ((1, D_MODEL, D_MODEL), 1, False),
    "ffn_w_up": ((DEPTH, D_MODEL, 2 * D_FF), 2, False),
    "ffn_w_dw": ((DEPTH, 3, 2 * D_FF), 2, True),
    "ffn_w_down": ((DEPTH, D_FF, D_MODEL), 1, False),
}
_NAMES = tuple(_WEIGHTS)
_VECTORS = tuple(k for k in _NAMES if _WEIGHTS[k][2])
_MATRICES = tuple(k for k in _NAMES if not _WEIGHTS[k][2])
_FFN = ("ffn_w_up", "ffn_w_down")
_ON_ROUTE_A = ("ffn_w_up", "attn_w_o", "conv_w_out", "pool_w_in")
PACK_ROWS = 16


def _route(name):
    return ROUTE_A if name in _ON_ROUTE_A else ROUTE_B


def _shard_shape(name):
    shape, ax, _ = _WEIGHTS[name]
    return tuple(s // N_DEV if i == ax else s for i, s in enumerate(shape))


def _full_from_slots(slots, name, layers=None):
    shape, ax, _ = _WEIGHTS[name]
    if layers is not None:
        shape = (layers,) + shape[1:]
    return jnp.moveaxis(slots, 0, ax).reshape(shape)


def _slots_from_full(full, name):
    shape, ax, _ = _WEIGHTS[name]
    split = shape[:ax] + (N_DEV, shape[ax] // N_DEV) + shape[ax + 1:]
    return jnp.moveaxis(full.reshape(split), ax, 0)


def _pack_vectors(parts, lead):
    rows = []
    for k in _VECTORS:
        r = parts[k].reshape(lead + (-1, LANES))
        pad = -r.shape[-2] % PACK_ROWS
        rows.append(jnp.pad(r, [(0, 0)] * len(lead) + [(0, pad), (0, 0)]))
    return jnp.concatenate(rows, axis=len(lead))


def _unpack_vectors(buf, lead):
    out, r0 = {}, 0
    for k in _VECTORS:
        shard = _shard_shape(k)
        rows = math.prod(shard) // LANES
        out[k] = buf[..., r0:r0 + rows, :].reshape(lead + shard)
        r0 += rows + (-rows % PACK_ROWS)
    return out


class _LayerWeights:
    def __init__(self, shards):
        self.cast = {k: shards[k].astype(BF16) for k in _MATRICES}
        first, ffn0 = _layer_matrices(0)[:-2], _layer_matrices(0)[-2:]
        state, _ = _gather_begin(self._send(first) + [_pack_vectors(shards, ())],
                                 [_route(k) for k, _ in first] + [ROUTE_B], "gather0", None)
        for _ in range(2):
            state, _ = _gather_next(state, None)
        outs, _ = _gather_next(state, None)
        vec = _unpack_vectors(outs[-1], (N_DEV,))
        self.vec = {k: _full_from_slots(vec[k], k) for k in _VECTORS}
        self.vec["ffn_w_dw"] = [vec["ffn_w_dw"][:, l] for l in range(DEPTH)]
        self.ready = {0: self._unpack(first, outs[:-1])}
        self.chains = {}
        tokens = []
        self._begin("ffn0", ffn0, "gather0f", outs[0], tokens)
        self._begin(1, _layer_matrices(1), "gather1", outs[0], tokens)
        self.vec["norm_g"] = self.vec["norm_g"] + (tokens[0] + tokens[1])

    def _send(self, items):
        return [self.cast[k][j] for k, j in items]

    @staticmethod
    def _unpack(items, outs):
        return {k: o if k in _FFN else _full_from_slots(o[:, None], k, layers=1)[0] for (k, _), o in zip(items, outs)}

    def _begin(self, key, items, name, after, tokens):
        state, token = _gather_begin(self._send(items), [_route(k) for k, _ in items], name, after)
        self.chains[key] = (items, state)
        tokens.append(token)

    def _advance(self, key, after, tokens):
        items, state = self.chains.pop(key)
        state, token = _gather_next(state, after)
        if token is None:
            self.ready.setdefault(0 if key == "ffn0" else key, {}).update(self._unpack(items, state))
        else:
            self.chains[key] = (items, state)
            tokens.append(token)

    def layer(self, i):
        return self.ready[i]

    def hook(self, i, point, after):
        tokens = []
        if i == 0 and point == 0:
            self._advance("ffn0", after, tokens)
        if i == 0 and point == 1:
            self._advance("ffn0", after, tokens)
            self._advance("ffn0", None, tokens)
        if point >= 1 and i + 1 in self.chains:
            self._advance(i + 1, after, tokens)
        if point == 1 and i + 2 < DEPTH:
            self._begin(i + 2, _layer_matrices(i + 2), f"gather{i + 2}", after, tokens)
        return functools.reduce(lambda a, b: a + b, tokens) if tokens else None


def _layer_slots(g, name):
    shape, ax, _ = _WEIGHTS[name]
    shape, ax = shape[1:], ax - 1
    split = shape[:ax] + (N_DEV, shape[ax] // N_DEV) + shape[ax + 1:]
    return jnp.moveaxis(g.reshape(split), ax, 0).astype(BF16)


class _GradSink:
    def __init__(self):
        self.state = None
        self.ffn_state = None
        self.sums = {}
        self.last = None

    def ffn_done(self, i, grads):
        if i != 0:
            return None
        self.ffn_items = _layer_matrices(0)[-2:]
        self.ffn_state, token = _scatter_begin([grads[k] for k, _ in self.ffn_items],
                                               [_route(k) for k, _ in self.ffn_items],
                                               [f"{k}{j}" for k, j in self.ffn_items], "scatter0f")
        return token

    def layer_done(self, i, grads):
        items = _layer_matrices(i)
        if i == 0:
            items = items[:-2]
            self.last = (items, [_layer_slots(grads[k], k) for k, _ in items])
            return None
        slots = [grads[k] if k in _FFN else _layer_slots(grads[k], k) for k, _ in items]
        self.items = items
        self.state, token = _scatter_begin(slots, [_route(k) for k, _ in items], [f"{k}{j}" for k, j in items],
                                           f"scatter{i}", (N_DEV, 3, 2 * D_FF // N_DEV))
        return token

    def hook(self, i, point, after):
        tokens = []
        if self.state is not None and point in (1, 2, 3):
            self.state, token = _scatter_next(self.state, after)
            if point == 3:
                self.sums.update(dict(zip(self.items, self.state)))
                self.state = None
            tokens.append(token)
        if self.ffn_state is not None and point in ("a", "b", 3):
            self.ffn_state, token = _scatter_next(self.ffn_state, after)
            if point == 3:
                self.sums.update(dict(zip(self.ffn_items, self.ffn_state)))
                self.ffn_state = None
            tokens.append(token)
        tokens = [t for t in tokens if t is not None]
        return functools.reduce(lambda a, b: a + b, tokens) if tokens else None


def _adamw(w, g, m, v, name, layer=None, prev=None):
    shape = w.shape
    cols = shape[-1]
    view = shape if len(shape) == 3 else (1, math.prod(shape[:-1]), cols)
    layers, rows, _ = view
    tr = _pick(rows, (512, 256, 128, 64, 32, 16, 8))
    n_prev = 0 if prev is None else 3

    def body(*refs):
        w_ref, g_ref, m_ref, v_ref = refs[n_prev:n_prev + 4]
        d_ref, nm_ref, nv_ref = refs[n_prev + 4:]
        gv = g_ref[...]
        nm = ADAM_B1 * m_ref[...] + (1.0 - ADAM_B1) * gv
        nv = ADAM_B2 * v_ref[...] + (1.0 - ADAM_B2) * jnp.square(gv)
        m_hat = nm / (1.0 - ADAM_B1 ** ADAM_STEP)
        v_hat = nv / (1.0 - ADAM_B2 ** ADAM_STEP)
        d_ref[...] = -ADAM_LR * (m_hat / (jnp.sqrt(v_hat) + ADAM_EPS) + ADAM_WD * w_ref[...])
        nm_ref[...] = nm
        nv_ref[...] = nv

    if layer is None:
        grid = (layers, rows // tr)
        blk = gblk = pl.BlockSpec((None, tr, cols), lambda l, i: (l, i, 0))
        gview = view
    else:
        grid = (rows // tr,)
        blk = pl.BlockSpec((None, tr, cols), lambda i: (layer, i, 0))
        gblk = pl.BlockSpec((tr, cols), lambda i: (i, 0))
        gview = (rows, cols)
    shp = jax.ShapeDtypeStruct(view, F32)
    outs = pl.pallas_call(
        body, name=name, grid=grid, in_specs=[_ANY] * n_prev + [blk, gblk, blk, blk], out_specs=[blk] * 3,
        out_shape=[shp] * 3, input_output_aliases={i: i for i in range(n_prev)},
        compiler_params=_params(("parallel",) * len(grid)),
    )(*([] if prev is None else [p.reshape(view) for p in prev]), w.reshape(view), g.reshape(gview), m.reshape(view),
      v.reshape(view))
    return [o.reshape(shape) for o in outs]


def kernel(x, norm_g, attn_w_qkv, attn_w_o, conv_w_in, conv_w_dw, conv_w_out, pool_w_in, pool_w_grp, pool_scale, pool_w_out, ffn_w_up, ffn_w_dw, ffn_w_down, loss_target, m_norm_g, m_attn_w_qkv, m_attn_w_o, m_conv_w_in, m_conv_w_dw, m_conv_w_out, m_pool_w_in, m_pool_w_grp, m_pool_scale, m_pool_w_out, m_ffn_w_up, m_ffn_w_dw, m_ffn_w_down, v_norm_g, v_attn_w_qkv, v_attn_w_o, v_conv_w_in, v_conv_w_dw, v_conv_w_out, v_pool_w_in, v_pool_w_grp, v_pool_scale, v_pool_w_out, v_ffn_w_up, v_ffn_w_dw, v_ffn_w_down):
    shards = dict(zip(_NAMES, (norm_g, attn_w_qkv, attn_w_o, conv_w_in, conv_w_dw, conv_w_out, pool_w_in,
                               pool_w_grp, pool_scale, pool_w_out, ffn_w_up, ffn_w_dw, ffn_w_down)))
    moms = dict(zip(_NAMES, (m_norm_g, m_attn_w_qkv, m_attn_w_o, m_conv_w_in, m_conv_w_dw, m_conv_w_out,
                             m_pool_w_in, m_pool_w_grp, m_pool_scale, m_pool_w_out, m_ffn_w_up, m_ffn_w_dw,
                             m_ffn_w_down)))
    vels = dict(zip(_NAMES, (v_norm_g, v_attn_w_qkv, v_attn_w_o, v_conv_w_in, v_conv_w_dw, v_conv_w_out,
                             v_pool_w_in, v_pool_w_grp, v_pool_scale, v_pool_w_out, v_ffn_w_up, v_ffn_w_dw,
                             v_ffn_w_down)))
    weights = _LayerWeights(shards)
    sink = _GradSink()
    loss, grad_x, vec_grads = _local_step(x[0], loss_target[0], weights.vec, weights, sink)
    loss = lax.psum(loss[0, 0], _AXES)

    items, slots = sink.last
    vec_slots = {k: _slots_from_full(vec_grads[k], k) for k in _VECTORS if k != "ffn_w_dw"}
    vec_slots["ffn_w_dw"] = jnp.stack(vec_grads["ffn_w_dw"], axis=1)
    state, _ = _scatter_begin(slots + [_pack_vectors(vec_slots, (N_DEV,)).astype(BF16)],
                              [_route(k) for k, _ in items] + [ROUTE_B], [f"{k}{j}" for k, j in items] + ["vectors"],
                              "scatter0")
    results = {}

    def step(matrices):
        last = None
        for k, j in matrices:
            g = sink.sums[(k, j)]
            if _WEIGHTS[k][0][0] == 1:
                results[k] = (g[None], _adamw(shards[k], g[None], moms[k], vels[k], f"adamw_{k}"))
            else:
                gs, prev = results.get(k, ({}, None))
                gs[j] = g
                results[k] = (gs, _adamw(shards[k], g, moms[k], vels[k], f"adamw_{k}{j}", layer=j, prev=prev))
            last = results[k][1][0]
        return last

    state, _ = _scatter_next(state, step(_layer_matrices(3)))
    state, _ = _scatter_next(state, step(_layer_matrices(2) + _layer_matrices(1)))
    sums, _ = _scatter_next(state, step(_layer_matrices(0)[-2:]))
    sink.sums.update(dict(zip(items, sums[:-1])))
    step(items)
    vec_sums = _unpack_vectors(sums[-1], ())
    for k in _VECTORS:
        results[k] = (vec_sums[k], _adamw(shards[k], vec_sums[k], moms[k], vels[k], f"adamw_{k}"))
    grads_out = {k: g if not isinstance(g, dict) else jnp.stack([g[j] for j in range(len(g))])
                 for k, (g, _) in results.items()}
    return (loss, grad_x[None], *[grads_out[k] for k in _NAMES], *[results[k][1][0] for k in _NAMES],
            *[results[k][1][1] for k in _NAMES], *[results[k][1][2] for k in _NAMES])
```

```python
import functools
import math

import numpy as np
import jax
import jax.numpy as jnp
from jax import lax
from jax.experimental import pallas as pl
from jax.experimental.pallas import tpu as pltpu

F32, BF16 = jnp.float32, jnp.bfloat16

D_MODEL = 1024
SEQ = 2048
DEPTH = 4
DILATED_CFG = ((128, 1), (512, 4), (2048, 16))
N_GROUPS_A = 3
HEADS = 8
HEAD_DIM = 64
ATTN_WIDTH = HEADS * HEAD_DIM
N_HEADS_A = N_GROUPS_A * HEADS
BLOCK = 128
NEG_INF = -1e30
POOL_GROUP_DIM = 256
D_FF = 2816
RMS_EPS = 1e-6
ADAM_LR, ADAM_B1, ADAM_B2, ADAM_EPS, ADAM_WD, ADAM_STEP = 0.001, 0.9, 0.999, 1e-08, 0.01, 10

N_DEV = 8
LANES = 128
V7X_VMEM_BYTES = 64 * 2 ** 20
VMEM_LIMIT_BYTES = V7X_VMEM_BYTES - 8 * 2 ** 20
COL_TILE = 256
ROW_TILE = 256
MATMUL_TILES = (1024, 1408, 512, 256, 128)
TN_RESIDENT_K = 2048

NN = (((1,), (0,)), ((), ()))
NT = (((1,), (1,)), ((), ()))
TN = (((0,), (0,)), ((), ()))


def _dot(a, b, dims=NN):
    return lax.dot_general(a, b, dims, preferred_element_type=F32)


def _params(sem=None):
    return pltpu.CompilerParams(dimension_semantics=sem, vmem_limit_bytes=VMEM_LIMIT_BYTES)


def _pick(n, prefs):
    for p in prefs:
        if n % p == 0:
            return p
    return n


def _matmul(a, b, mode, out_dtype, name, a_parts=1, b_parts=1):
    if mode == "nn":
        m, k = a.shape[-2], a.shape[-1] * a_parts
        n = b.shape[-1] * b_parts
    elif mode == "nt":
        m, k = a.shape[-2], a.shape[-1] * a_parts
        n = b.shape[-2]
    else:
        k, m = a.shape[-2], a.shape[-1] * a_parts
        n = b.shape[-1] * b_parts
    tm = _pick(m, MATMUL_TILES)
    tn = _pick(n // b_parts if mode != "nt" else n, MATMUL_TILES)
    kk = k // a_parts if mode != "tn" else k
    tk = _pick(kk, MATMUL_TILES)
    if mode == "tn":
        tm = _pick(m // a_parts, MATMUL_TILES)
        if k <= TN_RESIDENT_K:
            tk = k
    gm, gn, gk = m // tm, n // tn, k // tk

    def a_idx(i, j, kq):
        if mode == "tn":
            r, c, per = kq, i, (m // a_parts) // tm
        else:
            r, c, per = i, kq, (k // a_parts) // tk
        return (r, c) if a_parts == 1 else (c // per, r, c % per)

    def b_idx(i, j, kq):
        if mode == "nt":
            return (j, kq)
        per = (n // b_parts) // tn
        return (kq, j) if b_parts == 1 else (j // per, kq, j % per)

    a_blk = (tk, tm) if mode == "tn" else (tm, tk)
    b_blk = (tn, tk) if mode == "nt" else (tk, tn)
    if a_parts > 1:
        a_blk = (None,) + a_blk
    if b_parts > 1:
        b_blk = (None,) + b_blk
    dims = {"nn": NN, "nt": NT, "tn": TN}[mode]

    def body_single(a_ref, b_ref, o_ref):
        o_ref[...] = _dot(a_ref[...], b_ref[...], dims).astype(o_ref.dtype)

    def body(a_ref, b_ref, o_ref, acc_ref):
        kq = pl.program_id(2)

        @pl.when(kq == 0)
        def _():
            acc_ref[...] = jnp.zeros_like(acc_ref)

        acc_ref[...] += _dot(a_ref[...], b_ref[...], dims)

        @pl.when(kq == gk - 1)
        def _():
            o_ref[...] = acc_ref[...].astype(o_ref.dtype)

    return pl.pallas_call(
        body_single if gk == 1 else body, name=name, grid=(gm, gn, gk),
        in_specs=[pl.BlockSpec(a_blk, a_idx), pl.BlockSpec(b_blk, b_idx)],
        out_specs=pl.BlockSpec((tm, tn), lambda i, j, kq: (i, j)),
        out_shape=jax.ShapeDtypeStruct((m, n), out_dtype),
        scratch_shapes=[] if gk == 1 else [pltpu.VMEM((tm, tn), F32)],
        compiler_params=_params(("parallel", "parallel", "arbitrary")),
    )(a, b)


def _rms_fwd(xin, g, res, out_dtype, name):
    s, d = xin.shape
    has_res = res is not None

    def body(*refs):
        x_ref, g_ref = refs[0], refs[1]
        o_ref = refs[-1]
        x = x_ref[...]
        r = lax.rsqrt(jnp.mean(x * x, axis=-1, keepdims=True) + RMS_EPS)
        y = x * r * g_ref[...]
        if has_res:
            y = refs[2][...] + y
        o_ref[...] = y.astype(o_ref.dtype)

    row = pl.BlockSpec((ROW_TILE, d), lambda i: (i, 0))
    vec = pl.BlockSpec((1, d), lambda i: (0, 0))
    ins = [xin, g] + ([res] if has_res else [])
    return pl.pallas_call(
        body, name=name, grid=(s // ROW_TILE,),
        in_specs=[row, vec] + ([row] if has_res else []),
        out_specs=row, out_shape=jax.ShapeDtypeStruct((s, d), out_dtype),
        compiler_params=_params(("parallel",)),
    )(*ins)


def _rms_bwd(xin, g, dy, dres, out_dtype, name):
    s, d = xin.shape
    has_res = dres is not None

    def body(*refs):
        x_ref, g_ref, dy_ref = refs[0], refs[1], refs[2]
        dx_ref, dg_ref = refs[-2], refs[-1]

        @pl.when(pl.program_id(0) == 0)
        def _():
            dg_ref[...] = jnp.zeros_like(dg_ref)

        x = x_ref[...]
        dyv = dy_ref[...].astype(F32)
        r = lax.rsqrt(jnp.mean(x * x, axis=-1, keepdims=True) + RMS_EPS)
        xhat = x * r
        u = dyv * g_ref[...]
        dx = r * (u - xhat * jnp.mean(u * xhat, axis=-1, keepdims=True))
        if has_res:
            dx = refs[3][...] + dx
        dx_ref[...] = dx.astype(dx_ref.dtype)
        dg_ref[...] += jnp.sum(dyv * xhat, axis=0, keepdims=True)

    row = pl.BlockSpec((ROW_TILE, d), lambda i: (i, 0))
    vec = pl.BlockSpec((1, d), lambda i: (0, 0))
    ins = [xin, g, dy] + ([dres] if has_res else [])
    return pl.pallas_call(
        body, name=name, grid=(s // ROW_TILE,),
        in_specs=[row, vec, row] + ([row] if has_res else []),
        out_specs=[row, vec],
        out_shape=[jax.ShapeDtypeStruct((s, d), out_dtype), jax.ShapeDtypeStruct((1, d), F32)],
        compiler_params=_params(("arbitrary",)),
    )(*ins)


def _rms(x):
    r = lax.rsqrt(jnp.mean(x * x, axis=-1, keepdims=True) + RMS_EPS)
    return r, x * r


def _rms_grad(r, xhat, dy, g):
    u = dy * g
    return r * (u - xhat * jnp.mean(u * xhat, axis=-1, keepdims=True))


def _rms_res_pre(sub, g_post, res, g_pre, name):
    s, d = sub.shape

    def body(sub_ref, gp_ref, res_ref, gn_ref, x_ref, n_ref):
        xnew = res_ref[...] + _rms(sub_ref[...])[1] * gp_ref[...]
        x_ref[...] = xnew
        n_ref[...] = (_rms(xnew)[1] * gn_ref[...]).astype(BF16)

    row = pl.BlockSpec((ROW_TILE, d), lambda i: (i, 0))
    vec = pl.BlockSpec((1, d), lambda i: (0, 0))
    return pl.pallas_call(
        body, name=name, grid=(s // ROW_TILE,),
        in_specs=[row, vec, row, vec], out_specs=[row, row],
        out_shape=[jax.ShapeDtypeStruct((s, d), F32), jax.ShapeDtypeStruct((s, d), BF16)],
        compiler_params=_params(("parallel",)),
    )(sub, g_post, res, g_pre)


def _rms_bwd_pair(xmid, g_pre, dn, dres, sub, g_post, name):
    s, d = xmid.shape

    def body(x_ref, gn_ref, dn_ref, dres_ref, sub_ref, gp_ref, dx_ref, dsub_ref, dgn_ref, dgp_ref):
        @pl.when(pl.program_id(0) == 0)
        def _():
            dgn_ref[...] = jnp.zeros_like(dgn_ref)
            dgp_ref[...] = jnp.zeros_like(dgp_ref)

        dnv = dn_ref[...].astype(F32)
        r, xhat = _rms(x_ref[...])
        dx = dres_ref[...] + _rms_grad(r, xhat, dnv, gn_ref[...])
        dx_ref[...] = dx
        dgn_ref[...] += jnp.sum(dnv * xhat, axis=0, keepdims=True)
        rs, shat = _rms(sub_ref[...])
        dsub_ref[...] = _rms_grad(rs, shat, dx, gp_ref[...]).astype(BF16)
        dgp_ref[...] += jnp.sum(dx * shat, axis=0, keepdims=True)

    row = pl.BlockSpec((ROW_TILE, d), lambda i: (i, 0))
    vec = pl.BlockSpec((1, d), lambda i: (0, 0))
    return pl.pallas_call(
        body, name=name, grid=(s // ROW_TILE,),
        in_specs=[row, vec, row, row, row, vec], out_specs=[row, row, vec, vec],
        out_shape=[jax.ShapeDtypeStruct((s, d), F32), jax.ShapeDtypeStruct((s, d), BF16),
                   jax.ShapeDtypeStruct((1, d), F32), jax.ShapeDtypeStruct((1, d), F32)],
        compiler_params=_params(("arbitrary",)),
    )(xmid, g_pre, dn, dres, sub, g_post)


def _loss_head(y, tgt, name):
    s, d = y.shape

    def body(y_ref, t_ref, l_ref, dy_ref):
        @pl.when(pl.program_id(0) == 0)
        def _():
            l_ref[...] = jnp.zeros_like(l_ref)

        e = y_ref[...] - t_ref[...]
        dy_ref[...] = e / d
        per_tok = jnp.mean(e * e, axis=-1, keepdims=True)
        l_ref[...] += 0.5 * jnp.sum(per_tok, axis=0, keepdims=True)

    row = pl.BlockSpec((ROW_TILE, d), lambda i: (i, 0))
    return pl.pallas_call(
        body, name=name, grid=(s // ROW_TILE,),
        in_specs=[row, row],
        out_specs=[pl.BlockSpec((1, 1), lambda i: (0, 0)), row],
        out_shape=[jax.ShapeDtypeStruct((1, 1), F32), jax.ShapeDtypeStruct((s, d), F32)],
        compiler_params=_params(("arbitrary",)),
    )(y, tgt)


SUBLANES = 8


def _shift_down(x, k):
    t, c = x.shape
    r = pltpu.roll(x.reshape(t // SUBLANES, SUBLANES, c), k, axis=1)
    above = jnp.concatenate([jnp.zeros((1, SUBLANES, c), x.dtype), r[:-1]], axis=0)
    rows = lax.broadcasted_iota(jnp.int32, (1, SUBLANES, c), 1)
    return jnp.where(rows >= k, r, above).reshape(t, c)


def _shift_up(x, k):
    t, c = x.shape
    r = pltpu.roll(x.reshape(t // SUBLANES, SUBLANES, c), SUBLANES - k, axis=1)
    below = jnp.concatenate([r[1:], jnp.zeros((1, SUBLANES, c), x.dtype)], axis=0)
    rows = lax.broadcasted_iota(jnp.int32, (1, SUBLANES, c), 1)
    return jnp.where(rows < SUBLANES - k, r, below).reshape(t, c)


def _conv3(h, w):
    return w[2:3] * h + w[1:2] * _shift_down(h, 1) + w[0:1] * _shift_down(h, 2)


def _conv3_bwd(dc, h, w, dw_ref, cols=slice(None)):
    u1, u2 = _shift_up(dc, 1), _shift_up(dc, 2)
    dw_ref[0:1, cols] = jnp.sum(u2 * h, axis=0, keepdims=True)
    dw_ref[1:2, cols] = jnp.sum(u1 * h, axis=0, keepdims=True)
    dw_ref[2:3, cols] = jnp.sum(dc * h, axis=0, keepdims=True)
    return w[2:3] * dc + w[1:2] * u1 + w[0:1] * u2


FFN_PAIRS = N_DEV // 2


def _lane_chunks(width):
    return [(c0, min(COL_TILE, width - c0)) for c0 in range(0, width, COL_TILE)]


def _ffn_up(n, wup, wdw, name):
    s, d = n.shape
    cw = wup.shape[-1]

    def body(n_ref, wg_ref, wu_ref, dg_ref, du_ref, h_ref, c_ref, a_ref):
        x = n_ref[...]
        for c0, size in _lane_chunks(cw):
            cols = slice(c0, c0 + size)
            hg = _dot(x, wg_ref[:, cols])
            hu = _dot(x, wu_ref[:, cols])
            h_ref[0, :, cols] = hg.astype(BF16)
            h_ref[1, :, cols] = hu.astype(BF16)
            cg = _conv3(hg, dg_ref[:, cols])
            cu = _conv3(hu, du_ref[:, cols])
            c_ref[0, :, cols] = cg.astype(BF16)
            c_ref[1, :, cols] = cu.astype(BF16)
            a_ref[:, cols] = (cg * jax.nn.sigmoid(cg) * cu).astype(BF16)

    return pl.pallas_call(
        body, name=name, grid=(FFN_PAIRS,),
        in_specs=[pl.BlockSpec((s, d), lambda j: (0, 0)),
                  pl.BlockSpec((None, d, cw), lambda j: (j, 0, 0)),
                  pl.BlockSpec((None, d, cw), lambda j: (j + FFN_PAIRS, 0, 0)),
                  pl.BlockSpec((None, 3, cw), lambda j: (j, 0, 0)),
                  pl.BlockSpec((None, 3, cw), lambda j: (j + FFN_PAIRS, 0, 0))],
        out_specs=[pl.BlockSpec((None, 2, s, cw), lambda j: (j, 0, 0, 0)),
                   pl.BlockSpec((None, 2, s, cw), lambda j: (j, 0, 0, 0)),
                   pl.BlockSpec((None, s, cw), lambda j: (j, 0, 0))],
        out_shape=[jax.ShapeDtypeStruct((FFN_PAIRS, 2, s, cw), BF16), jax.ShapeDtypeStruct((FFN_PAIRS, 2, s, cw), BF16),
                   jax.ShapeDtypeStruct((FFN_PAIRS, s, cw), BF16)],
        compiler_params=_params(("parallel",)),
    )(n, wup, wup, wdw, wdw)


def _ffn_mid_bwd(do, wdown, h, c, wdw, name):
    s, d = do.shape
    cw = wdown.shape[1]

    def body(do_ref, wd_ref, h_ref, c_ref, wg_ref, wu_ref, dh_ref, dwg_ref, dwu_ref):
        dov = do_ref[...]
        for c0, size in _lane_chunks(cw):
            cols = slice(c0, c0 + size)
            da = _dot(dov, wd_ref[cols, :], NT)
            hg = h_ref[0, :, cols].astype(F32)
            hu = h_ref[1, :, cols].astype(F32)
            wg, wu = wg_ref[:, cols], wu_ref[:, cols]
            cg = c_ref[0, :, cols].astype(F32)
            cu = c_ref[1, :, cols].astype(F32)
            sg = jax.nn.sigmoid(cg)
            dcu = da * (cg * sg)
            dcg = da * cu * (sg * (1.0 + cg * (1.0 - sg)))
            dh_ref[0, :, cols] = _conv3_bwd(dcg, hg, wg, dwg_ref, cols).astype(BF16)
            dh_ref[1, :, cols] = _conv3_bwd(dcu, hu, wu, dwu_ref, cols).astype(BF16)

    vec = jax.ShapeDtypeStruct((FFN_PAIRS, 3, cw), F32)
    return pl.pallas_call(
        body, name=name, grid=(FFN_PAIRS,),
        in_specs=[pl.BlockSpec((s, d), lambda j: (0, 0)), pl.BlockSpec((None, cw, d), lambda j: (j, 0, 0)),
                  pl.BlockSpec((None, 2, s, cw), lambda j: (j, 0, 0, 0)),
                  pl.BlockSpec((None, 2, s, cw), lambda j: (j, 0, 0, 0)),
                  pl.BlockSpec((None, 3, cw), lambda j: (j, 0, 0)),
                  pl.BlockSpec((None, 3, cw), lambda j: (j + FFN_PAIRS, 0, 0))],
        out_specs=[pl.BlockSpec((None, 2, s, cw), lambda j: (j, 0, 0, 0)),
                   pl.BlockSpec((None, 3, cw), lambda j: (j, 0, 0)), pl.BlockSpec((None, 3, cw), lambda j: (j, 0, 0))],
        out_shape=[jax.ShapeDtypeStruct((FFN_PAIRS, 2, s, cw), BF16), vec, vec],
        compiler_params=_params(("parallel",)),
    )(do, wdown, h, c, wdw, wdw)


def _ffn_dwup(n, dh, name):
    s, d = n.shape
    cw = dh.shape[-1]

    def body(n_ref, dh_ref, o_ref):
        o_ref[...] = _dot(n_ref[...], dh_ref[...], TN).astype(BF16)

    return pl.pallas_call(
        body, name=name, grid=(N_DEV,),
        in_specs=[pl.BlockSpec((s, d), lambda k: (0, 0)),
                  pl.BlockSpec((None, None, s, cw), lambda k: (k % FFN_PAIRS, k // FFN_PAIRS, 0, 0))],
        out_specs=pl.BlockSpec((None, d, cw), lambda k: (k, 0, 0)),
        out_shape=jax.ShapeDtypeStruct((N_DEV, d, cw), BF16),
        compiler_params=_params(("parallel",)),
    )(n, dh)


def _ffn_dn(dh, wup, name, dep=None):
    s, cw = dh.shape[-2:]
    d = wup.shape[1]
    tm = _pick(s, MATMUL_TILES)
    deps = [] if dep is None else [dep]

    def body(dh_ref, w_ref, *rest):
        o_ref, acc_ref = rest[-2:]
        k = pl.program_id(1)

        @pl.when(k == 0)
        def _():
            acc_ref[...] = jnp.zeros_like(acc_ref)

        acc_ref[...] += _dot(dh_ref[...], w_ref[...], NT)

        @pl.when(k == N_DEV - 1)
        def _():
            o_ref[...] = acc_ref[...]

    return pl.pallas_call(
        body, name=name, grid=(s // tm, N_DEV),
        in_specs=[pl.BlockSpec((None, None, tm, cw), lambda i, k: (k % FFN_PAIRS, k // FFN_PAIRS, i, 0)),
                  pl.BlockSpec((None, d, cw), lambda i, k: (k, 0, 0))] + [_ANY] * len(deps),
        out_specs=pl.BlockSpec((tm, d), lambda i, k: (i, 0)),
        out_shape=jax.ShapeDtypeStruct((s, d), F32),
        scratch_shapes=[pltpu.VMEM((tm, d), F32)],
        compiler_params=_params(("parallel", "arbitrary")),
    )(dh, wup, *deps)


def _sconv_fwd(n, win, wdw, name):
    s, d = n.shape
    tn = COL_TILE
    nj = d // tn

    def body(n_ref, wb_ref, wc_ref, wh_ref, dw_ref, z_ref, y_ref):
        x = n_ref[...]
        zb = _dot(x, wb_ref[...])
        zc = _dot(x, wc_ref[...])
        zh = _dot(x, wh_ref[...])
        z_ref[0] = zb.astype(BF16)
        z_ref[1] = zc.astype(BF16)
        z_ref[2] = zh.astype(BF16)
        y_ref[...] = (zb * _conv3(zc * zh, dw_ref[...])).astype(BF16)

    return pl.pallas_call(
        body, name=name, grid=(nj,),
        in_specs=[pl.BlockSpec((s, d), lambda j: (0, 0)),
                  pl.BlockSpec((d, tn), lambda j: (0, j)), pl.BlockSpec((d, tn), lambda j: (0, j + nj)),
                  pl.BlockSpec((d, tn), lambda j: (0, j + 2 * nj)), pl.BlockSpec((3, tn), lambda j: (0, j))],
        out_specs=[pl.BlockSpec((3, s, tn), lambda j: (0, 0, j)), pl.BlockSpec((s, tn), lambda j: (0, j))],
        out_shape=[jax.ShapeDtypeStruct((3, s, d), BF16), jax.ShapeDtypeStruct((s, d), BF16)],
        compiler_params=_params(("parallel",)),
    )(n, win, win, win, wdw)


def _sconv_mid_bwd(dm, wout, z, wdw, name):
    s, d = dm.shape
    tn = COL_TILE
    nj = d // tn

    def body(dm_ref, wo_ref, z_ref, w_ref, dz_ref, dw_ref):
        dy = _dot(dm_ref[...], wo_ref[...], NT)
        zb = z_ref[0].astype(F32)
        zc = z_ref[1].astype(F32)
        zh = z_ref[2].astype(F32)
        w = w_ref[...]
        p = zc * zh
        cp = _conv3(p, w)
        dz_ref[0] = (dy * cp).astype(BF16)
        dcp = dy * zb
        dp = _conv3_bwd(dcp, p, w, dw_ref)
        dz_ref[1] = (dp * zh).astype(BF16)
        dz_ref[2] = (dp * zc).astype(BF16)

    return pl.pallas_call(
        body, name=name, grid=(nj,),
        in_specs=[pl.BlockSpec((s, d), lambda j: (0, 0)), pl.BlockSpec((tn, d), lambda j: (j, 0)),
                  pl.BlockSpec((3, s, tn), lambda j: (0, 0, j)), pl.BlockSpec((3, tn), lambda j: (0, j))],
        out_specs=[pl.BlockSpec((3, s, tn), lambda j: (0, 0, j)), pl.BlockSpec((3, tn), lambda j: (0, j))],
        out_shape=[jax.ShapeDtypeStruct((3, s, d), BF16), jax.ShapeDtypeStruct((3, d), F32)],
        compiler_params=_params(("parallel",)),
    )(dm, wout, z, wdw)


def _pool_select(g, c2, c4, c8, c16):
    return jnp.where(g == 0, c2, jnp.where(g == 1, c4, jnp.where(g == 2, c8, c16)))


def _pool_inv_count(g, shape):
    pos = lax.broadcasted_iota(jnp.int32, shape, 0).astype(F32) + 1.0
    win = (2 << g).astype(F32)
    return jnp.minimum(pos, win)


def _pool_fwd(n, win, wgrp, scale, name):
    s, d = n.shape
    tn = POOL_GROUP_DIM

    def body(n_ref, wi_ref, wg_ref, sc_ref, p_ref, y_ref):
        g = pl.program_id(0)
        u = _dot(n_ref[...], wi_ref[...])
        s2 = u + _shift_down(u, 1)
        s4 = s2 + _shift_down(s2, 2)
        s8 = s4 + _shift_down(s4, 4)
        s16 = s8 + _shift_down(s8, 8)
        tot = _pool_select(g, s2, s4, s8, s16)
        p = (tot / _pool_inv_count(g, u.shape) - u).astype(BF16)
        p_ref[...] = p
        y_ref[...] = (_dot(p, wg_ref[...]) * sc_ref[...]).astype(BF16)

    return pl.pallas_call(
        body, name=name, grid=(d // tn,),
        in_specs=[pl.BlockSpec((s, d), lambda g: (0, 0)), pl.BlockSpec((d, tn), lambda g: (0, g)),
                  pl.BlockSpec((None, tn, tn), lambda g: (g, 0, 0)), pl.BlockSpec((1, tn), lambda g: (0, g))],
        out_specs=[pl.BlockSpec((s, tn), lambda g: (0, g)), pl.BlockSpec((s, tn), lambda g: (0, g))],
        out_shape=[jax.ShapeDtypeStruct((s, d), BF16), jax.ShapeDtypeStruct((s, d), BF16)],
        compiler_params=_params(("parallel",)),
    )(n, win, wgrp, scale)


def _pool_mid_bwd(dm, wout, p, wgrp, scale, name):
    s, d = dm.shape
    tn = POOL_GROUP_DIM

    def body(dm_ref, wo_ref, p_ref, wg_ref, sc_ref, du_ref, dwg_ref, dsc_ref):
        g = pl.program_id(0)
        dy = _dot(dm_ref[...], wo_ref[...], NT)
        pv = p_ref[...]
        wg = wg_ref[...]
        ypre = _dot(pv, wg)
        dsc_ref[...] = jnp.sum(dy * ypre, axis=0, keepdims=True)
        dypre = (dy * sc_ref[...]).astype(BF16)
        dwg_ref[...] = _dot(pv, dypre, TN)
        dp = _dot(dypre, wg, NT)
        e = dp / _pool_inv_count(g, dp.shape)
        f2 = e + _shift_up(e, 1)
        f4 = f2 + _shift_up(f2, 2)
        f8 = f4 + _shift_up(f4, 4)
        f16 = f8 + _shift_up(f8, 8)
        du_ref[...] = (_pool_select(g, f2, f4, f8, f16) - dp).astype(BF16)

    return pl.pallas_call(
        body, name=name, grid=(d // tn,),
        in_specs=[pl.BlockSpec((s, d), lambda g: (0, 0)), pl.BlockSpec((tn, d), lambda g: (g, 0)),
                  pl.BlockSpec((s, tn), lambda g: (0, g)), pl.BlockSpec((None, tn, tn), lambda g: (g, 0, 0)),
                  pl.BlockSpec((1, tn), lambda g: (0, g))],
        out_specs=[pl.BlockSpec((s, tn), lambda g: (0, g)), pl.BlockSpec((None, tn, tn), lambda g: (g, 0, 0)),
                   pl.BlockSpec((1, tn), lambda g: (0, g))],
        out_shape=[jax.ShapeDtypeStruct((s, d), BF16), jax.ShapeDtypeStruct((4, tn, tn), F32),
                   jax.ShapeDtypeStruct((1, d), F32)],
        compiler_params=_params(("parallel",)),
    )(dm, wout, p, wgrp, scale)


PANEL = LANES
ATTN_EXT = ATTN_WIDTH + PANEL
DVEC_LANE = HEADS


def _alibi_slopes(g, dil):
    all_slopes = 2.0 ** (-8.0 * np.arange(1, N_HEADS_A + 1) / N_HEADS_A)
    return [float(np.float32(sl) * np.float32(dil)) for sl in all_slopes[g * HEADS:(g + 1) * HEADS]]


def _residue_order(a, dil, name):
    s, w = a.shape
    per = ROW_TILE // dil
    panels = w // PANEL

    def body(a_ref, o_ref, *tiles):
        for c in range(panels):
            cols = slice(c * PANEL, (c + 1) * PANEL)
            tiles[c][...] = a_ref[:, cols].astype(F32)
            for r in range(dil):
                o_ref[r, :, cols] = tiles[c][pl.ds(r, per, stride=dil), :].astype(o_ref.dtype)

    out = pl.pallas_call(
        body, name=name, grid=(s // ROW_TILE,),
        in_specs=[pl.BlockSpec((ROW_TILE, w), lambda i: (i, 0))],
        out_specs=pl.BlockSpec((dil, per, w), lambda i: (0, i, 0)),
        out_shape=jax.ShapeDtypeStruct((dil, s // dil, w), a.dtype),
        scratch_shapes=[pltpu.VMEM((ROW_TILE, PANEL), F32)] * panels,
        compiler_params=_params(("parallel",)),
    )(a)
    return out.reshape(s, w)


def _token_order(a, dil, acc, name):
    s, w = a.shape
    per = ROW_TILE // dil
    panels = w // PANEL
    has_acc = acc is not None

    def body(*refs):
        a_ref = refs[0]
        o_ref = refs[2] if has_acc else refs[1]
        tiles = refs[3:] if has_acc else refs[2:]
        for c in range(panels):
            cols = slice(c * PANEL, (c + 1) * PANEL)
            for r in range(dil):
                tiles[c][pl.ds(r, per, stride=dil), :] = a_ref[r, :, cols]
            v = tiles[c][...]
            if has_acc:
                v = v + refs[1][:, cols]
            o_ref[:, cols] = v

    row = pl.BlockSpec((ROW_TILE, w), lambda i: (i, 0))
    return pl.pallas_call(
        body, name=name, grid=(s // ROW_TILE,),
        in_specs=[pl.BlockSpec((dil, per, w), lambda i: (0, i, 0))] + ([row] if has_acc else []),
        out_specs=row, out_shape=jax.ShapeDtypeStruct((s, w), F32),
        scratch_shapes=[pltpu.VMEM((ROW_TILE, PANEL), F32)] * panels,
        compiler_params=_params(("parallel",)),
    )(*([a.reshape(dil, s // dil, w)] + ([acc] if has_acc else [])))


def _qkv_proj(n, wqkv, g, name):
    s, d = n.shape
    tm = _pick(s, MATMUL_TILES)

    def body(a_ref, b_ref, o_ref):
        o_ref[...] = _dot(a_ref[...], b_ref[...]).astype(BF16)

    return pl.pallas_call(
        body, name=name, grid=(s // tm, 3),
        in_specs=[pl.BlockSpec((tm, d), lambda i, t: (i, 0)),
                  pl.BlockSpec((d, ATTN_WIDTH), lambda i, t: (0, 3 * g + t))],
        out_specs=pl.BlockSpec((None, tm, ATTN_WIDTH), lambda i, t: (t, i, 0)),
        out_shape=jax.ShapeDtypeStruct((3, s, ATTN_WIDTH), BF16),
        compiler_params=_params(("parallel", "parallel")),
    )(n, wqkv)


def _attn_window(n, ln):
    if ln == BLOCK:
        return 0, BLOCK
    return pl.multiple_of(jnp.maximum(n - 1, 0) * BLOCK, BLOCK), 2 * BLOCK


def _attn_mask(n, k0, kw):
    qpos = n * BLOCK + lax.broadcasted_iota(jnp.int32, (BLOCK, kw), 0)
    kpos = k0 + lax.broadcasted_iota(jnp.int32, (BLOCK, kw), 1)
    dist = qpos - kpos
    return dist.astype(F32), (dist >= 0) & (dist <= BLOCK)


def _attn_scores(q, keys, slope, dist, valid):
    s = _dot(q, keys, NT) * (HEAD_DIM ** -0.5) - slope * dist
    return jnp.where(valid, s, NEG_INF)


ATTN_STEP_BLOCKS = 1
ATTN_BWD_STEP_BLOCKS = 4


def _attn_block(gb, ln):
    nb = ln // BLOCK
    n, base = (0, gb * ln) if nb == 1 else (gb % nb, (gb // nb) * ln)
    k0, kw = _attn_window(n, ln)
    cur = pl.ds(pl.multiple_of(gb * BLOCK, BLOCK), BLOCK)
    win = pl.ds(pl.multiple_of(base + k0, BLOCK), kw)
    return cur, win, n, k0, kw


def _attn_fwd(qkv, g, name):
    _, s, w = qkv.shape
    dil = DILATED_CFG[g][1]
    ln = s // dil
    slopes = _alibi_slopes(g, dil)
    rows = ATTN_STEP_BLOCKS * BLOCK

    def body(qkv_ref, o_ref):
        o_ref[:, w:] = jnp.zeros((rows, PANEL), F32)
        for b in range(ATTN_STEP_BLOCKS):
            cur, win, n, k0, kw = _attn_block(pl.program_id(0) * ATTN_STEP_BLOCKS + b, ln)
            dist, valid = _attn_mask(n, k0, kw)
            out = slice(b * BLOCK, (b + 1) * BLOCK)
            for h in range(HEADS):
                cols = slice(h * HEAD_DIM, (h + 1) * HEAD_DIM)
                sc = _attn_scores(qkv_ref[0, cur, cols], qkv_ref[1, win, cols], slopes[h], dist, valid)
                m = jnp.max(sc, axis=-1, keepdims=True)
                p = jnp.exp(sc - m)
                den = jnp.sum(p, axis=-1, keepdims=True)
                o_ref[out, cols] = _dot(p.astype(BF16), qkv_ref[2, win, cols]) / den
                o_ref[out, w + h:w + h + 1] = m + jnp.log(den)

    return pl.pallas_call(
        body, name=name, grid=(s // rows,),
        in_specs=[pl.BlockSpec((3, s, w), lambda i: (0, 0, 0))],
        out_specs=pl.BlockSpec((rows, ATTN_EXT), lambda i: (i, 0)),
        out_shape=jax.ShapeDtypeStruct((s, ATTN_EXT), F32),
        compiler_params=_params(("parallel",)),
    )(qkv)


def _attn_bwd(qkv, dext, g, name, dep=None):
    _, s, w = qkv.shape
    dil = DILATED_CFG[g][1]
    ln = s // dil
    slopes = _alibi_slopes(g, dil)
    scale = HEAD_DIM ** -0.5
    rows = ATTN_BWD_STEP_BLOCKS * BLOCK
    steps = s // rows
    deps = [] if dep is None else [dep]

    def body(qkv_ref, de_ref, *rest):
        d_ref, dk_ref, dv_ref = rest[-3:]

        @pl.when(pl.program_id(0) == 0)
        def _():
            dk_ref[...] = jnp.zeros_like(dk_ref)
            dv_ref[...] = jnp.zeros_like(dv_ref)

        for b in range(ATTN_BWD_STEP_BLOCKS):
            cur, win, n, k0, kw = _attn_block(pl.program_id(0) * ATTN_BWD_STEP_BLOCKS + b, ln)
            dist, valid = _attn_mask(n, k0, kw)
            blk = slice(b * BLOCK, (b + 1) * BLOCK)
            for h in range(HEADS):
                cols = slice(h * HEAD_DIM, (h + 1) * HEAD_DIM)
                q, keys = qkv_ref[0, cur, cols], qkv_ref[1, win, cols]
                dob = de_ref[blk, cols].astype(BF16)
                p = jnp.exp(_attn_scores(q, keys, slopes[h], dist, valid) - de_ref[blk, w + h:w + h + 1])
                dd = de_ref[blk, w + DVEC_LANE + h:w + DVEC_LANE + h + 1]
                ds = (p * (_dot(dob, qkv_ref[2, win, cols], NT) - dd)).astype(BF16)
                d_ref[0, cur, cols] = (scale * _dot(ds, keys)).astype(BF16)
                dv_ref[win, cols] += _dot(p.astype(BF16), dob, TN)
                dk_ref[win, cols] += scale * _dot(ds, q, TN)

        @pl.when(pl.program_id(0) == steps - 1)
        def _():
            d_ref[1] = dk_ref[...].astype(BF16)
            d_ref[2] = dv_ref[...].astype(BF16)

    whole = pl.BlockSpec((3, s, w), lambda i: (0, 0, 0))
    return pl.pallas_call(
        body, name=name, grid=(steps,),
        in_specs=[whole, pl.BlockSpec((rows, ATTN_EXT), lambda i: (i, 0))] + [_ANY] * len(deps),
        out_specs=whole, out_shape=jax.ShapeDtypeStruct((3, s, w), BF16),
        scratch_shapes=[pltpu.VMEM((s, w), F32), pltpu.VMEM((s, w), F32)],
        compiler_params=_params(("arbitrary",)),
    )(qkv, dext, *deps)


def _attn_merge(e0, e1, e2, name):
    s = e0.shape[0]
    w = ATTN_WIDTH

    def body(e0_ref, e1_ref, e2_ref, m_ref, mb_ref, lse_ref):
        refs = (e0_ref, e1_ref, e2_ref)
        l = [r[:, w:w + HEADS] for r in refs]
        mx = jnp.maximum(jnp.maximum(l[0], l[1]), l[2])
        e = [jnp.exp(v - mx) for v in l]
        z = e[0] + e[1] + e[2]
        lse_ref[...] = mx + jnp.log(z)
        wts = [v / z for v in e]
        for h in range(HEADS):
            cols = slice(h * HEAD_DIM, (h + 1) * HEAD_DIM)
            acc = wts[0][:, h:h + 1] * refs[0][:, cols]
            for g in range(1, N_GROUPS_A):
                acc = acc + wts[g][:, h:h + 1] * refs[g][:, cols]
            m_ref[:, cols] = acc
            mb_ref[:, cols] = acc.astype(BF16)

    ext = pl.BlockSpec((ROW_TILE, ATTN_EXT), lambda i: (i, 0))
    row = pl.BlockSpec((ROW_TILE, w), lambda i: (i, 0))
    return pl.pallas_call(
        body, name=name, grid=(s // ROW_TILE,),
        in_specs=[ext, ext, ext],
        out_specs=[row, row, pl.BlockSpec((ROW_TILE, HEADS), lambda i: (i, 0))],
        out_shape=[jax.ShapeDtypeStruct((s, w), F32), jax.ShapeDtypeStruct((s, w), BF16),
                   jax.ShapeDtypeStruct((s, HEADS), F32)],
        compiler_params=_params(("parallel",)),
    )(e0, e1, e2)


def _attn_dvec(dmerged, merged, lse_all, name, dep=None):
    s, w = merged.shape
    deps = [] if dep is None else [dep]

    def body(dm_ref, m_ref, lse_ref, *rest):
        de_ref = rest[-1]
        dmv = dm_ref[...]
        de_ref[:, :w] = dmv
        de_ref[:, w:] = jnp.zeros((ROW_TILE, PANEL), F32)
        de_ref[:, w:w + HEADS] = lse_ref[...]
        prod = dmv * m_ref[...]
        for h in range(HEADS):
            lane = w + DVEC_LANE + h
            de_ref[:, lane:lane + 1] = jnp.sum(prod[:, h * HEAD_DIM:(h + 1) * HEAD_DIM], axis=-1, keepdims=True)

    row = pl.BlockSpec((ROW_TILE, w), lambda i: (i, 0))
    return pl.pallas_call(
        body, name=name, grid=(s // ROW_TILE,),
        in_specs=[row, row, pl.BlockSpec((ROW_TILE, HEADS), lambda i: (i, 0))] + [_ANY] * len(deps),
        out_specs=pl.BlockSpec((ROW_TILE, ATTN_EXT), lambda i: (i, 0)),
        out_shape=jax.ShapeDtypeStruct((s, ATTN_EXT), F32),
        compiler_params=_params(("parallel",)),
    )(dmerged, merged, lse_all, *deps)


def _attention_fwd(n, wqkv, wo, tag):
    ns, qkvs, exts = [], [], []
    for g, (_, dil) in enumerate(DILATED_CFG):
        ng = n if dil == 1 else _residue_order(n, dil, f"{tag}_order_g{g}")
        qkv = _qkv_proj(ng, wqkv, g, f"{tag}_qkv_g{g}")
        ext = _attn_fwd(qkv, g, f"{tag}_fwd_g{g}")
        ns.append(ng)
        qkvs.append(qkv)
        exts.append(ext if dil == 1 else _token_order(ext, dil, None, f"{tag}_unorder_g{g}"))
    merged, merged_bf, lse_all = _attn_merge(*exts, f"{tag}_merge")
    m = _matmul(merged_bf, wo, "nn", F32, f"{tag}_wo")
    return m, (ns, qkvs, merged, merged_bf, lse_all)


def _attention_bwd(dm, wqkv, wo, saved, tag, dep=None, hook=None):
    ns, qkvs, merged, merged_bf, lse_all = saved
    d_wo = _matmul(merged_bf, dm, "tn", BF16, f"{tag}_dwo")
    dmerged = _matmul(dm, wo, "nt", F32, f"{tag}_dmerged")
    dext = _attn_dvec(dmerged, merged, lse_all, f"{tag}_dvec", dep)
    width = 3 * ATTN_WIDTH
    d_wqkv, dn, dep = [], None, None
    for g, (_, dil) in enumerate(DILATED_CFG):
        dext_g = dext if dil == 1 else _residue_order(dext, dil, f"{tag}_dorder_g{g}")
        dqkv = _attn_bwd(qkvs[g], dext_g, g, f"{tag}_bwd_g{g}", dep)
        dep = hook(g, dqkv) if hook is not None and g + 1 < N_GROUPS_A else None
        d_wqkv.append(_matmul(ns[g], dqkv, "tn", BF16, f"{tag}_dwqkv_g{g}", b_parts=3))
        dn_g = _matmul(dqkv, wqkv[:, g * width:(g + 1) * width], "nt", F32, f"{tag}_dn_g{g}", a_parts=3)
        dn = dn_g if dil == 1 else _token_order(dn_g, dil, dn, f"{tag}_dn_sum_g{g}")
    return dn, jnp.concatenate(d_wqkv, axis=1), d_wo


def _layer_matrices(i):
    mixer = (("attn_w_qkv", "attn_w_o"), ("conv_w_in", "conv_w_out"), ("pool_w_in", "pool_w_grp", "pool_w_out"))[i % 3]
    return [(k, i // 3) for k in mixer] + [("ffn_w_up", i), ("ffn_w_down", i)]


def _local_step(x, tgt, vec, weights, sink):
    ng = vec["norm_g"]

    def gain(i, j, token=None):
        g = ng[i, j][None, :]
        return g if token is None else g + token

    saved = []
    n = _rms_fwd(x, gain(0, 0), None, BF16, "norm_first")
    for i in range(DEPTH):
        wl = weights.layer(i)
        t0 = weights.hook(i, 0, n)
        kind, idx = i % 3, i // 3
        if kind == 0:
            m, ms = _attention_fwd(n, wl["attn_w_qkv"], wl["attn_w_o"], "attn")
        elif kind == 1:
            taps = vec["conv_w_dw"][idx] if t0 is None else vec["conv_w_dw"][idx] + t0
            z, y = _sconv_fwd(n, wl["conv_w_in"], taps, "sconv_fwd")
            m = _matmul(y, wl["conv_w_out"], "nn", F32, "sconv_out")
            ms = (z, y)
        else:
            scale = vec["pool_scale"][idx][None, :] if t0 is None else vec["pool_scale"][idx][None, :] + t0
            p, y = _pool_fwd(n, wl["pool_w_in"], wl["pool_w_grp"], scale, "pool_fwd")
            m = _matmul(y, wl["pool_w_out"], "nn", F32, "pool_out")
            ms = (p, y)
        t1 = weights.hook(i, 1, m)
        x1, n2 = _rms_res_pre(m, gain(i, 1, t0), x, gain(i, 2, t1), "norm_res_pre")
        h, c, a = _ffn_up(n2, wl["ffn_w_up"], vec["ffn_w_dw"][i], "ffn_up")
        t2 = weights.hook(i, 2, a)
        f = _matmul(a, wl["ffn_w_down"].reshape(D_FF, D_MODEL), "nn", F32, "ffn_down", a_parts=FFN_PAIRS)
        saved.append((x, n, m, ms, x1, n2, h, a, f, wl, c))
        if i + 1 < DEPTH:
            x, n = _rms_res_pre(f, gain(i, 3, t2), x1, gain(i + 1, 0), "norm_res_pre")
        else:
            x = _rms_fwd(f, gain(i, 3), x1, F32, "norm_res")
        weights.hook(i, 3, x)

    loss, dx = _loss_head(x, tgt, "loss_head")

    g_norm = [[None] * 4 for _ in range(DEPTH)]
    g_taps, g_scale, g_ffn_dw = [], [], [None] * DEPTH
    df, g_norm[DEPTH - 1][3] = _rms_bwd(saved[-1][8], gain(DEPTH - 1, 3), dx, None, BF16, "norm_bwd_sub")
    t0 = None
    for i in reversed(range(DEPTH)):
        xin, n, m, ms, x1, n2, h, a, f, wl, c = saved[i]
        kind, idx = i % 3, i // 3
        gl = {}
        d_wdown = _matmul(a, df, "tn", BF16, "ffn_dwdown", a_parts=FFN_PAIRS)
        gl["ffn_w_down"] = d_wdown.reshape(N_DEV, D_FF // N_DEV, D_MODEL)
        ffn_taps = vec["ffn_w_dw"][i] if t0 is None else vec["ffn_w_dw"][i] + t0
        dh, dwg, dwu = _ffn_mid_bwd(df, wl["ffn_w_down"].reshape(FFN_PAIRS, -1, D_MODEL), h, c, ffn_taps, "ffn_mid_bwd")
        g_ffn_dw[i] = jnp.concatenate([dwg, dwu], axis=0)
        t1 = sink.hook(i, 1, dh)
        gl["ffn_w_up"] = _ffn_dwup(n2, dh, "ffn_dwup")
        tf = sink.ffn_done(i, gl)
        dn2 = _ffn_dn(dh, wl["ffn_w_up"], "ffn_dn", t1)
        dx1, dm, g_norm[i][2], g_norm[i][1] = _rms_bwd_pair(x1, gain(i, 2, tf), dn2, dx, m, gain(i, 1), "norm_bwd_pair")
        t2 = sink.hook(i, 2, dm)
        if kind == 0:
            dn, gl["attn_w_qkv"], gl["attn_w_o"] = _attention_bwd(
                dm, wl["attn_w_qkv"], wl["attn_w_o"], ms, "attn", t2, lambda g, after, i=i: sink.hook(i, ("a", "b")[g], after))
        elif kind == 1:
            z, y = ms
            gl["conv_w_out"] = _matmul(y, dm, "tn", BF16, "sconv_dwout")
            taps = vec["conv_w_dw"][idx] if t2 is None else vec["conv_w_dw"][idx] + t2
            dz, ddw = _sconv_mid_bwd(dm, wl["conv_w_out"], z, taps, "sconv_mid_bwd")
            g_taps.append(ddw)
            gl["conv_w_in"] = _matmul(n, dz, "tn", BF16, "sconv_dwin", b_parts=3)
            dn = _matmul(dz, wl["conv_w_in"], "nt", F32, "sconv_dn", a_parts=3)
        else:
            p, y = ms
            gl["pool_w_out"] = _matmul(y, dm, "tn", BF16, "pool_dwout")
            scale = vec["pool_scale"][idx][None, :] if t2 is None else vec["pool_scale"][idx][None, :] + t2
            du, gl["pool_w_grp"], dscale = _pool_mid_bwd(dm, wl["pool_w_out"], p, wl["pool_w_grp"], scale, "pool_mid_bwd")
            g_scale.append(dscale[0])
            gl["pool_w_in"] = _matmul(n, du, "tn", BF16, "pool_dwin")
            dn = _matmul(du, wl["pool_w_in"], "nt", F32, "pool_dn")
        sink.hook(i, 3, dn)
        if i > 0:
            dx, df, g_norm[i][0], g_norm[i - 1][3] = _rms_bwd_pair(xin, gain(i, 0, t2), dn, dx1, saved[i - 1][8],
                                                                   gain(i - 1, 3), "norm_bwd_pair")
        else:
            dx, g_norm[0][0] = _rms_bwd(xin, gain(0, 0), dn, dx1, F32, "norm_bwd_res")
        t0 = sink.layer_done(i, gl)

    vec_grads = {"norm_g": jnp.stack([jnp.concatenate(row, axis=0) for row in g_norm]), "conv_w_dw": jnp.stack(g_taps),
                 "pool_scale": jnp.stack(g_scale), "ffn_w_dw": g_ffn_dw}
    return loss, dx, vec_grads


_AXES = ("x", "y", "c")
ROUTE_A = ("y", "x", "c")
ROUTE_B = ("x", "y", "c")
def _dev_index(pos):
    return 4 * pos["x"] + 2 * pos["y"] + pos["c"]


_HBM = pl.BlockSpec(memory_space=pltpu.HBM)
_SEM = pl.BlockSpec(memory_space=pltpu.SEMAPHORE)
_ANY = pl.BlockSpec(memory_space=pl.ANY)
_EFFECT = pltpu.SideEffectType.DATAFLOW_SIDE_EFFECTING


TOKEN_SHAPE = (1, D_MODEL)


def _copies_start(describe, arrays, n_copies, name, after, token_shape=TOKEN_SHAPE):
    n = len(arrays)
    deps = [] if after is None else [after]

    def body(*refs):
        send_sems, recv_sems = refs[n + len(deps)], refs[n + len(deps) + 1]
        for c in describe(refs[:n], send_sems, recv_sems):
            c.start()
        refs[-1][...] = jnp.zeros_like(refs[-1])

    outs = pl.pallas_call(
        body, name=f"{name}_start",
        out_shape=(pltpu.SemaphoreType.DMA((n_copies,)), pltpu.SemaphoreType.DMA((n_copies,)),
                   *[pltpu.HBM(a.shape, a.dtype) for a in arrays], jax.ShapeDtypeStruct(token_shape, F32)),
        in_specs=[_HBM] * n + [_ANY] * len(deps),
        out_specs=(_SEM, _SEM, *([_HBM] * n), pl.BlockSpec(memory_space=pltpu.VMEM)),
        input_output_aliases={i: 2 + i for i in range(n)},
        compiler_params=pltpu.CompilerParams(has_side_effects=_EFFECT),
    )(*[pltpu.with_memory_space_constraint(a, pltpu.HBM) for a in arrays], *deps)
    return (outs[0], outs[1], list(outs[2:2 + n])), outs[-1]


def _copies_wait(describe, handle, name, after):
    send_sems, recv_sems, arrays = handle
    n = len(arrays)
    deps = [] if after is None else list(after) if isinstance(after, (list, tuple)) else [after]

    def body(*refs):
        for c in describe(refs[:n], refs[n], refs[n + 1]):
            c.wait_send()
            c.wait_recv()

    outs = pl.pallas_call(
        body, name=f"{name}_wait",
        out_shape=tuple(pltpu.HBM(a.shape, a.dtype) for a in arrays),
        in_specs=[_HBM] * n + [_SEM, _SEM] + [_ANY] * len(deps), out_specs=tuple([_HBM] * n),
        input_output_aliases={i: i for i in range(n)},
        compiler_params=pltpu.CompilerParams(has_side_effects=_EFFECT),
    )(*arrays, send_sems, recv_sems, *deps)
    return list(outs)


GATHER_STAGE_COPIES = (3, 3, 1)


def _gather_copies(stage, routes):
    n = len(routes)

    def describe(refs, send_sems, recv_sems):
        pos = {a: lax.axis_index(a) for a in _AXES}

        def flipped(axes):
            return {a: 1 - pos[a] if a in axes else pos[a] for a in _AXES}

        copies = []
        for i, (a1, a2, a3) in enumerate(routes):
            land = refs[n + i] if stage == 1 else refs[i]
            p1, p2, p12, p3 = flipped((a1,)), flipped((a2,)), flipped((a1, a2)), flipped((a3,))
            plan = {1: [(None, p1), (None, p2), (None, p3)], 2: [(p1, p2), (p1, p3), (p2, p3)], 3: [(p12, p3)]}[stage]
            for holder, to in plan:
                slot = land.at[_dev_index(pos if holder is None else holder)]
                k = len(copies)
                copies.append(pltpu.make_async_remote_copy(
                    src_ref=refs[i] if holder is None else slot, dst_ref=slot,
                    send_sem=send_sems.at[k], recv_sem=recv_sems.at[k],
                    device_id=tuple(to[a] for a in _AXES), device_id_type=pl.DeviceIdType.MESH))
        return copies

    return describe


def _gather_begin(shards, routes, name, after):
    n = len(shards)
    lands = [lax.empty((N_DEV,) + a.shape, a.dtype) for a in shards]
    handle, token = _copies_start(_gather_copies(1, routes), list(shards) + lands, GATHER_STAGE_COPIES[0] * n,
                                  f"{name}_1", after)
    return {"stage": 1, "handle": handle, "routes": routes, "name": name, "n": n}, token


def _gather_next(state, after):
    stage, routes, name, n = state["stage"], state["routes"], state["name"], state["n"]
    arrays = _copies_wait(_gather_copies(stage, routes), state["handle"], f"{name}_{stage}", after)
    if stage == 1:
        state = dict(state, shards=arrays[:n])
        arrays = arrays[n:]
    if stage == 3:
        me = _dev_index({a: lax.axis_index(a) for a in _AXES})
        return [lax.dynamic_update_index_in_dim(o, s, me, 0) for o, s in zip(arrays, state["shards"])], None
    handle, token = _copies_start(_gather_copies(stage + 1, routes), arrays, GATHER_STAGE_COPIES[stage] * n,
                                  f"{name}_{stage + 1}", None)
    return dict(state, stage=stage + 1, handle=handle), token


ADD_ROW_TILES = (1024, 704, 512, 352, 256, 128, 96, 64, 32, 16)


def _add_half(a, recv, me, out_dtype, name):
    p, q, cols = recv.shape
    tr = _pick(q, ADD_ROW_TILES)

    def body(me_ref, a_ref, b_ref, o_ref):
        o_ref[...] = (a_ref[...].astype(F32) + b_ref[...].astype(F32)).astype(o_ref.dtype)

    return pl.pallas_call(
        body, name=name,
        grid_spec=pltpu.PrefetchScalarGridSpec(
            num_scalar_prefetch=1, grid=(p, q // tr),
            in_specs=[pl.BlockSpec((None, None, tr, cols), lambda j, i, m: (j, m[0], i, 0)),
                      pl.BlockSpec((None, tr, cols), lambda j, i, m: (j, i, 0))],
            out_specs=pl.BlockSpec((None, tr, cols), lambda j, i, m: (j, i, 0))),
        out_shape=jax.ShapeDtypeStruct((p, q, cols), out_dtype),
        compiler_params=_params(("parallel", "parallel")),
    )(me, a, recv)


def _half_copies(axes):
    n = len(axes)

    def describe(refs, send_sems, recv_sems):
        pos = {a: lax.axis_index(a) for a in _AXES}
        copies = []
        for i, axis in enumerate(axes):
            peer = tuple(1 - pos[a] if a == axis else pos[a] for a in _AXES)
            copies.append(pltpu.make_async_remote_copy(
                src_ref=refs[i].at[:, 1 - pos[axis]], dst_ref=refs[n + i], send_sem=send_sems.at[i],
                recv_sem=recv_sems.at[i], device_id=peer, device_id_type=pl.DeviceIdType.MESH))
        return copies

    return describe


def _scatter_begin(slots, routes, tags, name, token_shape=TOKEN_SHAPE):
    shapes = [a.shape[1:] for a in slots]
    rows = [math.prod(s[:-1]) for s in shapes]
    arrays = [a.reshape(4, 2, n, s[-1]) for a, n, s in zip(slots, rows, shapes)]
    return _scatter_start({"stage": 0, "arrays": arrays, "routes": routes, "tags": tags, "name": name,
                           "shapes": shapes, "rows": rows}, token_shape)


def _scatter_start(state, token_shape=TOKEN_SHAPE):
    stage, arrays = state["stage"], state["arrays"]
    axes = [r[2 - stage] for r in state["routes"]]
    lands = [lax.empty((a.shape[0],) + a.shape[2:], a.dtype) for a in arrays]
    handle, token = _copies_start(_half_copies(axes), arrays + lands, len(arrays), f"{state['name']}_{stage + 1}", None,
                                  token_shape)
    return dict(state, handle=handle, axes=axes), token


def _scatter_next(state, after):
    stage, axes, n = state["stage"], state["axes"], len(state["arrays"])
    both = _copies_wait(_half_copies(axes), state["handle"], f"{state['name']}_{stage + 1}", after)
    coord = {a: lax.axis_index(a).astype(jnp.int32).reshape(1) for a in _AXES}
    sums = [_add_half(a, r, coord[ax], F32 if stage == 2 else BF16, f"scatter_add_{stage + 1}_{t}")
            for a, r, ax, t in zip(both[:n], both[n:], axes, state["tags"])]
    if stage == 2:
        return [a.reshape(s) for a, s in zip(sums, state["shapes"])], None
    if stage == 0:
        views = [(1, 2, 2 * r, s[-1]) if route[1] == "x" else (2, 2, r, s[-1])
                 for r, s, route in zip(state["rows"], state["shapes"], state["routes"])]
    else:
        views = [(1, 2, r, s[-1]) for r, s in zip(state["rows"], state["shapes"])]
    return _scatter_start(dict(state, stage=stage + 1, arrays=[a.reshape(v) for a, v in zip(sums, views)]))


_WEIGHTS = {
    "norm_g": ((DEPTH, 4, D_MODEL), 2, True),
    "attn_w_qkv": ((2, D_MODEL, 4608), 2, False),
    "attn_w_o": ((2, ATTN_WIDTH, D_MODEL), 2, False),
    "conv_w_in": ((1, D_MODEL, 3 * D_MODEL), 2, False),
    "conv_w_dw": ((1, 3, D_MODEL), 2, True),
    "conv_w_out": ((1, D_MODEL, D_MODEL), 1, False),
    "pool_w_in": ((1, D_MODEL, D_MODEL), 1, False),
    "pool_w_grp": ((1, 4, POOL_GROUP_DIM, POOL_GROUP_DIM), 2, False),
    "pool_scale": ((1, D_MODEL), 1, True),
    "pool_w_out": ((1, D_MODEL, D_MODEL), 1, False),
    "ffn_w_up": ((DEPTH, D_MODEL, 2 * D_FF), 2, False),
    "ffn_w_dw": ((DEPTH, 3, 2 * D_FF), 2, True),
    "ffn_w_down": ((DEPTH, D_FF, D_MODEL), 1, False),
}
_NAMES = tuple(_WEIGHTS)
_VECTORS = tuple(k for k in _NAMES if _WEIGHTS[k][2])
_MATRICES = tuple(k for k in _NAMES if not _WEIGHTS[k][2])
_FFN = ("ffn_w_up", "ffn_w_down")
_ON_ROUTE_A = ("ffn_w_up", "attn_w_o", "conv_w_out", "pool_w_in")
PACK_ROWS = 16


def _route(name):
    return ROUTE_A if name in _ON_ROUTE_A else ROUTE_B


def _shard_shape(name):
    shape, ax, _ = _WEIGHTS[name]
    return tuple(s // N_DEV if i == ax else s for i, s in enumerate(shape))


def _full_from_slots(slots, name, layers=None):
    shape, ax, _ = _WEIGHTS[name]
    if layers is not None:
        shape = (layers,) + shape[1:]
    return jnp.moveaxis(slots, 0, ax).reshape(shape)


def _slots_from_full(full, name):
    shape, ax, _ = _WEIGHTS[name]
    split = shape[:ax] + (N_DEV, shape[ax] // N_DEV) + shape[ax + 1:]
    return jnp.moveaxis(full.reshape(split), ax, 0)


def _pack_vectors(parts, lead):
    rows = []
    for k in _VECTORS:
        r = parts[k].reshape(lead + (-1, LANES))
        pad = -r.shape[-2] % PACK_ROWS
        rows.append(jnp.pad(r, [(0, 0)] * len(lead) + [(0, pad), (0, 0)]))
    return jnp.concatenate(rows, axis=len(lead))


def _unpack_vectors(buf, lead):
    out, r0 = {}, 0
    for k in _VECTORS:
        shard = _shard_shape(k)
        rows = math.prod(shard) // LANES
        out[k] = buf[..., r0:r0 + rows, :].reshape(lead + shard)
        r0 += rows + (-rows % PACK_ROWS)
    return out


class _LayerWeights:
    def __init__(self, shards):
        self.cast = {k: shards[k].astype(BF16) for k in _MATRICES}
        first, ffn0 = _layer_matrices(0)[:-2], _layer_matrices(0)[-2:]
        state, _ = _gather_begin(self._send(first) + [_pack_vectors(shards, ())],
                                 [_route(k) for k, _ in first] + [ROUTE_B], "gather0", None)
        state, _ = _gather_next(state, self._send(ffn0) + self._send(_layer_matrices(1)))
        state, _ = _gather_next(state, None)
        outs, _ = _gather_next(state, None)
        vec = _unpack_vectors(outs[-1], (N_DEV,))
        self.vec = {k: _full_from_slots(vec[k], k) for k in _VECTORS}
        self.vec["ffn_w_dw"] = [vec["ffn_w_dw"][:, l] for l in range(DEPTH)]
        self.ready = {0: self._unpack(first, outs[:-1])}
        self.chains = {}
        tokens = []
        self._begin("ffn0", ffn0, "gather0f", outs[0], tokens)
        self._begin(1, _layer_matrices(1), "gather1", outs[0], tokens)
        self.vec["norm_g"] = self.vec["norm_g"] + (tokens[0] + tokens[1])

    def _send(self, items):
        return [self.cast[k][j] for k, j in items]

    @staticmethod
    def _unpack(items, outs):
        return {k: o if k in _FFN else _full_from_slots(o[:, None], k, layers=1)[0] for (k, _), o in zip(items, outs)}

    def _begin(self, key, items, name, after, tokens):
        state, token = _gather_begin(self._send(items), [_route(k) for k, _ in items], name, after)
        self.chains[key] = (items, state)
        tokens.append(token)

    def _advance(self, key, after, tokens):
        items, state = self.chains.pop(key)
        state, token = _gather_next(state, after)
        if token is None:
            self.ready.setdefault(0 if key == "ffn0" else key, {}).update(self._unpack(items, state))
        else:
            self.chains[key] = (items, state)
            tokens.append(token)

    def layer(self, i):
        return self.ready[i]

    def hook(self, i, point, after):
        tokens = []
        if i == 0 and point == 0:
            self._advance("ffn0", after, tokens)
        if i == 0 and point == 1:
            self._advance("ffn0", after, tokens)
            self._advance("ffn0", None, tokens)
        if point >= 1 and i + 1 in self.chains:
            self._advance(i + 1, after, tokens)
        if point == 1 and i + 2 < DEPTH:
            self._begin(i + 2, _layer_matrices(i + 2), f"gather{i + 2}", after, tokens)
        return functools.reduce(lambda a, b: a + b, tokens) if tokens else None


def _layer_slots(g, name):
    shape, ax, _ = _WEIGHTS[name]
    shape, ax = shape[1:], ax - 1
    split = shape[:ax] + (N_DEV, shape[ax] // N_DEV) + shape[ax + 1:]
    return jnp.moveaxis(g.reshape(split), ax, 0).astype(BF16)


class _GradSink:
    def __init__(self):
        self.state = None
        self.ffn_state = None
        self.sums = {}
        self.last = None

    def ffn_done(self, i, grads):
        if i != 0:
            return None
        self.ffn_items = _layer_matrices(0)[-2:]
        self.ffn_state, token = _scatter_begin([grads[k] for k, _ in self.ffn_items],
                                               [_route(k) for k, _ in self.ffn_items],
                                               [f"{k}{j}" for k, j in self.ffn_items], "scatter0f")
        return token

    def layer_done(self, i, grads):
        items = _layer_matrices(i)
        if i == 0:
            items = items[:-2]
            self.last = (items, [_layer_slots(grads[k], k) for k, _ in items])
            return None
        slots = [grads[k] if k in _FFN else _layer_slots(grads[k], k) for k, _ in items]
        self.items = items
        self.state, token = _scatter_begin(slots, [_route(k) for k, _ in items], [f"{k}{j}" for k, j in items],
                                           f"scatter{i}", (N_DEV, 3, 2 * D_FF // N_DEV))
        return token

    def hook(self, i, point, after):
        tokens = []
        if self.state is not None and point in (1, 2, 3):
            self.state, token = _scatter_next(self.state, after)
            if point == 3:
                self.sums.update(dict(zip(self.items, self.state)))
                self.state = None
            tokens.append(token)
        if self.ffn_state is not None and point in ("a", "b", 3):
            self.ffn_state, token = _scatter_next(self.ffn_state, after)
            if point == 3:
                self.sums.update(dict(zip(self.ffn_items, self.ffn_state)))
                self.ffn_state = None
            tokens.append(token)
        tokens = [t for t in tokens if t is not None]
        return functools.reduce(lambda a, b: a + b, tokens) if tokens else None


def _adamw(w, g, m, v, name, layer=None, prev=None, dep=None):
    shape = w.shape
    cols = shape[-1]
    view = shape if len(shape) == 3 else (1, math.prod(shape[:-1]), cols)
    layers, rows, _ = view
    tr = _pick(rows, (512, 256, 128, 64, 32, 16, 8))
    n_prev = 0 if prev is None else 3
    lead = ([] if prev is None else [p.reshape(view) for p in prev]) + ([] if dep is None else [dep])

    def body(*refs):
        w_ref, g_ref, m_ref, v_ref = refs[len(lead):len(lead) + 4]
        d_ref, nm_ref, nv_ref = refs[len(lead) + 4:]
        gv = g_ref[...]
        nm = ADAM_B1 * m_ref[...] + (1.0 - ADAM_B1) * gv
        nv = ADAM_B2 * v_ref[...] + (1.0 - ADAM_B2) * jnp.square(gv)
        m_hat = nm / (1.0 - ADAM_B1 ** ADAM_STEP)
        v_hat = nv / (1.0 - ADAM_B2 ** ADAM_STEP)
        d_ref[...] = -ADAM_LR * (m_hat / (jnp.sqrt(v_hat) + ADAM_EPS) + ADAM_WD * w_ref[...])
        nm_ref[...] = nm
        nv_ref[...] = nv

    if layer is None:
        grid = (layers, rows // tr)
        blk = gblk = pl.BlockSpec((None, tr, cols), lambda l, i: (l, i, 0))
        gview = view
    else:
        grid = (rows // tr,)
        blk = pl.BlockSpec((None, tr, cols), lambda i: (layer, i, 0))
        gblk = pl.BlockSpec((tr, cols), lambda i: (i, 0))
        gview = (rows, cols)
    shp = jax.ShapeDtypeStruct(view, F32)
    outs = pl.pallas_call(
        body, name=name, grid=grid, in_specs=[_ANY] * len(lead) + [blk, gblk, blk, blk], out_specs=[blk] * 3,
        out_shape=[shp] * 3, input_output_aliases={i: i for i in range(n_prev)},
        compiler_params=_params(("parallel",) * len(grid)),
    )(*lead, w.reshape(view), g.reshape(gview), m.reshape(view), v.reshape(view))
    return [o.reshape(shape) for o in outs]


def kernel(x, norm_g, attn_w_qkv, attn_w_o, conv_w_in, conv_w_dw, conv_w_out, pool_w_in, pool_w_grp, pool_scale, pool_w_out, ffn_w_up, ffn_w_dw, ffn_w_down, loss_target, m_norm_g, m_attn_w_qkv, m_attn_w_o, m_conv_w_in, m_conv_w_dw, m_conv_w_out, m_pool_w_in, m_pool_w_grp, m_pool_scale, m_pool_w_out, m_ffn_w_up, m_ffn_w_dw, m_ffn_w_down, v_norm_g, v_attn_w_qkv, v_attn_w_o, v_conv_w_in, v_conv_w_dw, v_conv_w_out, v_pool_w_in, v_pool_w_grp, v_pool_scale, v_pool_w_out, v_ffn_w_up, v_ffn_w_dw, v_ffn_w_down):
    shards = dict(zip(_NAMES, (norm_g, attn_w_qkv, attn_w_o, conv_w_in, conv_w_dw, conv_w_out, pool_w_in,
                               pool_w_grp, pool_scale, pool_w_out, ffn_w_up, ffn_w_dw, ffn_w_down)))
    moms = dict(zip(_NAMES, (m_norm_g, m_attn_w_qkv, m_attn_w_o, m_conv_w_in, m_conv_w_dw, m_conv_w_out,
                             m_pool_w_in, m_pool_w_grp, m_pool_scale, m_pool_w_out, m_ffn_w_up, m_ffn_w_dw,
                             m_ffn_w_down)))
    vels = dict(zip(_NAMES, (v_norm_g, v_attn_w_qkv, v_attn_w_o, v_conv_w_in, v_conv_w_dw, v_conv_w_out,
                             v_pool_w_in, v_pool_w_grp, v_pool_scale, v_pool_w_out, v_ffn_w_up, v_ffn_w_dw,
                             v_ffn_w_down)))
    weights = _LayerWeights(shards)
    sink = _GradSink()
    loss, grad_x, vec_grads = _local_step(x[0], loss_target[0], weights.vec, weights, sink)
    loss = lax.psum(loss[0, 0], _AXES)

    items, slots = sink.last
    vec_slots = {k: _slots_from_full(vec_grads[k], k) for k in _VECTORS if k != "ffn_w_dw"}
    vec_slots["ffn_w_dw"] = jnp.stack(vec_grads["ffn_w_dw"], axis=1)
    state, token = _scatter_begin(slots + [_pack_vectors(vec_slots, (N_DEV,)).astype(BF16)],
                                  [_route(k) for k, _ in items] + [ROUTE_B],
                                  [f"{k}{j}" for k, j in items] + ["vectors"], "scatter0")
    results = {}

    def step(matrices, dep=None):
        outs = []
        for k, j in matrices:
            g = sink.sums[(k, j)]
            if _WEIGHTS[k][0][0] == 1:
                results[k] = (g[None], _adamw(shards[k], g[None], moms[k], vels[k], f"adamw_{k}", dep=dep))
            else:
                gs, prev = results.get(k, ({}, None))
                gs[j] = g
                results[k] = (gs, _adamw(shards[k], g, moms[k], vels[k], f"adamw_{k}{j}", layer=j, prev=prev, dep=dep))
            outs.append(results[k][1][0])
        return outs

    state, token = _scatter_next(state, step(_layer_matrices(3), token))
    state, token = _scatter_next(state, step(_layer_matrices(2) + _layer_matrices(1), token))
    sums, _ = _scatter_next(state, step(_layer_matrices(0)[-2:], token))
    sink.sums.update(dict(zip(items, sums[:-1])))
    step(items)
    vec_sums = _unpack_vectors(sums[-1], ())
    for k in _VECTORS:
        results[k] = (vec_sums[k], _adamw(shards[k], vec_sums[k], moms[k], vels[k], f"adamw_{k}"))
    grads_out = {k: g if not isinstance(g, dict) else jnp.stack([g[j] for j in range(len(g))])
                 for k, (g, _) in results.items()}
    return (loss, grad_x[None], *[grads_out[k] for k in _NAMES], *[results[k][1][0] for k in _NAMES],
            *[results[k][1][1] for k in _NAMES], *[results[k][1][2] for k in _NAMES])
```

```python
import functools
import math

import numpy as np
import jax
import jax.numpy as jnp
from jax import lax
from jax.experimental import pallas as pl
from jax.experimental.pallas import tpu as pltpu

F32, BF16 = jnp.float32, jnp.bfloat16

D_MODEL = 1024
SEQ = 2048
DEPTH = 4
DILATED_CFG = ((128, 1), (512, 4), (2048, 16))
N_GROUPS_A = 3
HEADS = 8
HEAD_DIM = 64
ATTN_WIDTH = HEADS * HEAD_DIM
N_HEADS_A = N_GROUPS_A * HEADS
BLOCK = 128
NEG_INF = -1e30
POOL_GROUP_DIM = 256
D_FF = 2816
RMS_EPS = 1e-6
ADAM_LR, ADAM_B1, ADAM_B2, ADAM_EPS, ADAM_WD, ADAM_STEP = 0.001, 0.9, 0.999, 1e-08, 0.01, 10

N_DEV = 8
LANES = 128
V7X_VMEM_BYTES = 64 * 2 ** 20
VMEM_LIMIT_BYTES = V7X_VMEM_BYTES - 8 * 2 ** 20
COL_TILE = 256
ROW_TILE = 256
MATMUL_TILES = (1024, 1408, 512, 256, 128)
TN_RESIDENT_K = 2048

NN = (((1,), (0,)), ((), ()))
NT = (((1,), (1,)), ((), ()))
TN = (((0,), (0,)), ((), ()))


def _dot(a, b, dims=NN):
    return lax.dot_general(a, b, dims, preferred_element_type=F32)


def _params(sem=None):
    return pltpu.CompilerParams(dimension_semantics=sem, vmem_limit_bytes=VMEM_LIMIT_BYTES)


def _pick(n, prefs):
    for p in prefs:
        if n % p == 0:
            return p
    return n


def _matmul(a, b, mode, out_dtype, name, a_parts=1, b_parts=1):
    if mode == "nn":
        m, k = a.shape[-2], a.shape[-1] * a_parts
        n = b.shape[-1] * b_parts
    elif mode == "nt":
        m, k = a.shape[-2], a.shape[-1] * a_parts
        n = b.shape[-2]
    else:
        k, m = a.shape[-2], a.shape[-1] * a_parts
        n = b.shape[-1] * b_parts
    tm = _pick(m, MATMUL_TILES)
    tn = _pick(n // b_parts if mode != "nt" else n, MATMUL_TILES)
    kk = k // a_parts if mode != "tn" else k
    tk = _pick(kk, MATMUL_TILES)
    if mode == "tn":
        tm = _pick(m // a_parts, MATMUL_TILES)
        if k <= TN_RESIDENT_K:
            tk = k
    gm, gn, gk = m // tm, n // tn, k // tk

    def a_idx(i, j, kq):
        if mode == "tn":
            r, c, per = kq, i, (m // a_parts) // tm
        else:
            r, c, per = i, kq, (k // a_parts) // tk
        return (r, c) if a_parts == 1 else (c // per, r, c % per)

    def b_idx(i, j, kq):
        if mode == "nt":
            return (j, kq)
        per = (n // b_parts) // tn
        return (kq, j) if b_parts == 1 else (j // per, kq, j % per)

    a_blk = (tk, tm) if mode == "tn" else (tm, tk)
    b_blk = (tn, tk) if mode == "nt" else (tk, tn)
    if a_parts > 1:
        a_blk = (None,) + a_blk
    if b_parts > 1:
        b_blk = (None,) + b_blk
    dims = {"nn": NN, "nt": NT, "tn": TN}[mode]

    def body_single(a_ref, b_ref, o_ref):
        o_ref[...] = _dot(a_ref[...], b_ref[...], dims).astype(o_ref.dtype)

    def body(a_ref, b_ref, o_ref, acc_ref):
        kq = pl.program_id(2)

        @pl.when(kq == 0)
        def _():
            acc_ref[...] = jnp.zeros_like(acc_ref)

        acc_ref[...] += _dot(a_ref[...], b_ref[...], dims)

        @pl.when(kq == gk - 1)
        def _():
            o_ref[...] = acc_ref[...].astype(o_ref.dtype)

    return pl.pallas_call(
        body_single if gk == 1 else body, name=name, grid=(gm, gn, gk),
        in_specs=[pl.BlockSpec(a_blk, a_idx), pl.BlockSpec(b_blk, b_idx)],
        out_specs=pl.BlockSpec((tm, tn), lambda i, j, kq: (i, j)),
        out_shape=jax.ShapeDtypeStruct((m, n), out_dtype),
        scratch_shapes=[] if gk == 1 else [pltpu.VMEM((tm, tn), F32)],
        compiler_params=_params(("parallel", "parallel", "arbitrary")),
    )(a, b)


def _rms_fwd(xin, g, res, out_dtype, name):
    s, d = xin.shape
    has_res = res is not None

    def body(*refs):
        x_ref, g_ref = refs[0], refs[1]
        o_ref = refs[-1]
        x = x_ref[...]
        r = lax.rsqrt(jnp.mean(x * x, axis=-1, keepdims=True) + RMS_EPS)
        y = x * r * g_ref[...]
        if has_res:
            y = refs[2][...] + y
        o_ref[...] = y.astype(o_ref.dtype)

    row = pl.BlockSpec((ROW_TILE, d), lambda i: (i, 0))
    vec = pl.BlockSpec((1, d), lambda i: (0, 0))
    ins = [xin, g] + ([res] if has_res else [])
    return pl.pallas_call(
        body, name=name, grid=(s // ROW_TILE,),
        in_specs=[row, vec] + ([row] if has_res else []),
        out_specs=row, out_shape=jax.ShapeDtypeStruct((s, d), out_dtype),
        compiler_params=_params(("parallel",)),
    )(*ins)


def _rms_bwd(xin, g, dy, dres, out_dtype, name):
    s, d = xin.shape
    has_res = dres is not None

    def body(*refs):
        x_ref, g_ref, dy_ref = refs[0], refs[1], refs[2]
        dx_ref, dg_ref = refs[-2], refs[-1]

        @pl.when(pl.program_id(0) == 0)
        def _():
            dg_ref[...] = jnp.zeros_like(dg_ref)

        x = x_ref[...]
        dyv = dy_ref[...].astype(F32)
        r = lax.rsqrt(jnp.mean(x * x, axis=-1, keepdims=True) + RMS_EPS)
        xhat = x * r
        u = dyv * g_ref[...]
        dx = r * (u - xhat * jnp.mean(u * xhat, axis=-1, keepdims=True))
        if has_res:
            dx = refs[3][...] + dx
        dx_ref[...] = dx.astype(dx_ref.dtype)
        dg_ref[...] += jnp.sum(dyv * xhat, axis=0, keepdims=True)

    row = pl.BlockSpec((ROW_TILE, d), lambda i: (i, 0))
    vec = pl.BlockSpec((1, d), lambda i: (0, 0))
    ins = [xin, g, dy] + ([dres] if has_res else [])
    return pl.pallas_call(
        body, name=name, grid=(s // ROW_TILE,),
        in_specs=[row, vec, row] + ([row] if has_res else []),
        out_specs=[row, vec],
        out_shape=[jax.ShapeDtypeStruct((s, d), out_dtype), jax.ShapeDtypeStruct((1, d), F32)],
        compiler_params=_params(("arbitrary",)),
    )(*ins)


def _rms(x):
    r = lax.rsqrt(jnp.mean(x * x, axis=-1, keepdims=True) + RMS_EPS)
    return r, x * r


def _rms_grad(r, xhat, dy, g):
    u = dy * g
    return r * (u - xhat * jnp.mean(u * xhat, axis=-1, keepdims=True))


def _rms_res_pre(sub, g_post, res, g_pre, name):
    s, d = sub.shape

    def body(sub_ref, gp_ref, res_ref, gn_ref, x_ref, n_ref):
        xnew = res_ref[...] + _rms(sub_ref[...])[1] * gp_ref[...]
        x_ref[...] = xnew
        n_ref[...] = (_rms(xnew)[1] * gn_ref[...]).astype(BF16)

    row = pl.BlockSpec((ROW_TILE, d), lambda i: (i, 0))
    vec = pl.BlockSpec((1, d), lambda i: (0, 0))
    return pl.pallas_call(
        body, name=name, grid=(s // ROW_TILE,),
        in_specs=[row, vec, row, vec], out_specs=[row, row],
        out_shape=[jax.ShapeDtypeStruct((s, d), F32), jax.ShapeDtypeStruct((s, d), BF16)],
        compiler_params=_params(("parallel",)),
    )(sub, g_post, res, g_pre)


def _rms_bwd_pair(xmid, g_pre, dn, dres, sub, g_post, name):
    s, d = xmid.shape

    def body(x_ref, gn_ref, dn_ref, dres_ref, sub_ref, gp_ref, dx_ref, dsub_ref, dgn_ref, dgp_ref):
        @pl.when(pl.program_id(0) == 0)
        def _():
            dgn_ref[...] = jnp.zeros_like(dgn_ref)
            dgp_ref[...] = jnp.zeros_like(dgp_ref)

        dnv = dn_ref[...].astype(F32)
        r, xhat = _rms(x_ref[...])
        dx = dres_ref[...] + _rms_grad(r, xhat, dnv, gn_ref[...])
        dx_ref[...] = dx
        dgn_ref[...] += jnp.sum(dnv * xhat, axis=0, keepdims=True)
        rs, shat = _rms(sub_ref[...])
        dsub_ref[...] = _rms_grad(rs, shat, dx, gp_ref[...]).astype(BF16)
        dgp_ref[...] += jnp.sum(dx * shat, axis=0, keepdims=True)

    row = pl.BlockSpec((ROW_TILE, d), lambda i: (i, 0))
    vec = pl.BlockSpec((1, d), lambda i: (0, 0))
    return pl.pallas_call(
        body, name=name, grid=(s // ROW_TILE,),
        in_specs=[row, vec, row, row, row, vec], out_specs=[row, row, vec, vec],
        out_shape=[jax.ShapeDtypeStruct((s, d), F32), jax.ShapeDtypeStruct((s, d), BF16),
                   jax.ShapeDtypeStruct((1, d), F32), jax.ShapeDtypeStruct((1, d), F32)],
        compiler_params=_params(("arbitrary",)),
    )(xmid, g_pre, dn, dres, sub, g_post)


def _loss_head(y, tgt, name):
    s, d = y.shape

    def body(y_ref, t_ref, l_ref, dy_ref):
        @pl.when(pl.program_id(0) == 0)
        def _():
            l_ref[...] = jnp.zeros_like(l_ref)

        e = y_ref[...] - t_ref[...]
        dy_ref[...] = e / d
        per_tok = jnp.mean(e * e, axis=-1, keepdims=True)
        l_ref[...] += 0.5 * jnp.sum(per_tok, axis=0, keepdims=True)

    row = pl.BlockSpec((ROW_TILE, d), lambda i: (i, 0))
    return pl.pallas_call(
        body, name=name, grid=(s // ROW_TILE,),
        in_specs=[row, row],
        out_specs=[pl.BlockSpec((1, 1), lambda i: (0, 0)), row],
        out_shape=[jax.ShapeDtypeStruct((1, 1), F32), jax.ShapeDtypeStruct((s, d), F32)],
        compiler_params=_params(("arbitrary",)),
    )(y, tgt)


SUBLANES = 8


def _shift_down(x, k):
    t, c = x.shape
    r = pltpu.roll(x.reshape(t // SUBLANES, SUBLANES, c), k, axis=1)
    above = jnp.concatenate([jnp.zeros((1, SUBLANES, c), x.dtype), r[:-1]], axis=0)
    rows = lax.broadcasted_iota(jnp.int32, (1, SUBLANES, c), 1)
    return jnp.where(rows >= k, r, above).reshape(t, c)


def _shift_up(x, k):
    t, c = x.shape
    r = pltpu.roll(x.reshape(t // SUBLANES, SUBLANES, c), SUBLANES - k, axis=1)
    below = jnp.concatenate([r[1:], jnp.zeros((1, SUBLANES, c), x.dtype)], axis=0)
    rows = lax.broadcasted_iota(jnp.int32, (1, SUBLANES, c), 1)
    return jnp.where(rows < SUBLANES - k, r, below).reshape(t, c)


def _conv3(h, w):
    return w[2:3] * h + w[1:2] * _shift_down(h, 1) + w[0:1] * _shift_down(h, 2)


def _conv3_bwd(dc, h, w, dw_ref, cols=slice(None)):
    u1, u2 = _shift_up(dc, 1), _shift_up(dc, 2)
    dw_ref[0:1, cols] = jnp.sum(u2 * h, axis=0, keepdims=True)
    dw_ref[1:2, cols] = jnp.sum(u1 * h, axis=0, keepdims=True)
    dw_ref[2:3, cols] = jnp.sum(dc * h, axis=0, keepdims=True)
    return w[2:3] * dc + w[1:2] * u1 + w[0:1] * u2


FFN_PAIRS = N_DEV // 2


def _lane_chunks(width):
    return [(c0, min(COL_TILE, width - c0)) for c0 in range(0, width, COL_TILE)]


def _ffn_up(n, wup, wdw, name):
    s, d = n.shape
    cw = wup.shape[-1]

    def body(n_ref, wg_ref, wu_ref, dg_ref, du_ref, h_ref, c_ref, a_ref):
        x = n_ref[...]
        for c0, size in _lane_chunks(cw):
            cols = slice(c0, c0 + size)
            hg = _dot(x, wg_ref[:, cols])
            hu = _dot(x, wu_ref[:, cols])
            h_ref[0, :, cols] = hg.astype(BF16)
            h_ref[1, :, cols] = hu.astype(BF16)
            cg = _conv3(hg, dg_ref[:, cols])
            cu = _conv3(hu, du_ref[:, cols])
            c_ref[0, :, cols] = cg.astype(BF16)
            c_ref[1, :, cols] = cu.astype(BF16)
            a_ref[:, cols] = (cg * jax.nn.sigmoid(cg) * cu).astype(BF16)

    return pl.pallas_call(
        body, name=name, grid=(FFN_PAIRS,),
        in_specs=[pl.BlockSpec((s, d), lambda j: (0, 0)),
                  pl.BlockSpec((None, d, cw), lambda j: (j, 0, 0)),
                  pl.BlockSpec((None, d, cw), lambda j: (j + FFN_PAIRS, 0, 0)),
                  pl.BlockSpec((None, 3, cw), lambda j: (j, 0, 0)),
                  pl.BlockSpec((None, 3, cw), lambda j: (j + FFN_PAIRS, 0, 0))],
        out_specs=[pl.BlockSpec((None, 2, s, cw), lambda j: (j, 0, 0, 0)),
                   pl.BlockSpec((None, 2, s, cw), lambda j: (j, 0, 0, 0)),
                   pl.BlockSpec((None, s, cw), lambda j: (j, 0, 0))],
        out_shape=[jax.ShapeDtypeStruct((FFN_PAIRS, 2, s, cw), BF16), jax.ShapeDtypeStruct((FFN_PAIRS, 2, s, cw), BF16),
                   jax.ShapeDtypeStruct((FFN_PAIRS, s, cw), BF16)],
        compiler_params=_params(("parallel",)),
    )(n, wup, wup, wdw, wdw)


def _ffn_mid_bwd(do, wdown, h, c, wdw, name):
    s, d = do.shape
    cw = wdown.shape[1]

    def body(do_ref, wd_ref, h_ref, c_ref, wg_ref, wu_ref, dh_ref, dwg_ref, dwu_ref):
        dov = do_ref[...]
        for c0, size in _lane_chunks(cw):
            cols = slice(c0, c0 + size)
            da = _dot(dov, wd_ref[cols, :], NT)
            hg = h_ref[0, :, cols].astype(F32)
            hu = h_ref[1, :, cols].astype(F32)
            wg, wu = wg_ref[:, cols], wu_ref[:, cols]
            cg = c_ref[0, :, cols].astype(F32)
            cu = c_ref[1, :, cols].astype(F32)
            sg = jax.nn.sigmoid(cg)
            dcu = da * (cg * sg)
            dcg = da * cu * (sg * (1.0 + cg * (1.0 - sg)))
            dh_ref[0, :, cols] = _conv3_bwd(dcg, hg, wg, dwg_ref, cols).astype(BF16)
            dh_ref[1, :, cols] = _conv3_bwd(dcu, hu, wu, dwu_ref, cols).astype(BF16)

    vec = jax.ShapeDtypeStruct((FFN_PAIRS, 3, cw), F32)
    return pl.pallas_call(
        body, name=name, grid=(FFN_PAIRS,),
        in_specs=[pl.BlockSpec((s, d), lambda j: (0, 0)), pl.BlockSpec((None, cw, d), lambda j: (j, 0, 0)),
                  pl.BlockSpec((None, 2, s, cw), lambda j: (j, 0, 0, 0)),
                  pl.BlockSpec((None, 2, s, cw), lambda j: (j, 0, 0, 0)),
                  pl.BlockSpec((None, 3, cw), lambda j: (j, 0, 0)),
                  pl.BlockSpec((None, 3, cw), lambda j: (j + FFN_PAIRS, 0, 0))],
        out_specs=[pl.BlockSpec((None, 2, s, cw), lambda j: (j, 0, 0, 0)),
                   pl.BlockSpec((None, 3, cw), lambda j: (j, 0, 0)), pl.BlockSpec((None, 3, cw), lambda j: (j, 0, 0))],
        out_shape=[jax.ShapeDtypeStruct((FFN_PAIRS, 2, s, cw), BF16), vec, vec],
        compiler_params=_params(("parallel",)),
    )(do, wdown, h, c, wdw, wdw)


def _ffn_dwup(n, dh, name):
    s, d = n.shape
    cw = dh.shape[-1]

    def body(n_ref, dh_ref, o_ref):
        o_ref[...] = _dot(n_ref[...], dh_ref[...], TN).astype(BF16)

    return pl.pallas_call(
        body, name=name, grid=(N_DEV,),
        in_specs=[pl.BlockSpec((s, d), lambda k: (0, 0)),
                  pl.BlockSpec((None, None, s, cw), lambda k: (k % FFN_PAIRS, k // FFN_PAIRS, 0, 0))],
        out_specs=pl.BlockSpec((None, d, cw), lambda k: (k, 0, 0)),
        out_shape=jax.ShapeDtypeStruct((N_DEV, d, cw), BF16),
        compiler_params=_params(("parallel",)),
    )(n, dh)


def _ffn_dn(dh, wup, name, dep=None):
    s, cw = dh.shape[-2:]
    d = wup.shape[1]
    tm = _pick(s, MATMUL_TILES)
    deps = [] if dep is None else [dep]

    def body(dh_ref, w_ref, *rest):
        o_ref, acc_ref = rest[-2:]
        k = pl.program_id(1)

        @pl.when(k == 0)
        def _():
            acc_ref[...] = jnp.zeros_like(acc_ref)

        acc_ref[...] += _dot(dh_ref[...], w_ref[...], NT)

        @pl.when(k == N_DEV - 1)
        def _():
            o_ref[...] = acc_ref[...]

    return pl.pallas_call(
        body, name=name, grid=(s // tm, N_DEV),
        in_specs=[pl.BlockSpec((None, None, tm, cw), lambda i, k: (k % FFN_PAIRS, k // FFN_PAIRS, i, 0)),
                  pl.BlockSpec((None, d, cw), lambda i, k: (k, 0, 0))] + [_ANY] * len(deps),
        out_specs=pl.BlockSpec((tm, d), lambda i, k: (i, 0)),
        out_shape=jax.ShapeDtypeStruct((s, d), F32),
        scratch_shapes=[pltpu.VMEM((tm, d), F32)],
        compiler_params=_params(("parallel", "arbitrary")),
    )(dh, wup, *deps)


def _sconv_fwd(n, win, wdw, name):
    s, d = n.shape
    tn = COL_TILE
    nj = d // tn

    def body(n_ref, wb_ref, wc_ref, wh_ref, dw_ref, z_ref, y_ref):
        x = n_ref[...]
        zb = _dot(x, wb_ref[...])
        zc = _dot(x, wc_ref[...])
        zh = _dot(x, wh_ref[...])
        z_ref[0] = zb.astype(BF16)
        z_ref[1] = zc.astype(BF16)
        z_ref[2] = zh.astype(BF16)
        y_ref[...] = (zb * _conv3(zc * zh, dw_ref[...])).astype(BF16)

    return pl.pallas_call(
        body, name=name, grid=(nj,),
        in_specs=[pl.BlockSpec((s, d), lambda j: (0, 0)),
                  pl.BlockSpec((d, tn), lambda j: (0, j)), pl.BlockSpec((d, tn), lambda j: (0, j + nj)),
                  pl.BlockSpec((d, tn), lambda j: (0, j + 2 * nj)), pl.BlockSpec((3, tn), lambda j: (0, j))],
        out_specs=[pl.BlockSpec((3, s, tn), lambda j: (0, 0, j)), pl.BlockSpec((s, tn), lambda j: (0, j))],
        out_shape=[jax.ShapeDtypeStruct((3, s, d), BF16), jax.ShapeDtypeStruct((s, d), BF16)],
        compiler_params=_params(("parallel",)),
    )(n, win, win, win, wdw)


def _sconv_mid_bwd(dm, wout, z, wdw, name):
    s, d = dm.shape
    tn = COL_TILE
    nj = d // tn

    def body(dm_ref, wo_ref, z_ref, w_ref, dz_ref, dw_ref):
        dy = _dot(dm_ref[...], wo_ref[...], NT)
        zb = z_ref[0].astype(F32)
        zc = z_ref[1].astype(F32)
        zh = z_ref[2].astype(F32)
        w = w_ref[...]
        p = zc * zh
        cp = _conv3(p, w)
        dz_ref[0] = (dy * cp).astype(BF16)
        dcp = dy * zb
        dp = _conv3_bwd(dcp, p, w, dw_ref)
        dz_ref[1] = (dp * zh).astype(BF16)
        dz_ref[2] = (dp * zc).astype(BF16)

    return pl.pallas_call(
        body, name=name, grid=(nj,),
        in_specs=[pl.BlockSpec((s, d), lambda j: (0, 0)), pl.BlockSpec((tn, d), lambda j: (j, 0)),
                  pl.BlockSpec((3, s, tn), lambda j: (0, 0, j)), pl.BlockSpec((3, tn), lambda j: (0, j))],
        out_specs=[pl.BlockSpec((3, s, tn), lambda j: (0, 0, j)), pl.BlockSpec((3, tn), lambda j: (0, j))],
        out_shape=[jax.ShapeDtypeStruct((3, s, d), BF16), jax.ShapeDtypeStruct((3, d), F32)],
        compiler_params=_params(("parallel",)),
    )(dm, wout, z, wdw)


def _pool_select(g, c2, c4, c8, c16):
    return jnp.where(g == 0, c2, jnp.where(g == 1, c4, jnp.where(g == 2, c8, c16)))


def _pool_inv_count(g, shape):
    pos = lax.broadcasted_iota(jnp.int32, shape, 0).astype(F32) + 1.0
    win = (2 << g).astype(F32)
    return jnp.minimum(pos, win)


def _pool_fwd(n, win, wgrp, scale, name):
    s, d = n.shape
    tn = POOL_GROUP_DIM

    def body(n_ref, wi_ref, wg_ref, sc_ref, p_ref, y_ref):
        g = pl.program_id(0)
        u = _dot(n_ref[...], wi_ref[...])
        s2 = u + _shift_down(u, 1)
        s4 = s2 + _shift_down(s2, 2)
        s8 = s4 + _shift_down(s4, 4)
        s16 = s8 + _shift_down(s8, 8)
        tot = _pool_select(g, s2, s4, s8, s16)
        p = (tot / _pool_inv_count(g, u.shape) - u).astype(BF16)
        p_ref[...] = p
        y_ref[...] = (_dot(p, wg_ref[...]) * sc_ref[...]).astype(BF16)

    return pl.pallas_call(
        body, name=name, grid=(d // tn,),
        in_specs=[pl.BlockSpec((s, d), lambda g: (0, 0)), pl.BlockSpec((d, tn), lambda g: (0, g)),
                  pl.BlockSpec((None, tn, tn), lambda g: (g, 0, 0)), pl.BlockSpec((1, tn), lambda g: (0, g))],
        out_specs=[pl.BlockSpec((s, tn), lambda g: (0, g)), pl.BlockSpec((s, tn), lambda g: (0, g))],
        out_shape=[jax.ShapeDtypeStruct((s, d), BF16), jax.ShapeDtypeStruct((s, d), BF16)],
        compiler_params=_params(("parallel",)),
    )(n, win, wgrp, scale)


def _pool_mid_bwd(dm, wout, p, wgrp, scale, name):
    s, d = dm.shape
    tn = POOL_GROUP_DIM

    def body(dm_ref, wo_ref, p_ref, wg_ref, sc_ref, du_ref, dwg_ref, dsc_ref):
        g = pl.program_id(0)
        dy = _dot(dm_ref[...], wo_ref[...], NT)
        pv = p_ref[...]
        wg = wg_ref[...]
        ypre = _dot(pv, wg)
        dsc_ref[...] = jnp.sum(dy * ypre, axis=0, keepdims=True)
        dypre = (dy * sc_ref[...]).astype(BF16)
        dwg_ref[...] = _dot(pv, dypre, TN)
        dp = _dot(dypre, wg, NT)
        e = dp / _pool_inv_count(g, dp.shape)
        f2 = e + _shift_up(e, 1)
        f4 = f2 + _shift_up(f2, 2)
        f8 = f4 + _shift_up(f4, 4)
        f16 = f8 + _shift_up(f8, 8)
        du_ref[...] = (_pool_select(g, f2, f4, f8, f16) - dp).astype(BF16)

    return pl.pallas_call(
        body, name=name, grid=(d // tn,),
        in_specs=[pl.BlockSpec((s, d), lambda g: (0, 0)), pl.BlockSpec((tn, d), lambda g: (g, 0)),
                  pl.BlockSpec((s, tn), lambda g: (0, g)), pl.BlockSpec((None, tn, tn), lambda g: (g, 0, 0)),
                  pl.BlockSpec((1, tn), lambda g: (0, g))],
        out_specs=[pl.BlockSpec((s, tn), lambda g: (0, g)), pl.BlockSpec((None, tn, tn), lambda g: (g, 0, 0)),
                   pl.BlockSpec((1, tn), lambda g: (0, g))],
        out_shape=[jax.ShapeDtypeStruct((s, d), BF16), jax.ShapeDtypeStruct((4, tn, tn), F32),
                   jax.ShapeDtypeStruct((1, d), F32)],
        compiler_params=_params(("parallel",)),
    )(dm, wout, p, wgrp, scale)


PANEL = LANES
ATTN_EXT = ATTN_WIDTH + PANEL
DVEC_LANE = HEADS


def _alibi_slopes(g, dil):
    all_slopes = 2.0 ** (-8.0 * np.arange(1, N_HEADS_A + 1) / N_HEADS_A)
    return [float(np.float32(sl) * np.float32(dil)) for sl in all_slopes[g * HEADS:(g + 1) * HEADS]]


def _residue_order(a, dil, name):
    s, w = a.shape
    per = ROW_TILE // dil
    panels = w // PANEL

    def body(a_ref, o_ref, *tiles):
        for c in range(panels):
            cols = slice(c * PANEL, (c + 1) * PANEL)
            tiles[c][...] = a_ref[:, cols].astype(F32)
            for r in range(dil):
                o_ref[r, :, cols] = tiles[c][pl.ds(r, per, stride=dil), :].astype(o_ref.dtype)

    out = pl.pallas_call(
        body, name=name, grid=(s // ROW_TILE,),
        in_specs=[pl.BlockSpec((ROW_TILE, w), lambda i: (i, 0))],
        out_specs=pl.BlockSpec((dil, per, w), lambda i: (0, i, 0)),
        out_shape=jax.ShapeDtypeStruct((dil, s // dil, w), a.dtype),
        scratch_shapes=[pltpu.VMEM((ROW_TILE, PANEL), F32)] * panels,
        compiler_params=_params(("parallel",)),
    )(a)
    return out.reshape(s, w)


def _token_order(a, dil, acc, name):
    s, w = a.shape
    per = ROW_TILE // dil
    panels = w // PANEL
    has_acc = acc is not None

    def body(*refs):
        a_ref = refs[0]
        o_ref = refs[2] if has_acc else refs[1]
        tiles = refs[3:] if has_acc else refs[2:]
        for c in range(panels):
            cols = slice(c * PANEL, (c + 1) * PANEL)
            for r in range(dil):
                tiles[c][pl.ds(r, per, stride=dil), :] = a_ref[r, :, cols]
            v = tiles[c][...]
            if has_acc:
                v = v + refs[1][:, cols]
            o_ref[:, cols] = v

    row = pl.BlockSpec((ROW_TILE, w), lambda i: (i, 0))
    return pl.pallas_call(
        body, name=name, grid=(s // ROW_TILE,),
        in_specs=[pl.BlockSpec((dil, per, w), lambda i: (0, i, 0))] + ([row] if has_acc else []),
        out_specs=row, out_shape=jax.ShapeDtypeStruct((s, w), F32),
        scratch_shapes=[pltpu.VMEM((ROW_TILE, PANEL), F32)] * panels,
        compiler_params=_params(("parallel",)),
    )(*([a.reshape(dil, s // dil, w)] + ([acc] if has_acc else [])))


def _qkv_proj(n, wqkv, g, name):
    s, d = n.shape
    tm = _pick(s, MATMUL_TILES)

    def body(a_ref, b_ref, o_ref):
        o_ref[...] = _dot(a_ref[...], b_ref[...]).astype(BF16)

    return pl.pallas_call(
        body, name=name, grid=(s // tm, 3),
        in_specs=[pl.BlockSpec((tm, d), lambda i, t: (i, 0)),
                  pl.BlockSpec((d, ATTN_WIDTH), lambda i, t: (0, 3 * g + t))],
        out_specs=pl.BlockSpec((None, tm, ATTN_WIDTH), lambda i, t: (t, i, 0)),
        out_shape=jax.ShapeDtypeStruct((3, s, ATTN_WIDTH), BF16),
        compiler_params=_params(("parallel", "parallel")),
    )(n, wqkv)


def _attn_window(n, ln):
    if ln == BLOCK:
        return 0, BLOCK
    return pl.multiple_of(jnp.maximum(n - 1, 0) * BLOCK, BLOCK), 2 * BLOCK


def _attn_mask(n, k0, kw):
    qpos = n * BLOCK + lax.broadcasted_iota(jnp.int32, (BLOCK, kw), 0)
    kpos = k0 + lax.broadcasted_iota(jnp.int32, (BLOCK, kw), 1)
    dist = qpos - kpos
    return dist.astype(F32), (dist >= 0) & (dist <= BLOCK)


def _attn_scores(q, keys, slope, dist, valid):
    s = _dot(q, keys, NT) * (HEAD_DIM ** -0.5) - slope * dist
    return jnp.where(valid, s, NEG_INF)


ATTN_STEP_BLOCKS = 1
ATTN_BWD_STEP_BLOCKS = 4


def _attn_block(gb, ln):
    nb = ln // BLOCK
    n, base = (0, gb * ln) if nb == 1 else (gb % nb, (gb // nb) * ln)
    k0, kw = _attn_window(n, ln)
    cur = pl.ds(pl.multiple_of(gb * BLOCK, BLOCK), BLOCK)
    win = pl.ds(pl.multiple_of(base + k0, BLOCK), kw)
    return cur, win, n, k0, kw


def _attn_fwd(qkv, g, name):
    _, s, w = qkv.shape
    dil = DILATED_CFG[g][1]
    ln = s // dil
    slopes = _alibi_slopes(g, dil)
    rows = ATTN_STEP_BLOCKS * BLOCK

    def body(qkv_ref, o_ref):
        o_ref[:, w:] = jnp.zeros((rows, PANEL), F32)
        for b in range(ATTN_STEP_BLOCKS):
            cur, win, n, k0, kw = _attn_block(pl.program_id(0) * ATTN_STEP_BLOCKS + b, ln)
            dist, valid = _attn_mask(n, k0, kw)
            out = slice(b * BLOCK, (b + 1) * BLOCK)
            for h in range(HEADS):
                cols = slice(h * HEAD_DIM, (h + 1) * HEAD_DIM)
                sc = _attn_scores(qkv_ref[0, cur, cols], qkv_ref[1, win, cols], slopes[h], dist, valid)
                m = jnp.max(sc, axis=-1, keepdims=True)
                p = jnp.exp(sc - m)
                den = jnp.sum(p, axis=-1, keepdims=True)
                o_ref[out, cols] = _dot(p.astype(BF16), qkv_ref[2, win, cols]) / den
                o_ref[out, w + h:w + h + 1] = m + jnp.log(den)

    return pl.pallas_call(
        body, name=name, grid=(s // rows,),
        in_specs=[pl.BlockSpec((3, s, w), lambda i: (0, 0, 0))],
        out_specs=pl.BlockSpec((rows, ATTN_EXT), lambda i: (i, 0)),
        out_shape=jax.ShapeDtypeStruct((s, ATTN_EXT), F32),
        compiler_params=_params(("parallel",)),
    )(qkv)


def _attn_bwd(qkv, dext, g, name, dep=None):
    _, s, w = qkv.shape
    dil = DILATED_CFG[g][1]
    ln = s // dil
    slopes = _alibi_slopes(g, dil)
    scale = HEAD_DIM ** -0.5
    rows = ATTN_BWD_STEP_BLOCKS * BLOCK
    steps = s // rows
    deps = [] if dep is None else [dep]

    def body(qkv_ref, de_ref, *rest):
        d_ref, dk_ref, dv_ref = rest[-3:]

        @pl.when(pl.program_id(0) == 0)
        def _():
            dk_ref[...] = jnp.zeros_like(dk_ref)
            dv_ref[...] = jnp.zeros_like(dv_ref)

        for b in range(ATTN_BWD_STEP_BLOCKS):
            cur, win, n, k0, kw = _attn_block(pl.program_id(0) * ATTN_BWD_STEP_BLOCKS + b, ln)
            dist, valid = _attn_mask(n, k0, kw)
            blk = slice(b * BLOCK, (b + 1) * BLOCK)
            for h in range(HEADS):
                cols = slice(h * HEAD_DIM, (h + 1) * HEAD_DIM)
                q, keys = qkv_ref[0, cur, cols], qkv_ref[1, win, cols]
                dob = de_ref[blk, cols].astype(BF16)
                p = jnp.exp(_attn_scores(q, keys, slopes[h], dist, valid) - de_ref[blk, w + h:w + h + 1])
                dd = de_ref[blk, w + DVEC_LANE + h:w + DVEC_LANE + h + 1]
                ds = (p * (_dot(dob, qkv_ref[2, win, cols], NT) - dd)).astype(BF16)
                d_ref[0, cur, cols] = (scale * _dot(ds, keys)).astype(BF16)
                dv_ref[win, cols] += _dot(p.astype(BF16), dob, TN)
                dk_ref[win, cols] += scale * _dot(ds, q, TN)

        @pl.when(pl.program_id(0) == steps - 1)
        def _():
            d_ref[1] = dk_ref[...].astype(BF16)
            d_ref[2] = dv_ref[...].astype(BF16)

    whole = pl.BlockSpec((3, s, w), lambda i: (0, 0, 0))
    return pl.pallas_call(
        body, name=name, grid=(steps,),
        in_specs=[whole, pl.BlockSpec((rows, ATTN_EXT), lambda i: (i, 0))] + [_ANY] * len(deps),
        out_specs=whole, out_shape=jax.ShapeDtypeStruct((3, s, w), BF16),
        scratch_shapes=[pltpu.VMEM((s, w), F32), pltpu.VMEM((s, w), F32)],
        compiler_params=_params(("arbitrary",)),
    )(qkv, dext, *deps)


def _attn_merge(e0, e1, e2, name):
    s = e0.shape[0]
    w = ATTN_WIDTH

    def body(e0_ref, e1_ref, e2_ref, m_ref, mb_ref, lse_ref):
        refs = (e0_ref, e1_ref, e2_ref)
        l = [r[:, w:w + HEADS] for r in refs]
        mx = jnp.maximum(jnp.maximum(l[0], l[1]), l[2])
        e = [jnp.exp(v - mx) for v in l]
        z = e[0] + e[1] + e[2]
        lse_ref[...] = mx + jnp.log(z)
        wts = [v / z for v in e]
        for h in range(HEADS):
            cols = slice(h * HEAD_DIM, (h + 1) * HEAD_DIM)
            acc = wts[0][:, h:h + 1] * refs[0][:, cols]
            for g in range(1, N_GROUPS_A):
                acc = acc + wts[g][:, h:h + 1] * refs[g][:, cols]
            m_ref[:, cols] = acc
            mb_ref[:, cols] = acc.astype(BF16)

    ext = pl.BlockSpec((ROW_TILE, ATTN_EXT), lambda i: (i, 0))
    row = pl.BlockSpec((ROW_TILE, w), lambda i: (i, 0))
    return pl.pallas_call(
        body, name=name, grid=(s // ROW_TILE,),
        in_specs=[ext, ext, ext],
        out_specs=[row, row, pl.BlockSpec((ROW_TILE, HEADS), lambda i: (i, 0))],
        out_shape=[jax.ShapeDtypeStruct((s, w), F32), jax.ShapeDtypeStruct((s, w), BF16),
                   jax.ShapeDtypeStruct((s, HEADS), F32)],
        compiler_params=_params(("parallel",)),
    )(e0, e1, e2)


def _attn_dvec(dmerged, merged, lse_all, name, dep=None):
    s, w = merged.shape
    deps = [] if dep is None else [dep]

    def body(dm_ref, m_ref, lse_ref, *rest):
        de_ref = rest[-1]
        dmv = dm_ref[...]
        de_ref[:, :w] = dmv
        de_ref[:, w:] = jnp.zeros((ROW_TILE, PANEL), F32)
        de_ref[:, w:w + HEADS] = lse_ref[...]
        prod = dmv * m_ref[...]
        for h in range(HEADS):
            lane = w + DVEC_LANE + h
            de_ref[:, lane:lane + 1] = jnp.sum(prod[:, h * HEAD_DIM:(h + 1) * HEAD_DIM], axis=-1, keepdims=True)

    row = pl.BlockSpec((ROW_TILE, w), lambda i: (i, 0))
    return pl.pallas_call(
        body, name=name, grid=(s // ROW_TILE,),
        in_specs=[row, row, pl.BlockSpec((ROW_TILE, HEADS), lambda i: (i, 0))] + [_ANY] * len(deps),
        out_specs=pl.BlockSpec((ROW_TILE, ATTN_EXT), lambda i: (i, 0)),
        out_shape=jax.ShapeDtypeStruct((s, ATTN_EXT), F32),
        compiler_params=_params(("parallel",)),
    )(dmerged, merged, lse_all, *deps)


def _attention_fwd(n, wqkv, wo, tag):
    ns, qkvs, exts = [], [], []
    for g, (_, dil) in enumerate(DILATED_CFG):
        ng = n if dil == 1 else _residue_order(n, dil, f"{tag}_order_g{g}")
        qkv = _qkv_proj(ng, wqkv, g, f"{tag}_qkv_g{g}")
        ext = _attn_fwd(qkv, g, f"{tag}_fwd_g{g}")
        ns.append(ng)
        qkvs.append(qkv)
        exts.append(ext if dil == 1 else _token_order(ext, dil, None, f"{tag}_unorder_g{g}"))
    merged, merged_bf, lse_all = _attn_merge(*exts, f"{tag}_merge")
    m = _matmul(merged_bf, wo, "nn", F32, f"{tag}_wo")
    return m, (ns, qkvs, merged, merged_bf, lse_all)


def _attention_bwd(dm, wqkv, wo, saved, tag, dep=None, hook=None):
    ns, qkvs, merged, merged_bf, lse_all = saved
    d_wo = _matmul(merged_bf, dm, "tn", BF16, f"{tag}_dwo")
    dmerged = _matmul(dm, wo, "nt", F32, f"{tag}_dmerged")
    dext = _attn_dvec(dmerged, merged, lse_all, f"{tag}_dvec", dep)
    width = 3 * ATTN_WIDTH
    d_wqkv, dn, dep = [], None, None
    for g, (_, dil) in enumerate(DILATED_CFG):
        dext_g = dext if dil == 1 else _residue_order(dext, dil, f"{tag}_dorder_g{g}")
        dqkv = _attn_bwd(qkvs[g], dext_g, g, f"{tag}_bwd_g{g}", dep)
        dep = hook(g, dqkv) if hook is not None and g + 1 < N_GROUPS_A else None
        d_wqkv.append(_matmul(ns[g], dqkv, "tn", BF16, f"{tag}_dwqkv_g{g}", b_parts=3))
        dn_g = _matmul(dqkv, wqkv[:, g * width:(g + 1) * width], "nt", F32, f"{tag}_dn_g{g}", a_parts=3)
        dn = dn_g if dil == 1 else _token_order(dn_g, dil, dn, f"{tag}_dn_sum_g{g}")
    return dn, jnp.concatenate(d_wqkv, axis=1), d_wo


def _layer_matrices(i):
    mixer = (("attn_w_qkv", "attn_w_o"), ("conv_w_in", "conv_w_out"), ("pool_w_in", "pool_w_grp", "pool_w_out"))[i % 3]
    return [(k, i // 3) for k in mixer] + [("ffn_w_up", i), ("ffn_w_down", i)]


def _local_step(x, tgt, vec, weights, sink):
    ng = vec["norm_g"]

    def gain(i, j, token=None):
        g = ng[i, j][None, :]
        return g if token is None else g + token

    saved = []
    n = _rms_fwd(x, gain(0, 0), None, BF16, "norm_first")
    for i in range(DEPTH):
        wl = weights.layer(i)
        t0 = weights.hook(i, 0, n)
        kind, idx = i % 3, i // 3
        if kind == 0:
            m, ms = _attention_fwd(n, wl["attn_w_qkv"], wl["attn_w_o"], "attn")
        elif kind == 1:
            taps = vec["conv_w_dw"][idx] if t0 is None else vec["conv_w_dw"][idx] + t0
            z, y = _sconv_fwd(n, wl["conv_w_in"], taps, "sconv_fwd")
            m = _matmul(y, wl["conv_w_out"], "nn", F32, "sconv_out")
            ms = (z, y)
        else:
            scale = vec["pool_scale"][idx][None, :] if t0 is None else vec["pool_scale"][idx][None, :] + t0
            p, y = _pool_fwd(n, wl["pool_w_in"], wl["pool_w_grp"], scale, "pool_fwd")
            m = _matmul(y, wl["pool_w_out"], "nn", F32, "pool_out")
            ms = (p, y)
        t1 = weights.hook(i, 1, m)
        x1, n2 = _rms_res_pre(m, gain(i, 1, t0), x, gain(i, 2, t1), "norm_res_pre")
        h, c, a = _ffn_up(n2, wl["ffn_w_up"], vec["ffn_w_dw"][i], "ffn_up")
        t2 = weights.hook(i, 2, a)
        f = _matmul(a, wl["ffn_w_down"].reshape(D_FF, D_MODEL), "nn", F32, "ffn_down", a_parts=FFN_PAIRS)
        saved.append((x, n, m, ms, x1, n2, h, a, f, wl, c))
        if i + 1 < DEPTH:
            x, n = _rms_res_pre(f, gain(i, 3, t2), x1, gain(i + 1, 0), "norm_res_pre")
        else:
            x = _rms_fwd(f, gain(i, 3), x1, F32, "norm_res")
        weights.hook(i, 3, x)

    loss, dx = _loss_head(x, tgt, "loss_head")

    g_norm = [[None] * 4 for _ in range(DEPTH)]
    g_taps, g_scale, g_ffn_dw = [], [], [None] * DEPTH
    df, g_norm[DEPTH - 1][3] = _rms_bwd(saved[-1][8], gain(DEPTH - 1, 3), dx, None, BF16, "norm_bwd_sub")
    t0 = None
    for i in reversed(range(DEPTH)):
        xin, n, m, ms, x1, n2, h, a, f, wl, c = saved[i]
        kind, idx = i % 3, i // 3
        gl = {}
        d_wdown = _matmul(a, df, "tn", BF16, "ffn_dwdown", a_parts=FFN_PAIRS)
        gl["ffn_w_down"] = d_wdown.reshape(N_DEV, D_FF // N_DEV, D_MODEL)
        ffn_taps = vec["ffn_w_dw"][i] if t0 is None else vec["ffn_w_dw"][i] + t0
        dh, dwg, dwu = _ffn_mid_bwd(df, wl["ffn_w_down"].reshape(FFN_PAIRS, -1, D_MODEL), h, c, ffn_taps, "ffn_mid_bwd")
        g_ffn_dw[i] = jnp.concatenate([dwg, dwu], axis=0)
        t1 = sink.hook(i, 1, dh)
        gl["ffn_w_up"] = _ffn_dwup(n2, dh, "ffn_dwup")
        tf = sink.ffn_done(i, gl)
        dn2 = _ffn_dn(dh, wl["ffn_w_up"], "ffn_dn", t1)
        dx1, dm, g_norm[i][2], g_norm[i][1] = _rms_bwd_pair(x1, gain(i, 2, tf), dn2, dx, m, gain(i, 1), "norm_bwd_pair")
        t2 = sink.hook(i, 2, dm)
        if kind == 0:
            dn, gl["attn_w_qkv"], gl["attn_w_o"] = _attention_bwd(
                dm, wl["attn_w_qkv"], wl["attn_w_o"], ms, "attn", t2, lambda g, after, i=i: sink.hook(i, ("a", "b")[g], after))
        elif kind == 1:
            z, y = ms
            gl["conv_w_out"] = _matmul(y, dm, "tn", BF16, "sconv_dwout")
            taps = vec["conv_w_dw"][idx] if t2 is None else vec["conv_w_dw"][idx] + t2
            dz, ddw = _sconv_mid_bwd(dm, wl["conv_w_out"], z, taps, "sconv_mid_bwd")
            g_taps.append(ddw)
            gl["conv_w_in"] = _matmul(n, dz, "tn", BF16, "sconv_dwin", b_parts=3)
            dn = _matmul(dz, wl["conv_w_in"], "nt", F32, "sconv_dn", a_parts=3)
        else:
            p, y = ms
            gl["pool_w_out"] = _matmul(y, dm, "tn", BF16, "pool_dwout")
            scale = vec["pool_scale"][idx][None, :] if t2 is None else vec["pool_scale"][idx][None, :] + t2
            du, gl["pool_w_grp"], dscale = _pool_mid_bwd(dm, wl["pool_w_out"], p, wl["pool_w_grp"], scale, "pool_mid_bwd")
            g_scale.append(dscale[0])
            gl["pool_w_in"] = _matmul(n, du, "tn", BF16, "pool_dwin")
            dn = _matmul(du, wl["pool_w_in"], "nt", F32, "pool_dn")
        sink.hook(i, 3, dn)
        if i > 0:
            dx, df, g_norm[i][0], g_norm[i - 1][3] = _rms_bwd_pair(xin, gain(i, 0, t2), dn, dx1, saved[i - 1][8],
                                                                   gain(i - 1, 3), "norm_bwd_pair")
        else:
            dx, g_norm[0][0] = _rms_bwd(xin, gain(0, 0), dn, dx1, F32, "norm_bwd_res")
        t0 = sink.layer_done(i, gl)

    vec_grads = {"norm_g": jnp.stack([jnp.concatenate(row, axis=0) for row in g_norm]), "conv_w_dw": jnp.stack(g_taps),
                 "pool_scale": jnp.stack(g_scale), "ffn_w_dw": g_ffn_dw}
    return loss, dx, vec_grads


_AXES = ("x", "y", "c")
ROUTE_A = ("y", "x", "c")
ROUTE_B = ("x", "y", "c")
def _dev_index(pos):
    return 4 * pos["x"] + 2 * pos["y"] + pos["c"]


_HBM = pl.BlockSpec(memory_space=pltpu.HBM)
_SEM = pl.BlockSpec(memory_space=pltpu.SEMAPHORE)
_ANY = pl.BlockSpec(memory_space=pl.ANY)
_EFFECT = pltpu.SideEffectType.DATAFLOW_SIDE_EFFECTING


TOKEN_SHAPE = (1, D_MODEL)


def _copies_start(describe, arrays, n_copies, name, after, token_shape=TOKEN_SHAPE):
    n = len(arrays)
    deps = [] if after is None else [after]

    def body(*refs):
        send_sems, recv_sems = refs[n + len(deps)], refs[n + len(deps) + 1]
        for c in describe(refs[:n], send_sems, recv_sems):
            c.start()
        refs[-1][...] = jnp.zeros_like(refs[-1])

    outs = pl.pallas_call(
        body, name=f"{name}_start",
        out_shape=(pltpu.SemaphoreType.DMA((n_copies,)), pltpu.SemaphoreType.DMA((n_copies,)),
                   *[pltpu.HBM(a.shape, a.dtype) for a in arrays], jax.ShapeDtypeStruct(token_shape, F32)),
        in_specs=[_HBM] * n + [_ANY] * len(deps),
        out_specs=(_SEM, _SEM, *([_HBM] * n), pl.BlockSpec(memory_space=pltpu.VMEM)),
        input_output_aliases={i: 2 + i for i in range(n)},
        compiler_params=pltpu.CompilerParams(has_side_effects=_EFFECT),
    )(*[pltpu.with_memory_space_constraint(a, pltpu.HBM) for a in arrays], *deps)
    return (outs[0], outs[1], list(outs[2:2 + n])), outs[-1]


def _copies_wait(describe, handle, name, after):
    send_sems, recv_sems, arrays = handle
    n = len(arrays)
    deps = [] if after is None else list(after) if isinstance(after, (list, tuple)) else [after]

    def body(*refs):
        for c in describe(refs[:n], refs[n], refs[n + 1]):
            c.wait_send()
            c.wait_recv()

    outs = pl.pallas_call(
        body, name=f"{name}_wait",
        out_shape=tuple(pltpu.HBM(a.shape, a.dtype) for a in arrays),
        in_specs=[_HBM] * n + [_SEM, _SEM] + [_ANY] * len(deps), out_specs=tuple([_HBM] * n),
        input_output_aliases={i: i for i in range(n)},
        compiler_params=pltpu.CompilerParams(has_side_effects=_EFFECT),
    )(*arrays, send_sems, recv_sems, *deps)
    return list(outs)


GATHER_STAGE_COPIES = (3, 3, 1)


def _gather_copies(stage, routes):
    n = len(routes)

    def describe(refs, send_sems, recv_sems):
        pos = {a: lax.axis_index(a) for a in _AXES}

        def flipped(axes):
            return {a: 1 - pos[a] if a in axes else pos[a] for a in _AXES}

        copies = []
        for i, (a1, a2, a3) in enumerate(routes):
            land = refs[n + i] if stage == 1 else refs[i]
            p1, p2, p12, p3 = flipped((a1,)), flipped((a2,)), flipped((a1, a2)), flipped((a3,))
            plan = {1: [(None, p1), (None, p2), (None, p3)], 2: [(p1, p2), (p1, p3), (p2, p3)], 3: [(p12, p3)]}[stage]
            for holder, to in plan:
                slot = land.at[_dev_index(pos if holder is None else holder)]
                k = len(copies)
                copies.append(pltpu.make_async_remote_copy(
                    src_ref=refs[i] if holder is None else slot, dst_ref=slot,
                    send_sem=send_sems.at[k], recv_sem=recv_sems.at[k],
                    device_id=tuple(to[a] for a in _AXES), device_id_type=pl.DeviceIdType.MESH))
        return copies

    return describe


def _gather_begin(shards, routes, name, after):
    n = len(shards)
    lands = [lax.empty((N_DEV,) + a.shape, a.dtype) for a in shards]
    handle, token = _copies_start(_gather_copies(1, routes), list(shards) + lands, GATHER_STAGE_COPIES[0] * n,
                                  f"{name}_1", after)
    return {"stage": 1, "handle": handle, "routes": routes, "name": name, "n": n}, token


def _gather_next(state, after):
    stage, routes, name, n = state["stage"], state["routes"], state["name"], state["n"]
    arrays = _copies_wait(_gather_copies(stage, routes), state["handle"], f"{name}_{stage}", after)
    if stage == 1:
        state = dict(state, shards=arrays[:n])
        arrays = arrays[n:]
    if stage == 3:
        me = _dev_index({a: lax.axis_index(a) for a in _AXES})
        return [lax.dynamic_update_index_in_dim(o, s, me, 0) for o, s in zip(arrays, state["shards"])], None
    handle, token = _copies_start(_gather_copies(stage + 1, routes), arrays, GATHER_STAGE_COPIES[stage] * n,
                                  f"{name}_{stage + 1}", None)
    return dict(state, stage=stage + 1, handle=handle), token


ADD_ROW_TILES = (1024, 704, 512, 352, 256, 128, 96, 64, 32, 16)


def _add_half(a, recv, me, out_dtype, name):
    p, q, cols = recv.shape
    tr = _pick(q, ADD_ROW_TILES)

    def body(me_ref, a_ref, b_ref, o_ref):
        o_ref[...] = (a_ref[...].astype(F32) + b_ref[...].astype(F32)).astype(o_ref.dtype)

    return pl.pallas_call(
        body, name=name,
        grid_spec=pltpu.PrefetchScalarGridSpec(
            num_scalar_prefetch=1, grid=(p, q // tr),
            in_specs=[pl.BlockSpec((None, None, tr, cols), lambda j, i, m: (j, m[0], i, 0)),
                      pl.BlockSpec((None, tr, cols), lambda j, i, m: (j, i, 0))],
            out_specs=pl.BlockSpec((None, tr, cols), lambda j, i, m: (j, i, 0))),
        out_shape=jax.ShapeDtypeStruct((p, q, cols), out_dtype),
        compiler_params=_params(("parallel", "parallel")),
    )(me, a, recv)


def _half_copies(axes):
    n = len(axes)

    def describe(refs, send_sems, recv_sems):
        pos = {a: lax.axis_index(a) for a in _AXES}
        copies = []
        for i, axis in enumerate(axes):
            peer = tuple(1 - pos[a] if a == axis else pos[a] for a in _AXES)
            copies.append(pltpu.make_async_remote_copy(
                src_ref=refs[i].at[:, 1 - pos[axis]], dst_ref=refs[n + i], send_sem=send_sems.at[i],
                recv_sem=recv_sems.at[i], device_id=peer, device_id_type=pl.DeviceIdType.MESH))
        return copies

    return describe


def _scatter_begin(slots, routes, tags, name, token_shape=TOKEN_SHAPE):
    shapes = [a.shape[1:] for a in slots]
    rows = [math.prod(s[:-1]) for s in shapes]
    arrays = [a.reshape(4, 2, n, s[-1]) for a, n, s in zip(slots, rows, shapes)]
    return _scatter_start({"stage": 0, "arrays": arrays, "routes": routes, "tags": tags, "name": name,
                           "shapes": shapes, "rows": rows}, token_shape)


def _scatter_start(state, token_shape=TOKEN_SHAPE):
    stage, arrays = state["stage"], state["arrays"]
    axes = [r[2 - stage] for r in state["routes"]]
    lands = [lax.empty((a.shape[0],) + a.shape[2:], a.dtype) for a in arrays]
    handle, token = _copies_start(_half_copies(axes), arrays + lands, len(arrays), f"{state['name']}_{stage + 1}", None,
                                  token_shape)
    return dict(state, handle=handle, axes=axes), token


def _scatter_next(state, after):
    stage, axes, n = state["stage"], state["axes"], len(state["arrays"])
    both = _copies_wait(_half_copies(axes), state["handle"], f"{state['name']}_{stage + 1}", after)
    coord = {a: lax.axis_index(a).astype(jnp.int32).reshape(1) for a in _AXES}
    sums = [_add_half(a, r, coord[ax], F32 if stage == 2 else BF16, f"scatter_add_{stage + 1}_{t}")
            for a, r, ax, t in zip(both[:n], both[n:], axes, state["tags"])]
    if stage == 2:
        return [a.reshape(s) for a, s in zip(sums, state["shapes"])], None
    if stage == 0:
        views = [(1, 2, 2 * r, s[-1]) if route[1] == "x" else (2, 2, r, s[-1])
                 for r, s, route in zip(state["rows"], state["shapes"], state["routes"])]
    else:
        views = [(1, 2, r, s[-1]) for r, s in zip(state["rows"], state["shapes"])]
    return _scatter_start(dict(state, stage=stage + 1, arrays=[a.reshape(v) for a, v in zip(sums, views)]))


_WEIGHTS = {
    "norm_g": ((DEPTH, 4, D_MODEL), 2, True),
    "attn_w_qkv": ((2, D_MODEL, 4608), 2, False),
    "attn_w_o": ((2, ATTN_WIDTH, D_MODEL), 2, False),
    "conv_w_in": ((1, D_MODEL, 3 * D_MODEL), 2, False),
    "conv_w_dw": ((1, 3, D_MODEL), 2, True),
    "conv_w_out": ((1, D_MODEL, D_MODEL), 1, False),
    "pool_w_in": ((1, D_MODEL, D_MODEL), 1, False),
    "pool_w_grp": ((1, 4, POOL_GROUP_DIM, POOL_GROUP_DIM), 2, False),
    "pool_scale": ((1, D_MODEL), 1, True),
    "pool_w_out": ((1, D_MODEL, D_MODEL), 1, False),
    "ffn_w_up": ((DEPTH, D_MODEL, 2 * D_FF), 2, False),
    "ffn_w_dw": ((DEPTH, 3, 2 * D_FF), 2, True),
    "ffn_w_down": ((DEPTH, D_FF, D_MODEL), 1, False),
}
_NAMES = tuple(_WEIGHTS)
_VECTORS = tuple(k for k in _NAMES if _WEIGHTS[k][2])
_MATRICES = tuple(k for k in _NAMES if not _WEIGHTS[k][2])
_FFN = ("ffn_w_up", "ffn_w_down")
_ON_ROUTE_A = ("ffn_w_up", "attn_w_o", "conv_w_out", "pool_w_in")
PACK_ROWS = 16


def _route(name):
    return ROUTE_A if name in _ON_ROUTE_A else ROUTE_B


def _shard_shape(name):
    shape, ax, _ = _WEIGHTS[name]
    return tuple(s // N_DEV if i == ax else s for i, s in enumerate(shape))


def _full_from_slots(slots, name, layers=None):
    shape, ax, _ = _WEIGHTS[name]
    if layers is not None:
        shape = (layers,) + shape[1:]
    return jnp.moveaxis(slots, 0, ax).reshape(shape)


def _slots_from_full(full, name):
    shape, ax, _ = _WEIGHTS[name]
    split = shape[:ax] + (N_DEV, shape[ax] // N_DEV) + shape[ax + 1:]
    return jnp.moveaxis(full.reshape(split), ax, 0)


def _pack_vectors(parts, lead):
    rows = []
    for k in _VECTORS:
        r = parts[k].reshape(lead + (-1, LANES))
        pad = -r.shape[-2] % PACK_ROWS
        rows.append(jnp.pad(r, [(0, 0)] * len(lead) + [(0, pad), (0, 0)]))
    return jnp.concatenate(rows, axis=len(lead))


def _unpack_vectors(buf, lead):
    out, r0 = {}, 0
    for k in _VECTORS:
        shard = _shard_shape(k)
        rows = math.prod(shard) // LANES
        out[k] = buf[..., r0:r0 + rows, :].reshape(lead + shard)
        r0 += rows + (-rows % PACK_ROWS)
    return out


class _LayerWeights:
    def __init__(self, shards):
        self.cast = {k: shards[k].astype(BF16) for k in _MATRICES}
        first, ffn0 = _layer_matrices(0)[:-2], _layer_matrices(0)[-2:]
        state, _ = _gather_begin(self._send(first) + [_pack_vectors(shards, ())],
                                 [_route(k) for k, _ in first] + [ROUTE_B], "gather0", None)
        state, _ = _gather_next(state, self._send(ffn0) + self._send(_layer_matrices(1)))
        state, _ = _gather_next(state, None)
        outs, _ = _gather_next(state, None)
        vec = _unpack_vectors(outs[-1], (N_DEV,))
        self.vec = {k: _full_from_slots(vec[k], k) for k in _VECTORS}
        self.vec["ffn_w_dw"] = [vec["ffn_w_dw"][:, l] for l in range(DEPTH)]
        self.ready = {0: self._unpack(first, outs[:-1])}
        self.chains = {}
        tokens = []
        self._begin("ffn0", ffn0, "gather0f", outs[0], tokens)
        self._begin(1, _layer_matrices(1), "gather1", outs[0], tokens)
        self.vec["norm_g"] = self.vec["norm_g"] + (tokens[0] + tokens[1])

    def _send(self, items):
        return [self.cast[k][j] for k, j in items]

    @staticmethod
    def _unpack(items, outs):
        return {k: o if k in _FFN else _full_from_slots(o[:, None], k, layers=1)[0] for (k, _), o in zip(items, outs)}

    def _begin(self, key, items, name, after, tokens):
        state, token = _gather_begin(self._send(items), [_route(k) for k, _ in items], name, after)
        self.chains[key] = (items, state)
        tokens.append(token)

    def _advance(self, key, after, tokens):
        items, state = self.chains.pop(key)
        state, token = _gather_next(state, after)
        if token is None:
            self.ready.setdefault(0 if key == "ffn0" else key, {}).update(self._unpack(items, state))
        else:
            self.chains[key] = (items, state)
            tokens.append(token)

    def layer(self, i):
        return self.ready[i]

    def hook(self, i, point, after):
        tokens = []
        if i == 0 and point == 0:
            self._advance("ffn0", after, tokens)
        if i == 0 and point == 1:
            self._advance("ffn0", after, tokens)
            self._advance("ffn0", None, tokens)
        if point >= 1 and i + 1 in self.chains:
            self._advance(i + 1, after, tokens)
        if point == 1 and i + 2 < DEPTH:
            self._begin(i + 2, _layer_matrices(i + 2), f"gather{i + 2}", after, tokens)
        return functools.reduce(lambda a, b: a + b, tokens) if tokens else None


def _layer_slots(g, name):
    shape, ax, _ = _WEIGHTS[name]
    shape, ax = shape[1:], ax - 1
    split = shape[:ax] + (N_DEV, shape[ax] // N_DEV) + shape[ax + 1:]
    return jnp.moveaxis(g.reshape(split), ax, 0).astype(BF16)


class _GradSink:
    def __init__(self):
        self.state = None
        self.ffn_state = None
        self.sums = {}
        self.last = None

    def ffn_done(self, i, grads):
        if i != 0:
            return None
        self.ffn_items = _layer_matrices(0)[-2:]
        self.ffn_state, token = _scatter_begin([grads[k] for k, _ in self.ffn_items],
                                               [_route(k) for k, _ in self.ffn_items],
                                               [f"{k}{j}" for k, j in self.ffn_items], "scatter0f")
        return token

    def layer_done(self, i, grads):
        items = _layer_matrices(i)
        if i == 0:
            items = items[:-2]
            self.last = (items, [_layer_slots(grads[k], k) for k, _ in items])
            return None
        slots = [grads[k] if k in _FFN else _layer_slots(grads[k], k) for k, _ in items]
        self.items = items
        self.state, token = _scatter_begin(slots, [_route(k) for k, _ in items], [f"{k}{j}" for k, j in items],
                                           f"scatter{i}", (N_DEV, 3, 2 * D_FF // N_DEV))
        return token

    def hook(self, i, point, after):
        tokens = []
        if self.state is not None and point in (1, 2, 3):
            self.state, token = _scatter_next(self.state, after)
            if point == 3:
                self.sums.update(dict(zip(self.items, self.state)))
                self.state = None
            tokens.append(token)
        if self.ffn_state is not None and point in ("a", "b", 3):
            self.ffn_state, token = _scatter_next(self.ffn_state, after)
            if point == 3:
                self.sums.update(dict(zip(self.ffn_items, self.ffn_state)))
                self.ffn_state = None
            tokens.append(token)
        tokens = [t for t in tokens if t is not None]
        return functools.reduce(lambda a, b: a + b, tokens) if tokens else None


def _adamw(w, g, m, v, name, layer=None, prev=None, dep=None):
    shape = w.shape
    cols = shape[-1]
    view = shape if len(shape) == 3 else (1, math.prod(shape[:-1]), cols)
    layers, rows, _ = view
    tr = _pick(rows, (512, 352, 288, 256, 128, 64, 32, 16, 8))
    n_prev = 0 if prev is None else 3
    lead = ([] if prev is None else [p.reshape(view) for p in prev]) + ([] if dep is None else [dep])

    def body(*refs):
        w_ref, g_ref, m_ref, v_ref = refs[len(lead):len(lead) + 4]
        d_ref, nm_ref, nv_ref = refs[len(lead) + 4:]
        gv = g_ref[...]
        nm = ADAM_B1 * m_ref[...] + (1.0 - ADAM_B1) * gv
        nv = ADAM_B2 * v_ref[...] + (1.0 - ADAM_B2) * jnp.square(gv)
        m_hat = nm / (1.0 - ADAM_B1 ** ADAM_STEP)
        v_hat = nv / (1.0 - ADAM_B2 ** ADAM_STEP)
        d_ref[...] = -ADAM_LR * (m_hat / (jnp.sqrt(v_hat) + ADAM_EPS) + ADAM_WD * w_ref[...])
        nm_ref[...] = nm
        nv_ref[...] = nv

    if layer is None:
        grid = (layers, rows // tr)
        blk = gblk = pl.BlockSpec((None, tr, cols), lambda l, i: (l, i, 0))
        gview = view
    else:
        grid = (rows // tr,)
        blk = pl.BlockSpec((None, tr, cols), lambda i: (layer, i, 0))
        gblk = pl.BlockSpec((tr, cols), lambda i: (i, 0))
        gview = (rows, cols)
    shp = jax.ShapeDtypeStruct(view, F32)
    outs = pl.pallas_call(
        body, name=name, grid=grid, in_specs=[_ANY] * len(lead) + [blk, gblk, blk, blk], out_specs=[blk] * 3,
        out_shape=[shp] * 3, input_output_aliases={i: i for i in range(n_prev)},
        compiler_params=_params(("parallel",) * len(grid)),
    )(*lead, w.reshape(view), g.reshape(gview), m.reshape(view), v.reshape(view))
    return [o.reshape(shape) for o in outs]


def kernel(x, norm_g, attn_w_qkv, attn_w_o, conv_w_in, conv_w_dw, conv_w_out, pool_w_in, pool_w_grp, pool_scale, pool_w_out, ffn_w_up, ffn_w_dw, ffn_w_down, loss_target, m_norm_g, m_attn_w_qkv, m_attn_w_o, m_conv_w_in, m_conv_w_dw, m_conv_w_out, m_pool_w_in, m_pool_w_grp, m_pool_scale, m_pool_w_out, m_ffn_w_up, m_ffn_w_dw, m_ffn_w_down, v_norm_g, v_attn_w_qkv, v_attn_w_o, v_conv_w_in, v_conv_w_dw, v_conv_w_out, v_pool_w_in, v_pool_w_grp, v_pool_scale, v_pool_w_out, v_ffn_w_up, v_ffn_w_dw, v_ffn_w_down):
    shards = dict(zip(_NAMES, (norm_g, attn_w_qkv, attn_w_o, conv_w_in, conv_w_dw, conv_w_out, pool_w_in,
                               pool_w_grp, pool_scale, pool_w_out, ffn_w_up, ffn_w_dw, ffn_w_down)))
    moms = dict(zip(_NAMES, (m_norm_g, m_attn_w_qkv, m_attn_w_o, m_conv_w_in, m_conv_w_dw, m_conv_w_out,
                             m_pool_w_in, m_pool_w_grp, m_pool_scale, m_pool_w_out, m_ffn_w_up, m_ffn_w_dw,
                             m_ffn_w_down)))
    vels = dict(zip(_NAMES, (v_norm_g, v_attn_w_qkv, v_attn_w_o, v_conv_w_in, v_conv_w_dw, v_conv_w_out,
                             v_pool_w_in, v_pool_w_grp, v_pool_scale, v_pool_w_out, v_ffn_w_up, v_ffn_w_dw,
                             v_ffn_w_down)))
    weights = _LayerWeights(shards)
    sink = _GradSink()
    loss, grad_x, vec_grads = _local_step(x[0], loss_target[0], weights.vec, weights, sink)
    loss = lax.psum(loss[0, 0], _AXES)

    items, slots = sink.last
    vec_slots = {k: _slots_from_full(vec_grads[k], k) for k in _VECTORS if k != "ffn_w_dw"}
    vec_slots["ffn_w_dw"] = jnp.stack(vec_grads["ffn_w_dw"], axis=1)
    state, token = _scatter_begin(slots + [_pack_vectors(vec_slots, (N_DEV,)).astype(BF16)],
                                  [_route(k) for k, _ in items] + [ROUTE_B],
                                  [f"{k}{j}" for k, j in items] + ["vectors"], "scatter0")
    results = {}

    flipped = {k for k in _MATRICES if _WEIGHTS[k][0][0] > 1 and _shard_shape(k)[-1] % LANES}
    wmv = {k: [t.transpose(0, 2, 1) if k in flipped else t for t in (shards[k], moms[k], vels[k])] for k in _MATRICES}

    def step(matrices, dep=None):
        outs = []
        for k, j in matrices:
            g = sink.sums[(k, j)]
            w, m, v = wmv[k]
            if _WEIGHTS[k][0][0] == 1:
                results[k] = (g[None], _adamw(w, g[None], m, v, f"adamw_{k}", dep=dep))
            else:
                gs, prev = results.get(k, ({}, None))
                gs[j] = g
                results[k] = (gs, _adamw(w, g.T if k in flipped else g, m, v, f"adamw_{k}{j}", layer=j, prev=prev, dep=dep))
            outs.append(results[k][1][0])
        return outs

    state, token = _scatter_next(state, step(_layer_matrices(3), token))
    state, token = _scatter_next(state, step(_layer_matrices(2) + _layer_matrices(1), token))
    sums, _ = _scatter_next(state, step(_layer_matrices(0)[-2:], token))
    sink.sums.update(dict(zip(items, sums[:-1])))
    step(items)
    vec_sums = _unpack_vectors(sums[-1], ())
    for k in _VECTORS:
        results[k] = (vec_sums[k], _adamw(shards[k], vec_sums[k], moms[k], vels[k], f"adamw_{k}"))
    grads_out = {k: g if not isinstance(g, dict) else jnp.stack([g[j] for j in range(len(g))])
                 for k, (g, _) in results.items()}
    stepped = {k: [o.transpose(0, 2, 1) if k in flipped else o for o in outs] for k, (_, outs) in results.items()}
    return (loss, grad_x[None], *[grads_out[k] for k in _NAMES], *[stepped[k][0] for k in _NAMES],
            *[stepped[k][1] for k in _NAMES], *[stepped[k][2] for k in _NAMES])
```

```python
import functools
import math

import numpy as np
import jax
import jax.numpy as jnp
from jax import lax
from jax.experimental import pallas as pl
from jax.experimental.pallas import tpu as pltpu

F32, BF16 = jnp.float32, jnp.bfloat16

D_MODEL = 1024
SEQ = 2048
DEPTH = 4
DILATED_CFG = ((128, 1), (512, 4), (2048, 16))
N_GROUPS_A = 3
HEADS = 8
HEAD_DIM = 64
ATTN_WIDTH = HEADS * HEAD_DIM
N_HEADS_A = N_GROUPS_A * HEADS
BLOCK = 128
NEG_INF = -1e30
POOL_GROUP_DIM = 256
D_FF = 2816
RMS_EPS = 1e-6
ADAM_LR, ADAM_B1, ADAM_B2, ADAM_EPS, ADAM_WD, ADAM_STEP = 0.001, 0.9, 0.999, 1e-08, 0.01, 10

N_DEV = 8
LANES = 128
V7X_VMEM_BYTES = 64 * 2 ** 20
VMEM_LIMIT_BYTES = V7X_VMEM_BYTES - 8 * 2 ** 20
COL_TILE = 256
ROW_TILE = 256
MATMUL_TILES = (1024, 1408, 512, 256, 128)
TN_RESIDENT_K = 2048

NN = (((1,), (0,)), ((), ()))
NT = (((1,), (1,)), ((), ()))
TN = (((0,), (0,)), ((), ()))


def _dot(a, b, dims=NN):
    return lax.dot_general(a, b, dims, preferred_element_type=F32)


def _params(sem=None):
    return pltpu.CompilerParams(dimension_semantics=sem, vmem_limit_bytes=VMEM_LIMIT_BYTES)


def _pick(n, prefs):
    for p in prefs:
        if n % p == 0:
            return p
    return n


def _matmul(a, b, mode, out_dtype, name, a_parts=1, b_parts=1):
    if mode == "nn":
        m, k = a.shape[-2], a.shape[-1] * a_parts
        n = b.shape[-1] * b_parts
    elif mode == "nt":
        m, k = a.shape[-2], a.shape[-1] * a_parts
        n = b.shape[-2]
    else:
        k, m = a.shape[-2], a.shape[-1] * a_parts
        n = b.shape[-1] * b_parts
    tm = _pick(m, MATMUL_TILES)
    tn = _pick(n // b_parts if mode != "nt" else n, MATMUL_TILES)
    kk = k // a_parts if mode != "tn" else k
    tk = _pick(kk, MATMUL_TILES)
    if mode == "tn":
        tm = _pick(m // a_parts, MATMUL_TILES)
        if k <= TN_RESIDENT_K:
            tk = k
    gm, gn, gk = m // tm, n // tn, k // tk

    def a_idx(i, j, kq):
        if mode == "tn":
            r, c, per = kq, i, (m // a_parts) // tm
        else:
            r, c, per = i, kq, (k // a_parts) // tk
        return (r, c) if a_parts == 1 else (c // per, r, c % per)

    def b_idx(i, j, kq):
        if mode == "nt":
            return (j, kq)
        per = (n // b_parts) // tn
        return (kq, j) if b_parts == 1 else (j // per, kq, j % per)

    a_blk = (tk, tm) if mode == "tn" else (tm, tk)
    b_blk = (tn, tk) if mode == "nt" else (tk, tn)
    if a_parts > 1:
        a_blk = (None,) + a_blk
    if b_parts > 1:
        b_blk = (None,) + b_blk
    dims = {"nn": NN, "nt": NT, "tn": TN}[mode]

    def body_single(a_ref, b_ref, o_ref):
        o_ref[...] = _dot(a_ref[...], b_ref[...], dims).astype(o_ref.dtype)

    def body(a_ref, b_ref, o_ref, acc_ref):
        kq = pl.program_id(2)

        @pl.when(kq == 0)
        def _():
            acc_ref[...] = jnp.zeros_like(acc_ref)

        acc_ref[...] += _dot(a_ref[...], b_ref[...], dims)

        @pl.when(kq == gk - 1)
        def _():
            o_ref[...] = acc_ref[...].astype(o_ref.dtype)

    return pl.pallas_call(
        body_single if gk == 1 else body, name=name, grid=(gm, gn, gk),
        in_specs=[pl.BlockSpec(a_blk, a_idx), pl.BlockSpec(b_blk, b_idx)],
        out_specs=pl.BlockSpec((tm, tn), lambda i, j, kq: (i, j)),
        out_shape=jax.ShapeDtypeStruct((m, n), out_dtype),
        scratch_shapes=[] if gk == 1 else [pltpu.VMEM((tm, tn), F32)],
        compiler_params=_params(("parallel", "parallel", "arbitrary")),
    )(a, b)


def _rms_fwd(xin, g, res, out_dtype, name):
    s, d = xin.shape
    has_res = res is not None

    def body(*refs):
        x_ref, g_ref = refs[0], refs[1]
        o_ref = refs[-1]
        x = x_ref[...]
        r = lax.rsqrt(jnp.mean(x * x, axis=-1, keepdims=True) + RMS_EPS)
        y = x * r * g_ref[...]
        if has_res:
            y = refs[2][...] + y
        o_ref[...] = y.astype(o_ref.dtype)

    row = pl.BlockSpec((ROW_TILE, d), lambda i: (i, 0))
    vec = pl.BlockSpec((1, d), lambda i: (0, 0))
    ins = [xin, g] + ([res] if has_res else [])
    return pl.pallas_call(
        body, name=name, grid=(s // ROW_TILE,),
        in_specs=[row, vec] + ([row] if has_res else []),
        out_specs=row, out_shape=jax.ShapeDtypeStruct((s, d), out_dtype),
        compiler_params=_params(("parallel",)),
    )(*ins)


def _rms_bwd(xin, g, dy, dres, out_dtype, name):
    s, d = xin.shape
    has_res = dres is not None

    def body(*refs):
        x_ref, g_ref, dy_ref = refs[0], refs[1], refs[2]
        dx_ref, dg_ref = refs[-2], refs[-1]

        @pl.when(pl.program_id(0) == 0)
        def _():
            dg_ref[...] = jnp.zeros_like(dg_ref)

        x = x_ref[...]
        dyv = dy_ref[...].astype(F32)
        r = lax.rsqrt(jnp.mean(x * x, axis=-1, keepdims=True) + RMS_EPS)
        xhat = x * r
        u = dyv * g_ref[...]
        dx = r * (u - xhat * jnp.mean(u * xhat, axis=-1, keepdims=True))
        if has_res:
            dx = refs[3][...] + dx
        dx_ref[...] = dx.astype(dx_ref.dtype)
        dg_ref[...] += jnp.sum(dyv * xhat, axis=0, keepdims=True)

    row = pl.BlockSpec((ROW_TILE, d), lambda i: (i, 0))
    vec = pl.BlockSpec((1, d), lambda i: (0, 0))
    ins = [xin, g, dy] + ([dres] if has_res else [])
    return pl.pallas_call(
        body, name=name, grid=(s // ROW_TILE,),
        in_specs=[row, vec, row] + ([row] if has_res else []),
        out_specs=[row, vec],
        out_shape=[jax.ShapeDtypeStruct((s, d), out_dtype), jax.ShapeDtypeStruct((1, d), F32)],
        compiler_params=_params(("arbitrary",)),
    )(*ins)


def _rms(x):
    r = lax.rsqrt(jnp.mean(x * x, axis=-1, keepdims=True) + RMS_EPS)
    return r, x * r


def _rms_grad(r, xhat, dy, g):
    u = dy * g
    return r * (u - xhat * jnp.mean(u * xhat, axis=-1, keepdims=True))


def _rms_res_pre(sub, g_post, res, g_pre, name):
    s, d = sub.shape

    def body(sub_ref, gp_ref, res_ref, gn_ref, x_ref, n_ref):
        xnew = res_ref[...] + _rms(sub_ref[...])[1] * gp_ref[...]
        x_ref[...] = xnew
        n_ref[...] = (_rms(xnew)[1] * gn_ref[...]).astype(BF16)

    row = pl.BlockSpec((ROW_TILE, d), lambda i: (i, 0))
    vec = pl.BlockSpec((1, d), lambda i: (0, 0))
    return pl.pallas_call(
        body, name=name, grid=(s // ROW_TILE,),
        in_specs=[row, vec, row, vec], out_specs=[row, row],
        out_shape=[jax.ShapeDtypeStruct((s, d), F32), jax.ShapeDtypeStruct((s, d), BF16)],
        compiler_params=_params(("parallel",)),
    )(sub, g_post, res, g_pre)


def _rms_bwd_pair(xmid, g_pre, dn, dres, sub, g_post, name):
    s, d = xmid.shape

    def body(x_ref, gn_ref, dn_ref, dres_ref, sub_ref, gp_ref, dx_ref, dsub_ref, dgn_ref, dgp_ref):
        @pl.when(pl.program_id(0) == 0)
        def _():
            dgn_ref[...] = jnp.zeros_like(dgn_ref)
            dgp_ref[...] = jnp.zeros_like(dgp_ref)

        dnv = dn_ref[...].astype(F32)
        r, xhat = _rms(x_ref[...])
        dx = dres_ref[...] + _rms_grad(r, xhat, dnv, gn_ref[...])
        dx_ref[...] = dx
        dgn_ref[...] += jnp.sum(dnv * xhat, axis=0, keepdims=True)
        rs, shat = _rms(sub_ref[...])
        dsub_ref[...] = _rms_grad(rs, shat, dx, gp_ref[...]).astype(BF16)
        dgp_ref[...] += jnp.sum(dx * shat, axis=0, keepdims=True)

    row = pl.BlockSpec((ROW_TILE, d), lambda i: (i, 0))
    vec = pl.BlockSpec((1, d), lambda i: (0, 0))
    return pl.pallas_call(
        body, name=name, grid=(s // ROW_TILE,),
        in_specs=[row, vec, row, row, row, vec], out_specs=[row, row, vec, vec],
        out_shape=[jax.ShapeDtypeStruct((s, d), F32), jax.ShapeDtypeStruct((s, d), BF16),
                   jax.ShapeDtypeStruct((1, d), F32), jax.ShapeDtypeStruct((1, d), F32)],
        compiler_params=_params(("arbitrary",)),
    )(xmid, g_pre, dn, dres, sub, g_post)


def _loss_head(y, tgt, name):
    s, d = y.shape

    def body(y_ref, t_ref, l_ref, dy_ref):
        @pl.when(pl.program_id(0) == 0)
        def _():
            l_ref[...] = jnp.zeros_like(l_ref)

        e = y_ref[...] - t_ref[...]
        dy_ref[...] = e / d
        per_tok = jnp.mean(e * e, axis=-1, keepdims=True)
        l_ref[...] += 0.5 * jnp.sum(per_tok, axis=0, keepdims=True)

    row = pl.BlockSpec((ROW_TILE, d), lambda i: (i, 0))
    return pl.pallas_call(
        body, name=name, grid=(s // ROW_TILE,),
        in_specs=[row, row],
        out_specs=[pl.BlockSpec((1, 1), lambda i: (0, 0)), row],
        out_shape=[jax.ShapeDtypeStruct((1, 1), F32), jax.ShapeDtypeStruct((s, d), F32)],
        compiler_params=_params(("arbitrary",)),
    )(y, tgt)


SUBLANES = 8


def _shift_down(x, k):
    t, c = x.shape
    r = pltpu.roll(x.reshape(t // SUBLANES, SUBLANES, c), k, axis=1)
    above = jnp.concatenate([jnp.zeros((1, SUBLANES, c), x.dtype), r[:-1]], axis=0)
    rows = lax.broadcasted_iota(jnp.int32, (1, SUBLANES, c), 1)
    return jnp.where(rows >= k, r, above).reshape(t, c)


def _shift_up(x, k):
    t, c = x.shape
    r = pltpu.roll(x.reshape(t // SUBLANES, SUBLANES, c), SUBLANES - k, axis=1)
    below = jnp.concatenate([r[1:], jnp.zeros((1, SUBLANES, c), x.dtype)], axis=0)
    rows = lax.broadcasted_iota(jnp.int32, (1, SUBLANES, c), 1)
    return jnp.where(rows < SUBLANES - k, r, below).reshape(t, c)


def _conv3(h, w):
    return w[2:3] * h + w[1:2] * _shift_down(h, 1) + w[0:1] * _shift_down(h, 2)


def _conv3_bwd(dc, h, w, dw_ref, cols=slice(None)):
    u1, u2 = _shift_up(dc, 1), _shift_up(dc, 2)
    dw_ref[0:1, cols] = jnp.sum(u2 * h, axis=0, keepdims=True)
    dw_ref[1:2, cols] = jnp.sum(u1 * h, axis=0, keepdims=True)
    dw_ref[2:3, cols] = jnp.sum(dc * h, axis=0, keepdims=True)
    return w[2:3] * dc + w[1:2] * u1 + w[0:1] * u2


FFN_PAIRS = N_DEV // 2


def _lane_chunks(width):
    return [(c0, min(COL_TILE, width - c0)) for c0 in range(0, width, COL_TILE)]


def _ffn_up(n, wup, wdw, name):
    s, d = n.shape
    cw = wup.shape[-1]

    def body(n_ref, wg_ref, wu_ref, dg_ref, du_ref, h_ref, c_ref, a_ref):
        x = n_ref[...]
        for c0, size in _lane_chunks(cw):
            cols = slice(c0, c0 + size)
            hg = _dot(x, wg_ref[:, cols])
            hu = _dot(x, wu_ref[:, cols])
            h_ref[0, :, cols] = hg.astype(BF16)
            h_ref[1, :, cols] = hu.astype(BF16)
            cg = _conv3(hg, dg_ref[:, cols])
            cu = _conv3(hu, du_ref[:, cols])
            c_ref[0, :, cols] = cg.astype(BF16)
            c_ref[1, :, cols] = cu.astype(BF16)
            a_ref[:, cols] = (cg * jax.nn.sigmoid(cg) * cu).astype(BF16)

    return pl.pallas_call(
        body, name=name, grid=(FFN_PAIRS,),
        in_specs=[pl.BlockSpec((s, d), lambda j: (0, 0)),
                  pl.BlockSpec((None, d, cw), lambda j: (j, 0, 0)),
                  pl.BlockSpec((None, d, cw), lambda j: (j + FFN_PAIRS, 0, 0)),
                  pl.BlockSpec((None, 3, cw), lambda j: (j, 0, 0)),
                  pl.BlockSpec((None, 3, cw), lambda j: (j + FFN_PAIRS, 0, 0))],
        out_specs=[pl.BlockSpec((None, 2, s, cw), lambda j: (j, 0, 0, 0)),
                   pl.BlockSpec((None, 2, s, cw), lambda j: (j, 0, 0, 0)),
                   pl.BlockSpec((None, s, cw), lambda j: (j, 0, 0))],
        out_shape=[jax.ShapeDtypeStruct((FFN_PAIRS, 2, s, cw), BF16), jax.ShapeDtypeStruct((FFN_PAIRS, 2, s, cw), BF16),
                   jax.ShapeDtypeStruct((FFN_PAIRS, s, cw), BF16)],
        compiler_params=_params(("parallel",)),
    )(n, wup, wup, wdw, wdw)


def _ffn_mid_bwd(do, wdown, h, c, wdw, name):
    s, d = do.shape
    cw = wdown.shape[1]

    def body(do_ref, wd_ref, h_ref, c_ref, wg_ref, wu_ref, dh_ref, dwg_ref, dwu_ref):
        dov = do_ref[...]
        for c0, size in _lane_chunks(cw):
            cols = slice(c0, c0 + size)
            da = _dot(dov, wd_ref[cols, :], NT)
            hg = h_ref[0, :, cols].astype(F32)
            hu = h_ref[1, :, cols].astype(F32)
            wg, wu = wg_ref[:, cols], wu_ref[:, cols]
            cg = c_ref[0, :, cols].astype(F32)
            cu = c_ref[1, :, cols].astype(F32)
            sg = jax.nn.sigmoid(cg)
            dcu = da * (cg * sg)
            dcg = da * cu * (sg * (1.0 + cg * (1.0 - sg)))
            dh_ref[0, :, cols] = _conv3_bwd(dcg, hg, wg, dwg_ref, cols).astype(BF16)
            dh_ref[1, :, cols] = _conv3_bwd(dcu, hu, wu, dwu_ref, cols).astype(BF16)

    vec = jax.ShapeDtypeStruct((FFN_PAIRS, 3, cw), F32)
    return pl.pallas_call(
        body, name=name, grid=(FFN_PAIRS,),
        in_specs=[pl.BlockSpec((s, d), lambda j: (0, 0)), pl.BlockSpec((None, cw, d), lambda j: (j, 0, 0)),
                  pl.BlockSpec((None, 2, s, cw), lambda j: (j, 0, 0, 0)),
                  pl.BlockSpec((None, 2, s, cw), lambda j: (j, 0, 0, 0)),
                  pl.BlockSpec((None, 3, cw), lambda j: (j, 0, 0)),
                  pl.BlockSpec((None, 3, cw), lambda j: (j + FFN_PAIRS, 0, 0))],
        out_specs=[pl.BlockSpec((None, 2, s, cw), lambda j: (j, 0, 0, 0)),
                   pl.BlockSpec((None, 3, cw), lambda j: (j, 0, 0)), pl.BlockSpec((None, 3, cw), lambda j: (j, 0, 0))],
        out_shape=[jax.ShapeDtypeStruct((FFN_PAIRS, 2, s, cw), BF16), vec, vec],
        compiler_params=_params(("parallel",)),
    )(do, wdown, h, c, wdw, wdw)


def _ffn_dwup(n, dh, name):
    s, d = n.shape
    cw = dh.shape[-1]

    def body(n_ref, dh_ref, o_ref):
        o_ref[...] = _dot(n_ref[...], dh_ref[...], TN).astype(BF16)

    return pl.pallas_call(
        body, name=name, grid=(N_DEV,),
        in_specs=[pl.BlockSpec((s, d), lambda k: (0, 0)),
                  pl.BlockSpec((None, None, s, cw), lambda k: (k % FFN_PAIRS, k // FFN_PAIRS, 0, 0))],
        out_specs=pl.BlockSpec((None, d, cw), lambda k: (k, 0, 0)),
        out_shape=jax.ShapeDtypeStruct((N_DEV, d, cw), BF16),
        compiler_params=_params(("parallel",)),
    )(n, dh)


def _ffn_dn(dh, wup, name, dep=None):
    s, cw = dh.shape[-2:]
    d = wup.shape[1]
    tm = _pick(s, MATMUL_TILES)
    deps = [] if dep is None else [dep]

    def body(dh_ref, w_ref, *rest):
        o_ref, acc_ref = rest[-2:]
        k = pl.program_id(1)

        @pl.when(k == 0)
        def _():
            acc_ref[...] = jnp.zeros_like(acc_ref)

        acc_ref[...] += _dot(dh_ref[...], w_ref[...], NT)

        @pl.when(k == N_DEV - 1)
        def _():
            o_ref[...] = acc_ref[...]

    return pl.pallas_call(
        body, name=name, grid=(s // tm, N_DEV),
        in_specs=[pl.BlockSpec((None, None, tm, cw), lambda i, k: (k % FFN_PAIRS, k // FFN_PAIRS, i, 0)),
                  pl.BlockSpec((None, d, cw), lambda i, k: (k, 0, 0))] + [_ANY] * len(deps),
        out_specs=pl.BlockSpec((tm, d), lambda i, k: (i, 0)),
        out_shape=jax.ShapeDtypeStruct((s, d), F32),
        scratch_shapes=[pltpu.VMEM((tm, d), F32)],
        compiler_params=_params(("parallel", "arbitrary")),
    )(dh, wup, *deps)


def _sconv_fwd(n, win, wdw, name):
    s, d = n.shape
    tn = COL_TILE
    nj = d // tn

    def body(n_ref, wb_ref, wc_ref, wh_ref, dw_ref, z_ref, y_ref):
        x = n_ref[...]
        zb = _dot(x, wb_ref[...])
        zc = _dot(x, wc_ref[...])
        zh = _dot(x, wh_ref[...])
        z_ref[0] = zb.astype(BF16)
        z_ref[1] = zc.astype(BF16)
        z_ref[2] = zh.astype(BF16)
        y_ref[...] = (zb * _conv3(zc * zh, dw_ref[...])).astype(BF16)

    return pl.pallas_call(
        body, name=name, grid=(nj,),
        in_specs=[pl.BlockSpec((s, d), lambda j: (0, 0)),
                  pl.BlockSpec((d, tn), lambda j: (0, j)), pl.BlockSpec((d, tn), lambda j: (0, j + nj)),
                  pl.BlockSpec((d, tn), lambda j: (0, j + 2 * nj)), pl.BlockSpec((3, tn), lambda j: (0, j))],
        out_specs=[pl.BlockSpec((3, s, tn), lambda j: (0, 0, j)), pl.BlockSpec((s, tn), lambda j: (0, j))],
        out_shape=[jax.ShapeDtypeStruct((3, s, d), BF16), jax.ShapeDtypeStruct((s, d), BF16)],
        compiler_params=_params(("parallel",)),
    )(n, win, win, win, wdw)


def _sconv_mid_bwd(dm, wout, z, wdw, name):
    s, d = dm.shape
    tn = COL_TILE
    nj = d // tn

    def body(dm_ref, wo_ref, z_ref, w_ref, dz_ref, dw_ref):
        dy = _dot(dm_ref[...], wo_ref[...], NT)
        zb = z_ref[0].astype(F32)
        zc = z_ref[1].astype(F32)
        zh = z_ref[2].astype(F32)
        w = w_ref[...]
        p = zc * zh
        cp = _conv3(p, w)
        dz_ref[0] = (dy * cp).astype(BF16)
        dcp = dy * zb
        dp = _conv3_bwd(dcp, p, w, dw_ref)
        dz_ref[1] = (dp * zh).astype(BF16)
        dz_ref[2] = (dp * zc).astype(BF16)

    return pl.pallas_call(
        body, name=name, grid=(nj,),
        in_specs=[pl.BlockSpec((s, d), lambda j: (0, 0)), pl.BlockSpec((tn, d), lambda j: (j, 0)),
                  pl.BlockSpec((3, s, tn), lambda j: (0, 0, j)), pl.BlockSpec((3, tn), lambda j: (0, j))],
        out_specs=[pl.BlockSpec((3, s, tn), lambda j: (0, 0, j)), pl.BlockSpec((3, tn), lambda j: (0, j))],
        out_shape=[jax.ShapeDtypeStruct((3, s, d), BF16), jax.ShapeDtypeStruct((3, d), F32)],
        compiler_params=_params(("parallel",)),
    )(dm, wout, z, wdw)


def _pool_select(g, c2, c4, c8, c16):
    return jnp.where(g == 0, c2, jnp.where(g == 1, c4, jnp.where(g == 2, c8, c16)))


def _pool_inv_count(g, shape):
    pos = lax.broadcasted_iota(jnp.int32, shape, 0).astype(F32) + 1.0
    win = (2 << g).astype(F32)
    return jnp.minimum(pos, win)


def _pool_fwd(n, win, wgrp, scale, name):
    s, d = n.shape
    tn = POOL_GROUP_DIM

    def body(n_ref, wi_ref, wg_ref, sc_ref, p_ref, y_ref):
        g = pl.program_id(0)
        u = _dot(n_ref[...], wi_ref[...])
        s2 = u + _shift_down(u, 1)
        s4 = s2 + _shift_down(s2, 2)
        s8 = s4 + _shift_down(s4, 4)
        s16 = s8 + _shift_down(s8, 8)
        tot = _pool_select(g, s2, s4, s8, s16)
        p = (tot / _pool_inv_count(g, u.shape) - u).astype(BF16)
        p_ref[...] = p
        y_ref[...] = (_dot(p, wg_ref[...]) * sc_ref[...]).astype(BF16)

    return pl.pallas_call(
        body, name=name, grid=(d // tn,),
        in_specs=[pl.BlockSpec((s, d), lambda g: (0, 0)), pl.BlockSpec((d, tn), lambda g: (0, g)),
                  pl.BlockSpec((None, tn, tn), lambda g: (g, 0, 0)), pl.BlockSpec((1, tn), lambda g: (0, g))],
        out_specs=[pl.BlockSpec((s, tn), lambda g: (0, g)), pl.BlockSpec((s, tn), lambda g: (0, g))],
        out_shape=[jax.ShapeDtypeStruct((s, d), BF16), jax.ShapeDtypeStruct((s, d), BF16)],
        compiler_params=_params(("parallel",)),
    )(n, win, wgrp, scale)


def _pool_mid_bwd(dm, wout, p, wgrp, scale, name):
    s, d = dm.shape
    tn = POOL_GROUP_DIM

    def body(dm_ref, wo_ref, p_ref, wg_ref, sc_ref, du_ref, dwg_ref, dsc_ref):
        g = pl.program_id(0)
        dy = _dot(dm_ref[...], wo_ref[...], NT)
        pv = p_ref[...]
        wg = wg_ref[...]
        ypre = _dot(pv, wg)
        dsc_ref[...] = jnp.sum(dy * ypre, axis=0, keepdims=True)
        dypre = (dy * sc_ref[...]).astype(BF16)
        dwg_ref[...] = _dot(pv, dypre, TN)
        dp = _dot(dypre, wg, NT)
        e = dp / _pool_inv_count(g, dp.shape)
        f2 = e + _shift_up(e, 1)
        f4 = f2 + _shift_up(f2, 2)
        f8 = f4 + _shift_up(f4, 4)
        f16 = f8 + _shift_up(f8, 8)
        du_ref[...] = (_pool_select(g, f2, f4, f8, f16) - dp).astype(BF16)

    return pl.pallas_call(
        body, name=name, grid=(d // tn,),
        in_specs=[pl.BlockSpec((s, d), lambda g: (0, 0)), pl.BlockSpec((tn, d), lambda g: (g, 0)),
                  pl.BlockSpec((s, tn), lambda g: (0, g)), pl.BlockSpec((None, tn, tn), lambda g: (g, 0, 0)),
                  pl.BlockSpec((1, tn), lambda g: (0, g))],
        out_specs=[pl.BlockSpec((s, tn), lambda g: (0, g)), pl.BlockSpec((None, tn, tn), lambda g: (g, 0, 0)),
                   pl.BlockSpec((1, tn), lambda g: (0, g))],
        out_shape=[jax.ShapeDtypeStruct((s, d), BF16), jax.ShapeDtypeStruct((4, tn, tn), F32),
                   jax.ShapeDtypeStruct((1, d), F32)],
        compiler_params=_params(("parallel",)),
    )(dm, wout, p, wgrp, scale)


PANEL = LANES
ATTN_EXT = ATTN_WIDTH + PANEL
DVEC_LANE = HEADS


def _alibi_slopes(g, dil):
    all_slopes = 2.0 ** (-8.0 * np.arange(1, N_HEADS_A + 1) / N_HEADS_A)
    return [float(np.float32(sl) * np.float32(dil)) for sl in all_slopes[g * HEADS:(g + 1) * HEADS]]


def _residue_order(a, dil, name):
    s, w = a.shape
    per = ROW_TILE // dil
    panels = w // PANEL

    def body(a_ref, o_ref, *tiles):
        for c in range(panels):
            cols = slice(c * PANEL, (c + 1) * PANEL)
            tiles[c][...] = a_ref[:, cols].astype(F32)
            for r in range(dil):
                o_ref[r, :, cols] = tiles[c][pl.ds(r, per, stride=dil), :].astype(o_ref.dtype)

    out = pl.pallas_call(
        body, name=name, grid=(s // ROW_TILE,),
        in_specs=[pl.BlockSpec((ROW_TILE, w), lambda i: (i, 0))],
        out_specs=pl.BlockSpec((dil, per, w), lambda i: (0, i, 0)),
        out_shape=jax.ShapeDtypeStruct((dil, s // dil, w), a.dtype),
        scratch_shapes=[pltpu.VMEM((ROW_TILE, PANEL), F32)] * panels,
        compiler_params=_params(("parallel",)),
    )(a)
    return out.reshape(s, w)


def _token_order(a, dil, acc, name):
    s, w = a.shape
    per = ROW_TILE // dil
    panels = w // PANEL
    has_acc = acc is not None

    def body(*refs):
        a_ref = refs[0]
        o_ref = refs[2] if has_acc else refs[1]
        tiles = refs[3:] if has_acc else refs[2:]
        for c in range(panels):
            cols = slice(c * PANEL, (c + 1) * PANEL)
            for r in range(dil):
                tiles[c][pl.ds(r, per, stride=dil), :] = a_ref[r, :, cols]
            v = tiles[c][...]
            if has_acc:
                v = v + refs[1][:, cols]
            o_ref[:, cols] = v

    row = pl.BlockSpec((ROW_TILE, w), lambda i: (i, 0))
    return pl.pallas_call(
        body, name=name, grid=(s // ROW_TILE,),
        in_specs=[pl.BlockSpec((dil, per, w), lambda i: (0, i, 0))] + ([row] if has_acc else []),
        out_specs=row, out_shape=jax.ShapeDtypeStruct((s, w), F32),
        scratch_shapes=[pltpu.VMEM((ROW_TILE, PANEL), F32)] * panels,
        compiler_params=_params(("parallel",)),
    )(*([a.reshape(dil, s // dil, w)] + ([acc] if has_acc else [])))


def _qkv_proj(n, wqkv, g, name):
    s, d = n.shape
    tm = _pick(s, MATMUL_TILES)

    def body(a_ref, b_ref, o_ref):
        o_ref[...] = _dot(a_ref[...], b_ref[...]).astype(BF16)

    return pl.pallas_call(
        body, name=name, grid=(s // tm, 3),
        in_specs=[pl.BlockSpec((tm, d), lambda i, t: (i, 0)),
                  pl.BlockSpec((d, ATTN_WIDTH), lambda i, t: (0, 3 * g + t))],
        out_specs=pl.BlockSpec((None, tm, ATTN_WIDTH), lambda i, t: (t, i, 0)),
        out_shape=jax.ShapeDtypeStruct((3, s, ATTN_WIDTH), BF16),
        compiler_params=_params(("parallel", "parallel")),
    )(n, wqkv)


def _attn_window(n, ln):
    if ln == BLOCK:
        return 0, BLOCK
    return pl.multiple_of(jnp.maximum(n - 1, 0) * BLOCK, BLOCK), 2 * BLOCK


def _attn_mask(n, k0, kw):
    qpos = n * BLOCK + lax.broadcasted_iota(jnp.int32, (BLOCK, kw), 0)
    kpos = k0 + lax.broadcasted_iota(jnp.int32, (BLOCK, kw), 1)
    dist = qpos - kpos
    return dist.astype(F32), (dist >= 0) & (dist <= BLOCK)


def _attn_scores(q, keys, slope, dist, valid):
    s = _dot(q, keys, NT) * (HEAD_DIM ** -0.5) - slope * dist
    return jnp.where(valid, s, NEG_INF)


ATTN_STEP_BLOCKS = 1
ATTN_BWD_STEP_BLOCKS = 4


def _attn_block(gb, ln):
    nb = ln // BLOCK
    n, base = (0, gb * ln) if nb == 1 else (gb % nb, (gb // nb) * ln)
    k0, kw = _attn_window(n, ln)
    cur = pl.ds(pl.multiple_of(gb * BLOCK, BLOCK), BLOCK)
    win = pl.ds(pl.multiple_of(base + k0, BLOCK), kw)
    return cur, win, n, k0, kw


def _attn_fwd(qkv, g, name):
    _, s, w = qkv.shape
    dil = DILATED_CFG[g][1]
    ln = s // dil
    slopes = _alibi_slopes(g, dil)
    rows = ATTN_STEP_BLOCKS * BLOCK

    def body(qkv_ref, o_ref):
        o_ref[:, w:] = jnp.zeros((rows, PANEL), F32)
        for b in range(ATTN_STEP_BLOCKS):
            cur, win, n, k0, kw = _attn_block(pl.program_id(0) * ATTN_STEP_BLOCKS + b, ln)
            dist, valid = _attn_mask(n, k0, kw)
            out = slice(b * BLOCK, (b + 1) * BLOCK)
            for h in range(HEADS):
                cols = slice(h * HEAD_DIM, (h + 1) * HEAD_DIM)
                sc = _attn_scores(qkv_ref[0, cur, cols], qkv_ref[1, win, cols], slopes[h], dist, valid)
                m = jnp.max(sc, axis=-1, keepdims=True)
                p = jnp.exp(sc - m)
                den = jnp.sum(p, axis=-1, keepdims=True)
                o_ref[out, cols] = _dot(p.astype(BF16), qkv_ref[2, win, cols]) / den
                o_ref[out, w + h:w + h + 1] = m + jnp.log(den)

    return pl.pallas_call(
        body, name=name, grid=(s // rows,),
        in_specs=[pl.BlockSpec((3, s, w), lambda i: (0, 0, 0))],
        out_specs=pl.BlockSpec((rows, ATTN_EXT), lambda i: (i, 0)),
        out_shape=jax.ShapeDtypeStruct((s, ATTN_EXT), F32),
        compiler_params=_params(("parallel",)),
    )(qkv)


def _attn_bwd(qkv, dext, g, name, dep=None):
    _, s, w = qkv.shape
    dil = DILATED_CFG[g][1]
    ln = s // dil
    slopes = _alibi_slopes(g, dil)
    scale = HEAD_DIM ** -0.5
    rows = ATTN_BWD_STEP_BLOCKS * BLOCK
    steps = s // rows
    deps = [] if dep is None else [dep]

    def body(qkv_ref, de_ref, *rest):
        d_ref, dk_ref, dv_ref = rest[-3:]

        @pl.when(pl.program_id(0) == 0)
        def _():
            dk_ref[...] = jnp.zeros_like(dk_ref)
            dv_ref[...] = jnp.zeros_like(dv_ref)

        for b in range(ATTN_BWD_STEP_BLOCKS):
            cur, win, n, k0, kw = _attn_block(pl.program_id(0) * ATTN_BWD_STEP_BLOCKS + b, ln)
            dist, valid = _attn_mask(n, k0, kw)
            blk = slice(b * BLOCK, (b + 1) * BLOCK)
            for h in range(HEADS):
                cols = slice(h * HEAD_DIM, (h + 1) * HEAD_DIM)
                q, keys = qkv_ref[0, cur, cols], qkv_ref[1, win, cols]
                dob = de_ref[blk, cols].astype(BF16)
                p = jnp.exp(_attn_scores(q, keys, slopes[h], dist, valid) - de_ref[blk, w + h:w + h + 1])
                dd = de_ref[blk, w + DVEC_LANE + h:w + DVEC_LANE + h + 1]
                ds = (p * (_dot(dob, qkv_ref[2, win, cols], NT) - dd)).astype(BF16)
                d_ref[0, cur, cols] = (scale * _dot(ds, keys)).astype(BF16)
                dv_ref[win, cols] += _dot(p.astype(BF16), dob, TN)
                dk_ref[win, cols] += scale * _dot(ds, q, TN)

        @pl.when(pl.program_id(0) == steps - 1)
        def _():
            d_ref[1] = dk_ref[...].astype(BF16)
            d_ref[2] = dv_ref[...].astype(BF16)

    whole = pl.BlockSpec((3, s, w), lambda i: (0, 0, 0))
    return pl.pallas_call(
        body, name=name, grid=(steps,),
        in_specs=[whole, pl.BlockSpec((rows, ATTN_EXT), lambda i: (i, 0))] + [_ANY] * len(deps),
        out_specs=whole, out_shape=jax.ShapeDtypeStruct((3, s, w), BF16),
        scratch_shapes=[pltpu.VMEM((s, w), F32), pltpu.VMEM((s, w), F32)],
        compiler_params=_params(("arbitrary",)),
    )(qkv, dext, *deps)


def _attn_merge(e0, e1, e2, name):
    s = e0.shape[0]
    w = ATTN_WIDTH

    def body(e0_ref, e1_ref, e2_ref, m_ref, mb_ref, lse_ref):
        refs = (e0_ref, e1_ref, e2_ref)
        l = [r[:, w:w + HEADS] for r in refs]
        mx = jnp.maximum(jnp.maximum(l[0], l[1]), l[2])
        e = [jnp.exp(v - mx) for v in l]
        z = e[0] + e[1] + e[2]
        lse_ref[...] = mx + jnp.log(z)
        wts = [v / z for v in e]
        for h in range(HEADS):
            cols = slice(h * HEAD_DIM, (h + 1) * HEAD_DIM)
            acc = wts[0][:, h:h + 1] * refs[0][:, cols]
            for g in range(1, N_GROUPS_A):
                acc = acc + wts[g][:, h:h + 1] * refs[g][:, cols]
            m_ref[:, cols] = acc
            mb_ref[:, cols] = acc.astype(BF16)

    ext = pl.BlockSpec((ROW_TILE, ATTN_EXT), lambda i: (i, 0))
    row = pl.BlockSpec((ROW_TILE, w), lambda i: (i, 0))
    return pl.pallas_call(
        body, name=name, grid=(s // ROW_TILE,),
        in_specs=[ext, ext, ext],
        out_specs=[row, row, pl.BlockSpec((ROW_TILE, HEADS), lambda i: (i, 0))],
        out_shape=[jax.ShapeDtypeStruct((s, w), F32), jax.ShapeDtypeStruct((s, w), BF16),
                   jax.ShapeDtypeStruct((s, HEADS), F32)],
        compiler_params=_params(("parallel",)),
    )(e0, e1, e2)


def _attn_dvec(dmerged, merged, lse_all, name, dep=None):
    s, w = merged.shape
    deps = [] if dep is None else [dep]

    def body(dm_ref, m_ref, lse_ref, *rest):
        de_ref = rest[-1]
        dmv = dm_ref[...]
        de_ref[:, :w] = dmv
        de_ref[:, w:] = jnp.zeros((ROW_TILE, PANEL), F32)
        de_ref[:, w:w + HEADS] = lse_ref[...]
        prod = dmv * m_ref[...]
        for h in range(HEADS):
            lane = w + DVEC_LANE + h
            de_ref[:, lane:lane + 1] = jnp.sum(prod[:, h * HEAD_DIM:(h + 1) * HEAD_DIM], axis=-1, keepdims=True)

    row = pl.BlockSpec((ROW_TILE, w), lambda i: (i, 0))
    return pl.pallas_call(
        body, name=name, grid=(s // ROW_TILE,),
        in_specs=[row, row, pl.BlockSpec((ROW_TILE, HEADS), lambda i: (i, 0))] + [_ANY] * len(deps),
        out_specs=pl.BlockSpec((ROW_TILE, ATTN_EXT), lambda i: (i, 0)),
        out_shape=jax.ShapeDtypeStruct((s, ATTN_EXT), F32),
        compiler_params=_params(("parallel",)),
    )(dmerged, merged, lse_all, *deps)


def _attention_fwd(n, wqkv, wo, tag):
    ns, qkvs, exts = [], [], []
    for g, (_, dil) in enumerate(DILATED_CFG):
        ng = n if dil == 1 else _residue_order(n, dil, f"{tag}_order_g{g}")
        qkv = _qkv_proj(ng, wqkv, g, f"{tag}_qkv_g{g}")
        ext = _attn_fwd(qkv, g, f"{tag}_fwd_g{g}")
        ns.append(ng)
        qkvs.append(qkv)
        exts.append(ext if dil == 1 else _token_order(ext, dil, None, f"{tag}_unorder_g{g}"))
    merged, merged_bf, lse_all = _attn_merge(*exts, f"{tag}_merge")
    m = _matmul(merged_bf, wo, "nn", F32, f"{tag}_wo")
    return m, (ns, qkvs, merged, merged_bf, lse_all)


def _attention_bwd(dm, wqkv, wo, saved, tag, dep=None, hook=None):
    ns, qkvs, merged, merged_bf, lse_all = saved
    d_wo = _matmul(merged_bf, dm, "tn", BF16, f"{tag}_dwo")
    dmerged = _matmul(dm, wo, "nt", F32, f"{tag}_dmerged")
    dext = _attn_dvec(dmerged, merged, lse_all, f"{tag}_dvec", dep)
    width = 3 * ATTN_WIDTH
    d_wqkv, dn, dep = [], None, None
    for g, (_, dil) in enumerate(DILATED_CFG):
        dext_g = dext if dil == 1 else _residue_order(dext, dil, f"{tag}_dorder_g{g}")
        dqkv = _attn_bwd(qkvs[g], dext_g, g, f"{tag}_bwd_g{g}", dep)
        dep = hook(g, dqkv) if hook is not None and g + 1 < N_GROUPS_A else None
        d_wqkv.append(_matmul(ns[g], dqkv, "tn", BF16, f"{tag}_dwqkv_g{g}", b_parts=3))
        dn_g = _matmul(dqkv, wqkv[:, g * width:(g + 1) * width], "nt", F32, f"{tag}_dn_g{g}", a_parts=3)
        dn = dn_g if dil == 1 else _token_order(dn_g, dil, dn, f"{tag}_dn_sum_g{g}")
    return dn, jnp.concatenate(d_wqkv, axis=1), d_wo


def _layer_matrices(i):
    mixer = (("attn_w_qkv", "attn_w_o"), ("conv_w_in", "conv_w_out"), ("pool_w_in", "pool_w_grp", "pool_w_out"))[i % 3]
    return [(k, i // 3) for k in mixer] + [("ffn_w_up", i), ("ffn_w_down", i)]


def _local_step(x, tgt, vec, weights, sink):
    ng = vec["norm_g"]

    def gain(i, j, token=None):
        g = ng[i, j][None, :]
        return g if token is None else g + token

    saved = []
    n = _rms_fwd(x, gain(0, 0), None, BF16, "norm_first")
    for i in range(DEPTH):
        wl = weights.layer(i)
        t0 = weights.hook(i, 0, n)
        kind, idx = i % 3, i // 3
        if kind == 0:
            m, ms = _attention_fwd(n, wl["attn_w_qkv"], wl["attn_w_o"], "attn")
        elif kind == 1:
            taps = vec["conv_w_dw"][idx] if t0 is None else vec["conv_w_dw"][idx] + t0
            z, y = _sconv_fwd(n, wl["conv_w_in"], taps, "sconv_fwd")
            m = _matmul(y, wl["conv_w_out"], "nn", F32, "sconv_out")
            ms = (z, y)
        else:
            scale = vec["pool_scale"][idx][None, :] if t0 is None else vec["pool_scale"][idx][None, :] + t0
            p, y = _pool_fwd(n, wl["pool_w_in"], wl["pool_w_grp"], scale, "pool_fwd")
            m = _matmul(y, wl["pool_w_out"], "nn", F32, "pool_out")
            ms = (p, y)
        t1 = weights.hook(i, 1, m)
        x1, n2 = _rms_res_pre(m, gain(i, 1, t0), x, gain(i, 2, t1), "norm_res_pre")
        h, c, a = _ffn_up(n2, wl["ffn_w_up"], vec["ffn_w_dw"][i], "ffn_up")
        t2 = weights.hook(i, 2, a)
        f = _matmul(a, wl["ffn_w_down"].reshape(D_FF, D_MODEL), "nn", F32, "ffn_down", a_parts=FFN_PAIRS)
        saved.append((x, n, m, ms, x1, n2, h, a, f, wl, c))
        if i + 1 < DEPTH:
            x, n = _rms_res_pre(f, gain(i, 3, t2), x1, gain(i + 1, 0), "norm_res_pre")
        else:
            x = _rms_fwd(f, gain(i, 3), x1, F32, "norm_res")
        weights.hook(i, 3, x)

    loss, dx = _loss_head(x, tgt, "loss_head")

    g_norm = [[None] * 4 for _ in range(DEPTH)]
    g_taps, g_scale, g_ffn_dw = [], [], [None] * DEPTH
    df, g_norm[DEPTH - 1][3] = _rms_bwd(saved[-1][8], gain(DEPTH - 1, 3), dx, None, BF16, "norm_bwd_sub")
    t0 = None
    for i in reversed(range(DEPTH)):
        xin, n, m, ms, x1, n2, h, a, f, wl, c = saved[i]
        kind, idx = i % 3, i // 3
        gl = {}
        d_wdown = _matmul(a, df, "tn", BF16, "ffn_dwdown", a_parts=FFN_PAIRS)
        gl["ffn_w_down"] = d_wdown.reshape(N_DEV, D_FF // N_DEV, D_MODEL)
        ffn_taps = vec["ffn_w_dw"][i] if t0 is None else vec["ffn_w_dw"][i] + t0
        dh, dwg, dwu = _ffn_mid_bwd(df, wl["ffn_w_down"].reshape(FFN_PAIRS, -1, D_MODEL), h, c, ffn_taps, "ffn_mid_bwd")
        g_ffn_dw[i] = jnp.concatenate([dwg, dwu], axis=0)
        t1 = sink.hook(i, 1, dh)
        gl["ffn_w_up"] = _ffn_dwup(n2, dh, "ffn_dwup")
        tf = sink.ffn_done(i, gl)
        dn2 = _ffn_dn(dh, wl["ffn_w_up"], "ffn_dn", t1)
        dx1, dm, g_norm[i][2], g_norm[i][1] = _rms_bwd_pair(x1, gain(i, 2, tf), dn2, dx, m, gain(i, 1), "norm_bwd_pair")
        t2 = sink.hook(i, 2, dm)
        if kind == 0:
            dn, gl["attn_w_qkv"], gl["attn_w_o"] = _attention_bwd(
                dm, wl["attn_w_qkv"], wl["attn_w_o"], ms, "attn", t2, lambda g, after, i=i: sink.hook(i, ("a", "b")[g], after))
        elif kind == 1:
            z, y = ms
            gl["conv_w_out"] = _matmul(y, dm, "tn", BF16, "sconv_dwout")
            taps = vec["conv_w_dw"][idx] if t2 is None else vec["conv_w_dw"][idx] + t2
            dz, ddw = _sconv_mid_bwd(dm, wl["conv_w_out"], z, taps, "sconv_mid_bwd")
            g_taps.append(ddw)
            gl["conv_w_in"] = _matmul(n, dz, "tn", BF16, "sconv_dwin", b_parts=3)
            dn = _matmul(dz, wl["conv_w_in"], "nt", F32, "sconv_dn", a_parts=3)
        else:
            p, y = ms
            gl["pool_w_out"] = _matmul(y, dm, "tn", BF16, "pool_dwout")
            scale = vec["pool_scale"][idx][None, :] if t2 is None else vec["pool_scale"][idx][None, :] + t2
            du, gl["pool_w_grp"], dscale = _pool_mid_bwd(dm, wl["pool_w_out"], p, wl["pool_w_grp"], scale, "pool_mid_bwd")
            g_scale.append(dscale[0])
            gl["pool_w_in"] = _matmul(n, du, "tn", BF16, "pool_dwin")
            dn = _matmul(du, wl["pool_w_in"], "nt", F32, "pool_dn")
        sink.hook(i, 3, dn)
        if i > 0:
            dx, df, g_norm[i][0], g_norm[i - 1][3] = _rms_bwd_pair(xin, gain(i, 0, t2), dn, dx1, saved[i - 1][8],
                                                                   gain(i - 1, 3), "norm_bwd_pair")
        else:
            dx, g_norm[0][0] = _rms_bwd(xin, gain(0, 0), dn, dx1, F32, "norm_bwd_res")
        t0 = sink.layer_done(i, gl)

    vec_grads = {"norm_g": jnp.stack([jnp.concatenate(row, axis=0) for row in g_norm]), "conv_w_dw": jnp.stack(g_taps),
                 "pool_scale": jnp.stack(g_scale), "ffn_w_dw": g_ffn_dw}
    return loss, dx, vec_grads


_AXES = ("x", "y", "c")
ROUTE_A = ("y", "x", "c")
ROUTE_B = ("x", "y", "c")
def _dev_index(pos):
    return 4 * pos["x"] + 2 * pos["y"] + pos["c"]


_HBM = pl.BlockSpec(memory_space=pltpu.HBM)
_SEM = pl.BlockSpec(memory_space=pltpu.SEMAPHORE)
_ANY = pl.BlockSpec(memory_space=pl.ANY)
_EFFECT = pltpu.SideEffectType.DATAFLOW_SIDE_EFFECTING


TOKEN_SHAPE = (1, D_MODEL)


def _copies_start(describe, arrays, n_copies, name, after, token_shape=TOKEN_SHAPE):
    n = len(arrays)
    deps = [] if after is None else [after]

    def body(*refs):
        send_sems, recv_sems = refs[n + len(deps)], refs[n + len(deps) + 1]
        for c in describe(refs[:n], send_sems, recv_sems):
            c.start()
        refs[-1][...] = jnp.zeros_like(refs[-1])

    outs = pl.pallas_call(
        body, name=f"{name}_start",
        out_shape=(pltpu.SemaphoreType.DMA((n_copies,)), pltpu.SemaphoreType.DMA((n_copies,)),
                   *[pltpu.HBM(a.shape, a.dtype) for a in arrays], jax.ShapeDtypeStruct(token_shape, F32)),
        in_specs=[_HBM] * n + [_ANY] * len(deps),
        out_specs=(_SEM, _SEM, *([_HBM] * n), pl.BlockSpec(memory_space=pltpu.VMEM)),
        input_output_aliases={i: 2 + i for i in range(n)},
        compiler_params=pltpu.CompilerParams(has_side_effects=_EFFECT),
    )(*[pltpu.with_memory_space_constraint(a, pltpu.HBM) for a in arrays], *deps)
    return (outs[0], outs[1], list(outs[2:2 + n])), outs[-1]


def _copies_wait(describe, handle, name, after):
    send_sems, recv_sems, arrays = handle
    n = len(arrays)
    deps = [] if after is None else list(after) if isinstance(after, (list, tuple)) else [after]

    def body(*refs):
        for c in describe(refs[:n], refs[n], refs[n + 1]):
            c.wait_send()
            c.wait_recv()

    outs = pl.pallas_call(
        body, name=f"{name}_wait",
        out_shape=tuple(pltpu.HBM(a.shape, a.dtype) for a in arrays),
        in_specs=[_HBM] * n + [_SEM, _SEM] + [_ANY] * len(deps), out_specs=tuple([_HBM] * n),
        input_output_aliases={i: i for i in range(n)},
        compiler_params=pltpu.CompilerParams(has_side_effects=_EFFECT),
    )(*arrays, send_sems, recv_sems, *deps)
    return list(outs)


GATHER_STAGE_COPIES = (3, 3, 1)


def _gather_copies(stage, routes):
    n = len(routes)

    def describe(refs, send_sems, recv_sems):
        pos = {a: lax.axis_index(a) for a in _AXES}

        def flipped(axes):
            return {a: 1 - pos[a] if a in axes else pos[a] for a in _AXES}

        copies = []
        for i, (a1, a2, a3) in enumerate(routes):
            land = refs[n + i] if stage == 1 else refs[i]
            p1, p2, p12, p3 = flipped((a1,)), flipped((a2,)), flipped((a1, a2)), flipped((a3,))
            plan = {1: [(None, p1), (None, p2), (None, p3)], 2: [(p1, p2), (p1, p3), (p2, p3)], 3: [(p12, p3)]}[stage]
            for holder, to in plan:
                slot = land.at[_dev_index(pos if holder is None else holder)]
                k = len(copies)
                copies.append(pltpu.make_async_remote_copy(
                    src_ref=refs[i] if holder is None else slot, dst_ref=slot,
                    send_sem=send_sems.at[k], recv_sem=recv_sems.at[k],
                    device_id=tuple(to[a] for a in _AXES), device_id_type=pl.DeviceIdType.MESH))
        return copies

    return describe


def _gather_begin(shards, routes, name, after):
    n = len(shards)
    lands = [lax.empty((N_DEV,) + a.shape, a.dtype) for a in shards]
    handle, token = _copies_start(_gather_copies(1, routes), list(shards) + lands, GATHER_STAGE_COPIES[0] * n,
                                  f"{name}_1", after)
    return {"stage": 1, "handle": handle, "routes": routes, "name": name, "n": n}, token


def _gather_next(state, after):
    stage, routes, name, n = state["stage"], state["routes"], state["name"], state["n"]
    arrays = _copies_wait(_gather_copies(stage, routes), state["handle"], f"{name}_{stage}", after)
    if stage == 1:
        state = dict(state, shards=arrays[:n])
        arrays = arrays[n:]
    if stage == 3:
        me = _dev_index({a: lax.axis_index(a) for a in _AXES})
        return [lax.dynamic_update_index_in_dim(o, s, me, 0) for o, s in zip(arrays, state["shards"])], None
    handle, token = _copies_start(_gather_copies(stage + 1, routes), arrays, GATHER_STAGE_COPIES[stage] * n,
                                  f"{name}_{stage + 1}", None)
    return dict(state, stage=stage + 1, handle=handle), token


ADD_ROW_TILES = (1024, 704, 512, 352, 256, 128, 96, 64, 32, 16)


def _add_half(a, recv, me, out_dtype, name):
    p, q, cols = recv.shape
    tr = _pick(q, ADD_ROW_TILES)

    def body(me_ref, a_ref, b_ref, o_ref):
        o_ref[...] = (a_ref[...].astype(F32) + b_ref[...].astype(F32)).astype(o_ref.dtype)

    return pl.pallas_call(
        body, name=name,
        grid_spec=pltpu.PrefetchScalarGridSpec(
            num_scalar_prefetch=1, grid=(p, q // tr),
            in_specs=[pl.BlockSpec((None, None, tr, cols), lambda j, i, m: (j, m[0], i, 0)),
                      pl.BlockSpec((None, tr, cols), lambda j, i, m: (j, i, 0))],
            out_specs=pl.BlockSpec((None, tr, cols), lambda j, i, m: (j, i, 0))),
        out_shape=jax.ShapeDtypeStruct((p, q, cols), out_dtype),
        compiler_params=_params(("parallel", "parallel")),
    )(me, a, recv)


def _half_copies(axes):
    n = len(axes)

    def describe(refs, send_sems, recv_sems):
        pos = {a: lax.axis_index(a) for a in _AXES}
        copies = []
        for i, axis in enumerate(axes):
            peer = tuple(1 - pos[a] if a == axis else pos[a] for a in _AXES)
            copies.append(pltpu.make_async_remote_copy(
                src_ref=refs[i].at[:, 1 - pos[axis]], dst_ref=refs[n + i], send_sem=send_sems.at[i],
                recv_sem=recv_sems.at[i], device_id=peer, device_id_type=pl.DeviceIdType.MESH))
        return copies

    return describe


def _scatter_begin(slots, routes, tags, name, token_shape=TOKEN_SHAPE):
    shapes = [a.shape[1:] for a in slots]
    rows = [math.prod(s[:-1]) for s in shapes]
    arrays = [a.reshape(4, 2, n, s[-1]) for a, n, s in zip(slots, rows, shapes)]
    return _scatter_start({"stage": 0, "arrays": arrays, "routes": routes, "tags": tags, "name": name,
                           "shapes": shapes, "rows": rows}, token_shape)


def _scatter_start(state, token_shape=TOKEN_SHAPE):
    stage, arrays = state["stage"], state["arrays"]
    axes = [r[2 - stage] for r in state["routes"]]
    lands = [lax.empty((a.shape[0],) + a.shape[2:], a.dtype) for a in arrays]
    handle, token = _copies_start(_half_copies(axes), arrays + lands, len(arrays), f"{state['name']}_{stage + 1}", None,
                                  token_shape)
    return dict(state, handle=handle, axes=axes), token


def _scatter_next(state, after):
    stage, axes, n = state["stage"], state["axes"], len(state["arrays"])
    both = _copies_wait(_half_copies(axes), state["handle"], f"{state['name']}_{stage + 1}", after)
    coord = {a: lax.axis_index(a).astype(jnp.int32).reshape(1) for a in _AXES}
    sums = [_add_half(a, r, coord[ax], F32 if stage == 2 else BF16, f"scatter_add_{stage + 1}_{t}")
            for a, r, ax, t in zip(both[:n], both[n:], axes, state["tags"])]
    if stage == 2:
        return [a.reshape(s) for a, s in zip(sums, state["shapes"])], None
    if stage == 0:
        views = [(1, 2, 2 * r, s[-1]) if route[1] == "x" else (2, 2, r, s[-1])
                 for r, s, route in zip(state["rows"], state["shapes"], state["routes"])]
    else:
        views = [(1, 2, r, s[-1]) for r, s in zip(state["rows"], state["shapes"])]
    return _scatter_start(dict(state, stage=stage + 1, arrays=[a.reshape(v) for a, v in zip(sums, views)]))


_WEIGHTS = {
    "norm_g": ((DEPTH, 4, D_MODEL), 2, True),
    "attn_w_qkv": ((2, D_MODEL, 4608), 2, False),
    "attn_w_o": ((2, ATTN_WIDTH, D_MODEL), 2, False),
    "conv_w_in": ((1, D_MODEL, 3 * D_MODEL), 2, False),
    "conv_w_dw": ((1, 3, D_MODEL), 2, True),
    "conv_w_out": ((1, D_MODEL, D_MODEL), 1, False),
    "pool_w_in": ((1, D_MODEL, D_MODEL), 1, False),
    "pool_w_grp": ((1, 4, POOL_GROUP_DIM, POOL_GROUP_DIM), 2, False),
    "pool_scale": ((1, D_MODEL), 1, True),
    "pool_w_out": ((1, D_MODEL, D_MODEL), 1, False),
    "ffn_w_up": ((DEPTH, D_MODEL, 2 * D_FF), 2, False),
    "ffn_w_dw": ((DEPTH, 3, 2 * D_FF), 2, True),
    "ffn_w_down": ((DEPTH, D_FF, D_MODEL), 1, False),
}
_NAMES = tuple(_WEIGHTS)
_VECTORS = tuple(k for k in _NAMES if _WEIGHTS[k][2])
_MATRICES = tuple(k for k in _NAMES if not _WEIGHTS[k][2])
_FFN = ("ffn_w_up", "ffn_w_down")
_ON_ROUTE_A = ("ffn_w_up", "attn_w_o", "conv_w_out", "pool_w_in")
PACK_ROWS = 16


def _route(name):
    return ROUTE_A if name in _ON_ROUTE_A else ROUTE_B


def _shard_shape(name):
    shape, ax, _ = _WEIGHTS[name]
    return tuple(s // N_DEV if i == ax else s for i, s in enumerate(shape))


def _full_from_slots(slots, name, layers=None):
    shape, ax, _ = _WEIGHTS[name]
    if layers is not None:
        shape = (layers,) + shape[1:]
    return jnp.moveaxis(slots, 0, ax).reshape(shape)


def _slots_from_full(full, name):
    shape, ax, _ = _WEIGHTS[name]
    split = shape[:ax] + (N_DEV, shape[ax] // N_DEV) + shape[ax + 1:]
    return jnp.moveaxis(full.reshape(split), ax, 0)


def _pack_vectors(parts, lead):
    rows = []
    for k in _VECTORS:
        r = parts[k].reshape(lead + (-1, LANES))
        pad = -r.shape[-2] % PACK_ROWS
        rows.append(jnp.pad(r, [(0, 0)] * len(lead) + [(0, pad), (0, 0)]))
    return jnp.concatenate(rows, axis=len(lead))


def _unpack_vectors(buf, lead):
    out, r0 = {}, 0
    for k in _VECTORS:
        shard = _shard_shape(k)
        rows = math.prod(shard) // LANES
        out[k] = buf[..., r0:r0 + rows, :].reshape(lead + shard)
        r0 += rows + (-rows % PACK_ROWS)
    return out


class _LayerWeights:
    def __init__(self, shards):
        first, ffn0 = _layer_matrices(0)[:-2], _layer_matrices(0)[-2:]
        state, token = _gather_begin([shards[k][j].astype(BF16) for k, j in first] + [_pack_vectors(shards, ())],
                                     [_route(k) for k, _ in first] + [ROUTE_B], "gather0", None)
        self.cast = {k: (shards[k] + token[0, 0]).astype(BF16) for k in _MATRICES}
        state, _ = _gather_next(state, self._send(ffn0) + self._send(_layer_matrices(1)))
        state, _ = _gather_next(state, None)
        outs, _ = _gather_next(state, None)
        vec = _unpack_vectors(outs[-1], (N_DEV,))
        self.vec = {k: _full_from_slots(vec[k], k) for k in _VECTORS}
        self.vec["ffn_w_dw"] = [vec["ffn_w_dw"][:, l] for l in range(DEPTH)]
        self.ready = {0: self._unpack(first, outs[:-1])}
        self.chains = {}
        tokens = []
        self._begin("ffn0", ffn0, "gather0f", outs[0], tokens)
        self._begin(1, _layer_matrices(1), "gather1", outs[0], tokens)
        self.vec["norm_g"] = self.vec["norm_g"] + (tokens[0] + tokens[1])

    def _send(self, items):
        return [self.cast[k][j] for k, j in items]

    @staticmethod
    def _unpack(items, outs):
        return {k: o if k in _FFN else _full_from_slots(o[:, None], k, layers=1)[0] for (k, _), o in zip(items, outs)}

    def _begin(self, key, items, name, after, tokens):
        state, token = _gather_begin(self._send(items), [_route(k) for k, _ in items], name, after)
        self.chains[key] = (items, state)
        tokens.append(token)

    def _advance(self, key, after, tokens):
        items, state = self.chains.pop(key)
        state, token = _gather_next(state, after)
        if token is None:
            self.ready.setdefault(0 if key == "ffn0" else key, {}).update(self._unpack(items, state))
        else:
            self.chains[key] = (items, state)
            tokens.append(token)

    def layer(self, i):
        return self.ready[i]

    def hook(self, i, point, after):
        tokens = []
        if i == 0 and point == 0:
            self._advance("ffn0", after, tokens)
        if i == 0 and point == 1:
            self._advance("ffn0", after, tokens)
            self._advance("ffn0", None, tokens)
        if point >= 1 and i + 1 in self.chains:
            self._advance(i + 1, after, tokens)
        if point == 1 and i + 2 < DEPTH:
            self._begin(i + 2, _layer_matrices(i + 2), f"gather{i + 2}", after, tokens)
        return functools.reduce(lambda a, b: a + b, tokens) if tokens else None


def _layer_slots(g, name):
    shape, ax, _ = _WEIGHTS[name]
    shape, ax = shape[1:], ax - 1
    split = shape[:ax] + (N_DEV, shape[ax] // N_DEV) + shape[ax + 1:]
    return jnp.moveaxis(g.reshape(split), ax, 0).astype(BF16)


class _GradSink:
    def __init__(self):
        self.state = None
        self.ffn_state = None
        self.sums = {}
        self.last = None

    def ffn_done(self, i, grads):
        if i != 0:
            return None
        self.ffn_items = _layer_matrices(0)[-2:]
        self.ffn_state, token = _scatter_begin([grads[k] for k, _ in self.ffn_items],
                                               [_route(k) for k, _ in self.ffn_items],
                                               [f"{k}{j}" for k, j in self.ffn_items], "scatter0f")
        return token

    def layer_done(self, i, grads):
        items = _layer_matrices(i)
        if i == 0:
            items = items[:-2]
            self.last = (items, [_layer_slots(grads[k], k) for k, _ in items])
            return None
        slots = [grads[k] if k in _FFN else _layer_slots(grads[k], k) for k, _ in items]
        self.items = items
        self.state, token = _scatter_begin(slots, [_route(k) for k, _ in items], [f"{k}{j}" for k, j in items],
                                           f"scatter{i}", (N_DEV, 3, 2 * D_FF // N_DEV))
        return token

    def hook(self, i, point, after):
        tokens = []
        if self.state is not None and point in (1, 2, 3):
            self.state, token = _scatter_next(self.state, after)
            if point == 3:
                self.sums.update(dict(zip(self.items, self.state)))
                self.state = None
            tokens.append(token)
        if self.ffn_state is not None and point in ("a", "b", 3):
            self.ffn_state, token = _scatter_next(self.ffn_state, after)
            if point == 3:
                self.sums.update(dict(zip(self.ffn_items, self.ffn_state)))
                self.ffn_state = None
            tokens.append(token)
        tokens = [t for t in tokens if t is not None]
        return functools.reduce(lambda a, b: a + b, tokens) if tokens else None


def _adamw(w, g, m, v, name, layer=None, prev=None, dep=None):
    shape = w.shape
    cols = shape[-1]
    view = shape if len(shape) == 3 else (1, math.prod(shape[:-1]), cols)
    layers, rows, _ = view
    tr = _pick(rows, (512, 352, 288, 256, 128, 64, 32, 16, 8))
    n_prev = 0 if prev is None else 3
    lead = ([] if prev is None else [p.reshape(view) for p in prev]) + ([] if dep is None else [dep])

    def body(*refs):
        w_ref, g_ref, m_ref, v_ref = refs[len(lead):len(lead) + 4]
        d_ref, nm_ref, nv_ref = refs[len(lead) + 4:]
        gv = g_ref[...]
        nm = ADAM_B1 * m_ref[...] + (1.0 - ADAM_B1) * gv
        nv = ADAM_B2 * v_ref[...] + (1.0 - ADAM_B2) * jnp.square(gv)
        m_hat = nm / (1.0 - ADAM_B1 ** ADAM_STEP)
        v_hat = nv / (1.0 - ADAM_B2 ** ADAM_STEP)
        d_ref[...] = -ADAM_LR * (m_hat / (jnp.sqrt(v_hat) + ADAM_EPS) + ADAM_WD * w_ref[...])
        nm_ref[...] = nm
        nv_ref[...] = nv

    if layer is None:
        grid = (layers, rows // tr)
        blk = gblk = pl.BlockSpec((None, tr, cols), lambda l, i: (l, i, 0))
        gview = view
    else:
        grid = (rows // tr,)
        blk = pl.BlockSpec((None, tr, cols), lambda i: (layer, i, 0))
        gblk = pl.BlockSpec((tr, cols), lambda i: (i, 0))
        gview = (rows, cols)
    shp = jax.ShapeDtypeStruct(view, F32)
    outs = pl.pallas_call(
        body, name=name, grid=grid, in_specs=[_ANY] * len(lead) + [blk, gblk, blk, blk], out_specs=[blk] * 3,
        out_shape=[shp] * 3, input_output_aliases={i: i for i in range(n_prev)},
        compiler_params=_params(("parallel",) * len(grid)),
    )(*lead, w.reshape(view), g.reshape(gview), m.reshape(view), v.reshape(view))
    return [o.reshape(shape) for o in outs]


def kernel(x, norm_g, attn_w_qkv, attn_w_o, conv_w_in, conv_w_dw, conv_w_out, pool_w_in, pool_w_grp, pool_scale, pool_w_out, ffn_w_up, ffn_w_dw, ffn_w_down, loss_target, m_norm_g, m_attn_w_qkv, m_attn_w_o, m_conv_w_in, m_conv_w_dw, m_conv_w_out, m_pool_w_in, m_pool_w_grp, m_pool_scale, m_pool_w_out, m_ffn_w_up, m_ffn_w_dw, m_ffn_w_down, v_norm_g, v_attn_w_qkv, v_attn_w_o, v_conv_w_in, v_conv_w_dw, v_conv_w_out, v_pool_w_in, v_pool_w_grp, v_pool_scale, v_pool_w_out, v_ffn_w_up, v_ffn_w_dw, v_ffn_w_down):
    shards = dict(zip(_NAMES, (norm_g, attn_w_qkv, attn_w_o, conv_w_in, conv_w_dw, conv_w_out, pool_w_in,
                               pool_w_grp, pool_scale, pool_w_out, ffn_w_up, ffn_w_dw, ffn_w_down)))
    moms = dict(zip(_NAMES, (m_norm_g, m_attn_w_qkv, m_attn_w_o, m_conv_w_in, m_conv_w_dw, m_conv_w_out,
                             m_pool_w_in, m_pool_w_grp, m_pool_scale, m_pool_w_out, m_ffn_w_up, m_ffn_w_dw,
                             m_ffn_w_down)))
    vels = dict(zip(_NAMES, (v_norm_g, v_attn_w_qkv, v_attn_w_o, v_conv_w_in, v_conv_w_dw, v_conv_w_out,
                             v_pool_w_in, v_pool_w_grp, v_pool_scale, v_pool_w_out, v_ffn_w_up, v_ffn_w_dw,
                             v_ffn_w_down)))
    weights = _LayerWeights(shards)
    sink = _GradSink()
    loss, grad_x, vec_grads = _local_step(x[0], loss_target[0], weights.vec, weights, sink)
    loss = lax.psum(loss[0, 0], _AXES)

    items, slots = sink.last
    vec_slots = {k: _slots_from_full(vec_grads[k], k) for k in _VECTORS if k != "ffn_w_dw"}
    vec_slots["ffn_w_dw"] = jnp.stack(vec_grads["ffn_w_dw"], axis=1)
    state, token = _scatter_begin(slots + [_pack_vectors(vec_slots, (N_DEV,)).astype(BF16)],
                                  [_route(k) for k, _ in items] + [ROUTE_B],
                                  [f"{k}{j}" for k, j in items] + ["vectors"], "scatter0")
    results = {}

    flipped = {k for k in _MATRICES if _WEIGHTS[k][0][0] > 1 and _shard_shape(k)[-1] % LANES}
    wmv = {k: [t.transpose(0, 2, 1) if k in flipped else t for t in (shards[k], moms[k], vels[k])] for k in _MATRICES}

    def step(matrices, dep=None):
        outs = []
        for k, j in matrices:
            g = sink.sums[(k, j)]
            w, m, v = wmv[k]
            if _WEIGHTS[k][0][0] == 1:
                results[k] = (g[None], _adamw(w, g[None], m, v, f"adamw_{k}", dep=dep))
            else:
                gs, prev = results.get(k, ({}, None))
                gs[j] = g
                results[k] = (gs, _adamw(w, g.T if k in flipped else g, m, v, f"adamw_{k}{j}", layer=j, prev=prev, dep=dep))
            outs.append(results[k][1][0])
        return outs

    state, token = _scatter_next(state, step(_layer_matrices(3), token))
    state, token = _scatter_next(state, step(_layer_matrices(2) + _layer_matrices(1), token))
    sums, _ = _scatter_next(state, step(_layer_matrices(0)[-2:], token))
    sink.sums.update(dict(zip(items, sums[:-1])))
    step(items)
    vec_sums = _unpack_vectors(sums[-1], ())
    for k in _VECTORS:
        results[k] = (vec_sums[k], _adamw(shards[k], vec_sums[k], moms[k], vels[k], f"adamw_{k}"))
    grads_out = {k: g if not isinstance(g, dict) else jnp.stack([g[j] for j in range(len(g))])
                 for k, (g, _) in results.items()}
    stepped = {k: [o.transpose(0, 2, 1) if k in flipped else o for o in outs] for k, (_, outs) in results.items()}
    return (loss, grad_x[None], *[grads_out[k] for k in _NAMES], *[stepped[k][0] for k in _NAMES],
            *[stepped[k][1] for k in _NAMES], *[stepped[k][2] for k in _NAMES])
```

```python
import functools
import math

import numpy as np
import jax
import jax.numpy as jnp
from jax import lax
from jax.experimental import pallas as pl
from jax.experimental.pallas import tpu as pltpu

F32, BF16 = jnp.float32, jnp.bfloat16

D_MODEL = 1024
SEQ = 2048
DEPTH = 4
DILATED_CFG = ((128, 1), (512, 4), (2048, 16))
N_GROUPS_A = 3
HEADS = 8
HEAD_DIM = 64
ATTN_WIDTH = HEADS * HEAD_DIM
N_HEADS_A = N_GROUPS_A * HEADS
BLOCK = 128
NEG_INF = -1e30
POOL_GROUP_DIM = 256
D_FF = 2816
RMS_EPS = 1e-6
ADAM_LR, ADAM_B1, ADAM_B2, ADAM_EPS, ADAM_WD, ADAM_STEP = 0.001, 0.9, 0.999, 1e-08, 0.01, 10

N_DEV = 8
LANES = 128
V7X_VMEM_BYTES = 64 * 2 ** 20
VMEM_LIMIT_BYTES = V7X_VMEM_BYTES - 8 * 2 ** 20
COL_TILE = 256
ROW_TILE = 512
MATMUL_TILES = (1024, 1408, 512, 256, 128)
TN_RESIDENT_K = 2048

NN = (((1,), (0,)), ((), ()))
NT = (((1,), (1,)), ((), ()))
TN = (((0,), (0,)), ((), ()))


def _dot(a, b, dims=NN):
    return lax.dot_general(a, b, dims, preferred_element_type=F32)


def _params(sem=None):
    return pltpu.CompilerParams(dimension_semantics=sem, vmem_limit_bytes=VMEM_LIMIT_BYTES)


def _pick(n, prefs):
    for p in prefs:
        if n % p == 0:
            return p
    return n


def _matmul(a, b, mode, out_dtype, name, a_parts=1, b_parts=1):
    if mode == "nn":
        m, k = a.shape[-2], a.shape[-1] * a_parts
        n = b.shape[-1] * b_parts
    elif mode == "nt":
        m, k = a.shape[-2], a.shape[-1] * a_parts
        n = b.shape[-2]
    else:
        k, m = a.shape[-2], a.shape[-1] * a_parts
        n = b.shape[-1] * b_parts
    tm = _pick(m, MATMUL_TILES)
    tn = _pick(n // b_parts if mode != "nt" else n, MATMUL_TILES)
    kk = k // a_parts if mode != "tn" else k
    tk = _pick(kk, MATMUL_TILES)
    if mode == "tn":
        tm = _pick(m // a_parts, MATMUL_TILES)
        if k <= TN_RESIDENT_K:
            tk = k
    gm, gn, gk = m // tm, n // tn, k // tk

    def a_idx(i, j, kq):
        if mode == "tn":
            r, c, per = kq, i, (m // a_parts) // tm
        else:
            r, c, per = i, kq, (k // a_parts) // tk
        return (r, c) if a_parts == 1 else (c // per, r, c % per)

    def b_idx(i, j, kq):
        if mode == "nt":
            return (j, kq)
        per = (n // b_parts) // tn
        return (kq, j) if b_parts == 1 else (j // per, kq, j % per)

    a_blk = (tk, tm) if mode == "tn" else (tm, tk)
    b_blk = (tn, tk) if mode == "nt" else (tk, tn)
    if a_parts > 1:
        a_blk = (None,) + a_blk
    if b_parts > 1:
        b_blk = (None,) + b_blk
    dims = {"nn": NN, "nt": NT, "tn": TN}[mode]

    def body_single(a_ref, b_ref, o_ref):
        o_ref[...] = _dot(a_ref[...], b_ref[...], dims).astype(o_ref.dtype)

    def body(a_ref, b_ref, o_ref, acc_ref):
        kq = pl.program_id(2)

        @pl.when(kq == 0)
        def _():
            acc_ref[...] = jnp.zeros_like(acc_ref)

        acc_ref[...] += _dot(a_ref[...], b_ref[...], dims)

        @pl.when(kq == gk - 1)
        def _():
            o_ref[...] = acc_ref[...].astype(o_ref.dtype)

    return pl.pallas_call(
        body_single if gk == 1 else body, name=name, grid=(gm, gn, gk),
        in_specs=[pl.BlockSpec(a_blk, a_idx), pl.BlockSpec(b_blk, b_idx)],
        out_specs=pl.BlockSpec((tm, tn), lambda i, j, kq: (i, j)),
        out_shape=jax.ShapeDtypeStruct((m, n), out_dtype),
        scratch_shapes=[] if gk == 1 else [pltpu.VMEM((tm, tn), F32)],
        compiler_params=_params(("parallel", "parallel", "arbitrary")),
    )(a, b)


def _rms_fwd(xin, g, res, out_dtype, name):
    s, d = xin.shape
    has_res = res is not None

    def body(*refs):
        x_ref, g_ref = refs[0], refs[1]
        o_ref = refs[-1]
        x = x_ref[...]
        r = lax.rsqrt(jnp.mean(x * x, axis=-1, keepdims=True) + RMS_EPS)
        y = x * r * g_ref[...]
        if has_res:
            y = refs[2][...] + y
        o_ref[...] = y.astype(o_ref.dtype)

    row = pl.BlockSpec((ROW_TILE, d), lambda i: (i, 0))
    vec = pl.BlockSpec((1, d), lambda i: (0, 0))
    ins = [xin, g] + ([res] if has_res else [])
    return pl.pallas_call(
        body, name=name, grid=(s // ROW_TILE,),
        in_specs=[row, vec] + ([row] if has_res else []),
        out_specs=row, out_shape=jax.ShapeDtypeStruct((s, d), out_dtype),
        compiler_params=_params(("parallel",)),
    )(*ins)


def _rms_bwd(xin, g, dy, dres, out_dtype, name):
    s, d = xin.shape
    has_res = dres is not None

    def body(*refs):
        x_ref, g_ref, dy_ref = refs[0], refs[1], refs[2]
        dx_ref, dg_ref = refs[-2], refs[-1]

        @pl.when(pl.program_id(0) == 0)
        def _():
            dg_ref[...] = jnp.zeros_like(dg_ref)

        x = x_ref[...]
        dyv = dy_ref[...].astype(F32)
        r = lax.rsqrt(jnp.mean(x * x, axis=-1, keepdims=True) + RMS_EPS)
        xhat = x * r
        u = dyv * g_ref[...]
        dx = r * (u - xhat * jnp.mean(u * xhat, axis=-1, keepdims=True))
        if has_res:
            dx = refs[3][...] + dx
        dx_ref[...] = dx.astype(dx_ref.dtype)
        dg_ref[...] += jnp.sum(dyv * xhat, axis=0, keepdims=True)

    row = pl.BlockSpec((ROW_TILE, d), lambda i: (i, 0))
    vec = pl.BlockSpec((1, d), lambda i: (0, 0))
    ins = [xin, g, dy] + ([dres] if has_res else [])
    return pl.pallas_call(
        body, name=name, grid=(s // ROW_TILE,),
        in_specs=[row, vec, row] + ([row] if has_res else []),
        out_specs=[row, vec],
        out_shape=[jax.ShapeDtypeStruct((s, d), out_dtype), jax.ShapeDtypeStruct((1, d), F32)],
        compiler_params=_params(("arbitrary",)),
    )(*ins)


def _rms(x):
    r = lax.rsqrt(jnp.mean(x * x, axis=-1, keepdims=True) + RMS_EPS)
    return r, x * r


def _rms_grad(r, xhat, dy, g):
    u = dy * g
    return r * (u - xhat * jnp.mean(u * xhat, axis=-1, keepdims=True))


def _rms_res_pre(sub, g_post, res, g_pre, name):
    s, d = sub.shape

    def body(sub_ref, gp_ref, res_ref, gn_ref, x_ref, n_ref):
        xnew = res_ref[...] + _rms(sub_ref[...])[1] * gp_ref[...]
        x_ref[...] = xnew
        n_ref[...] = (_rms(xnew)[1] * gn_ref[...]).astype(BF16)

    row = pl.BlockSpec((ROW_TILE, d), lambda i: (i, 0))
    vec = pl.BlockSpec((1, d), lambda i: (0, 0))
    return pl.pallas_call(
        body, name=name, grid=(s // ROW_TILE,),
        in_specs=[row, vec, row, vec], out_specs=[row, row],
        out_shape=[jax.ShapeDtypeStruct((s, d), F32), jax.ShapeDtypeStruct((s, d), BF16)],
        compiler_params=_params(("parallel",)),
    )(sub, g_post, res, g_pre)


def _rms_bwd_pair(xmid, g_pre, dn, dres, sub, g_post, name):
    s, d = xmid.shape

    def body(x_ref, gn_ref, dn_ref, dres_ref, sub_ref, gp_ref, dx_ref, dsub_ref, dgn_ref, dgp_ref):
        @pl.when(pl.program_id(0) == 0)
        def _():
            dgn_ref[...] = jnp.zeros_like(dgn_ref)
            dgp_ref[...] = jnp.zeros_like(dgp_ref)

        dnv = dn_ref[...].astype(F32)
        r, xhat = _rms(x_ref[...])
        dx = dres_ref[...] + _rms_grad(r, xhat, dnv, gn_ref[...])
        dx_ref[...] = dx
        dgn_ref[...] += jnp.sum(dnv * xhat, axis=0, keepdims=True)
        rs, shat = _rms(sub_ref[...])
        dsub_ref[...] = _rms_grad(rs, shat, dx, gp_ref[...]).astype(BF16)
        dgp_ref[...] += jnp.sum(dx * shat, axis=0, keepdims=True)

    row = pl.BlockSpec((ROW_TILE, d), lambda i: (i, 0))
    vec = pl.BlockSpec((1, d), lambda i: (0, 0))
    return pl.pallas_call(
        body, name=name, grid=(s // ROW_TILE,),
        in_specs=[row, vec, row, row, row, vec], out_specs=[row, row, vec, vec],
        out_shape=[jax.ShapeDtypeStruct((s, d), F32), jax.ShapeDtypeStruct((s, d), BF16),
                   jax.ShapeDtypeStruct((1, d), F32), jax.ShapeDtypeStruct((1, d), F32)],
        compiler_params=_params(("arbitrary",)),
    )(xmid, g_pre, dn, dres, sub, g_post)


def _loss_head(y, tgt, name):
    s, d = y.shape

    def body(y_ref, t_ref, l_ref, dy_ref):
        @pl.when(pl.program_id(0) == 0)
        def _():
            l_ref[...] = jnp.zeros_like(l_ref)

        e = y_ref[...] - t_ref[...]
        dy_ref[...] = e / d
        per_tok = jnp.mean(e * e, axis=-1, keepdims=True)
        l_ref[...] += 0.5 * jnp.sum(per_tok, axis=0, keepdims=True)

    row = pl.BlockSpec((ROW_TILE, d), lambda i: (i, 0))
    return pl.pallas_call(
        body, name=name, grid=(s // ROW_TILE,),
        in_specs=[row, row],
        out_specs=[pl.BlockSpec((1, 1), lambda i: (0, 0)), row],
        out_shape=[jax.ShapeDtypeStruct((1, 1), F32), jax.ShapeDtypeStruct((s, d), F32)],
        compiler_params=_params(("arbitrary",)),
    )(y, tgt)


SUBLANES = 8


def _shift_down(x, k):
    t, c = x.shape
    r = pltpu.roll(x.reshape(t // SUBLANES, SUBLANES, c), k, axis=1)
    above = jnp.concatenate([jnp.zeros((1, SUBLANES, c), x.dtype), r[:-1]], axis=0)
    rows = lax.broadcasted_iota(jnp.int32, (1, SUBLANES, c), 1)
    return jnp.where(rows >= k, r, above).reshape(t, c)


def _shift_up(x, k):
    t, c = x.shape
    r = pltpu.roll(x.reshape(t // SUBLANES, SUBLANES, c), SUBLANES - k, axis=1)
    below = jnp.concatenate([r[1:], jnp.zeros((1, SUBLANES, c), x.dtype)], axis=0)
    rows = lax.broadcasted_iota(jnp.int32, (1, SUBLANES, c), 1)
    return jnp.where(rows < SUBLANES - k, r, below).reshape(t, c)


def _conv3(h, w):
    return w[2:3] * h + w[1:2] * _shift_down(h, 1) + w[0:1] * _shift_down(h, 2)


def _conv3_bwd(dc, h, w, dw_ref, cols=slice(None)):
    u1, u2 = _shift_up(dc, 1), _shift_up(dc, 2)
    dw_ref[0:1, cols] = jnp.sum(u2 * h, axis=0, keepdims=True)
    dw_ref[1:2, cols] = jnp.sum(u1 * h, axis=0, keepdims=True)
    dw_ref[2:3, cols] = jnp.sum(dc * h, axis=0, keepdims=True)
    return w[2:3] * dc + w[1:2] * u1 + w[0:1] * u2


FFN_PAIRS = N_DEV // 2


def _lane_chunks(width):
    return [(c0, min(COL_TILE, width - c0)) for c0 in range(0, width, COL_TILE)]


def _ffn_up(n, wup, wdw, name):
    s, d = n.shape
    cw = wup.shape[-1]

    def body(n_ref, wg_ref, wu_ref, dg_ref, du_ref, h_ref, c_ref, a_ref):
        x = n_ref[...]
        for c0, size in _lane_chunks(cw):
            cols = slice(c0, c0 + size)
            hg = _dot(x, wg_ref[:, cols])
            hu = _dot(x, wu_ref[:, cols])
            h_ref[0, :, cols] = hg.astype(BF16)
            h_ref[1, :, cols] = hu.astype(BF16)
            cg = _conv3(hg, dg_ref[:, cols])
            cu = _conv3(hu, du_ref[:, cols])
            c_ref[0, :, cols] = cg.astype(BF16)
            c_ref[1, :, cols] = cu.astype(BF16)
            a_ref[:, cols] = (cg * jax.nn.sigmoid(cg) * cu).astype(BF16)

    return pl.pallas_call(
        body, name=name, grid=(FFN_PAIRS,),
        in_specs=[pl.BlockSpec((s, d), lambda j: (0, 0)),
                  pl.BlockSpec((None, d, cw), lambda j: (j, 0, 0)),
                  pl.BlockSpec((None, d, cw), lambda j: (j + FFN_PAIRS, 0, 0)),
                  pl.BlockSpec((None, 3, cw), lambda j: (j, 0, 0)),
                  pl.BlockSpec((None, 3, cw), lambda j: (j + FFN_PAIRS, 0, 0))],
        out_specs=[pl.BlockSpec((None, 2, s, cw), lambda j: (j, 0, 0, 0)),
                   pl.BlockSpec((None, 2, s, cw), lambda j: (j, 0, 0, 0)),
                   pl.BlockSpec((None, s, cw), lambda j: (j, 0, 0))],
        out_shape=[jax.ShapeDtypeStruct((FFN_PAIRS, 2, s, cw), BF16), jax.ShapeDtypeStruct((FFN_PAIRS, 2, s, cw), BF16),
                   jax.ShapeDtypeStruct((FFN_PAIRS, s, cw), BF16)],
        compiler_params=_params(("parallel",)),
    )(n, wup, wup, wdw, wdw)


def _ffn_mid_bwd(do, wdown, h, c, wdw, name):
    s, d = do.shape
    cw = wdown.shape[1]

    def body(do_ref, wd_ref, h_ref, c_ref, wg_ref, wu_ref, dh_ref, dwg_ref, dwu_ref):
        dov = do_ref[...]
        for c0, size in _lane_chunks(cw):
            cols = slice(c0, c0 + size)
            da = _dot(dov, wd_ref[cols, :], NT)
            hg = h_ref[0, :, cols].astype(F32)
            hu = h_ref[1, :, cols].astype(F32)
            wg, wu = wg_ref[:, cols], wu_ref[:, cols]
            cg = c_ref[0, :, cols].astype(F32)
            cu = c_ref[1, :, cols].astype(F32)
            sg = jax.nn.sigmoid(cg)
            dcu = da * (cg * sg)
            dcg = da * cu * (sg * (1.0 + cg * (1.0 - sg)))
            dh_ref[0, :, cols] = _conv3_bwd(dcg, hg, wg, dwg_ref, cols).astype(BF16)
            dh_ref[1, :, cols] = _conv3_bwd(dcu, hu, wu, dwu_ref, cols).astype(BF16)

    vec = jax.ShapeDtypeStruct((FFN_PAIRS, 3, cw), F32)
    return pl.pallas_call(
        body, name=name, grid=(FFN_PAIRS,),
        in_specs=[pl.BlockSpec((s, d), lambda j: (0, 0)), pl.BlockSpec((None, cw, d), lambda j: (j, 0, 0)),
                  pl.BlockSpec((None, 2, s, cw), lambda j: (j, 0, 0, 0)),
                  pl.BlockSpec((None, 2, s, cw), lambda j: (j, 0, 0, 0)),
                  pl.BlockSpec((None, 3, cw), lambda j: (j, 0, 0)),
                  pl.BlockSpec((None, 3, cw), lambda j: (j + FFN_PAIRS, 0, 0))],
        out_specs=[pl.BlockSpec((None, 2, s, cw), lambda j: (j, 0, 0, 0)),
                   pl.BlockSpec((None, 3, cw), lambda j: (j, 0, 0)), pl.BlockSpec((None, 3, cw), lambda j: (j, 0, 0))],
        out_shape=[jax.ShapeDtypeStruct((FFN_PAIRS, 2, s, cw), BF16), vec, vec],
        compiler_params=_params(("parallel",)),
    )(do, wdown, h, c, wdw, wdw)


def _ffn_dwup(n, dh, name):
    s, d = n.shape
    cw = dh.shape[-1]

    def body(n_ref, dh_ref, o_ref):
        o_ref[...] = _dot(n_ref[...], dh_ref[...], TN).astype(BF16)

    return pl.pallas_call(
        body, name=name, grid=(N_DEV,),
        in_specs=[pl.BlockSpec((s, d), lambda k: (0, 0)),
                  pl.BlockSpec((None, None, s, cw), lambda k: (k % FFN_PAIRS, k // FFN_PAIRS, 0, 0))],
        out_specs=pl.BlockSpec((None, d, cw), lambda k: (k, 0, 0)),
        out_shape=jax.ShapeDtypeStruct((N_DEV, d, cw), BF16),
        compiler_params=_params(("parallel",)),
    )(n, dh)


def _ffn_dn(dh, wup, name, dep=None):
    s, cw = dh.shape[-2:]
    d = wup.shape[1]
    tm = _pick(s, MATMUL_TILES)
    deps = [] if dep is None else [dep]

    def body(dh_ref, w_ref, *rest):
        o_ref, acc_ref = rest[-2:]
        k = pl.program_id(1)

        @pl.when(k == 0)
        def _():
            acc_ref[...] = jnp.zeros_like(acc_ref)

        acc_ref[...] += _dot(dh_ref[...], w_ref[...], NT)

        @pl.when(k == N_DEV - 1)
        def _():
            o_ref[...] = acc_ref[...]

    return pl.pallas_call(
        body, name=name, grid=(s // tm, N_DEV),
        in_specs=[pl.BlockSpec((None, None, tm, cw), lambda i, k: (k % FFN_PAIRS, k // FFN_PAIRS, i, 0)),
                  pl.BlockSpec((None, d, cw), lambda i, k: (k, 0, 0))] + [_ANY] * len(deps),
        out_specs=pl.BlockSpec((tm, d), lambda i, k: (i, 0)),
        out_shape=jax.ShapeDtypeStruct((s, d), F32),
        scratch_shapes=[pltpu.VMEM((tm, d), F32)],
        compiler_params=_params(("parallel", "arbitrary")),
    )(dh, wup, *deps)


def _sconv_fwd(n, win, wdw, name):
    s, d = n.shape
    tn = COL_TILE
    nj = d // tn

    def body(n_ref, wb_ref, wc_ref, wh_ref, dw_ref, z_ref, y_ref):
        x = n_ref[...]
        zb = _dot(x, wb_ref[...])
        zc = _dot(x, wc_ref[...])
        zh = _dot(x, wh_ref[...])
        z_ref[0] = zb.astype(BF16)
        z_ref[1] = zc.astype(BF16)
        z_ref[2] = zh.astype(BF16)
        y_ref[...] = (zb * _conv3(zc * zh, dw_ref[...])).astype(BF16)

    return pl.pallas_call(
        body, name=name, grid=(nj,),
        in_specs=[pl.BlockSpec((s, d), lambda j: (0, 0)),
                  pl.BlockSpec((d, tn), lambda j: (0, j)), pl.BlockSpec((d, tn), lambda j: (0, j + nj)),
                  pl.BlockSpec((d, tn), lambda j: (0, j + 2 * nj)), pl.BlockSpec((3, tn), lambda j: (0, j))],
        out_specs=[pl.BlockSpec((3, s, tn), lambda j: (0, 0, j)), pl.BlockSpec((s, tn), lambda j: (0, j))],
        out_shape=[jax.ShapeDtypeStruct((3, s, d), BF16), jax.ShapeDtypeStruct((s, d), BF16)],
        compiler_params=_params(("parallel",)),
    )(n, win, win, win, wdw)


def _sconv_mid_bwd(dm, wout, z, wdw, name):
    s, d = dm.shape
    tn = COL_TILE
    nj = d // tn

    def body(dm_ref, wo_ref, z_ref, w_ref, dz_ref, dw_ref):
        dy = _dot(dm_ref[...], wo_ref[...], NT)
        zb = z_ref[0].astype(F32)
        zc = z_ref[1].astype(F32)
        zh = z_ref[2].astype(F32)
        w = w_ref[...]
        p = zc * zh
        cp = _conv3(p, w)
        dz_ref[0] = (dy * cp).astype(BF16)
        dcp = dy * zb
        dp = _conv3_bwd(dcp, p, w, dw_ref)
        dz_ref[1] = (dp * zh).astype(BF16)
        dz_ref[2] = (dp * zc).astype(BF16)

    return pl.pallas_call(
        body, name=name, grid=(nj,),
        in_specs=[pl.BlockSpec((s, d), lambda j: (0, 0)), pl.BlockSpec((tn, d), lambda j: (j, 0)),
                  pl.BlockSpec((3, s, tn), lambda j: (0, 0, j)), pl.BlockSpec((3, tn), lambda j: (0, j))],
        out_specs=[pl.BlockSpec((3, s, tn), lambda j: (0, 0, j)), pl.BlockSpec((3, tn), lambda j: (0, j))],
        out_shape=[jax.ShapeDtypeStruct((3, s, d), BF16), jax.ShapeDtypeStruct((3, d), F32)],
        compiler_params=_params(("parallel",)),
    )(dm, wout, z, wdw)


def _pool_select(g, c2, c4, c8, c16):
    return jnp.where(g == 0, c2, jnp.where(g == 1, c4, jnp.where(g == 2, c8, c16)))


def _pool_inv_count(g, shape):
    pos = lax.broadcasted_iota(jnp.int32, shape, 0).astype(F32) + 1.0
    win = (2 << g).astype(F32)
    return jnp.minimum(pos, win)


def _pool_fwd(n, win, wgrp, scale, name):
    s, d = n.shape
    tn = POOL_GROUP_DIM

    def body(n_ref, wi_ref, wg_ref, sc_ref, p_ref, y_ref):
        g = pl.program_id(0)
        u = _dot(n_ref[...], wi_ref[...])
        s2 = u + _shift_down(u, 1)
        s4 = s2 + _shift_down(s2, 2)
        s8 = s4 + _shift_down(s4, 4)
        s16 = s8 + _shift_down(s8, 8)
        tot = _pool_select(g, s2, s4, s8, s16)
        p = (tot / _pool_inv_count(g, u.shape) - u).astype(BF16)
        p_ref[...] = p
        y_ref[...] = (_dot(p, wg_ref[...]) * sc_ref[...]).astype(BF16)

    return pl.pallas_call(
        body, name=name, grid=(d // tn,),
        in_specs=[pl.BlockSpec((s, d), lambda g: (0, 0)), pl.BlockSpec((d, tn), lambda g: (0, g)),
                  pl.BlockSpec((None, tn, tn), lambda g: (g, 0, 0)), pl.BlockSpec((1, tn), lambda g: (0, g))],
        out_specs=[pl.BlockSpec((s, tn), lambda g: (0, g)), pl.BlockSpec((s, tn), lambda g: (0, g))],
        out_shape=[jax.ShapeDtypeStruct((s, d), BF16), jax.ShapeDtypeStruct((s, d), BF16)],
        compiler_params=_params(("parallel",)),
    )(n, win, wgrp, scale)


def _pool_mid_bwd(dm, wout, p, wgrp, scale, name):
    s, d = dm.shape
    tn = POOL_GROUP_DIM

    def body(dm_ref, wo_ref, p_ref, wg_ref, sc_ref, du_ref, dwg_ref, dsc_ref):
        g = pl.program_id(0)
        dy = _dot(dm_ref[...], wo_ref[...], NT)
        pv = p_ref[...]
        wg = wg_ref[...]
        ypre = _dot(pv, wg)
        dsc_ref[...] = jnp.sum(dy * ypre, axis=0, keepdims=True)
        dypre = (dy * sc_ref[...]).astype(BF16)
        dwg_ref[...] = _dot(pv, dypre, TN)
        dp = _dot(dypre, wg, NT)
        e = dp / _pool_inv_count(g, dp.shape)
        f2 = e + _shift_up(e, 1)
        f4 = f2 + _shift_up(f2, 2)
        f8 = f4 + _shift_up(f4, 4)
        f16 = f8 + _shift_up(f8, 8)
        du_ref[...] = (_pool_select(g, f2, f4, f8, f16) - dp).astype(BF16)

    return pl.pallas_call(
        body, name=name, grid=(d // tn,),
        in_specs=[pl.BlockSpec((s, d), lambda g: (0, 0)), pl.BlockSpec((tn, d), lambda g: (g, 0)),
                  pl.BlockSpec((s, tn), lambda g: (0, g)), pl.BlockSpec((None, tn, tn), lambda g: (g, 0, 0)),
                  pl.BlockSpec((1, tn), lambda g: (0, g))],
        out_specs=[pl.BlockSpec((s, tn), lambda g: (0, g)), pl.BlockSpec((None, tn, tn), lambda g: (g, 0, 0)),
                   pl.BlockSpec((1, tn), lambda g: (0, g))],
        out_shape=[jax.ShapeDtypeStruct((s, d), BF16), jax.ShapeDtypeStruct((4, tn, tn), F32),
                   jax.ShapeDtypeStruct((1, d), F32)],
        compiler_params=_params(("parallel",)),
    )(dm, wout, p, wgrp, scale)


PANEL = LANES
ATTN_EXT = ATTN_WIDTH + PANEL
DVEC_LANE = HEADS


def _alibi_slopes(g, dil):
    all_slopes = 2.0 ** (-8.0 * np.arange(1, N_HEADS_A + 1) / N_HEADS_A)
    return [float(np.float32(sl) * np.float32(dil)) for sl in all_slopes[g * HEADS:(g + 1) * HEADS]]


def _residue_order(a, dil, name):
    s, w = a.shape
    per = ROW_TILE // dil
    panels = w // PANEL

    def body(a_ref, o_ref, *tiles):
        for c in range(panels):
            cols = slice(c * PANEL, (c + 1) * PANEL)
            tiles[c][...] = a_ref[:, cols].astype(F32)
            for r in range(dil):
                o_ref[r, :, cols] = tiles[c][pl.ds(r, per, stride=dil), :].astype(o_ref.dtype)

    out = pl.pallas_call(
        body, name=name, grid=(s // ROW_TILE,),
        in_specs=[pl.BlockSpec((ROW_TILE, w), lambda i: (i, 0))],
        out_specs=pl.BlockSpec((dil, per, w), lambda i: (0, i, 0)),
        out_shape=jax.ShapeDtypeStruct((dil, s // dil, w), a.dtype),
        scratch_shapes=[pltpu.VMEM((ROW_TILE, PANEL), F32)] * panels,
        compiler_params=_params(("parallel",)),
    )(a)
    return out.reshape(s, w)


def _token_order(a, dil, acc, name):
    s, w = a.shape
    per = ROW_TILE // dil
    panels = w // PANEL
    has_acc = acc is not None

    def body(*refs):
        a_ref = refs[0]
        o_ref = refs[2] if has_acc else refs[1]
        tiles = refs[3:] if has_acc else refs[2:]
        for c in range(panels):
            cols = slice(c * PANEL, (c + 1) * PANEL)
            for r in range(dil):
                tiles[c][pl.ds(r, per, stride=dil), :] = a_ref[r, :, cols]
            v = tiles[c][...]
            if has_acc:
                v = v + refs[1][:, cols]
            o_ref[:, cols] = v

    row = pl.BlockSpec((ROW_TILE, w), lambda i: (i, 0))
    return pl.pallas_call(
        body, name=name, grid=(s // ROW_TILE,),
        in_specs=[pl.BlockSpec((dil, per, w), lambda i: (0, i, 0))] + ([row] if has_acc else []),
        out_specs=row, out_shape=jax.ShapeDtypeStruct((s, w), F32),
        scratch_shapes=[pltpu.VMEM((ROW_TILE, PANEL), F32)] * panels,
        compiler_params=_params(("parallel",)),
    )(*([a.reshape(dil, s // dil, w)] + ([acc] if has_acc else [])))


def _qkv_proj(n, wqkv, g, name):
    s, d = n.shape
    tm = _pick(s, MATMUL_TILES)

    def body(a_ref, b_ref, o_ref):
        o_ref[...] = _dot(a_ref[...], b_ref[...]).astype(BF16)

    return pl.pallas_call(
        body, name=name, grid=(s // tm, 3),
        in_specs=[pl.BlockSpec((tm, d), lambda i, t: (i, 0)),
                  pl.BlockSpec((d, ATTN_WIDTH), lambda i, t: (0, 3 * g + t))],
        out_specs=pl.BlockSpec((None, tm, ATTN_WIDTH), lambda i, t: (t, i, 0)),
        out_shape=jax.ShapeDtypeStruct((3, s, ATTN_WIDTH), BF16),
        compiler_params=_params(("parallel", "parallel")),
    )(n, wqkv)


def _attn_window(n, ln):
    if ln == BLOCK:
        return 0, BLOCK
    return pl.multiple_of(jnp.maximum(n - 1, 0) * BLOCK, BLOCK), 2 * BLOCK


def _attn_mask(n, k0, kw):
    qpos = n * BLOCK + lax.broadcasted_iota(jnp.int32, (BLOCK, kw), 0)
    kpos = k0 + lax.broadcasted_iota(jnp.int32, (BLOCK, kw), 1)
    dist = qpos - kpos
    return dist.astype(F32), (dist >= 0) & (dist <= BLOCK)


def _attn_scores(q, keys, slope, dist, valid):
    s = _dot(q, keys, NT) * (HEAD_DIM ** -0.5) - slope * dist
    return jnp.where(valid, s, NEG_INF)


ATTN_STEP_BLOCKS = 1
ATTN_BWD_STEP_BLOCKS = 4


def _attn_block(gb, ln):
    nb = ln // BLOCK
    n, base = (0, gb * ln) if nb == 1 else (gb % nb, (gb // nb) * ln)
    k0, kw = _attn_window(n, ln)
    cur = pl.ds(pl.multiple_of(gb * BLOCK, BLOCK), BLOCK)
    win = pl.ds(pl.multiple_of(base + k0, BLOCK), kw)
    return cur, win, n, k0, kw


def _attn_fwd(qkv, g, name):
    _, s, w = qkv.shape
    dil = DILATED_CFG[g][1]
    ln = s // dil
    slopes = _alibi_slopes(g, dil)
    rows = ATTN_STEP_BLOCKS * BLOCK

    def body(qkv_ref, o_ref):
        o_ref[:, w:] = jnp.zeros((rows, PANEL), F32)
        for b in range(ATTN_STEP_BLOCKS):
            cur, win, n, k0, kw = _attn_block(pl.program_id(0) * ATTN_STEP_BLOCKS + b, ln)
            dist, valid = _attn_mask(n, k0, kw)
            out = slice(b * BLOCK, (b + 1) * BLOCK)
            for h in range(HEADS):
                cols = slice(h * HEAD_DIM, (h + 1) * HEAD_DIM)
                sc = _attn_scores(qkv_ref[0, cur, cols], qkv_ref[1, win, cols], slopes[h], dist, valid)
                m = jnp.max(sc, axis=-1, keepdims=True)
                p = jnp.exp(sc - m)
                den = jnp.sum(p, axis=-1, keepdims=True)
                o_ref[out, cols] = _dot(p.astype(BF16), qkv_ref[2, win, cols]) / den
                o_ref[out, w + h:w + h + 1] = m + jnp.log(den)

    return pl.pallas_call(
        body, name=name, grid=(s // rows,),
        in_specs=[pl.BlockSpec((3, s, w), lambda i: (0, 0, 0))],
        out_specs=pl.BlockSpec((rows, ATTN_EXT), lambda i: (i, 0)),
        out_shape=jax.ShapeDtypeStruct((s, ATTN_EXT), F32),
        compiler_params=_params(("parallel",)),
    )(qkv)


def _attn_bwd(qkv, dext, g, name, dep=None):
    _, s, w = qkv.shape
    dil = DILATED_CFG[g][1]
    ln = s // dil
    slopes = _alibi_slopes(g, dil)
    scale = HEAD_DIM ** -0.5
    rows = ATTN_BWD_STEP_BLOCKS * BLOCK
    steps = s // rows
    deps = [] if dep is None else [dep]

    def body(qkv_ref, de_ref, *rest):
        d_ref, dk_ref, dv_ref = rest[-3:]

        @pl.when(pl.program_id(0) == 0)
        def _():
            dk_ref[...] = jnp.zeros_like(dk_ref)
            dv_ref[...] = jnp.zeros_like(dv_ref)

        for b in range(ATTN_BWD_STEP_BLOCKS):
            cur, win, n, k0, kw = _attn_block(pl.program_id(0) * ATTN_BWD_STEP_BLOCKS + b, ln)
            dist, valid = _attn_mask(n, k0, kw)
            blk = slice(b * BLOCK, (b + 1) * BLOCK)
            for h in range(HEADS):
                cols = slice(h * HEAD_DIM, (h + 1) * HEAD_DIM)
                q, keys = qkv_ref[0, cur, cols], qkv_ref[1, win, cols]
                dob = de_ref[blk, cols].astype(BF16)
                p = jnp.exp(_attn_scores(q, keys, slopes[h], dist, valid) - de_ref[blk, w + h:w + h + 1])
                dd = de_ref[blk, w + DVEC_LANE + h:w + DVEC_LANE + h + 1]
                ds = (p * (_dot(dob, qkv_ref[2, win, cols], NT) - dd)).astype(BF16)
                d_ref[0, cur, cols] = (scale * _dot(ds, keys)).astype(BF16)
                dv_ref[win, cols] += _dot(p.astype(BF16), dob, TN)
                dk_ref[win, cols] += scale * _dot(ds, q, TN)

        @pl.when(pl.program_id(0) == steps - 1)
        def _():
            d_ref[1] = dk_ref[...].astype(BF16)
            d_ref[2] = dv_ref[...].astype(BF16)

    whole = pl.BlockSpec((3, s, w), lambda i: (0, 0, 0))
    return pl.pallas_call(
        body, name=name, grid=(steps,),
        in_specs=[whole, pl.BlockSpec((rows, ATTN_EXT), lambda i: (i, 0))] + [_ANY] * len(deps),
        out_specs=whole, out_shape=jax.ShapeDtypeStruct((3, s, w), BF16),
        scratch_shapes=[pltpu.VMEM((s, w), F32), pltpu.VMEM((s, w), F32)],
        compiler_params=_params(("arbitrary",)),
    )(qkv, dext, *deps)


def _attn_merge(e0, e1, e2, name):
    s = e0.shape[0]
    w = ATTN_WIDTH

    def body(e0_ref, e1_ref, e2_ref, m_ref, mb_ref, lse_ref):
        refs = (e0_ref, e1_ref, e2_ref)
        l = [r[:, w:w + HEADS] for r in refs]
        mx = jnp.maximum(jnp.maximum(l[0], l[1]), l[2])
        e = [jnp.exp(v - mx) for v in l]
        z = e[0] + e[1] + e[2]
        lse_ref[...] = mx + jnp.log(z)
        wts = [v / z for v in e]
        for h in range(HEADS):
            cols = slice(h * HEAD_DIM, (h + 1) * HEAD_DIM)
            acc = wts[0][:, h:h + 1] * refs[0][:, cols]
            for g in range(1, N_GROUPS_A):
                acc = acc + wts[g][:, h:h + 1] * refs[g][:, cols]
            m_ref[:, cols] = acc
            mb_ref[:, cols] = acc.astype(BF16)

    ext = pl.BlockSpec((ROW_TILE, ATTN_EXT), lambda i: (i, 0))
    row = pl.BlockSpec((ROW_TILE, w), lambda i: (i, 0))
    return pl.pallas_call(
        body, name=name, grid=(s // ROW_TILE,),
        in_specs=[ext, ext, ext],
        out_specs=[row, row, pl.BlockSpec((ROW_TILE, HEADS), lambda i: (i, 0))],
        out_shape=[jax.ShapeDtypeStruct((s, w), F32), jax.ShapeDtypeStruct((s, w), BF16),
                   jax.ShapeDtypeStruct((s, HEADS), F32)],
        compiler_params=_params(("parallel",)),
    )(e0, e1, e2)


def _attn_dvec(dmerged, merged, lse_all, name, dep=None):
    s, w = merged.shape
    deps = [] if dep is None else [dep]

    def body(dm_ref, m_ref, lse_ref, *rest):
        de_ref = rest[-1]
        dmv = dm_ref[...]
        de_ref[:, :w] = dmv
        de_ref[:, w:] = jnp.zeros((ROW_TILE, PANEL), F32)
        de_ref[:, w:w + HEADS] = lse_ref[...]
        prod = dmv * m_ref[...]
        for h in range(HEADS):
            lane = w + DVEC_LANE + h
            de_ref[:, lane:lane + 1] = jnp.sum(prod[:, h * HEAD_DIM:(h + 1) * HEAD_DIM], axis=-1, keepdims=True)

    row = pl.BlockSpec((ROW_TILE, w), lambda i: (i, 0))
    return pl.pallas_call(
        body, name=name, grid=(s // ROW_TILE,),
        in_specs=[row, row, pl.BlockSpec((ROW_TILE, HEADS), lambda i: (i, 0))] + [_ANY] * len(deps),
        out_specs=pl.BlockSpec((ROW_TILE, ATTN_EXT), lambda i: (i, 0)),
        out_shape=jax.ShapeDtypeStruct((s, ATTN_EXT), F32),
        compiler_params=_params(("parallel",)),
    )(dmerged, merged, lse_all, *deps)


def _attention_fwd(n, wqkv, wo, tag):
    ns, qkvs, exts = [], [], []
    for g, (_, dil) in enumerate(DILATED_CFG):
        ng = n if dil == 1 else _residue_order(n, dil, f"{tag}_order_g{g}")
        qkv = _qkv_proj(ng, wqkv, g, f"{tag}_qkv_g{g}")
        ext = _attn_fwd(qkv, g, f"{tag}_fwd_g{g}")
        ns.append(ng)
        qkvs.append(qkv)
        exts.append(ext if dil == 1 else _token_order(ext, dil, None, f"{tag}_unorder_g{g}"))
    merged, merged_bf, lse_all = _attn_merge(*exts, f"{tag}_merge")
    m = _matmul(merged_bf, wo, "nn", F32, f"{tag}_wo")
    return m, (ns, qkvs, merged, merged_bf, lse_all)


def _attention_bwd(dm, wqkv, wo, saved, tag, dep=None, hook=None):
    ns, qkvs, merged, merged_bf, lse_all = saved
    d_wo = _matmul(merged_bf, dm, "tn", BF16, f"{tag}_dwo")
    dmerged = _matmul(dm, wo, "nt", F32, f"{tag}_dmerged")
    dext = _attn_dvec(dmerged, merged, lse_all, f"{tag}_dvec", dep)
    width = 3 * ATTN_WIDTH
    d_wqkv, dn, dep = [], None, None
    for g, (_, dil) in enumerate(DILATED_CFG):
        dext_g = dext if dil == 1 else _residue_order(dext, dil, f"{tag}_dorder_g{g}")
        dqkv = _attn_bwd(qkvs[g], dext_g, g, f"{tag}_bwd_g{g}", dep)
        dep = hook(g, dqkv) if hook is not None and g + 1 < N_GROUPS_A else None
        d_wqkv.append(_matmul(ns[g], dqkv, "tn", BF16, f"{tag}_dwqkv_g{g}", b_parts=3))
        dn_g = _matmul(dqkv, wqkv[:, g * width:(g + 1) * width], "nt", F32, f"{tag}_dn_g{g}", a_parts=3)
        dn = dn_g if dil == 1 else _token_order(dn_g, dil, dn, f"{tag}_dn_sum_g{g}")
    return dn, jnp.concatenate(d_wqkv, axis=1), d_wo


def _layer_matrices(i):
    mixer = (("attn_w_qkv", "attn_w_o"), ("conv_w_in", "conv_w_out"), ("pool_w_in", "pool_w_grp", "pool_w_out"))[i % 3]
    return [(k, i // 3) for k in mixer] + [("ffn_w_up", i), ("ffn_w_down", i)]


def _local_step(x, tgt, vec, weights, sink):
    ng = vec["norm_g"]

    def gain(i, j, token=None):
        g = ng[i, j][None, :]
        return g if token is None else g + token

    saved = []
    n = _rms_fwd(x, gain(0, 0), None, BF16, "norm_first")
    for i in range(DEPTH):
        wl = weights.layer(i)
        t0 = weights.hook(i, 0, n)
        kind, idx = i % 3, i // 3
        if kind == 0:
            m, ms = _attention_fwd(n, wl["attn_w_qkv"], wl["attn_w_o"], "attn")
        elif kind == 1:
            taps = vec["conv_w_dw"][idx] if t0 is None else vec["conv_w_dw"][idx] + t0
            z, y = _sconv_fwd(n, wl["conv_w_in"], taps, "sconv_fwd")
            m = _matmul(y, wl["conv_w_out"], "nn", F32, "sconv_out")
            ms = (z, y)
        else:
            scale = vec["pool_scale"][idx][None, :] if t0 is None else vec["pool_scale"][idx][None, :] + t0
            p, y = _pool_fwd(n, wl["pool_w_in"], wl["pool_w_grp"], scale, "pool_fwd")
            m = _matmul(y, wl["pool_w_out"], "nn", F32, "pool_out")
            ms = (p, y)
        t1 = weights.hook(i, 1, m)
        x1, n2 = _rms_res_pre(m, gain(i, 1, t0), x, gain(i, 2, t1), "norm_res_pre")
        h, c, a = _ffn_up(n2, wl["ffn_w_up"], vec["ffn_w_dw"][i], "ffn_up")
        t2 = weights.hook(i, 2, a)
        f = _matmul(a, wl["ffn_w_down"].reshape(D_FF, D_MODEL), "nn", F32, "ffn_down", a_parts=FFN_PAIRS)
        saved.append((x, n, m, ms, x1, n2, h, a, f, wl, c))
        if i + 1 < DEPTH:
            x, n = _rms_res_pre(f, gain(i, 3, t2), x1, gain(i + 1, 0), "norm_res_pre")
        else:
            x = _rms_fwd(f, gain(i, 3), x1, F32, "norm_res")
        weights.hook(i, 3, x)

    loss, dx = _loss_head(x, tgt, "loss_head")

    g_norm = [[None] * 4 for _ in range(DEPTH)]
    g_taps, g_scale, g_ffn_dw = [], [], [None] * DEPTH
    df, g_norm[DEPTH - 1][3] = _rms_bwd(saved[-1][8], gain(DEPTH - 1, 3), dx, None, BF16, "norm_bwd_sub")
    t0 = None
    for i in reversed(range(DEPTH)):
        xin, n, m, ms, x1, n2, h, a, f, wl, c = saved[i]
        kind, idx = i % 3, i // 3
        gl = {}
        d_wdown = _matmul(a, df, "tn", BF16, "ffn_dwdown", a_parts=FFN_PAIRS)
        gl["ffn_w_down"] = d_wdown.reshape(N_DEV, D_FF // N_DEV, D_MODEL)
        ffn_taps = vec["ffn_w_dw"][i] if t0 is None else vec["ffn_w_dw"][i] + t0
        dh, dwg, dwu = _ffn_mid_bwd(df, wl["ffn_w_down"].reshape(FFN_PAIRS, -1, D_MODEL), h, c, ffn_taps, "ffn_mid_bwd")
        g_ffn_dw[i] = jnp.concatenate([dwg, dwu], axis=0)
        t1 = sink.hook(i, 1, dh)
        gl["ffn_w_up"] = _ffn_dwup(n2, dh, "ffn_dwup")
        tf = sink.ffn_done(i, gl)
        dn2 = _ffn_dn(dh, wl["ffn_w_up"], "ffn_dn", t1)
        dx1, dm, g_norm[i][2], g_norm[i][1] = _rms_bwd_pair(x1, gain(i, 2, tf), dn2, dx, m, gain(i, 1), "norm_bwd_pair")
        t2 = sink.hook(i, 2, dm)
        if kind == 0:
            dn, gl["attn_w_qkv"], gl["attn_w_o"] = _attention_bwd(
                dm, wl["attn_w_qkv"], wl["attn_w_o"], ms, "attn", t2, lambda g, after, i=i: sink.hook(i, ("a", "b")[g], after))
        elif kind == 1:
            z, y = ms
            gl["conv_w_out"] = _matmul(y, dm, "tn", BF16, "sconv_dwout")
            taps = vec["conv_w_dw"][idx] if t2 is None else vec["conv_w_dw"][idx] + t2
            dz, ddw = _sconv_mid_bwd(dm, wl["conv_w_out"], z, taps, "sconv_mid_bwd")
            g_taps.append(ddw)
            gl["conv_w_in"] = _matmul(n, dz, "tn", BF16, "sconv_dwin", b_parts=3)
            dn = _matmul(dz, wl["conv_w_in"], "nt", F32, "sconv_dn", a_parts=3)
        else:
            p, y = ms
            gl["pool_w_out"] = _matmul(y, dm, "tn", BF16, "pool_dwout")
            scale = vec["pool_scale"][idx][None, :] if t2 is None else vec["pool_scale"][idx][None, :] + t2
            du, gl["pool_w_grp"], dscale = _pool_mid_bwd(dm, wl["pool_w_out"], p, wl["pool_w_grp"], scale, "pool_mid_bwd")
            g_scale.append(dscale[0])
            gl["pool_w_in"] = _matmul(n, du, "tn", BF16, "pool_dwin")
            dn = _matmul(du, wl["pool_w_in"], "nt", F32, "pool_dn")
        sink.hook(i, 3, dn)
        if i > 0:
            dx, df, g_norm[i][0], g_norm[i - 1][3] = _rms_bwd_pair(xin, gain(i, 0, t2), dn, dx1, saved[i - 1][8],
                                                                   gain(i - 1, 3), "norm_bwd_pair")
        else:
            dx, g_norm[0][0] = _rms_bwd(xin, gain(0, 0), dn, dx1, F32, "norm_bwd_res")
        t0 = sink.layer_done(i, gl)

    vec_grads = {"norm_g": jnp.stack([jnp.concatenate(row, axis=0) for row in g_norm]), "conv_w_dw": jnp.stack(g_taps),
                 "pool_scale": jnp.stack(g_scale), "ffn_w_dw": g_ffn_dw}
    return loss, dx, vec_grads


_AXES = ("x", "y", "c")
ROUTE_A = ("y", "x", "c")
ROUTE_B = ("x", "y", "c")
def _dev_index(pos):
    return 4 * pos["x"] + 2 * pos["y"] + pos["c"]


_HBM = pl.BlockSpec(memory_space=pltpu.HBM)
_SEM = pl.BlockSpec(memory_space=pltpu.SEMAPHORE)
_ANY = pl.BlockSpec(memory_space=pl.ANY)
_EFFECT = pltpu.SideEffectType.DATAFLOW_SIDE_EFFECTING


TOKEN_SHAPE = (1, D_MODEL)


def _copies_start(describe, arrays, n_copies, name, after, token_shape=TOKEN_SHAPE):
    n = len(arrays)
    deps = [] if after is None else [after]

    def body(*refs):
        send_sems, recv_sems = refs[n + len(deps)], refs[n + len(deps) + 1]
        for c in describe(refs[:n], send_sems, recv_sems):
            c.start()
        refs[-1][...] = jnp.zeros_like(refs[-1])

    outs = pl.pallas_call(
        body, name=f"{name}_start",
        out_shape=(pltpu.SemaphoreType.DMA((n_copies,)), pltpu.SemaphoreType.DMA((n_copies,)),
                   *[pltpu.HBM(a.shape, a.dtype) for a in arrays], jax.ShapeDtypeStruct(token_shape, F32)),
        in_specs=[_HBM] * n + [_ANY] * len(deps),
        out_specs=(_SEM, _SEM, *([_HBM] * n), pl.BlockSpec(memory_space=pltpu.VMEM)),
        input_output_aliases={i: 2 + i for i in range(n)},
        compiler_params=pltpu.CompilerParams(has_side_effects=_EFFECT),
    )(*[pltpu.with_memory_space_constraint(a, pltpu.HBM) for a in arrays], *deps)
    return (outs[0], outs[1], list(outs[2:2 + n])), outs[-1]


def _copies_wait(describe, handle, name, after):
    send_sems, recv_sems, arrays = handle
    n = len(arrays)
    deps = [] if after is None else list(after) if isinstance(after, (list, tuple)) else [after]

    def body(*refs):
        for c in describe(refs[:n], refs[n], refs[n + 1]):
            c.wait_send()
            c.wait_recv()

    outs = pl.pallas_call(
        body, name=f"{name}_wait",
        out_shape=tuple(pltpu.HBM(a.shape, a.dtype) for a in arrays),
        in_specs=[_HBM] * n + [_SEM, _SEM] + [_ANY] * len(deps), out_specs=tuple([_HBM] * n),
        input_output_aliases={i: i for i in range(n)},
        compiler_params=pltpu.CompilerParams(has_side_effects=_EFFECT),
    )(*arrays, send_sems, recv_sems, *deps)
    return list(outs)


GATHER_STAGE_COPIES = (3, 3, 1)


def _gather_copies(stage, routes):
    n = len(routes)

    def describe(refs, send_sems, recv_sems):
        pos = {a: lax.axis_index(a) for a in _AXES}

        def flipped(axes):
            return {a: 1 - pos[a] if a in axes else pos[a] for a in _AXES}

        copies = []
        for i, (a1, a2, a3) in enumerate(routes):
            land = refs[n + i] if stage == 1 else refs[i]
            p1, p2, p12, p3 = flipped((a1,)), flipped((a2,)), flipped((a1, a2)), flipped((a3,))
            plan = {1: [(None, p1), (None, p2), (None, p3)], 2: [(p1, p2), (p1, p3), (p2, p3)], 3: [(p12, p3)]}[stage]
            for holder, to in plan:
                slot = land.at[_dev_index(pos if holder is None else holder)]
                k = len(copies)
                copies.append(pltpu.make_async_remote_copy(
                    src_ref=refs[i] if holder is None else slot, dst_ref=slot,
                    send_sem=send_sems.at[k], recv_sem=recv_sems.at[k],
                    device_id=tuple(to[a] for a in _AXES), device_id_type=pl.DeviceIdType.MESH))
        return copies

    return describe


def _gather_begin(shards, routes, name, after):
    n = len(shards)
    lands = [lax.empty((N_DEV,) + a.shape, a.dtype) for a in shards]
    handle, token = _copies_start(_gather_copies(1, routes), list(shards) + lands, GATHER_STAGE_COPIES[0] * n,
                                  f"{name}_1", after)
    return {"stage": 1, "handle": handle, "routes": routes, "name": name, "n": n}, token


def _gather_next(state, after):
    stage, routes, name, n = state["stage"], state["routes"], state["name"], state["n"]
    arrays = _copies_wait(_gather_copies(stage, routes), state["handle"], f"{name}_{stage}", after)
    if stage == 1:
        state = dict(state, shards=arrays[:n])
        arrays = arrays[n:]
    if stage == 3:
        me = _dev_index({a: lax.axis_index(a) for a in _AXES})
        return [lax.dynamic_update_index_in_dim(o, s, me, 0) for o, s in zip(arrays, state["shards"])], None
    handle, token = _copies_start(_gather_copies(stage + 1, routes), arrays, GATHER_STAGE_COPIES[stage] * n,
                                  f"{name}_{stage + 1}", None)
    return dict(state, stage=stage + 1, handle=handle), token


ADD_ROW_TILES = (1024, 704, 512, 352, 256, 128, 96, 64, 32, 16)


def _add_half(a, recv, me, out_dtype, name):
    p, q, cols = recv.shape
    tr = _pick(q, ADD_ROW_TILES)

    def body(me_ref, a_ref, b_ref, o_ref):
        o_ref[...] = (a_ref[...].astype(F32) + b_ref[...].astype(F32)).astype(o_ref.dtype)

    return pl.pallas_call(
        body, name=name,
        grid_spec=pltpu.PrefetchScalarGridSpec(
            num_scalar_prefetch=1, grid=(p, q // tr),
            in_specs=[pl.BlockSpec((None, None, tr, cols), lambda j, i, m: (j, m[0], i, 0)),
                      pl.BlockSpec((None, tr, cols), lambda j, i, m: (j, i, 0))],
            out_specs=pl.BlockSpec((None, tr, cols), lambda j, i, m: (j, i, 0))),
        out_shape=jax.ShapeDtypeStruct((p, q, cols), out_dtype),
        compiler_params=_params(("parallel", "parallel")),
    )(me, a, recv)


def _half_copies(axes):
    n = len(axes)

    def describe(refs, send_sems, recv_sems):
        pos = {a: lax.axis_index(a) for a in _AXES}
        copies = []
        for i, axis in enumerate(axes):
            peer = tuple(1 - pos[a] if a == axis else pos[a] for a in _AXES)
            copies.append(pltpu.make_async_remote_copy(
                src_ref=refs[i].at[:, 1 - pos[axis]], dst_ref=refs[n + i], send_sem=send_sems.at[i],
                recv_sem=recv_sems.at[i], device_id=peer, device_id_type=pl.DeviceIdType.MESH))
        return copies

    return describe


def _scatter_begin(slots, routes, tags, name, token_shape=TOKEN_SHAPE):
    shapes = [a.shape[1:] for a in slots]
    rows = [math.prod(s[:-1]) for s in shapes]
    arrays = [a.reshape(4, 2, n, s[-1]) for a, n, s in zip(slots, rows, shapes)]
    return _scatter_start({"stage": 0, "arrays": arrays, "routes": routes, "tags": tags, "name": name,
                           "shapes": shapes, "rows": rows}, token_shape)


def _scatter_start(state, token_shape=TOKEN_SHAPE):
    stage, arrays = state["stage"], state["arrays"]
    axes = [r[2 - stage] for r in state["routes"]]
    lands = [lax.empty((a.shape[0],) + a.shape[2:], a.dtype) for a in arrays]
    handle, token = _copies_start(_half_copies(axes), arrays + lands, len(arrays), f"{state['name']}_{stage + 1}", None,
                                  token_shape)
    return dict(state, handle=handle, axes=axes), token


def _scatter_next(state, after):
    stage, axes, n = state["stage"], state["axes"], len(state["arrays"])
    both = _copies_wait(_half_copies(axes), state["handle"], f"{state['name']}_{stage + 1}", after)
    coord = {a: lax.axis_index(a).astype(jnp.int32).reshape(1) for a in _AXES}
    sums = [_add_half(a, r, coord[ax], F32 if stage == 2 else BF16, f"scatter_add_{stage + 1}_{t}")
            for a, r, ax, t in zip(both[:n], both[n:], axes, state["tags"])]
    if stage == 2:
        return [a.reshape(s) for a, s in zip(sums, state["shapes"])], None
    if stage == 0:
        views = [(1, 2, 2 * r, s[-1]) if route[1] == "x" else (2, 2, r, s[-1])
                 for r, s, route in zip(state["rows"], state["shapes"], state["routes"])]
    else:
        views = [(1, 2, r, s[-1]) for r, s in zip(state["rows"], state["shapes"])]
    return _scatter_start(dict(state, stage=stage + 1, arrays=[a.reshape(v) for a, v in zip(sums, views)]))


_WEIGHTS = {
    "norm_g": ((DEPTH, 4, D_MODEL), 2, True),
    "attn_w_qkv": ((2, D_MODEL, 4608), 2, False),
    "attn_w_o": ((2, ATTN_WIDTH, D_MODEL), 2, False),
    "conv_w_in": ((1, D_MODEL, 3 * D_MODEL), 2, False),
    "conv_w_dw": ((1, 3, D_MODEL), 2, True),
    "conv_w_out": ((1, D_MODEL, D_MODEL), 1, False),
    "pool_w_in": ((1, D_MODEL, D_MODEL), 1, False),
    "pool_w_grp": ((1, 4, POOL_GROUP_DIM, POOL_GROUP_DIM), 2, False),
    "pool_scale": ((1, D_MODEL), 1, True),
    "pool_w_out": ((1, D_MODEL, D_MODEL), 1, False),
    "ffn_w_up": ((DEPTH, D_MODEL, 2 * D_FF), 2, False),
    "ffn_w_dw": ((DEPTH, 3, 2 * D_FF), 2, True),
    "ffn_w_down": ((DEPTH, D_FF, D_MODEL), 1, False),
}
_NAMES = tuple(_WEIGHTS)
_VECTORS = tuple(k for k in _NAMES if _WEIGHTS[k][2])
_MATRICES = tuple(k for k in _NAMES if not _WEIGHTS[k][2])
_FFN = ("ffn_w_up", "ffn_w_down")
_ON_ROUTE_A = ("ffn_w_up", "attn_w_o", "conv_w_out", "pool_w_in")
PACK_ROWS = 16


def _route(name):
    return ROUTE_A if name in _ON_ROUTE_A else ROUTE_B


def _shard_shape(name):
    shape, ax, _ = _WEIGHTS[name]
    return tuple(s // N_DEV if i == ax else s for i, s in enumerate(shape))


def _full_from_slots(slots, name, layers=None):
    shape, ax, _ = _WEIGHTS[name]
    if layers is not None:
        shape = (layers,) + shape[1:]
    return jnp.moveaxis(slots, 0, ax).reshape(shape)


def _slots_from_full(full, name):
    shape, ax, _ = _WEIGHTS[name]
    split = shape[:ax] + (N_DEV, shape[ax] // N_DEV) + shape[ax + 1:]
    return jnp.moveaxis(full.reshape(split), ax, 0)


def _pack_vectors(parts, lead):
    rows = []
    for k in _VECTORS:
        r = parts[k].reshape(lead + (-1, LANES))
        pad = -r.shape[-2] % PACK_ROWS
        rows.append(jnp.pad(r, [(0, 0)] * len(lead) + [(0, pad), (0, 0)]))
    return jnp.concatenate(rows, axis=len(lead))


def _unpack_vectors(buf, lead):
    out, r0 = {}, 0
    for k in _VECTORS:
        shard = _shard_shape(k)
        rows = math.prod(shard) // LANES
        out[k] = buf[..., r0:r0 + rows, :].reshape(lead + shard)
        r0 += rows + (-rows % PACK_ROWS)
    return out


class _LayerWeights:
    def __init__(self, shards):
        first, ffn0 = _layer_matrices(0)[:-2], _layer_matrices(0)[-2:]
        state, token = _gather_begin([shards[k][j].astype(BF16) for k, j in first] + [_pack_vectors(shards, ())],
                                     [_route(k) for k, _ in first] + [ROUTE_B], "gather0", None)
        self.cast = {k: (shards[k] + token[0, 0]).astype(BF16) for k in _MATRICES}
        state, _ = _gather_next(state, self._send(ffn0) + self._send(_layer_matrices(1)))
        state, _ = _gather_next(state, None)
        outs, _ = _gather_next(state, None)
        vec = _unpack_vectors(outs[-1], (N_DEV,))
        self.vec = {k: _full_from_slots(vec[k], k) for k in _VECTORS}
        self.vec["ffn_w_dw"] = [vec["ffn_w_dw"][:, l] for l in range(DEPTH)]
        self.ready = {0: self._unpack(first, outs[:-1])}
        self.chains = {}
        tokens = []
        self._begin("ffn0", ffn0, "gather0f", outs[0], tokens)
        self._begin(1, _layer_matrices(1), "gather1", outs[0], tokens)
        self.vec["norm_g"] = self.vec["norm_g"] + (tokens[0] + tokens[1])

    def _send(self, items):
        return [self.cast[k][j] for k, j in items]

    @staticmethod
    def _unpack(items, outs):
        return {k: o if k in _FFN else _full_from_slots(o[:, None], k, layers=1)[0] for (k, _), o in zip(items, outs)}

    def _begin(self, key, items, name, after, tokens):
        state, token = _gather_begin(self._send(items), [_route(k) for k, _ in items], name, after)
        self.chains[key] = (items, state)
        tokens.append(token)

    def _advance(self, key, after, tokens):
        items, state = self.chains.pop(key)
        state, token = _gather_next(state, after)
        if token is None:
            self.ready.setdefault(0 if key == "ffn0" else key, {}).update(self._unpack(items, state))
        else:
            self.chains[key] = (items, state)
            tokens.append(token)

    def layer(self, i):
        return self.ready[i]

    def hook(self, i, point, after):
        tokens = []
        if i == 0 and point == 0:
            self._advance("ffn0", after, tokens)
        if i == 0 and point == 1:
            self._advance("ffn0", after, tokens)
            self._advance("ffn0", None, tokens)
        if point >= 1 and i + 1 in self.chains:
            self._advance(i + 1, after, tokens)
        if point == 1 and i + 2 < DEPTH:
            self._begin(i + 2, _layer_matrices(i + 2), f"gather{i + 2}", after, tokens)
        return functools.reduce(lambda a, b: a + b, tokens) if tokens else None


def _layer_slots(g, name):
    shape, ax, _ = _WEIGHTS[name]
    shape, ax = shape[1:], ax - 1
    split = shape[:ax] + (N_DEV, shape[ax] // N_DEV) + shape[ax + 1:]
    return jnp.moveaxis(g.reshape(split), ax, 0).astype(BF16)


class _GradSink:
    def __init__(self):
        self.state = None
        self.ffn_state = None
        self.sums = {}
        self.last = None

    def ffn_done(self, i, grads):
        if i != 0:
            return None
        self.ffn_items = _layer_matrices(0)[-2:]
        self.ffn_state, token = _scatter_begin([grads[k] for k, _ in self.ffn_items],
                                               [_route(k) for k, _ in self.ffn_items],
                                               [f"{k}{j}" for k, j in self.ffn_items], "scatter0f")
        return token

    def layer_done(self, i, grads):
        items = _layer_matrices(i)
        if i == 0:
            items = items[:-2]
            self.last = (items, [_layer_slots(grads[k], k) for k, _ in items])
            return None
        slots = [grads[k] if k in _FFN else _layer_slots(grads[k], k) for k, _ in items]
        self.items = items
        self.state, token = _scatter_begin(slots, [_route(k) for k, _ in items], [f"{k}{j}" for k, j in items],
                                           f"scatter{i}", (N_DEV, 3, 2 * D_FF // N_DEV))
        return token

    def hook(self, i, point, after):
        tokens = []
        if self.state is not None and point in (1, 2, 3):
            self.state, token = _scatter_next(self.state, after)
            if point == 3:
                self.sums.update(dict(zip(self.items, self.state)))
                self.state = None
            tokens.append(token)
        if self.ffn_state is not None and point in ("a", "b", 3):
            self.ffn_state, token = _scatter_next(self.ffn_state, after)
            if point == 3:
                self.sums.update(dict(zip(self.ffn_items, self.ffn_state)))
                self.ffn_state = None
            tokens.append(token)
        tokens = [t for t in tokens if t is not None]
        return functools.reduce(lambda a, b: a + b, tokens) if tokens else None


def _adamw(w, g, m, v, name, layer=None, prev=None, dep=None):
    shape = w.shape
    cols = shape[-1]
    view = shape if len(shape) == 3 else (1, math.prod(shape[:-1]), cols)
    layers, rows, _ = view
    tr = _pick(rows, (512, 352, 288, 256, 128, 64, 32, 16, 8))
    n_prev = 0 if prev is None else 3
    lead = ([] if prev is None else [p.reshape(view) for p in prev]) + ([] if dep is None else [dep])

    def body(*refs):
        w_ref, g_ref, m_ref, v_ref = refs[len(lead):len(lead) + 4]
        d_ref, nm_ref, nv_ref = refs[len(lead) + 4:]
        gv = g_ref[...]
        nm = ADAM_B1 * m_ref[...] + (1.0 - ADAM_B1) * gv
        nv = ADAM_B2 * v_ref[...] + (1.0 - ADAM_B2) * jnp.square(gv)
        m_hat = nm / (1.0 - ADAM_B1 ** ADAM_STEP)
        v_hat = nv / (1.0 - ADAM_B2 ** ADAM_STEP)
        d_ref[...] = -ADAM_LR * (m_hat / (jnp.sqrt(v_hat) + ADAM_EPS) + ADAM_WD * w_ref[...])
        nm_ref[...] = nm
        nv_ref[...] = nv

    if layer is None:
        grid = (layers, rows // tr)
        blk = gblk = pl.BlockSpec((None, tr, cols), lambda l, i: (l, i, 0))
        gview = view
    else:
        grid = (rows // tr,)
        blk = pl.BlockSpec((None, tr, cols), lambda i: (layer, i, 0))
        gblk = pl.BlockSpec((tr, cols), lambda i: (i, 0))
        gview = (rows, cols)
    shp = jax.ShapeDtypeStruct(view, F32)
    outs = pl.pallas_call(
        body, name=name, grid=grid, in_specs=[_ANY] * len(lead) + [blk, gblk, blk, blk], out_specs=[blk] * 3,
        out_shape=[shp] * 3, input_output_aliases={i: i for i in range(n_prev)},
        compiler_params=_params(("parallel",) * len(grid)),
    )(*lead, w.reshape(view), g.reshape(gview), m.reshape(view), v.reshape(view))
    return [o.reshape(shape) for o in outs]


def kernel(x, norm_g, attn_w_qkv, attn_w_o, conv_w_in, conv_w_dw, conv_w_out, pool_w_in, pool_w_grp, pool_scale, pool_w_out, ffn_w_up, ffn_w_dw, ffn_w_down, loss_target, m_norm_g, m_attn_w_qkv, m_attn_w_o, m_conv_w_in, m_conv_w_dw, m_conv_w_out, m_pool_w_in, m_pool_w_grp, m_pool_scale, m_pool_w_out, m_ffn_w_up, m_ffn_w_dw, m_ffn_w_down, v_norm_g, v_attn_w_qkv, v_attn_w_o, v_conv_w_in, v_conv_w_dw, v_conv_w_out, v_pool_w_in, v_pool_w_grp, v_pool_scale, v_pool_w_out, v_ffn_w_up, v_ffn_w_dw, v_ffn_w_down):
    shards = dict(zip(_NAMES, (norm_g, attn_w_qkv, attn_w_o, conv_w_in, conv_w_dw, conv_w_out, pool_w_in,
                               pool_w_grp, pool_scale, pool_w_out, ffn_w_up, ffn_w_dw, ffn_w_down)))
    moms = dict(zip(_NAMES, (m_norm_g, m_attn_w_qkv, m_attn_w_o, m_conv_w_in, m_conv_w_dw, m_conv_w_out,
                             m_pool_w_in, m_pool_w_grp, m_pool_scale, m_pool_w_out, m_ffn_w_up, m_ffn_w_dw,
                             m_ffn_w_down)))
    vels = dict(zip(_NAMES, (v_norm_g, v_attn_w_qkv, v_attn_w_o, v_conv_w_in, v_conv_w_dw, v_conv_w_out,
                             v_pool_w_in, v_pool_w_grp, v_pool_scale, v_pool_w_out, v_ffn_w_up, v_ffn_w_dw,
                             v_ffn_w_down)))
    weights = _LayerWeights(shards)
    sink = _GradSink()
    loss, grad_x, vec_grads = _local_step(x[0], loss_target[0], weights.vec, weights, sink)
    loss = lax.psum(loss[0, 0], _AXES)

    items, slots = sink.last
    vec_slots = {k: _slots_from_full(vec_grads[k], k) for k in _VECTORS if k != "ffn_w_dw"}
    vec_slots["ffn_w_dw"] = jnp.stack(vec_grads["ffn_w_dw"], axis=1)
    state, token = _scatter_begin(slots + [_pack_vectors(vec_slots, (N_DEV,)).astype(BF16)],
                                  [_route(k) for k, _ in items] + [ROUTE_B],
                                  [f"{k}{j}" for k, j in items] + ["vectors"], "scatter0")
    results = {}

    flipped = {k for k in _MATRICES if _WEIGHTS[k][0][0] > 1 and _shard_shape(k)[-1] % LANES}
    wmv = {k: [t.transpose(0, 2, 1) if k in flipped else t for t in (shards[k], moms[k], vels[k])] for k in _MATRICES}

    def step(matrices, dep=None):
        outs = []
        for k, j in matrices:
            g = sink.sums[(k, j)]
            w, m, v = wmv[k]
            if _WEIGHTS[k][0][0] == 1:
                results[k] = (g[None], _adamw(w, g[None], m, v, f"adamw_{k}", dep=dep))
            else:
                gs, prev = results.get(k, ({}, None))
                gs[j] = g
                results[k] = (gs, _adamw(w, g.T if k in flipped else g, m, v, f"adamw_{k}{j}", layer=j, prev=prev, dep=dep))
            outs.append(results[k][1][0])
        return outs

    state, token = _scatter_next(state, step(_layer_matrices(3), token))
    state, token = _scatter_next(state, step(_layer_matrices(2) + _layer_matrices(1), token))
    sums, _ = _scatter_next(state, step(_layer_matrices(0)[-2:], token))
    sink.sums.update(dict(zip(items, sums[:-1])))
    step(items)
    vec_sums = _unpack_vectors(sums[-1], ())
    for k in _VECTORS:
        results[k] = (vec_sums[k], _adamw(shards[k], vec_sums[k], moms[k], vels[k], f"adamw_{k}"))
    grads_out = {k: g if not isinstance(g, dict) else jnp.stack([g[j] for j in range(len(g))])
                 for k, (g, _) in results.items()}
    stepped = {k: [o.transpose(0, 2, 1) if k in flipped else o for o in outs] for k, (_, outs) in results.items()}
    return (loss, grad_x[None], *[grads_out[k] for k in _NAMES], *[stepped[k][0] for k in _NAMES],
            *[stepped[k][1] for k in _NAMES], *[stepped[k][2] for k in _NAMES])
```

```python
import functools
import math

import numpy as np
import jax
import jax.numpy as jnp
from jax import lax
from jax.experimental import pallas as pl
from jax.experimental.pallas import tpu as pltpu

F32, BF16 = jnp.float32, jnp.bfloat16

D_MODEL = 1024
SEQ = 2048
DEPTH = 4
DILATED_CFG = ((128, 1), (512, 4), (2048, 16))
N_GROUPS_A = 3
HEADS = 8
HEAD_DIM = 64
ATTN_WIDTH = HEADS * HEAD_DIM
N_HEADS_A = N_GROUPS_A * HEADS
BLOCK = 128
NEG_INF = -1e30
POOL_GROUP_DIM = 256
D_FF = 2816
RMS_EPS = 1e-6
ADAM_LR, ADAM_B1, ADAM_B2, ADAM_EPS, ADAM_WD, ADAM_STEP = 0.001, 0.9, 0.999, 1e-08, 0.01, 10

N_DEV = 8
LANES = 128
V7X_VMEM_BYTES = 64 * 2 ** 20
VMEM_LIMIT_BYTES = V7X_VMEM_BYTES - 8 * 2 ** 20
COL_TILE = 256
ROW_TILE = 1024
MATMUL_TILES = (1024, 1408, 512, 256, 128)
TN_RESIDENT_K = 2048

NN = (((1,), (0,)), ((), ()))
NT = (((1,), (1,)), ((), ()))
TN = (((0,), (0,)), ((), ()))


def _dot(a, b, dims=NN):
    return lax.dot_general(a, b, dims, preferred_element_type=F32)


def _params(sem=None):
    return pltpu.CompilerParams(dimension_semantics=sem, vmem_limit_bytes=VMEM_LIMIT_BYTES)


def _pick(n, prefs):
    for p in prefs:
        if n % p == 0:
            return p
    return n


def _matmul(a, b, mode, out_dtype, name, a_parts=1, b_parts=1):
    if mode == "nn":
        m, k = a.shape[-2], a.shape[-1] * a_parts
        n = b.shape[-1] * b_parts
    elif mode == "nt":
        m, k = a.shape[-2], a.shape[-1] * a_parts
        n = b.shape[-2]
    else:
        k, m = a.shape[-2], a.shape[-1] * a_parts
        n = b.shape[-1] * b_parts
    tm = _pick(m, MATMUL_TILES)
    tn = _pick(n // b_parts if mode != "nt" else n, MATMUL_TILES)
    kk = k // a_parts if mode != "tn" else k
    tk = _pick(kk, MATMUL_TILES)
    if mode == "tn":
        tm = _pick(m // a_parts, MATMUL_TILES)
        if k <= TN_RESIDENT_K:
            tk = k
    gm, gn, gk = m // tm, n // tn, k // tk

    def a_idx(i, j, kq):
        if mode == "tn":
            r, c, per = kq, i, (m // a_parts) // tm
        else:
            r, c, per = i, kq, (k // a_parts) // tk
        return (r, c) if a_parts == 1 else (c // per, r, c % per)

    def b_idx(i, j, kq):
        if mode == "nt":
            return (j, kq)
        per = (n // b_parts) // tn
        return (kq, j) if b_parts == 1 else (j // per, kq, j % per)

    a_blk = (tk, tm) if mode == "tn" else (tm, tk)
    b_blk = (tn, tk) if mode == "nt" else (tk, tn)
    if a_parts > 1:
        a_blk = (None,) + a_blk
    if b_parts > 1:
        b_blk = (None,) + b_blk
    dims = {"nn": NN, "nt": NT, "tn": TN}[mode]

    def body_single(a_ref, b_ref, o_ref):
        o_ref[...] = _dot(a_ref[...], b_ref[...], dims).astype(o_ref.dtype)

    def body(a_ref, b_ref, o_ref, acc_ref):
        kq = pl.program_id(2)

        @pl.when(kq == 0)
        def _():
            acc_ref[...] = jnp.zeros_like(acc_ref)

        acc_ref[...] += _dot(a_ref[...], b_ref[...], dims)

        @pl.when(kq == gk - 1)
        def _():
            o_ref[...] = acc_ref[...].astype(o_ref.dtype)

    return pl.pallas_call(
        body_single if gk == 1 else body, name=name, grid=(gm, gn, gk),
        in_specs=[pl.BlockSpec(a_blk, a_idx), pl.BlockSpec(b_blk, b_idx)],
        out_specs=pl.BlockSpec((tm, tn), lambda i, j, kq: (i, j)),
        out_shape=jax.ShapeDtypeStruct((m, n), out_dtype),
        scratch_shapes=[] if gk == 1 else [pltpu.VMEM((tm, tn), F32)],
        compiler_params=_params(("parallel", "parallel", "arbitrary")),
    )(a, b)


def _rms_fwd(xin, g, res, out_dtype, name):
    s, d = xin.shape
    has_res = res is not None

    def body(*refs):
        x_ref, g_ref = refs[0], refs[1]
        o_ref = refs[-1]
        x = x_ref[...]
        r = lax.rsqrt(jnp.mean(x * x, axis=-1, keepdims=True) + RMS_EPS)
        y = x * r * g_ref[...]
        if has_res:
            y = refs[2][...] + y
        o_ref[...] = y.astype(o_ref.dtype)

    row = pl.BlockSpec((ROW_TILE, d), lambda i: (i, 0))
    vec = pl.BlockSpec((1, d), lambda i: (0, 0))
    ins = [xin, g] + ([res] if has_res else [])
    return pl.pallas_call(
        body, name=name, grid=(s // ROW_TILE,),
        in_specs=[row, vec] + ([row] if has_res else []),
        out_specs=row, out_shape=jax.ShapeDtypeStruct((s, d), out_dtype),
        compiler_params=_params(("parallel",)),
    )(*ins)


def _rms_bwd(xin, g, dy, dres, out_dtype, name):
    s, d = xin.shape
    has_res = dres is not None

    def body(*refs):
        x_ref, g_ref, dy_ref = refs[0], refs[1], refs[2]
        dx_ref, dg_ref = refs[-2], refs[-1]

        @pl.when(pl.program_id(0) == 0)
        def _():
            dg_ref[...] = jnp.zeros_like(dg_ref)

        x = x_ref[...]
        dyv = dy_ref[...].astype(F32)
        r = lax.rsqrt(jnp.mean(x * x, axis=-1, keepdims=True) + RMS_EPS)
        xhat = x * r
        u = dyv * g_ref[...]
        dx = r * (u - xhat * jnp.mean(u * xhat, axis=-1, keepdims=True))
        if has_res:
            dx = refs[3][...] + dx
        dx_ref[...] = dx.astype(dx_ref.dtype)
        dg_ref[...] += jnp.sum(dyv * xhat, axis=0, keepdims=True)

    row = pl.BlockSpec((ROW_TILE, d), lambda i: (i, 0))
    vec = pl.BlockSpec((1, d), lambda i: (0, 0))
    ins = [xin, g, dy] + ([dres] if has_res else [])
    return pl.pallas_call(
        body, name=name, grid=(s // ROW_TILE,),
        in_specs=[row, vec, row] + ([row] if has_res else []),
        out_specs=[row, vec],
        out_shape=[jax.ShapeDtypeStruct((s, d), out_dtype), jax.ShapeDtypeStruct((1, d), F32)],
        compiler_params=_params(("arbitrary",)),
    )(*ins)


def _rms(x):
    r = lax.rsqrt(jnp.mean(x * x, axis=-1, keepdims=True) + RMS_EPS)
    return r, x * r


def _rms_grad(r, xhat, dy, g):
    u = dy * g
    return r * (u - xhat * jnp.mean(u * xhat, axis=-1, keepdims=True))


def _rms_res_pre(sub, g_post, res, g_pre, name):
    s, d = sub.shape

    def body(sub_ref, gp_ref, res_ref, gn_ref, x_ref, n_ref):
        xnew = res_ref[...] + _rms(sub_ref[...])[1] * gp_ref[...]
        x_ref[...] = xnew
        n_ref[...] = (_rms(xnew)[1] * gn_ref[...]).astype(BF16)

    row = pl.BlockSpec((ROW_TILE, d), lambda i: (i, 0))
    vec = pl.BlockSpec((1, d), lambda i: (0, 0))
    return pl.pallas_call(
        body, name=name, grid=(s // ROW_TILE,),
        in_specs=[row, vec, row, vec], out_specs=[row, row],
        out_shape=[jax.ShapeDtypeStruct((s, d), F32), jax.ShapeDtypeStruct((s, d), BF16)],
        compiler_params=_params(("parallel",)),
    )(sub, g_post, res, g_pre)


def _rms_bwd_pair(xmid, g_pre, dn, dres, sub, g_post, name):
    s, d = xmid.shape

    def body(x_ref, gn_ref, dn_ref, dres_ref, sub_ref, gp_ref, dx_ref, dsub_ref, dgn_ref, dgp_ref):
        @pl.when(pl.program_id(0) == 0)
        def _():
            dgn_ref[...] = jnp.zeros_like(dgn_ref)
            dgp_ref[...] = jnp.zeros_like(dgp_ref)

        dnv = dn_ref[...].astype(F32)
        r, xhat = _rms(x_ref[...])
        dx = dres_ref[...] + _rms_grad(r, xhat, dnv, gn_ref[...])
        dx_ref[...] = dx
        dgn_ref[...] += jnp.sum(dnv * xhat, axis=0, keepdims=True)
        rs, shat = _rms(sub_ref[...])
        dsub_ref[...] = _rms_grad(rs, shat, dx, gp_ref[...]).astype(BF16)
        dgp_ref[...] += jnp.sum(dx * shat, axis=0, keepdims=True)

    row = pl.BlockSpec((ROW_TILE, d), lambda i: (i, 0))
    vec = pl.BlockSpec((1, d), lambda i: (0, 0))
    return pl.pallas_call(
        body, name=name, grid=(s // ROW_TILE,),
        in_specs=[row, vec, row, row, row, vec], out_specs=[row, row, vec, vec],
        out_shape=[jax.ShapeDtypeStruct((s, d), F32), jax.ShapeDtypeStruct((s, d), BF16),
                   jax.ShapeDtypeStruct((1, d), F32), jax.ShapeDtypeStruct((1, d), F32)],
        compiler_params=_params(("arbitrary",)),
    )(xmid, g_pre, dn, dres, sub, g_post)


def _loss_head(y, tgt, name):
    s, d = y.shape

    def body(y_ref, t_ref, l_ref, dy_ref):
        @pl.when(pl.program_id(0) == 0)
        def _():
            l_ref[...] = jnp.zeros_like(l_ref)

        e = y_ref[...] - t_ref[...]
        dy_ref[...] = e / d
        per_tok = jnp.mean(e * e, axis=-1, keepdims=True)
        l_ref[...] += 0.5 * jnp.sum(per_tok, axis=0, keepdims=True)

    row = pl.BlockSpec((ROW_TILE, d), lambda i: (i, 0))
    return pl.pallas_call(
        body, name=name, grid=(s // ROW_TILE,),
        in_specs=[row, row],
        out_specs=[pl.BlockSpec((1, 1), lambda i: (0, 0)), row],
        out_shape=[jax.ShapeDtypeStruct((1, 1), F32), jax.ShapeDtypeStruct((s, d), F32)],
        compiler_params=_params(("arbitrary",)),
    )(y, tgt)


SUBLANES = 8


def _shift_down(x, k):
    t, c = x.shape
    r = pltpu.roll(x.reshape(t // SUBLANES, SUBLANES, c), k, axis=1)
    above = jnp.concatenate([jnp.zeros((1, SUBLANES, c), x.dtype), r[:-1]], axis=0)
    rows = lax.broadcasted_iota(jnp.int32, (1, SUBLANES, c), 1)
    return jnp.where(rows >= k, r, above).reshape(t, c)


def _shift_up(x, k):
    t, c = x.shape
    r = pltpu.roll(x.reshape(t // SUBLANES, SUBLANES, c), SUBLANES - k, axis=1)
    below = jnp.concatenate([r[1:], jnp.zeros((1, SUBLANES, c), x.dtype)], axis=0)
    rows = lax.broadcasted_iota(jnp.int32, (1, SUBLANES, c), 1)
    return jnp.where(rows < SUBLANES - k, r, below).reshape(t, c)


def _conv3(h, w):
    return w[2:3] * h + w[1:2] * _shift_down(h, 1) + w[0:1] * _shift_down(h, 2)


def _conv3_bwd(dc, h, w, dw_ref, cols=slice(None)):
    u1, u2 = _shift_up(dc, 1), _shift_up(dc, 2)
    dw_ref[0:1, cols] = jnp.sum(u2 * h, axis=0, keepdims=True)
    dw_ref[1:2, cols] = jnp.sum(u1 * h, axis=0, keepdims=True)
    dw_ref[2:3, cols] = jnp.sum(dc * h, axis=0, keepdims=True)
    return w[2:3] * dc + w[1:2] * u1 + w[0:1] * u2


FFN_PAIRS = N_DEV // 2


def _lane_chunks(width):
    return [(c0, min(COL_TILE, width - c0)) for c0 in range(0, width, COL_TILE)]


def _ffn_up(n, wup, wdw, name):
    s, d = n.shape
    cw = wup.shape[-1]

    def body(n_ref, wg_ref, wu_ref, dg_ref, du_ref, h_ref, c_ref, a_ref):
        x = n_ref[...]
        for c0, size in _lane_chunks(cw):
            cols = slice(c0, c0 + size)
            hg = _dot(x, wg_ref[:, cols])
            hu = _dot(x, wu_ref[:, cols])
            h_ref[0, :, cols] = hg.astype(BF16)
            h_ref[1, :, cols] = hu.astype(BF16)
            cg = _conv3(hg, dg_ref[:, cols])
            cu = _conv3(hu, du_ref[:, cols])
            c_ref[0, :, cols] = cg.astype(BF16)
            c_ref[1, :, cols] = cu.astype(BF16)
            a_ref[:, cols] = (cg * jax.nn.sigmoid(cg) * cu).astype(BF16)

    return pl.pallas_call(
        body, name=name, grid=(FFN_PAIRS,),
        in_specs=[pl.BlockSpec((s, d), lambda j: (0, 0)),
                  pl.BlockSpec((None, d, cw), lambda j: (j, 0, 0)),
                  pl.BlockSpec((None, d, cw), lambda j: (j + FFN_PAIRS, 0, 0)),
                  pl.BlockSpec((None, 3, cw), lambda j: (j, 0, 0)),
                  pl.BlockSpec((None, 3, cw), lambda j: (j + FFN_PAIRS, 0, 0))],
        out_specs=[pl.BlockSpec((None, 2, s, cw), lambda j: (j, 0, 0, 0)),
                   pl.BlockSpec((None, 2, s, cw), lambda j: (j, 0, 0, 0)),
                   pl.BlockSpec((None, s, cw), lambda j: (j, 0, 0))],
        out_shape=[jax.ShapeDtypeStruct((FFN_PAIRS, 2, s, cw), BF16), jax.ShapeDtypeStruct((FFN_PAIRS, 2, s, cw), BF16),
                   jax.ShapeDtypeStruct((FFN_PAIRS, s, cw), BF16)],
        compiler_params=_params(("parallel",)),
    )(n, wup, wup, wdw, wdw)


def _ffn_mid_bwd(do, wdown, h, c, wdw, name):
    s, d = do.shape
    cw = wdown.shape[1]

    def body(do_ref, wd_ref, h_ref, c_ref, wg_ref, wu_ref, dh_ref, dwg_ref, dwu_ref):
        dov = do_ref[...]
        for c0, size in _lane_chunks(cw):
            cols = slice(c0, c0 + size)
            da = _dot(dov, wd_ref[cols, :], NT)
            hg = h_ref[0, :, cols].astype(F32)
            hu = h_ref[1, :, cols].astype(F32)
            wg, wu = wg_ref[:, cols], wu_ref[:, cols]
            cg = c_ref[0, :, cols].astype(F32)
            cu = c_ref[1, :, cols].astype(F32)
            sg = jax.nn.sigmoid(cg)
            dcu = da * (cg * sg)
            dcg = da * cu * (sg * (1.0 + cg * (1.0 - sg)))
            dh_ref[0, :, cols] = _conv3_bwd(dcg, hg, wg, dwg_ref, cols).astype(BF16)
            dh_ref[1, :, cols] = _conv3_bwd(dcu, hu, wu, dwu_ref, cols).astype(BF16)

    vec = jax.ShapeDtypeStruct((FFN_PAIRS, 3, cw), F32)
    return pl.pallas_call(
        body, name=name, grid=(FFN_PAIRS,),
        in_specs=[pl.BlockSpec((s, d), lambda j: (0, 0)), pl.BlockSpec((None, cw, d), lambda j: (j, 0, 0)),
                  pl.BlockSpec((None, 2, s, cw), lambda j: (j, 0, 0, 0)),
                  pl.BlockSpec((None, 2, s, cw), lambda j: (j, 0, 0, 0)),
                  pl.BlockSpec((None, 3, cw), lambda j: (j, 0, 0)),
                  pl.BlockSpec((None, 3, cw), lambda j: (j + FFN_PAIRS, 0, 0))],
        out_specs=[pl.BlockSpec((None, 2, s, cw), lambda j: (j, 0, 0, 0)),
                   pl.BlockSpec((None, 3, cw), lambda j: (j, 0, 0)), pl.BlockSpec((None, 3, cw), lambda j: (j, 0, 0))],
        out_shape=[jax.ShapeDtypeStruct((FFN_PAIRS, 2, s, cw), BF16), vec, vec],
        compiler_params=_params(("parallel",)),
    )(do, wdown, h, c, wdw, wdw)


def _ffn_dwup(n, dh, name):
    s, d = n.shape
    cw = dh.shape[-1]

    def body(n_ref, dh_ref, o_ref):
        o_ref[...] = _dot(n_ref[...], dh_ref[...], TN).astype(BF16)

    return pl.pallas_call(
        body, name=name, grid=(N_DEV,),
        in_specs=[pl.BlockSpec((s, d), lambda k: (0, 0)),
                  pl.BlockSpec((None, None, s, cw), lambda k: (k % FFN_PAIRS, k // FFN_PAIRS, 0, 0))],
        out_specs=pl.BlockSpec((None, d, cw), lambda k: (k, 0, 0)),
        out_shape=jax.ShapeDtypeStruct((N_DEV, d, cw), BF16),
        compiler_params=_params(("parallel",)),
    )(n, dh)


def _ffn_dn(dh, wup, name, dep=None):
    s, cw = dh.shape[-2:]
    d = wup.shape[1]
    tm = _pick(s, MATMUL_TILES)
    deps = [] if dep is None else [dep]

    def body(dh_ref, w_ref, *rest):
        o_ref, acc_ref = rest[-2:]
        k = pl.program_id(1)

        @pl.when(k == 0)
        def _():
            acc_ref[...] = jnp.zeros_like(acc_ref)

        acc_ref[...] += _dot(dh_ref[...], w_ref[...], NT)

        @pl.when(k == N_DEV - 1)
        def _():
            o_ref[...] = acc_ref[...]

    return pl.pallas_call(
        body, name=name, grid=(s // tm, N_DEV),
        in_specs=[pl.BlockSpec((None, None, tm, cw), lambda i, k: (k % FFN_PAIRS, k // FFN_PAIRS, i, 0)),
                  pl.BlockSpec((None, d, cw), lambda i, k: (k, 0, 0))] + [_ANY] * len(deps),
        out_specs=pl.BlockSpec((tm, d), lambda i, k: (i, 0)),
        out_shape=jax.ShapeDtypeStruct((s, d), F32),
        scratch_shapes=[pltpu.VMEM((tm, d), F32)],
        compiler_params=_params(("parallel", "arbitrary")),
    )(dh, wup, *deps)


def _sconv_fwd(n, win, wdw, name):
    s, d = n.shape
    tn = COL_TILE
    nj = d // tn

    def body(n_ref, wb_ref, wc_ref, wh_ref, dw_ref, z_ref, y_ref):
        x = n_ref[...]
        zb = _dot(x, wb_ref[...])
        zc = _dot(x, wc_ref[...])
        zh = _dot(x, wh_ref[...])
        z_ref[0] = zb.astype(BF16)
        z_ref[1] = zc.astype(BF16)
        z_ref[2] = zh.astype(BF16)
        y_ref[...] = (zb * _conv3(zc * zh, dw_ref[...])).astype(BF16)

    return pl.pallas_call(
        body, name=name, grid=(nj,),
        in_specs=[pl.BlockSpec((s, d), lambda j: (0, 0)),
                  pl.BlockSpec((d, tn), lambda j: (0, j)), pl.BlockSpec((d, tn), lambda j: (0, j + nj)),
                  pl.BlockSpec((d, tn), lambda j: (0, j + 2 * nj)), pl.BlockSpec((3, tn), lambda j: (0, j))],
        out_specs=[pl.BlockSpec((3, s, tn), lambda j: (0, 0, j)), pl.BlockSpec((s, tn), lambda j: (0, j))],
        out_shape=[jax.ShapeDtypeStruct((3, s, d), BF16), jax.ShapeDtypeStruct((s, d), BF16)],
        compiler_params=_params(("parallel",)),
    )(n, win, win, win, wdw)


def _sconv_mid_bwd(dm, wout, z, wdw, name):
    s, d = dm.shape
    tn = COL_TILE
    nj = d // tn

    def body(dm_ref, wo_ref, z_ref, w_ref, dz_ref, dw_ref):
        dy = _dot(dm_ref[...], wo_ref[...], NT)
        zb = z_ref[0].astype(F32)
        zc = z_ref[1].astype(F32)
        zh = z_ref[2].astype(F32)
        w = w_ref[...]
        p = zc * zh
        cp = _conv3(p, w)
        dz_ref[0] = (dy * cp).astype(BF16)
        dcp = dy * zb
        dp = _conv3_bwd(dcp, p, w, dw_ref)
        dz_ref[1] = (dp * zh).astype(BF16)
        dz_ref[2] = (dp * zc).astype(BF16)

    return pl.pallas_call(
        body, name=name, grid=(nj,),
        in_specs=[pl.BlockSpec((s, d), lambda j: (0, 0)), pl.BlockSpec((tn, d), lambda j: (j, 0)),
                  pl.BlockSpec((3, s, tn), lambda j: (0, 0, j)), pl.BlockSpec((3, tn), lambda j: (0, j))],
        out_specs=[pl.BlockSpec((3, s, tn), lambda j: (0, 0, j)), pl.BlockSpec((3, tn), lambda j: (0, j))],
        out_shape=[jax.ShapeDtypeStruct((3, s, d), BF16), jax.ShapeDtypeStruct((3, d), F32)],
        compiler_params=_params(("parallel",)),
    )(dm, wout, z, wdw)


def _pool_select(g, c2, c4, c8, c16):
    return jnp.where(g == 0, c2, jnp.where(g == 1, c4, jnp.where(g == 2, c8, c16)))


def _pool_inv_count(g, shape):
    pos = lax.broadcasted_iota(jnp.int32, shape, 0).astype(F32) + 1.0
    win = (2 << g).astype(F32)
    return jnp.minimum(pos, win)


def _pool_fwd(n, win, wgrp, scale, name):
    s, d = n.shape
    tn = POOL_GROUP_DIM

    def body(n_ref, wi_ref, wg_ref, sc_ref, p_ref, y_ref):
        g = pl.program_id(0)
        u = _dot(n_ref[...], wi_ref[...])
        s2 = u + _shift_down(u, 1)
        s4 = s2 + _shift_down(s2, 2)
        s8 = s4 + _shift_down(s4, 4)
        s16 = s8 + _shift_down(s8, 8)
        tot = _pool_select(g, s2, s4, s8, s16)
        p = (tot / _pool_inv_count(g, u.shape) - u).astype(BF16)
        p_ref[...] = p
        y_ref[...] = (_dot(p, wg_ref[...]) * sc_ref[...]).astype(BF16)

    return pl.pallas_call(
        body, name=name, grid=(d // tn,),
        in_specs=[pl.BlockSpec((s, d), lambda g: (0, 0)), pl.BlockSpec((d, tn), lambda g: (0, g)),
                  pl.BlockSpec((None, tn, tn), lambda g: (g, 0, 0)), pl.BlockSpec((1, tn), lambda g: (0, g))],
        out_specs=[pl.BlockSpec((s, tn), lambda g: (0, g)), pl.BlockSpec((s, tn), lambda g: (0, g))],
        out_shape=[jax.ShapeDtypeStruct((s, d), BF16), jax.ShapeDtypeStruct((s, d), BF16)],
        compiler_params=_params(("parallel",)),
    )(n, win, wgrp, scale)


def _pool_mid_bwd(dm, wout, p, wgrp, scale, name):
    s, d = dm.shape
    tn = POOL_GROUP_DIM

    def body(dm_ref, wo_ref, p_ref, wg_ref, sc_ref, du_ref, dwg_ref, dsc_ref):
        g = pl.program_id(0)
        dy = _dot(dm_ref[...], wo_ref[...], NT)
        pv = p_ref[...]
        wg = wg_ref[...]
        ypre = _dot(pv, wg)
        dsc_ref[...] = jnp.sum(dy * ypre, axis=0, keepdims=True)
        dypre = (dy * sc_ref[...]).astype(BF16)
        dwg_ref[...] = _dot(pv, dypre, TN)
        dp = _dot(dypre, wg, NT)
        e = dp / _pool_inv_count(g, dp.shape)
        f2 = e + _shift_up(e, 1)
        f4 = f2 + _shift_up(f2, 2)
        f8 = f4 + _shift_up(f4, 4)
        f16 = f8 + _shift_up(f8, 8)
        du_ref[...] = (_pool_select(g, f2, f4, f8, f16) - dp).astype(BF16)

    return pl.pallas_call(
        body, name=name, grid=(d // tn,),
        in_specs=[pl.BlockSpec((s, d), lambda g: (0, 0)), pl.BlockSpec((tn, d), lambda g: (g, 0)),
                  pl.BlockSpec((s, tn), lambda g: (0, g)), pl.BlockSpec((None, tn, tn), lambda g: (g, 0, 0)),
                  pl.BlockSpec((1, tn), lambda g: (0, g))],
        out_specs=[pl.BlockSpec((s, tn), lambda g: (0, g)), pl.BlockSpec((None, tn, tn), lambda g: (g, 0, 0)),
                   pl.BlockSpec((1, tn), lambda g: (0, g))],
        out_shape=[jax.ShapeDtypeStruct((s, d), BF16), jax.ShapeDtypeStruct((4, tn, tn), F32),
                   jax.ShapeDtypeStruct((1, d), F32)],
        compiler_params=_params(("parallel",)),
    )(dm, wout, p, wgrp, scale)


PANEL = LANES
ATTN_EXT = ATTN_WIDTH + PANEL
DVEC_LANE = HEADS


def _alibi_slopes(g, dil):
    all_slopes = 2.0 ** (-8.0 * np.arange(1, N_HEADS_A + 1) / N_HEADS_A)
    return [float(np.float32(sl) * np.float32(dil)) for sl in all_slopes[g * HEADS:(g + 1) * HEADS]]


def _residue_order(a, dil, name):
    s, w = a.shape
    per = ROW_TILE // dil
    panels = w // PANEL

    def body(a_ref, o_ref, *tiles):
        for c in range(panels):
            cols = slice(c * PANEL, (c + 1) * PANEL)
            tiles[c][...] = a_ref[:, cols].astype(F32)
            for r in range(dil):
                o_ref[r, :, cols] = tiles[c][pl.ds(r, per, stride=dil), :].astype(o_ref.dtype)

    out = pl.pallas_call(
        body, name=name, grid=(s // ROW_TILE,),
        in_specs=[pl.BlockSpec((ROW_TILE, w), lambda i: (i, 0))],
        out_specs=pl.BlockSpec((dil, per, w), lambda i: (0, i, 0)),
        out_shape=jax.ShapeDtypeStruct((dil, s // dil, w), a.dtype),
        scratch_shapes=[pltpu.VMEM((ROW_TILE, PANEL), F32)] * panels,
        compiler_params=_params(("parallel",)),
    )(a)
    return out.reshape(s, w)


def _token_order(a, dil, acc, name):
    s, w = a.shape
    per = ROW_TILE // dil
    panels = w // PANEL
    has_acc = acc is not None

    def body(*refs):
        a_ref = refs[0]
        o_ref = refs[2] if has_acc else refs[1]
        tiles = refs[3:] if has_acc else refs[2:]
        for c in range(panels):
            cols = slice(c * PANEL, (c + 1) * PANEL)
            for r in range(dil):
                tiles[c][pl.ds(r, per, stride=dil), :] = a_ref[r, :, cols]
            v = tiles[c][...]
            if has_acc:
                v = v + refs[1][:, cols]
            o_ref[:, cols] = v

    row = pl.BlockSpec((ROW_TILE, w), lambda i: (i, 0))
    return pl.pallas_call(
        body, name=name, grid=(s // ROW_TILE,),
        in_specs=[pl.BlockSpec((dil, per, w), lambda i: (0, i, 0))] + ([row] if has_acc else []),
        out_specs=row, out_shape=jax.ShapeDtypeStruct((s, w), F32),
        scratch_shapes=[pltpu.VMEM((ROW_TILE, PANEL), F32)] * panels,
        compiler_params=_params(("parallel",)),
    )(*([a.reshape(dil, s // dil, w)] + ([acc] if has_acc else [])))


def _qkv_proj(n, wqkv, g, name):
    s, d = n.shape
    tm = _pick(s, MATMUL_TILES)

    def body(a_ref, b_ref, o_ref):
        o_ref[...] = _dot(a_ref[...], b_ref[...]).astype(BF16)

    return pl.pallas_call(
        body, name=name, grid=(s // tm, 3),
        in_specs=[pl.BlockSpec((tm, d), lambda i, t: (i, 0)),
                  pl.BlockSpec((d, ATTN_WIDTH), lambda i, t: (0, 3 * g + t))],
        out_specs=pl.BlockSpec((None, tm, ATTN_WIDTH), lambda i, t: (t, i, 0)),
        out_shape=jax.ShapeDtypeStruct((3, s, ATTN_WIDTH), BF16),
        compiler_params=_params(("parallel", "parallel")),
    )(n, wqkv)


def _attn_window(n, ln):
    if ln == BLOCK:
        return 0, BLOCK
    return pl.multiple_of(jnp.maximum(n - 1, 0) * BLOCK, BLOCK), 2 * BLOCK


def _attn_mask(n, k0, kw):
    qpos = n * BLOCK + lax.broadcasted_iota(jnp.int32, (BLOCK, kw), 0)
    kpos = k0 + lax.broadcasted_iota(jnp.int32, (BLOCK, kw), 1)
    dist = qpos - kpos
    return dist.astype(F32), (dist >= 0) & (dist <= BLOCK)


def _attn_scores(q, keys, slope, dist, valid):
    s = _dot(q, keys, NT) * (HEAD_DIM ** -0.5) - slope * dist
    return jnp.where(valid, s, NEG_INF)


ATTN_STEP_BLOCKS = 1
ATTN_BWD_STEP_BLOCKS = 4


def _attn_block(gb, ln):
    nb = ln // BLOCK
    n, base = (0, gb * ln) if nb == 1 else (gb % nb, (gb // nb) * ln)
    k0, kw = _attn_window(n, ln)
    cur = pl.ds(pl.multiple_of(gb * BLOCK, BLOCK), BLOCK)
    win = pl.ds(pl.multiple_of(base + k0, BLOCK), kw)
    return cur, win, n, k0, kw


def _attn_fwd(qkv, g, name):
    _, s, w = qkv.shape
    dil = DILATED_CFG[g][1]
    ln = s // dil
    slopes = _alibi_slopes(g, dil)
    rows = ATTN_STEP_BLOCKS * BLOCK

    def body(qkv_ref, o_ref):
        o_ref[:, w:] = jnp.zeros((rows, PANEL), F32)
        for b in range(ATTN_STEP_BLOCKS):
            cur, win, n, k0, kw = _attn_block(pl.program_id(0) * ATTN_STEP_BLOCKS + b, ln)
            dist, valid = _attn_mask(n, k0, kw)
            out = slice(b * BLOCK, (b + 1) * BLOCK)
            for h in range(HEADS):
                cols = slice(h * HEAD_DIM, (h + 1) * HEAD_DIM)
                sc = _attn_scores(qkv_ref[0, cur, cols], qkv_ref[1, win, cols], slopes[h], dist, valid)
                m = jnp.max(sc, axis=-1, keepdims=True)
                p = jnp.exp(sc - m)
                den = jnp.sum(p, axis=-1, keepdims=True)
                o_ref[out, cols] = _dot(p.astype(BF16), qkv_ref[2, win, cols]) / den
                o_ref[out, w + h:w + h + 1] = m + jnp.log(den)

    return pl.pallas_call(
        body, name=name, grid=(s // rows,),
        in_specs=[pl.BlockSpec((3, s, w), lambda i: (0, 0, 0))],
        out_specs=pl.BlockSpec((rows, ATTN_EXT), lambda i: (i, 0)),
        out_shape=jax.ShapeDtypeStruct((s, ATTN_EXT), F32),
        compiler_params=_params(("parallel",)),
    )(qkv)


def _attn_bwd(qkv, dext, g, name, dep=None):
    _, s, w = qkv.shape
    dil = DILATED_CFG[g][1]
    ln = s // dil
    slopes = _alibi_slopes(g, dil)
    scale = HEAD_DIM ** -0.5
    rows = ATTN_BWD_STEP_BLOCKS * BLOCK
    steps = s // rows
    deps = [] if dep is None else [dep]

    def body(qkv_ref, de_ref, *rest):
        d_ref, dk_ref, dv_ref = rest[-3:]

        @pl.when(pl.program_id(0) == 0)
        def _():
            dk_ref[...] = jnp.zeros_like(dk_ref)
            dv_ref[...] = jnp.zeros_like(dv_ref)

        for b in range(ATTN_BWD_STEP_BLOCKS):
            cur, win, n, k0, kw = _attn_block(pl.program_id(0) * ATTN_BWD_STEP_BLOCKS + b, ln)
            dist, valid = _attn_mask(n, k0, kw)
            blk = slice(b * BLOCK, (b + 1) * BLOCK)
            for h in range(HEADS):
                cols = slice(h * HEAD_DIM, (h + 1) * HEAD_DIM)
                q, keys = qkv_ref[0, cur, cols], qkv_ref[1, win, cols]
                dob = de_ref[blk, cols].astype(BF16)
                p = jnp.exp(_attn_scores(q, keys, slopes[h], dist, valid) - de_ref[blk, w + h:w + h + 1])
                dd = de_ref[blk, w + DVEC_LANE + h:w + DVEC_LANE + h + 1]
                ds = (p * (_dot(dob, qkv_ref[2, win, cols], NT) - dd)).astype(BF16)
                d_ref[0, cur, cols] = (scale * _dot(ds, keys)).astype(BF16)
                dv_ref[win, cols] += _dot(p.astype(BF16), dob, TN)
                dk_ref[win, cols] += scale * _dot(ds, q, TN)

        @pl.when(pl.program_id(0) == steps - 1)
        def _():
            d_ref[1] = dk_ref[...].astype(BF16)
            d_ref[2] = dv_ref[...].astype(BF16)

    whole = pl.BlockSpec((3, s, w), lambda i: (0, 0, 0))
    return pl.pallas_call(
        body, name=name, grid=(steps,),
        in_specs=[whole, pl.BlockSpec((rows, ATTN_EXT), lambda i: (i, 0))] + [_ANY] * len(deps),
        out_specs=whole, out_shape=jax.ShapeDtypeStruct((3, s, w), BF16),
        scratch_shapes=[pltpu.VMEM((s, w), F32), pltpu.VMEM((s, w), F32)],
        compiler_params=_params(("arbitrary",)),
    )(qkv, dext, *deps)


def _attn_merge(e0, e1, e2, name):
    s = e0.shape[0]
    w = ATTN_WIDTH

    def body(e0_ref, e1_ref, e2_ref, m_ref, mb_ref, lse_ref):
        refs = (e0_ref, e1_ref, e2_ref)
        l = [r[:, w:w + HEADS] for r in refs]
        mx = jnp.maximum(jnp.maximum(l[0], l[1]), l[2])
        e = [jnp.exp(v - mx) for v in l]
        z = e[0] + e[1] + e[2]
        lse_ref[...] = mx + jnp.log(z)
        wts = [v / z for v in e]
        for h in range(HEADS):
            cols = slice(h * HEAD_DIM, (h + 1) * HEAD_DIM)
            acc = wts[0][:, h:h + 1] * refs[0][:, cols]
            for g in range(1, N_GROUPS_A):
                acc = acc + wts[g][:, h:h + 1] * refs[g][:, cols]
            m_ref[:, cols] = acc
            mb_ref[:, cols] = acc.astype(BF16)

    ext = pl.BlockSpec((ROW_TILE, ATTN_EXT), lambda i: (i, 0))
    row = pl.BlockSpec((ROW_TILE, w), lambda i: (i, 0))
    return pl.pallas_call(
        body, name=name, grid=(s // ROW_TILE,),
        in_specs=[ext, ext, ext],
        out_specs=[row, row, pl.BlockSpec((ROW_TILE, HEADS), lambda i: (i, 0))],
        out_shape=[jax.ShapeDtypeStruct((s, w), F32), jax.ShapeDtypeStruct((s, w), BF16),
                   jax.ShapeDtypeStruct((s, HEADS), F32)],
        compiler_params=_params(("parallel",)),
    )(e0, e1, e2)


def _attn_dvec(dmerged, merged, lse_all, name, dep=None):
    s, w = merged.shape
    deps = [] if dep is None else [dep]

    def body(dm_ref, m_ref, lse_ref, *rest):
        de_ref = rest[-1]
        dmv = dm_ref[...]
        de_ref[:, :w] = dmv
        de_ref[:, w:] = jnp.zeros((ROW_TILE, PANEL), F32)
        de_ref[:, w:w + HEADS] = lse_ref[...]
        prod = dmv * m_ref[...]
        for h in range(HEADS):
            lane = w + DVEC_LANE + h
            de_ref[:, lane:lane + 1] = jnp.sum(prod[:, h * HEAD_DIM:(h + 1) * HEAD_DIM], axis=-1, keepdims=True)

    row = pl.BlockSpec((ROW_TILE, w), lambda i: (i, 0))
    return pl.pallas_call(
        body, name=name, grid=(s // ROW_TILE,),
        in_specs=[row, row, pl.BlockSpec((ROW_TILE, HEADS), lambda i: (i, 0))] + [_ANY] * len(deps),
        out_specs=pl.BlockSpec((ROW_TILE, ATTN_EXT), lambda i: (i, 0)),
        out_shape=jax.ShapeDtypeStruct((s, ATTN_EXT), F32),
        compiler_params=_params(("parallel",)),
    )(dmerged, merged, lse_all, *deps)


def _attention_fwd(n, wqkv, wo, tag):
    ns, qkvs, exts = [], [], []
    for g, (_, dil) in enumerate(DILATED_CFG):
        ng = n if dil == 1 else _residue_order(n, dil, f"{tag}_order_g{g}")
        qkv = _qkv_proj(ng, wqkv, g, f"{tag}_qkv_g{g}")
        ext = _attn_fwd(qkv, g, f"{tag}_fwd_g{g}")
        ns.append(ng)
        qkvs.append(qkv)
        exts.append(ext if dil == 1 else _token_order(ext, dil, None, f"{tag}_unorder_g{g}"))
    merged, merged_bf, lse_all = _attn_merge(*exts, f"{tag}_merge")
    m = _matmul(merged_bf, wo, "nn", F32, f"{tag}_wo")
    return m, (ns, qkvs, merged, merged_bf, lse_all)


def _attention_bwd(dm, wqkv, wo, saved, tag, dep=None, hook=None):
    ns, qkvs, merged, merged_bf, lse_all = saved
    d_wo = _matmul(merged_bf, dm, "tn", BF16, f"{tag}_dwo")
    dmerged = _matmul(dm, wo, "nt", F32, f"{tag}_dmerged")
    dext = _attn_dvec(dmerged, merged, lse_all, f"{tag}_dvec", dep)
    width = 3 * ATTN_WIDTH
    d_wqkv, dn, dep = [], None, None
    for g, (_, dil) in enumerate(DILATED_CFG):
        dext_g = dext if dil == 1 else _residue_order(dext, dil, f"{tag}_dorder_g{g}")
        dqkv = _attn_bwd(qkvs[g], dext_g, g, f"{tag}_bwd_g{g}", dep)
        dep = hook(g, dqkv) if hook is not None and g + 1 < N_GROUPS_A else None
        d_wqkv.append(_matmul(ns[g], dqkv, "tn", BF16, f"{tag}_dwqkv_g{g}", b_parts=3))
        dn_g = _matmul(dqkv, wqkv[:, g * width:(g + 1) * width], "nt", F32, f"{tag}_dn_g{g}", a_parts=3)
        dn = dn_g if dil == 1 else _token_order(dn_g, dil, dn, f"{tag}_dn_sum_g{g}")
    return dn, jnp.concatenate(d_wqkv, axis=1), d_wo


def _layer_matrices(i):
    mixer = (("attn_w_qkv", "attn_w_o"), ("conv_w_in", "conv_w_out"), ("pool_w_in", "pool_w_grp", "pool_w_out"))[i % 3]
    return [(k, i // 3) for k in mixer] + [("ffn_w_up", i), ("ffn_w_down", i)]


def _local_step(x, tgt, vec, weights, sink):
    ng = vec["norm_g"]

    def gain(i, j, token=None):
        g = ng[i, j][None, :]
        return g if token is None else g + token

    saved = []
    n = _rms_fwd(x, gain(0, 0), None, BF16, "norm_first")
    for i in range(DEPTH):
        wl = weights.layer(i)
        t0 = weights.hook(i, 0, n)
        kind, idx = i % 3, i // 3
        if kind == 0:
            m, ms = _attention_fwd(n, wl["attn_w_qkv"], wl["attn_w_o"], "attn")
        elif kind == 1:
            taps = vec["conv_w_dw"][idx] if t0 is None else vec["conv_w_dw"][idx] + t0
            z, y = _sconv_fwd(n, wl["conv_w_in"], taps, "sconv_fwd")
            m = _matmul(y, wl["conv_w_out"], "nn", F32, "sconv_out")
            ms = (z, y)
        else:
            scale = vec["pool_scale"][idx][None, :] if t0 is None else vec["pool_scale"][idx][None, :] + t0
            p, y = _pool_fwd(n, wl["pool_w_in"], wl["pool_w_grp"], scale, "pool_fwd")
            m = _matmul(y, wl["pool_w_out"], "nn", F32, "pool_out")
            ms = (p, y)
        t1 = weights.hook(i, 1, m)
        x1, n2 = _rms_res_pre(m, gain(i, 1, t0), x, gain(i, 2, t1), "norm_res_pre")
        h, c, a = _ffn_up(n2, wl["ffn_w_up"], vec["ffn_w_dw"][i], "ffn_up")
        t2 = weights.hook(i, 2, a)
        f = _matmul(a, wl["ffn_w_down"].reshape(D_FF, D_MODEL), "nn", F32, "ffn_down", a_parts=FFN_PAIRS)
        saved.append((x, n, m, ms, x1, n2, h, a, f, wl, c))
        if i + 1 < DEPTH:
            x, n = _rms_res_pre(f, gain(i, 3, t2), x1, gain(i + 1, 0), "norm_res_pre")
        else:
            x = _rms_fwd(f, gain(i, 3), x1, F32, "norm_res")
        weights.hook(i, 3, x)

    loss, dx = _loss_head(x, tgt, "loss_head")

    g_norm = [[None] * 4 for _ in range(DEPTH)]
    g_taps, g_scale, g_ffn_dw = [], [], [None] * DEPTH
    df, g_norm[DEPTH - 1][3] = _rms_bwd(saved[-1][8], gain(DEPTH - 1, 3), dx, None, BF16, "norm_bwd_sub")
    t0 = None
    for i in reversed(range(DEPTH)):
        xin, n, m, ms, x1, n2, h, a, f, wl, c = saved[i]
        kind, idx = i % 3, i // 3
        gl = {}
        d_wdown = _matmul(a, df, "tn", BF16, "ffn_dwdown", a_parts=FFN_PAIRS)
        gl["ffn_w_down"] = d_wdown.reshape(N_DEV, D_FF // N_DEV, D_MODEL)
        ffn_taps = vec["ffn_w_dw"][i] if t0 is None else vec["ffn_w_dw"][i] + t0
        dh, dwg, dwu = _ffn_mid_bwd(df, wl["ffn_w_down"].reshape(FFN_PAIRS, -1, D_MODEL), h, c, ffn_taps, "ffn_mid_bwd")
        g_ffn_dw[i] = jnp.concatenate([dwg, dwu], axis=0)
        t1 = sink.hook(i, 1, dh)
        gl["ffn_w_up"] = _ffn_dwup(n2, dh, "ffn_dwup")
        tf = sink.ffn_done(i, gl)
        dn2 = _ffn_dn(dh, wl["ffn_w_up"], "ffn_dn", t1)
        dx1, dm, g_norm[i][2], g_norm[i][1] = _rms_bwd_pair(x1, gain(i, 2, tf), dn2, dx, m, gain(i, 1), "norm_bwd_pair")
        t2 = sink.hook(i, 2, dm)
        if kind == 0:
            dn, gl["attn_w_qkv"], gl["attn_w_o"] = _attention_bwd(
                dm, wl["attn_w_qkv"], wl["attn_w_o"], ms, "attn", t2, lambda g, after, i=i: sink.hook(i, ("a", "b")[g], after))
        elif kind == 1:
            z, y = ms
            gl["conv_w_out"] = _matmul(y, dm, "tn", BF16, "sconv_dwout")
            taps = vec["conv_w_dw"][idx] if t2 is None else vec["conv_w_dw"][idx] + t2
            dz, ddw = _sconv_mid_bwd(dm, wl["conv_w_out"], z, taps, "sconv_mid_bwd")
            g_taps.append(ddw)
            gl["conv_w_in"] = _matmul(n, dz, "tn", BF16, "sconv_dwin", b_parts=3)
            dn = _matmul(dz, wl["conv_w_in"], "nt", F32, "sconv_dn", a_parts=3)
        else:
            p, y = ms
            gl["pool_w_out"] = _matmul(y, dm, "tn", BF16, "pool_dwout")
            scale = vec["pool_scale"][idx][None, :] if t2 is None else vec["pool_scale"][idx][None, :] + t2
            du, gl["pool_w_grp"], dscale = _pool_mid_bwd(dm, wl["pool_w_out"], p, wl["pool_w_grp"], scale, "pool_mid_bwd")
            g_scale.append(dscale[0])
            gl["pool_w_in"] = _matmul(n, du, "tn", BF16, "pool_dwin")
            dn = _matmul(du, wl["pool_w_in"], "nt", F32, "pool_dn")
        sink.hook(i, 3, dn)
        if i > 0:
            dx, df, g_norm[i][0], g_norm[i - 1][3] = _rms_bwd_pair(xin, gain(i, 0, t2), dn, dx1, saved[i - 1][8],
                                                                   gain(i - 1, 3), "norm_bwd_pair")
        else:
            dx, g_norm[0][0] = _rms_bwd(xin, gain(0, 0), dn, dx1, F32, "norm_bwd_res")
        t0 = sink.layer_done(i, gl)

    vec_grads = {"norm_g": jnp.stack([jnp.concatenate(row, axis=0) for row in g_norm]), "conv_w_dw": jnp.stack(g_taps),
                 "pool_scale": jnp.stack(g_scale), "ffn_w_dw": g_ffn_dw}
    return loss, dx, vec_grads


_AXES = ("x", "y", "c")
ROUTE_A = ("y", "x", "c")
ROUTE_B = ("x", "y", "c")
def _dev_index(pos):
    return 4 * pos["x"] + 2 * pos["y"] + pos["c"]


_HBM = pl.BlockSpec(memory_space=pltpu.HBM)
_SEM = pl.BlockSpec(memory_space=pltpu.SEMAPHORE)
_ANY = pl.BlockSpec(memory_space=pl.ANY)
_EFFECT = pltpu.SideEffectType.DATAFLOW_SIDE_EFFECTING


TOKEN_SHAPE = (1, D_MODEL)


def _copies_start(describe, arrays, n_copies, name, after, token_shape=TOKEN_SHAPE):
    n = len(arrays)
    deps = [] if after is None else [after]

    def body(*refs):
        send_sems, recv_sems = refs[n + len(deps)], refs[n + len(deps) + 1]
        for c in describe(refs[:n], send_sems, recv_sems):
            c.start()
        refs[-1][...] = jnp.zeros_like(refs[-1])

    outs = pl.pallas_call(
        body, name=f"{name}_start",
        out_shape=(pltpu.SemaphoreType.DMA((n_copies,)), pltpu.SemaphoreType.DMA((n_copies,)),
                   *[pltpu.HBM(a.shape, a.dtype) for a in arrays], jax.ShapeDtypeStruct(token_shape, F32)),
        in_specs=[_HBM] * n + [_ANY] * len(deps),
        out_specs=(_SEM, _SEM, *([_HBM] * n), pl.BlockSpec(memory_space=pltpu.VMEM)),
        input_output_aliases={i: 2 + i for i in range(n)},
        compiler_params=pltpu.CompilerParams(has_side_effects=_EFFECT),
    )(*[pltpu.with_memory_space_constraint(a, pltpu.HBM) for a in arrays], *deps)
    return (outs[0], outs[1], list(outs[2:2 + n])), outs[-1]


def _copies_wait(describe, handle, name, after):
    send_sems, recv_sems, arrays = handle
    n = len(arrays)
    deps = [] if after is None else list(after) if isinstance(after, (list, tuple)) else [after]

    def body(*refs):
        for c in describe(refs[:n], refs[n], refs[n + 1]):
            c.wait_send()
            c.wait_recv()

    outs = pl.pallas_call(
        body, name=f"{name}_wait",
        out_shape=tuple(pltpu.HBM(a.shape, a.dtype) for a in arrays),
        in_specs=[_HBM] * n + [_SEM, _SEM] + [_ANY] * len(deps), out_specs=tuple([_HBM] * n),
        input_output_aliases={i: i for i in range(n)},
        compiler_params=pltpu.CompilerParams(has_side_effects=_EFFECT),
    )(*arrays, send_sems, recv_sems, *deps)
    return list(outs)


GATHER_STAGE_COPIES = (3, 3, 1)


def _gather_copies(stage, routes):
    n = len(routes)

    def describe(refs, send_sems, recv_sems):
        pos = {a: lax.axis_index(a) for a in _AXES}

        def flipped(axes):
            return {a: 1 - pos[a] if a in axes else pos[a] for a in _AXES}

        copies = []
        for i, (a1, a2, a3) in enumerate(routes):
            land = refs[n + i] if stage == 1 else refs[i]
            p1, p2, p12, p3 = flipped((a1,)), flipped((a2,)), flipped((a1, a2)), flipped((a3,))
            plan = {1: [(None, p1), (None, p2), (None, p3)], 2: [(p1, p2), (p1, p3), (p2, p3)], 3: [(p12, p3)]}[stage]
            for holder, to in plan:
                slot = land.at[_dev_index(pos if holder is None else holder)]
                k = len(copies)
                copies.append(pltpu.make_async_remote_copy(
                    src_ref=refs[i] if holder is None else slot, dst_ref=slot,
                    send_sem=send_sems.at[k], recv_sem=recv_sems.at[k],
                    device_id=tuple(to[a] for a in _AXES), device_id_type=pl.DeviceIdType.MESH))
        return copies

    return describe


def _gather_begin(shards, routes, name, after):
    n = len(shards)
    lands = [lax.empty((N_DEV,) + a.shape, a.dtype) for a in shards]
    handle, token = _copies_start(_gather_copies(1, routes), list(shards) + lands, GATHER_STAGE_COPIES[0] * n,
                                  f"{name}_1", after)
    return {"stage": 1, "handle": handle, "routes": routes, "name": name, "n": n}, token


def _gather_next(state, after):
    stage, routes, name, n = state["stage"], state["routes"], state["name"], state["n"]
    arrays = _copies_wait(_gather_copies(stage, routes), state["handle"], f"{name}_{stage}", after)
    if stage == 1:
        state = dict(state, shards=arrays[:n])
        arrays = arrays[n:]
    if stage == 3:
        me = _dev_index({a: lax.axis_index(a) for a in _AXES})
        return [lax.dynamic_update_index_in_dim(o, s, me, 0) for o, s in zip(arrays, state["shards"])], None
    handle, token = _copies_start(_gather_copies(stage + 1, routes), arrays, GATHER_STAGE_COPIES[stage] * n,
                                  f"{name}_{stage + 1}", None)
    return dict(state, stage=stage + 1, handle=handle), token


ADD_ROW_TILES = (1024, 704, 512, 352, 256, 128, 96, 64, 32, 16)


def _add_half(a, recv, me, out_dtype, name):
    p, q, cols = recv.shape
    tr = _pick(q, ADD_ROW_TILES)

    def body(me_ref, a_ref, b_ref, o_ref):
        o_ref[...] = (a_ref[...].astype(F32) + b_ref[...].astype(F32)).astype(o_ref.dtype)

    return pl.pallas_call(
        body, name=name,
        grid_spec=pltpu.PrefetchScalarGridSpec(
            num_scalar_prefetch=1, grid=(p, q // tr),
            in_specs=[pl.BlockSpec((None, None, tr, cols), lambda j, i, m: (j, m[0], i, 0)),
                      pl.BlockSpec((None, tr, cols), lambda j, i, m: (j, i, 0))],
            out_specs=pl.BlockSpec((None, tr, cols), lambda j, i, m: (j, i, 0))),
        out_shape=jax.ShapeDtypeStruct((p, q, cols), out_dtype),
        compiler_params=_params(("parallel", "parallel")),
    )(me, a, recv)


def _half_copies(axes):
    n = len(axes)

    def describe(refs, send_sems, recv_sems):
        pos = {a: lax.axis_index(a) for a in _AXES}
        copies = []
        for i, axis in enumerate(axes):
            peer = tuple(1 - pos[a] if a == axis else pos[a] for a in _AXES)
            copies.append(pltpu.make_async_remote_copy(
                src_ref=refs[i].at[:, 1 - pos[axis]], dst_ref=refs[n + i], send_sem=send_sems.at[i],
                recv_sem=recv_sems.at[i], device_id=peer, device_id_type=pl.DeviceIdType.MESH))
        return copies

    return describe


def _scatter_begin(slots, routes, tags, name, token_shape=TOKEN_SHAPE):
    shapes = [a.shape[1:] for a in slots]
    rows = [math.prod(s[:-1]) for s in shapes]
    arrays = [a.reshape(4, 2, n, s[-1]) for a, n, s in zip(slots, rows, shapes)]
    return _scatter_start({"stage": 0, "arrays": arrays, "routes": routes, "tags": tags, "name": name,
                           "shapes": shapes, "rows": rows}, token_shape)


def _scatter_start(state, token_shape=TOKEN_SHAPE):
    stage, arrays = state["stage"], state["arrays"]
    axes = [r[2 - stage] for r in state["routes"]]
    lands = [lax.empty((a.shape[0],) + a.shape[2:], a.dtype) for a in arrays]
    handle, token = _copies_start(_half_copies(axes), arrays + lands, len(arrays), f"{state['name']}_{stage + 1}", None,
                                  token_shape)
    return dict(state, handle=handle, axes=axes), token


def _scatter_next(state, after):
    stage, axes, n = state["stage"], state["axes"], len(state["arrays"])
    both = _copies_wait(_half_copies(axes), state["handle"], f"{state['name']}_{stage + 1}", after)
    coord = {a: lax.axis_index(a).astype(jnp.int32).reshape(1) for a in _AXES}
    sums = [_add_half(a, r, coord[ax], F32 if stage == 2 else BF16, f"scatter_add_{stage + 1}_{t}")
            for a, r, ax, t in zip(both[:n], both[n:], axes, state["tags"])]
    if stage == 2:
        return [a.reshape(s) for a, s in zip(sums, state["shapes"])], None
    if stage == 0:
        views = [(1, 2, 2 * r, s[-1]) if route[1] == "x" else (2, 2, r, s[-1])
                 for r, s, route in zip(state["rows"], state["shapes"], state["routes"])]
    else:
        views = [(1, 2, r, s[-1]) for r, s in zip(state["rows"], state["shapes"])]
    return _scatter_start(dict(state, stage=stage + 1, arrays=[a.reshape(v) for a, v in zip(sums, views)]))


_WEIGHTS = {
    "norm_g": ((DEPTH, 4, D_MODEL), 2, True),
    "attn_w_qkv": ((2, D_MODEL, 4608), 2, False),
    "attn_w_o": ((2, ATTN_WIDTH, D_MODEL), 2, False),
    "conv_w_in": ((1, D_MODEL, 3 * D_MODEL), 2, False),
    "conv_w_dw": ((1, 3, D_MODEL), 2, True),
    "conv_w_out": ((1, D_MODEL, D_MODEL), 1, False),
    "pool_w_in": ((1, D_MODEL, D_MODEL), 1, False),
    "pool_w_grp": ((1, 4, POOL_GROUP_DIM, POOL_GROUP_DIM), 2, False),
    "pool_scale": ((1, D_MODEL), 1, True),
    "pool_w_out": ((1, D_MODEL, D_MODEL), 1, False),
    "ffn_w_up": ((DEPTH, D_MODEL, 2 * D_FF), 2, False),
    "ffn_w_dw": ((DEPTH, 3, 2 * D_FF), 2, True),
    "ffn_w_down": ((DEPTH, D_FF, D_MODEL), 1, False),
}
_NAMES = tuple(_WEIGHTS)
_VECTORS = tuple(k for k in _NAMES if _WEIGHTS[k][2])
_MATRICES = tuple(k for k in _NAMES if not _WEIGHTS[k][2])
_FFN = ("ffn_w_up", "ffn_w_down")
_ON_ROUTE_A = ("ffn_w_up", "attn_w_o", "conv_w_out", "pool_w_in")
PACK_ROWS = 16


def _route(name):
    return ROUTE_A if name in _ON_ROUTE_A else ROUTE_B


def _shard_shape(name):
    shape, ax, _ = _WEIGHTS[name]
    return tuple(s // N_DEV if i == ax else s for i, s in enumerate(shape))


def _full_from_slots(slots, name, layers=None):
    shape, ax, _ = _WEIGHTS[name]
    if layers is not None:
        shape = (layers,) + shape[1:]
    return jnp.moveaxis(slots, 0, ax).reshape(shape)


def _slots_from_full(full, name):
    shape, ax, _ = _WEIGHTS[name]
    split = shape[:ax] + (N_DEV, shape[ax] // N_DEV) + shape[ax + 1:]
    return jnp.moveaxis(full.reshape(split), ax, 0)


def _pack_vectors(parts, lead):
    rows = []
    for k in _VECTORS:
        r = parts[k].reshape(lead + (-1, LANES))
        pad = -r.shape[-2] % PACK_ROWS
        rows.append(jnp.pad(r, [(0, 0)] * len(lead) + [(0, pad), (0, 0)]))
    return jnp.concatenate(rows, axis=len(lead))


def _unpack_vectors(buf, lead):
    out, r0 = {}, 0
    for k in _VECTORS:
        shard = _shard_shape(k)
        rows = math.prod(shard) // LANES
        out[k] = buf[..., r0:r0 + rows, :].reshape(lead + shard)
        r0 += rows + (-rows % PACK_ROWS)
    return out


class _LayerWeights:
    def __init__(self, shards):
        first, ffn0 = _layer_matrices(0)[:-2], _layer_matrices(0)[-2:]
        state, token = _gather_begin([shards[k][j].astype(BF16) for k, j in first] + [_pack_vectors(shards, ())],
                                     [_route(k) for k, _ in first] + [ROUTE_B], "gather0", None)
        self.cast = {k: (shards[k] + token[0, 0]).astype(BF16) for k in _MATRICES}
        state, _ = _gather_next(state, self._send(ffn0) + self._send(_layer_matrices(1)))
        state, _ = _gather_next(state, None)
        outs, _ = _gather_next(state, None)
        vec = _unpack_vectors(outs[-1], (N_DEV,))
        self.vec = {k: _full_from_slots(vec[k], k) for k in _VECTORS}
        self.vec["ffn_w_dw"] = [vec["ffn_w_dw"][:, l] for l in range(DEPTH)]
        self.ready = {0: self._unpack(first, outs[:-1])}
        self.chains = {}
        tokens = []
        self._begin("ffn0", ffn0, "gather0f", outs[0], tokens)
        self._begin(1, _layer_matrices(1), "gather1", outs[0], tokens)
        self.vec["norm_g"] = self.vec["norm_g"] + (tokens[0] + tokens[1])

    def _send(self, items):
        return [self.cast[k][j] for k, j in items]

    @staticmethod
    def _unpack(items, outs):
        return {k: o if k in _FFN else _full_from_slots(o[:, None], k, layers=1)[0] for (k, _), o in zip(items, outs)}

    def _begin(self, key, items, name, after, tokens):
        state, token = _gather_begin(self._send(items), [_route(k) for k, _ in items], name, after)
        self.chains[key] = (items, state)
        tokens.append(token)

    def _advance(self, key, after, tokens):
        items, state = self.chains.pop(key)
        state, token = _gather_next(state, after)
        if token is None:
            self.ready.setdefault(0 if key == "ffn0" else key, {}).update(self._unpack(items, state))
        else:
            self.chains[key] = (items, state)
            tokens.append(token)

    def layer(self, i):
        return self.ready[i]

    def hook(self, i, point, after):
        tokens = []
        if i == 0 and point == 0:
            self._advance("ffn0", after, tokens)
        if i == 0 and point == 1:
            self._advance("ffn0", after, tokens)
            self._advance("ffn0", None, tokens)
        if point >= 1 and i + 1 in self.chains:
            self._advance(i + 1, after, tokens)
        if point == 1 and i + 2 < DEPTH:
            self._begin(i + 2, _layer_matrices(i + 2), f"gather{i + 2}", after, tokens)
        return functools.reduce(lambda a, b: a + b, tokens) if tokens else None


def _layer_slots(g, name):
    shape, ax, _ = _WEIGHTS[name]
    shape, ax = shape[1:], ax - 1
    split = shape[:ax] + (N_DEV, shape[ax] // N_DEV) + shape[ax + 1:]
    return jnp.moveaxis(g.reshape(split), ax, 0).astype(BF16)


class _GradSink:
    def __init__(self):
        self.state = None
        self.ffn_state = None
        self.sums = {}
        self.last = None

    def ffn_done(self, i, grads):
        if i != 0:
            return None
        self.ffn_items = _layer_matrices(0)[-2:]
        self.ffn_state, token = _scatter_begin([grads[k] for k, _ in self.ffn_items],
                                               [_route(k) for k, _ in self.ffn_items],
                                               [f"{k}{j}" for k, j in self.ffn_items], "scatter0f")
        return token

    def layer_done(self, i, grads):
        items = _layer_matrices(i)
        if i == 0:
            items = items[:-2]
            self.last = (items, [_layer_slots(grads[k], k) for k, _ in items])
            return None
        slots = [grads[k] if k in _FFN else _layer_slots(grads[k], k) for k, _ in items]
        self.items = items
        self.state, token = _scatter_begin(slots, [_route(k) for k, _ in items], [f"{k}{j}" for k, j in items],
                                           f"scatter{i}", (N_DEV, 3, 2 * D_FF // N_DEV))
        return token

    def hook(self, i, point, after):
        tokens = []
        if self.state is not None and point in (1, 2, 3):
            self.state, token = _scatter_next(self.state, after)
            if point == 3:
                self.sums.update(dict(zip(self.items, self.state)))
                self.state = None
            tokens.append(token)
        if self.ffn_state is not None and point in ("a", "b", 3):
            self.ffn_state, token = _scatter_next(self.ffn_state, after)
            if point == 3:
                self.sums.update(dict(zip(self.ffn_items, self.ffn_state)))
                self.ffn_state = None
            tokens.append(token)
        tokens = [t for t in tokens if t is not None]
        return functools.reduce(lambda a, b: a + b, tokens) if tokens else None


def _adamw(w, g, m, v, name, layer=None, prev=None, dep=None):
    shape = w.shape
    cols = shape[-1]
    view = shape if len(shape) == 3 else (1, math.prod(shape[:-1]), cols)
    layers, rows, _ = view
    tr = _pick(rows, (512, 352, 288, 256, 128, 64, 32, 16, 8))
    n_prev = 0 if prev is None else 3
    lead = ([] if prev is None else [p.reshape(view) for p in prev]) + ([] if dep is None else [dep])

    def body(*refs):
        w_ref, g_ref, m_ref, v_ref = refs[len(lead):len(lead) + 4]
        d_ref, nm_ref, nv_ref = refs[len(lead) + 4:]
        gv = g_ref[...]
        nm = ADAM_B1 * m_ref[...] + (1.0 - ADAM_B1) * gv
        nv = ADAM_B2 * v_ref[...] + (1.0 - ADAM_B2) * jnp.square(gv)
        m_hat = nm / (1.0 - ADAM_B1 ** ADAM_STEP)
        v_hat = nv / (1.0 - ADAM_B2 ** ADAM_STEP)
        d_ref[...] = -ADAM_LR * (m_hat / (jnp.sqrt(v_hat) + ADAM_EPS) + ADAM_WD * w_ref[...])
        nm_ref[...] = nm
        nv_ref[...] = nv

    if layer is None:
        grid = (layers, rows // tr)
        blk = gblk = pl.BlockSpec((None, tr, cols), lambda l, i: (l, i, 0))
        gview = view
    else:
        grid = (rows // tr,)
        blk = pl.BlockSpec((None, tr, cols), lambda i: (layer, i, 0))
        gblk = pl.BlockSpec((tr, cols), lambda i: (i, 0))
        gview = (rows, cols)
    shp = jax.ShapeDtypeStruct(view, F32)
    outs = pl.pallas_call(
        body, name=name, grid=grid, in_specs=[_ANY] * len(lead) + [blk, gblk, blk, blk], out_specs=[blk] * 3,
        out_shape=[shp] * 3, input_output_aliases={i: i for i in range(n_prev)},
        compiler_params=_params(("parallel",) * len(grid)),
    )(*lead, w.reshape(view), g.reshape(gview), m.reshape(view), v.reshape(view))
    return [o.reshape(shape) for o in outs]


def kernel(x, norm_g, attn_w_qkv, attn_w_o, conv_w_in, conv_w_dw, conv_w_out, pool_w_in, pool_w_grp, pool_scale, pool_w_out, ffn_w_up, ffn_w_dw, ffn_w_down, loss_target, m_norm_g, m_attn_w_qkv, m_attn_w_o, m_conv_w_in, m_conv_w_dw, m_conv_w_out, m_pool_w_in, m_pool_w_grp, m_pool_scale, m_pool_w_out, m_ffn_w_up, m_ffn_w_dw, m_ffn_w_down, v_norm_g, v_attn_w_qkv, v_attn_w_o, v_conv_w_in, v_conv_w_dw, v_conv_w_out, v_pool_w_in, v_pool_w_grp, v_pool_scale, v_pool_w_out, v_ffn_w_up, v_ffn_w_dw, v_ffn_w_down):
    shards = dict(zip(_NAMES, (norm_g, attn_w_qkv, attn_w_o, conv_w_in, conv_w_dw, conv_w_out, pool_w_in,
                               pool_w_grp, pool_scale, pool_w_out, ffn_w_up, ffn_w_dw, ffn_w_down)))
    moms = dict(zip(_NAMES, (m_norm_g, m_attn_w_qkv, m_attn_w_o, m_conv_w_in, m_conv_w_dw, m_conv_w_out,
                             m_pool_w_in, m_pool_w_grp, m_pool_scale, m_pool_w_out, m_ffn_w_up, m_ffn_w_dw,
                             m_ffn_w_down)))
    vels = dict(zip(_NAMES, (v_norm_g, v_attn_w_qkv, v_attn_w_o, v_conv_w_in, v_conv_w_dw, v_conv_w_out,
                             v_pool_w_in, v_pool_w_grp, v_pool_scale, v_pool_w_out, v_ffn_w_up, v_ffn_w_dw,
                             v_ffn_w_down)))
    weights = _LayerWeights(shards)
    sink = _GradSink()
    loss, grad_x, vec_grads = _local_step(x[0], loss_target[0], weights.vec, weights, sink)
    loss = lax.psum(loss[0, 0], _AXES)

    items, slots = sink.last
    vec_slots = {k: _slots_from_full(vec_grads[k], k) for k in _VECTORS if k != "ffn_w_dw"}
    vec_slots["ffn_w_dw"] = jnp.stack(vec_grads["ffn_w_dw"], axis=1)
    state, token = _scatter_begin(slots + [_pack_vectors(vec_slots, (N_DEV,)).astype(BF16)],
                                  [_route(k) for k, _ in items] + [ROUTE_B],
                                  [f"{k}{j}" for k, j in items] + ["vectors"], "scatter0")
    results = {}

    flipped = {k for k in _MATRICES if _WEIGHTS[k][0][0] > 1 and _shard_shape(k)[-1] % LANES}
    wmv = {k: [t.transpose(0, 2, 1) if k in flipped else t for t in (shards[k], moms[k], vels[k])] for k in _MATRICES}

    def step(matrices, dep=None):
        outs = []
        for k, j in matrices:
            g = sink.sums[(k, j)]
            w, m, v = wmv[k]
            if _WEIGHTS[k][0][0] == 1:
                results[k] = (g[None], _adamw(w, g[None], m, v, f"adamw_{k}", dep=dep))
            else:
                gs, prev = results.get(k, ({}, None))
                gs[j] = g
                results[k] = (gs, _adamw(w, g.T if k in flipped else g, m, v, f"adamw_{k}{j}", layer=j, prev=prev, dep=dep))
            outs.append(results[k][1][0])
        return outs

    state, token = _scatter_next(state, step(_layer_matrices(3), token))
    state, token = _scatter_next(state, step(_layer_matrices(2) + _layer_matrices(1), token))
    sums, _ = _scatter_next(state, step(_layer_matrices(0)[-2:], token))
    sink.sums.update(dict(zip(items, sums[:-1])))
    step(items)
    vec_sums = _unpack_vectors(sums[-1], ())
    for k in _VECTORS:
        results[k] = (vec_sums[k], _adamw(shards[k], vec_sums[k], moms[k], vels[k], f"adamw_{k}"))
    grads_out = {k: g if not isinstance(g, dict) else jnp.stack([g[j] for j in range(len(g))])
                 for k, (g, _) in results.items()}
    stepped = {k: [o.transpose(0, 2, 1) if k in flipped else o for o in outs] for k, (_, outs) in results.items()}
    return (loss, grad_x[None], *[grads_out[k] for k in _NAMES], *[stepped[k][0] for k in _NAMES],
            *[stepped[k][1] for k in _NAMES], *[stepped[k][2] for k in _NAMES])
```

```python
import functools
import math

import numpy as np
import jax
import jax.numpy as jnp
from jax import lax
from jax.experimental import pallas as pl
from jax.experimental.pallas import tpu as pltpu

F32, BF16 = jnp.float32, jnp.bfloat16

D_MODEL = 1024
SEQ = 2048
DEPTH = 4
DILATED_CFG = ((128, 1), (512, 4), (2048, 16))
N_GROUPS_A = 3
HEADS = 8
HEAD_DIM = 64
ATTN_WIDTH = HEADS * HEAD_DIM
N_HEADS_A = N_GROUPS_A * HEADS
BLOCK = 128
NEG_INF = -1e30
POOL_GROUP_DIM = 256
D_FF = 2816
RMS_EPS = 1e-6
ADAM_LR, ADAM_B1, ADAM_B2, ADAM_EPS, ADAM_WD, ADAM_STEP = 0.001, 0.9, 0.999, 1e-08, 0.01, 10

N_DEV = 8
LANES = 128
V7X_VMEM_BYTES = 64 * 2 ** 20
VMEM_LIMIT_BYTES = V7X_VMEM_BYTES - 8 * 2 ** 20
COL_TILE = 256
ROW_TILE = 1024
MATMUL_TILES = (2048, 1024, 1408, 512, 256, 128)
TN_RESIDENT_K = 2048

NN = (((1,), (0,)), ((), ()))
NT = (((1,), (1,)), ((), ()))
TN = (((0,), (0,)), ((), ()))


def _dot(a, b, dims=NN):
    return lax.dot_general(a, b, dims, preferred_element_type=F32)


def _params(sem=None):
    return pltpu.CompilerParams(dimension_semantics=sem, vmem_limit_bytes=VMEM_LIMIT_BYTES)


def _pick(n, prefs):
    for p in prefs:
        if n % p == 0:
            return p
    return n


def _matmul(a, b, mode, out_dtype, name, a_parts=1, b_parts=1):
    if mode == "nn":
        m, k = a.shape[-2], a.shape[-1] * a_parts
        n = b.shape[-1] * b_parts
    elif mode == "nt":
        m, k = a.shape[-2], a.shape[-1] * a_parts
        n = b.shape[-2]
    else:
        k, m = a.shape[-2], a.shape[-1] * a_parts
        n = b.shape[-1] * b_parts
    tm = _pick(m, MATMUL_TILES)
    tn = _pick(n // b_parts if mode != "nt" else n, MATMUL_TILES)
    kk = k // a_parts if mode != "tn" else k
    tk = _pick(kk, MATMUL_TILES)
    if mode == "tn":
        tm = _pick(m // a_parts, MATMUL_TILES)
        if k <= TN_RESIDENT_K:
            tk = k
    gm, gn, gk = m // tm, n // tn, k // tk

    def a_idx(i, j, kq):
        if mode == "tn":
            r, c, per = kq, i, (m // a_parts) // tm
        else:
            r, c, per = i, kq, (k // a_parts) // tk
        return (r, c) if a_parts == 1 else (c // per, r, c % per)

    def b_idx(i, j, kq):
        if mode == "nt":
            return (j, kq)
        per = (n // b_parts) // tn
        return (kq, j) if b_parts == 1 else (j // per, kq, j % per)

    a_blk = (tk, tm) if mode == "tn" else (tm, tk)
    b_blk = (tn, tk) if mode == "nt" else (tk, tn)
    if a_parts > 1:
        a_blk = (None,) + a_blk
    if b_parts > 1:
        b_blk = (None,) + b_blk
    dims = {"nn": NN, "nt": NT, "tn": TN}[mode]

    def body_single(a_ref, b_ref, o_ref):
        o_ref[...] = _dot(a_ref[...], b_ref[...], dims).astype(o_ref.dtype)

    def body(a_ref, b_ref, o_ref, acc_ref):
        kq = pl.program_id(2)

        @pl.when(kq == 0)
        def _():
            acc_ref[...] = jnp.zeros_like(acc_ref)

        acc_ref[...] += _dot(a_ref[...], b_ref[...], dims)

        @pl.when(kq == gk - 1)
        def _():
            o_ref[...] = acc_ref[...].astype(o_ref.dtype)

    return pl.pallas_call(
        body_single if gk == 1 else body, name=name, grid=(gm, gn, gk),
        in_specs=[pl.BlockSpec(a_blk, a_idx), pl.BlockSpec(b_blk, b_idx)],
        out_specs=pl.BlockSpec((tm, tn), lambda i, j, kq: (i, j)),
        out_shape=jax.ShapeDtypeStruct((m, n), out_dtype),
        scratch_shapes=[] if gk == 1 else [pltpu.VMEM((tm, tn), F32)],
        compiler_params=_params(("parallel", "parallel", "arbitrary")),
    )(a, b)


def _rms_fwd(xin, g, res, out_dtype, name):
    s, d = xin.shape
    has_res = res is not None

    def body(*refs):
        x_ref, g_ref = refs[0], refs[1]
        o_ref = refs[-1]
        x = x_ref[...]
        r = lax.rsqrt(jnp.mean(x * x, axis=-1, keepdims=True) + RMS_EPS)
        y = x * r * g_ref[...]
        if has_res:
            y = refs[2][...] + y
        o_ref[...] = y.astype(o_ref.dtype)

    row = pl.BlockSpec((ROW_TILE, d), lambda i: (i, 0))
    vec = pl.BlockSpec((1, d), lambda i: (0, 0))
    ins = [xin, g] + ([res] if has_res else [])
    return pl.pallas_call(
        body, name=name, grid=(s // ROW_TILE,),
        in_specs=[row, vec] + ([row] if has_res else []),
        out_specs=row, out_shape=jax.ShapeDtypeStruct((s, d), out_dtype),
        compiler_params=_params(("parallel",)),
    )(*ins)


def _rms_bwd(xin, g, dy, dres, out_dtype, name):
    s, d = xin.shape
    has_res = dres is not None

    def body(*refs):
        x_ref, g_ref, dy_ref = refs[0], refs[1], refs[2]
        dx_ref, dg_ref = refs[-2], refs[-1]

        @pl.when(pl.program_id(0) == 0)
        def _():
            dg_ref[...] = jnp.zeros_like(dg_ref)

        x = x_ref[...]
        dyv = dy_ref[...].astype(F32)
        r = lax.rsqrt(jnp.mean(x * x, axis=-1, keepdims=True) + RMS_EPS)
        xhat = x * r
        u = dyv * g_ref[...]
        dx = r * (u - xhat * jnp.mean(u * xhat, axis=-1, keepdims=True))
        if has_res:
            dx = refs[3][...] + dx
        dx_ref[...] = dx.astype(dx_ref.dtype)
        dg_ref[...] += jnp.sum(dyv * xhat, axis=0, keepdims=True)

    row = pl.BlockSpec((ROW_TILE, d), lambda i: (i, 0))
    vec = pl.BlockSpec((1, d), lambda i: (0, 0))
    ins = [xin, g, dy] + ([dres] if has_res else [])
    return pl.pallas_call(
        body, name=name, grid=(s // ROW_TILE,),
        in_specs=[row, vec, row] + ([row] if has_res else []),
        out_specs=[row, vec],
        out_shape=[jax.ShapeDtypeStruct((s, d), out_dtype), jax.ShapeDtypeStruct((1, d), F32)],
        compiler_params=_params(("arbitrary",)),
    )(*ins)


def _rms(x):
    r = lax.rsqrt(jnp.mean(x * x, axis=-1, keepdims=True) + RMS_EPS)
    return r, x * r


def _rms_grad(r, xhat, dy, g):
    u = dy * g
    return r * (u - xhat * jnp.mean(u * xhat, axis=-1, keepdims=True))


def _rms_res_pre(sub, g_post, res, g_pre, name):
    s, d = sub.shape

    def body(sub_ref, gp_ref, res_ref, gn_ref, x_ref, n_ref):
        xnew = res_ref[...] + _rms(sub_ref[...])[1] * gp_ref[...]
        x_ref[...] = xnew
        n_ref[...] = (_rms(xnew)[1] * gn_ref[...]).astype(BF16)

    row = pl.BlockSpec((ROW_TILE, d), lambda i: (i, 0))
    vec = pl.BlockSpec((1, d), lambda i: (0, 0))
    return pl.pallas_call(
        body, name=name, grid=(s // ROW_TILE,),
        in_specs=[row, vec, row, vec], out_specs=[row, row],
        out_shape=[jax.ShapeDtypeStruct((s, d), F32), jax.ShapeDtypeStruct((s, d), BF16)],
        compiler_params=_params(("parallel",)),
    )(sub, g_post, res, g_pre)


def _rms_bwd_pair(xmid, g_pre, dn, dres, sub, g_post, name):
    s, d = xmid.shape

    def body(x_ref, gn_ref, dn_ref, dres_ref, sub_ref, gp_ref, dx_ref, dsub_ref, dgn_ref, dgp_ref):
        @pl.when(pl.program_id(0) == 0)
        def _():
            dgn_ref[...] = jnp.zeros_like(dgn_ref)
            dgp_ref[...] = jnp.zeros_like(dgp_ref)

        dnv = dn_ref[...].astype(F32)
        r, xhat = _rms(x_ref[...])
        dx = dres_ref[...] + _rms_grad(r, xhat, dnv, gn_ref[...])
        dx_ref[...] = dx
        dgn_ref[...] += jnp.sum(dnv * xhat, axis=0, keepdims=True)
        rs, shat = _rms(sub_ref[...])
        dsub_ref[...] = _rms_grad(rs, shat, dx, gp_ref[...]).astype(BF16)
        dgp_ref[...] += jnp.sum(dx * shat, axis=0, keepdims=True)

    row = pl.BlockSpec((ROW_TILE, d), lambda i: (i, 0))
    vec = pl.BlockSpec((1, d), lambda i: (0, 0))
    return pl.pallas_call(
        body, name=name, grid=(s // ROW_TILE,),
        in_specs=[row, vec, row, row, row, vec], out_specs=[row, row, vec, vec],
        out_shape=[jax.ShapeDtypeStruct((s, d), F32), jax.ShapeDtypeStruct((s, d), BF16),
                   jax.ShapeDtypeStruct((1, d), F32), jax.ShapeDtypeStruct((1, d), F32)],
        compiler_params=_params(("arbitrary",)),
    )(xmid, g_pre, dn, dres, sub, g_post)


def _loss_head(y, tgt, name):
    s, d = y.shape

    def body(y_ref, t_ref, l_ref, dy_ref):
        @pl.when(pl.program_id(0) == 0)
        def _():
            l_ref[...] = jnp.zeros_like(l_ref)

        e = y_ref[...] - t_ref[...]
        dy_ref[...] = e / d
        per_tok = jnp.mean(e * e, axis=-1, keepdims=True)
        l_ref[...] += 0.5 * jnp.sum(per_tok, axis=0, keepdims=True)

    row = pl.BlockSpec((ROW_TILE, d), lambda i: (i, 0))
    return pl.pallas_call(
        body, name=name, grid=(s // ROW_TILE,),
        in_specs=[row, row],
        out_specs=[pl.BlockSpec((1, 1), lambda i: (0, 0)), row],
        out_shape=[jax.ShapeDtypeStruct((1, 1), F32), jax.ShapeDtypeStruct((s, d), F32)],
        compiler_params=_params(("arbitrary",)),
    )(y, tgt)


SUBLANES = 8


def _shift_down(x, k):
    t, c = x.shape
    r = pltpu.roll(x.reshape(t // SUBLANES, SUBLANES, c), k, axis=1)
    above = jnp.concatenate([jnp.zeros((1, SUBLANES, c), x.dtype), r[:-1]], axis=0)
    rows = lax.broadcasted_iota(jnp.int32, (1, SUBLANES, c), 1)
    return jnp.where(rows >= k, r, above).reshape(t, c)


def _shift_up(x, k):
    t, c = x.shape
    r = pltpu.roll(x.reshape(t // SUBLANES, SUBLANES, c), SUBLANES - k, axis=1)
    below = jnp.concatenate([r[1:], jnp.zeros((1, SUBLANES, c), x.dtype)], axis=0)
    rows = lax.broadcasted_iota(jnp.int32, (1, SUBLANES, c), 1)
    return jnp.where(rows < SUBLANES - k, r, below).reshape(t, c)


def _conv3(h, w):
    return w[2:3] * h + w[1:2] * _shift_down(h, 1) + w[0:1] * _shift_down(h, 2)


def _conv3_bwd(dc, h, w, dw_ref, cols=slice(None)):
    u1, u2 = _shift_up(dc, 1), _shift_up(dc, 2)
    dw_ref[0:1, cols] = jnp.sum(u2 * h, axis=0, keepdims=True)
    dw_ref[1:2, cols] = jnp.sum(u1 * h, axis=0, keepdims=True)
    dw_ref[2:3, cols] = jnp.sum(dc * h, axis=0, keepdims=True)
    return w[2:3] * dc + w[1:2] * u1 + w[0:1] * u2


FFN_PAIRS = N_DEV // 2


def _lane_chunks(width):
    return [(c0, min(COL_TILE, width - c0)) for c0 in range(0, width, COL_TILE)]


def _ffn_up(n, wup, wdw, name):
    s, d = n.shape
    cw = wup.shape[-1]

    def body(n_ref, wg_ref, wu_ref, dg_ref, du_ref, h_ref, c_ref, a_ref):
        x = n_ref[...]
        for c0, size in _lane_chunks(cw):
            cols = slice(c0, c0 + size)
            hg = _dot(x, wg_ref[:, cols])
            hu = _dot(x, wu_ref[:, cols])
            h_ref[0, :, cols] = hg.astype(BF16)
            h_ref[1, :, cols] = hu.astype(BF16)
            cg = _conv3(hg, dg_ref[:, cols])
            cu = _conv3(hu, du_ref[:, cols])
            c_ref[0, :, cols] = cg.astype(BF16)
            c_ref[1, :, cols] = cu.astype(BF16)
            a_ref[:, cols] = (cg * jax.nn.sigmoid(cg) * cu).astype(BF16)

    return pl.pallas_call(
        body, name=name, grid=(FFN_PAIRS,),
        in_specs=[pl.BlockSpec((s, d), lambda j: (0, 0)),
                  pl.BlockSpec((None, d, cw), lambda j: (j, 0, 0)),
                  pl.BlockSpec((None, d, cw), lambda j: (j + FFN_PAIRS, 0, 0)),
                  pl.BlockSpec((None, 3, cw), lambda j: (j, 0, 0)),
                  pl.BlockSpec((None, 3, cw), lambda j: (j + FFN_PAIRS, 0, 0))],
        out_specs=[pl.BlockSpec((None, 2, s, cw), lambda j: (j, 0, 0, 0)),
                   pl.BlockSpec((None, 2, s, cw), lambda j: (j, 0, 0, 0)),
                   pl.BlockSpec((None, s, cw), lambda j: (j, 0, 0))],
        out_shape=[jax.ShapeDtypeStruct((FFN_PAIRS, 2, s, cw), BF16), jax.ShapeDtypeStruct((FFN_PAIRS, 2, s, cw), BF16),
                   jax.ShapeDtypeStruct((FFN_PAIRS, s, cw), BF16)],
        compiler_params=_params(("parallel",)),
    )(n, wup, wup, wdw, wdw)


def _ffn_mid_bwd(do, wdown, h, c, wdw, name):
    s, d = do.shape
    cw = wdown.shape[1]

    def body(do_ref, wd_ref, h_ref, c_ref, wg_ref, wu_ref, dh_ref, dwg_ref, dwu_ref):
        dov = do_ref[...]
        for c0, size in _lane_chunks(cw):
            cols = slice(c0, c0 + size)
            da = _dot(dov, wd_ref[cols, :], NT)
            hg = h_ref[0, :, cols].astype(F32)
            hu = h_ref[1, :, cols].astype(F32)
            wg, wu = wg_ref[:, cols], wu_ref[:, cols]
            cg = c_ref[0, :, cols].astype(F32)
            cu = c_ref[1, :, cols].astype(F32)
            sg = jax.nn.sigmoid(cg)
            dcu = da * (cg * sg)
            dcg = da * cu * (sg * (1.0 + cg * (1.0 - sg)))
            dh_ref[0, :, cols] = _conv3_bwd(dcg, hg, wg, dwg_ref, cols).astype(BF16)
            dh_ref[1, :, cols] = _conv3_bwd(dcu, hu, wu, dwu_ref, cols).astype(BF16)

    vec = jax.ShapeDtypeStruct((FFN_PAIRS, 3, cw), F32)
    return pl.pallas_call(
        body, name=name, grid=(FFN_PAIRS,),
        in_specs=[pl.BlockSpec((s, d), lambda j: (0, 0)), pl.BlockSpec((None, cw, d), lambda j: (j, 0, 0)),
                  pl.BlockSpec((None, 2, s, cw), lambda j: (j, 0, 0, 0)),
                  pl.BlockSpec((None, 2, s, cw), lambda j: (j, 0, 0, 0)),
                  pl.BlockSpec((None, 3, cw), lambda j: (j, 0, 0)),
                  pl.BlockSpec((None, 3, cw), lambda j: (j + FFN_PAIRS, 0, 0))],
        out_specs=[pl.BlockSpec((None, 2, s, cw), lambda j: (j, 0, 0, 0)),
                   pl.BlockSpec((None, 3, cw), lambda j: (j, 0, 0)), pl.BlockSpec((None, 3, cw), lambda j: (j, 0, 0))],
        out_shape=[jax.ShapeDtypeStruct((FFN_PAIRS, 2, s, cw), BF16), vec, vec],
        compiler_params=_params(("parallel",)),
    )(do, wdown, h, c, wdw, wdw)


def _ffn_dwup(n, dh, name):
    s, d = n.shape
    cw = dh.shape[-1]

    def body(n_ref, dh_ref, o_ref):
        o_ref[...] = _dot(n_ref[...], dh_ref[...], TN).astype(BF16)

    return pl.pallas_call(
        body, name=name, grid=(N_DEV,),
        in_specs=[pl.BlockSpec((s, d), lambda k: (0, 0)),
                  pl.BlockSpec((None, None, s, cw), lambda k: (k % FFN_PAIRS, k // FFN_PAIRS, 0, 0))],
        out_specs=pl.BlockSpec((None, d, cw), lambda k: (k, 0, 0)),
        out_shape=jax.ShapeDtypeStruct((N_DEV, d, cw), BF16),
        compiler_params=_params(("parallel",)),
    )(n, dh)


def _ffn_dn(dh, wup, name, dep=None):
    s, cw = dh.shape[-2:]
    d = wup.shape[1]
    tm = _pick(s, MATMUL_TILES)
    deps = [] if dep is None else [dep]

    def body(dh_ref, w_ref, *rest):
        o_ref, acc_ref = rest[-2:]
        k = pl.program_id(1)

        @pl.when(k == 0)
        def _():
            acc_ref[...] = jnp.zeros_like(acc_ref)

        acc_ref[...] += _dot(dh_ref[...], w_ref[...], NT)

        @pl.when(k == N_DEV - 1)
        def _():
            o_ref[...] = acc_ref[...]

    return pl.pallas_call(
        body, name=name, grid=(s // tm, N_DEV),
        in_specs=[pl.BlockSpec((None, None, tm, cw), lambda i, k: (k % FFN_PAIRS, k // FFN_PAIRS, i, 0)),
                  pl.BlockSpec((None, d, cw), lambda i, k: (k, 0, 0))] + [_ANY] * len(deps),
        out_specs=pl.BlockSpec((tm, d), lambda i, k: (i, 0)),
        out_shape=jax.ShapeDtypeStruct((s, d), F32),
        scratch_shapes=[pltpu.VMEM((tm, d), F32)],
        compiler_params=_params(("parallel", "arbitrary")),
    )(dh, wup, *deps)


def _sconv_fwd(n, win, wdw, name):
    s, d = n.shape
    tn = COL_TILE
    nj = d // tn

    def body(n_ref, wb_ref, wc_ref, wh_ref, dw_ref, z_ref, y_ref):
        x = n_ref[...]
        zb = _dot(x, wb_ref[...])
        zc = _dot(x, wc_ref[...])
        zh = _dot(x, wh_ref[...])
        z_ref[0] = zb.astype(BF16)
        z_ref[1] = zc.astype(BF16)
        z_ref[2] = zh.astype(BF16)
        y_ref[...] = (zb * _conv3(zc * zh, dw_ref[...])).astype(BF16)

    return pl.pallas_call(
        body, name=name, grid=(nj,),
        in_specs=[pl.BlockSpec((s, d), lambda j: (0, 0)),
                  pl.BlockSpec((d, tn), lambda j: (0, j)), pl.BlockSpec((d, tn), lambda j: (0, j + nj)),
                  pl.BlockSpec((d, tn), lambda j: (0, j + 2 * nj)), pl.BlockSpec((3, tn), lambda j: (0, j))],
        out_specs=[pl.BlockSpec((3, s, tn), lambda j: (0, 0, j)), pl.BlockSpec((s, tn), lambda j: (0, j))],
        out_shape=[jax.ShapeDtypeStruct((3, s, d), BF16), jax.ShapeDtypeStruct((s, d), BF16)],
        compiler_params=_params(("parallel",)),
    )(n, win, win, win, wdw)


def _sconv_mid_bwd(dm, wout, z, wdw, name):
    s, d = dm.shape
    tn = COL_TILE
    nj = d // tn

    def body(dm_ref, wo_ref, z_ref, w_ref, dz_ref, dw_ref):
        dy = _dot(dm_ref[...], wo_ref[...], NT)
        zb = z_ref[0].astype(F32)
        zc = z_ref[1].astype(F32)
        zh = z_ref[2].astype(F32)
        w = w_ref[...]
        p = zc * zh
        cp = _conv3(p, w)
        dz_ref[0] = (dy * cp).astype(BF16)
        dcp = dy * zb
        dp = _conv3_bwd(dcp, p, w, dw_ref)
        dz_ref[1] = (dp * zh).astype(BF16)
        dz_ref[2] = (dp * zc).astype(BF16)

    return pl.pallas_call(
        body, name=name, grid=(nj,),
        in_specs=[pl.BlockSpec((s, d), lambda j: (0, 0)), pl.BlockSpec((tn, d), lambda j: (j, 0)),
                  pl.BlockSpec((3, s, tn), lambda j: (0, 0, j)), pl.BlockSpec((3, tn), lambda j: (0, j))],
        out_specs=[pl.BlockSpec((3, s, tn), lambda j: (0, 0, j)), pl.BlockSpec((3, tn), lambda j: (0, j))],
        out_shape=[jax.ShapeDtypeStruct((3, s, d), BF16), jax.ShapeDtypeStruct((3, d), F32)],
        compiler_params=_params(("parallel",)),
    )(dm, wout, z, wdw)


def _pool_select(g, c2, c4, c8, c16):
    return jnp.where(g == 0, c2, jnp.where(g == 1, c4, jnp.where(g == 2, c8, c16)))


def _pool_inv_count(g, shape):
    pos = lax.broadcasted_iota(jnp.int32, shape, 0).astype(F32) + 1.0
    win = (2 << g).astype(F32)
    return jnp.minimum(pos, win)


def _pool_fwd(n, win, wgrp, scale, name):
    s, d = n.shape
    tn = POOL_GROUP_DIM

    def body(n_ref, wi_ref, wg_ref, sc_ref, p_ref, y_ref):
        g = pl.program_id(0)
        u = _dot(n_ref[...], wi_ref[...])
        s2 = u + _shift_down(u, 1)
        s4 = s2 + _shift_down(s2, 2)
        s8 = s4 + _shift_down(s4, 4)
        s16 = s8 + _shift_down(s8, 8)
        tot = _pool_select(g, s2, s4, s8, s16)
        p = (tot / _pool_inv_count(g, u.shape) - u).astype(BF16)
        p_ref[...] = p
        y_ref[...] = (_dot(p, wg_ref[...]) * sc_ref[...]).astype(BF16)

    return pl.pallas_call(
        body, name=name, grid=(d // tn,),
        in_specs=[pl.BlockSpec((s, d), lambda g: (0, 0)), pl.BlockSpec((d, tn), lambda g: (0, g)),
                  pl.BlockSpec((None, tn, tn), lambda g: (g, 0, 0)), pl.BlockSpec((1, tn), lambda g: (0, g))],
        out_specs=[pl.BlockSpec((s, tn), lambda g: (0, g)), pl.BlockSpec((s, tn), lambda g: (0, g))],
        out_shape=[jax.ShapeDtypeStruct((s, d), BF16), jax.ShapeDtypeStruct((s, d), BF16)],
        compiler_params=_params(("parallel",)),
    )(n, win, wgrp, scale)


def _pool_mid_bwd(dm, wout, p, wgrp, scale, name):
    s, d = dm.shape
    tn = POOL_GROUP_DIM

    def body(dm_ref, wo_ref, p_ref, wg_ref, sc_ref, du_ref, dwg_ref, dsc_ref):
        g = pl.program_id(0)
        dy = _dot(dm_ref[...], wo_ref[...], NT)
        pv = p_ref[...]
        wg = wg_ref[...]
        ypre = _dot(pv, wg)
        dsc_ref[...] = jnp.sum(dy * ypre, axis=0, keepdims=True)
        dypre = (dy * sc_ref[...]).astype(BF16)
        dwg_ref[...] = _dot(pv, dypre, TN)
        dp = _dot(dypre, wg, NT)
        e = dp / _pool_inv_count(g, dp.shape)
        f2 = e + _shift_up(e, 1)
        f4 = f2 + _shift_up(f2, 2)
        f8 = f4 + _shift_up(f4, 4)
        f16 = f8 + _shift_up(f8, 8)
        du_ref[...] = (_pool_select(g, f2, f4, f8, f16) - dp).astype(BF16)

    return pl.pallas_call(
        body, name=name, grid=(d // tn,),
        in_specs=[pl.BlockSpec((s, d), lambda g: (0, 0)), pl.BlockSpec((tn, d), lambda g: (g, 0)),
                  pl.BlockSpec((s, tn), lambda g: (0, g)), pl.BlockSpec((None, tn, tn), lambda g: (g, 0, 0)),
                  pl.BlockSpec((1, tn), lambda g: (0, g))],
        out_specs=[pl.BlockSpec((s, tn), lambda g: (0, g)), pl.BlockSpec((None, tn, tn), lambda g: (g, 0, 0)),
                   pl.BlockSpec((1, tn), lambda g: (0, g))],
        out_shape=[jax.ShapeDtypeStruct((s, d), BF16), jax.ShapeDtypeStruct((4, tn, tn), F32),
                   jax.ShapeDtypeStruct((1, d), F32)],
        compiler_params=_params(("parallel",)),
    )(dm, wout, p, wgrp, scale)


PANEL = LANES
ATTN_EXT = ATTN_WIDTH + PANEL
DVEC_LANE = HEADS


def _alibi_slopes(g, dil):
    all_slopes = 2.0 ** (-8.0 * np.arange(1, N_HEADS_A + 1) / N_HEADS_A)
    return [float(np.float32(sl) * np.float32(dil)) for sl in all_slopes[g * HEADS:(g + 1) * HEADS]]


def _residue_order(a, dil, name):
    s, w = a.shape
    per = ROW_TILE // dil
    panels = w // PANEL

    def body(a_ref, o_ref, *tiles):
        for c in range(panels):
            cols = slice(c * PANEL, (c + 1) * PANEL)
            tiles[c][...] = a_ref[:, cols].astype(F32)
            for r in range(dil):
                o_ref[r, :, cols] = tiles[c][pl.ds(r, per, stride=dil), :].astype(o_ref.dtype)

    out = pl.pallas_call(
        body, name=name, grid=(s // ROW_TILE,),
        in_specs=[pl.BlockSpec((ROW_TILE, w), lambda i: (i, 0))],
        out_specs=pl.BlockSpec((dil, per, w), lambda i: (0, i, 0)),
        out_shape=jax.ShapeDtypeStruct((dil, s // dil, w), a.dtype),
        scratch_shapes=[pltpu.VMEM((ROW_TILE, PANEL), F32)] * panels,
        compiler_params=_params(("parallel",)),
    )(a)
    return out.reshape(s, w)


def _token_order(a, dil, acc, name):
    s, w = a.shape
    per = ROW_TILE // dil
    panels = w // PANEL
    has_acc = acc is not None

    def body(*refs):
        a_ref = refs[0]
        o_ref = refs[2] if has_acc else refs[1]
        tiles = refs[3:] if has_acc else refs[2:]
        for c in range(panels):
            cols = slice(c * PANEL, (c + 1) * PANEL)
            for r in range(dil):
                tiles[c][pl.ds(r, per, stride=dil), :] = a_ref[r, :, cols]
            v = tiles[c][...]
            if has_acc:
                v = v + refs[1][:, cols]
            o_ref[:, cols] = v

    row = pl.BlockSpec((ROW_TILE, w), lambda i: (i, 0))
    return pl.pallas_call(
        body, name=name, grid=(s // ROW_TILE,),
        in_specs=[pl.BlockSpec((dil, per, w), lambda i: (0, i, 0))] + ([row] if has_acc else []),
        out_specs=row, out_shape=jax.ShapeDtypeStruct((s, w), F32),
        scratch_shapes=[pltpu.VMEM((ROW_TILE, PANEL), F32)] * panels,
        compiler_params=_params(("parallel",)),
    )(*([a.reshape(dil, s // dil, w)] + ([acc] if has_acc else [])))


def _qkv_proj(n, wqkv, g, name):
    s, d = n.shape
    tm = _pick(s, MATMUL_TILES)

    def body(a_ref, b_ref, o_ref):
        o_ref[...] = _dot(a_ref[...], b_ref[...]).astype(BF16)

    return pl.pallas_call(
        body, name=name, grid=(s // tm, 3),
        in_specs=[pl.BlockSpec((tm, d), lambda i, t: (i, 0)),
                  pl.BlockSpec((d, ATTN_WIDTH), lambda i, t: (0, 3 * g + t))],
        out_specs=pl.BlockSpec((None, tm, ATTN_WIDTH), lambda i, t: (t, i, 0)),
        out_shape=jax.ShapeDtypeStruct((3, s, ATTN_WIDTH), BF16),
        compiler_params=_params(("parallel", "parallel")),
    )(n, wqkv)


def _attn_window(n, ln):
    if ln == BLOCK:
        return 0, BLOCK
    return pl.multiple_of(jnp.maximum(n - 1, 0) * BLOCK, BLOCK), 2 * BLOCK


def _attn_mask(n, k0, kw):
    qpos = n * BLOCK + lax.broadcasted_iota(jnp.int32, (BLOCK, kw), 0)
    kpos = k0 + lax.broadcasted_iota(jnp.int32, (BLOCK, kw), 1)
    dist = qpos - kpos
    return dist.astype(F32), (dist >= 0) & (dist <= BLOCK)


def _attn_scores(q, keys, slope, dist, valid):
    s = _dot(q, keys, NT) * (HEAD_DIM ** -0.5) - slope * dist
    return jnp.where(valid, s, NEG_INF)


ATTN_STEP_BLOCKS = 1
ATTN_BWD_STEP_BLOCKS = 4


def _attn_block(gb, ln):
    nb = ln // BLOCK
    n, base = (0, gb * ln) if nb == 1 else (gb % nb, (gb // nb) * ln)
    k0, kw = _attn_window(n, ln)
    cur = pl.ds(pl.multiple_of(gb * BLOCK, BLOCK), BLOCK)
    win = pl.ds(pl.multiple_of(base + k0, BLOCK), kw)
    return cur, win, n, k0, kw


def _attn_fwd(qkv, g, name):
    _, s, w = qkv.shape
    dil = DILATED_CFG[g][1]
    ln = s // dil
    slopes = _alibi_slopes(g, dil)
    rows = ATTN_STEP_BLOCKS * BLOCK

    def body(qkv_ref, o_ref):
        o_ref[:, w:] = jnp.zeros((rows, PANEL), F32)
        for b in range(ATTN_STEP_BLOCKS):
            cur, win, n, k0, kw = _attn_block(pl.program_id(0) * ATTN_STEP_BLOCKS + b, ln)
            dist, valid = _attn_mask(n, k0, kw)
            out = slice(b * BLOCK, (b + 1) * BLOCK)
            for h in range(HEADS):
                cols = slice(h * HEAD_DIM, (h + 1) * HEAD_DIM)
                sc = _attn_scores(qkv_ref[0, cur, cols], qkv_ref[1, win, cols], slopes[h], dist, valid)
                m = jnp.max(sc, axis=-1, keepdims=True)
                p = jnp.exp(sc - m)
                den = jnp.sum(p, axis=-1, keepdims=True)
                o_ref[out, cols] = _dot(p.astype(BF16), qkv_ref[2, win, cols]) / den
                o_ref[out, w + h:w + h + 1] = m + jnp.log(den)

    return pl.pallas_call(
        body, name=name, grid=(s // rows,),
        in_specs=[pl.BlockSpec((3, s, w), lambda i: (0, 0, 0))],
        out_specs=pl.BlockSpec((rows, ATTN_EXT), lambda i: (i, 0)),
        out_shape=jax.ShapeDtypeStruct((s, ATTN_EXT), F32),
        compiler_params=_params(("parallel",)),
    )(qkv)


def _attn_bwd(qkv, dext, g, name, dep=None):
    _, s, w = qkv.shape
    dil = DILATED_CFG[g][1]
    ln = s // dil
    slopes = _alibi_slopes(g, dil)
    scale = HEAD_DIM ** -0.5
    rows = ATTN_BWD_STEP_BLOCKS * BLOCK
    steps = s // rows
    deps = [] if dep is None else [dep]

    def body(qkv_ref, de_ref, *rest):
        d_ref, dk_ref, dv_ref = rest[-3:]

        @pl.when(pl.program_id(0) == 0)
        def _():
            dk_ref[...] = jnp.zeros_like(dk_ref)
            dv_ref[...] = jnp.zeros_like(dv_ref)

        for b in range(ATTN_BWD_STEP_BLOCKS):
            cur, win, n, k0, kw = _attn_block(pl.program_id(0) * ATTN_BWD_STEP_BLOCKS + b, ln)
            dist, valid = _attn_mask(n, k0, kw)
            blk = slice(b * BLOCK, (b + 1) * BLOCK)
            for h in range(HEADS):
                cols = slice(h * HEAD_DIM, (h + 1) * HEAD_DIM)
                q, keys = qkv_ref[0, cur, cols], qkv_ref[1, win, cols]
                dob = de_ref[blk, cols].astype(BF16)
                p = jnp.exp(_attn_scores(q, keys, slopes[h], dist, valid) - de_ref[blk, w + h:w + h + 1])
                dd = de_ref[blk, w + DVEC_LANE + h:w + DVEC_LANE + h + 1]
                ds = (p * (_dot(dob, qkv_ref[2, win, cols], NT) - dd)).astype(BF16)
                d_ref[0, cur, cols] = (scale * _dot(ds, keys)).astype(BF16)
                dv_ref[win, cols] += _dot(p.astype(BF16), dob, TN)
                dk_ref[win, cols] += scale * _dot(ds, q, TN)

        @pl.when(pl.program_id(0) == steps - 1)
        def _():
            d_ref[1] = dk_ref[...].astype(BF16)
            d_ref[2] = dv_ref[...].astype(BF16)

    whole = pl.BlockSpec((3, s, w), lambda i: (0, 0, 0))
    return pl.pallas_call(
        body, name=name, grid=(steps,),
        in_specs=[whole, pl.BlockSpec((rows, ATTN_EXT), lambda i: (i, 0))] + [_ANY] * len(deps),
        out_specs=whole, out_shape=jax.ShapeDtypeStruct((3, s, w), BF16),
        scratch_shapes=[pltpu.VMEM((s, w), F32), pltpu.VMEM((s, w), F32)],
        compiler_params=_params(("arbitrary",)),
    )(qkv, dext, *deps)


def _attn_merge(e0, e1, e2, name):
    s = e0.shape[0]
    w = ATTN_WIDTH

    def body(e0_ref, e1_ref, e2_ref, m_ref, mb_ref, lse_ref):
        refs = (e0_ref, e1_ref, e2_ref)
        l = [r[:, w:w + HEADS] for r in refs]
        mx = jnp.maximum(jnp.maximum(l[0], l[1]), l[2])
        e = [jnp.exp(v - mx) for v in l]
        z = e[0] + e[1] + e[2]
        lse_ref[...] = mx + jnp.log(z)
        wts = [v / z for v in e]
        for h in range(HEADS):
            cols = slice(h * HEAD_DIM, (h + 1) * HEAD_DIM)
            acc = wts[0][:, h:h + 1] * refs[0][:, cols]
            for g in range(1, N_GROUPS_A):
                acc = acc + wts[g][:, h:h + 1] * refs[g][:, cols]
            m_ref[:, cols] = acc
            mb_ref[:, cols] = acc.astype(BF16)

    ext = pl.BlockSpec((ROW_TILE, ATTN_EXT), lambda i: (i, 0))
    row = pl.BlockSpec((ROW_TILE, w), lambda i: (i, 0))
    return pl.pallas_call(
        body, name=name, grid=(s // ROW_TILE,),
        in_specs=[ext, ext, ext],
        out_specs=[row, row, pl.BlockSpec((ROW_TILE, HEADS), lambda i: (i, 0))],
        out_shape=[jax.ShapeDtypeStruct((s, w), F32), jax.ShapeDtypeStruct((s, w), BF16),
                   jax.ShapeDtypeStruct((s, HEADS), F32)],
        compiler_params=_params(("parallel",)),
    )(e0, e1, e2)


def _attn_dvec(dmerged, merged, lse_all, name, dep=None):
    s, w = merged.shape
    deps = [] if dep is None else [dep]

    def body(dm_ref, m_ref, lse_ref, *rest):
        de_ref = rest[-1]
        dmv = dm_ref[...]
        de_ref[:, :w] = dmv
        de_ref[:, w:] = jnp.zeros((ROW_TILE, PANEL), F32)
        de_ref[:, w:w + HEADS] = lse_ref[...]
        prod = dmv * m_ref[...]
        for h in range(HEADS):
            lane = w + DVEC_LANE + h
            de_ref[:, lane:lane + 1] = jnp.sum(prod[:, h * HEAD_DIM:(h + 1) * HEAD_DIM], axis=-1, keepdims=True)

    row = pl.BlockSpec((ROW_TILE, w), lambda i: (i, 0))
    return pl.pallas_call(
        body, name=name, grid=(s // ROW_TILE,),
        in_specs=[row, row, pl.BlockSpec((ROW_TILE, HEADS), lambda i: (i, 0))] + [_ANY] * len(deps),
        out_specs=pl.BlockSpec((ROW_TILE, ATTN_EXT), lambda i: (i, 0)),
        out_shape=jax.ShapeDtypeStruct((s, ATTN_EXT), F32),
        compiler_params=_params(("parallel",)),
    )(dmerged, merged, lse_all, *deps)


def _attention_fwd(n, wqkv, wo, tag):
    ns, qkvs, exts = [], [], []
    for g, (_, dil) in enumerate(DILATED_CFG):
        ng = n if dil == 1 else _residue_order(n, dil, f"{tag}_order_g{g}")
        qkv = _qkv_proj(ng, wqkv, g, f"{tag}_qkv_g{g}")
        ext = _attn_fwd(qkv, g, f"{tag}_fwd_g{g}")
        ns.append(ng)
        qkvs.append(qkv)
        exts.append(ext if dil == 1 else _token_order(ext, dil, None, f"{tag}_unorder_g{g}"))
    merged, merged_bf, lse_all = _attn_merge(*exts, f"{tag}_merge")
    m = _matmul(merged_bf, wo, "nn", F32, f"{tag}_wo")
    return m, (ns, qkvs, merged, merged_bf, lse_all)


def _attention_bwd(dm, wqkv, wo, saved, tag, dep=None, hook=None):
    ns, qkvs, merged, merged_bf, lse_all = saved
    d_wo = _matmul(merged_bf, dm, "tn", BF16, f"{tag}_dwo")
    dmerged = _matmul(dm, wo, "nt", F32, f"{tag}_dmerged")
    dext = _attn_dvec(dmerged, merged, lse_all, f"{tag}_dvec", dep)
    width = 3 * ATTN_WIDTH
    d_wqkv, dn, dep = [], None, None
    for g, (_, dil) in enumerate(DILATED_CFG):
        dext_g = dext if dil == 1 else _residue_order(dext, dil, f"{tag}_dorder_g{g}")
        dqkv = _attn_bwd(qkvs[g], dext_g, g, f"{tag}_bwd_g{g}", dep)
        dep = hook(g, dqkv) if hook is not None and g + 1 < N_GROUPS_A else None
        d_wqkv.append(_matmul(ns[g], dqkv, "tn", BF16, f"{tag}_dwqkv_g{g}", b_parts=3))
        dn_g = _matmul(dqkv, wqkv[:, g * width:(g + 1) * width], "nt", F32, f"{tag}_dn_g{g}", a_parts=3)
        dn = dn_g if dil == 1 else _token_order(dn_g, dil, dn, f"{tag}_dn_sum_g{g}")
    return dn, jnp.concatenate(d_wqkv, axis=1), d_wo


def _layer_matrices(i):
    mixer = (("attn_w_qkv", "attn_w_o"), ("conv_w_in", "conv_w_out"), ("pool_w_in", "pool_w_grp", "pool_w_out"))[i % 3]
    return [(k, i // 3) for k in mixer] + [("ffn_w_up", i), ("ffn_w_down", i)]


def _local_step(x, tgt, vec, weights, sink):
    ng = vec["norm_g"]

    def gain(i, j, token=None):
        g = ng[i, j][None, :]
        return g if token is None else g + token

    saved = []
    n = _rms_fwd(x, gain(0, 0), None, BF16, "norm_first")
    for i in range(DEPTH):
        wl = weights.layer(i)
        t0 = weights.hook(i, 0, n)
        kind, idx = i % 3, i // 3
        if kind == 0:
            m, ms = _attention_fwd(n, wl["attn_w_qkv"], wl["attn_w_o"], "attn")
        elif kind == 1:
            taps = vec["conv_w_dw"][idx] if t0 is None else vec["conv_w_dw"][idx] + t0
            z, y = _sconv_fwd(n, wl["conv_w_in"], taps, "sconv_fwd")
            m = _matmul(y, wl["conv_w_out"], "nn", F32, "sconv_out")
            ms = (z, y)
        else:
            scale = vec["pool_scale"][idx][None, :] if t0 is None else vec["pool_scale"][idx][None, :] + t0
            p, y = _pool_fwd(n, wl["pool_w_in"], wl["pool_w_grp"], scale, "pool_fwd")
            m = _matmul(y, wl["pool_w_out"], "nn", F32, "pool_out")
            ms = (p, y)
        t1 = weights.hook(i, 1, m)
        x1, n2 = _rms_res_pre(m, gain(i, 1, t0), x, gain(i, 2, t1), "norm_res_pre")
        h, c, a = _ffn_up(n2, wl["ffn_w_up"], vec["ffn_w_dw"][i], "ffn_up")
        t2 = weights.hook(i, 2, a)
        f = _matmul(a, wl["ffn_w_down"].reshape(D_FF, D_MODEL), "nn", F32, "ffn_down", a_parts=FFN_PAIRS)
        saved.append((x, n, m, ms, x1, n2, h, a, f, wl, c))
        if i + 1 < DEPTH:
            x, n = _rms_res_pre(f, gain(i, 3, t2), x1, gain(i + 1, 0), "norm_res_pre")
        else:
            x = _rms_fwd(f, gain(i, 3), x1, F32, "norm_res")
        weights.hook(i, 3, x)

    loss, dx = _loss_head(x, tgt, "loss_head")

    g_norm = [[None] * 4 for _ in range(DEPTH)]
    g_taps, g_scale, g_ffn_dw = [], [], [None] * DEPTH
    df, g_norm[DEPTH - 1][3] = _rms_bwd(saved[-1][8], gain(DEPTH - 1, 3), dx, None, BF16, "norm_bwd_sub")
    t0 = None
    for i in reversed(range(DEPTH)):
        xin, n, m, ms, x1, n2, h, a, f, wl, c = saved[i]
        kind, idx = i % 3, i // 3
        gl = {}
        d_wdown = _matmul(a, df, "tn", BF16, "ffn_dwdown", a_parts=FFN_PAIRS)
        gl["ffn_w_down"] = d_wdown.reshape(N_DEV, D_FF // N_DEV, D_MODEL)
        ffn_taps = vec["ffn_w_dw"][i] if t0 is None else vec["ffn_w_dw"][i] + t0
        dh, dwg, dwu = _ffn_mid_bwd(df, wl["ffn_w_down"].reshape(FFN_PAIRS, -1, D_MODEL), h, c, ffn_taps, "ffn_mid_bwd")
        g_ffn_dw[i] = jnp.concatenate([dwg, dwu], axis=0)
        t1 = sink.hook(i, 1, dh)
        gl["ffn_w_up"] = _ffn_dwup(n2, dh, "ffn_dwup")
        tf = sink.ffn_done(i, gl)
        dn2 = _ffn_dn(dh, wl["ffn_w_up"], "ffn_dn", t1)
        dx1, dm, g_norm[i][2], g_norm[i][1] = _rms_bwd_pair(x1, gain(i, 2, tf), dn2, dx, m, gain(i, 1), "norm_bwd_pair")
        t2 = sink.hook(i, 2, dm)
        if kind == 0:
            dn, gl["attn_w_qkv"], gl["attn_w_o"] = _attention_bwd(
                dm, wl["attn_w_qkv"], wl["attn_w_o"], ms, "attn", t2, lambda g, after, i=i: sink.hook(i, ("a", "b")[g], after))
        elif kind == 1:
            z, y = ms
            gl["conv_w_out"] = _matmul(y, dm, "tn", BF16, "sconv_dwout")
            taps = vec["conv_w_dw"][idx] if t2 is None else vec["conv_w_dw"][idx] + t2
            dz, ddw = _sconv_mid_bwd(dm, wl["conv_w_out"], z, taps, "sconv_mid_bwd")
            g_taps.append(ddw)
            gl["conv_w_in"] = _matmul(n, dz, "tn", BF16, "sconv_dwin", b_parts=3)
            dn = _matmul(dz, wl["conv_w_in"], "nt", F32, "sconv_dn", a_parts=3)
        else:
            p, y = ms
            gl["pool_w_out"] = _matmul(y, dm, "tn", BF16, "pool_dwout")
            scale = vec["pool_scale"][idx][None, :] if t2 is None else vec["pool_scale"][idx][None, :] + t2
            du, gl["pool_w_grp"], dscale = _pool_mid_bwd(dm, wl["pool_w_out"], p, wl["pool_w_grp"], scale, "pool_mid_bwd")
            g_scale.append(dscale[0])
            gl["pool_w_in"] = _matmul(n, du, "tn", BF16, "pool_dwin")
            dn = _matmul(du, wl["pool_w_in"], "nt", F32, "pool_dn")
        sink.hook(i, 3, dn)
        if i > 0:
            dx, df, g_norm[i][0], g_norm[i - 1][3] = _rms_bwd_pair(xin, gain(i, 0, t2), dn, dx1, saved[i - 1][8],
                                                                   gain(i - 1, 3), "norm_bwd_pair")
        else:
            dx, g_norm[0][0] = _rms_bwd(xin, gain(0, 0), dn, dx1, F32, "norm_bwd_res")
        t0 = sink.layer_done(i, gl)

    vec_grads = {"norm_g": jnp.stack([jnp.concatenate(row, axis=0) for row in g_norm]), "conv_w_dw": jnp.stack(g_taps),
                 "pool_scale": jnp.stack(g_scale), "ffn_w_dw": g_ffn_dw}
    return loss, dx, vec_grads


_AXES = ("x", "y", "c")
ROUTE_A = ("y", "x", "c")
ROUTE_B = ("x", "y", "c")
def _dev_index(pos):
    return 4 * pos["x"] + 2 * pos["y"] + pos["c"]


_HBM = pl.BlockSpec(memory_space=pltpu.HBM)
_SEM = pl.BlockSpec(memory_space=pltpu.SEMAPHORE)
_ANY = pl.BlockSpec(memory_space=pl.ANY)
_EFFECT = pltpu.SideEffectType.DATAFLOW_SIDE_EFFECTING


TOKEN_SHAPE = (1, D_MODEL)


def _copies_start(describe, arrays, n_copies, name, after, token_shape=TOKEN_SHAPE):
    n = len(arrays)
    deps = [] if after is None else [after]

    def body(*refs):
        send_sems, recv_sems = refs[n + len(deps)], refs[n + len(deps) + 1]
        for c in describe(refs[:n], send_sems, recv_sems):
            c.start()
        refs[-1][...] = jnp.zeros_like(refs[-1])

    outs = pl.pallas_call(
        body, name=f"{name}_start",
        out_shape=(pltpu.SemaphoreType.DMA((n_copies,)), pltpu.SemaphoreType.DMA((n_copies,)),
                   *[pltpu.HBM(a.shape, a.dtype) for a in arrays], jax.ShapeDtypeStruct(token_shape, F32)),
        in_specs=[_HBM] * n + [_ANY] * len(deps),
        out_specs=(_SEM, _SEM, *([_HBM] * n), pl.BlockSpec(memory_space=pltpu.VMEM)),
        input_output_aliases={i: 2 + i for i in range(n)},
        compiler_params=pltpu.CompilerParams(has_side_effects=_EFFECT),
    )(*[pltpu.with_memory_space_constraint(a, pltpu.HBM) for a in arrays], *deps)
    return (outs[0], outs[1], list(outs[2:2 + n])), outs[-1]


def _copies_wait(describe, handle, name, after):
    send_sems, recv_sems, arrays = handle
    n = len(arrays)
    deps = [] if after is None else list(after) if isinstance(after, (list, tuple)) else [after]

    def body(*refs):
        for c in describe(refs[:n], refs[n], refs[n + 1]):
            c.wait_send()
            c.wait_recv()

    outs = pl.pallas_call(
        body, name=f"{name}_wait",
        out_shape=tuple(pltpu.HBM(a.shape, a.dtype) for a in arrays),
        in_specs=[_HBM] * n + [_SEM, _SEM] + [_ANY] * len(deps), out_specs=tuple([_HBM] * n),
        input_output_aliases={i: i for i in range(n)},
        compiler_params=pltpu.CompilerParams(has_side_effects=_EFFECT),
    )(*arrays, send_sems, recv_sems, *deps)
    return list(outs)


GATHER_STAGE_COPIES = (3, 3, 1)


def _gather_copies(stage, routes):
    n = len(routes)

    def describe(refs, send_sems, recv_sems):
        pos = {a: lax.axis_index(a) for a in _AXES}

        def flipped(axes):
            return {a: 1 - pos[a] if a in axes else pos[a] for a in _AXES}

        copies = []
        for i, (a1, a2, a3) in enumerate(routes):
            land = refs[n + i] if stage == 1 else refs[i]
            p1, p2, p12, p3 = flipped((a1,)), flipped((a2,)), flipped((a1, a2)), flipped((a3,))
            plan = {1: [(None, p1), (None, p2), (None, p3)], 2: [(p1, p2), (p1, p3), (p2, p3)], 3: [(p12, p3)]}[stage]
            for holder, to in plan:
                slot = land.at[_dev_index(pos if holder is None else holder)]
                k = len(copies)
                copies.append(pltpu.make_async_remote_copy(
                    src_ref=refs[i] if holder is None else slot, dst_ref=slot,
                    send_sem=send_sems.at[k], recv_sem=recv_sems.at[k],
                    device_id=tuple(to[a] for a in _AXES), device_id_type=pl.DeviceIdType.MESH))
        return copies

    return describe


def _gather_begin(shards, routes, name, after):
    n = len(shards)
    lands = [lax.empty((N_DEV,) + a.shape, a.dtype) for a in shards]
    handle, token = _copies_start(_gather_copies(1, routes), list(shards) + lands, GATHER_STAGE_COPIES[0] * n,
                                  f"{name}_1", after)
    return {"stage": 1, "handle": handle, "routes": routes, "name": name, "n": n}, token


def _gather_next(state, after):
    stage, routes, name, n = state["stage"], state["routes"], state["name"], state["n"]
    arrays = _copies_wait(_gather_copies(stage, routes), state["handle"], f"{name}_{stage}", after)
    if stage == 1:
        state = dict(state, shards=arrays[:n])
        arrays = arrays[n:]
    if stage == 3:
        me = _dev_index({a: lax.axis_index(a) for a in _AXES})
        return [lax.dynamic_update_index_in_dim(o, s, me, 0) for o, s in zip(arrays, state["shards"])], None
    handle, token = _copies_start(_gather_copies(stage + 1, routes), arrays, GATHER_STAGE_COPIES[stage] * n,
                                  f"{name}_{stage + 1}", None)
    return dict(state, stage=stage + 1, handle=handle), token


ADD_ROW_TILES = (1024, 704, 512, 352, 256, 128, 96, 64, 32, 16)


def _add_half(a, recv, me, out_dtype, name):
    p, q, cols = recv.shape
    tr = _pick(q, ADD_ROW_TILES)

    def body(me_ref, a_ref, b_ref, o_ref):
        o_ref[...] = (a_ref[...].astype(F32) + b_ref[...].astype(F32)).astype(o_ref.dtype)

    return pl.pallas_call(
        body, name=name,
        grid_spec=pltpu.PrefetchScalarGridSpec(
            num_scalar_prefetch=1, grid=(p, q // tr),
            in_specs=[pl.BlockSpec((None, None, tr, cols), lambda j, i, m: (j, m[0], i, 0)),
                      pl.BlockSpec((None, tr, cols), lambda j, i, m: (j, i, 0))],
            out_specs=pl.BlockSpec((None, tr, cols), lambda j, i, m: (j, i, 0))),
        out_shape=jax.ShapeDtypeStruct((p, q, cols), out_dtype),
        compiler_params=_params(("parallel", "parallel")),
    )(me, a, recv)


def _half_copies(axes):
    n = len(axes)

    def describe(refs, send_sems, recv_sems):
        pos = {a: lax.axis_index(a) for a in _AXES}
        copies = []
        for i, axis in enumerate(axes):
            peer = tuple(1 - pos[a] if a == axis else pos[a] for a in _AXES)
            copies.append(pltpu.make_async_remote_copy(
                src_ref=refs[i].at[:, 1 - pos[axis]], dst_ref=refs[n + i], send_sem=send_sems.at[i],
                recv_sem=recv_sems.at[i], device_id=peer, device_id_type=pl.DeviceIdType.MESH))
        return copies

    return describe


def _scatter_begin(slots, routes, tags, name, token_shape=TOKEN_SHAPE):
    shapes = [a.shape[1:] for a in slots]
    rows = [math.prod(s[:-1]) for s in shapes]
    arrays = [a.reshape(4, 2, n, s[-1]) for a, n, s in zip(slots, rows, shapes)]
    return _scatter_start({"stage": 0, "arrays": arrays, "routes": routes, "tags": tags, "name": name,
                           "shapes": shapes, "rows": rows}, token_shape)


def _scatter_start(state, token_shape=TOKEN_SHAPE):
    stage, arrays = state["stage"], state["arrays"]
    axes = [r[2 - stage] for r in state["routes"]]
    lands = [lax.empty((a.shape[0],) + a.shape[2:], a.dtype) for a in arrays]
    handle, token = _copies_start(_half_copies(axes), arrays + lands, len(arrays), f"{state['name']}_{stage + 1}", None,
                                  token_shape)
    return dict(state, handle=handle, axes=axes), token


def _scatter_next(state, after):
    stage, axes, n = state["stage"], state["axes"], len(state["arrays"])
    both = _copies_wait(_half_copies(axes), state["handle"], f"{state['name']}_{stage + 1}", after)
    coord = {a: lax.axis_index(a).astype(jnp.int32).reshape(1) for a in _AXES}
    sums = [_add_half(a, r, coord[ax], F32 if stage == 2 else BF16, f"scatter_add_{stage + 1}_{t}")
            for a, r, ax, t in zip(both[:n], both[n:], axes, state["tags"])]
    if stage == 2:
        return [a.reshape(s) for a, s in zip(sums, state["shapes"])], None
    if stage == 0:
        views = [(1, 2, 2 * r, s[-1]) if route[1] == "x" else (2, 2, r, s[-1])
                 for r, s, route in zip(state["rows"], state["shapes"], state["routes"])]
    else:
        views = [(1, 2, r, s[-1]) for r, s in zip(state["rows"], state["shapes"])]
    return _scatter_start(dict(state, stage=stage + 1, arrays=[a.reshape(v) for a, v in zip(sums, views)]))


_WEIGHTS = {
    "norm_g": ((DEPTH, 4, D_MODEL), 2, True),
    "attn_w_qkv": ((2, D_MODEL, 4608), 2, False),
    "attn_w_o": ((2, ATTN_WIDTH, D_MODEL), 2, False),
    "conv_w_in": ((1, D_MODEL, 3 * D_MODEL), 2, False),
    "conv_w_dw": ((1, 3, D_MODEL), 2, True),
    "conv_w_out": ((1, D_MODEL, D_MODEL), 1, False),
    "pool_w_in": ((1, D_MODEL, D_MODEL), 1, False),
    "pool_w_grp": ((1, 4, POOL_GROUP_DIM, POOL_GROUP_DIM), 2, False),
    "pool_scale": ((1, D_MODEL), 1, True),
    "pool_w_out": ((1, D_MODEL, D_MODEL), 1, False),
    "ffn_w_up": ((DEPTH, D_MODEL, 2 * D_FF), 2, False),
    "ffn_w_dw": ((DEPTH, 3, 2 * D_FF), 2, True),
    "ffn_w_down": ((DEPTH, D_FF, D_MODEL), 1, False),
}
_NAMES = tuple(_WEIGHTS)
_VECTORS = tuple(k for k in _NAMES if _WEIGHTS[k][2])
_MATRICES = tuple(k for k in _NAMES if not _WEIGHTS[k][2])
_FFN = ("ffn_w_up", "ffn_w_down")
_ON_ROUTE_A = ("ffn_w_up", "attn_w_o", "conv_w_out", "pool_w_in")
PACK_ROWS = 16


def _route(name):
    return ROUTE_A if name in _ON_ROUTE_A else ROUTE_B


def _shard_shape(name):
    shape, ax, _ = _WEIGHTS[name]
    return tuple(s // N_DEV if i == ax else s for i, s in enumerate(shape))


def _full_from_slots(slots, name, layers=None):
    shape, ax, _ = _WEIGHTS[name]
    if layers is not None:
        shape = (layers,) + shape[1:]
    return jnp.moveaxis(slots, 0, ax).reshape(shape)


def _slots_from_full(full, name):
    shape, ax, _ = _WEIGHTS[name]
    split = shape[:ax] + (N_DEV, shape[ax] // N_DEV) + shape[ax + 1:]
    return jnp.moveaxis(full.reshape(split), ax, 0)


def _pack_vectors(parts, lead):
    rows = []
    for k in _VECTORS:
        r = parts[k].reshape(lead + (-1, LANES))
        pad = -r.shape[-2] % PACK_ROWS
        rows.append(jnp.pad(r, [(0, 0)] * len(lead) + [(0, pad), (0, 0)]))
    return jnp.concatenate(rows, axis=len(lead))


def _unpack_vectors(buf, lead):
    out, r0 = {}, 0
    for k in _VECTORS:
        shard = _shard_shape(k)
        rows = math.prod(shard) // LANES
        out[k] = buf[..., r0:r0 + rows, :].reshape(lead + shard)
        r0 += rows + (-rows % PACK_ROWS)
    return out


class _LayerWeights:
    def __init__(self, shards):
        first, ffn0 = _layer_matrices(0)[:-2], _layer_matrices(0)[-2:]
        state, token = _gather_begin([shards[k][j].astype(BF16) for k, j in first] + [_pack_vectors(shards, ())],
                                     [_route(k) for k, _ in first] + [ROUTE_B], "gather0", None)
        self.cast = {k: (shards[k] + token[0, 0]).astype(BF16) for k in _MATRICES}
        state, _ = _gather_next(state, self._send(ffn0) + self._send(_layer_matrices(1)))
        state, _ = _gather_next(state, None)
        outs, _ = _gather_next(state, None)
        vec = _unpack_vectors(outs[-1], (N_DEV,))
        self.vec = {k: _full_from_slots(vec[k], k) for k in _VECTORS}
        self.vec["ffn_w_dw"] = [vec["ffn_w_dw"][:, l] for l in range(DEPTH)]
        self.ready = {0: self._unpack(first, outs[:-1])}
        self.chains = {}
        tokens = []
        self._begin("ffn0", ffn0, "gather0f", outs[0], tokens)
        self._begin(1, _layer_matrices(1), "gather1", outs[0], tokens)
        self.vec["norm_g"] = self.vec["norm_g"] + (tokens[0] + tokens[1])

    def _send(self, items):
        return [self.cast[k][j] for k, j in items]

    @staticmethod
    def _unpack(items, outs):
        return {k: o if k in _FFN else _full_from_slots(o[:, None], k, layers=1)[0] for (k, _), o in zip(items, outs)}

    def _begin(self, key, items, name, after, tokens):
        state, token = _gather_begin(self._send(items), [_route(k) for k, _ in items], name, after)
        self.chains[key] = (items, state)
        tokens.append(token)

    def _advance(self, key, after, tokens):
        items, state = self.chains.pop(key)
        state, token = _gather_next(state, after)
        if token is None:
            self.ready.setdefault(0 if key == "ffn0" else key, {}).update(self._unpack(items, state))
        else:
            self.chains[key] = (items, state)
            tokens.append(token)

    def layer(self, i):
        return self.ready[i]

    def hook(self, i, point, after):
        tokens = []
        if i == 0 and point == 0:
            self._advance("ffn0", after, tokens)
        if i == 0 and point == 1:
            self._advance("ffn0", after, tokens)
            self._advance("ffn0", None, tokens)
        if point >= 1 and i + 1 in self.chains:
            self._advance(i + 1, after, tokens)
        if point == 1 and i + 2 < DEPTH:
            self._begin(i + 2, _layer_matrices(i + 2), f"gather{i + 2}", after, tokens)
        return functools.reduce(lambda a, b: a + b, tokens) if tokens else None


def _layer_slots(g, name):
    shape, ax, _ = _WEIGHTS[name]
    shape, ax = shape[1:], ax - 1
    split = shape[:ax] + (N_DEV, shape[ax] // N_DEV) + shape[ax + 1:]
    return jnp.moveaxis(g.reshape(split), ax, 0).astype(BF16)


class _GradSink:
    def __init__(self):
        self.state = None
        self.ffn_state = None
        self.sums = {}
        self.last = None

    def ffn_done(self, i, grads):
        if i != 0:
            return None
        self.ffn_items = _layer_matrices(0)[-2:]
        self.ffn_state, token = _scatter_begin([grads[k] for k, _ in self.ffn_items],
                                               [_route(k) for k, _ in self.ffn_items],
                                               [f"{k}{j}" for k, j in self.ffn_items], "scatter0f")
        return token

    def layer_done(self, i, grads):
        items = _layer_matrices(i)
        if i == 0:
            items = items[:-2]
            self.last = (items, [_layer_slots(grads[k], k) for k, _ in items])
            return None
        slots = [grads[k] if k in _FFN else _layer_slots(grads[k], k) for k, _ in items]
        self.items = items
        self.state, token = _scatter_begin(slots, [_route(k) for k, _ in items], [f"{k}{j}" for k, j in items],
                                           f"scatter{i}", (N_DEV, 3, 2 * D_FF // N_DEV))
        return token

    def hook(self, i, point, after):
        tokens = []
        if self.state is not None and point in (1, 2, 3):
            self.state, token = _scatter_next(self.state, after)
            if point == 3:
                self.sums.update(dict(zip(self.items, self.state)))
                self.state = None
            tokens.append(token)
        if self.ffn_state is not None and point in ("a", "b", 3):
            self.ffn_state, token = _scatter_next(self.ffn_state, after)
            if point == 3:
                self.sums.update(dict(zip(self.ffn_items, self.ffn_state)))
                self.ffn_state = None
            tokens.append(token)
        tokens = [t for t in tokens if t is not None]
        return functools.reduce(lambda a, b: a + b, tokens) if tokens else None


def _adamw(w, g, m, v, name, layer=None, prev=None, dep=None):
    shape = w.shape
    cols = shape[-1]
    view = shape if len(shape) == 3 else (1, math.prod(shape[:-1]), cols)
    layers, rows, _ = view
    tr = _pick(rows, (704, 576, 512, 352, 256, 128, 64, 32, 16, 8))
    n_prev = 0 if prev is None else 3
    lead = ([] if prev is None else [p.reshape(view) for p in prev]) + ([] if dep is None else [dep])

    def body(*refs):
        w_ref, g_ref, m_ref, v_ref = refs[len(lead):len(lead) + 4]
        d_ref, nm_ref, nv_ref = refs[len(lead) + 4:]
        gv = g_ref[...]
        nm = ADAM_B1 * m_ref[...] + (1.0 - ADAM_B1) * gv
        nv = ADAM_B2 * v_ref[...] + (1.0 - ADAM_B2) * jnp.square(gv)
        m_hat = nm / (1.0 - ADAM_B1 ** ADAM_STEP)
        v_hat = nv / (1.0 - ADAM_B2 ** ADAM_STEP)
        d_ref[...] = -ADAM_LR * (m_hat / (jnp.sqrt(v_hat) + ADAM_EPS) + ADAM_WD * w_ref[...])
        nm_ref[...] = nm
        nv_ref[...] = nv

    if layer is None:
        grid = (layers, rows // tr)
        blk = gblk = pl.BlockSpec((None, tr, cols), lambda l, i: (l, i, 0))
        gview = view
    else:
        grid = (rows // tr,)
        blk = pl.BlockSpec((None, tr, cols), lambda i: (layer, i, 0))
        gblk = pl.BlockSpec((tr, cols), lambda i: (i, 0))
        gview = (rows, cols)
    shp = jax.ShapeDtypeStruct(view, F32)
    outs = pl.pallas_call(
        body, name=name, grid=grid, in_specs=[_ANY] * len(lead) + [blk, gblk, blk, blk], out_specs=[blk] * 3,
        out_shape=[shp] * 3, input_output_aliases={i: i for i in range(n_prev)},
        compiler_params=_params(("parallel",) * len(grid)),
    )(*lead, w.reshape(view), g.reshape(gview), m.reshape(view), v.reshape(view))
    return [o.reshape(shape) for o in outs]


def kernel(x, norm_g, attn_w_qkv, attn_w_o, conv_w_in, conv_w_dw, conv_w_out, pool_w_in, pool_w_grp, pool_scale, pool_w_out, ffn_w_up, ffn_w_dw, ffn_w_down, loss_target, m_norm_g, m_attn_w_qkv, m_attn_w_o, m_conv_w_in, m_conv_w_dw, m_conv_w_out, m_pool_w_in, m_pool_w_grp, m_pool_scale, m_pool_w_out, m_ffn_w_up, m_ffn_w_dw, m_ffn_w_down, v_norm_g, v_attn_w_qkv, v_attn_w_o, v_conv_w_in, v_conv_w_dw, v_conv_w_out, v_pool_w_in, v_pool_w_grp, v_pool_scale, v_pool_w_out, v_ffn_w_up, v_ffn_w_dw, v_ffn_w_down):
    shards = dict(zip(_NAMES, (norm_g, attn_w_qkv, attn_w_o, conv_w_in, conv_w_dw, conv_w_out, pool_w_in,
                               pool_w_grp, pool_scale, pool_w_out, ffn_w_up, ffn_w_dw, ffn_w_down)))
    moms = dict(zip(_NAMES, (m_norm_g, m_attn_w_qkv, m_attn_w_o, m_conv_w_in, m_conv_w_dw, m_conv_w_out,
                             m_pool_w_in, m_pool_w_grp, m_pool_scale, m_pool_w_out, m_ffn_w_up, m_ffn_w_dw,
                             m_ffn_w_down)))
    vels = dict(zip(_NAMES, (v_norm_g, v_attn_w_qkv, v_attn_w_o, v_conv_w_in, v_conv_w_dw, v_conv_w_out,
                             v_pool_w_in, v_pool_w_grp, v_pool_scale, v_pool_w_out, v_ffn_w_up, v_ffn_w_dw,
                             v_ffn_w_down)))
    weights = _LayerWeights(shards)
    sink = _GradSink()
    loss, grad_x, vec_grads = _local_step(x[0], loss_target[0], weights.vec, weights, sink)
    loss = lax.psum(loss[0, 0], _AXES)

    items, slots = sink.last
    vec_slots = {k: _slots_from_full(vec_grads[k], k) for k in _VECTORS if k != "ffn_w_dw"}
    vec_slots["ffn_w_dw"] = jnp.stack(vec_grads["ffn_w_dw"], axis=1)
    state, token = _scatter_begin(slots + [_pack_vectors(vec_slots, (N_DEV,)).astype(BF16)],
                                  [_route(k) for k, _ in items] + [ROUTE_B],
                                  [f"{k}{j}" for k, j in items] + ["vectors"], "scatter0")
    results = {}

    flipped = {k for k in _MATRICES if _WEIGHTS[k][0][0] > 1 and _shard_shape(k)[-1] % LANES}
    wmv = {k: [t.transpose(0, 2, 1) if k in flipped else t for t in (shards[k], moms[k], vels[k])] for k in _MATRICES}

    def step(matrices, dep=None):
        outs = []
        for k, j in matrices:
            g = sink.sums[(k, j)]
            w, m, v = wmv[k]
            if _WEIGHTS[k][0][0] == 1:
                results[k] = (g[None], _adamw(w, g[None], m, v, f"adamw_{k}", dep=dep))
            else:
                gs, prev = results.get(k, ({}, None))
                gs[j] = g
                results[k] = (gs, _adamw(w, g.T if k in flipped else g, m, v, f"adamw_{k}{j}", layer=j, prev=prev, dep=dep))
            outs.append(results[k][1][0])
        return outs

    state, token = _scatter_next(state, step(_layer_matrices(3), token))
    state, token = _scatter_next(state, step(_layer_matrices(2) + _layer_matrices(1), token))
    sums, _ = _scatter_next(state, step(_layer_matrices(0)[-2:], token))
    sink.sums.update(dict(zip(items, sums[:-1])))
    step(items)
    vec_sums = _unpack_vectors(sums[-1], ())
    for k in _VECTORS:
        results[k] = (vec_sums[k], _adamw(shards[k], vec_sums[k], moms[k], vels[k], f"adamw_{k}"))
    grads_out = {k: g if not isinstance(g, dict) else jnp.stack([g[j] for j in range(len(g))])
                 for k, (g, _) in results.items()}
    stepped = {k: [o.transpose(0, 2, 1) if k in flipped else o for o in outs] for k, (_, outs) in results.items()}
    return (loss, grad_x[None], *[grads_out[k] for k in _NAMES], *[stepped[k][0] for k in _NAMES],
            *[stepped[k][1] for k in _NAMES], *[stepped[k][2] for k in _NAMES])
```

```python
import functools
import math

import numpy as np
import jax
import jax.numpy as jnp
from jax import lax
from jax.experimental import pallas as pl
from jax.experimental.pallas import tpu as pltpu

F32, BF16 = jnp.float32, jnp.bfloat16

D_MODEL = 1024
SEQ = 2048
DEPTH = 4
DILATED_CFG = ((128, 1), (512, 4), (2048, 16))
N_GROUPS_A = 3
HEADS = 8
HEAD_DIM = 64
ATTN_WIDTH = HEADS * HEAD_DIM
N_HEADS_A = N_GROUPS_A * HEADS
BLOCK = 128
NEG_INF = -1e30
POOL_GROUP_DIM = 256
D_FF = 2816
RMS_EPS = 1e-6
ADAM_LR, ADAM_B1, ADAM_B2, ADAM_EPS, ADAM_WD, ADAM_STEP = 0.001, 0.9, 0.999, 1e-08, 0.01, 10

N_DEV = 8
LANES = 128
V7X_VMEM_BYTES = 64 * 2 ** 20
VMEM_LIMIT_BYTES = V7X_VMEM_BYTES - 8 * 2 ** 20
COL_TILE = 256
ROW_TILE = 1024
MATMUL_TILES = (1024, 1408, 512, 256, 128)
TN_RESIDENT_K = 2048

NN = (((1,), (0,)), ((), ()))
NT = (((1,), (1,)), ((), ()))
TN = (((0,), (0,)), ((), ()))


def _dot(a, b, dims=NN):
    return lax.dot_general(a, b, dims, preferred_element_type=F32)


def _params(sem=None):
    return pltpu.CompilerParams(dimension_semantics=sem, vmem_limit_bytes=VMEM_LIMIT_BYTES)


def _pick(n, prefs):
    for p in prefs:
        if n % p == 0:
            return p
    return n


def _matmul(a, b, mode, out_dtype, name, a_parts=1, b_parts=1):
    if mode == "nn":
        m, k = a.shape[-2], a.shape[-1] * a_parts
        n = b.shape[-1] * b_parts
    elif mode == "nt":
        m, k = a.shape[-2], a.shape[-1] * a_parts
        n = b.shape[-2]
    else:
        k, m = a.shape[-2], a.shape[-1] * a_parts
        n = b.shape[-1] * b_parts
    tm = _pick(m, MATMUL_TILES)
    tn = _pick(n // b_parts if mode != "nt" else n, MATMUL_TILES)
    kk = k // a_parts if mode != "tn" else k
    tk = _pick(kk, MATMUL_TILES)
    if mode == "tn":
        tm = _pick(m // a_parts, MATMUL_TILES)
        if k <= TN_RESIDENT_K:
            tk = k
    gm, gn, gk = m // tm, n // tn, k // tk

    def a_idx(i, j, kq):
        if mode == "tn":
            r, c, per = kq, i, (m // a_parts) // tm
        else:
            r, c, per = i, kq, (k // a_parts) // tk
        return (r, c) if a_parts == 1 else (c // per, r, c % per)

    def b_idx(i, j, kq):
        if mode == "nt":
            return (j, kq)
        per = (n // b_parts) // tn
        return (kq, j) if b_parts == 1 else (j // per, kq, j % per)

    a_blk = (tk, tm) if mode == "tn" else (tm, tk)
    b_blk = (tn, tk) if mode == "nt" else (tk, tn)
    if a_parts > 1:
        a_blk = (None,) + a_blk
    if b_parts > 1:
        b_blk = (None,) + b_blk
    dims = {"nn": NN, "nt": NT, "tn": TN}[mode]

    def body_single(a_ref, b_ref, o_ref):
        o_ref[...] = _dot(a_ref[...], b_ref[...], dims).astype(o_ref.dtype)

    def body(a_ref, b_ref, o_ref, acc_ref):
        kq = pl.program_id(2)

        @pl.when(kq == 0)
        def _():
            acc_ref[...] = jnp.zeros_like(acc_ref)

        acc_ref[...] += _dot(a_ref[...], b_ref[...], dims)

        @pl.when(kq == gk - 1)
        def _():
            o_ref[...] = acc_ref[...].astype(o_ref.dtype)

    return pl.pallas_call(
        body_single if gk == 1 else body, name=name, grid=(gm, gn, gk),
        in_specs=[pl.BlockSpec(a_blk, a_idx), pl.BlockSpec(b_blk, b_idx)],
        out_specs=pl.BlockSpec((tm, tn), lambda i, j, kq: (i, j)),
        out_shape=jax.ShapeDtypeStruct((m, n), out_dtype),
        scratch_shapes=[] if gk == 1 else [pltpu.VMEM((tm, tn), F32)],
        compiler_params=_params(("parallel", "parallel", "arbitrary")),
    )(a, b)


def _rms_fwd(xin, g, res, out_dtype, name):
    s, d = xin.shape
    has_res = res is not None

    def body(*refs):
        x_ref, g_ref = refs[0], refs[1]
        o_ref = refs[-1]
        x = x_ref[...]
        r = lax.rsqrt(jnp.mean(x * x, axis=-1, keepdims=True) + RMS_EPS)
        y = x * r * g_ref[...]
        if has_res:
            y = refs[2][...] + y
        o_ref[...] = y.astype(o_ref.dtype)

    row = pl.BlockSpec((ROW_TILE, d), lambda i: (i, 0))
    vec = pl.BlockSpec((1, d), lambda i: (0, 0))
    ins = [xin, g] + ([res] if has_res else [])
    return pl.pallas_call(
        body, name=name, grid=(s // ROW_TILE,),
        in_specs=[row, vec] + ([row] if has_res else []),
        out_specs=row, out_shape=jax.ShapeDtypeStruct((s, d), out_dtype),
        compiler_params=_params(("parallel",)),
    )(*ins)


def _rms_bwd(xin, g, dy, dres, out_dtype, name):
    s, d = xin.shape
    has_res = dres is not None

    def body(*refs):
        x_ref, g_ref, dy_ref = refs[0], refs[1], refs[2]
        dx_ref, dg_ref = refs[-2], refs[-1]

        @pl.when(pl.program_id(0) == 0)
        def _():
            dg_ref[...] = jnp.zeros_like(dg_ref)

        x = x_ref[...]
        dyv = dy_ref[...].astype(F32)
        r = lax.rsqrt(jnp.mean(x * x, axis=-1, keepdims=True) + RMS_EPS)
        xhat = x * r
        u = dyv * g_ref[...]
        dx = r * (u - xhat * jnp.mean(u * xhat, axis=-1, keepdims=True))
        if has_res:
            dx = refs[3][...] + dx
        dx_ref[...] = dx.astype(dx_ref.dtype)
        dg_ref[...] += jnp.sum(dyv * xhat, axis=0, keepdims=True)

    row = pl.BlockSpec((ROW_TILE, d), lambda i: (i, 0))
    vec = pl.BlockSpec((1, d), lambda i: (0, 0))
    ins = [xin, g, dy] + ([dres] if has_res else [])
    return pl.pallas_call(
        body, name=name, grid=(s // ROW_TILE,),
        in_specs=[row, vec, row] + ([row] if has_res else []),
        out_specs=[row, vec],
        out_shape=[jax.ShapeDtypeStruct((s, d), out_dtype), jax.ShapeDtypeStruct((1, d), F32)],
        compiler_params=_params(("arbitrary",)),
    )(*ins)


def _rms(x):
    r = lax.rsqrt(jnp.mean(x * x, axis=-1, keepdims=True) + RMS_EPS)
    return r, x * r


def _rms_grad(r, xhat, dy, g):
    u = dy * g
    return r * (u - xhat * jnp.mean(u * xhat, axis=-1, keepdims=True))


def _rms_res_pre(sub, g_post, res, g_pre, name):
    s, d = sub.shape

    def body(sub_ref, gp_ref, res_ref, gn_ref, x_ref, n_ref):
        xnew = res_ref[...] + _rms(sub_ref[...])[1] * gp_ref[...]
        x_ref[...] = xnew
        n_ref[...] = (_rms(xnew)[1] * gn_ref[...]).astype(BF16)

    row = pl.BlockSpec((ROW_TILE, d), lambda i: (i, 0))
    vec = pl.BlockSpec((1, d), lambda i: (0, 0))
    return pl.pallas_call(
        body, name=name, grid=(s // ROW_TILE,),
        in_specs=[row, vec, row, vec], out_specs=[row, row],
        out_shape=[jax.ShapeDtypeStruct((s, d), F32), jax.ShapeDtypeStruct((s, d), BF16)],
        compiler_params=_params(("parallel",)),
    )(sub, g_post, res, g_pre)


def _rms_bwd_pair(xmid, g_pre, dn, dres, sub, g_post, name):
    s, d = xmid.shape

    def body(x_ref, gn_ref, dn_ref, dres_ref, sub_ref, gp_ref, dx_ref, dsub_ref, dgn_ref, dgp_ref):
        @pl.when(pl.program_id(0) == 0)
        def _():
            dgn_ref[...] = jnp.zeros_like(dgn_ref)
            dgp_ref[...] = jnp.zeros_like(dgp_ref)

        dnv = dn_ref[...].astype(F32)
        r, xhat = _rms(x_ref[...])
        dx = dres_ref[...] + _rms_grad(r, xhat, dnv, gn_ref[...])
        dx_ref[...] = dx
        dgn_ref[...] += jnp.sum(dnv * xhat, axis=0, keepdims=True)
        rs, shat = _rms(sub_ref[...])
        dsub_ref[...] = _rms_grad(rs, shat, dx, gp_ref[...]).astype(BF16)
        dgp_ref[...] += jnp.sum(dx * shat, axis=0, keepdims=True)

    row = pl.BlockSpec((ROW_TILE, d), lambda i: (i, 0))
    vec = pl.BlockSpec((1, d), lambda i: (0, 0))
    return pl.pallas_call(
        body, name=name, grid=(s // ROW_TILE,),
        in_specs=[row, vec, row, row, row, vec], out_specs=[row, row, vec, vec],
        out_shape=[jax.ShapeDtypeStruct((s, d), F32), jax.ShapeDtypeStruct((s, d), BF16),
                   jax.ShapeDtypeStruct((1, d), F32), jax.ShapeDtypeStruct((1, d), F32)],
        compiler_params=_params(("arbitrary",)),
    )(xmid, g_pre, dn, dres, sub, g_post)


def _loss_head(y, tgt, name):
    s, d = y.shape

    def body(y_ref, t_ref, l_ref, dy_ref):
        @pl.when(pl.program_id(0) == 0)
        def _():
            l_ref[...] = jnp.zeros_like(l_ref)

        e = y_ref[...] - t_ref[...]
        dy_ref[...] = e / d
        per_tok = jnp.mean(e * e, axis=-1, keepdims=True)
        l_ref[...] += 0.5 * jnp.sum(per_tok, axis=0, keepdims=True)

    row = pl.BlockSpec((ROW_TILE, d), lambda i: (i, 0))
    return pl.pallas_call(
        body, name=name, grid=(s // ROW_TILE,),
        in_specs=[row, row],
        out_specs=[pl.BlockSpec((1, 1), lambda i: (0, 0)), row],
        out_shape=[jax.ShapeDtypeStruct((1, 1), F32), jax.ShapeDtypeStruct((s, d), F32)],
        compiler_params=_params(("arbitrary",)),
    )(y, tgt)


SUBLANES = 8


def _shift_down(x, k):
    t, c = x.shape
    r = pltpu.roll(x.reshape(t // SUBLANES, SUBLANES, c), k, axis=1)
    above = jnp.concatenate([jnp.zeros((1, SUBLANES, c), x.dtype), r[:-1]], axis=0)
    rows = lax.broadcasted_iota(jnp.int32, (1, SUBLANES, c), 1)
    return jnp.where(rows >= k, r, above).reshape(t, c)


def _shift_up(x, k):
    t, c = x.shape
    r = pltpu.roll(x.reshape(t // SUBLANES, SUBLANES, c), SUBLANES - k, axis=1)
    below = jnp.concatenate([r[1:], jnp.zeros((1, SUBLANES, c), x.dtype)], axis=0)
    rows = lax.broadcasted_iota(jnp.int32, (1, SUBLANES, c), 1)
    return jnp.where(rows < SUBLANES - k, r, below).reshape(t, c)


def _conv3(h, w):
    return w[2:3] * h + w[1:2] * _shift_down(h, 1) + w[0:1] * _shift_down(h, 2)


def _conv3_bwd(dc, h, w, dw_ref, cols=slice(None)):
    u1, u2 = _shift_up(dc, 1), _shift_up(dc, 2)
    dw_ref[0:1, cols] = jnp.sum(u2 * h, axis=0, keepdims=True)
    dw_ref[1:2, cols] = jnp.sum(u1 * h, axis=0, keepdims=True)
    dw_ref[2:3, cols] = jnp.sum(dc * h, axis=0, keepdims=True)
    return w[2:3] * dc + w[1:2] * u1 + w[0:1] * u2


FFN_PAIRS = N_DEV // 2


def _lane_chunks(width):
    return [(c0, min(COL_TILE, width - c0)) for c0 in range(0, width, COL_TILE)]


def _ffn_up(n, wup, wdw, name):
    s, d = n.shape
    cw = wup.shape[-1]

    def body(n_ref, wg_ref, wu_ref, dg_ref, du_ref, h_ref, c_ref, a_ref):
        x = n_ref[...]
        for c0, size in _lane_chunks(cw):
            cols = slice(c0, c0 + size)
            hg = _dot(x, wg_ref[:, cols])
            hu = _dot(x, wu_ref[:, cols])
            h_ref[0, :, cols] = hg.astype(BF16)
            h_ref[1, :, cols] = hu.astype(BF16)
            cg = _conv3(hg, dg_ref[:, cols])
            cu = _conv3(hu, du_ref[:, cols])
            c_ref[0, :, cols] = cg.astype(BF16)
            c_ref[1, :, cols] = cu.astype(BF16)
            a_ref[:, cols] = (cg * jax.nn.sigmoid(cg) * cu).astype(BF16)

    return pl.pallas_call(
        body, name=name, grid=(FFN_PAIRS,),
        in_specs=[pl.BlockSpec((s, d), lambda j: (0, 0)),
                  pl.BlockSpec((None, d, cw), lambda j: (j, 0, 0)),
                  pl.BlockSpec((None, d, cw), lambda j: (j + FFN_PAIRS, 0, 0)),
                  pl.BlockSpec((None, 3, cw), lambda j: (j, 0, 0)),
                  pl.BlockSpec((None, 3, cw), lambda j: (j + FFN_PAIRS, 0, 0))],
        out_specs=[pl.BlockSpec((None, 2, s, cw), lambda j: (j, 0, 0, 0)),
                   pl.BlockSpec((None, 2, s, cw), lambda j: (j, 0, 0, 0)),
                   pl.BlockSpec((None, s, cw), lambda j: (j, 0, 0))],
        out_shape=[jax.ShapeDtypeStruct((FFN_PAIRS, 2, s, cw), BF16), jax.ShapeDtypeStruct((FFN_PAIRS, 2, s, cw), BF16),
                   jax.ShapeDtypeStruct((FFN_PAIRS, s, cw), BF16)],
        compiler_params=_params(("parallel",)),
    )(n, wup, wup, wdw, wdw)


def _ffn_mid_bwd(do, wdown, h, c, wdw, name):
    s, d = do.shape
    cw = wdown.shape[1]

    def body(do_ref, wd_ref, h_ref, c_ref, wg_ref, wu_ref, dh_ref, dwg_ref, dwu_ref):
        dov = do_ref[...]
        for c0, size in _lane_chunks(cw):
            cols = slice(c0, c0 + size)
            da = _dot(dov, wd_ref[cols, :], NT)
            hg = h_ref[0, :, cols].astype(F32)
            hu = h_ref[1, :, cols].astype(F32)
            wg, wu = wg_ref[:, cols], wu_ref[:, cols]
            cg = c_ref[0, :, cols].astype(F32)
            cu = c_ref[1, :, cols].astype(F32)
            sg = jax.nn.sigmoid(cg)
            dcu = da * (cg * sg)
            dcg = da * cu * (sg * (1.0 + cg * (1.0 - sg)))
            dh_ref[0, :, cols] = _conv3_bwd(dcg, hg, wg, dwg_ref, cols).astype(BF16)
            dh_ref[1, :, cols] = _conv3_bwd(dcu, hu, wu, dwu_ref, cols).astype(BF16)

    vec = jax.ShapeDtypeStruct((FFN_PAIRS, 3, cw), F32)
    return pl.pallas_call(
        body, name=name, grid=(FFN_PAIRS,),
        in_specs=[pl.BlockSpec((s, d), lambda j: (0, 0)), pl.BlockSpec((None, cw, d), lambda j: (j, 0, 0)),
                  pl.BlockSpec((None, 2, s, cw), lambda j: (j, 0, 0, 0)),
                  pl.BlockSpec((None, 2, s, cw), lambda j: (j, 0, 0, 0)),
                  pl.BlockSpec((None, 3, cw), lambda j: (j, 0, 0)),
                  pl.BlockSpec((None, 3, cw), lambda j: (j + FFN_PAIRS, 0, 0))],
        out_specs=[pl.BlockSpec((None, 2, s, cw), lambda j: (j, 0, 0, 0)),
                   pl.BlockSpec((None, 3, cw), lambda j: (j, 0, 0)), pl.BlockSpec((None, 3, cw), lambda j: (j, 0, 0))],
        out_shape=[jax.ShapeDtypeStruct((FFN_PAIRS, 2, s, cw), BF16), vec, vec],
        compiler_params=_params(("parallel",)),
    )(do, wdown, h, c, wdw, wdw)


def _ffn_dwup(n, dh, name, dep=None):
    s, d = n.shape
    cw = dh.shape[-1]
    deps = [] if dep is None else [dep]

    def body(n_ref, dh_ref, *rest):
        rest[-1][...] = _dot(n_ref[...], dh_ref[...], TN).astype(BF16)

    return pl.pallas_call(
        body, name=name, grid=(N_DEV,),
        in_specs=[pl.BlockSpec((s, d), lambda k: (0, 0)),
                  pl.BlockSpec((None, None, s, cw), lambda k: (k % FFN_PAIRS, k // FFN_PAIRS, 0, 0))] + [_ANY] * len(deps),
        out_specs=pl.BlockSpec((None, d, cw), lambda k: (k, 0, 0)),
        out_shape=jax.ShapeDtypeStruct((N_DEV, d, cw), BF16),
        compiler_params=_params(("parallel",)),
    )(n, dh, *deps)


def _ffn_dn(dh, wup, name, dep=None):
    s, cw = dh.shape[-2:]
    d = wup.shape[1]
    tm = _pick(s, MATMUL_TILES)
    deps = [] if dep is None else [dep]

    def body(dh_ref, w_ref, *rest):
        o_ref, acc_ref = rest[-2:]
        k = pl.program_id(1)

        @pl.when(k == 0)
        def _():
            acc_ref[...] = jnp.zeros_like(acc_ref)

        acc_ref[...] += _dot(dh_ref[...], w_ref[...], NT)

        @pl.when(k == N_DEV - 1)
        def _():
            o_ref[...] = acc_ref[...]

    return pl.pallas_call(
        body, name=name, grid=(s // tm, N_DEV),
        in_specs=[pl.BlockSpec((None, None, tm, cw), lambda i, k: (k % FFN_PAIRS, k // FFN_PAIRS, i, 0)),
                  pl.BlockSpec((None, d, cw), lambda i, k: (k, 0, 0))] + [_ANY] * len(deps),
        out_specs=pl.BlockSpec((tm, d), lambda i, k: (i, 0)),
        out_shape=jax.ShapeDtypeStruct((s, d), F32),
        scratch_shapes=[pltpu.VMEM((tm, d), F32)],
        compiler_params=_params(("parallel", "arbitrary")),
    )(dh, wup, *deps)


def _sconv_fwd(n, win, wdw, name):
    s, d = n.shape
    tn = COL_TILE
    nj = d // tn

    def body(n_ref, wb_ref, wc_ref, wh_ref, dw_ref, z_ref, y_ref):
        x = n_ref[...]
        zb = _dot(x, wb_ref[...])
        zc = _dot(x, wc_ref[...])
        zh = _dot(x, wh_ref[...])
        z_ref[0] = zb.astype(BF16)
        z_ref[1] = zc.astype(BF16)
        z_ref[2] = zh.astype(BF16)
        y_ref[...] = (zb * _conv3(zc * zh, dw_ref[...])).astype(BF16)

    return pl.pallas_call(
        body, name=name, grid=(nj,),
        in_specs=[pl.BlockSpec((s, d), lambda j: (0, 0)),
                  pl.BlockSpec((d, tn), lambda j: (0, j)), pl.BlockSpec((d, tn), lambda j: (0, j + nj)),
                  pl.BlockSpec((d, tn), lambda j: (0, j + 2 * nj)), pl.BlockSpec((3, tn), lambda j: (0, j))],
        out_specs=[pl.BlockSpec((3, s, tn), lambda j: (0, 0, j)), pl.BlockSpec((s, tn), lambda j: (0, j))],
        out_shape=[jax.ShapeDtypeStruct((3, s, d), BF16), jax.ShapeDtypeStruct((s, d), BF16)],
        compiler_params=_params(("parallel",)),
    )(n, win, win, win, wdw)


def _sconv_mid_bwd(dm, wout, z, wdw, name):
    s, d = dm.shape
    tn = COL_TILE
    nj = d // tn

    def body(dm_ref, wo_ref, z_ref, w_ref, dz_ref, dw_ref):
        dy = _dot(dm_ref[...], wo_ref[...], NT)
        zb = z_ref[0].astype(F32)
        zc = z_ref[1].astype(F32)
        zh = z_ref[2].astype(F32)
        w = w_ref[...]
        p = zc * zh
        cp = _conv3(p, w)
        dz_ref[0] = (dy * cp).astype(BF16)
        dcp = dy * zb
        dp = _conv3_bwd(dcp, p, w, dw_ref)
        dz_ref[1] = (dp * zh).astype(BF16)
        dz_ref[2] = (dp * zc).astype(BF16)

    return pl.pallas_call(
        body, name=name, grid=(nj,),
        in_specs=[pl.BlockSpec((s, d), lambda j: (0, 0)), pl.BlockSpec((tn, d), lambda j: (j, 0)),
                  pl.BlockSpec((3, s, tn), lambda j: (0, 0, j)), pl.BlockSpec((3, tn), lambda j: (0, j))],
        out_specs=[pl.BlockSpec((3, s, tn), lambda j: (0, 0, j)), pl.BlockSpec((3, tn), lambda j: (0, j))],
        out_shape=[jax.ShapeDtypeStruct((3, s, d), BF16), jax.ShapeDtypeStruct((3, d), F32)],
        compiler_params=_params(("parallel",)),
    )(dm, wout, z, wdw)


def _pool_select(g, c2, c4, c8, c16):
    return jnp.where(g == 0, c2, jnp.where(g == 1, c4, jnp.where(g == 2, c8, c16)))


def _pool_inv_count(g, shape):
    pos = lax.broadcasted_iota(jnp.int32, shape, 0).astype(F32) + 1.0
    win = (2 << g).astype(F32)
    return jnp.minimum(pos, win)


def _pool_fwd(n, win, wgrp, scale, name):
    s, d = n.shape
    tn = POOL_GROUP_DIM

    def body(n_ref, wi_ref, wg_ref, sc_ref, p_ref, y_ref):
        g = pl.program_id(0)
        u = _dot(n_ref[...], wi_ref[...])
        s2 = u + _shift_down(u, 1)
        s4 = s2 + _shift_down(s2, 2)
        s8 = s4 + _shift_down(s4, 4)
        s16 = s8 + _shift_down(s8, 8)
        tot = _pool_select(g, s2, s4, s8, s16)
        p = (tot / _pool_inv_count(g, u.shape) - u).astype(BF16)
        p_ref[...] = p
        y_ref[...] = (_dot(p, wg_ref[...]) * sc_ref[...]).astype(BF16)

    return pl.pallas_call(
        body, name=name, grid=(d // tn,),
        in_specs=[pl.BlockSpec((s, d), lambda g: (0, 0)), pl.BlockSpec((d, tn), lambda g: (0, g)),
                  pl.BlockSpec((None, tn, tn), lambda g: (g, 0, 0)), pl.BlockSpec((1, tn), lambda g: (0, g))],
        out_specs=[pl.BlockSpec((s, tn), lambda g: (0, g)), pl.BlockSpec((s, tn), lambda g: (0, g))],
        out_shape=[jax.ShapeDtypeStruct((s, d), BF16), jax.ShapeDtypeStruct((s, d), BF16)],
        compiler_params=_params(("parallel",)),
    )(n, win, wgrp, scale)


def _pool_mid_bwd(dm, wout, p, wgrp, scale, name):
    s, d = dm.shape
    tn = POOL_GROUP_DIM

    def body(dm_ref, wo_ref, p_ref, wg_ref, sc_ref, du_ref, dwg_ref, dsc_ref):
        g = pl.program_id(0)
        dy = _dot(dm_ref[...], wo_ref[...], NT)
        pv = p_ref[...]
        wg = wg_ref[...]
        ypre = _dot(pv, wg)
        dsc_ref[...] = jnp.sum(dy * ypre, axis=0, keepdims=True)
        dypre = (dy * sc_ref[...]).astype(BF16)
        dwg_ref[...] = _dot(pv, dypre, TN)
        dp = _dot(dypre, wg, NT)
        e = dp / _pool_inv_count(g, dp.shape)
        f2 = e + _shift_up(e, 1)
        f4 = f2 + _shift_up(f2, 2)
        f8 = f4 + _shift_up(f4, 4)
        f16 = f8 + _shift_up(f8, 8)
        du_ref[...] = (_pool_select(g, f2, f4, f8, f16) - dp).astype(BF16)

    return pl.pallas_call(
        body, name=name, grid=(d // tn,),
        in_specs=[pl.BlockSpec((s, d), lambda g: (0, 0)), pl.BlockSpec((tn, d), lambda g: (g, 0)),
                  pl.BlockSpec((s, tn), lambda g: (0, g)), pl.BlockSpec((None, tn, tn), lambda g: (g, 0, 0)),
                  pl.BlockSpec((1, tn), lambda g: (0, g))],
        out_specs=[pl.BlockSpec((s, tn), lambda g: (0, g)), pl.BlockSpec((None, tn, tn), lambda g: (g, 0, 0)),
                   pl.BlockSpec((1, tn), lambda g: (0, g))],
        out_shape=[jax.ShapeDtypeStruct((s, d), BF16), jax.ShapeDtypeStruct((4, tn, tn), F32),
                   jax.ShapeDtypeStruct((1, d), F32)],
        compiler_params=_params(("parallel",)),
    )(dm, wout, p, wgrp, scale)


PANEL = LANES
ATTN_EXT = ATTN_WIDTH + PANEL
DVEC_LANE = HEADS


def _alibi_slopes(g, dil):
    all_slopes = 2.0 ** (-8.0 * np.arange(1, N_HEADS_A + 1) / N_HEADS_A)
    return [float(np.float32(sl) * np.float32(dil)) for sl in all_slopes[g * HEADS:(g + 1) * HEADS]]


def _residue_order(a, dil, name):
    s, w = a.shape
    per = ROW_TILE // dil
    panels = w // PANEL

    def body(a_ref, o_ref, *tiles):
        for c in range(panels):
            cols = slice(c * PANEL, (c + 1) * PANEL)
            tiles[c][...] = a_ref[:, cols].astype(F32)
            for r in range(dil):
                o_ref[r, :, cols] = tiles[c][pl.ds(r, per, stride=dil), :].astype(o_ref.dtype)

    out = pl.pallas_call(
        body, name=name, grid=(s // ROW_TILE,),
        in_specs=[pl.BlockSpec((ROW_TILE, w), lambda i: (i, 0))],
        out_specs=pl.BlockSpec((dil, per, w), lambda i: (0, i, 0)),
        out_shape=jax.ShapeDtypeStruct((dil, s // dil, w), a.dtype),
        scratch_shapes=[pltpu.VMEM((ROW_TILE, PANEL), F32)] * panels,
        compiler_params=_params(("parallel",)),
    )(a)
    return out.reshape(s, w)


def _token_order(a, dil, acc, name):
    s, w = a.shape
    per = ROW_TILE // dil
    panels = w // PANEL
    has_acc = acc is not None

    def body(*refs):
        a_ref = refs[0]
        o_ref = refs[2] if has_acc else refs[1]
        tiles = refs[3:] if has_acc else refs[2:]
        for c in range(panels):
            cols = slice(c * PANEL, (c + 1) * PANEL)
            for r in range(dil):
                tiles[c][pl.ds(r, per, stride=dil), :] = a_ref[r, :, cols]
            v = tiles[c][...]
            if has_acc:
                v = v + refs[1][:, cols]
            o_ref[:, cols] = v

    row = pl.BlockSpec((ROW_TILE, w), lambda i: (i, 0))
    return pl.pallas_call(
        body, name=name, grid=(s // ROW_TILE,),
        in_specs=[pl.BlockSpec((dil, per, w), lambda i: (0, i, 0))] + ([row] if has_acc else []),
        out_specs=row, out_shape=jax.ShapeDtypeStruct((s, w), F32),
        scratch_shapes=[pltpu.VMEM((ROW_TILE, PANEL), F32)] * panels,
        compiler_params=_params(("parallel",)),
    )(*([a.reshape(dil, s // dil, w)] + ([acc] if has_acc else [])))


def _qkv_proj(n, wqkv, g, name):
    s, d = n.shape
    tm = _pick(s, MATMUL_TILES)

    def body(a_ref, b_ref, o_ref):
        o_ref[...] = _dot(a_ref[...], b_ref[...]).astype(BF16)

    return pl.pallas_call(
        body, name=name, grid=(s // tm, 3),
        in_specs=[pl.BlockSpec((tm, d), lambda i, t: (i, 0)),
                  pl.BlockSpec((d, ATTN_WIDTH), lambda i, t: (0, 3 * g + t))],
        out_specs=pl.BlockSpec((None, tm, ATTN_WIDTH), lambda i, t: (t, i, 0)),
        out_shape=jax.ShapeDtypeStruct((3, s, ATTN_WIDTH), BF16),
        compiler_params=_params(("parallel", "parallel")),
    )(n, wqkv)


def _attn_window(n, ln):
    if ln == BLOCK:
        return 0, BLOCK
    return pl.multiple_of(jnp.maximum(n - 1, 0) * BLOCK, BLOCK), 2 * BLOCK


def _attn_mask(n, k0, kw):
    qpos = n * BLOCK + lax.broadcasted_iota(jnp.int32, (BLOCK, kw), 0)
    kpos = k0 + lax.broadcasted_iota(jnp.int32, (BLOCK, kw), 1)
    dist = qpos - kpos
    return dist.astype(F32), (dist >= 0) & (dist <= BLOCK)


def _attn_scores(q, keys, slope, dist, valid):
    s = _dot(q, keys, NT) * (HEAD_DIM ** -0.5) - slope * dist
    return jnp.where(valid, s, NEG_INF)


ATTN_STEP_BLOCKS = 1
ATTN_BWD_STEP_BLOCKS = 4


def _attn_block(gb, ln):
    nb = ln // BLOCK
    n, base = (0, gb * ln) if nb == 1 else (gb % nb, (gb // nb) * ln)
    k0, kw = _attn_window(n, ln)
    cur = pl.ds(pl.multiple_of(gb * BLOCK, BLOCK), BLOCK)
    win = pl.ds(pl.multiple_of(base + k0, BLOCK), kw)
    return cur, win, n, k0, kw


def _attn_fwd(qkv, g, name):
    _, s, w = qkv.shape
    dil = DILATED_CFG[g][1]
    ln = s // dil
    slopes = _alibi_slopes(g, dil)
    rows = ATTN_STEP_BLOCKS * BLOCK

    def body(qkv_ref, o_ref):
        o_ref[:, w:] = jnp.zeros((rows, PANEL), F32)
        for b in range(ATTN_STEP_BLOCKS):
            cur, win, n, k0, kw = _attn_block(pl.program_id(0) * ATTN_STEP_BLOCKS + b, ln)
            dist, valid = _attn_mask(n, k0, kw)
            out = slice(b * BLOCK, (b + 1) * BLOCK)
            for h in range(HEADS):
                cols = slice(h * HEAD_DIM, (h + 1) * HEAD_DIM)
                sc = _attn_scores(qkv_ref[0, cur, cols], qkv_ref[1, win, cols], slopes[h], dist, valid)
                m = jnp.max(sc, axis=-1, keepdims=True)
                p = jnp.exp(sc - m)
                den = jnp.sum(p, axis=-1, keepdims=True)
                o_ref[out, cols] = _dot(p.astype(BF16), qkv_ref[2, win, cols]) / den
                o_ref[out, w + h:w + h + 1] = m + jnp.log(den)

    return pl.pallas_call(
        body, name=name, grid=(s // rows,),
        in_specs=[pl.BlockSpec((3, s, w), lambda i: (0, 0, 0))],
        out_specs=pl.BlockSpec((rows, ATTN_EXT), lambda i: (i, 0)),
        out_shape=jax.ShapeDtypeStruct((s, ATTN_EXT), F32),
        compiler_params=_params(("parallel",)),
    )(qkv)


def _attn_bwd(qkv, dext, g, name, dep=None):
    _, s, w = qkv.shape
    dil = DILATED_CFG[g][1]
    ln = s // dil
    slopes = _alibi_slopes(g, dil)
    scale = HEAD_DIM ** -0.5
    rows = ATTN_BWD_STEP_BLOCKS * BLOCK
    steps = s // rows
    deps = [] if dep is None else [dep]

    def body(qkv_ref, de_ref, *rest):
        d_ref, dk_ref, dv_ref = rest[-3:]

        @pl.when(pl.program_id(0) == 0)
        def _():
            dk_ref[...] = jnp.zeros_like(dk_ref)
            dv_ref[...] = jnp.zeros_like(dv_ref)

        for b in range(ATTN_BWD_STEP_BLOCKS):
            cur, win, n, k0, kw = _attn_block(pl.program_id(0) * ATTN_BWD_STEP_BLOCKS + b, ln)
            dist, valid = _attn_mask(n, k0, kw)
            blk = slice(b * BLOCK, (b + 1) * BLOCK)
            for h in range(HEADS):
                cols = slice(h * HEAD_DIM, (h + 1) * HEAD_DIM)
                q, keys = qkv_ref[0, cur, cols], qkv_ref[1, win, cols]
                dob = de_ref[blk, cols].astype(BF16)
                p = jnp.exp(_attn_scores(q, keys, slopes[h], dist, valid) - de_ref[blk, w + h:w + h + 1])
                dd = de_ref[blk, w + DVEC_LANE + h:w + DVEC_LANE + h + 1]
                ds = (p * (_dot(dob, qkv_ref[2, win, cols], NT) - dd)).astype(BF16)
                d_ref[0, cur, cols] = (scale * _dot(ds, keys)).astype(BF16)
                dv_ref[win, cols] += _dot(p.astype(BF16), dob, TN)
                dk_ref[win, cols] += scale * _dot(ds, q, TN)

        @pl.when(pl.program_id(0) == steps - 1)
        def _():
            d_ref[1] = dk_ref[...].astype(BF16)
            d_ref[2] = dv_ref[...].astype(BF16)

    whole = pl.BlockSpec((3, s, w), lambda i: (0, 0, 0))
    return pl.pallas_call(
        body, name=name, grid=(steps,),
        in_specs=[whole, pl.BlockSpec((rows, ATTN_EXT), lambda i: (i, 0))] + [_ANY] * len(deps),
        out_specs=whole, out_shape=jax.ShapeDtypeStruct((3, s, w), BF16),
        scratch_shapes=[pltpu.VMEM((s, w), F32), pltpu.VMEM((s, w), F32)],
        compiler_params=_params(("arbitrary",)),
    )(qkv, dext, *deps)


def _attn_merge(e0, e1, e2, name):
    s = e0.shape[0]
    w = ATTN_WIDTH

    def body(e0_ref, e1_ref, e2_ref, m_ref, mb_ref, lse_ref):
        refs = (e0_ref, e1_ref, e2_ref)
        l = [r[:, w:w + HEADS] for r in refs]
        mx = jnp.maximum(jnp.maximum(l[0], l[1]), l[2])
        e = [jnp.exp(v - mx) for v in l]
        z = e[0] + e[1] + e[2]
        lse_ref[...] = mx + jnp.log(z)
        wts = [v / z for v in e]
        for h in range(HEADS):
            cols = slice(h * HEAD_DIM, (h + 1) * HEAD_DIM)
            acc = wts[0][:, h:h + 1] * refs[0][:, cols]
            for g in range(1, N_GROUPS_A):
                acc = acc + wts[g][:, h:h + 1] * refs[g][:, cols]
            m_ref[:, cols] = acc
            mb_ref[:, cols] = acc.astype(BF16)

    ext = pl.BlockSpec((ROW_TILE, ATTN_EXT), lambda i: (i, 0))
    row = pl.BlockSpec((ROW_TILE, w), lambda i: (i, 0))
    return pl.pallas_call(
        body, name=name, grid=(s // ROW_TILE,),
        in_specs=[ext, ext, ext],
        out_specs=[row, row, pl.BlockSpec((ROW_TILE, HEADS), lambda i: (i, 0))],
        out_shape=[jax.ShapeDtypeStruct((s, w), F32), jax.ShapeDtypeStruct((s, w), BF16),
                   jax.ShapeDtypeStruct((s, HEADS), F32)],
        compiler_params=_params(("parallel",)),
    )(e0, e1, e2)


def _attn_dvec(dmerged, merged, lse_all, name, dep=None):
    s, w = merged.shape
    deps = [] if dep is None else [dep]

    def body(dm_ref, m_ref, lse_ref, *rest):
        de_ref = rest[-1]
        dmv = dm_ref[...]
        de_ref[:, :w] = dmv
        de_ref[:, w:] = jnp.zeros((ROW_TILE, PANEL), F32)
        de_ref[:, w:w + HEADS] = lse_ref[...]
        prod = dmv * m_ref[...]
        for h in range(HEADS):
            lane = w + DVEC_LANE + h
            de_ref[:, lane:lane + 1] = jnp.sum(prod[:, h * HEAD_DIM:(h + 1) * HEAD_DIM], axis=-1, keepdims=True)

    row = pl.BlockSpec((ROW_TILE, w), lambda i: (i, 0))
    return pl.pallas_call(
        body, name=name, grid=(s // ROW_TILE,),
        in_specs=[row, row, pl.BlockSpec((ROW_TILE, HEADS), lambda i: (i, 0))] + [_ANY] * len(deps),
        out_specs=pl.BlockSpec((ROW_TILE, ATTN_EXT), lambda i: (i, 0)),
        out_shape=jax.ShapeDtypeStruct((s, ATTN_EXT), F32),
        compiler_params=_params(("parallel",)),
    )(dmerged, merged, lse_all, *deps)


def _attention_fwd(n, wqkv, wo, tag):
    ns, qkvs, exts = [], [], []
    for g, (_, dil) in enumerate(DILATED_CFG):
        ng = n if dil == 1 else _residue_order(n, dil, f"{tag}_order_g{g}")
        qkv = _qkv_proj(ng, wqkv, g, f"{tag}_qkv_g{g}")
        ext = _attn_fwd(qkv, g, f"{tag}_fwd_g{g}")
        ns.append(ng)
        qkvs.append(qkv)
        exts.append(ext if dil == 1 else _token_order(ext, dil, None, f"{tag}_unorder_g{g}"))
    merged, merged_bf, lse_all = _attn_merge(*exts, f"{tag}_merge")
    m = _matmul(merged_bf, wo, "nn", F32, f"{tag}_wo")
    return m, (ns, qkvs, merged, merged_bf, lse_all)


def _attention_bwd(dm, wqkv, wo, saved, tag, dep=None, hook=None):
    ns, qkvs, merged, merged_bf, lse_all = saved
    d_wo = _matmul(merged_bf, dm, "tn", BF16, f"{tag}_dwo")
    dmerged = _matmul(dm, wo, "nt", F32, f"{tag}_dmerged")
    dext = _attn_dvec(dmerged, merged, lse_all, f"{tag}_dvec", dep)
    width = 3 * ATTN_WIDTH
    d_wqkv, dn, dep = [], None, None
    for g, (_, dil) in enumerate(DILATED_CFG):
        dext_g = dext if dil == 1 else _residue_order(dext, dil, f"{tag}_dorder_g{g}")
        dqkv = _attn_bwd(qkvs[g], dext_g, g, f"{tag}_bwd_g{g}", dep)
        dep = hook(g, dqkv) if hook is not None and g + 1 < N_GROUPS_A else None
        d_wqkv.append(_matmul(ns[g], dqkv, "tn", BF16, f"{tag}_dwqkv_g{g}", b_parts=3))
        dn_g = _matmul(dqkv, wqkv[:, g * width:(g + 1) * width], "nt", F32, f"{tag}_dn_g{g}", a_parts=3)
        dn = dn_g if dil == 1 else _token_order(dn_g, dil, dn, f"{tag}_dn_sum_g{g}")
    return dn, jnp.concatenate(d_wqkv, axis=1), d_wo


def _layer_matrices(i):
    mixer = (("attn_w_qkv", "attn_w_o"), ("conv_w_in", "conv_w_out"), ("pool_w_in", "pool_w_grp", "pool_w_out"))[i % 3]
    return [(k, i // 3) for k in mixer] + [("ffn_w_up", i), ("ffn_w_down", i)]


def _local_step(x, tgt, vec, weights, sink):
    ng = vec["norm_g"]

    def gain(i, j, token=None):
        g = ng[i, j][None, :]
        return g if token is None else g + token

    saved = []
    n = _rms_fwd(x, gain(0, 0), None, BF16, "norm_first")
    for i in range(DEPTH):
        wl = weights.layer(i)
        t0 = weights.hook(i, 0, n)
        kind, idx = i % 3, i // 3
        if kind == 0:
            m, ms = _attention_fwd(n, wl["attn_w_qkv"], wl["attn_w_o"], "attn")
        elif kind == 1:
            taps = vec["conv_w_dw"][idx] if t0 is None else vec["conv_w_dw"][idx] + t0
            z, y = _sconv_fwd(n, wl["conv_w_in"], taps, "sconv_fwd")
            m = _matmul(y, wl["conv_w_out"], "nn", F32, "sconv_out")
            ms = (z, y)
        else:
            scale = vec["pool_scale"][idx][None, :] if t0 is None else vec["pool_scale"][idx][None, :] + t0
            p, y = _pool_fwd(n, wl["pool_w_in"], wl["pool_w_grp"], scale, "pool_fwd")
            m = _matmul(y, wl["pool_w_out"], "nn", F32, "pool_out")
            ms = (p, y)
        t1 = weights.hook(i, 1, m)
        x1, n2 = _rms_res_pre(m, gain(i, 1, t0), x, gain(i, 2, t1), "norm_res_pre")
        h, c, a = _ffn_up(n2, wl["ffn_w_up"], vec["ffn_w_dw"][i], "ffn_up")
        t2 = weights.hook(i, 2, a)
        f = _matmul(a, wl["ffn_w_down"].reshape(D_FF, D_MODEL), "nn", F32, "ffn_down", a_parts=FFN_PAIRS)
        saved.append((x, n, m, ms, x1, n2, h, a, f, wl, c))
        if i + 1 < DEPTH:
            x, n = _rms_res_pre(f, gain(i, 3, t2), x1, gain(i + 1, 0), "norm_res_pre")
        else:
            x = _rms_fwd(f, gain(i, 3), x1, F32, "norm_res")
        weights.hook(i, 3, x)

    loss, dx = _loss_head(x, tgt, "loss_head")

    g_norm = [[None] * 4 for _ in range(DEPTH)]
    g_taps, g_scale, g_ffn_dw = [], [], [None] * DEPTH
    df, g_norm[DEPTH - 1][3] = _rms_bwd(saved[-1][8], gain(DEPTH - 1, 3), dx, None, BF16, "norm_bwd_sub")
    t0 = None
    for i in reversed(range(DEPTH)):
        xin, n, m, ms, x1, n2, h, a, f, wl, c = saved[i]
        kind, idx = i % 3, i // 3
        gl = {}
        d_wdown = _matmul(a, df, "tn", BF16, "ffn_dwdown", a_parts=FFN_PAIRS)
        gl["ffn_w_down"] = d_wdown.reshape(N_DEV, D_FF // N_DEV, D_MODEL)
        ffn_taps = vec["ffn_w_dw"][i] if t0 is None else vec["ffn_w_dw"][i] + t0
        dh, dwg, dwu = _ffn_mid_bwd(df, wl["ffn_w_down"].reshape(FFN_PAIRS, -1, D_MODEL), h, c, ffn_taps, "ffn_mid_bwd")
        g_ffn_dw[i] = jnp.concatenate([dwg, dwu], axis=0)
        t1 = sink.hook(i, 1, dh)
        gl["ffn_w_up"] = _ffn_dwup(n2, dh, "ffn_dwup", t1)
        tf = sink.ffn_done(i, gl)
        dn2 = _ffn_dn(dh, wl["ffn_w_up"], "ffn_dn", t1)
        dx1, dm, g_norm[i][2], g_norm[i][1] = _rms_bwd_pair(x1, gain(i, 2, tf), dn2, dx, m, gain(i, 1), "norm_bwd_pair")
        t2 = sink.hook(i, 2, dm)
        if kind == 0:
            dn, gl["attn_w_qkv"], gl["attn_w_o"] = _attention_bwd(
                dm, wl["attn_w_qkv"], wl["attn_w_o"], ms, "attn", t2, lambda g, after, i=i: sink.hook(i, ("a", "b")[g], after))
        elif kind == 1:
            z, y = ms
            gl["conv_w_out"] = _matmul(y, dm, "tn", BF16, "sconv_dwout")
            taps = vec["conv_w_dw"][idx] if t2 is None else vec["conv_w_dw"][idx] + t2
            dz, ddw = _sconv_mid_bwd(dm, wl["conv_w_out"], z, taps, "sconv_mid_bwd")
            g_taps.append(ddw)
            gl["conv_w_in"] = _matmul(n, dz, "tn", BF16, "sconv_dwin", b_parts=3)
            dn = _matmul(dz, wl["conv_w_in"], "nt", F32, "sconv_dn", a_parts=3)
        else:
            p, y = ms
            gl["pool_w_out"] = _matmul(y, dm, "tn", BF16, "pool_dwout")
            scale = vec["pool_scale"][idx][None, :] if t2 is None else vec["pool_scale"][idx][None, :] + t2
            du, gl["pool_w_grp"], dscale = _pool_mid_bwd(dm, wl["pool_w_out"], p, wl["pool_w_grp"], scale, "pool_mid_bwd")
            g_scale.append(dscale[0])
            gl["pool_w_in"] = _matmul(n, du, "tn", BF16, "pool_dwin")
            dn = _matmul(du, wl["pool_w_in"], "nt", F32, "pool_dn")
        sink.hook(i, 3, dn)
        if i > 0:
            dx, df, g_norm[i][0], g_norm[i - 1][3] = _rms_bwd_pair(xin, gain(i, 0, t2), dn, dx1, saved[i - 1][8],
                                                                   gain(i - 1, 3), "norm_bwd_pair")
        else:
            dx, g_norm[0][0] = _rms_bwd(xin, gain(0, 0), dn, dx1, F32, "norm_bwd_res")
        t0 = sink.layer_done(i, gl)

    vec_grads = {"norm_g": jnp.stack([jnp.concatenate(row, axis=0) for row in g_norm]), "conv_w_dw": jnp.stack(g_taps),
                 "pool_scale": jnp.stack(g_scale), "ffn_w_dw": g_ffn_dw}
    return loss, dx, vec_grads


_AXES = ("x", "y", "c")
ROUTE_A = ("y", "x", "c")
ROUTE_B = ("x", "y", "c")
def _dev_index(pos):
    return 4 * pos["x"] + 2 * pos["y"] + pos["c"]


_HBM = pl.BlockSpec(memory_space=pltpu.HBM)
_SEM = pl.BlockSpec(memory_space=pltpu.SEMAPHORE)
_ANY = pl.BlockSpec(memory_space=pl.ANY)
_EFFECT = pltpu.SideEffectType.DATAFLOW_SIDE_EFFECTING


TOKEN_SHAPE = (1, D_MODEL)


def _copies_start(describe, arrays, n_copies, name, after, token_shape=TOKEN_SHAPE):
    n = len(arrays)
    deps = [] if after is None else [after]

    def body(*refs):
        send_sems, recv_sems = refs[n + len(deps)], refs[n + len(deps) + 1]
        for c in describe(refs[:n], send_sems, recv_sems):
            c.start()
        refs[-1][...] = jnp.zeros_like(refs[-1])

    outs = pl.pallas_call(
        body, name=f"{name}_start",
        out_shape=(pltpu.SemaphoreType.DMA((n_copies,)), pltpu.SemaphoreType.DMA((n_copies,)),
                   *[pltpu.HBM(a.shape, a.dtype) for a in arrays], jax.ShapeDtypeStruct(token_shape, F32)),
        in_specs=[_HBM] * n + [_ANY] * len(deps),
        out_specs=(_SEM, _SEM, *([_HBM] * n), pl.BlockSpec(memory_space=pltpu.VMEM)),
        input_output_aliases={i: 2 + i for i in range(n)},
        compiler_params=pltpu.CompilerParams(has_side_effects=_EFFECT),
    )(*[pltpu.with_memory_space_constraint(a, pltpu.HBM) for a in arrays], *deps)
    return (outs[0], outs[1], list(outs[2:2 + n])), outs[-1]


def _copies_wait(describe, handle, name, after):
    send_sems, recv_sems, arrays = handle
    n = len(arrays)
    deps = [] if after is None else list(after) if isinstance(after, (list, tuple)) else [after]

    def body(*refs):
        for c in describe(refs[:n], refs[n], refs[n + 1]):
            c.wait_send()
            c.wait_recv()

    outs = pl.pallas_call(
        body, name=f"{name}_wait",
        out_shape=tuple(pltpu.HBM(a.shape, a.dtype) for a in arrays),
        in_specs=[_HBM] * n + [_SEM, _SEM] + [_ANY] * len(deps), out_specs=tuple([_HBM] * n),
        input_output_aliases={i: i for i in range(n)},
        compiler_params=pltpu.CompilerParams(has_side_effects=_EFFECT),
    )(*arrays, send_sems, recv_sems, *deps)
    return list(outs)


GATHER_STAGE_COPIES = (3, 3, 1)


def _gather_copies(stage, routes):
    n = len(routes)

    def describe(refs, send_sems, recv_sems):
        pos = {a: lax.axis_index(a) for a in _AXES}

        def flipped(axes):
            return {a: 1 - pos[a] if a in axes else pos[a] for a in _AXES}

        copies = []
        for i, (a1, a2, a3) in enumerate(routes):
            land = refs[n + i] if stage == 1 else refs[i]
            p1, p2, p12, p3 = flipped((a1,)), flipped((a2,)), flipped((a1, a2)), flipped((a3,))
            plan = {1: [(None, p1), (None, p2), (None, p3)], 2: [(p1, p2), (p1, p3), (p2, p3)], 3: [(p12, p3)]}[stage]
            for holder, to in plan:
                slot = land.at[_dev_index(pos if holder is None else holder)]
                k = len(copies)
                copies.append(pltpu.make_async_remote_copy(
                    src_ref=refs[i] if holder is None else slot, dst_ref=slot,
                    send_sem=send_sems.at[k], recv_sem=recv_sems.at[k],
                    device_id=tuple(to[a] for a in _AXES), device_id_type=pl.DeviceIdType.MESH))
        return copies

    return describe


def _gather_begin(shards, routes, name, after):
    n = len(shards)
    lands = [lax.empty((N_DEV,) + a.shape, a.dtype) for a in shards]
    handle, token = _copies_start(_gather_copies(1, routes), list(shards) + lands, GATHER_STAGE_COPIES[0] * n,
                                  f"{name}_1", after)
    return {"stage": 1, "handle": handle, "routes": routes, "name": name, "n": n}, token


def _gather_next(state, after):
    stage, routes, name, n = state["stage"], state["routes"], state["name"], state["n"]
    arrays = _copies_wait(_gather_copies(stage, routes), state["handle"], f"{name}_{stage}", after)
    if stage == 1:
        state = dict(state, shards=arrays[:n])
        arrays = arrays[n:]
    if stage == 3:
        me = _dev_index({a: lax.axis_index(a) for a in _AXES})
        return [lax.dynamic_update_index_in_dim(o, s, me, 0) for o, s in zip(arrays, state["shards"])], None
    handle, token = _copies_start(_gather_copies(stage + 1, routes), arrays, GATHER_STAGE_COPIES[stage] * n,
                                  f"{name}_{stage + 1}", None)
    return dict(state, stage=stage + 1, handle=handle), token


ADD_ROW_TILES = (1024, 704, 512, 352, 256, 128, 96, 64, 32, 16)


def _add_half(a, recv, me, out_dtype, name):
    p, q, cols = recv.shape
    tr = _pick(q, ADD_ROW_TILES)

    def body(me_ref, a_ref, b_ref, o_ref):
        o_ref[...] = (a_ref[...].astype(F32) + b_ref[...].astype(F32)).astype(o_ref.dtype)

    return pl.pallas_call(
        body, name=name,
        grid_spec=pltpu.PrefetchScalarGridSpec(
            num_scalar_prefetch=1, grid=(p, q // tr),
            in_specs=[pl.BlockSpec((None, None, tr, cols), lambda j, i, m: (j, m[0], i, 0)),
                      pl.BlockSpec((None, tr, cols), lambda j, i, m: (j, i, 0))],
            out_specs=pl.BlockSpec((None, tr, cols), lambda j, i, m: (j, i, 0))),
        out_shape=jax.ShapeDtypeStruct((p, q, cols), out_dtype),
        compiler_params=_params(("parallel", "parallel")),
    )(me, a, recv)


def _half_copies(axes):
    n = len(axes)

    def describe(refs, send_sems, recv_sems):
        pos = {a: lax.axis_index(a) for a in _AXES}
        copies = []
        for i, axis in enumerate(axes):
            peer = tuple(1 - pos[a] if a == axis else pos[a] for a in _AXES)
            copies.append(pltpu.make_async_remote_copy(
                src_ref=refs[i].at[:, 1 - pos[axis]], dst_ref=refs[n + i], send_sem=send_sems.at[i],
                recv_sem=recv_sems.at[i], device_id=peer, device_id_type=pl.DeviceIdType.MESH))
        return copies

    return describe


def _scatter_begin(slots, routes, tags, name, token_shape=TOKEN_SHAPE):
    shapes = [a.shape[1:] for a in slots]
    rows = [math.prod(s[:-1]) for s in shapes]
    arrays = [a.reshape(4, 2, n, s[-1]) for a, n, s in zip(slots, rows, shapes)]
    return _scatter_start({"stage": 0, "arrays": arrays, "routes": routes, "tags": tags, "name": name,
                           "shapes": shapes, "rows": rows}, token_shape)


def _scatter_start(state, token_shape=TOKEN_SHAPE):
    stage, arrays = state["stage"], state["arrays"]
    axes = [r[2 - stage] for r in state["routes"]]
    lands = [lax.empty((a.shape[0],) + a.shape[2:], a.dtype) for a in arrays]
    handle, token = _copies_start(_half_copies(axes), arrays + lands, len(arrays), f"{state['name']}_{stage + 1}", None,
                                  token_shape)
    return dict(state, handle=handle, axes=axes), token


def _scatter_next(state, after):
    stage, axes, n = state["stage"], state["axes"], len(state["arrays"])
    both = _copies_wait(_half_copies(axes), state["handle"], f"{state['name']}_{stage + 1}", after)
    coord = {a: lax.axis_index(a).astype(jnp.int32).reshape(1) for a in _AXES}
    sums = [_add_half(a, r, coord[ax], F32 if stage == 2 else BF16, f"scatter_add_{stage + 1}_{t}")
            for a, r, ax, t in zip(both[:n], both[n:], axes, state["tags"])]
    if stage == 2:
        return [a.reshape(s) for a, s in zip(sums, state["shapes"])], None
    if stage == 0:
        views = [(1, 2, 2 * r, s[-1]) if route[1] == "x" else (2, 2, r, s[-1])
                 for r, s, route in zip(state["rows"], state["shapes"], state["routes"])]
    else:
        views = [(1, 2, r, s[-1]) for r, s in zip(state["rows"], state["shapes"])]
    return _scatter_start(dict(state, stage=stage + 1, arrays=[a.reshape(v) for a, v in zip(sums, views)]))


_WEIGHTS = {
    "norm_g": ((DEPTH, 4, D_MODEL), 2, True),
    "attn_w_qkv": ((2, D_MODEL, 4608), 2, False),
    "attn_w_o": ((2, ATTN_WIDTH, D_MODEL), 2, False),
    "conv_w_in": ((1, D_MODEL, 3 * D_MODEL), 2, False),
    "conv_w_dw": ((1, 3, D_MODEL), 2, True),
    "conv_w_out": ((1, D_MODEL, D_MODEL), 1, False),
    "pool_w_in": ((1, D_MODEL, D_MODEL), 1, False),
    "pool_w_grp": ((1, 4, POOL_GROUP_DIM, POOL_GROUP_DIM), 2, False),
    "pool_scale": ((1, D_MODEL), 1, True),
    "pool_w_out": ((1, D_MODEL, D_MODEL), 1, False),
    "ffn_w_up": ((DEPTH, D_MODEL, 2 * D_FF), 2, False),
    "ffn_w_dw": ((DEPTH, 3, 2 * D_FF), 2, True),
    "ffn_w_down": ((DEPTH, D_FF, D_MODEL), 1, False),
}
_NAMES = tuple(_WEIGHTS)
_VECTORS = tuple(k for k in _NAMES if _WEIGHTS[k][2])
_MATRICES = tuple(k for k in _NAMES if not _WEIGHTS[k][2])
_FFN = ("ffn_w_up", "ffn_w_down")
_ON_ROUTE_A = ("ffn_w_up", "attn_w_o", "conv_w_out", "pool_w_in")
PACK_ROWS = 16


def _route(name):
    return ROUTE_A if name in _ON_ROUTE_A else ROUTE_B


def _shard_shape(name):
    shape, ax, _ = _WEIGHTS[name]
    return tuple(s // N_DEV if i == ax else s for i, s in enumerate(shape))


def _full_from_slots(slots, name, layers=None):
    shape, ax, _ = _WEIGHTS[name]
    if layers is not None:
        shape = (layers,) + shape[1:]
    return jnp.moveaxis(slots, 0, ax).reshape(shape)


def _slots_from_full(full, name):
    shape, ax, _ = _WEIGHTS[name]
    split = shape[:ax] + (N_DEV, shape[ax] // N_DEV) + shape[ax + 1:]
    return jnp.moveaxis(full.reshape(split), ax, 0)


def _pack_vectors(parts, lead):
    rows = []
    for k in _VECTORS:
        r = parts[k].reshape(lead + (-1, LANES))
        pad = -r.shape[-2] % PACK_ROWS
        rows.append(jnp.pad(r, [(0, 0)] * len(lead) + [(0, pad), (0, 0)]))
    return jnp.concatenate(rows, axis=len(lead))


def _unpack_vectors(buf, lead):
    out, r0 = {}, 0
    for k in _VECTORS:
        shard = _shard_shape(k)
        rows = math.prod(shard) // LANES
        out[k] = buf[..., r0:r0 + rows, :].reshape(lead + shard)
        r0 += rows + (-rows % PACK_ROWS)
    return out


class _LayerWeights:
    def __init__(self, shards):
        first, ffn0 = _layer_matrices(0)[:-2], _layer_matrices(0)[-2:]
        state, token = _gather_begin([shards[k][j].astype(BF16) for k, j in first] + [_pack_vectors(shards, ())],
                                     [_route(k) for k, _ in first] + [ROUTE_B], "gather0", None)
        self.cast = {k: (shards[k] + token[0, 0]).astype(BF16) for k in _MATRICES}
        state, _ = _gather_next(state, self._send(ffn0) + self._send(_layer_matrices(1)))
        state, _ = _gather_next(state, None)
        outs, _ = _gather_next(state, None)
        vec = _unpack_vectors(outs[-1], (N_DEV,))
        self.vec = {k: _full_from_slots(vec[k], k) for k in _VECTORS}
        self.vec["ffn_w_dw"] = [vec["ffn_w_dw"][:, l] for l in range(DEPTH)]
        self.ready = {0: self._unpack(first, outs[:-1])}
        self.chains = {}
        tokens = []
        self._begin("ffn0", ffn0, "gather0f", outs[0], tokens)
        self._begin(1, _layer_matrices(1), "gather1", outs[0], tokens)
        self.vec["norm_g"] = self.vec["norm_g"] + (tokens[0] + tokens[1])

    def _send(self, items):
        return [self.cast[k][j] for k, j in items]

    @staticmethod
    def _unpack(items, outs):
        return {k: o if k in _FFN else _full_from_slots(o[:, None], k, layers=1)[0] for (k, _), o in zip(items, outs)}

    def _begin(self, key, items, name, after, tokens):
        state, token = _gather_begin(self._send(items), [_route(k) for k, _ in items], name, after)
        self.chains[key] = (items, state)
        tokens.append(token)

    def _advance(self, key, after, tokens):
        items, state = self.chains.pop(key)
        state, token = _gather_next(state, after)
        if token is None:
            self.ready.setdefault(0 if key == "ffn0" else key, {}).update(self._unpack(items, state))
        else:
            self.chains[key] = (items, state)
            tokens.append(token)

    def layer(self, i):
        return self.ready[i]

    def hook(self, i, point, after):
        tokens = []
        if i == 0 and point == 0:
            self._advance("ffn0", after, tokens)
        if i == 0 and point == 1:
            self._advance("ffn0", after, tokens)
            self._advance("ffn0", None, tokens)
        if point >= 1 and i + 1 in self.chains:
            self._advance(i + 1, after, tokens)
        if point == 1 and i + 2 < DEPTH:
            self._begin(i + 2, _layer_matrices(i + 2), f"gather{i + 2}", after, tokens)
        return functools.reduce(lambda a, b: a + b, tokens) if tokens else None


def _layer_slots(g, name):
    shape, ax, _ = _WEIGHTS[name]
    shape, ax = shape[1:], ax - 1
    split = shape[:ax] + (N_DEV, shape[ax] // N_DEV) + shape[ax + 1:]
    return jnp.moveaxis(g.reshape(split), ax, 0).astype(BF16)


class _GradSink:
    def __init__(self):
        self.state = None
        self.ffn_state = None
        self.sums = {}
        self.last = None

    def ffn_done(self, i, grads):
        if i != 0:
            return None
        self.ffn_items = _layer_matrices(0)[-2:]
        self.ffn_state, token = _scatter_begin([grads[k] for k, _ in self.ffn_items],
                                               [_route(k) for k, _ in self.ffn_items],
                                               [f"{k}{j}" for k, j in self.ffn_items], "scatter0f")
        return token

    def layer_done(self, i, grads):
        items = _layer_matrices(i)
        if i == 0:
            items = items[:-2]
            self.last = (items, [_layer_slots(grads[k], k) for k, _ in items])
            return None
        slots = [grads[k] if k in _FFN else _layer_slots(grads[k], k) for k, _ in items]
        self.items = items
        self.state, token = _scatter_begin(slots, [_route(k) for k, _ in items], [f"{k}{j}" for k, j in items],
                                           f"scatter{i}", (N_DEV, 3, 2 * D_FF // N_DEV))
        return token

    def hook(self, i, point, after):
        tokens = []
        if self.state is not None and point in (1, 2, 3):
            self.state, token = _scatter_next(self.state, after)
            if point == 3:
                self.sums.update(dict(zip(self.items, self.state)))
                self.state = None
            tokens.append(token)
        if self.ffn_state is not None and point in ("a", "b", 3):
            self.ffn_state, token = _scatter_next(self.ffn_state, after)
            if point == 3:
                self.sums.update(dict(zip(self.ffn_items, self.ffn_state)))
                self.ffn_state = None
            tokens.append(token)
        tokens = [t for t in tokens if t is not None]
        return functools.reduce(lambda a, b: a + b, tokens) if tokens else None


def _adamw(w, g, m, v, name, layer=None, prev=None, dep=None):
    shape = w.shape
    cols = shape[-1]
    view = shape if len(shape) == 3 else (1, math.prod(shape[:-1]), cols)
    layers, rows, _ = view
    tr = _pick(rows, (512, 352, 288, 256, 128, 64, 32, 16, 8))
    n_prev = 0 if prev is None else 3
    lead = ([] if prev is None else [p.reshape(view) for p in prev]) + ([] if dep is None else [dep])

    def body(*refs):
        w_ref, g_ref, m_ref, v_ref = refs[len(lead):len(lead) + 4]
        d_ref, nm_ref, nv_ref = refs[len(lead) + 4:]
        gv = g_ref[...]
        nm = ADAM_B1 * m_ref[...] + (1.0 - ADAM_B1) * gv
        nv = ADAM_B2 * v_ref[...] + (1.0 - ADAM_B2) * jnp.square(gv)
        m_hat = nm / (1.0 - ADAM_B1 ** ADAM_STEP)
        v_hat = nv / (1.0 - ADAM_B2 ** ADAM_STEP)
        d_ref[...] = -ADAM_LR * (m_hat / (jnp.sqrt(v_hat) + ADAM_EPS) + ADAM_WD * w_ref[...])
        nm_ref[...] = nm
        nv_ref[...] = nv

    if layer is None:
        grid = (layers, rows // tr)
        blk = gblk = pl.BlockSpec((None, tr, cols), lambda l, i: (l, i, 0))
        gview = view
    else:
        grid = (rows // tr,)
        blk = pl.BlockSpec((None, tr, cols), lambda i: (layer, i, 0))
        gblk = pl.BlockSpec((tr, cols), lambda i: (i, 0))
        gview = (rows, cols)
    shp = jax.ShapeDtypeStruct(view, F32)
    outs = pl.pallas_call(
        body, name=name, grid=grid, in_specs=[_ANY] * len(lead) + [blk, gblk, blk, blk], out_specs=[blk] * 3,
        out_shape=[shp] * 3, input_output_aliases={i: i for i in range(n_prev)},
        compiler_params=_params(("parallel",) * len(grid)),
    )(*lead, w.reshape(view), g.reshape(gview), m.reshape(view), v.reshape(view))
    return [o.reshape(shape) for o in outs]


def kernel(x, norm_g, attn_w_qkv, attn_w_o, conv_w_in, conv_w_dw, conv_w_out, pool_w_in, pool_w_grp, pool_scale, pool_w_out, ffn_w_up, ffn_w_dw, ffn_w_down, loss_target, m_norm_g, m_attn_w_qkv, m_attn_w_o, m_conv_w_in, m_conv_w_dw, m_conv_w_out, m_pool_w_in, m_pool_w_grp, m_pool_scale, m_pool_w_out, m_ffn_w_up, m_ffn_w_dw, m_ffn_w_down, v_norm_g, v_attn_w_qkv, v_attn_w_o, v_conv_w_in, v_conv_w_dw, v_conv_w_out, v_pool_w_in, v_pool_w_grp, v_pool_scale, v_pool_w_out, v_ffn_w_up, v_ffn_w_dw, v_ffn_w_down):
    shards = dict(zip(_NAMES, (norm_g, attn_w_qkv, attn_w_o, conv_w_in, conv_w_dw, conv_w_out, pool_w_in,
                               pool_w_grp, pool_scale, pool_w_out, ffn_w_up, ffn_w_dw, ffn_w_down)))
    moms = dict(zip(_NAMES, (m_norm_g, m_attn_w_qkv, m_attn_w_o, m_conv_w_in, m_conv_w_dw, m_conv_w_out,
                             m_pool_w_in, m_pool_w_grp, m_pool_scale, m_pool_w_out, m_ffn_w_up, m_ffn_w_dw,
                             m_ffn_w_down)))
    vels = dict(zip(_NAMES, (v_norm_g, v_attn_w_qkv, v_attn_w_o, v_conv_w_in, v_conv_w_dw, v_conv_w_out,
                             v_pool_w_in, v_pool_w_grp, v_pool_scale, v_pool_w_out, v_ffn_w_up, v_ffn_w_dw,
                             v_ffn_w_down)))
    weights = _LayerWeights(shards)
    sink = _GradSink()
    loss, grad_x, vec_grads = _local_step(x[0], loss_target[0], weights.vec, weights, sink)
    loss = lax.psum(loss[0, 0], _AXES)

    items, slots = sink.last
    vec_slots = {k: _slots_from_full(vec_grads[k], k) for k in _VECTORS if k != "ffn_w_dw"}
    vec_slots["ffn_w_dw"] = jnp.stack(vec_grads["ffn_w_dw"], axis=1)
    state, token = _scatter_begin(slots + [_pack_vectors(vec_slots, (N_DEV,)).astype(BF16)],
                                  [_route(k) for k, _ in items] + [ROUTE_B],
                                  [f"{k}{j}" for k, j in items] + ["vectors"], "scatter0")
    results = {}

    flipped = {k for k in _MATRICES if _WEIGHTS[k][0][0] > 1 and _shard_shape(k)[-1] % LANES}
    wmv = {k: [t.transpose(0, 2, 1) if k in flipped else t for t in (shards[k], moms[k], vels[k])] for k in _MATRICES}

    def step(matrices, dep=None):
        outs = []
        for k, j in matrices:
            g = sink.sums[(k, j)]
            w, m, v = wmv[k]
            if _WEIGHTS[k][0][0] == 1:
                results[k] = (g[None], _adamw(w, g[None], m, v, f"adamw_{k}", dep=dep))
            else:
                gs, prev = results.get(k, ({}, None))
                gs[j] = g
                results[k] = (gs, _adamw(w, g.T if k in flipped else g, m, v, f"adamw_{k}{j}", layer=j, prev=prev, dep=dep))
            outs.append(results[k][1][0])
        return outs

    state, token = _scatter_next(state, step(_layer_matrices(3), token))
    state, token = _scatter_next(state, step(_layer_matrices(2) + _layer_matrices(1), token))
    sums, _ = _scatter_next(state, step(_layer_matrices(0)[-2:], token))
    sink.sums.update(dict(zip(items, sums[:-1])))
    step(items)
    vec_sums = _unpack_vectors(sums[-1], ())
    for k in _VECTORS:
        results[k] = (vec_sums[k], _adamw(shards[k], vec_sums[k], moms[k], vels[k], f"adamw_{k}"))
    grads_out = {k: g if not isinstance(g, dict) else jnp.stack([g[j] for j in range(len(g))])
                 for k, (g, _) in results.items()}
    stepped = {k: [o.transpose(0, 2, 1) if k in flipped else o for o in outs] for k, (_, outs) in results.items()}
    return (loss, grad_x[None], *[grads_out[k] for k in _NAMES], *[stepped[k][0] for k in _NAMES],
            *[stepped[k][1] for k in _NAMES], *[stepped[k][2] for k in _NAMES])
```

```python
import functools
import math

import numpy as np
import jax
import jax.numpy as jnp
from jax import lax
from jax.experimental import pallas as pl
from jax.experimental.pallas import tpu as pltpu

F32, BF16 = jnp.float32, jnp.bfloat16

D_MODEL = 1024
SEQ = 2048
DEPTH = 4
DILATED_CFG = ((128, 1), (512, 4), (2048, 16))
N_GROUPS_A = 3
HEADS = 8
HEAD_DIM = 64
ATTN_WIDTH = HEADS * HEAD_DIM
N_HEADS_A = N_GROUPS_A * HEADS
BLOCK = 128
NEG_INF = -1e30
POOL_GROUP_DIM = 256
D_FF = 2816
RMS_EPS = 1e-6
ADAM_LR, ADAM_B1, ADAM_B2, ADAM_EPS, ADAM_WD, ADAM_STEP = 0.001, 0.9, 0.999, 1e-08, 0.01, 10

N_DEV = 8
LANES = 128
V7X_VMEM_BYTES = 64 * 2 ** 20
VMEM_LIMIT_BYTES = V7X_VMEM_BYTES - 8 * 2 ** 20
COL_TILE = 256
ROW_TILE = 1024
MATMUL_TILES = (1024, 1408, 512, 256, 128)
TN_RESIDENT_K = 2048

NN = (((1,), (0,)), ((), ()))
NT = (((1,), (1,)), ((), ()))
TN = (((0,), (0,)), ((), ()))


def _dot(a, b, dims=NN):
    return lax.dot_general(a, b, dims, preferred_element_type=F32)


def _params(sem=None):
    return pltpu.CompilerParams(dimension_semantics=sem, vmem_limit_bytes=VMEM_LIMIT_BYTES)


def _pick(n, prefs):
    for p in prefs:
        if n % p == 0:
            return p
    return n


def _matmul(a, b, mode, out_dtype, name, a_parts=1, b_parts=1):
    if mode == "nn":
        m, k = a.shape[-2], a.shape[-1] * a_parts
        n = b.shape[-1] * b_parts
    elif mode == "nt":
        m, k = a.shape[-2], a.shape[-1] * a_parts
        n = b.shape[-2]
    else:
        k, m = a.shape[-2], a.shape[-1] * a_parts
        n = b.shape[-1] * b_parts
    tm = _pick(m, MATMUL_TILES)
    tn = _pick(n // b_parts if mode != "nt" else n, MATMUL_TILES)
    kk = k // a_parts if mode != "tn" else k
    tk = _pick(kk, MATMUL_TILES)
    if mode == "tn":
        tm = _pick(m // a_parts, MATMUL_TILES)
        if k <= TN_RESIDENT_K:
            tk = k
    gm, gn, gk = m // tm, n // tn, k // tk

    def a_idx(i, j, kq):
        if mode == "tn":
            r, c, per = kq, i, (m // a_parts) // tm
        else:
            r, c, per = i, kq, (k // a_parts) // tk
        return (r, c) if a_parts == 1 else (c // per, r, c % per)

    def b_idx(i, j, kq):
        if mode == "nt":
            return (j, kq)
        per = (n // b_parts) // tn
        return (kq, j) if b_parts == 1 else (j // per, kq, j % per)

    a_blk = (tk, tm) if mode == "tn" else (tm, tk)
    b_blk = (tn, tk) if mode == "nt" else (tk, tn)
    if a_parts > 1:
        a_blk = (None,) + a_blk
    if b_parts > 1:
        b_blk = (None,) + b_blk
    dims = {"nn": NN, "nt": NT, "tn": TN}[mode]

    def body_single(a_ref, b_ref, o_ref):
        o_ref[...] = _dot(a_ref[...], b_ref[...], dims).astype(o_ref.dtype)

    def body(a_ref, b_ref, o_ref, acc_ref):
        kq = pl.program_id(2)

        @pl.when(kq == 0)
        def _():
            acc_ref[...] = jnp.zeros_like(acc_ref)

        acc_ref[...] += _dot(a_ref[...], b_ref[...], dims)

        @pl.when(kq == gk - 1)
        def _():
            o_ref[...] = acc_ref[...].astype(o_ref.dtype)

    return pl.pallas_call(
        body_single if gk == 1 else body, name=name, grid=(gm, gn, gk),
        in_specs=[pl.BlockSpec(a_blk, a_idx), pl.BlockSpec(b_blk, b_idx)],
        out_specs=pl.BlockSpec((tm, tn), lambda i, j, kq: (i, j)),
        out_shape=jax.ShapeDtypeStruct((m, n), out_dtype),
        scratch_shapes=[] if gk == 1 else [pltpu.VMEM((tm, tn), F32)],
        compiler_params=_params(("parallel", "parallel", "arbitrary")),
    )(a, b)


def _rms_fwd(xin, g, res, out_dtype, name):
    s, d = xin.shape
    has_res = res is not None

    def body(*refs):
        x_ref, g_ref = refs[0], refs[1]
        o_ref = refs[-1]
        x = x_ref[...]
        r = lax.rsqrt(jnp.mean(x * x, axis=-1, keepdims=True) + RMS_EPS)
        y = x * r * g_ref[...]
        if has_res:
            y = refs[2][...] + y
        o_ref[...] = y.astype(o_ref.dtype)

    row = pl.BlockSpec((ROW_TILE, d), lambda i: (i, 0))
    vec = pl.BlockSpec((1, d), lambda i: (0, 0))
    ins = [xin, g] + ([res] if has_res else [])
    return pl.pallas_call(
        body, name=name, grid=(s // ROW_TILE,),
        in_specs=[row, vec] + ([row] if has_res else []),
        out_specs=row, out_shape=jax.ShapeDtypeStruct((s, d), out_dtype),
        compiler_params=_params(("parallel",)),
    )(*ins)


def _rms_bwd(xin, g, dy, dres, out_dtype, name):
    s, d = xin.shape
    has_res = dres is not None

    def body(*refs):
        x_ref, g_ref, dy_ref = refs[0], refs[1], refs[2]
        dx_ref, dg_ref = refs[-2], refs[-1]

        @pl.when(pl.program_id(0) == 0)
        def _():
            dg_ref[...] = jnp.zeros_like(dg_ref)

        x = x_ref[...]
        dyv = dy_ref[...].astype(F32)
        r = lax.rsqrt(jnp.mean(x * x, axis=-1, keepdims=True) + RMS_EPS)
        xhat = x * r
        u = dyv * g_ref[...]
        dx = r * (u - xhat * jnp.mean(u * xhat, axis=-1, keepdims=True))
        if has_res:
            dx = refs[3][...] + dx
        dx_ref[...] = dx.astype(dx_ref.dtype)
        dg_ref[...] += jnp.sum(dyv * xhat, axis=0, keepdims=True)

    row = pl.BlockSpec((ROW_TILE, d), lambda i: (i, 0))
    vec = pl.BlockSpec((1, d), lambda i: (0, 0))
    ins = [xin, g, dy] + ([dres] if has_res else [])
    return pl.pallas_call(
        body, name=name, grid=(s // ROW_TILE,),
        in_specs=[row, vec, row] + ([row] if has_res else []),
        out_specs=[row, vec],
        out_shape=[jax.ShapeDtypeStruct((s, d), out_dtype), jax.ShapeDtypeStruct((1, d), F32)],
        compiler_params=_params(("arbitrary",)),
    )(*ins)


def _rms(x):
    r = lax.rsqrt(jnp.mean(x * x, axis=-1, keepdims=True) + RMS_EPS)
    return r, x * r


def _rms_grad(r, xhat, dy, g):
    u = dy * g
    return r * (u - xhat * jnp.mean(u * xhat, axis=-1, keepdims=True))


def _rms_res_pre(sub, g_post, res, g_pre, name):
    s, d = sub.shape

    def body(sub_ref, gp_ref, res_ref, gn_ref, x_ref, n_ref):
        xnew = res_ref[...] + _rms(sub_ref[...])[1] * gp_ref[...]
        x_ref[...] = xnew
        n_ref[...] = (_rms(xnew)[1] * gn_ref[...]).astype(BF16)

    row = pl.BlockSpec((ROW_TILE, d), lambda i: (i, 0))
    vec = pl.BlockSpec((1, d), lambda i: (0, 0))
    return pl.pallas_call(
        body, name=name, grid=(s // ROW_TILE,),
        in_specs=[row, vec, row, vec], out_specs=[row, row],
        out_shape=[jax.ShapeDtypeStruct((s, d), F32), jax.ShapeDtypeStruct((s, d), BF16)],
        compiler_params=_params(("parallel",)),
    )(sub, g_post, res, g_pre)


def _rms_bwd_pair(xmid, g_pre, dn, dres, sub, g_post, name):
    s, d = xmid.shape

    def body(x_ref, gn_ref, dn_ref, dres_ref, sub_ref, gp_ref, dx_ref, dsub_ref, dgn_ref, dgp_ref):
        @pl.when(pl.program_id(0) == 0)
        def _():
            dgn_ref[...] = jnp.zeros_like(dgn_ref)
            dgp_ref[...] = jnp.zeros_like(dgp_ref)

        dnv = dn_ref[...].astype(F32)
        r, xhat = _rms(x_ref[...])
        dx = dres_ref[...] + _rms_grad(r, xhat, dnv, gn_ref[...])
        dx_ref[...] = dx
        dgn_ref[...] += jnp.sum(dnv * xhat, axis=0, keepdims=True)
        rs, shat = _rms(sub_ref[...])
        dsub_ref[...] = _rms_grad(rs, shat, dx, gp_ref[...]).astype(BF16)
        dgp_ref[...] += jnp.sum(dx * shat, axis=0, keepdims=True)

    row = pl.BlockSpec((ROW_TILE, d), lambda i: (i, 0))
    vec = pl.BlockSpec((1, d), lambda i: (0, 0))
    return pl.pallas_call(
        body, name=name, grid=(s // ROW_TILE,),
        in_specs=[row, vec, row, row, row, vec], out_specs=[row, row, vec, vec],
        out_shape=[jax.ShapeDtypeStruct((s, d), F32), jax.ShapeDtypeStruct((s, d), BF16),
                   jax.ShapeDtypeStruct((1, d), F32), jax.ShapeDtypeStruct((1, d), F32)],
        compiler_params=_params(("arbitrary",)),
    )(xmid, g_pre, dn, dres, sub, g_post)


def _loss_head(y, tgt, name):
    s, d = y.shape

    def body(y_ref, t_ref, l_ref, dy_ref):
        @pl.when(pl.program_id(0) == 0)
        def _():
            l_ref[...] = jnp.zeros_like(l_ref)

        e = y_ref[...] - t_ref[...]
        dy_ref[...] = e / d
        per_tok = jnp.mean(e * e, axis=-1, keepdims=True)
        l_ref[...] += 0.5 * jnp.sum(per_tok, axis=0, keepdims=True)

    row = pl.BlockSpec((ROW_TILE, d), lambda i: (i, 0))
    return pl.pallas_call(
        body, name=name, grid=(s // ROW_TILE,),
        in_specs=[row, row],
        out_specs=[pl.BlockSpec((1, 1), lambda i: (0, 0)), row],
        out_shape=[jax.ShapeDtypeStruct((1, 1), F32), jax.ShapeDtypeStruct((s, d), F32)],
        compiler_params=_params(("arbitrary",)),
    )(y, tgt)


SUBLANES = 8


def _shift_down(x, k):
    t, c = x.shape
    r = pltpu.roll(x.reshape(t // SUBLANES, SUBLANES, c), k, axis=1)
    above = jnp.concatenate([jnp.zeros((1, SUBLANES, c), x.dtype), r[:-1]], axis=0)
    rows = lax.broadcasted_iota(jnp.int32, (1, SUBLANES, c), 1)
    return jnp.where(rows >= k, r, above).reshape(t, c)


def _shift_up(x, k):
    t, c = x.shape
    r = pltpu.roll(x.reshape(t // SUBLANES, SUBLANES, c), SUBLANES - k, axis=1)
    below = jnp.concatenate([r[1:], jnp.zeros((1, SUBLANES, c), x.dtype)], axis=0)
    rows = lax.broadcasted_iota(jnp.int32, (1, SUBLANES, c), 1)
    return jnp.where(rows < SUBLANES - k, r, below).reshape(t, c)


def _conv3(h, w):
    return w[2:3] * h + w[1:2] * _shift_down(h, 1) + w[0:1] * _shift_down(h, 2)


def _conv3_bwd(dc, h, w, dw_ref, cols=slice(None)):
    u1, u2 = _shift_up(dc, 1), _shift_up(dc, 2)
    dw_ref[0:1, cols] = jnp.sum(u2 * h, axis=0, keepdims=True)
    dw_ref[1:2, cols] = jnp.sum(u1 * h, axis=0, keepdims=True)
    dw_ref[2:3, cols] = jnp.sum(dc * h, axis=0, keepdims=True)
    return w[2:3] * dc + w[1:2] * u1 + w[0:1] * u2


FFN_PAIRS = N_DEV // 2


def _lane_chunks(width):
    return [(c0, min(COL_TILE, width - c0)) for c0 in range(0, width, COL_TILE)]


def _ffn_up(n, wup, wdw, name):
    s, d = n.shape
    cw = wup.shape[-1]

    def body(n_ref, wg_ref, wu_ref, dg_ref, du_ref, h_ref, c_ref, a_ref):
        x = n_ref[...]
        for c0, size in _lane_chunks(cw):
            cols = slice(c0, c0 + size)
            hg = _dot(x, wg_ref[:, cols])
            hu = _dot(x, wu_ref[:, cols])
            h_ref[0, :, cols] = hg.astype(BF16)
            h_ref[1, :, cols] = hu.astype(BF16)
            cg = _conv3(hg, dg_ref[:, cols])
            cu = _conv3(hu, du_ref[:, cols])
            c_ref[0, :, cols] = cg.astype(BF16)
            c_ref[1, :, cols] = cu.astype(BF16)
            a_ref[:, cols] = (cg * jax.nn.sigmoid(cg) * cu).astype(BF16)

    return pl.pallas_call(
        body, name=name, grid=(FFN_PAIRS,),
        in_specs=[pl.BlockSpec((s, d), lambda j: (0, 0)),
                  pl.BlockSpec((None, d, cw), lambda j: (j, 0, 0)),
                  pl.BlockSpec((None, d, cw), lambda j: (j + FFN_PAIRS, 0, 0)),
                  pl.BlockSpec((None, 3, cw), lambda j: (j, 0, 0)),
                  pl.BlockSpec((None, 3, cw), lambda j: (j + FFN_PAIRS, 0, 0))],
        out_specs=[pl.BlockSpec((None, 2, s, cw), lambda j: (j, 0, 0, 0)),
                   pl.BlockSpec((None, 2, s, cw), lambda j: (j, 0, 0, 0)),
                   pl.BlockSpec((None, s, cw), lambda j: (j, 0, 0))],
        out_shape=[jax.ShapeDtypeStruct((FFN_PAIRS, 2, s, cw), BF16), jax.ShapeDtypeStruct((FFN_PAIRS, 2, s, cw), BF16),
                   jax.ShapeDtypeStruct((FFN_PAIRS, s, cw), BF16)],
        compiler_params=_params(("parallel",)),
    )(n, wup, wup, wdw, wdw)


def _ffn_mid_bwd(do, wdown, h, c, wdw, name):
    s, d = do.shape
    cw = wdown.shape[1]

    def body(do_ref, wd_ref, h_ref, c_ref, wg_ref, wu_ref, dh_ref, dwg_ref, dwu_ref):
        dov = do_ref[...]
        for c0, size in _lane_chunks(cw):
            cols = slice(c0, c0 + size)
            da = _dot(dov, wd_ref[cols, :], NT)
            hg = h_ref[0, :, cols].astype(F32)
            hu = h_ref[1, :, cols].astype(F32)
            wg, wu = wg_ref[:, cols], wu_ref[:, cols]
            cg = c_ref[0, :, cols].astype(F32)
            cu = c_ref[1, :, cols].astype(F32)
            sg = jax.nn.sigmoid(cg)
            dcu = da * (cg * sg)
            dcg = da * cu * (sg * (1.0 + cg * (1.0 - sg)))
            dh_ref[0, :, cols] = _conv3_bwd(dcg, hg, wg, dwg_ref, cols).astype(BF16)
            dh_ref[1, :, cols] = _conv3_bwd(dcu, hu, wu, dwu_ref, cols).astype(BF16)

    vec = jax.ShapeDtypeStruct((FFN_PAIRS, 3, cw), F32)
    return pl.pallas_call(
        body, name=name, grid=(FFN_PAIRS,),
        in_specs=[pl.BlockSpec((s, d), lambda j: (0, 0)), pl.BlockSpec((None, cw, d), lambda j: (j, 0, 0)),
                  pl.BlockSpec((None, 2, s, cw), lambda j: (j, 0, 0, 0)),
                  pl.BlockSpec((None, 2, s, cw), lambda j: (j, 0, 0, 0)),
                  pl.BlockSpec((None, 3, cw), lambda j: (j, 0, 0)),
                  pl.BlockSpec((None, 3, cw), lambda j: (j + FFN_PAIRS, 0, 0))],
        out_specs=[pl.BlockSpec((None, 2, s, cw), lambda j: (j, 0, 0, 0)),
                   pl.BlockSpec((None, 3, cw), lambda j: (j, 0, 0)), pl.BlockSpec((None, 3, cw), lambda j: (j, 0, 0))],
        out_shape=[jax.ShapeDtypeStruct((FFN_PAIRS, 2, s, cw), BF16), vec, vec],
        compiler_params=_params(("parallel",)),
    )(do, wdown, h, c, wdw, wdw)


def _ffn_dwup(n, dh, name, dep=None):
    s, d = n.shape
    cw = dh.shape[-1]
    deps = [] if dep is None else [dep]

    def body(n_ref, dh_ref, *rest):
        rest[-1][...] = _dot(n_ref[...], dh_ref[...], TN).astype(BF16)

    return pl.pallas_call(
        body, name=name, grid=(N_DEV,),
        in_specs=[pl.BlockSpec((s, d), lambda k: (0, 0)),
                  pl.BlockSpec((None, None, s, cw), lambda k: (k % FFN_PAIRS, k // FFN_PAIRS, 0, 0))] + [_ANY] * len(deps),
        out_specs=pl.BlockSpec((None, d, cw), lambda k: (k, 0, 0)),
        out_shape=jax.ShapeDtypeStruct((N_DEV, d, cw), BF16),
        compiler_params=_params(("parallel",)),
    )(n, dh, *deps)


def _ffn_dn(dh, wup, name, dep=None):
    s, cw = dh.shape[-2:]
    d = wup.shape[1]
    tm = _pick(s, MATMUL_TILES)
    deps = [] if dep is None else [dep]

    def body(dh_ref, w_ref, *rest):
        o_ref, acc_ref = rest[-2:]
        k = pl.program_id(1)

        @pl.when(k == 0)
        def _():
            acc_ref[...] = jnp.zeros_like(acc_ref)

        acc_ref[...] += _dot(dh_ref[...], w_ref[...], NT)

        @pl.when(k == N_DEV - 1)
        def _():
            o_ref[...] = acc_ref[...]

    return pl.pallas_call(
        body, name=name, grid=(s // tm, N_DEV),
        in_specs=[pl.BlockSpec((None, None, tm, cw), lambda i, k: (k % FFN_PAIRS, k // FFN_PAIRS, i, 0)),
                  pl.BlockSpec((None, d, cw), lambda i, k: (k, 0, 0))] + [_ANY] * len(deps),
        out_specs=pl.BlockSpec((tm, d), lambda i, k: (i, 0)),
        out_shape=jax.ShapeDtypeStruct((s, d), F32),
        scratch_shapes=[pltpu.VMEM((tm, d), F32)],
        compiler_params=_params(("parallel", "arbitrary")),
    )(dh, wup, *deps)


def _sconv_fwd(n, win, wdw, name):
    s, d = n.shape
    tn = COL_TILE
    nj = d // tn

    def body(n_ref, wb_ref, wc_ref, wh_ref, dw_ref, z_ref, y_ref):
        x = n_ref[...]
        zb = _dot(x, wb_ref[...])
        zc = _dot(x, wc_ref[...])
        zh = _dot(x, wh_ref[...])
        z_ref[0] = zb.astype(BF16)
        z_ref[1] = zc.astype(BF16)
        z_ref[2] = zh.astype(BF16)
        y_ref[...] = (zb * _conv3(zc * zh, dw_ref[...])).astype(BF16)

    return pl.pallas_call(
        body, name=name, grid=(nj,),
        in_specs=[pl.BlockSpec((s, d), lambda j: (0, 0)),
                  pl.BlockSpec((d, tn), lambda j: (0, j)), pl.BlockSpec((d, tn), lambda j: (0, j + nj)),
                  pl.BlockSpec((d, tn), lambda j: (0, j + 2 * nj)), pl.BlockSpec((3, tn), lambda j: (0, j))],
        out_specs=[pl.BlockSpec((3, s, tn), lambda j: (0, 0, j)), pl.BlockSpec((s, tn), lambda j: (0, j))],
        out_shape=[jax.ShapeDtypeStruct((3, s, d), BF16), jax.ShapeDtypeStruct((s, d), BF16)],
        compiler_params=_params(("parallel",)),
    )(n, win, win, win, wdw)


def _sconv_mid_bwd(dm, wout, z, wdw, name):
    s, d = dm.shape
    tn = COL_TILE
    nj = d // tn

    def body(dm_ref, wo_ref, z_ref, w_ref, dz_ref, dw_ref):
        dy = _dot(dm_ref[...], wo_ref[...], NT)
        zb = z_ref[0].astype(F32)
        zc = z_ref[1].astype(F32)
        zh = z_ref[2].astype(F32)
        w = w_ref[...]
        p = zc * zh
        cp = _conv3(p, w)
        dz_ref[0] = (dy * cp).astype(BF16)
        dcp = dy * zb
        dp = _conv3_bwd(dcp, p, w, dw_ref)
        dz_ref[1] = (dp * zh).astype(BF16)
        dz_ref[2] = (dp * zc).astype(BF16)

    return pl.pallas_call(
        body, name=name, grid=(nj,),
        in_specs=[pl.BlockSpec((s, d), lambda j: (0, 0)), pl.BlockSpec((tn, d), lambda j: (j, 0)),
                  pl.BlockSpec((3, s, tn), lambda j: (0, 0, j)), pl.BlockSpec((3, tn), lambda j: (0, j))],
        out_specs=[pl.BlockSpec((3, s, tn), lambda j: (0, 0, j)), pl.BlockSpec((3, tn), lambda j: (0, j))],
        out_shape=[jax.ShapeDtypeStruct((3, s, d), BF16), jax.ShapeDtypeStruct((3, d), F32)],
        compiler_params=_params(("parallel",)),
    )(dm, wout, z, wdw)


def _pool_select(g, c2, c4, c8, c16):
    return jnp.where(g == 0, c2, jnp.where(g == 1, c4, jnp.where(g == 2, c8, c16)))


def _pool_inv_count(g, shape):
    pos = lax.broadcasted_iota(jnp.int32, shape, 0).astype(F32) + 1.0
    win = (2 << g).astype(F32)
    return jnp.minimum(pos, win)


def _pool_fwd(n, win, wgrp, scale, name):
    s, d = n.shape
    tn = POOL_GROUP_DIM

    def body(n_ref, wi_ref, wg_ref, sc_ref, p_ref, y_ref):
        g = pl.program_id(0)
        u = _dot(n_ref[...], wi_ref[...])
        s2 = u + _shift_down(u, 1)
        s4 = s2 + _shift_down(s2, 2)
        s8 = s4 + _shift_down(s4, 4)
        s16 = s8 + _shift_down(s8, 8)
        tot = _pool_select(g, s2, s4, s8, s16)
        p = (tot / _pool_inv_count(g, u.shape) - u).astype(BF16)
        p_ref[...] = p
        y_ref[...] = (_dot(p, wg_ref[...]) * sc_ref[...]).astype(BF16)

    return pl.pallas_call(
        body, name=name, grid=(d // tn,),
        in_specs=[pl.BlockSpec((s, d), lambda g: (0, 0)), pl.BlockSpec((d, tn), lambda g: (0, g)),
                  pl.BlockSpec((None, tn, tn), lambda g: (g, 0, 0)), pl.BlockSpec((1, tn), lambda g: (0, g))],
        out_specs=[pl.BlockSpec((s, tn), lambda g: (0, g)), pl.BlockSpec((s, tn), lambda g: (0, g))],
        out_shape=[jax.ShapeDtypeStruct((s, d), BF16), jax.ShapeDtypeStruct((s, d), BF16)],
        compiler_params=_params(("parallel",)),
    )(n, win, wgrp, scale)


def _pool_mid_bwd(dm, wout, p, wgrp, scale, name):
    s, d = dm.shape
    tn = POOL_GROUP_DIM

    def body(dm_ref, wo_ref, p_ref, wg_ref, sc_ref, du_ref, dwg_ref, dsc_ref):
        g = pl.program_id(0)
        dy = _dot(dm_ref[...], wo_ref[...], NT)
        pv = p_ref[...]
        wg = wg_ref[...]
        ypre = _dot(pv, wg)
        dsc_ref[...] = jnp.sum(dy * ypre, axis=0, keepdims=True)
        dypre = (dy * sc_ref[...]).astype(BF16)
        dwg_ref[...] = _dot(pv, dypre, TN)
        dp = _dot(dypre, wg, NT)
        e = dp / _pool_inv_count(g, dp.shape)
        f2 = e + _shift_up(e, 1)
        f4 = f2 + _shift_up(f2, 2)
        f8 = f4 + _shift_up(f4, 4)
        f16 = f8 + _shift_up(f8, 8)
        du_ref[...] = (_pool_select(g, f2, f4, f8, f16) - dp).astype(BF16)

    return pl.pallas_call(
        body, name=name, grid=(d // tn,),
        in_specs=[pl.BlockSpec((s, d), lambda g: (0, 0)), pl.BlockSpec((tn, d), lambda g: (g, 0)),
                  pl.BlockSpec((s, tn), lambda g: (0, g)), pl.BlockSpec((None, tn, tn), lambda g: (g, 0, 0)),
                  pl.BlockSpec((1, tn), lambda g: (0, g))],
        out_specs=[pl.BlockSpec((s, tn), lambda g: (0, g)), pl.BlockSpec((None, tn, tn), lambda g: (g, 0, 0)),
                   pl.BlockSpec((1, tn), lambda g: (0, g))],
        out_shape=[jax.ShapeDtypeStruct((s, d), BF16), jax.ShapeDtypeStruct((4, tn, tn), F32),
                   jax.ShapeDtypeStruct((1, d), F32)],
        compiler_params=_params(("parallel",)),
    )(dm, wout, p, wgrp, scale)


PANEL = LANES
ATTN_EXT = ATTN_WIDTH + PANEL
DVEC_LANE = HEADS


def _alibi_slopes(g, dil):
    all_slopes = 2.0 ** (-8.0 * np.arange(1, N_HEADS_A + 1) / N_HEADS_A)
    return [float(np.float32(sl) * np.float32(dil)) for sl in all_slopes[g * HEADS:(g + 1) * HEADS]]


def _residue_order(a, dil, name):
    s, w = a.shape
    per = ROW_TILE // dil
    panels = w // PANEL

    def body(a_ref, o_ref, *tiles):
        for c in range(panels):
            cols = slice(c * PANEL, (c + 1) * PANEL)
            tiles[c][...] = a_ref[:, cols].astype(F32)
            for r in range(dil):
                o_ref[r, :, cols] = tiles[c][pl.ds(r, per, stride=dil), :].astype(o_ref.dtype)

    out = pl.pallas_call(
        body, name=name, grid=(s // ROW_TILE,),
        in_specs=[pl.BlockSpec((ROW_TILE, w), lambda i: (i, 0))],
        out_specs=pl.BlockSpec((dil, per, w), lambda i: (0, i, 0)),
        out_shape=jax.ShapeDtypeStruct((dil, s // dil, w), a.dtype),
        scratch_shapes=[pltpu.VMEM((ROW_TILE, PANEL), F32)] * panels,
        compiler_params=_params(("parallel",)),
    )(a)
    return out.reshape(s, w)


def _token_order(a, dil, acc, name):
    s, w = a.shape
    per = ROW_TILE // dil
    panels = w // PANEL
    has_acc = acc is not None

    def body(*refs):
        a_ref = refs[0]
        o_ref = refs[2] if has_acc else refs[1]
        tiles = refs[3:] if has_acc else refs[2:]
        for c in range(panels):
            cols = slice(c * PANEL, (c + 1) * PANEL)
            for r in range(dil):
                tiles[c][pl.ds(r, per, stride=dil), :] = a_ref[r, :, cols]
            v = tiles[c][...]
            if has_acc:
                v = v + refs[1][:, cols]
            o_ref[:, cols] = v

    row = pl.BlockSpec((ROW_TILE, w), lambda i: (i, 0))
    return pl.pallas_call(
        body, name=name, grid=(s // ROW_TILE,),
        in_specs=[pl.BlockSpec((dil, per, w), lambda i: (0, i, 0))] + ([row] if has_acc else []),
        out_specs=row, out_shape=jax.ShapeDtypeStruct((s, w), F32),
        scratch_shapes=[pltpu.VMEM((ROW_TILE, PANEL), F32)] * panels,
        compiler_params=_params(("parallel",)),
    )(*([a.reshape(dil, s // dil, w)] + ([acc] if has_acc else [])))


def _qkv_proj(n, wqkv, g, name):
    s, d = n.shape
    tm = _pick(s, MATMUL_TILES)

    def body(a_ref, b_ref, o_ref):
        o_ref[...] = _dot(a_ref[...], b_ref[...]).astype(BF16)

    return pl.pallas_call(
        body, name=name, grid=(s // tm, 3),
        in_specs=[pl.BlockSpec((tm, d), lambda i, t: (i, 0)),
                  pl.BlockSpec((d, ATTN_WIDTH), lambda i, t: (0, 3 * g + t))],
        out_specs=pl.BlockSpec((None, tm, ATTN_WIDTH), lambda i, t: (t, i, 0)),
        out_shape=jax.ShapeDtypeStruct((3, s, ATTN_WIDTH), BF16),
        compiler_params=_params(("parallel", "parallel")),
    )(n, wqkv)


def _attn_window(n, ln):
    if ln == BLOCK:
        return 0, BLOCK
    return pl.multiple_of(jnp.maximum(n - 1, 0) * BLOCK, BLOCK), 2 * BLOCK


def _attn_mask(n, k0, kw):
    qpos = n * BLOCK + lax.broadcasted_iota(jnp.int32, (BLOCK, kw), 0)
    kpos = k0 + lax.broadcasted_iota(jnp.int32, (BLOCK, kw), 1)
    dist = qpos - kpos
    return dist.astype(F32), (dist >= 0) & (dist <= BLOCK)


def _attn_scores(q, keys, slope, dist, valid):
    s = _dot(q, keys, NT) * (HEAD_DIM ** -0.5) - slope * dist
    return jnp.where(valid, s, NEG_INF)


ATTN_STEP_BLOCKS = 1
ATTN_BWD_STEP_BLOCKS = 4


def _attn_block(gb, ln):
    nb = ln // BLOCK
    n, base = (0, gb * ln) if nb == 1 else (gb % nb, (gb // nb) * ln)
    k0, kw = _attn_window(n, ln)
    cur = pl.ds(pl.multiple_of(gb * BLOCK, BLOCK), BLOCK)
    win = pl.ds(pl.multiple_of(base + k0, BLOCK), kw)
    return cur, win, n, k0, kw


def _attn_fwd(qkv, g, name):
    _, s, w = qkv.shape
    dil = DILATED_CFG[g][1]
    ln = s // dil
    slopes = _alibi_slopes(g, dil)
    rows = ATTN_STEP_BLOCKS * BLOCK

    def body(qkv_ref, o_ref):
        o_ref[:, w:] = jnp.zeros((rows, PANEL), F32)
        for b in range(ATTN_STEP_BLOCKS):
            cur, win, n, k0, kw = _attn_block(pl.program_id(0) * ATTN_STEP_BLOCKS + b, ln)
            dist, valid = _attn_mask(n, k0, kw)
            out = slice(b * BLOCK, (b + 1) * BLOCK)
            for h in range(HEADS):
                cols = slice(h * HEAD_DIM, (h + 1) * HEAD_DIM)
                sc = _attn_scores(qkv_ref[0, cur, cols], qkv_ref[1, win, cols], slopes[h], dist, valid)
                m = jnp.max(sc, axis=-1, keepdims=True)
                p = jnp.exp(sc - m)
                den = jnp.sum(p, axis=-1, keepdims=True)
                o_ref[out, cols] = _dot(p.astype(BF16), qkv_ref[2, win, cols]) / den
                o_ref[out, w + h:w + h + 1] = m + jnp.log(den)

    return pl.pallas_call(
        body, name=name, grid=(s // rows,),
        in_specs=[pl.BlockSpec((3, s, w), lambda i: (0, 0, 0))],
        out_specs=pl.BlockSpec((rows, ATTN_EXT), lambda i: (i, 0)),
        out_shape=jax.ShapeDtypeStruct((s, ATTN_EXT), F32),
        compiler_params=_params(("parallel",)),
    )(qkv)


def _attn_bwd(qkv, dext, g, name, dep=None):
    _, s, w = qkv.shape
    dil = DILATED_CFG[g][1]
    ln = s // dil
    slopes = _alibi_slopes(g, dil)
    scale = HEAD_DIM ** -0.5
    rows = ATTN_BWD_STEP_BLOCKS * BLOCK
    steps = s // rows
    deps = [] if dep is None else [dep]

    def body(qkv_ref, de_ref, *rest):
        d_ref, dk_ref, dv_ref = rest[-3:]

        @pl.when(pl.program_id(0) == 0)
        def _():
            dk_ref[...] = jnp.zeros_like(dk_ref)
            dv_ref[...] = jnp.zeros_like(dv_ref)

        for b in range(ATTN_BWD_STEP_BLOCKS):
            cur, win, n, k0, kw = _attn_block(pl.program_id(0) * ATTN_BWD_STEP_BLOCKS + b, ln)
            dist, valid = _attn_mask(n, k0, kw)
            blk = slice(b * BLOCK, (b + 1) * BLOCK)
            for h in range(HEADS):
                cols = slice(h * HEAD_DIM, (h + 1) * HEAD_DIM)
                q, keys = qkv_ref[0, cur, cols], qkv_ref[1, win, cols]
                dob = de_ref[blk, cols].astype(BF16)
                p = jnp.exp(_attn_scores(q, keys, slopes[h], dist, valid) - de_ref[blk, w + h:w + h + 1])
                dd = de_ref[blk, w + DVEC_LANE + h:w + DVEC_LANE + h + 1]
                ds = (p * (_dot(dob, qkv_ref[2, win, cols], NT) - dd)).astype(BF16)
                d_ref[0, cur, cols] = (scale * _dot(ds, keys)).astype(BF16)
                dv_ref[win, cols] += _dot(p.astype(BF16), dob, TN)
                dk_ref[win, cols] += scale * _dot(ds, q, TN)

        @pl.when(pl.program_id(0) == steps - 1)
        def _():
            d_ref[1] = dk_ref[...].astype(BF16)
            d_ref[2] = dv_ref[...].astype(BF16)

    whole = pl.BlockSpec((3, s, w), lambda i: (0, 0, 0))
    return pl.pallas_call(
        body, name=name, grid=(steps,),
        in_specs=[whole, pl.BlockSpec((rows, ATTN_EXT), lambda i: (i, 0))] + [_ANY] * len(deps),
        out_specs=whole, out_shape=jax.ShapeDtypeStruct((3, s, w), BF16),
        scratch_shapes=[pltpu.VMEM((s, w), F32), pltpu.VMEM((s, w), F32)],
        compiler_params=_params(("arbitrary",)),
    )(qkv, dext, *deps)


def _attn_merge(e0, e1, e2, name):
    s = e0.shape[0]
    w = ATTN_WIDTH

    def body(e0_ref, e1_ref, e2_ref, m_ref, mb_ref, lse_ref):
        refs = (e0_ref, e1_ref, e2_ref)
        l = [r[:, w:w + HEADS] for r in refs]
        mx = jnp.maximum(jnp.maximum(l[0], l[1]), l[2])
        e = [jnp.exp(v - mx) for v in l]
        z = e[0] + e[1] + e[2]
        lse_ref[...] = mx + jnp.log(z)
        wts = [v / z for v in e]
        for h in range(HEADS):
            cols = slice(h * HEAD_DIM, (h + 1) * HEAD_DIM)
            acc = wts[0][:, h:h + 1] * refs[0][:, cols]
            for g in range(1, N_GROUPS_A):
                acc = acc + wts[g][:, h:h + 1] * refs[g][:, cols]
            m_ref[:, cols] = acc
            mb_ref[:, cols] = acc.astype(BF16)

    ext = pl.BlockSpec((ROW_TILE, ATTN_EXT), lambda i: (i, 0))
    row = pl.BlockSpec((ROW_TILE, w), lambda i: (i, 0))
    return pl.pallas_call(
        body, name=name, grid=(s // ROW_TILE,),
        in_specs=[ext, ext, ext],
        out_specs=[row, row, pl.BlockSpec((ROW_TILE, HEADS), lambda i: (i, 0))],
        out_shape=[jax.ShapeDtypeStruct((s, w), F32), jax.ShapeDtypeStruct((s, w), BF16),
                   jax.ShapeDtypeStruct((s, HEADS), F32)],
        compiler_params=_params(("parallel",)),
    )(e0, e1, e2)


def _attn_dvec(dmerged, merged, lse_all, name, dep=None):
    s, w = merged.shape
    deps = [] if dep is None else [dep]

    def body(dm_ref, m_ref, lse_ref, *rest):
        de_ref = rest[-1]
        dmv = dm_ref[...]
        de_ref[:, :w] = dmv
        de_ref[:, w:] = jnp.zeros((ROW_TILE, PANEL), F32)
        de_ref[:, w:w + HEADS] = lse_ref[...]
        prod = dmv * m_ref[...]
        for h in range(HEADS):
            lane = w + DVEC_LANE + h
            de_ref[:, lane:lane + 1] = jnp.sum(prod[:, h * HEAD_DIM:(h + 1) * HEAD_DIM], axis=-1, keepdims=True)

    row = pl.BlockSpec((ROW_TILE, w), lambda i: (i, 0))
    return pl.pallas_call(
        body, name=name, grid=(s // ROW_TILE,),
        in_specs=[row, row, pl.BlockSpec((ROW_TILE, HEADS), lambda i: (i, 0))] + [_ANY] * len(deps),
        out_specs=pl.BlockSpec((ROW_TILE, ATTN_EXT), lambda i: (i, 0)),
        out_shape=jax.ShapeDtypeStruct((s, ATTN_EXT), F32),
        compiler_params=_params(("parallel",)),
    )(dmerged, merged, lse_all, *deps)


def _attention_fwd(n, wqkv, wo, tag):
    ns, qkvs, exts = [], [], []
    for g, (_, dil) in enumerate(DILATED_CFG):
        ng = n if dil == 1 else _residue_order(n, dil, f"{tag}_order_g{g}")
        qkv = _qkv_proj(ng, wqkv, g, f"{tag}_qkv_g{g}")
        ext = _attn_fwd(qkv, g, f"{tag}_fwd_g{g}")
        ns.append(ng)
        qkvs.append(qkv)
        exts.append(ext if dil == 1 else _token_order(ext, dil, None, f"{tag}_unorder_g{g}"))
    merged, merged_bf, lse_all = _attn_merge(*exts, f"{tag}_merge")
    m = _matmul(merged_bf, wo, "nn", F32, f"{tag}_wo")
    return m, (ns, qkvs, merged, merged_bf, lse_all)


def _attention_bwd(dm, wqkv, wo, saved, tag, dep=None, hook=None):
    ns, qkvs, merged, merged_bf, lse_all = saved
    d_wo = _matmul(merged_bf, dm, "tn", BF16, f"{tag}_dwo")
    dmerged = _matmul(dm, wo, "nt", F32, f"{tag}_dmerged")
    dext = _attn_dvec(dmerged, merged, lse_all, f"{tag}_dvec", dep)
    width = 3 * ATTN_WIDTH
    d_wqkv, dn, dep = [], None, None
    for g, (_, dil) in enumerate(DILATED_CFG):
        dext_g = dext if dil == 1 else _residue_order(dext, dil, f"{tag}_dorder_g{g}")
        dqkv = _attn_bwd(qkvs[g], dext_g, g, f"{tag}_bwd_g{g}", dep)
        dep = hook(g, dqkv) if hook is not None and g + 1 < N_GROUPS_A else None
        d_wqkv.append(_matmul(ns[g], dqkv, "tn", BF16, f"{tag}_dwqkv_g{g}", b_parts=3))
        dn_g = _matmul(dqkv, wqkv[:, g * width:(g + 1) * width], "nt", F32, f"{tag}_dn_g{g}", a_parts=3)
        dn = dn_g if dil == 1 else _token_order(dn_g, dil, dn, f"{tag}_dn_sum_g{g}")
    return dn, jnp.concatenate(d_wqkv, axis=1), d_wo


def _layer_matrices(i):
    mixer = (("attn_w_qkv", "attn_w_o"), ("conv_w_in", "conv_w_out"), ("pool_w_in", "pool_w_grp", "pool_w_out"))[i % 3]
    return [(k, i // 3) for k in mixer] + [("ffn_w_up", i), ("ffn_w_down", i)]


def _local_step(x, tgt, vec, weights, sink):
    ng = vec["norm_g"]

    def gain(i, j, token=None):
        g = ng[i, j][None, :]
        return g if token is None else g + token

    saved = []
    n = _rms_fwd(x, gain(0, 0), None, BF16, "norm_first")
    for i in range(DEPTH):
        wl = weights.layer(i)
        t0 = weights.hook(i, 0, n)
        kind, idx = i % 3, i // 3
        if kind == 0:
            m, ms = _attention_fwd(n, wl["attn_w_qkv"], wl["attn_w_o"], "attn")
        elif kind == 1:
            taps = vec["conv_w_dw"][idx] if t0 is None else vec["conv_w_dw"][idx] + t0
            z, y = _sconv_fwd(n, wl["conv_w_in"], taps, "sconv_fwd")
            m = _matmul(y, wl["conv_w_out"], "nn", F32, "sconv_out")
            ms = (z, y)
        else:
            scale = vec["pool_scale"][idx][None, :] if t0 is None else vec["pool_scale"][idx][None, :] + t0
            p, y = _pool_fwd(n, wl["pool_w_in"], wl["pool_w_grp"], scale, "pool_fwd")
            m = _matmul(y, wl["pool_w_out"], "nn", F32, "pool_out")
            ms = (p, y)
        t1 = weights.hook(i, 1, m)
        x1, n2 = _rms_res_pre(m, gain(i, 1, t0), x, gain(i, 2, t1), "norm_res_pre")
        h, c, a = _ffn_up(n2, wl["ffn_w_up"], vec["ffn_w_dw"][i], "ffn_up")
        t2 = weights.hook(i, 2, a)
        f = _matmul(a, wl["ffn_w_down"].reshape(D_FF, D_MODEL), "nn", F32, "ffn_down", a_parts=FFN_PAIRS)
        saved.append((x, n, m, ms, x1, n2, h, a, f, wl, c))
        if i + 1 < DEPTH:
            x, n = _rms_res_pre(f, gain(i, 3, t2), x1, gain(i + 1, 0), "norm_res_pre")
        else:
            x = _rms_fwd(f, gain(i, 3), x1, F32, "norm_res")
        weights.hook(i, 3, x)

    loss, dx = _loss_head(x, tgt, "loss_head")

    g_norm = [[None] * 4 for _ in range(DEPTH)]
    g_taps, g_scale, g_ffn_dw = [], [], [None] * DEPTH
    df, g_norm[DEPTH - 1][3] = _rms_bwd(saved[-1][8], gain(DEPTH - 1, 3), dx, None, BF16, "norm_bwd_sub")
    t0 = None
    for i in reversed(range(DEPTH)):
        xin, n, m, ms, x1, n2, h, a, f, wl, c = saved[i]
        kind, idx = i % 3, i // 3
        gl = {}
        d_wdown = _matmul(a, df, "tn", BF16, "ffn_dwdown", a_parts=FFN_PAIRS)
        gl["ffn_w_down"] = d_wdown.reshape(N_DEV, D_FF // N_DEV, D_MODEL)
        ffn_taps = vec["ffn_w_dw"][i] if t0 is None else vec["ffn_w_dw"][i] + t0
        dh, dwg, dwu = _ffn_mid_bwd(df, wl["ffn_w_down"].reshape(FFN_PAIRS, -1, D_MODEL), h, c, ffn_taps, "ffn_mid_bwd")
        g_ffn_dw[i] = jnp.concatenate([dwg, dwu], axis=0)
        t1 = sink.hook(i, 1, dh)
        gl["ffn_w_up"] = _ffn_dwup(n2, dh, "ffn_dwup", t1)
        tf = sink.ffn_done(i, gl)
        dn2 = _ffn_dn(dh, wl["ffn_w_up"], "ffn_dn", t1)
        dx1, dm, g_norm[i][2], g_norm[i][1] = _rms_bwd_pair(x1, gain(i, 2, tf), dn2, dx, m, gain(i, 1), "norm_bwd_pair")
        t2 = sink.hook(i, 2, dm)
        if kind == 0:
            dn, gl["attn_w_qkv"], gl["attn_w_o"] = _attention_bwd(
                dm, wl["attn_w_qkv"], wl["attn_w_o"], ms, "attn", t2, lambda g, after, i=i: sink.hook(i, ("a", "b")[g], after))
        elif kind == 1:
            z, y = ms
            gl["conv_w_out"] = _matmul(y, dm, "tn", BF16, "sconv_dwout")
            taps = vec["conv_w_dw"][idx] if t2 is None else vec["conv_w_dw"][idx] + t2
            dz, ddw = _sconv_mid_bwd(dm, wl["conv_w_out"], z, taps, "sconv_mid_bwd")
            g_taps.append(ddw)
            gl["conv_w_in"] = _matmul(n, dz, "tn", BF16, "sconv_dwin", b_parts=3)
            dn = _matmul(dz, wl["conv_w_in"], "nt", F32, "sconv_dn", a_parts=3)
        else:
            p, y = ms
            gl["pool_w_out"] = _matmul(y, dm, "tn", BF16, "pool_dwout")
            scale = vec["pool_scale"][idx][None, :] if t2 is None else vec["pool_scale"][idx][None, :] + t2
            du, gl["pool_w_grp"], dscale = _pool_mid_bwd(dm, wl["pool_w_out"], p, wl["pool_w_grp"], scale, "pool_mid_bwd")
            g_scale.append(dscale[0])
            gl["pool_w_in"] = _matmul(n, du, "tn", BF16, "pool_dwin")
            dn = _matmul(du, wl["pool_w_in"], "nt", F32, "pool_dn")
        sink.hook(i, 3, dn)
        if i > 0:
            dx, df, g_norm[i][0], g_norm[i - 1][3] = _rms_bwd_pair(xin, gain(i, 0, t2), dn, dx1, saved[i - 1][8],
                                                                   gain(i - 1, 3), "norm_bwd_pair")
        else:
            dx, g_norm[0][0] = _rms_bwd(xin, gain(0, 0), dn, dx1, F32, "norm_bwd_res")
        t0 = sink.layer_done(i, gl)

    vec_grads = {"norm_g": jnp.stack([jnp.concatenate(row, axis=0) for row in g_norm]), "conv_w_dw": jnp.stack(g_taps),
                 "pool_scale": jnp.stack(g_scale), "ffn_w_dw": g_ffn_dw}
    return loss, dx, vec_grads


_AXES = ("x", "y", "c")
ROUTE_A = ("y", "x", "c")
ROUTE_B = ("x", "y", "c")
def _dev_index(pos):
    return 4 * pos["x"] + 2 * pos["y"] + pos["c"]


_HBM = pl.BlockSpec(memory_space=pltpu.HBM)
_SEM = pl.BlockSpec(memory_space=pltpu.SEMAPHORE)
_ANY = pl.BlockSpec(memory_space=pl.ANY)
_EFFECT = pltpu.SideEffectType.DATAFLOW_SIDE_EFFECTING


TOKEN_SHAPE = (1, D_MODEL)


def _copies_start(describe, arrays, n_copies, name, after, token_shape=TOKEN_SHAPE):
    n = len(arrays)
    deps = [] if after is None else [after]

    def body(*refs):
        send_sems, recv_sems = refs[n + len(deps)], refs[n + len(deps) + 1]
        for c in describe(refs[:n], send_sems, recv_sems):
            c.start()
        refs[-1][...] = jnp.zeros_like(refs[-1])

    outs = pl.pallas_call(
        body, name=f"{name}_start",
        out_shape=(pltpu.SemaphoreType.DMA((n_copies,)), pltpu.SemaphoreType.DMA((n_copies,)),
                   *[pltpu.HBM(a.shape, a.dtype) for a in arrays], jax.ShapeDtypeStruct(token_shape, F32)),
        in_specs=[_HBM] * n + [_ANY] * len(deps),
        out_specs=(_SEM, _SEM, *([_HBM] * n), pl.BlockSpec(memory_space=pltpu.VMEM)),
        input_output_aliases={i: 2 + i for i in range(n)},
        compiler_params=pltpu.CompilerParams(has_side_effects=_EFFECT),
    )(*[pltpu.with_memory_space_constraint(a, pltpu.HBM) for a in arrays], *deps)
    return (outs[0], outs[1], list(outs[2:2 + n])), outs[-1]


def _copies_wait(describe, handle, name, after):
    send_sems, recv_sems, arrays = handle
    n = len(arrays)
    deps = [] if after is None else list(after) if isinstance(after, (list, tuple)) else [after]

    def body(*refs):
        for c in describe(refs[:n], refs[n], refs[n + 1]):
            c.wait_send()
            c.wait_recv()

    outs = pl.pallas_call(
        body, name=f"{name}_wait",
        out_shape=tuple(pltpu.HBM(a.shape, a.dtype) for a in arrays),
        in_specs=[_HBM] * n + [_SEM, _SEM] + [_ANY] * len(deps), out_specs=tuple([_HBM] * n),
        input_output_aliases={i: i for i in range(n)},
        compiler_params=pltpu.CompilerParams(has_side_effects=_EFFECT),
    )(*arrays, send_sems, recv_sems, *deps)
    return list(outs)


GATHER_STAGE_COPIES = (3, 3, 1)


def _gather_copies(stage, routes):
    n = len(routes)

    def describe(refs, send_sems, recv_sems):
        pos = {a: lax.axis_index(a) for a in _AXES}

        def flipped(axes):
            return {a: 1 - pos[a] if a in axes else pos[a] for a in _AXES}

        copies = []
        for i, (a1, a2, a3) in enumerate(routes):
            land = refs[n + i] if stage == 1 else refs[i]
            p1, p2, p12, p3 = flipped((a1,)), flipped((a2,)), flipped((a1, a2)), flipped((a3,))
            plan = {1: [(None, p1), (None, p2), (None, p3)], 2: [(p1, p2), (p1, p3), (p2, p3)], 3: [(p12, p3)]}[stage]
            for holder, to in plan:
                slot = land.at[_dev_index(pos if holder is None else holder)]
                k = len(copies)
                copies.append(pltpu.make_async_remote_copy(
                    src_ref=refs[i] if holder is None else slot, dst_ref=slot,
                    send_sem=send_sems.at[k], recv_sem=recv_sems.at[k],
                    device_id=tuple(to[a] for a in _AXES), device_id_type=pl.DeviceIdType.MESH))
        return copies

    return describe


def _gather_begin(shards, routes, name, after):
    n = len(shards)
    lands = [lax.empty((N_DEV,) + a.shape, a.dtype) for a in shards]
    handle, token = _copies_start(_gather_copies(1, routes), list(shards) + lands, GATHER_STAGE_COPIES[0] * n,
                                  f"{name}_1", after)
    return {"stage": 1, "handle": handle, "routes": routes, "name": name, "n": n}, token


def _gather_next(state, after):
    stage, routes, name, n = state["stage"], state["routes"], state["name"], state["n"]
    arrays = _copies_wait(_gather_copies(stage, routes), state["handle"], f"{name}_{stage}", after)
    if stage == 1:
        state = dict(state, shards=arrays[:n])
        arrays = arrays[n:]
    if stage == 3:
        me = _dev_index({a: lax.axis_index(a) for a in _AXES})
        return [lax.dynamic_update_index_in_dim(o, s, me, 0) for o, s in zip(arrays, state["shards"])], None
    handle, token = _copies_start(_gather_copies(stage + 1, routes), arrays, GATHER_STAGE_COPIES[stage] * n,
                                  f"{name}_{stage + 1}", None)
    return dict(state, stage=stage + 1, handle=handle), token


ADD_ROW_TILES = (1024, 704, 512, 352, 256, 128, 96, 64, 32, 16)


def _add_half(a, recv, me, out_dtype, name):
    p, q, cols = recv.shape
    tr = _pick(q, ADD_ROW_TILES)

    def body(me_ref, a_ref, b_ref, o_ref):
        o_ref[...] = (a_ref[...].astype(F32) + b_ref[...].astype(F32)).astype(o_ref.dtype)

    return pl.pallas_call(
        body, name=name,
        grid_spec=pltpu.PrefetchScalarGridSpec(
            num_scalar_prefetch=1, grid=(p, q // tr),
            in_specs=[pl.BlockSpec((None, None, tr, cols), lambda j, i, m: (j, m[0], i, 0)),
                      pl.BlockSpec((None, tr, cols), lambda j, i, m: (j, i, 0))],
            out_specs=pl.BlockSpec((None, tr, cols), lambda j, i, m: (j, i, 0))),
        out_shape=jax.ShapeDtypeStruct((p, q, cols), out_dtype),
        compiler_params=_params(("parallel", "parallel")),
    )(me, a, recv)


def _half_copies(axes):
    n = len(axes)

    def describe(refs, send_sems, recv_sems):
        pos = {a: lax.axis_index(a) for a in _AXES}
        copies = []
        for i, axis in enumerate(axes):
            peer = tuple(1 - pos[a] if a == axis else pos[a] for a in _AXES)
            copies.append(pltpu.make_async_remote_copy(
                src_ref=refs[i].at[:, 1 - pos[axis]], dst_ref=refs[n + i], send_sem=send_sems.at[i],
                recv_sem=recv_sems.at[i], device_id=peer, device_id_type=pl.DeviceIdType.MESH))
        return copies

    return describe


def _scatter_begin(slots, routes, tags, name, token_shape=TOKEN_SHAPE):
    shapes = [a.shape[1:] for a in slots]
    rows = [math.prod(s[:-1]) for s in shapes]
    arrays = [a.reshape(4, 2, n, s[-1]) for a, n, s in zip(slots, rows, shapes)]
    return _scatter_start({"stage": 0, "arrays": arrays, "routes": routes, "tags": tags, "name": name,
                           "shapes": shapes, "rows": rows}, token_shape)


def _scatter_start(state, token_shape=TOKEN_SHAPE):
    stage, arrays = state["stage"], state["arrays"]
    axes = [r[2 - stage] for r in state["routes"]]
    lands = [lax.empty((a.shape[0],) + a.shape[2:], a.dtype) for a in arrays]
    handle, token = _copies_start(_half_copies(axes), arrays + lands, len(arrays), f"{state['name']}_{stage + 1}", None,
                                  token_shape)
    return dict(state, handle=handle, axes=axes), token


def _scatter_next(state, after):
    stage, axes, n = state["stage"], state["axes"], len(state["arrays"])
    both = _copies_wait(_half_copies(axes), state["handle"], f"{state['name']}_{stage + 1}", after)
    coord = {a: lax.axis_index(a).astype(jnp.int32).reshape(1) for a in _AXES}
    sums = [_add_half(a, r, coord[ax], F32 if stage == 2 else BF16, f"scatter_add_{stage + 1}_{t}")
            for a, r, ax, t in zip(both[:n], both[n:], axes, state["tags"])]
    if stage == 2:
        return [a.reshape(s) for a, s in zip(sums, state["shapes"])], None
    if stage == 0:
        views = [(1, 2, 2 * r, s[-1]) if route[1] == "x" else (2, 2, r, s[-1])
                 for r, s, route in zip(state["rows"], state["shapes"], state["routes"])]
    else:
        views = [(1, 2, r, s[-1]) for r, s in zip(state["rows"], state["shapes"])]
    return _scatter_start(dict(state, stage=stage + 1, arrays=[a.reshape(v) for a, v in zip(sums, views)]))


_WEIGHTS = {
    "norm_g": ((DEPTH, 4, D_MODEL), 2, True),
    "attn_w_qkv": ((2, D_MODEL, 4608), 2, False),
    "attn_w_o": ((2, ATTN_WIDTH, D_MODEL), 2, False),
    "conv_w_in": ((1, D_MODEL, 3 * D_MODEL), 2, False),
    "conv_w_dw": ((1, 3, D_MODEL), 2, True),
    "conv_w_out": ((1, D_MODEL, D_MODEL), 1, False),
    "pool_w_in": ((1, D_MODEL, D_MODEL), 1, False),
    "pool_w_grp": ((1, 4, POOL_GROUP_DIM, POOL_GROUP_DIM), 2, False),
    "pool_scale": ((1, D_MODEL), 1, True),
    "pool_w_out": ((1, D_MODEL, D_MODEL), 1, False),
    "ffn_w_up": ((DEPTH, D_MODEL, 2 * D_FF), 2, False),
    "ffn_w_dw": ((DEPTH, 3, 2 * D_FF), 2, True),
    "ffn_w_down": ((DEPTH, D_FF, D_MODEL), 1, False),
}
_NAMES = tuple(_WEIGHTS)
_VECTORS = tuple(k for k in _NAMES if _WEIGHTS[k][2])
_MATRICES = tuple(k for k in _NAMES if not _WEIGHTS[k][2])
_FFN = ("ffn_w_up", "ffn_w_down")
_ON_ROUTE_A = ("ffn_w_up", "attn_w_o", "conv_w_out", "pool_w_in")
PACK_ROWS = 16


def _route(name):
    return ROUTE_A if name in _ON_ROUTE_A else ROUTE_B


def _shard_shape(name):
    shape, ax, _ = _WEIGHTS[name]
    return tuple(s // N_DEV if i == ax else s for i, s in enumerate(shape))


def _full_from_slots(slots, name, layers=None):
    shape, ax, _ = _WEIGHTS[name]
    if layers is not None:
        shape = (layers,) + shape[1:]
    return jnp.moveaxis(slots, 0, ax).reshape(shape)


def _slots_from_full(full, name):
    shape, ax, _ = _WEIGHTS[name]
    split = shape[:ax] + (N_DEV, shape[ax] // N_DEV) + shape[ax + 1:]
    return jnp.moveaxis(full.reshape(split), ax, 0)


def _pack_vectors(parts, lead):
    rows = []
    for k in _VECTORS:
        r = parts[k].reshape(lead + (-1, LANES))
        pad = -r.shape[-2] % PACK_ROWS
        rows.append(jnp.pad(r, [(0, 0)] * len(lead) + [(0, pad), (0, 0)]))
    return jnp.concatenate(rows, axis=len(lead))


def _unpack_vectors(buf, lead):
    out, r0 = {}, 0
    for k in _VECTORS:
        shard = _shard_shape(k)
        rows = math.prod(shard) // LANES
        out[k] = buf[..., r0:r0 + rows, :].reshape(lead + shard)
        r0 += rows + (-rows % PACK_ROWS)
    return out


class _LayerWeights:
    def __init__(self, shards):
        first, ffn0 = _layer_matrices(0)[:-2], _layer_matrices(0)[-2:]
        state, token = _gather_begin([shards[k][j].astype(BF16) for k, j in first] + [_pack_vectors(shards, ())],
                                     [_route(k) for k, _ in first] + [ROUTE_B], "gather0", None)
        self.cast = {k: (shards[k] + token[0, 0]).astype(BF16) for k in _MATRICES}
        state, _ = _gather_next(state, self._send(ffn0) + self._send(_layer_matrices(1)))
        state, _ = _gather_next(state, None)
        outs, _ = _gather_next(state, None)
        vec = _unpack_vectors(outs[-1], (N_DEV,))
        self.vec = {k: _full_from_slots(vec[k], k) for k in _VECTORS}
        self.vec["ffn_w_dw"] = [vec["ffn_w_dw"][:, l] for l in range(DEPTH)]
        self.ready = {0: self._unpack(first, outs[:-1])}
        self.chains = {}
        tokens = []
        self._begin("ffn0", ffn0, "gather0f", outs[0], tokens)
        self._begin(1, _layer_matrices(1), "gather1", outs[0], tokens)
        self.vec["norm_g"] = self.vec["norm_g"] + (tokens[0] + tokens[1])

    def _send(self, items):
        return [self.cast[k][j] for k, j in items]

    @staticmethod
    def _unpack(items, outs):
        return {k: o if k in _FFN else _full_from_slots(o[:, None], k, layers=1)[0] for (k, _), o in zip(items, outs)}

    def _begin(self, key, items, name, after, tokens):
        state, token = _gather_begin(self._send(items), [_route(k) for k, _ in items], name, after)
        self.chains[key] = (items, state)
        tokens.append(token)

    def _advance(self, key, after, tokens):
        items, state = self.chains.pop(key)
        state, token = _gather_next(state, after)
        if token is None:
            self.ready.setdefault(0 if key == "ffn0" else key, {}).update(self._unpack(items, state))
        else:
            self.chains[key] = (items, state)
            tokens.append(token)

    def layer(self, i):
        return self.ready[i]

    def hook(self, i, point, after):
        tokens = []
        if i == 0 and point == 0:
            self._advance("ffn0", after, tokens)
        if i == 0 and point == 1:
            self._advance("ffn0", after, tokens)
            self._advance("ffn0", None, tokens)
        if point >= 1 and i + 1 in self.chains:
            self._advance(i + 1, after, tokens)
        if point == 1 and i + 2 < DEPTH:
            self._begin(i + 2, _layer_matrices(i + 2), f"gather{i + 2}", after, tokens)
        return functools.reduce(lambda a, b: a + b, tokens) if tokens else None


def _layer_slots(g, name):
    shape, ax, _ = _WEIGHTS[name]
    shape, ax = shape[1:], ax - 1
    split = shape[:ax] + (N_DEV, shape[ax] // N_DEV) + shape[ax + 1:]
    return jnp.moveaxis(g.reshape(split), ax, 0).astype(BF16)


class _GradSink:
    def __init__(self):
        self.state = None
        self.ffn_state = None
        self.sums = {}
        self.last = None

    def ffn_done(self, i, grads):
        if i != 0:
            return None
        self.ffn_items = _layer_matrices(0)[-2:]
        self.ffn_state, token = _scatter_begin([grads[k] for k, _ in self.ffn_items],
                                               [_route(k) for k, _ in self.ffn_items],
                                               [f"{k}{j}" for k, j in self.ffn_items], "scatter0f")
        return token

    def layer_done(self, i, grads):
        items = _layer_matrices(i)
        if i == 0:
            items = items[:-2]
            self.last = (items, [_layer_slots(grads[k], k) for k, _ in items])
            return None
        slots = [grads[k] if k in _FFN else _layer_slots(grads[k], k) for k, _ in items]
        self.items = items
        self.state, token = _scatter_begin(slots, [_route(k) for k, _ in items], [f"{k}{j}" for k, j in items],
                                           f"scatter{i}", (N_DEV, 3, 2 * D_FF // N_DEV))
        return token

    def hook(self, i, point, after):
        tokens = []
        if self.state is not None and point in (1, 2, 3):
            self.state, token = _scatter_next(self.state, after)
            if point == 3:
                self.sums.update(dict(zip(self.items, self.state)))
                self.state = None
            tokens.append(token)
        if self.ffn_state is not None and point in ("a", "b", 3):
            self.ffn_state, token = _scatter_next(self.ffn_state, after)
            if point == 3:
                self.sums.update(dict(zip(self.ffn_items, self.ffn_state)))
                self.ffn_state = None
            tokens.append(token)
        tokens = [t for t in tokens if t is not None]
        return functools.reduce(lambda a, b: a + b, tokens) if tokens else None


def _adamw(w, g, m, v, name, layer=None, prev=None, dep=None):
    shape = w.shape
    cols = shape[-1]
    view = shape if len(shape) == 3 else (1, math.prod(shape[:-1]), cols)
    layers, rows, _ = view
    tr = _pick(rows, (512, 352, 288, 256, 128, 64, 32, 16, 8))
    n_prev = 0 if prev is None else 3
    lead = ([] if prev is None else [p.reshape(view) for p in prev]) + ([] if dep is None else [dep])

    def body(*refs):
        w_ref, g_ref, m_ref, v_ref = refs[len(lead):len(lead) + 4]
        d_ref, nm_ref, nv_ref = refs[len(lead) + 4:]
        gv = g_ref[...]
        nm = ADAM_B1 * m_ref[...] + (1.0 - ADAM_B1) * gv
        nv = ADAM_B2 * v_ref[...] + (1.0 - ADAM_B2) * jnp.square(gv)
        m_hat = nm / (1.0 - ADAM_B1 ** ADAM_STEP)
        v_hat = nv / (1.0 - ADAM_B2 ** ADAM_STEP)
        d_ref[...] = -ADAM_LR * (m_hat / (jnp.sqrt(v_hat) + ADAM_EPS) + ADAM_WD * w_ref[...])
        nm_ref[...] = nm
        nv_ref[...] = nv

    if layer is None:
        grid = (layers, rows // tr)
        blk = gblk = pl.BlockSpec((None, tr, cols), lambda l, i: (l, i, 0))
        gview = view
    else:
        grid = (rows // tr,)
        blk = pl.BlockSpec((None, tr, cols), lambda i: (layer, i, 0))
        gblk = pl.BlockSpec((tr, cols), lambda i: (i, 0))
        gview = (rows, cols)
    shp = jax.ShapeDtypeStruct(view, F32)
    outs = pl.pallas_call(
        body, name=name, grid=grid, in_specs=[_ANY] * len(lead) + [blk, gblk, blk, blk], out_specs=[blk] * 3,
        out_shape=[shp] * 3, input_output_aliases={i: i for i in range(n_prev)},
        compiler_params=_params(("parallel",) * len(grid)),
    )(*lead, w.reshape(view), g.reshape(gview), m.reshape(view), v.reshape(view))
    return [o.reshape(shape) for o in outs]


def kernel(x, norm_g, attn_w_qkv, attn_w_o, conv_w_in, conv_w_dw, conv_w_out, pool_w_in, pool_w_grp, pool_scale, pool_w_out, ffn_w_up, ffn_w_dw, ffn_w_down, loss_target, m_norm_g, m_attn_w_qkv, m_attn_w_o, m_conv_w_in, m_conv_w_dw, m_conv_w_out, m_pool_w_in, m_pool_w_grp, m_pool_scale, m_pool_w_out, m_ffn_w_up, m_ffn_w_dw, m_ffn_w_down, v_norm_g, v_attn_w_qkv, v_attn_w_o, v_conv_w_in, v_conv_w_dw, v_conv_w_out, v_pool_w_in, v_pool_w_grp, v_pool_scale, v_pool_w_out, v_ffn_w_up, v_ffn_w_dw, v_ffn_w_down):
    shards = dict(zip(_NAMES, (norm_g, attn_w_qkv, attn_w_o, conv_w_in, conv_w_dw, conv_w_out, pool_w_in,
                               pool_w_grp, pool_scale, pool_w_out, ffn_w_up, ffn_w_dw, ffn_w_down)))
    moms = dict(zip(_NAMES, (m_norm_g, m_attn_w_qkv, m_attn_w_o, m_conv_w_in, m_conv_w_dw, m_conv_w_out,
                             m_pool_w_in, m_pool_w_grp, m_pool_scale, m_pool_w_out, m_ffn_w_up, m_ffn_w_dw,
                             m_ffn_w_down)))
    vels = dict(zip(_NAMES, (v_norm_g, v_attn_w_qkv, v_attn_w_o, v_conv_w_in, v_conv_w_dw, v_conv_w_out,
                             v_pool_w_in, v_pool_w_grp, v_pool_scale, v_pool_w_out, v_ffn_w_up, v_ffn_w_dw,
                             v_ffn_w_down)))
    weights = _LayerWeights(shards)
    sink = _GradSink()
    loss, grad_x, vec_grads = _local_step(x[0], loss_target[0], weights.vec, weights, sink)

    items, slots = sink.last
    vec_slots = {k: _slots_from_full(vec_grads[k], k) for k in _VECTORS if k != "ffn_w_dw"}
    vec_slots["ffn_w_dw"] = jnp.stack(vec_grads["ffn_w_dw"], axis=1)
    state, token = _scatter_begin(slots + [_pack_vectors(vec_slots, (N_DEV,)).astype(BF16)],
                                  [_route(k) for k, _ in items] + [ROUTE_B],
                                  [f"{k}{j}" for k, j in items] + ["vectors"], "scatter0")
    results = {}

    flipped = {k for k in _MATRICES if _WEIGHTS[k][0][0] > 1 and _shard_shape(k)[-1] % LANES}
    wmv = {k: [t.transpose(0, 2, 1) if k in flipped else t for t in (shards[k], moms[k], vels[k])] for k in _MATRICES}

    def step(matrices, dep=None):
        outs = []
        for k, j in matrices:
            g = sink.sums[(k, j)]
            w, m, v = wmv[k]
            if _WEIGHTS[k][0][0] == 1:
                results[k] = (g[None], _adamw(w, g[None], m, v, f"adamw_{k}", dep=dep))
            else:
                gs, prev = results.get(k, ({}, None))
                gs[j] = g
                results[k] = (gs, _adamw(w, g.T if k in flipped else g, m, v, f"adamw_{k}{j}", layer=j, prev=prev, dep=dep))
            outs.append(results[k][1][0])
        return outs

    state, token = _scatter_next(state, step(_layer_matrices(3), token))
    state, token = _scatter_next(state, step(_layer_matrices(2) + _layer_matrices(1), token))
    loss = lax.psum(loss[0, 0] + token[0, 0], _AXES)
    sums, _ = _scatter_next(state, step(_layer_matrices(0)[-2:], token))
    sink.sums.update(dict(zip(items, sums[:-1])))
    step(items)
    vec_sums = _unpack_vectors(sums[-1], ())
    for k in _VECTORS:
        results[k] = (vec_sums[k], _adamw(shards[k], vec_sums[k], moms[k], vels[k], f"adamw_{k}"))
    grads_out = {k: g if not isinstance(g, dict) else jnp.stack([g[j] for j in range(len(g))])
                 for k, (g, _) in results.items()}
    stepped = {k: [o.transpose(0, 2, 1) if k in flipped else o for o in outs] for k, (_, outs) in results.items()}
    return (loss, grad_x[None], *[grads_out[k] for k in _NAMES], *[stepped[k][0] for k in _NAMES],
            *[stepped[k][1] for k in _NAMES], *[stepped[k][2] for k in _NAMES])
```

```python
import functools
import math

import numpy as np
import jax
import jax.numpy as jnp
from jax import lax
from jax.experimental import pallas as pl
from jax.experimental.pallas import tpu as pltpu

F32, BF16 = jnp.float32, jnp.bfloat16

D_MODEL = 1024
SEQ = 2048
DEPTH = 4
DILATED_CFG = ((128, 1), (512, 4), (2048, 16))
N_GROUPS_A = 3
HEADS = 8
HEAD_DIM = 64
ATTN_WIDTH = HEADS * HEAD_DIM
N_HEADS_A = N_GROUPS_A * HEADS
BLOCK = 128
NEG_INF = -1e30
POOL_GROUP_DIM = 256
D_FF = 2816
RMS_EPS = 1e-6
ADAM_LR, ADAM_B1, ADAM_B2, ADAM_EPS, ADAM_WD, ADAM_STEP = 0.001, 0.9, 0.999, 1e-08, 0.01, 10

N_DEV = 8
LANES = 128
V7X_VMEM_BYTES = 64 * 2 ** 20
VMEM_LIMIT_BYTES = V7X_VMEM_BYTES - 8 * 2 ** 20
COL_TILE = 256
ROW_TILE = 1024
MATMUL_TILES = (1024, 1408, 512, 256, 128)
TN_RESIDENT_K = 2048

NN = (((1,), (0,)), ((), ()))
NT = (((1,), (1,)), ((), ()))
TN = (((0,), (0,)), ((), ()))


def _dot(a, b, dims=NN):
    return lax.dot_general(a, b, dims, preferred_element_type=F32)


def _params(sem=None):
    return pltpu.CompilerParams(dimension_semantics=sem, vmem_limit_bytes=VMEM_LIMIT_BYTES)


def _pick(n, prefs):
    for p in prefs:
        if n % p == 0:
            return p
    return n


def _matmul(a, b, mode, out_dtype, name, a_parts=1, b_parts=1):
    if mode == "nn":
        m, k = a.shape[-2], a.shape[-1] * a_parts
        n = b.shape[-1] * b_parts
    elif mode == "nt":
        m, k = a.shape[-2], a.shape[-1] * a_parts
        n = b.shape[-2]
    else:
        k, m = a.shape[-2], a.shape[-1] * a_parts
        n = b.shape[-1] * b_parts
    tm = _pick(m, MATMUL_TILES)
    tn = _pick(n // b_parts if mode != "nt" else n, MATMUL_TILES)
    kk = k // a_parts if mode != "tn" else k
    tk = _pick(kk, MATMUL_TILES)
    if mode == "tn":
        tm = _pick(m // a_parts, MATMUL_TILES)
        if k <= TN_RESIDENT_K:
            tk = k
    gm, gn, gk = m // tm, n // tn, k // tk

    def a_idx(i, j, kq):
        if mode == "tn":
            r, c, per = kq, i, (m // a_parts) // tm
        else:
            r, c, per = i, kq, (k // a_parts) // tk
        return (r, c) if a_parts == 1 else (c // per, r, c % per)

    def b_idx(i, j, kq):
        if mode == "nt":
            return (j, kq)
        per = (n // b_parts) // tn
        return (kq, j) if b_parts == 1 else (j // per, kq, j % per)

    a_blk = (tk, tm) if mode == "tn" else (tm, tk)
    b_blk = (tn, tk) if mode == "nt" else (tk, tn)
    if a_parts > 1:
        a_blk = (None,) + a_blk
    if b_parts > 1:
        b_blk = (None,) + b_blk
    dims = {"nn": NN, "nt": NT, "tn": TN}[mode]

    def body_single(a_ref, b_ref, o_ref):
        o_ref[...] = _dot(a_ref[...], b_ref[...], dims).astype(o_ref.dtype)

    def body(a_ref, b_ref, o_ref, acc_ref):
        kq = pl.program_id(2)

        @pl.when(kq == 0)
        def _():
            acc_ref[...] = jnp.zeros_like(acc_ref)

        acc_ref[...] += _dot(a_ref[...], b_ref[...], dims)

        @pl.when(kq == gk - 1)
        def _():
            o_ref[...] = acc_ref[...].astype(o_ref.dtype)

    return pl.pallas_call(
        body_single if gk == 1 else body, name=name, grid=(gm, gn, gk),
        in_specs=[pl.BlockSpec(a_blk, a_idx), pl.BlockSpec(b_blk, b_idx)],
        out_specs=pl.BlockSpec((tm, tn), lambda i, j, kq: (i, j)),
        out_shape=jax.ShapeDtypeStruct((m, n), out_dtype),
        scratch_shapes=[] if gk == 1 else [pltpu.VMEM((tm, tn), F32)],
        compiler_params=_params(("parallel", "parallel", "arbitrary")),
    )(a, b)


def _rms_fwd(xin, g, res, out_dtype, name):
    s, d = xin.shape
    has_res = res is not None

    def body(*refs):
        x_ref, g_ref = refs[0], refs[1]
        o_ref = refs[-1]
        x = x_ref[...]
        r = lax.rsqrt(jnp.mean(x * x, axis=-1, keepdims=True) + RMS_EPS)
        y = x * r * g_ref[...]
        if has_res:
            y = refs[2][...] + y
        o_ref[...] = y.astype(o_ref.dtype)

    row = pl.BlockSpec((ROW_TILE, d), lambda i: (i, 0))
    vec = pl.BlockSpec((1, d), lambda i: (0, 0))
    ins = [xin, g] + ([res] if has_res else [])
    return pl.pallas_call(
        body, name=name, grid=(s // ROW_TILE,),
        in_specs=[row, vec] + ([row] if has_res else []),
        out_specs=row, out_shape=jax.ShapeDtypeStruct((s, d), out_dtype),
        compiler_params=_params(("parallel",)),
    )(*ins)


def _rms_bwd(xin, g, dy, dres, out_dtype, name):
    s, d = xin.shape
    has_res = dres is not None

    def body(*refs):
        x_ref, g_ref, dy_ref = refs[0], refs[1], refs[2]
        dx_ref, dg_ref = refs[-2], refs[-1]

        @pl.when(pl.program_id(0) == 0)
        def _():
            dg_ref[...] = jnp.zeros_like(dg_ref)

        x = x_ref[...]
        dyv = dy_ref[...].astype(F32)
        r = lax.rsqrt(jnp.mean(x * x, axis=-1, keepdims=True) + RMS_EPS)
        xhat = x * r
        u = dyv * g_ref[...]
        dx = r * (u - xhat * jnp.mean(u * xhat, axis=-1, keepdims=True))
        if has_res:
            dx = refs[3][...] + dx
        dx_ref[...] = dx.astype(dx_ref.dtype)
        dg_ref[...] += jnp.sum(dyv * xhat, axis=0, keepdims=True)

    row = pl.BlockSpec((ROW_TILE, d), lambda i: (i, 0))
    vec = pl.BlockSpec((1, d), lambda i: (0, 0))
    ins = [xin, g, dy] + ([dres] if has_res else [])
    return pl.pallas_call(
        body, name=name, grid=(s // ROW_TILE,),
        in_specs=[row, vec, row] + ([row] if has_res else []),
        out_specs=[row, vec],
        out_shape=[jax.ShapeDtypeStruct((s, d), out_dtype), jax.ShapeDtypeStruct((1, d), F32)],
        compiler_params=_params(("arbitrary",)),
    )(*ins)


def _rms(x):
    r = lax.rsqrt(jnp.mean(x * x, axis=-1, keepdims=True) + RMS_EPS)
    return r, x * r


def _rms_grad(r, xhat, dy, g):
    u = dy * g
    return r * (u - xhat * jnp.mean(u * xhat, axis=-1, keepdims=True))


def _rms_res_pre(sub, g_post, res, g_pre, name):
    s, d = sub.shape

    def body(sub_ref, gp_ref, res_ref, gn_ref, x_ref, n_ref):
        xnew = res_ref[...] + _rms(sub_ref[...])[1] * gp_ref[...]
        x_ref[...] = xnew
        n_ref[...] = (_rms(xnew)[1] * gn_ref[...]).astype(BF16)

    row = pl.BlockSpec((ROW_TILE, d), lambda i: (i, 0))
    vec = pl.BlockSpec((1, d), lambda i: (0, 0))
    return pl.pallas_call(
        body, name=name, grid=(s // ROW_TILE,),
        in_specs=[row, vec, row, vec], out_specs=[row, row],
        out_shape=[jax.ShapeDtypeStruct((s, d), F32), jax.ShapeDtypeStruct((s, d), BF16)],
        compiler_params=_params(("parallel",)),
    )(sub, g_post, res, g_pre)


def _rms_bwd_pair(xmid, g_pre, dn, dres, sub, g_post, name):
    s, d = xmid.shape

    def body(x_ref, gn_ref, dn_ref, dres_ref, sub_ref, gp_ref, dx_ref, dsub_ref, dgn_ref, dgp_ref):
        @pl.when(pl.program_id(0) == 0)
        def _():
            dgn_ref[...] = jnp.zeros_like(dgn_ref)
            dgp_ref[...] = jnp.zeros_like(dgp_ref)

        dnv = dn_ref[...].astype(F32)
        r, xhat = _rms(x_ref[...])
        dx = dres_ref[...] + _rms_grad(r, xhat, dnv, gn_ref[...])
        dx_ref[...] = dx
        dgn_ref[...] += jnp.sum(dnv * xhat, axis=0, keepdims=True)
        rs, shat = _rms(sub_ref[...])
        dsub_ref[...] = _rms_grad(rs, shat, dx, gp_ref[...]).astype(BF16)
        dgp_ref[...] += jnp.sum(dx * shat, axis=0, keepdims=True)

    row = pl.BlockSpec((ROW_TILE, d), lambda i: (i, 0))
    vec = pl.BlockSpec((1, d), lambda i: (0, 0))
    return pl.pallas_call(
        body, name=name, grid=(s // ROW_TILE,),
        in_specs=[row, vec, row, row, row, vec], out_specs=[row, row, vec, vec],
        out_shape=[jax.ShapeDtypeStruct((s, d), F32), jax.ShapeDtypeStruct((s, d), BF16),
                   jax.ShapeDtypeStruct((1, d), F32), jax.ShapeDtypeStruct((1, d), F32)],
        compiler_params=_params(("arbitrary",)),
    )(xmid, g_pre, dn, dres, sub, g_post)


def _loss_head(y, tgt, name):
    s, d = y.shape

    def body(y_ref, t_ref, l_ref, dy_ref):
        @pl.when(pl.program_id(0) == 0)
        def _():
            l_ref[...] = jnp.zeros_like(l_ref)

        e = y_ref[...] - t_ref[...]
        dy_ref[...] = e / d
        per_tok = jnp.mean(e * e, axis=-1, keepdims=True)
        l_ref[...] += 0.5 * jnp.sum(per_tok, axis=0, keepdims=True)

    row = pl.BlockSpec((ROW_TILE, d), lambda i: (i, 0))
    return pl.pallas_call(
        body, name=name, grid=(s // ROW_TILE,),
        in_specs=[row, row],
        out_specs=[pl.BlockSpec((1, 1), lambda i: (0, 0)), row],
        out_shape=[jax.ShapeDtypeStruct((1, 1), F32), jax.ShapeDtypeStruct((s, d), F32)],
        compiler_params=_params(("arbitrary",)),
    )(y, tgt)


SUBLANES = 8


def _shift_down(x, k):
    t, c = x.shape
    r = pltpu.roll(x.reshape(t // SUBLANES, SUBLANES, c), k, axis=1)
    above = jnp.concatenate([jnp.zeros((1, SUBLANES, c), x.dtype), r[:-1]], axis=0)
    rows = lax.broadcasted_iota(jnp.int32, (1, SUBLANES, c), 1)
    return jnp.where(rows >= k, r, above).reshape(t, c)


def _shift_up(x, k):
    t, c = x.shape
    r = pltpu.roll(x.reshape(t // SUBLANES, SUBLANES, c), SUBLANES - k, axis=1)
    below = jnp.concatenate([r[1:], jnp.zeros((1, SUBLANES, c), x.dtype)], axis=0)
    rows = lax.broadcasted_iota(jnp.int32, (1, SUBLANES, c), 1)
    return jnp.where(rows < SUBLANES - k, r, below).reshape(t, c)


def _conv3(h, w):
    return w[2:3] * h + w[1:2] * _shift_down(h, 1) + w[0:1] * _shift_down(h, 2)


def _conv3_bwd(dc, h, w, dw_ref, cols=slice(None)):
    u1, u2 = _shift_up(dc, 1), _shift_up(dc, 2)
    dw_ref[0:1, cols] = jnp.sum(u2 * h, axis=0, keepdims=True)
    dw_ref[1:2, cols] = jnp.sum(u1 * h, axis=0, keepdims=True)
    dw_ref[2:3, cols] = jnp.sum(dc * h, axis=0, keepdims=True)
    return w[2:3] * dc + w[1:2] * u1 + w[0:1] * u2


FFN_PAIRS = N_DEV // 2


def _lane_chunks(width):
    return [(c0, min(COL_TILE, width - c0)) for c0 in range(0, width, COL_TILE)]


def _ffn_up(n, wup, wdw, name):
    s, d = n.shape
    cw = wup.shape[-1]

    def body(n_ref, wg_ref, wu_ref, dg_ref, du_ref, h_ref, c_ref, a_ref):
        x = n_ref[...]
        for c0, size in _lane_chunks(cw):
            cols = slice(c0, c0 + size)
            hg = _dot(x, wg_ref[:, cols])
            hu = _dot(x, wu_ref[:, cols])
            h_ref[0, :, cols] = hg.astype(BF16)
            h_ref[1, :, cols] = hu.astype(BF16)
            cg = _conv3(hg, dg_ref[:, cols])
            cu = _conv3(hu, du_ref[:, cols])
            c_ref[0, :, cols] = cg.astype(BF16)
            c_ref[1, :, cols] = cu.astype(BF16)
            a_ref[:, cols] = (cg * jax.nn.sigmoid(cg) * cu).astype(BF16)

    return pl.pallas_call(
        body, name=name, grid=(FFN_PAIRS,),
        in_specs=[pl.BlockSpec((s, d), lambda j: (0, 0)),
                  pl.BlockSpec((None, d, cw), lambda j: (j, 0, 0)),
                  pl.BlockSpec((None, d, cw), lambda j: (j + FFN_PAIRS, 0, 0)),
                  pl.BlockSpec((None, 3, cw), lambda j: (j, 0, 0)),
                  pl.BlockSpec((None, 3, cw), lambda j: (j + FFN_PAIRS, 0, 0))],
        out_specs=[pl.BlockSpec((None, 2, s, cw), lambda j: (j, 0, 0, 0)),
                   pl.BlockSpec((None, 2, s, cw), lambda j: (j, 0, 0, 0)),
                   pl.BlockSpec((None, s, cw), lambda j: (j, 0, 0))],
        out_shape=[jax.ShapeDtypeStruct((FFN_PAIRS, 2, s, cw), BF16), jax.ShapeDtypeStruct((FFN_PAIRS, 2, s, cw), BF16),
                   jax.ShapeDtypeStruct((FFN_PAIRS, s, cw), BF16)],
        compiler_params=_params(("parallel",)),
    )(n, wup, wup, wdw, wdw)


def _ffn_mid_bwd(do, wdown, h, c, wdw, name):
    s, d = do.shape
    cw = wdown.shape[1]

    def body(do_ref, wd_ref, h_ref, c_ref, wg_ref, wu_ref, dh_ref, dwg_ref, dwu_ref):
        dov = do_ref[...]
        for c0, size in _lane_chunks(cw):
            cols = slice(c0, c0 + size)
            da = _dot(dov, wd_ref[cols, :], NT)
            hg = h_ref[0, :, cols].astype(F32)
            hu = h_ref[1, :, cols].astype(F32)
            wg, wu = wg_ref[:, cols], wu_ref[:, cols]
            cg = c_ref[0, :, cols].astype(F32)
            cu = c_ref[1, :, cols].astype(F32)
            sg = jax.nn.sigmoid(cg)
            dcu = da * (cg * sg)
            dcg = da * cu * (sg * (1.0 + cg * (1.0 - sg)))
            dh_ref[0, :, cols] = _conv3_bwd(dcg, hg, wg, dwg_ref, cols).astype(BF16)
            dh_ref[1, :, cols] = _conv3_bwd(dcu, hu, wu, dwu_ref, cols).astype(BF16)

    vec = jax.ShapeDtypeStruct((FFN_PAIRS, 3, cw), F32)
    return pl.pallas_call(
        body, name=name, grid=(FFN_PAIRS,),
        in_specs=[pl.BlockSpec((s, d), lambda j: (0, 0)), pl.BlockSpec((None, cw, d), lambda j: (j, 0, 0)),
                  pl.BlockSpec((None, 2, s, cw), lambda j: (j, 0, 0, 0)),
                  pl.BlockSpec((None, 2, s, cw), lambda j: (j, 0, 0, 0)),
                  pl.BlockSpec((None, 3, cw), lambda j: (j, 0, 0)),
                  pl.BlockSpec((None, 3, cw), lambda j: (j + FFN_PAIRS, 0, 0))],
        out_specs=[pl.BlockSpec((None, 2, s, cw), lambda j: (j, 0, 0, 0)),
                   pl.BlockSpec((None, 3, cw), lambda j: (j, 0, 0)), pl.BlockSpec((None, 3, cw), lambda j: (j, 0, 0))],
        out_shape=[jax.ShapeDtypeStruct((FFN_PAIRS, 2, s, cw), BF16), vec, vec],
        compiler_params=_params(("parallel",)),
    )(do, wdown, h, c, wdw, wdw)


def _ffn_dwup(n, dh, name, dep=None):
    s, d = n.shape
    cw = dh.shape[-1]
    deps = [] if dep is None else [dep]

    def body(n_ref, dh_ref, *rest):
        rest[-1][...] = _dot(n_ref[...], dh_ref[...], TN).astype(BF16)

    return pl.pallas_call(
        body, name=name, grid=(N_DEV,),
        in_specs=[pl.BlockSpec((s, d), lambda k: (0, 0)),
                  pl.BlockSpec((None, None, s, cw), lambda k: (k % FFN_PAIRS, k // FFN_PAIRS, 0, 0))] + [_ANY] * len(deps),
        out_specs=pl.BlockSpec((None, d, cw), lambda k: (k, 0, 0)),
        out_shape=jax.ShapeDtypeStruct((N_DEV, d, cw), BF16),
        compiler_params=_params(("parallel",)),
    )(n, dh, *deps)


def _ffn_dn(dh, wup, name, dep=None):
    s, cw = dh.shape[-2:]
    d = wup.shape[1]
    tm = _pick(s, MATMUL_TILES)
    deps = [] if dep is None else [dep]

    def body(dh_ref, w_ref, *rest):
        o_ref, acc_ref = rest[-2:]
        k = pl.program_id(1)

        @pl.when(k == 0)
        def _():
            acc_ref[...] = jnp.zeros_like(acc_ref)

        acc_ref[...] += _dot(dh_ref[...], w_ref[...], NT)

        @pl.when(k == N_DEV - 1)
        def _():
            o_ref[...] = acc_ref[...]

    return pl.pallas_call(
        body, name=name, grid=(s // tm, N_DEV),
        in_specs=[pl.BlockSpec((None, None, tm, cw), lambda i, k: (k % FFN_PAIRS, k // FFN_PAIRS, i, 0)),
                  pl.BlockSpec((None, d, cw), lambda i, k: (k, 0, 0))] + [_ANY] * len(deps),
        out_specs=pl.BlockSpec((tm, d), lambda i, k: (i, 0)),
        out_shape=jax.ShapeDtypeStruct((s, d), F32),
        scratch_shapes=[pltpu.VMEM((tm, d), F32)],
        compiler_params=_params(("parallel", "arbitrary")),
    )(dh, wup, *deps)


def _sconv_fwd(n, win, wdw, name):
    s, d = n.shape
    tn = COL_TILE
    nj = d // tn

    def body(n_ref, wb_ref, wc_ref, wh_ref, dw_ref, z_ref, y_ref):
        x = n_ref[...]
        zb = _dot(x, wb_ref[...])
        zc = _dot(x, wc_ref[...])
        zh = _dot(x, wh_ref[...])
        z_ref[0] = zb.astype(BF16)
        z_ref[1] = zc.astype(BF16)
        z_ref[2] = zh.astype(BF16)
        y_ref[...] = (zb * _conv3(zc * zh, dw_ref[...])).astype(BF16)

    return pl.pallas_call(
        body, name=name, grid=(nj,),
        in_specs=[pl.BlockSpec((s, d), lambda j: (0, 0)),
                  pl.BlockSpec((d, tn), lambda j: (0, j)), pl.BlockSpec((d, tn), lambda j: (0, j + nj)),
                  pl.BlockSpec((d, tn), lambda j: (0, j + 2 * nj)), pl.BlockSpec((3, tn), lambda j: (0, j))],
        out_specs=[pl.BlockSpec((3, s, tn), lambda j: (0, 0, j)), pl.BlockSpec((s, tn), lambda j: (0, j))],
        out_shape=[jax.ShapeDtypeStruct((3, s, d), BF16), jax.ShapeDtypeStruct((s, d), BF16)],
        compiler_params=_params(("parallel",)),
    )(n, win, win, win, wdw)


def _sconv_mid_bwd(dm, wout, z, wdw, name):
    s, d = dm.shape
    tn = COL_TILE
    nj = d // tn

    def body(dm_ref, wo_ref, z_ref, w_ref, dz_ref, dw_ref):
        dy = _dot(dm_ref[...], wo_ref[...], NT)
        zb = z_ref[0].astype(F32)
        zc = z_ref[1].astype(F32)
        zh = z_ref[2].astype(F32)
        w = w_ref[...]
        p = zc * zh
        cp = _conv3(p, w)
        dz_ref[0] = (dy * cp).astype(BF16)
        dcp = dy * zb
        dp = _conv3_bwd(dcp, p, w, dw_ref)
        dz_ref[1] = (dp * zh).astype(BF16)
        dz_ref[2] = (dp * zc).astype(BF16)

    return pl.pallas_call(
        body, name=name, grid=(nj,),
        in_specs=[pl.BlockSpec((s, d), lambda j: (0, 0)), pl.BlockSpec((tn, d), lambda j: (j, 0)),
                  pl.BlockSpec((3, s, tn), lambda j: (0, 0, j)), pl.BlockSpec((3, tn), lambda j: (0, j))],
        out_specs=[pl.BlockSpec((3, s, tn), lambda j: (0, 0, j)), pl.BlockSpec((3, tn), lambda j: (0, j))],
        out_shape=[jax.ShapeDtypeStruct((3, s, d), BF16), jax.ShapeDtypeStruct((3, d), F32)],
        compiler_params=_params(("parallel",)),
    )(dm, wout, z, wdw)


def _pool_select(g, c2, c4, c8, c16):
    return jnp.where(g == 0, c2, jnp.where(g == 1, c4, jnp.where(g == 2, c8, c16)))


def _pool_inv_count(g, shape):
    pos = lax.broadcasted_iota(jnp.int32, shape, 0).astype(F32) + 1.0
    win = (2 << g).astype(F32)
    return jnp.minimum(pos, win)


def _pool_fwd(n, win, wgrp, scale, name):
    s, d = n.shape
    tn = POOL_GROUP_DIM

    def body(n_ref, wi_ref, wg_ref, sc_ref, p_ref, y_ref):
        g = pl.program_id(0)
        u = _dot(n_ref[...], wi_ref[...])
        s2 = u + _shift_down(u, 1)
        s4 = s2 + _shift_down(s2, 2)
        s8 = s4 + _shift_down(s4, 4)
        s16 = s8 + _shift_down(s8, 8)
        tot = _pool_select(g, s2, s4, s8, s16)
        p = (tot / _pool_inv_count(g, u.shape) - u).astype(BF16)
        p_ref[...] = p
        y_ref[...] = (_dot(p, wg_ref[...]) * sc_ref[...]).astype(BF16)

    return pl.pallas_call(
        body, name=name, grid=(d // tn,),
        in_specs=[pl.BlockSpec((s, d), lambda g: (0, 0)), pl.BlockSpec((d, tn), lambda g: (0, g)),
                  pl.BlockSpec((None, tn, tn), lambda g: (g, 0, 0)), pl.BlockSpec((1, tn), lambda g: (0, g))],
        out_specs=[pl.BlockSpec((s, tn), lambda g: (0, g)), pl.BlockSpec((s, tn), lambda g: (0, g))],
        out_shape=[jax.ShapeDtypeStruct((s, d), BF16), jax.ShapeDtypeStruct((s, d), BF16)],
        compiler_params=_params(("parallel",)),
    )(n, win, wgrp, scale)


def _pool_mid_bwd(dm, wout, p, wgrp, scale, name):
    s, d = dm.shape
    tn = POOL_GROUP_DIM

    def body(dm_ref, wo_ref, p_ref, wg_ref, sc_ref, du_ref, dwg_ref, dsc_ref):
        g = pl.program_id(0)
        dy = _dot(dm_ref[...], wo_ref[...], NT)
        pv = p_ref[...]
        wg = wg_ref[...]
        ypre = _dot(pv, wg)
        dsc_ref[...] = jnp.sum(dy * ypre, axis=0, keepdims=True)
        dypre = (dy * sc_ref[...]).astype(BF16)
        dwg_ref[...] = _dot(pv, dypre, TN)
        dp = _dot(dypre, wg, NT)
        e = dp / _pool_inv_count(g, dp.shape)
        f2 = e + _shift_up(e, 1)
        f4 = f2 + _shift_up(f2, 2)
        f8 = f4 + _shift_up(f4, 4)
        f16 = f8 + _shift_up(f8, 8)
        du_ref[...] = (_pool_select(g, f2, f4, f8, f16) - dp).astype(BF16)

    return pl.pallas_call(
        body, name=name, grid=(d // tn,),
        in_specs=[pl.BlockSpec((s, d), lambda g: (0, 0)), pl.BlockSpec((tn, d), lambda g: (g, 0)),
                  pl.BlockSpec((s, tn), lambda g: (0, g)), pl.BlockSpec((None, tn, tn), lambda g: (g, 0, 0)),
                  pl.BlockSpec((1, tn), lambda g: (0, g))],
        out_specs=[pl.BlockSpec((s, tn), lambda g: (0, g)), pl.BlockSpec((None, tn, tn), lambda g: (g, 0, 0)),
                   pl.BlockSpec((1, tn), lambda g: (0, g))],
        out_shape=[jax.ShapeDtypeStruct((s, d), BF16), jax.ShapeDtypeStruct((4, tn, tn), F32),
                   jax.ShapeDtypeStruct((1, d), F32)],
        compiler_params=_params(("parallel",)),
    )(dm, wout, p, wgrp, scale)


PANEL = LANES
ATTN_EXT = ATTN_WIDTH + PANEL
DVEC_LANE = HEADS


def _alibi_slopes(g, dil):
    all_slopes = 2.0 ** (-8.0 * np.arange(1, N_HEADS_A + 1) / N_HEADS_A)
    return [float(np.float32(sl) * np.float32(dil)) for sl in all_slopes[g * HEADS:(g + 1) * HEADS]]


def _residue_order(a, dil, name):
    s, w = a.shape
    per = ROW_TILE // dil
    panels = w // PANEL

    def body(a_ref, o_ref, *tiles):
        for c in range(panels):
            cols = slice(c * PANEL, (c + 1) * PANEL)
            tiles[c][...] = a_ref[:, cols].astype(F32)
            for r in range(dil):
                o_ref[r, :, cols] = tiles[c][pl.ds(r, per, stride=dil), :].astype(o_ref.dtype)

    out = pl.pallas_call(
        body, name=name, grid=(s // ROW_TILE,),
        in_specs=[pl.BlockSpec((ROW_TILE, w), lambda i: (i, 0))],
        out_specs=pl.BlockSpec((dil, per, w), lambda i: (0, i, 0)),
        out_shape=jax.ShapeDtypeStruct((dil, s // dil, w), a.dtype),
        scratch_shapes=[pltpu.VMEM((ROW_TILE, PANEL), F32)] * panels,
        compiler_params=_params(("parallel",)),
    )(a)
    return out.reshape(s, w)


def _token_order(a, dil, acc, name):
    s, w = a.shape
    per = ROW_TILE // dil
    panels = w // PANEL
    has_acc = acc is not None

    def body(*refs):
        a_ref = refs[0]
        o_ref = refs[2] if has_acc else refs[1]
        tiles = refs[3:] if has_acc else refs[2:]
        for c in range(panels):
            cols = slice(c * PANEL, (c + 1) * PANEL)
            for r in range(dil):
                tiles[c][pl.ds(r, per, stride=dil), :] = a_ref[r, :, cols]
            v = tiles[c][...]
            if has_acc:
                v = v + refs[1][:, cols]
            o_ref[:, cols] = v

    row = pl.BlockSpec((ROW_TILE, w), lambda i: (i, 0))
    return pl.pallas_call(
        body, name=name, grid=(s // ROW_TILE,),
        in_specs=[pl.BlockSpec((dil, per, w), lambda i: (0, i, 0))] + ([row] if has_acc else []),
        out_specs=row, out_shape=jax.ShapeDtypeStruct((s, w), F32),
        scratch_shapes=[pltpu.VMEM((ROW_TILE, PANEL), F32)] * panels,
        compiler_params=_params(("parallel",)),
    )(*([a.reshape(dil, s // dil, w)] + ([acc] if has_acc else [])))


def _qkv_proj(n, wqkv, g, name):
    s, d = n.shape
    tm = _pick(s, MATMUL_TILES)

    def body(a_ref, b_ref, o_ref):
        o_ref[...] = _dot(a_ref[...], b_ref[...]).astype(BF16)

    return pl.pallas_call(
        body, name=name, grid=(s // tm, 3),
        in_specs=[pl.BlockSpec((tm, d), lambda i, t: (i, 0)),
                  pl.BlockSpec((d, ATTN_WIDTH), lambda i, t: (0, 3 * g + t))],
        out_specs=pl.BlockSpec((None, tm, ATTN_WIDTH), lambda i, t: (t, i, 0)),
        out_shape=jax.ShapeDtypeStruct((3, s, ATTN_WIDTH), BF16),
        compiler_params=_params(("parallel", "parallel")),
    )(n, wqkv)


def _attn_window(n, ln):
    if ln == BLOCK:
        return 0, BLOCK
    return pl.multiple_of(jnp.maximum(n - 1, 0) * BLOCK, BLOCK), 2 * BLOCK


def _attn_mask(n, k0, kw):
    qpos = n * BLOCK + lax.broadcasted_iota(jnp.int32, (BLOCK, kw), 0)
    kpos = k0 + lax.broadcasted_iota(jnp.int32, (BLOCK, kw), 1)
    dist = qpos - kpos
    return dist.astype(F32), (dist >= 0) & (dist <= BLOCK)


def _attn_scores(q, keys, slope, dist, valid):
    s = _dot(q, keys, NT) * (HEAD_DIM ** -0.5) - slope * dist
    return jnp.where(valid, s, NEG_INF)


ATTN_STEP_BLOCKS = 1
ATTN_BWD_STEP_BLOCKS = 4


def _attn_block(gb, ln):
    nb = ln // BLOCK
    n, base = (0, gb * ln) if nb == 1 else (gb % nb, (gb // nb) * ln)
    k0, kw = _attn_window(n, ln)
    cur = pl.ds(pl.multiple_of(gb * BLOCK, BLOCK), BLOCK)
    win = pl.ds(pl.multiple_of(base + k0, BLOCK), kw)
    return cur, win, n, k0, kw


def _attn_fwd(qkv, g, name):
    _, s, w = qkv.shape
    dil = DILATED_CFG[g][1]
    ln = s // dil
    slopes = _alibi_slopes(g, dil)
    rows = ATTN_STEP_BLOCKS * BLOCK

    def body(qkv_ref, o_ref):
        o_ref[:, w:] = jnp.zeros((rows, PANEL), F32)
        for b in range(ATTN_STEP_BLOCKS):
            cur, win, n, k0, kw = _attn_block(pl.program_id(0) * ATTN_STEP_BLOCKS + b, ln)
            dist, valid = _attn_mask(n, k0, kw)
            out = slice(b * BLOCK, (b + 1) * BLOCK)
            for h in range(HEADS):
                cols = slice(h * HEAD_DIM, (h + 1) * HEAD_DIM)
                sc = _attn_scores(qkv_ref[0, cur, cols], qkv_ref[1, win, cols], slopes[h], dist, valid)
                m = jnp.max(sc, axis=-1, keepdims=True)
                p = jnp.exp(sc - m)
                den = jnp.sum(p, axis=-1, keepdims=True)
                o_ref[out, cols] = _dot(p.astype(BF16), qkv_ref[2, win, cols]) / den
                o_ref[out, w + h:w + h + 1] = m + jnp.log(den)

    return pl.pallas_call(
        body, name=name, grid=(s // rows,),
        in_specs=[pl.BlockSpec((3, s, w), lambda i: (0, 0, 0))],
        out_specs=pl.BlockSpec((rows, ATTN_EXT), lambda i: (i, 0)),
        out_shape=jax.ShapeDtypeStruct((s, ATTN_EXT), F32),
        compiler_params=_params(("parallel",)),
    )(qkv)


def _attn_bwd(qkv, dext, g, name, dep=None):
    _, s, w = qkv.shape
    dil = DILATED_CFG[g][1]
    ln = s // dil
    slopes = _alibi_slopes(g, dil)
    scale = HEAD_DIM ** -0.5
    rows = ATTN_BWD_STEP_BLOCKS * BLOCK
    steps = s // rows
    deps = [] if dep is None else [dep]

    def body(qkv_ref, de_ref, *rest):
        d_ref, dk_ref, dv_ref = rest[-3:]

        @pl.when(pl.program_id(0) == 0)
        def _():
            dk_ref[...] = jnp.zeros_like(dk_ref)
            dv_ref[...] = jnp.zeros_like(dv_ref)

        for b in range(ATTN_BWD_STEP_BLOCKS):
            cur, win, n, k0, kw = _attn_block(pl.program_id(0) * ATTN_BWD_STEP_BLOCKS + b, ln)
            dist, valid = _attn_mask(n, k0, kw)
            blk = slice(b * BLOCK, (b + 1) * BLOCK)
            for h in range(HEADS):
                cols = slice(h * HEAD_DIM, (h + 1) * HEAD_DIM)
                q, keys = qkv_ref[0, cur, cols], qkv_ref[1, win, cols]
                dob = de_ref[blk, cols].astype(BF16)
                p = jnp.exp(_attn_scores(q, keys, slopes[h], dist, valid) - de_ref[blk, w + h:w + h + 1])
                dd = de_ref[blk, w + DVEC_LANE + h:w + DVEC_LANE + h + 1]
                ds = (p * (_dot(dob, qkv_ref[2, win, cols], NT) - dd)).astype(BF16)
                d_ref[0, cur, cols] = (scale * _dot(ds, keys)).astype(BF16)
                dv_ref[win, cols] += _dot(p.astype(BF16), dob, TN)
                dk_ref[win, cols] += scale * _dot(ds, q, TN)

        @pl.when(pl.program_id(0) == steps - 1)
        def _():
            d_ref[1] = dk_ref[...].astype(BF16)
            d_ref[2] = dv_ref[...].astype(BF16)

    whole = pl.BlockSpec((3, s, w), lambda i: (0, 0, 0))
    return pl.pallas_call(
        body, name=name, grid=(steps,),
        in_specs=[whole, pl.BlockSpec((rows, ATTN_EXT), lambda i: (i, 0))] + [_ANY] * len(deps),
        out_specs=whole, out_shape=jax.ShapeDtypeStruct((3, s, w), BF16),
        scratch_shapes=[pltpu.VMEM((s, w), F32), pltpu.VMEM((s, w), F32)],
        compiler_params=_params(("arbitrary",)),
    )(qkv, dext, *deps)


def _attn_merge(e0, e1, e2, name):
    s = e0.shape[0]
    w = ATTN_WIDTH

    def body(e0_ref, e1_ref, e2_ref, m_ref, mb_ref, lse_ref):
        refs = (e0_ref, e1_ref, e2_ref)
        l = [r[:, w:w + HEADS] for r in refs]
        mx = jnp.maximum(jnp.maximum(l[0], l[1]), l[2])
        e = [jnp.exp(v - mx) for v in l]
        z = e[0] + e[1] + e[2]
        lse_ref[...] = mx + jnp.log(z)
        wts = [v / z for v in e]
        for h in range(HEADS):
            cols = slice(h * HEAD_DIM, (h + 1) * HEAD_DIM)
            acc = wts[0][:, h:h + 1] * refs[0][:, cols]
            for g in range(1, N_GROUPS_A):
                acc = acc + wts[g][:, h:h + 1] * refs[g][:, cols]
            m_ref[:, cols] = acc
            mb_ref[:, cols] = acc.astype(BF16)

    ext = pl.BlockSpec((ROW_TILE, ATTN_EXT), lambda i: (i, 0))
    row = pl.BlockSpec((ROW_TILE, w), lambda i: (i, 0))
    return pl.pallas_call(
        body, name=name, grid=(s // ROW_TILE,),
        in_specs=[ext, ext, ext],
        out_specs=[row, row, pl.BlockSpec((ROW_TILE, HEADS), lambda i: (i, 0))],
        out_shape=[jax.ShapeDtypeStruct((s, w), F32), jax.ShapeDtypeStruct((s, w), BF16),
                   jax.ShapeDtypeStruct((s, HEADS), F32)],
        compiler_params=_params(("parallel",)),
    )(e0, e1, e2)


def _attn_dvec(dmerged, merged, lse_all, name, dep=None):
    s, w = merged.shape
    deps = [] if dep is None else [dep]

    def body(dm_ref, m_ref, lse_ref, *rest):
        de_ref = rest[-1]
        dmv = dm_ref[...]
        de_ref[:, :w] = dmv
        de_ref[:, w:] = jnp.zeros((ROW_TILE, PANEL), F32)
        de_ref[:, w:w + HEADS] = lse_ref[...]
        prod = dmv * m_ref[...]
        for h in range(HEADS):
            lane = w + DVEC_LANE + h
            de_ref[:, lane:lane + 1] = jnp.sum(prod[:, h * HEAD_DIM:(h + 1) * HEAD_DIM], axis=-1, keepdims=True)

    row = pl.BlockSpec((ROW_TILE, w), lambda i: (i, 0))
    return pl.pallas_call(
        body, name=name, grid=(s // ROW_TILE,),
        in_specs=[row, row, pl.BlockSpec((ROW_TILE, HEADS), lambda i: (i, 0))] + [_ANY] * len(deps),
        out_specs=pl.BlockSpec((ROW_TILE, ATTN_EXT), lambda i: (i, 0)),
        out_shape=jax.ShapeDtypeStruct((s, ATTN_EXT), F32),
        compiler_params=_params(("parallel",)),
    )(dmerged, merged, lse_all, *deps)


def _attention_fwd(n, wqkv, wo, tag):
    ns, qkvs, exts = [], [], []
    for g, (_, dil) in enumerate(DILATED_CFG):
        ng = n if dil == 1 else _residue_order(n, dil, f"{tag}_order_g{g}")
        qkv = _qkv_proj(ng, wqkv, g, f"{tag}_qkv_g{g}")
        ext = _attn_fwd(qkv, g, f"{tag}_fwd_g{g}")
        ns.append(ng)
        qkvs.append(qkv)
        exts.append(ext if dil == 1 else _token_order(ext, dil, None, f"{tag}_unorder_g{g}"))
    merged, merged_bf, lse_all = _attn_merge(*exts, f"{tag}_merge")
    m = _matmul(merged_bf, wo, "nn", F32, f"{tag}_wo")
    return m, (ns, qkvs, merged, merged_bf, lse_all)


def _attention_bwd(dm, wqkv, wo, saved, tag, dep=None, hook=None):
    ns, qkvs, merged, merged_bf, lse_all = saved
    d_wo = _matmul(merged_bf, dm, "tn", BF16, f"{tag}_dwo")
    dmerged = _matmul(dm, wo, "nt", F32, f"{tag}_dmerged")
    dext = _attn_dvec(dmerged, merged, lse_all, f"{tag}_dvec", dep)
    width = 3 * ATTN_WIDTH
    d_wqkv, dn, dep = [], None, None
    for g, (_, dil) in enumerate(DILATED_CFG):
        dext_g = dext if dil == 1 else _residue_order(dext, dil, f"{tag}_dorder_g{g}")
        dqkv = _attn_bwd(qkvs[g], dext_g, g, f"{tag}_bwd_g{g}", dep)
        dep = hook(g, dqkv) if hook is not None and g + 1 < N_GROUPS_A else None
        d_wqkv.append(_matmul(ns[g], dqkv, "tn", BF16, f"{tag}_dwqkv_g{g}", b_parts=3))
        dn_g = _matmul(dqkv, wqkv[:, g * width:(g + 1) * width], "nt", F32, f"{tag}_dn_g{g}", a_parts=3)
        dn = dn_g if dil == 1 else _token_order(dn_g, dil, dn, f"{tag}_dn_sum_g{g}")
    return dn, jnp.concatenate(d_wqkv, axis=1), d_wo


def _layer_matrices(i):
    mixer = (("attn_w_qkv", "attn_w_o"), ("conv_w_in", "conv_w_out"), ("pool_w_in", "pool_w_grp", "pool_w_out"))[i % 3]
    return [(k, i // 3) for k in mixer] + [("ffn_w_up", i), ("ffn_w_down", i)]


def _local_step(x, tgt, vec, weights, sink):
    ng = vec["norm_g"]

    def gain(i, j, token=None):
        g = ng[i, j][None, :]
        return g if token is None else g + token

    saved = []
    n = _rms_fwd(x, gain(0, 0), None, BF16, "norm_first")
    for i in range(DEPTH):
        wl = weights.layer(i)
        t0 = weights.hook(i, 0, n)
        kind, idx = i % 3, i // 3
        if kind == 0:
            m, ms = _attention_fwd(n, wl["attn_w_qkv"], wl["attn_w_o"], "attn")
        elif kind == 1:
            taps = vec["conv_w_dw"][idx] if t0 is None else vec["conv_w_dw"][idx] + t0
            z, y = _sconv_fwd(n, wl["conv_w_in"], taps, "sconv_fwd")
            m = _matmul(y, wl["conv_w_out"], "nn", F32, "sconv_out")
            ms = (z, y)
        else:
            scale = vec["pool_scale"][idx][None, :] if t0 is None else vec["pool_scale"][idx][None, :] + t0
            p, y = _pool_fwd(n, wl["pool_w_in"], wl["pool_w_grp"], scale, "pool_fwd")
            m = _matmul(y, wl["pool_w_out"], "nn", F32, "pool_out")
            ms = (p, y)
        t1 = weights.hook(i, 1, m)
        x1, n2 = _rms_res_pre(m, gain(i, 1, t0), x, gain(i, 2, t1), "norm_res_pre")
        h, c, a = _ffn_up(n2, wl["ffn_w_up"], vec["ffn_w_dw"][i], "ffn_up")
        t2 = weights.hook(i, 2, a)
        f = _matmul(a, wl["ffn_w_down"].reshape(D_FF, D_MODEL), "nn", F32, "ffn_down", a_parts=FFN_PAIRS)
        saved.append((x, n, m, ms, x1, n2, h, a, f, wl, c))
        if i + 1 < DEPTH:
            x, n = _rms_res_pre(f, gain(i, 3, t2), x1, gain(i + 1, 0), "norm_res_pre")
        else:
            x = _rms_fwd(f, gain(i, 3), x1, F32, "norm_res")
        weights.hook(i, 3, x)

    loss, dx = _loss_head(x, tgt, "loss_head")

    g_norm = [[None] * 4 for _ in range(DEPTH)]
    g_taps, g_scale, g_ffn_dw = [], [], [None] * DEPTH
    df, g_norm[DEPTH - 1][3] = _rms_bwd(saved[-1][8], gain(DEPTH - 1, 3), dx, None, BF16, "norm_bwd_sub")
    t0 = None
    for i in reversed(range(DEPTH)):
        xin, n, m, ms, x1, n2, h, a, f, wl, c = saved[i]
        kind, idx = i % 3, i // 3
        gl = {}
        d_wdown = _matmul(a, df, "tn", BF16, "ffn_dwdown", a_parts=FFN_PAIRS)
        gl["ffn_w_down"] = d_wdown.reshape(N_DEV, D_FF // N_DEV, D_MODEL)
        ffn_taps = vec["ffn_w_dw"][i] if t0 is None else vec["ffn_w_dw"][i] + t0
        dh, dwg, dwu = _ffn_mid_bwd(df, wl["ffn_w_down"].reshape(FFN_PAIRS, -1, D_MODEL), h, c, ffn_taps, "ffn_mid_bwd")
        g_ffn_dw[i] = jnp.concatenate([dwg, dwu], axis=0)
        t1 = sink.hook(i, 1, dh)
        gl["ffn_w_up"] = _ffn_dwup(n2, dh, "ffn_dwup", t1)
        tf = sink.ffn_done(i, gl)
        dn2 = _ffn_dn(dh, wl["ffn_w_up"], "ffn_dn", t1)
        dx1, dm, g_norm[i][2], g_norm[i][1] = _rms_bwd_pair(x1, gain(i, 2, tf), dn2, dx, m, gain(i, 1), "norm_bwd_pair")
        t2 = sink.hook(i, 2, dm)
        if kind == 0:
            dn, gl["attn_w_qkv"], gl["attn_w_o"] = _attention_bwd(
                dm, wl["attn_w_qkv"], wl["attn_w_o"], ms, "attn", t2, lambda g, after, i=i: sink.hook(i, ("a", "b")[g], after))
        elif kind == 1:
            z, y = ms
            gl["conv_w_out"] = _matmul(y, dm, "tn", BF16, "sconv_dwout")
            taps = vec["conv_w_dw"][idx] if t2 is None else vec["conv_w_dw"][idx] + t2
            dz, ddw = _sconv_mid_bwd(dm, wl["conv_w_out"], z, taps, "sconv_mid_bwd")
            g_taps.append(ddw)
            gl["conv_w_in"] = _matmul(n, dz, "tn", BF16, "sconv_dwin", b_parts=3)
            dn = _matmul(dz, wl["conv_w_in"], "nt", F32, "sconv_dn", a_parts=3)
        else:
            p, y = ms
            gl["pool_w_out"] = _matmul(y, dm, "tn", BF16, "pool_dwout")
            scale = vec["pool_scale"][idx][None, :] if t2 is None else vec["pool_scale"][idx][None, :] + t2
            du, gl["pool_w_grp"], dscale = _pool_mid_bwd(dm, wl["pool_w_out"], p, wl["pool_w_grp"], scale, "pool_mid_bwd")
            g_scale.append(dscale[0])
            gl["pool_w_in"] = _matmul(n, du, "tn", BF16, "pool_dwin")
            dn = _matmul(du, wl["pool_w_in"], "nt", F32, "pool_dn")
        sink.hook(i, 3, dn)
        if i > 0:
            dx, df, g_norm[i][0], g_norm[i - 1][3] = _rms_bwd_pair(xin, gain(i, 0, t2), dn, dx1, saved[i - 1][8],
                                                                   gain(i - 1, 3), "norm_bwd_pair")
        else:
            dx, g_norm[0][0] = _rms_bwd(xin, gain(0, 0), dn, dx1, F32, "norm_bwd_res")
        t0 = sink.layer_done(i, gl)

    vec_grads = {"norm_g": jnp.stack([jnp.concatenate(row, axis=0) for row in g_norm]), "conv_w_dw": jnp.stack(g_taps),
                 "pool_scale": jnp.stack(g_scale), "ffn_w_dw": g_ffn_dw}
    return loss, dx, vec_grads


_AXES = ("x", "y", "c")
ROUTE_A = ("y", "x", "c")
ROUTE_B = ("x", "y", "c")
def _dev_index(pos):
    return 4 * pos["x"] + 2 * pos["y"] + pos["c"]


_HBM = pl.BlockSpec(memory_space=pltpu.HBM)
_SEM = pl.BlockSpec(memory_space=pltpu.SEMAPHORE)
_ANY = pl.BlockSpec(memory_space=pl.ANY)
_EFFECT = pltpu.SideEffectType.DATAFLOW_SIDE_EFFECTING


TOKEN_SHAPE = (1, D_MODEL)


def _copies_start(describe, arrays, n_copies, name, after, token_shape=TOKEN_SHAPE):
    n = len(arrays)
    deps = [] if after is None else [after]

    def body(*refs):
        send_sems, recv_sems = refs[n + len(deps)], refs[n + len(deps) + 1]
        for c in describe(refs[:n], send_sems, recv_sems):
            c.start()
        refs[-1][...] = jnp.zeros_like(refs[-1])

    outs = pl.pallas_call(
        body, name=f"{name}_start",
        out_shape=(pltpu.SemaphoreType.DMA((n_copies,)), pltpu.SemaphoreType.DMA((n_copies,)),
                   *[pltpu.HBM(a.shape, a.dtype) for a in arrays], jax.ShapeDtypeStruct(token_shape, F32)),
        in_specs=[_HBM] * n + [_ANY] * len(deps),
        out_specs=(_SEM, _SEM, *([_HBM] * n), pl.BlockSpec(memory_space=pltpu.VMEM)),
        input_output_aliases={i: 2 + i for i in range(n)},
        compiler_params=pltpu.CompilerParams(has_side_effects=_EFFECT),
    )(*[pltpu.with_memory_space_constraint(a, pltpu.HBM) for a in arrays], *deps)
    return (outs[0], outs[1], list(outs[2:2 + n])), outs[-1]


def _copies_wait(describe, handle, name, after):
    send_sems, recv_sems, arrays = handle
    n = len(arrays)
    deps = [] if after is None else list(after) if isinstance(after, (list, tuple)) else [after]

    def body(*refs):
        for c in describe(refs[:n], refs[n], refs[n + 1]):
            c.wait_send()
            c.wait_recv()

    outs = pl.pallas_call(
        body, name=f"{name}_wait",
        out_shape=tuple(pltpu.HBM(a.shape, a.dtype) for a in arrays),
        in_specs=[_HBM] * n + [_SEM, _SEM] + [_ANY] * len(deps), out_specs=tuple([_HBM] * n),
        input_output_aliases={i: i for i in range(n)},
        compiler_params=pltpu.CompilerParams(has_side_effects=_EFFECT),
    )(*arrays, send_sems, recv_sems, *deps)
    return list(outs)


GATHER_STAGE_COPIES = (3, 3, 1)


def _gather_copies(stage, routes):
    n = len(routes)

    def describe(refs, send_sems, recv_sems):
        pos = {a: lax.axis_index(a) for a in _AXES}

        def flipped(axes):
            return {a: 1 - pos[a] if a in axes else pos[a] for a in _AXES}

        copies = []
        for i, (a1, a2, a3) in enumerate(routes):
            land = refs[n + i] if stage == 1 else refs[i]
            p1, p2, p12, p3 = flipped((a1,)), flipped((a2,)), flipped((a1, a2)), flipped((a3,))
            plan = {1: [(None, p1), (None, p2), (None, p3)], 2: [(p1, p2), (p1, p3), (p2, p3)], 3: [(p12, p3)]}[stage]
            for holder, to in plan:
                slot = land.at[_dev_index(pos if holder is None else holder)]
                k = len(copies)
                copies.append(pltpu.make_async_remote_copy(
                    src_ref=refs[i] if holder is None else slot, dst_ref=slot,
                    send_sem=send_sems.at[k], recv_sem=recv_sems.at[k],
                    device_id=tuple(to[a] for a in _AXES), device_id_type=pl.DeviceIdType.MESH))
        return copies

    return describe


def _gather_begin(shards, routes, name, after):
    n = len(shards)
    lands = [lax.empty((N_DEV,) + a.shape, a.dtype) for a in shards]
    handle, token = _copies_start(_gather_copies(1, routes), list(shards) + lands, GATHER_STAGE_COPIES[0] * n,
                                  f"{name}_1", after)
    return {"stage": 1, "handle": handle, "routes": routes, "name": name, "n": n}, token


def _gather_next(state, after):
    stage, routes, name, n = state["stage"], state["routes"], state["name"], state["n"]
    arrays = _copies_wait(_gather_copies(stage, routes), state["handle"], f"{name}_{stage}", after)
    if stage == 1:
        state = dict(state, shards=arrays[:n])
        arrays = arrays[n:]
    if stage == 3:
        me = _dev_index({a: lax.axis_index(a) for a in _AXES})
        return [lax.dynamic_update_index_in_dim(o, s, me, 0) for o, s in zip(arrays, state["shards"])], None
    handle, token = _copies_start(_gather_copies(stage + 1, routes), arrays, GATHER_STAGE_COPIES[stage] * n,
                                  f"{name}_{stage + 1}", None)
    return dict(state, stage=stage + 1, handle=handle), token


ADD_ROW_TILES = (1024, 704, 512, 352, 256, 128, 96, 64, 32, 16)


def _add_half(a, recv, me, out_dtype, name):
    p, q, cols = recv.shape
    tr = _pick(q, ADD_ROW_TILES)

    def body(me_ref, a_ref, b_ref, o_ref):
        o_ref[...] = (a_ref[...].astype(F32) + b_ref[...].astype(F32)).astype(o_ref.dtype)

    return pl.pallas_call(
        body, name=name,
        grid_spec=pltpu.PrefetchScalarGridSpec(
            num_scalar_prefetch=1, grid=(p, q // tr),
            in_specs=[pl.BlockSpec((None, None, tr, cols), lambda j, i, m: (j, m[0], i, 0)),
                      pl.BlockSpec((None, tr, cols), lambda j, i, m: (j, i, 0))],
            out_specs=pl.BlockSpec((None, tr, cols), lambda j, i, m: (j, i, 0))),
        out_shape=jax.ShapeDtypeStruct((p, q, cols), out_dtype),
        compiler_params=_params(("parallel", "parallel")),
    )(me, a, recv)


def _half_copies(axes):
    n = len(axes)

    def describe(refs, send_sems, recv_sems):
        pos = {a: lax.axis_index(a) for a in _AXES}
        copies = []
        for i, axis in enumerate(axes):
            peer = tuple(1 - pos[a] if a == axis else pos[a] for a in _AXES)
            copies.append(pltpu.make_async_remote_copy(
                src_ref=refs[i].at[:, 1 - pos[axis]], dst_ref=refs[n + i], send_sem=send_sems.at[i],
                recv_sem=recv_sems.at[i], device_id=peer, device_id_type=pl.DeviceIdType.MESH))
        return copies

    return describe


def _scatter_begin(slots, routes, tags, name, token_shape=TOKEN_SHAPE):
    shapes = [a.shape[1:] for a in slots]
    rows = [math.prod(s[:-1]) for s in shapes]
    arrays = [a.reshape(4, 2, n, s[-1]) for a, n, s in zip(slots, rows, shapes)]
    return _scatter_start({"stage": 0, "arrays": arrays, "routes": routes, "tags": tags, "name": name,
                           "shapes": shapes, "rows": rows}, token_shape)


def _scatter_start(state, token_shape=TOKEN_SHAPE):
    stage, arrays = state["stage"], state["arrays"]
    axes = [r[2 - stage] for r in state["routes"]]
    lands = [lax.empty((a.shape[0],) + a.shape[2:], a.dtype) for a in arrays]
    handle, token = _copies_start(_half_copies(axes), arrays + lands, len(arrays), f"{state['name']}_{stage + 1}", None,
                                  token_shape)
    return dict(state, handle=handle, axes=axes), token


def _scatter_next(state, after):
    stage, axes, n = state["stage"], state["axes"], len(state["arrays"])
    both = _copies_wait(_half_copies(axes), state["handle"], f"{state['name']}_{stage + 1}", after)
    coord = {a: lax.axis_index(a).astype(jnp.int32).reshape(1) for a in _AXES}
    sums = [_add_half(a, r, coord[ax], F32 if stage == 2 else BF16, f"scatter_add_{stage + 1}_{t}")
            for a, r, ax, t in zip(both[:n], both[n:], axes, state["tags"])]
    if stage == 2:
        return [a.reshape(s) for a, s in zip(sums, state["shapes"])], None
    if stage == 0:
        views = [(1, 2, 2 * r, s[-1]) if route[1] == "x" else (2, 2, r, s[-1])
                 for r, s, route in zip(state["rows"], state["shapes"], state["routes"])]
    else:
        views = [(1, 2, r, s[-1]) for r, s in zip(state["rows"], state["shapes"])]
    return _scatter_start(dict(state, stage=stage + 1, arrays=[a.reshape(v) for a, v in zip(sums, views)]))


_WEIGHTS = {
    "norm_g": ((DEPTH, 4, D_MODEL), 2, True),
    "attn_w_qkv": ((2, D_MODEL, 4608), 2, False),
    "attn_w_o": ((2, ATTN_WIDTH, D_MODEL), 2, False),
    "conv_w_in": ((1, D_MODEL, 3 * D_MODEL), 2, False),
    "conv_w_dw": ((1, 3, D_MODEL), 2, True),
    "conv_w_out": ((1, D_MODEL, D_MODEL), 1, False),
    "pool_w_in": ((1, D_MODEL, D_MODEL), 1, False),
    "pool_w_grp": ((1, 4, POOL_GROUP_DIM, POOL_GROUP_DIM), 2, False),
    "pool_scale": ((1, D_MODEL), 1, True),
    "pool_w_out": ((1, D_MODEL, D_MODEL), 1, False),
    "ffn_w_up": ((DEPTH, D_MODEL, 2 * D_FF), 2, False),
    "ffn_w_dw": ((DEPTH, 3, 2 * D_FF), 2, True),
    "ffn_w_down": ((DEPTH, D_FF, D_MODEL), 1, False),
}
_NAMES = tuple(_WEIGHTS)
_VECTORS = tuple(k for k in _NAMES if _WEIGHTS[k][2])
_MATRICES = tuple(k for k in _NAMES if not _WEIGHTS[k][2])
_FFN = ("ffn_w_up", "ffn_w_down")
_ON_ROUTE_A = ("ffn_w_up", "attn_w_o", "conv_w_out", "pool_w_in")
PACK_ROWS = 16


def _route(name):
    return ROUTE_A if name in _ON_ROUTE_A else ROUTE_B


def _shard_shape(name):
    shape, ax, _ = _WEIGHTS[name]
    return tuple(s // N_DEV if i == ax else s for i, s in enumerate(shape))


def _full_from_slots(slots, name, layers=None):
    shape, ax, _ = _WEIGHTS[name]
    if layers is not None:
        shape = (layers,) + shape[1:]
    return jnp.moveaxis(slots, 0, ax).reshape(shape)


def _slots_from_full(full, name):
    shape, ax, _ = _WEIGHTS[name]
    split = shape[:ax] + (N_DEV, shape[ax] // N_DEV) + shape[ax + 1:]
    return jnp.moveaxis(full.reshape(split), ax, 0)


def _pack_vectors(parts, lead):
    rows = []
    for k in _VECTORS:
        r = parts[k].reshape(lead + (-1, LANES))
        pad = -r.shape[-2] % PACK_ROWS
        rows.append(jnp.pad(r, [(0, 0)] * len(lead) + [(0, pad), (0, 0)]))
    return jnp.concatenate(rows, axis=len(lead))


def _unpack_vectors(buf, lead):
    out, r0 = {}, 0
    for k in _VECTORS:
        shard = _shard_shape(k)
        rows = math.prod(shard) // LANES
        out[k] = buf[..., r0:r0 + rows, :].reshape(lead + shard)
        r0 += rows + (-rows % PACK_ROWS)
    return out


class _LayerWeights:
    def __init__(self, shards):
        first, ffn0 = _layer_matrices(0)[:-2], _layer_matrices(0)[-2:]
        qkv0, wo0 = [shards[k][j].astype(BF16) for k, j in first]
        half = qkv0.shape[0] // 2
        state, token = _gather_begin([qkv0[:half], qkv0[half:], wo0, _pack_vectors(shards, ())],
                                     [ROUTE_A, ROUTE_B, ROUTE_A, ROUTE_B], "gather0", None)
        self.cast = {k: (shards[k] + token[0, 0]).astype(BF16) for k in _MATRICES}
        state, _ = _gather_next(state, self._send(ffn0) + self._send(_layer_matrices(1)))
        state, _ = _gather_next(state, None)
        outs, _ = _gather_next(state, None)
        vec = _unpack_vectors(outs[-1], (N_DEV,))
        self.vec = {k: _full_from_slots(vec[k], k) for k in _VECTORS}
        self.vec["ffn_w_dw"] = [vec["ffn_w_dw"][:, l] for l in range(DEPTH)]
        self.ready = {0: self._unpack(first, [jnp.concatenate(outs[:2], axis=1), outs[2]])}
        self.chains = {}
        tokens = []
        self._begin("ffn0", ffn0, "gather0f", outs[0], tokens)
        self._begin(1, _layer_matrices(1), "gather1", outs[0], tokens)
        self.vec["norm_g"] = self.vec["norm_g"] + (tokens[0] + tokens[1])

    def _send(self, items):
        return [self.cast[k][j] for k, j in items]

    @staticmethod
    def _unpack(items, outs):
        return {k: o if k in _FFN else _full_from_slots(o[:, None], k, layers=1)[0] for (k, _), o in zip(items, outs)}

    def _begin(self, key, items, name, after, tokens):
        state, token = _gather_begin(self._send(items), [_route(k) for k, _ in items], name, after)
        self.chains[key] = (items, state)
        tokens.append(token)

    def _advance(self, key, after, tokens):
        items, state = self.chains.pop(key)
        state, token = _gather_next(state, after)
        if token is None:
            self.ready.setdefault(0 if key == "ffn0" else key, {}).update(self._unpack(items, state))
        else:
            self.chains[key] = (items, state)
            tokens.append(token)

    def layer(self, i):
        return self.ready[i]

    def hook(self, i, point, after):
        tokens = []
        if i == 0 and point == 0:
            self._advance("ffn0", after, tokens)
        if i == 0 and point == 1:
            self._advance("ffn0", after, tokens)
            self._advance("ffn0", None, tokens)
        if point >= 1 and i + 1 in self.chains:
            self._advance(i + 1, after, tokens)
        if point == 1 and i + 2 < DEPTH:
            self._begin(i + 2, _layer_matrices(i + 2), f"gather{i + 2}", after, tokens)
        return functools.reduce(lambda a, b: a + b, tokens) if tokens else None


def _layer_slots(g, name):
    shape, ax, _ = _WEIGHTS[name]
    shape, ax = shape[1:], ax - 1
    split = shape[:ax] + (N_DEV, shape[ax] // N_DEV) + shape[ax + 1:]
    return jnp.moveaxis(g.reshape(split), ax, 0).astype(BF16)


class _GradSink:
    def __init__(self):
        self.state = None
        self.ffn_state = None
        self.sums = {}
        self.last = None

    def ffn_done(self, i, grads):
        if i != 0:
            return None
        self.ffn_items = _layer_matrices(0)[-2:]
        self.ffn_state, token = _scatter_begin([grads[k] for k, _ in self.ffn_items],
                                               [_route(k) for k, _ in self.ffn_items],
                                               [f"{k}{j}" for k, j in self.ffn_items], "scatter0f")
        return token

    def layer_done(self, i, grads):
        items = _layer_matrices(i)
        if i == 0:
            items = items[:-2]
            self.last = (items, [_layer_slots(grads[k], k) for k, _ in items])
            return None
        slots = [grads[k] if k in _FFN else _layer_slots(grads[k], k) for k, _ in items]
        self.items = items
        self.state, token = _scatter_begin(slots, [_route(k) for k, _ in items], [f"{k}{j}" for k, j in items],
                                           f"scatter{i}", (N_DEV, 3, 2 * D_FF // N_DEV))
        return token

    def hook(self, i, point, after):
        tokens = []
        if self.state is not None and point in (1, 2, 3):
            self.state, token = _scatter_next(self.state, after)
            if point == 3:
                self.sums.update(dict(zip(self.items, self.state)))
                self.state = None
            tokens.append(token)
        if self.ffn_state is not None and point in ("a", "b", 3):
            self.ffn_state, token = _scatter_next(self.ffn_state, after)
            if point == 3:
                self.sums.update(dict(zip(self.ffn_items, self.ffn_state)))
                self.ffn_state = None
            tokens.append(token)
        tokens = [t for t in tokens if t is not None]
        return functools.reduce(lambda a, b: a + b, tokens) if tokens else None


def _adamw(w, g, m, v, name, layer=None, prev=None, dep=None):
    shape = w.shape
    cols = shape[-1]
    view = shape if len(shape) == 3 else (1, math.prod(shape[:-1]), cols)
    layers, rows, _ = view
    tr = _pick(rows, (512, 352, 288, 256, 128, 64, 32, 16, 8))
    n_prev = 0 if prev is None else 3
    lead = ([] if prev is None else [p.reshape(view) for p in prev]) + ([] if dep is None else [dep])

    def body(*refs):
        w_ref, g_ref, m_ref, v_ref = refs[len(lead):len(lead) + 4]
        d_ref, nm_ref, nv_ref = refs[len(lead) + 4:]
        gv = g_ref[...]
        nm = ADAM_B1 * m_ref[...] + (1.0 - ADAM_B1) * gv
        nv = ADAM_B2 * v_ref[...] + (1.0 - ADAM_B2) * jnp.square(gv)
        m_hat = nm / (1.0 - ADAM_B1 ** ADAM_STEP)
        v_hat = nv / (1.0 - ADAM_B2 ** ADAM_STEP)
        d_ref[...] = -ADAM_LR * (m_hat / (jnp.sqrt(v_hat) + ADAM_EPS) + ADAM_WD * w_ref[...])
        nm_ref[...] = nm
        nv_ref[...] = nv

    if layer is None:
        grid = (layers, rows // tr)
        blk = gblk = pl.BlockSpec((None, tr, cols), lambda l, i: (l, i, 0))
        gview = view
    else:
        grid = (rows // tr,)
        blk = pl.BlockSpec((None, tr, cols), lambda i: (layer, i, 0))
        gblk = pl.BlockSpec((tr, cols), lambda i: (i, 0))
        gview = (rows, cols)
    shp = jax.ShapeDtypeStruct(view, F32)
    outs = pl.pallas_call(
        body, name=name, grid=grid, in_specs=[_ANY] * len(lead) + [blk, gblk, blk, blk], out_specs=[blk] * 3,
        out_shape=[shp] * 3, input_output_aliases={i: i for i in range(n_prev)},
        compiler_params=_params(("parallel",) * len(grid)),
    )(*lead, w.reshape(view), g.reshape(gview), m.reshape(view), v.reshape(view))
    return [o.reshape(shape) for o in outs]


def kernel(x, norm_g, attn_w_qkv, attn_w_o, conv_w_in, conv_w_dw, conv_w_out, pool_w_in, pool_w_grp, pool_scale, pool_w_out, ffn_w_up, ffn_w_dw, ffn_w_down, loss_target, m_norm_g, m_attn_w_qkv, m_attn_w_o, m_conv_w_in, m_conv_w_dw, m_conv_w_out, m_pool_w_in, m_pool_w_grp, m_pool_scale, m_pool_w_out, m_ffn_w_up, m_ffn_w_dw, m_ffn_w_down, v_norm_g, v_attn_w_qkv, v_attn_w_o, v_conv_w_in, v_conv_w_dw, v_conv_w_out, v_pool_w_in, v_pool_w_grp, v_pool_scale, v_pool_w_out, v_ffn_w_up, v_ffn_w_dw, v_ffn_w_down):
    shards = dict(zip(_NAMES, (norm_g, attn_w_qkv, attn_w_o, conv_w_in, conv_w_dw, conv_w_out, pool_w_in,
                               pool_w_grp, pool_scale, pool_w_out, ffn_w_up, ffn_w_dw, ffn_w_down)))
    moms = dict(zip(_NAMES, (m_norm_g, m_attn_w_qkv, m_attn_w_o, m_conv_w_in, m_conv_w_dw, m_conv_w_out,
                             m_pool_w_in, m_pool_w_grp, m_pool_scale, m_pool_w_out, m_ffn_w_up, m_ffn_w_dw,
                             m_ffn_w_down)))
    vels = dict(zip(_NAMES, (v_norm_g, v_attn_w_qkv, v_attn_w_o, v_conv_w_in, v_conv_w_dw, v_conv_w_out,
                             v_pool_w_in, v_pool_w_grp, v_pool_scale, v_pool_w_out, v_ffn_w_up, v_ffn_w_dw,
                             v_ffn_w_down)))
    weights = _LayerWeights(shards)
    sink = _GradSink()
    loss, grad_x, vec_grads = _local_step(x[0], loss_target[0], weights.vec, weights, sink)
    loss = lax.psum(loss[0, 0], _AXES)

    items, slots = sink.last
    vec_slots = {k: _slots_from_full(vec_grads[k], k) for k in _VECTORS if k != "ffn_w_dw"}
    vec_slots["ffn_w_dw"] = jnp.stack(vec_grads["ffn_w_dw"], axis=1)
    state, token = _scatter_begin(slots + [_pack_vectors(vec_slots, (N_DEV,)).astype(BF16)],
                                  [_route(k) for k, _ in items] + [ROUTE_B],
                                  [f"{k}{j}" for k, j in items] + ["vectors"], "scatter0")
    results = {}

    flipped = {k for k in _MATRICES if _WEIGHTS[k][0][0] > 1 and _shard_shape(k)[-1] % LANES}
    wmv = {k: [t.transpose(0, 2, 1) if k in flipped else t for t in (shards[k], moms[k], vels[k])] for k in _MATRICES}

    def step(matrices, dep=None):
        outs = []
        for k, j in matrices:
            g = sink.sums[(k, j)]
            w, m, v = wmv[k]
            if _WEIGHTS[k][0][0] == 1:
                results[k] = (g[None], _adamw(w, g[None], m, v, f"adamw_{k}", dep=dep))
            else:
                gs, prev = results.get(k, ({}, None))
                gs[j] = g
                results[k] = (gs, _adamw(w, g.T if k in flipped else g, m, v, f"adamw_{k}{j}", layer=j, prev=prev, dep=dep))
            outs.append(results[k][1][0])
        return outs

    state, token = _scatter_next(state, step(_layer_matrices(3), token))
    state, token = _scatter_next(state, step(_layer_matrices(2) + _layer_matrices(1), token))
    sums, _ = _scatter_next(state, step(_layer_matrices(0)[-2:], token))
    sink.sums.update(dict(zip(items, sums[:-1])))
    step(items)
    vec_sums = _unpack_vectors(sums[-1], ())
    for k in _VECTORS:
        results[k] = (vec_sums[k], _adamw(shards[k], vec_sums[k], moms[k], vels[k], f"adamw_{k}"))
    grads_out = {k: g if not isinstance(g, dict) else jnp.stack([g[j] for j in range(len(g))])
                 for k, (g, _) in results.items()}
    stepped = {k: [o.transpose(0, 2, 1) if k in flipped else o for o in outs] for k, (_, outs) in results.items()}
    return (loss, grad_x[None], *[grads_out[k] for k in _NAMES], *[stepped[k][0] for k in _NAMES],
            *[stepped[k][1] for k in _NAMES], *[stepped[k][2] for k in _NAMES])
```

```python
import functools
import math

import numpy as np
import jax
import jax.numpy as jnp
from jax import lax
from jax.experimental import pallas as pl
from jax.experimental.pallas import tpu as pltpu

F32, BF16 = jnp.float32, jnp.bfloat16

D_MODEL = 1024
SEQ = 2048
DEPTH = 4
DILATED_CFG = ((128, 1), (512, 4), (2048, 16))
N_GROUPS_A = 3
HEADS = 8
HEAD_DIM = 64
ATTN_WIDTH = HEADS * HEAD_DIM
N_HEADS_A = N_GROUPS_A * HEADS
BLOCK = 128
NEG_INF = -1e30
POOL_GROUP_DIM = 256
D_FF = 2816
RMS_EPS = 1e-6
ADAM_LR, ADAM_B1, ADAM_B2, ADAM_EPS, ADAM_WD, ADAM_STEP = 0.001, 0.9, 0.999, 1e-08, 0.01, 10

N_DEV = 8
LANES = 128
V7X_VMEM_BYTES = 64 * 2 ** 20
VMEM_LIMIT_BYTES = V7X_VMEM_BYTES - 8 * 2 ** 20
COL_TILE = 256
ROW_TILE = 1024
MATMUL_TILES = (1024, 1408, 512, 256, 128)
TN_RESIDENT_K = 2048

NN = (((1,), (0,)), ((), ()))
NT = (((1,), (1,)), ((), ()))
TN = (((0,), (0,)), ((), ()))


def _dot(a, b, dims=NN):
    return lax.dot_general(a, b, dims, preferred_element_type=F32)


def _params(sem=None):
    return pltpu.CompilerParams(dimension_semantics=sem, vmem_limit_bytes=VMEM_LIMIT_BYTES)


def _pick(n, prefs):
    for p in prefs:
        if n % p == 0:
            return p
    return n


def _matmul(a, b, mode, out_dtype, name, a_parts=1, b_parts=1):
    if mode == "nn":
        m, k = a.shape[-2], a.shape[-1] * a_parts
        n = b.shape[-1] * b_parts
    elif mode == "nt":
        m, k = a.shape[-2], a.shape[-1] * a_parts
        n = b.shape[-2]
    else:
        k, m = a.shape[-2], a.shape[-1] * a_parts
        n = b.shape[-1] * b_parts
    tm = _pick(m, MATMUL_TILES)
    tn = _pick(n // b_parts if mode != "nt" else n, MATMUL_TILES)
    kk = k // a_parts if mode != "tn" else k
    tk = _pick(kk, MATMUL_TILES)
    if mode == "tn":
        tm = _pick(m // a_parts, MATMUL_TILES)
        if k <= TN_RESIDENT_K:
            tk = k
    gm, gn, gk = m // tm, n // tn, k // tk

    def a_idx(i, j, kq):
        if mode == "tn":
            r, c, per = kq, i, (m // a_parts) // tm
        else:
            r, c, per = i, kq, (k // a_parts) // tk
        return (r, c) if a_parts == 1 else (c // per, r, c % per)

    def b_idx(i, j, kq):
        if mode == "nt":
            return (j, kq)
        per = (n // b_parts) // tn
        return (kq, j) if b_parts == 1 else (j // per, kq, j % per)

    a_blk = (tk, tm) if mode == "tn" else (tm, tk)
    b_blk = (tn, tk) if mode == "nt" else (tk, tn)
    if a_parts > 1:
        a_blk = (None,) + a_blk
    if b_parts > 1:
        b_blk = (None,) + b_blk
    dims = {"nn": NN, "nt": NT, "tn": TN}[mode]

    def body_single(a_ref, b_ref, o_ref):
        o_ref[...] = _dot(a_ref[...], b_ref[...], dims).astype(o_ref.dtype)

    def body(a_ref, b_ref, o_ref, acc_ref):
        kq = pl.program_id(2)

        @pl.when(kq == 0)
        def _():
            acc_ref[...] = jnp.zeros_like(acc_ref)

        acc_ref[...] += _dot(a_ref[...], b_ref[...], dims)

        @pl.when(kq == gk - 1)
        def _():
            o_ref[...] = acc_ref[...].astype(o_ref.dtype)

    return pl.pallas_call(
        body_single if gk == 1 else body, name=name, grid=(gm, gn, gk),
        in_specs=[pl.BlockSpec(a_blk, a_idx), pl.BlockSpec(b_blk, b_idx)],
        out_specs=pl.BlockSpec((tm, tn), lambda i, j, kq: (i, j)),
        out_shape=jax.ShapeDtypeStruct((m, n), out_dtype),
        scratch_shapes=[] if gk == 1 else [pltpu.VMEM((tm, tn), F32)],
        compiler_params=_params(("parallel", "parallel", "arbitrary")),
    )(a, b)


def _rms_fwd(xin, g, res, out_dtype, name):
    s, d = xin.shape
    has_res = res is not None

    def body(*refs):
        x_ref, g_ref = refs[0], refs[1]
        o_ref = refs[-1]
        x = x_ref[...]
        r = lax.rsqrt(jnp.mean(x * x, axis=-1, keepdims=True) + RMS_EPS)
        y = x * r * g_ref[...]
        if has_res:
            y = refs[2][...] + y
        o_ref[...] = y.astype(o_ref.dtype)

    row = pl.BlockSpec((ROW_TILE, d), lambda i: (i, 0))
    vec = pl.BlockSpec((1, d), lambda i: (0, 0))
    ins = [xin, g] + ([res] if has_res else [])
    return pl.pallas_call(
        body, name=name, grid=(s // ROW_TILE,),
        in_specs=[row, vec] + ([row] if has_res else []),
        out_specs=row, out_shape=jax.ShapeDtypeStruct((s, d), out_dtype),
        compiler_params=_params(("parallel",)),
    )(*ins)


def _rms_bwd(xin, g, dy, dres, out_dtype, name):
    s, d = xin.shape
    has_res = dres is not None

    def body(*refs):
        x_ref, g_ref, dy_ref = refs[0], refs[1], refs[2]
        dx_ref, dg_ref = refs[-2], refs[-1]

        @pl.when(pl.program_id(0) == 0)
        def _():
            dg_ref[...] = jnp.zeros_like(dg_ref)

        x = x_ref[...]
        dyv = dy_ref[...].astype(F32)
        r = lax.rsqrt(jnp.mean(x * x, axis=-1, keepdims=True) + RMS_EPS)
        xhat = x * r
        u = dyv * g_ref[...]
        dx = r * (u - xhat * jnp.mean(u * xhat, axis=-1, keepdims=True))
        if has_res:
            dx = refs[3][...] + dx
        dx_ref[...] = dx.astype(dx_ref.dtype)
        dg_ref[...] += jnp.sum(dyv * xhat, axis=0, keepdims=True)

    row = pl.BlockSpec((ROW_TILE, d), lambda i: (i, 0))
    vec = pl.BlockSpec((1, d), lambda i: (0, 0))
    ins = [xin, g, dy] + ([dres] if has_res else [])
    return pl.pallas_call(
        body, name=name, grid=(s // ROW_TILE,),
        in_specs=[row, vec, row] + ([row] if has_res else []),
        out_specs=[row, vec],
        out_shape=[jax.ShapeDtypeStruct((s, d), out_dtype), jax.ShapeDtypeStruct((1, d), F32)],
        compiler_params=_params(("arbitrary",)),
    )(*ins)


def _rms(x):
    r = lax.rsqrt(jnp.mean(x * x, axis=-1, keepdims=True) + RMS_EPS)
    return r, x * r


def _rms_grad(r, xhat, dy, g):
    u = dy * g
    return r * (u - xhat * jnp.mean(u * xhat, axis=-1, keepdims=True))


def _rms_res_pre(sub, g_post, res, g_pre, name):
    s, d = sub.shape

    def body(sub_ref, gp_ref, res_ref, gn_ref, x_ref, n_ref):
        xnew = res_ref[...] + _rms(sub_ref[...])[1] * gp_ref[...]
        x_ref[...] = xnew
        n_ref[...] = (_rms(xnew)[1] * gn_ref[...]).astype(BF16)

    row = pl.BlockSpec((ROW_TILE, d), lambda i: (i, 0))
    vec = pl.BlockSpec((1, d), lambda i: (0, 0))
    return pl.pallas_call(
        body, name=name, grid=(s // ROW_TILE,),
        in_specs=[row, vec, row, vec], out_specs=[row, row],
        out_shape=[jax.ShapeDtypeStruct((s, d), F32), jax.ShapeDtypeStruct((s, d), BF16)],
        compiler_params=_params(("parallel",)),
    )(sub, g_post, res, g_pre)


def _rms_bwd_pair(xmid, g_pre, dn, dres, sub, g_post, name):
    s, d = xmid.shape

    def body(x_ref, gn_ref, dn_ref, dres_ref, sub_ref, gp_ref, dx_ref, dsub_ref, dgn_ref, dgp_ref):
        @pl.when(pl.program_id(0) == 0)
        def _():
            dgn_ref[...] = jnp.zeros_like(dgn_ref)
            dgp_ref[...] = jnp.zeros_like(dgp_ref)

        dnv = dn_ref[...].astype(F32)
        r, xhat = _rms(x_ref[...])
        dx = dres_ref[...] + _rms_grad(r, xhat, dnv, gn_ref[...])
        dx_ref[...] = dx
        dgn_ref[...] += jnp.sum(dnv * xhat, axis=0, keepdims=True)
        rs, shat = _rms(sub_ref[...])
        dsub_ref[...] = _rms_grad(rs, shat, dx, gp_ref[...]).astype(BF16)
        dgp_ref[...] += jnp.sum(dx * shat, axis=0, keepdims=True)

    row = pl.BlockSpec((ROW_TILE, d), lambda i: (i, 0))
    vec = pl.BlockSpec((1, d), lambda i: (0, 0))
    return pl.pallas_call(
        body, name=name, grid=(s // ROW_TILE,),
        in_specs=[row, vec, row, row, row, vec], out_specs=[row, row, vec, vec],
        out_shape=[jax.ShapeDtypeStruct((s, d), F32), jax.ShapeDtypeStruct((s, d), BF16),
                   jax.ShapeDtypeStruct((1, d), F32), jax.ShapeDtypeStruct((1, d), F32)],
        compiler_params=_params(("arbitrary",)),
    )(xmid, g_pre, dn, dres, sub, g_post)


def _loss_head(y, tgt, name):
    s, d = y.shape

    def body(y_ref, t_ref, l_ref, dy_ref):
        @pl.when(pl.program_id(0) == 0)
        def _():
            l_ref[...] = jnp.zeros_like(l_ref)

        e = y_ref[...] - t_ref[...]
        dy_ref[...] = e / d
        per_tok = jnp.mean(e * e, axis=-1, keepdims=True)
        l_ref[...] += 0.5 * jnp.sum(per_tok, axis=0, keepdims=True)

    row = pl.BlockSpec((ROW_TILE, d), lambda i: (i, 0))
    return pl.pallas_call(
        body, name=name, grid=(s // ROW_TILE,),
        in_specs=[row, row],
        out_specs=[pl.BlockSpec((1, 1), lambda i: (0, 0)), row],
        out_shape=[jax.ShapeDtypeStruct((1, 1), F32), jax.ShapeDtypeStruct((s, d), F32)],
        compiler_params=_params(("arbitrary",)),
    )(y, tgt)


SUBLANES = 8


def _shift_down(x, k):
    t, c = x.shape
    r = pltpu.roll(x.reshape(t // SUBLANES, SUBLANES, c), k, axis=1)
    above = jnp.concatenate([jnp.zeros((1, SUBLANES, c), x.dtype), r[:-1]], axis=0)
    rows = lax.broadcasted_iota(jnp.int32, (1, SUBLANES, c), 1)
    return jnp.where(rows >= k, r, above).reshape(t, c)


def _shift_up(x, k):
    t, c = x.shape
    r = pltpu.roll(x.reshape(t // SUBLANES, SUBLANES, c), SUBLANES - k, axis=1)
    below = jnp.concatenate([r[1:], jnp.zeros((1, SUBLANES, c), x.dtype)], axis=0)
    rows = lax.broadcasted_iota(jnp.int32, (1, SUBLANES, c), 1)
    return jnp.where(rows < SUBLANES - k, r, below).reshape(t, c)


def _conv3(h, w):
    return w[2:3] * h + w[1:2] * _shift_down(h, 1) + w[0:1] * _shift_down(h, 2)


def _conv3_bwd(dc, h, w, dw_ref, cols=slice(None)):
    u1, u2 = _shift_up(dc, 1), _shift_up(dc, 2)
    dw_ref[0:1, cols] = jnp.sum(u2 * h, axis=0, keepdims=True)
    dw_ref[1:2, cols] = jnp.sum(u1 * h, axis=0, keepdims=True)
    dw_ref[2:3, cols] = jnp.sum(dc * h, axis=0, keepdims=True)
    return w[2:3] * dc + w[1:2] * u1 + w[0:1] * u2


FFN_PAIRS = N_DEV // 2


def _lane_chunks(width):
    return [(c0, min(COL_TILE, width - c0)) for c0 in range(0, width, COL_TILE)]


def _ffn_up(n, wup, wdw, name):
    s, d = n.shape
    cw = wup.shape[-1]

    def body(n_ref, wg_ref, wu_ref, dg_ref, du_ref, h_ref, c_ref, a_ref):
        x = n_ref[...]
        for c0, size in _lane_chunks(cw):
            cols = slice(c0, c0 + size)
            hg = _dot(x, wg_ref[:, cols])
            hu = _dot(x, wu_ref[:, cols])
            h_ref[0, :, cols] = hg.astype(BF16)
            h_ref[1, :, cols] = hu.astype(BF16)
            cg = _conv3(hg, dg_ref[:, cols])
            cu = _conv3(hu, du_ref[:, cols])
            c_ref[0, :, cols] = cg.astype(BF16)
            c_ref[1, :, cols] = cu.astype(BF16)
            a_ref[:, cols] = (cg * jax.nn.sigmoid(cg) * cu).astype(BF16)

    return pl.pallas_call(
        body, name=name, grid=(FFN_PAIRS,),
        in_specs=[pl.BlockSpec((s, d), lambda j: (0, 0)),
                  pl.BlockSpec((None, d, cw), lambda j: (j, 0, 0)),
                  pl.BlockSpec((None, d, cw), lambda j: (j + FFN_PAIRS, 0, 0)),
                  pl.BlockSpec((None, 3, cw), lambda j: (j, 0, 0)),
                  pl.BlockSpec((None, 3, cw), lambda j: (j + FFN_PAIRS, 0, 0))],
        out_specs=[pl.BlockSpec((None, 2, s, cw), lambda j: (j, 0, 0, 0)),
                   pl.BlockSpec((None, 2, s, cw), lambda j: (j, 0, 0, 0)),
                   pl.BlockSpec((None, s, cw), lambda j: (j, 0, 0))],
        out_shape=[jax.ShapeDtypeStruct((FFN_PAIRS, 2, s, cw), BF16), jax.ShapeDtypeStruct((FFN_PAIRS, 2, s, cw), BF16),
                   jax.ShapeDtypeStruct((FFN_PAIRS, s, cw), BF16)],
        compiler_params=_params(("parallel",)),
    )(n, wup, wup, wdw, wdw)


def _ffn_mid_bwd(do, wdown, h, c, wdw, name):
    s, d = do.shape
    cw = wdown.shape[1]

    def body(do_ref, wd_ref, h_ref, c_ref, wg_ref, wu_ref, dh_ref, dwg_ref, dwu_ref):
        dov = do_ref[...]
        for c0, size in _lane_chunks(cw):
            cols = slice(c0, c0 + size)
            da = _dot(dov, wd_ref[cols, :], NT)
            hg = h_ref[0, :, cols].astype(F32)
            hu = h_ref[1, :, cols].astype(F32)
            wg, wu = wg_ref[:, cols], wu_ref[:, cols]
            cg = c_ref[0, :, cols].astype(F32)
            cu = c_ref[1, :, cols].astype(F32)
            sg = jax.nn.sigmoid(cg)
            dcu = da * (cg * sg)
            dcg = da * cu * (sg * (1.0 + cg * (1.0 - sg)))
            dh_ref[0, :, cols] = _conv3_bwd(dcg, hg, wg, dwg_ref, cols).astype(BF16)
            dh_ref[1, :, cols] = _conv3_bwd(dcu, hu, wu, dwu_ref, cols).astype(BF16)

    vec = jax.ShapeDtypeStruct((FFN_PAIRS, 3, cw), F32)
    return pl.pallas_call(
        body, name=name, grid=(FFN_PAIRS,),
        in_specs=[pl.BlockSpec((s, d), lambda j: (0, 0)), pl.BlockSpec((None, cw, d), lambda j: (j, 0, 0)),
                  pl.BlockSpec((None, 2, s, cw), lambda j: (j, 0, 0, 0)),
                  pl.BlockSpec((None, 2, s, cw), lambda j: (j, 0, 0, 0)),
                  pl.BlockSpec((None, 3, cw), lambda j: (j, 0, 0)),
                  pl.BlockSpec((None, 3, cw), lambda j: (j + FFN_PAIRS, 0, 0))],
        out_specs=[pl.BlockSpec((None, 2, s, cw), lambda j: (j, 0, 0, 0)),
                   pl.BlockSpec((None, 3, cw), lambda j: (j, 0, 0)), pl.BlockSpec((None, 3, cw), lambda j: (j, 0, 0))],
        out_shape=[jax.ShapeDtypeStruct((FFN_PAIRS, 2, s, cw), BF16), vec, vec],
        compiler_params=_params(("parallel",)),
    )(do, wdown, h, c, wdw, wdw)


def _ffn_dwup(n, dh, name, dep=None):
    s, d = n.shape
    cw = dh.shape[-1]
    deps = [] if dep is None else [dep]

    def body(n_ref, dh_ref, *rest):
        rest[-1][...] = _dot(n_ref[...], dh_ref[...], TN).astype(BF16)

    return pl.pallas_call(
        body, name=name, grid=(N_DEV,),
        in_specs=[pl.BlockSpec((s, d), lambda k: (0, 0)),
                  pl.BlockSpec((None, None, s, cw), lambda k: (k % FFN_PAIRS, k // FFN_PAIRS, 0, 0))] + [_ANY] * len(deps),
        out_specs=pl.BlockSpec((None, d, cw), lambda k: (k, 0, 0)),
        out_shape=jax.ShapeDtypeStruct((N_DEV, d, cw), BF16),
        compiler_params=_params(("parallel",)),
    )(n, dh, *deps)


def _ffn_dn(dh, wup, name, dep=None):
    s, cw = dh.shape[-2:]
    d = wup.shape[1]
    tm = _pick(s, MATMUL_TILES)
    deps = [] if dep is None else [dep]

    def body(dh_ref, w_ref, *rest):
        o_ref, acc_ref = rest[-2:]
        k = pl.program_id(1)

        @pl.when(k == 0)
        def _():
            acc_ref[...] = jnp.zeros_like(acc_ref)

        acc_ref[...] += _dot(dh_ref[...], w_ref[...], NT)

        @pl.when(k == N_DEV - 1)
        def _():
            o_ref[...] = acc_ref[...]

    return pl.pallas_call(
        body, name=name, grid=(s // tm, N_DEV),
        in_specs=[pl.BlockSpec((None, None, tm, cw), lambda i, k: (k % FFN_PAIRS, k // FFN_PAIRS, i, 0)),
                  pl.BlockSpec((None, d, cw), lambda i, k: (k, 0, 0))] + [_ANY] * len(deps),
        out_specs=pl.BlockSpec((tm, d), lambda i, k: (i, 0)),
        out_shape=jax.ShapeDtypeStruct((s, d), F32),
        scratch_shapes=[pltpu.VMEM((tm, d), F32)],
        compiler_params=_params(("parallel", "arbitrary")),
    )(dh, wup, *deps)


def _sconv_fwd(n, win, wdw, name):
    s, d = n.shape
    tn = COL_TILE
    nj = d // tn

    def body(n_ref, wb_ref, wc_ref, wh_ref, dw_ref, z_ref, y_ref):
        x = n_ref[...]
        zb = _dot(x, wb_ref[...])
        zc = _dot(x, wc_ref[...])
        zh = _dot(x, wh_ref[...])
        z_ref[0] = zb.astype(BF16)
        z_ref[1] = zc.astype(BF16)
        z_ref[2] = zh.astype(BF16)
        y_ref[...] = (zb * _conv3(zc * zh, dw_ref[...])).astype(BF16)

    return pl.pallas_call(
        body, name=name, grid=(nj,),
        in_specs=[pl.BlockSpec((s, d), lambda j: (0, 0)),
                  pl.BlockSpec((d, tn), lambda j: (0, j)), pl.BlockSpec((d, tn), lambda j: (0, j + nj)),
                  pl.BlockSpec((d, tn), lambda j: (0, j + 2 * nj)), pl.BlockSpec((3, tn), lambda j: (0, j))],
        out_specs=[pl.BlockSpec((3, s, tn), lambda j: (0, 0, j)), pl.BlockSpec((s, tn), lambda j: (0, j))],
        out_shape=[jax.ShapeDtypeStruct((3, s, d), BF16), jax.ShapeDtypeStruct((s, d), BF16)],
        compiler_params=_params(("parallel",)),
    )(n, win, win, win, wdw)


def _sconv_mid_bwd(dm, wout, z, wdw, name):
    s, d = dm.shape
    tn = COL_TILE
    nj = d // tn

    def body(dm_ref, wo_ref, z_ref, w_ref, dz_ref, dw_ref):
        dy = _dot(dm_ref[...], wo_ref[...], NT)
        zb = z_ref[0].astype(F32)
        zc = z_ref[1].astype(F32)
        zh = z_ref[2].astype(F32)
        w = w_ref[...]
        p = zc * zh
        cp = _conv3(p, w)
        dz_ref[0] = (dy * cp).astype(BF16)
        dcp = dy * zb
        dp = _conv3_bwd(dcp, p, w, dw_ref)
        dz_ref[1] = (dp * zh).astype(BF16)
        dz_ref[2] = (dp * zc).astype(BF16)

    return pl.pallas_call(
        body, name=name, grid=(nj,),
        in_specs=[pl.BlockSpec((s, d), lambda j: (0, 0)), pl.BlockSpec((tn, d), lambda j: (j, 0)),
                  pl.BlockSpec((3, s, tn), lambda j: (0, 0, j)), pl.BlockSpec((3, tn), lambda j: (0, j))],
        out_specs=[pl.BlockSpec((3, s, tn), lambda j: (0, 0, j)), pl.BlockSpec((3, tn), lambda j: (0, j))],
        out_shape=[jax.ShapeDtypeStruct((3, s, d), BF16), jax.ShapeDtypeStruct((3, d), F32)],
        compiler_params=_params(("parallel",)),
    )(dm, wout, z, wdw)


def _pool_select(g, c2, c4, c8, c16):
    return jnp.where(g == 0, c2, jnp.where(g == 1, c4, jnp.where(g == 2, c8, c16)))


def _pool_inv_count(g, shape):
    pos = lax.broadcasted_iota(jnp.int32, shape, 0).astype(F32) + 1.0
    win = (2 << g).astype(F32)
    return jnp.minimum(pos, win)


def _pool_fwd(n, win, wgrp, scale, name):
    s, d = n.shape
    tn = POOL_GROUP_DIM

    def body(n_ref, wi_ref, wg_ref, sc_ref, p_ref, y_ref):
        g = pl.program_id(0)
        u = _dot(n_ref[...], wi_ref[...])
        s2 = u + _shift_down(u, 1)
        s4 = s2 + _shift_down(s2, 2)
        s8 = s4 + _shift_down(s4, 4)
        s16 = s8 + _shift_down(s8, 8)
        tot = _pool_select(g, s2, s4, s8, s16)
        p = (tot / _pool_inv_count(g, u.shape) - u).astype(BF16)
        p_ref[...] = p
        y_ref[...] = (_dot(p, wg_ref[...]) * sc_ref[...]).astype(BF16)

    return pl.pallas_call(
        body, name=name, grid=(d // tn,),
        in_specs=[pl.BlockSpec((s, d), lambda g: (0, 0)), pl.BlockSpec((d, tn), lambda g: (0, g)),
                  pl.BlockSpec((None, tn, tn), lambda g: (g, 0, 0)), pl.BlockSpec((1, tn), lambda g: (0, g))],
        out_specs=[pl.BlockSpec((s, tn), lambda g: (0, g)), pl.BlockSpec((s, tn), lambda g: (0, g))],
        out_shape=[jax.ShapeDtypeStruct((s, d), BF16), jax.ShapeDtypeStruct((s, d), BF16)],
        compiler_params=_params(("parallel",)),
    )(n, win, wgrp, scale)


def _pool_mid_bwd(dm, wout, p, wgrp, scale, name):
    s, d = dm.shape
    tn = POOL_GROUP_DIM

    def body(dm_ref, wo_ref, p_ref, wg_ref, sc_ref, du_ref, dwg_ref, dsc_ref):
        g = pl.program_id(0)
        dy = _dot(dm_ref[...], wo_ref[...], NT)
        pv = p_ref[...]
        wg = wg_ref[...]
        ypre = _dot(pv, wg)
        dsc_ref[...] = jnp.sum(dy * ypre, axis=0, keepdims=True)
        dypre = (dy * sc_ref[...]).astype(BF16)
        dwg_ref[...] = _dot(pv, dypre, TN)
        dp = _dot(dypre, wg, NT)
        e = dp / _pool_inv_count(g, dp.shape)
        f2 = e + _shift_up(e, 1)
        f4 = f2 + _shift_up(f2, 2)
        f8 = f4 + _shift_up(f4, 4)
        f16 = f8 + _shift_up(f8, 8)
        du_ref[...] = (_pool_select(g, f2, f4, f8, f16) - dp).astype(BF16)

    return pl.pallas_call(
        body, name=name, grid=(d // tn,),
        in_specs=[pl.BlockSpec((s, d), lambda g: (0, 0)), pl.BlockSpec((tn, d), lambda g: (g, 0)),
                  pl.BlockSpec((s, tn), lambda g: (0, g)), pl.BlockSpec((None, tn, tn), lambda g: (g, 0, 0)),
                  pl.BlockSpec((1, tn), lambda g: (0, g))],
        out_specs=[pl.BlockSpec((s, tn), lambda g: (0, g)), pl.BlockSpec((None, tn, tn), lambda g: (g, 0, 0)),
                   pl.BlockSpec((1, tn), lambda g: (0, g))],
        out_shape=[jax.ShapeDtypeStruct((s, d), BF16), jax.ShapeDtypeStruct((4, tn, tn), F32),
                   jax.ShapeDtypeStruct((1, d), F32)],
        compiler_params=_params(("parallel",)),
    )(dm, wout, p, wgrp, scale)


PANEL = LANES
ATTN_EXT = ATTN_WIDTH + PANEL
DVEC_LANE = HEADS


def _alibi_slopes(g, dil):
    all_slopes = 2.0 ** (-8.0 * np.arange(1, N_HEADS_A + 1) / N_HEADS_A)
    return [float(np.float32(sl) * np.float32(dil)) for sl in all_slopes[g * HEADS:(g + 1) * HEADS]]


def _residue_order(a, dil, name):
    s, w = a.shape
    per = ROW_TILE // dil
    panels = w // PANEL

    def body(a_ref, o_ref, *tiles):
        for c in range(panels):
            cols = slice(c * PANEL, (c + 1) * PANEL)
            tiles[c][...] = a_ref[:, cols].astype(F32)
            for r in range(dil):
                o_ref[r, :, cols] = tiles[c][pl.ds(r, per, stride=dil), :].astype(o_ref.dtype)

    out = pl.pallas_call(
        body, name=name, grid=(s // ROW_TILE,),
        in_specs=[pl.BlockSpec((ROW_TILE, w), lambda i: (i, 0))],
        out_specs=pl.BlockSpec((dil, per, w), lambda i: (0, i, 0)),
        out_shape=jax.ShapeDtypeStruct((dil, s // dil, w), a.dtype),
        scratch_shapes=[pltpu.VMEM((ROW_TILE, PANEL), F32)] * panels,
        compiler_params=_params(("parallel",)),
    )(a)
    return out.reshape(s, w)


def _token_order(a, dil, acc, name):
    s, w = a.shape
    per = ROW_TILE // dil
    panels = w // PANEL
    has_acc = acc is not None

    def body(*refs):
        a_ref = refs[0]
        o_ref = refs[2] if has_acc else refs[1]
        tiles = refs[3:] if has_acc else refs[2:]
        for c in range(panels):
            cols = slice(c * PANEL, (c + 1) * PANEL)
            for r in range(dil):
                tiles[c][pl.ds(r, per, stride=dil), :] = a_ref[r, :, cols]
            v = tiles[c][...]
            if has_acc:
                v = v + refs[1][:, cols]
            o_ref[:, cols] = v

    row = pl.BlockSpec((ROW_TILE, w), lambda i: (i, 0))
    return pl.pallas_call(
        body, name=name, grid=(s // ROW_TILE,),
        in_specs=[pl.BlockSpec((dil, per, w), lambda i: (0, i, 0))] + ([row] if has_acc else []),
        out_specs=row, out_shape=jax.ShapeDtypeStruct((s, w), F32),
        scratch_shapes=[pltpu.VMEM((ROW_TILE, PANEL), F32)] * panels,
        compiler_params=_params(("parallel",)),
    )(*([a.reshape(dil, s // dil, w)] + ([acc] if has_acc else [])))


def _qkv_proj(n, wqkv, g, name):
    s, d = n.shape
    tm = _pick(s, MATMUL_TILES)

    def body(a_ref, b_ref, o_ref):
        o_ref[...] = _dot(a_ref[...], b_ref[...]).astype(BF16)

    return pl.pallas_call(
        body, name=name, grid=(s // tm, 3),
        in_specs=[pl.BlockSpec((tm, d), lambda i, t: (i, 0)),
                  pl.BlockSpec((d, ATTN_WIDTH), lambda i, t: (0, 3 * g + t))],
        out_specs=pl.BlockSpec((None, tm, ATTN_WIDTH), lambda i, t: (t, i, 0)),
        out_shape=jax.ShapeDtypeStruct((3, s, ATTN_WIDTH), BF16),
        compiler_params=_params(("parallel", "parallel")),
    )(n, wqkv)


def _attn_window(n, ln):
    if ln == BLOCK:
        return 0, BLOCK
    return pl.multiple_of(jnp.maximum(n - 1, 0) * BLOCK, BLOCK), 2 * BLOCK


def _attn_mask(n, k0, kw):
    qpos = n * BLOCK + lax.broadcasted_iota(jnp.int32, (BLOCK, kw), 0)
    kpos = k0 + lax.broadcasted_iota(jnp.int32, (BLOCK, kw), 1)
    dist = qpos - kpos
    return dist.astype(F32), (dist >= 0) & (dist <= BLOCK)


def _attn_scores(q, keys, slope, dist, valid):
    s = _dot(q, keys, NT) * (HEAD_DIM ** -0.5) - slope * dist
    return jnp.where(valid, s, NEG_INF)


ATTN_STEP_BLOCKS = 1
ATTN_BWD_STEP_BLOCKS = 4


def _attn_block(gb, ln):
    nb = ln // BLOCK
    n, base = (0, gb * ln) if nb == 1 else (gb % nb, (gb // nb) * ln)
    k0, kw = _attn_window(n, ln)
    cur = pl.ds(pl.multiple_of(gb * BLOCK, BLOCK), BLOCK)
    win = pl.ds(pl.multiple_of(base + k0, BLOCK), kw)
    return cur, win, n, k0, kw


def _attn_fwd(qkv, g, name):
    _, s, w = qkv.shape
    dil = DILATED_CFG[g][1]
    ln = s // dil
    slopes = _alibi_slopes(g, dil)
    rows = ATTN_STEP_BLOCKS * BLOCK

    def body(qkv_ref, o_ref):
        o_ref[:, w:] = jnp.zeros((rows, PANEL), F32)
        for b in range(ATTN_STEP_BLOCKS):
            cur, win, n, k0, kw = _attn_block(pl.program_id(0) * ATTN_STEP_BLOCKS + b, ln)
            dist, valid = _attn_mask(n, k0, kw)
            out = slice(b * BLOCK, (b + 1) * BLOCK)
            for h in range(HEADS):
                cols = slice(h * HEAD_DIM, (h + 1) * HEAD_DIM)
                sc = _attn_scores(qkv_ref[0, cur, cols], qkv_ref[1, win, cols], slopes[h], dist, valid)
                m = jnp.max(sc, axis=-1, keepdims=True)
                p = jnp.exp(sc - m)
                den = jnp.sum(p, axis=-1, keepdims=True)
                o_ref[out, cols] = _dot(p.astype(BF16), qkv_ref[2, win, cols]) / den
                o_ref[out, w + h:w + h + 1] = m + jnp.log(den)

    return pl.pallas_call(
        body, name=name, grid=(s // rows,),
        in_specs=[pl.BlockSpec((3, s, w), lambda i: (0, 0, 0))],
        out_specs=pl.BlockSpec((rows, ATTN_EXT), lambda i: (i, 0)),
        out_shape=jax.ShapeDtypeStruct((s, ATTN_EXT), F32),
        compiler_params=_params(("parallel",)),
    )(qkv)


def _attn_bwd(qkv, dext, g, name, dep=None):
    _, s, w = qkv.shape
    dil = DILATED_CFG[g][1]
    ln = s // dil
    slopes = _alibi_slopes(g, dil)
    scale = HEAD_DIM ** -0.5
    rows = ATTN_BWD_STEP_BLOCKS * BLOCK
    steps = s // rows
    deps = [] if dep is None else [dep]

    def body(qkv_ref, de_ref, *rest):
        d_ref, dk_ref, dv_ref = rest[-3:]

        @pl.when(pl.program_id(0) == 0)
        def _():
            dk_ref[...] = jnp.zeros_like(dk_ref)
            dv_ref[...] = jnp.zeros_like(dv_ref)

        for b in range(ATTN_BWD_STEP_BLOCKS):
            cur, win, n, k0, kw = _attn_block(pl.program_id(0) * ATTN_BWD_STEP_BLOCKS + b, ln)
            dist, valid = _attn_mask(n, k0, kw)
            blk = slice(b * BLOCK, (b + 1) * BLOCK)
            for h in range(HEADS):
                cols = slice(h * HEAD_DIM, (h + 1) * HEAD_DIM)
                q, keys = qkv_ref[0, cur, cols], qkv_ref[1, win, cols]
                dob = de_ref[blk, cols].astype(BF16)
                p = jnp.exp(_attn_scores(q, keys, slopes[h], dist, valid) - de_ref[blk, w + h:w + h + 1])
                dd = de_ref[blk, w + DVEC_LANE + h:w + DVEC_LANE + h + 1]
                ds = (p * (_dot(dob, qkv_ref[2, win, cols], NT) - dd)).astype(BF16)
                d_ref[0, cur, cols] = (scale * _dot(ds, keys)).astype(BF16)
                dv_ref[win, cols] += _dot(p.astype(BF16), dob, TN)
                dk_ref[win, cols] += scale * _dot(ds, q, TN)

        @pl.when(pl.program_id(0) == steps - 1)
        def _():
            d_ref[1] = dk_ref[...].astype(BF16)
            d_ref[2] = dv_ref[...].astype(BF16)

    whole = pl.BlockSpec((3, s, w), lambda i: (0, 0, 0))
    return pl.pallas_call(
        body, name=name, grid=(steps,),
        in_specs=[whole, pl.BlockSpec((rows, ATTN_EXT), lambda i: (i, 0))] + [_ANY] * len(deps),
        out_specs=whole, out_shape=jax.ShapeDtypeStruct((3, s, w), BF16),
        scratch_shapes=[pltpu.VMEM((s, w), F32), pltpu.VMEM((s, w), F32)],
        compiler_params=_params(("arbitrary",)),
    )(qkv, dext, *deps)


def _attn_merge(e0, e1, e2, name):
    s = e0.shape[0]
    w = ATTN_WIDTH

    def body(e0_ref, e1_ref, e2_ref, m_ref, mb_ref, lse_ref):
        refs = (e0_ref, e1_ref, e2_ref)
        l = [r[:, w:w + HEADS] for r in refs]
        mx = jnp.maximum(jnp.maximum(l[0], l[1]), l[2])
        e = [jnp.exp(v - mx) for v in l]
        z = e[0] + e[1] + e[2]
        lse_ref[...] = mx + jnp.log(z)
        wts = [v / z for v in e]
        for h in range(HEADS):
            cols = slice(h * HEAD_DIM, (h + 1) * HEAD_DIM)
            acc = wts[0][:, h:h + 1] * refs[0][:, cols]
            for g in range(1, N_GROUPS_A):
                acc = acc + wts[g][:, h:h + 1] * refs[g][:, cols]
            m_ref[:, cols] = acc
            mb_ref[:, cols] = acc.astype(BF16)

    ext = pl.BlockSpec((ROW_TILE, ATTN_EXT), lambda i: (i, 0))
    row = pl.BlockSpec((ROW_TILE, w), lambda i: (i, 0))
    return pl.pallas_call(
        body, name=name, grid=(s // ROW_TILE,),
        in_specs=[ext, ext, ext],
        out_specs=[row, row, pl.BlockSpec((ROW_TILE, HEADS), lambda i: (i, 0))],
        out_shape=[jax.ShapeDtypeStruct((s, w), F32), jax.ShapeDtypeStruct((s, w), BF16),
                   jax.ShapeDtypeStruct((s, HEADS), F32)],
        compiler_params=_params(("parallel",)),
    )(e0, e1, e2)


def _attn_dvec(dmerged, merged, lse_all, name, dep=None):
    s, w = merged.shape
    deps = [] if dep is None else [dep]

    def body(dm_ref, m_ref, lse_ref, *rest):
        de_ref = rest[-1]
        dmv = dm_ref[...]
        de_ref[:, :w] = dmv
        de_ref[:, w:] = jnp.zeros((ROW_TILE, PANEL), F32)
        de_ref[:, w:w + HEADS] = lse_ref[...]
        prod = dmv * m_ref[...]
        for h in range(HEADS):
            lane = w + DVEC_LANE + h
            de_ref[:, lane:lane + 1] = jnp.sum(prod[:, h * HEAD_DIM:(h + 1) * HEAD_DIM], axis=-1, keepdims=True)

    row = pl.BlockSpec((ROW_TILE, w), lambda i: (i, 0))
    return pl.pallas_call(
        body, name=name, grid=(s // ROW_TILE,),
        in_specs=[row, row, pl.BlockSpec((ROW_TILE, HEADS), lambda i: (i, 0))] + [_ANY] * len(deps),
        out_specs=pl.BlockSpec((ROW_TILE, ATTN_EXT), lambda i: (i, 0)),
        out_shape=jax.ShapeDtypeStruct((s, ATTN_EXT), F32),
        compiler_params=_params(("parallel",)),
    )(dmerged, merged, lse_all, *deps)


def _attention_fwd(n, wqkv, wo, tag):
    ns, qkvs, exts = [], [], []
    for g, (_, dil) in enumerate(DILATED_CFG):
        ng = n if dil == 1 else _residue_order(n, dil, f"{tag}_order_g{g}")
        qkv = _qkv_proj(ng, wqkv, g, f"{tag}_qkv_g{g}")
        ext = _attn_fwd(qkv, g, f"{tag}_fwd_g{g}")
        ns.append(ng)
        qkvs.append(qkv)
        exts.append(ext if dil == 1 else _token_order(ext, dil, None, f"{tag}_unorder_g{g}"))
    merged, merged_bf, lse_all = _attn_merge(*exts, f"{tag}_merge")
    m = _matmul(merged_bf, wo, "nn", F32, f"{tag}_wo")
    return m, (ns, qkvs, merged, merged_bf, lse_all)


def _attention_bwd(dm, wqkv, wo, saved, tag, dep=None, hook=None):
    ns, qkvs, merged, merged_bf, lse_all = saved
    d_wo = _matmul(merged_bf, dm, "tn", BF16, f"{tag}_dwo")
    dmerged = _matmul(dm, wo, "nt", F32, f"{tag}_dmerged")
    dext = _attn_dvec(dmerged, merged, lse_all, f"{tag}_dvec", dep)
    width = 3 * ATTN_WIDTH
    d_wqkv, dn, dep = [], None, None
    for g, (_, dil) in enumerate(DILATED_CFG):
        dext_g = dext if dil == 1 else _residue_order(dext, dil, f"{tag}_dorder_g{g}")
        dqkv = _attn_bwd(qkvs[g], dext_g, g, f"{tag}_bwd_g{g}", dep)
        dep = hook(g, dqkv) if hook is not None and g + 1 < N_GROUPS_A else None
        d_wqkv.append(_matmul(ns[g], dqkv, "tn", BF16, f"{tag}_dwqkv_g{g}", b_parts=3))
        dn_g = _matmul(dqkv, wqkv[:, g * width:(g + 1) * width], "nt", F32, f"{tag}_dn_g{g}", a_parts=3)
        dn = dn_g if dil == 1 else _token_order(dn_g, dil, dn, f"{tag}_dn_sum_g{g}")
    return dn, jnp.concatenate(d_wqkv, axis=1), d_wo


def _layer_matrices(i):
    mixer = (("attn_w_qkv", "attn_w_o"), ("conv_w_in", "conv_w_out"), ("pool_w_in", "pool_w_grp", "pool_w_out"))[i % 3]
    return [(k, i // 3) for k in mixer] + [("ffn_w_up", i), ("ffn_w_down", i)]


def _local_step(x, tgt, vec, weights, sink):
    ng = vec["norm_g"]

    def gain(i, j, token=None):
        g = ng[i, j][None, :]
        return g if token is None else g + token

    saved = []
    n = _rms_fwd(x, gain(0, 0), None, BF16, "norm_first")
    for i in range(DEPTH):
        wl = weights.layer(i)
        t0 = weights.hook(i, 0, n)
        kind, idx = i % 3, i // 3
        if kind == 0:
            m, ms = _attention_fwd(n, wl["attn_w_qkv"], wl["attn_w_o"], "attn")
        elif kind == 1:
            taps = vec["conv_w_dw"][idx] if t0 is None else vec["conv_w_dw"][idx] + t0
            z, y = _sconv_fwd(n, wl["conv_w_in"], taps, "sconv_fwd")
            m = _matmul(y, wl["conv_w_out"], "nn", F32, "sconv_out")
            ms = (z, y)
        else:
            scale = vec["pool_scale"][idx][None, :] if t0 is None else vec["pool_scale"][idx][None, :] + t0
            p, y = _pool_fwd(n, wl["pool_w_in"], wl["pool_w_grp"], scale, "pool_fwd")
            m = _matmul(y, wl["pool_w_out"], "nn", F32, "pool_out")
            ms = (p, y)
        t1 = weights.hook(i, 1, m)
        x1, n2 = _rms_res_pre(m, gain(i, 1, t0), x, gain(i, 2, t1), "norm_res_pre")
        h, c, a = _ffn_up(n2, wl["ffn_w_up"], vec["ffn_w_dw"][i], "ffn_up")
        t2 = weights.hook(i, 2, a)
        f = _matmul(a, wl["ffn_w_down"].reshape(D_FF, D_MODEL), "nn", F32, "ffn_down", a_parts=FFN_PAIRS)
        saved.append((x, n, m, ms, x1, n2, h, a, f, wl, c))
        if i + 1 < DEPTH:
            x, n = _rms_res_pre(f, gain(i, 3, t2), x1, gain(i + 1, 0), "norm_res_pre")
        else:
            x = _rms_fwd(f, gain(i, 3), x1, F32, "norm_res")
        weights.hook(i, 3, x)

    loss, dx = _loss_head(x, tgt, "loss_head")

    g_norm = [[None] * 4 for _ in range(DEPTH)]
    g_taps, g_scale, g_ffn_dw = [], [], [None] * DEPTH
    df, g_norm[DEPTH - 1][3] = _rms_bwd(saved[-1][8], gain(DEPTH - 1, 3), dx, None, BF16, "norm_bwd_sub")
    t0 = None
    for i in reversed(range(DEPTH)):
        xin, n, m, ms, x1, n2, h, a, f, wl, c = saved[i]
        kind, idx = i % 3, i // 3
        gl = {}
        d_wdown = _matmul(a, df, "tn", BF16, "ffn_dwdown", a_parts=FFN_PAIRS)
        gl["ffn_w_down"] = d_wdown.reshape(N_DEV, D_FF // N_DEV, D_MODEL)
        ffn_taps = vec["ffn_w_dw"][i] if t0 is None else vec["ffn_w_dw"][i] + t0
        dh, dwg, dwu = _ffn_mid_bwd(df, wl["ffn_w_down"].reshape(FFN_PAIRS, -1, D_MODEL), h, c, ffn_taps, "ffn_mid_bwd")
        g_ffn_dw[i] = jnp.concatenate([dwg, dwu], axis=0)
        t1 = sink.hook(i, 1, dh)
        gl["ffn_w_up"] = _ffn_dwup(n2, dh, "ffn_dwup", t1)
        tf = sink.ffn_done(i, gl)
        dn2 = _ffn_dn(dh, wl["ffn_w_up"], "ffn_dn", t1)
        dx1, dm, g_norm[i][2], g_norm[i][1] = _rms_bwd_pair(x1, gain(i, 2, tf), dn2, dx, m, gain(i, 1), "norm_bwd_pair")
        t2 = sink.hook(i, 2, dm)
        if kind == 0:
            dn, gl["attn_w_qkv"], gl["attn_w_o"] = _attention_bwd(
                dm, wl["attn_w_qkv"], wl["attn_w_o"], ms, "attn", t2, lambda g, after, i=i: sink.hook(i, ("a", "b")[g], after))
        elif kind == 1:
            z, y = ms
            gl["conv_w_out"] = _matmul(y, dm, "tn", BF16, "sconv_dwout")
            taps = vec["conv_w_dw"][idx] if t2 is None else vec["conv_w_dw"][idx] + t2
            dz, ddw = _sconv_mid_bwd(dm, wl["conv_w_out"], z, taps, "sconv_mid_bwd")
            g_taps.append(ddw)
            gl["conv_w_in"] = _matmul(n, dz, "tn", BF16, "sconv_dwin", b_parts=3)
            dn = _matmul(dz, wl["conv_w_in"], "nt", F32, "sconv_dn", a_parts=3)
        else:
            p, y = ms
            gl["pool_w_out"] = _matmul(y, dm, "tn", BF16, "pool_dwout")
            scale = vec["pool_scale"][idx][None, :] if t2 is None else vec["pool_scale"][idx][None, :] + t2
            du, gl["pool_w_grp"], dscale = _pool_mid_bwd(dm, wl["pool_w_out"], p, wl["pool_w_grp"], scale, "pool_mid_bwd")
            g_scale.append(dscale[0])
            gl["pool_w_in"] = _matmul(n, du, "tn", BF16, "pool_dwin")
            dn = _matmul(du, wl["pool_w_in"], "nt", F32, "pool_dn")
        sink.hook(i, 3, dn)
        if i > 0:
            dx, df, g_norm[i][0], g_norm[i - 1][3] = _rms_bwd_pair(xin, gain(i, 0, t2), dn, dx1, saved[i - 1][8],
                                                                   gain(i - 1, 3), "norm_bwd_pair")
        else:
            dx, g_norm[0][0] = _rms_bwd(xin, gain(0, 0), dn, dx1, F32, "norm_bwd_res")
        t0 = sink.layer_done(i, gl)

    vec_grads = {"norm_g": jnp.stack([jnp.concatenate(row, axis=0) for row in g_norm]), "conv_w_dw": jnp.stack(g_taps),
                 "pool_scale": jnp.stack(g_scale), "ffn_w_dw": g_ffn_dw}
    return loss, dx, vec_grads


_AXES = ("x", "y", "c")
ROUTE_A = ("y", "x", "c")
ROUTE_B = ("x", "y", "c")
def _dev_index(pos):
    return 4 * pos["x"] + 2 * pos["y"] + pos["c"]


_HBM = pl.BlockSpec(memory_space=pltpu.HBM)
_SEM = pl.BlockSpec(memory_space=pltpu.SEMAPHORE)
_ANY = pl.BlockSpec(memory_space=pl.ANY)
_EFFECT = pltpu.SideEffectType.DATAFLOW_SIDE_EFFECTING


TOKEN_SHAPE = (1, D_MODEL)


def _copies_start(describe, arrays, n_copies, name, after, token_shape=TOKEN_SHAPE):
    n = len(arrays)
    deps = [] if after is None else [after]

    def body(*refs):
        send_sems, recv_sems = refs[n + len(deps)], refs[n + len(deps) + 1]
        for c in describe(refs[:n], send_sems, recv_sems):
            c.start()
        refs[-1][...] = jnp.zeros_like(refs[-1])

    outs = pl.pallas_call(
        body, name=f"{name}_start",
        out_shape=(pltpu.SemaphoreType.DMA((n_copies,)), pltpu.SemaphoreType.DMA((n_copies,)),
                   *[pltpu.HBM(a.shape, a.dtype) for a in arrays], jax.ShapeDtypeStruct(token_shape, F32)),
        in_specs=[_HBM] * n + [_ANY] * len(deps),
        out_specs=(_SEM, _SEM, *([_HBM] * n), pl.BlockSpec(memory_space=pltpu.VMEM)),
        input_output_aliases={i: 2 + i for i in range(n)},
        compiler_params=pltpu.CompilerParams(has_side_effects=_EFFECT),
    )(*[pltpu.with_memory_space_constraint(a, pltpu.HBM) for a in arrays], *deps)
    return (outs[0], outs[1], list(outs[2:2 + n])), outs[-1]


def _copies_wait(describe, handle, name, after):
    send_sems, recv_sems, arrays = handle
    n = len(arrays)
    deps = [] if after is None else list(after) if isinstance(after, (list, tuple)) else [after]

    def body(*refs):
        for c in describe(refs[:n], refs[n], refs[n + 1]):
            c.wait_send()
            c.wait_recv()

    outs = pl.pallas_call(
        body, name=f"{name}_wait",
        out_shape=tuple(pltpu.HBM(a.shape, a.dtype) for a in arrays),
        in_specs=[_HBM] * n + [_SEM, _SEM] + [_ANY] * len(deps), out_specs=tuple([_HBM] * n),
        input_output_aliases={i: i for i in range(n)},
        compiler_params=pltpu.CompilerParams(has_side_effects=_EFFECT),
    )(*arrays, send_sems, recv_sems, *deps)
    return list(outs)


GATHER_STAGE_COPIES = (3, 3, 1)


def _gather_copies(stage, routes):
    n = len(routes)

    def describe(refs, send_sems, recv_sems):
        pos = {a: lax.axis_index(a) for a in _AXES}

        def flipped(axes):
            return {a: 1 - pos[a] if a in axes else pos[a] for a in _AXES}

        copies = []
        for i, (a1, a2, a3) in enumerate(routes):
            land = refs[n + i] if stage == 1 else refs[i]
            p1, p2, p12, p3 = flipped((a1,)), flipped((a2,)), flipped((a1, a2)), flipped((a3,))
            plan = {1: [(None, p1), (None, p2), (None, p3)], 2: [(p1, p2), (p1, p3), (p2, p3)], 3: [(p12, p3)]}[stage]
            for holder, to in plan:
                slot = land.at[_dev_index(pos if holder is None else holder)]
                k = len(copies)
                copies.append(pltpu.make_async_remote_copy(
                    src_ref=refs[i] if holder is None else slot, dst_ref=slot,
                    send_sem=send_sems.at[k], recv_sem=recv_sems.at[k],
                    device_id=tuple(to[a] for a in _AXES), device_id_type=pl.DeviceIdType.MESH))
        return copies

    return describe


def _gather_begin(shards, routes, name, after):
    n = len(shards)
    lands = [lax.empty((N_DEV,) + a.shape, a.dtype) for a in shards]
    handle, token = _copies_start(_gather_copies(1, routes), list(shards) + lands, GATHER_STAGE_COPIES[0] * n,
                                  f"{name}_1", after)
    return {"stage": 1, "handle": handle, "routes": routes, "name": name, "n": n}, token


def _gather_next(state, after):
    stage, routes, name, n = state["stage"], state["routes"], state["name"], state["n"]
    arrays = _copies_wait(_gather_copies(stage, routes), state["handle"], f"{name}_{stage}", after)
    if stage == 1:
        state = dict(state, shards=arrays[:n])
        arrays = arrays[n:]
    if stage == 3:
        me = _dev_index({a: lax.axis_index(a) for a in _AXES})
        return [lax.dynamic_update_index_in_dim(o, s, me, 0) for o, s in zip(arrays, state["shards"])], None
    handle, token = _copies_start(_gather_copies(stage + 1, routes), arrays, GATHER_STAGE_COPIES[stage] * n,
                                  f"{name}_{stage + 1}", None)
    return dict(state, stage=stage + 1, handle=handle), token


ADD_ROW_TILES = (1024, 704, 512, 352, 256, 128, 96, 64, 32, 16)


def _add_half(a, recv, me, out_dtype, name):
    p, q, cols = recv.shape
    tr = _pick(q, ADD_ROW_TILES)

    def body(me_ref, a_ref, b_ref, o_ref):
        o_ref[...] = (a_ref[...].astype(F32) + b_ref[...].astype(F32)).astype(o_ref.dtype)

    return pl.pallas_call(
        body, name=name,
        grid_spec=pltpu.PrefetchScalarGridSpec(
            num_scalar_prefetch=1, grid=(p, q // tr),
            in_specs=[pl.BlockSpec((None, None, tr, cols), lambda j, i, m: (j, m[0], i, 0)),
                      pl.BlockSpec((None, tr, cols), lambda j, i, m: (j, i, 0))],
            out_specs=pl.BlockSpec((None, tr, cols), lambda j, i, m: (j, i, 0))),
        out_shape=jax.ShapeDtypeStruct((p, q, cols), out_dtype),
        compiler_params=_params(("parallel", "parallel")),
    )(me, a, recv)


def _half_copies(axes):
    n = len(axes)

    def describe(refs, send_sems, recv_sems):
        pos = {a: lax.axis_index(a) for a in _AXES}
        copies = []
        for i, axis in enumerate(axes):
            peer = tuple(1 - pos[a] if a == axis else pos[a] for a in _AXES)
            copies.append(pltpu.make_async_remote_copy(
                src_ref=refs[i].at[:, 1 - pos[axis]], dst_ref=refs[n + i], send_sem=send_sems.at[i],
                recv_sem=recv_sems.at[i], device_id=peer, device_id_type=pl.DeviceIdType.MESH))
        return copies

    return describe


def _scatter_begin(slots, routes, tags, name, token_shape=TOKEN_SHAPE):
    shapes = [a.shape[1:] for a in slots]
    rows = [math.prod(s[:-1]) for s in shapes]
    arrays = [a.reshape(4, 2, n, s[-1]) for a, n, s in zip(slots, rows, shapes)]
    return _scatter_start({"stage": 0, "arrays": arrays, "routes": routes, "tags": tags, "name": name,
                           "shapes": shapes, "rows": rows}, token_shape)


def _scatter_start(state, token_shape=TOKEN_SHAPE):
    stage, arrays = state["stage"], state["arrays"]
    axes = [r[2 - stage] for r in state["routes"]]
    lands = [lax.empty((a.shape[0],) + a.shape[2:], a.dtype) for a in arrays]
    handle, token = _copies_start(_half_copies(axes), arrays + lands, len(arrays), f"{state['name']}_{stage + 1}", None,
                                  token_shape)
    return dict(state, handle=handle, axes=axes), token


def _scatter_next(state, after):
    stage, axes, n = state["stage"], state["axes"], len(state["arrays"])
    both = _copies_wait(_half_copies(axes), state["handle"], f"{state['name']}_{stage + 1}", after)
    coord = {a: lax.axis_index(a).astype(jnp.int32).reshape(1) for a in _AXES}
    sums = [_add_half(a, r, coord[ax], F32 if stage == 2 else BF16, f"scatter_add_{stage + 1}_{t}")
            for a, r, ax, t in zip(both[:n], both[n:], axes, state["tags"])]
    if stage == 2:
        return [a.reshape(s) for a, s in zip(sums, state["shapes"])], None
    if stage == 0:
        views = [(1, 2, 2 * r, s[-1]) if route[1] == "x" else (2, 2, r, s[-1])
                 for r, s, route in zip(state["rows"], state["shapes"], state["routes"])]
    else:
        views = [(1, 2, r, s[-1]) for r, s in zip(state["rows"], state["shapes"])]
    return _scatter_start(dict(state, stage=stage + 1, arrays=[a.reshape(v) for a, v in zip(sums, views)]))


_WEIGHTS = {
    "norm_g": ((DEPTH, 4, D_MODEL), 2, True),
    "attn_w_qkv": ((2, D_MODEL, 4608), 2, False),
    "attn_w_o": ((2, ATTN_WIDTH, D_MODEL), 2, False),
    "conv_w_in": ((1, D_MODEL, 3 * D_MODEL), 2, False),
    "conv_w_dw": ((1, 3, D_MODEL), 2, True),
    "conv_w_out": ((1, D_MODEL, D_MODEL), 1, False),
    "pool_w_in": ((1, D_MODEL, D_MODEL), 1, False),
    "pool_w_grp": ((1, 4, POOL_GROUP_DIM, POOL_GROUP_DIM), 2, False),
    "pool_scale": ((1, D_MODEL), 1, True),
    "pool_w_out": ((1, D_MODEL, D_MODEL), 1, False),
    "ffn_w_up": ((DEPTH, D_MODEL, 2 * D_FF), 2, False),
    "ffn_w_dw": ((DEPTH, 3, 2 * D_FF), 2, True),
    "ffn_w_down": ((DEPTH, D_FF, D_MODEL), 1, False),
}
_NAMES = tuple(_WEIGHTS)
_VECTORS = tuple(k for k in _NAMES if _WEIGHTS[k][2])
_MATRICES = tuple(k for k in _NAMES if not _WEIGHTS[k][2])
_FFN = ("ffn_w_up", "ffn_w_down")
_ON_ROUTE_A = ("ffn_w_up", "attn_w_o", "conv_w_out", "pool_w_in")
PACK_ROWS = 16


def _route(name):
    return ROUTE_A if name in _ON_ROUTE_A else ROUTE_B


def _shard_shape(name):
    shape, ax, _ = _WEIGHTS[name]
    return tuple(s // N_DEV if i == ax else s for i, s in enumerate(shape))


def _full_from_slots(slots, name, layers=None):
    shape, ax, _ = _WEIGHTS[name]
    if layers is not None:
        shape = (layers,) + shape[1:]
    return jnp.moveaxis(slots, 0, ax).reshape(shape)


def _slots_from_full(full, name):
    shape, ax, _ = _WEIGHTS[name]
    split = shape[:ax] + (N_DEV, shape[ax] // N_DEV) + shape[ax + 1:]
    return jnp.moveaxis(full.reshape(split), ax, 0)


def _pack_vectors(parts, lead):
    rows = []
    for k in _VECTORS:
        r = parts[k].reshape(lead + (-1, LANES))
        pad = -r.shape[-2] % PACK_ROWS
        rows.append(jnp.pad(r, [(0, 0)] * len(lead) + [(0, pad), (0, 0)]))
    return jnp.concatenate(rows, axis=len(lead))


def _unpack_vectors(buf, lead):
    out, r0 = {}, 0
    for k in _VECTORS:
        shard = _shard_shape(k)
        rows = math.prod(shard) // LANES
        out[k] = buf[..., r0:r0 + rows, :].reshape(lead + shard)
        r0 += rows + (-rows % PACK_ROWS)
    return out


class _LayerWeights:
    def __init__(self, shards):
        first, ffn0 = _layer_matrices(0)[:-2], _layer_matrices(0)[-2:]
        qkv0, wo0 = [shards[k][j].astype(BF16) for k, j in first]
        half = qkv0.shape[0] // 2
        state, token = _gather_begin([qkv0[:half], qkv0[half:], wo0, _pack_vectors(shards, ())],
                                     [ROUTE_A, ROUTE_B, ROUTE_A, ROUTE_B], "gather0", None)
        self.cast = {k: (shards[k] + token[0, 0]).astype(BF16) for k in _MATRICES}
        state, _ = _gather_next(state, self._send(ffn0) + self._send(_layer_matrices(1)))
        state, _ = _gather_next(state, None)
        outs, _ = _gather_next(state, None)
        vec = _unpack_vectors(outs[-1], (N_DEV,))
        self.vec = {k: _full_from_slots(vec[k], k) for k in _VECTORS}
        self.vec["ffn_w_dw"] = [vec["ffn_w_dw"][:, l] for l in range(DEPTH)]
        self.ready = {0: self._unpack(first, [jnp.concatenate(outs[:2], axis=1), outs[2]])}
        self.chains = {}
        tokens = []
        self._begin("ffn0", ffn0, "gather0f", outs[0], tokens)
        self._begin(1, _layer_matrices(1), "gather1", outs[0], tokens)
        self.vec["norm_g"] = self.vec["norm_g"] + (tokens[0] + tokens[1])

    def _send(self, items):
        return [self.cast[k][j] for k, j in items]

    @staticmethod
    def _unpack(items, outs):
        return {k: o if k in _FFN else _full_from_slots(o[:, None], k, layers=1)[0] for (k, _), o in zip(items, outs)}

    def _begin(self, key, items, name, after, tokens):
        state, token = _gather_begin(self._send(items), [_route(k) for k, _ in items], name, after)
        self.chains[key] = (items, state)
        tokens.append(token)

    def _advance(self, key, after, tokens):
        items, state = self.chains.pop(key)
        state, token = _gather_next(state, after)
        if token is None:
            self.ready.setdefault(0 if key == "ffn0" else key, {}).update(self._unpack(items, state))
        else:
            self.chains[key] = (items, state)
            tokens.append(token)

    def layer(self, i):
        return self.ready[i]

    def hook(self, i, point, after):
        tokens = []
        if i == 0 and point == 0:
            self._advance("ffn0", after, tokens)
        if i == 0 and point == 1:
            self._advance("ffn0", after, tokens)
            self._advance("ffn0", None, tokens)
        if point >= 1 and i + 1 in self.chains:
            self._advance(i + 1, after, tokens)
        if point == 1 and i + 2 < DEPTH:
            self._begin(i + 2, _layer_matrices(i + 2), f"gather{i + 2}", after, tokens)
        return functools.reduce(lambda a, b: a + b, tokens) if tokens else None


def _layer_slots(g, name):
    shape, ax, _ = _WEIGHTS[name]
    shape, ax = shape[1:], ax - 1
    split = shape[:ax] + (N_DEV, shape[ax] // N_DEV) + shape[ax + 1:]
    return jnp.moveaxis(g.reshape(split), ax, 0).astype(BF16)


class _GradSink:
    def __init__(self):
        self.state = None
        self.ffn_state = None
        self.sums = {}
        self.last = None

    def ffn_done(self, i, grads):
        if i != 0:
            return None
        self.ffn_items = _layer_matrices(0)[-2:]
        self.ffn_state, token = _scatter_begin([grads[k] for k, _ in self.ffn_items],
                                               [_route(k) for k, _ in self.ffn_items],
                                               [f"{k}{j}" for k, j in self.ffn_items], "scatter0f")
        return token

    def layer_done(self, i, grads):
        items = _layer_matrices(i)
        if i == 0:
            items = items[:-2]
            self.last = (items, [_layer_slots(grads[k], k) for k, _ in items])
            return None
        slots = [grads[k] if k in _FFN else _layer_slots(grads[k], k) for k, _ in items]
        self.items = items
        self.state, token = _scatter_begin(slots, [_route(k) for k, _ in items], [f"{k}{j}" for k, j in items],
                                           f"scatter{i}", (N_DEV, 3, 2 * D_FF // N_DEV))
        return token

    def hook(self, i, point, after):
        tokens = []
        if self.state is not None and point in (1, 2, 3):
            self.state, token = _scatter_next(self.state, after)
            if point == 3:
                self.sums.update(dict(zip(self.items, self.state)))
                self.state = None
            tokens.append(token)
        if self.ffn_state is not None and point in ("a", "b", 3):
            self.ffn_state, token = _scatter_next(self.ffn_state, after)
            if point == 3:
                self.sums.update(dict(zip(self.ffn_items, self.ffn_state)))
                self.ffn_state = None
            tokens.append(token)
        tokens = [t for t in tokens if t is not None]
        return functools.reduce(lambda a, b: a + b, tokens) if tokens else None


def _adamw(w, g, m, v, name, layer=None, prev=None, dep=None):
    shape = w.shape
    cols = shape[-1]
    view = shape if len(shape) == 3 else (1, math.prod(shape[:-1]), cols)
    layers, rows, _ = view
    tr = _pick(rows, (512, 352, 288, 256, 128, 64, 32, 16, 8))
    n_prev = 0 if prev is None else 3
    lead = ([] if prev is None else [p.reshape(view) for p in prev]) + ([] if dep is None else [dep])

    def body(*refs):
        w_ref, g_ref, m_ref, v_ref = refs[len(lead):len(lead) + 4]
        d_ref, nm_ref, nv_ref = refs[len(lead) + 4:]
        gv = g_ref[...]
        nm = ADAM_B1 * m_ref[...] + (1.0 - ADAM_B1) * gv
        nv = ADAM_B2 * v_ref[...] + (1.0 - ADAM_B2) * jnp.square(gv)
        m_hat = nm / (1.0 - ADAM_B1 ** ADAM_STEP)
        v_hat = nv / (1.0 - ADAM_B2 ** ADAM_STEP)
        d_ref[...] = -ADAM_LR * (m_hat / (jnp.sqrt(v_hat) + ADAM_EPS) + ADAM_WD * w_ref[...])
        nm_ref[...] = nm
        nv_ref[...] = nv

    if layer is None:
        grid = (layers, rows // tr)
        blk = gblk = pl.BlockSpec((None, tr, cols), lambda l, i: (l, i, 0))
        gview = view
    else:
        grid = (rows // tr,)
        blk = pl.BlockSpec((None, tr, cols), lambda i: (layer, i, 0))
        gblk = pl.BlockSpec((tr, cols), lambda i: (i, 0))
        gview = (rows, cols)
    shp = jax.ShapeDtypeStruct(view, F32)
    outs = pl.pallas_call(
        body, name=name, grid=grid, in_specs=[_ANY] * len(lead) + [blk, gblk, blk, blk], out_specs=[blk] * 3,
        out_shape=[shp] * 3, input_output_aliases={i: i for i in range(n_prev)},
        compiler_params=_params(("parallel",) * len(grid)),
    )(*lead, w.reshape(view), g.reshape(gview), m.reshape(view), v.reshape(view))
    return [o.reshape(shape) for o in outs]


def kernel(x, norm_g, attn_w_qkv, attn_w_o, conv_w_in, conv_w_dw, conv_w_out, pool_w_in, pool_w_grp, pool_scale, pool_w_out, ffn_w_up, ffn_w_dw, ffn_w_down, loss_target, m_norm_g, m_attn_w_qkv, m_attn_w_o, m_conv_w_in, m_conv_w_dw, m_conv_w_out, m_pool_w_in, m_pool_w_grp, m_pool_scale, m_pool_w_out, m_ffn_w_up, m_ffn_w_dw, m_ffn_w_down, v_norm_g, v_attn_w_qkv, v_attn_w_o, v_conv_w_in, v_conv_w_dw, v_conv_w_out, v_pool_w_in, v_pool_w_grp, v_pool_scale, v_pool_w_out, v_ffn_w_up, v_ffn_w_dw, v_ffn_w_down):
    shards = dict(zip(_NAMES, (norm_g, attn_w_qkv, attn_w_o, conv_w_in, conv_w_dw, conv_w_out, pool_w_in,
                               pool_w_grp, pool_scale, pool_w_out, ffn_w_up, ffn_w_dw, ffn_w_down)))
    moms = dict(zip(_NAMES, (m_norm_g, m_attn_w_qkv, m_attn_w_o, m_conv_w_in, m_conv_w_dw, m_conv_w_out,
                             m_pool_w_in, m_pool_w_grp, m_pool_scale, m_pool_w_out, m_ffn_w_up, m_ffn_w_dw,
                             m_ffn_w_down)))
    vels = dict(zip(_NAMES, (v_norm_g, v_attn_w_qkv, v_attn_w_o, v_conv_w_in, v_conv_w_dw, v_conv_w_out,
                             v_pool_w_in, v_pool_w_grp, v_pool_scale, v_pool_w_out, v_ffn_w_up, v_ffn_w_dw,
                             v_ffn_w_down)))
    weights = _LayerWeights(shards)
    sink = _GradSink()
    loss, grad_x, vec_grads = _local_step(x[0], loss_target[0], weights.vec, weights, sink)
    loss = lax.psum(loss[0, 0], _AXES)

    items, slots = sink.last
    vec_slots = {k: _slots_from_full(vec_grads[k], k) for k in _VECTORS if k != "ffn_w_dw"}
    vec_slots["ffn_w_dw"] = jnp.stack(vec_grads["ffn_w_dw"], axis=1)
    half = slots[0].shape[1] // 2
    state, token = _scatter_begin([slots[0][:, :half], slots[0][:, half:], slots[1],
                                   _pack_vectors(vec_slots, (N_DEV,)).astype(BF16)],
                                  [ROUTE_A, ROUTE_B, ROUTE_A, ROUTE_B], ["qkv0a", "qkv0b", "wo0", "vectors"], "scatter0")
    results = {}

    flipped = {k for k in _MATRICES if _WEIGHTS[k][0][0] > 1 and _shard_shape(k)[-1] % LANES}
    wmv = {k: [t.transpose(0, 2, 1) if k in flipped else t for t in (shards[k], moms[k], vels[k])] for k in _MATRICES}

    def step(matrices, dep=None):
        outs = []
        for k, j in matrices:
            g = sink.sums[(k, j)]
            w, m, v = wmv[k]
            if _WEIGHTS[k][0][0] == 1:
                results[k] = (g[None], _adamw(w, g[None], m, v, f"adamw_{k}", dep=dep))
            else:
                gs, prev = results.get(k, ({}, None))
                gs[j] = g
                results[k] = (gs, _adamw(w, g.T if k in flipped else g, m, v, f"adamw_{k}{j}", layer=j, prev=prev, dep=dep))
            outs.append(results[k][1][0])
        return outs

    state, token = _scatter_next(state, step(_layer_matrices(3), token))
    state, token = _scatter_next(state, step(_layer_matrices(2) + _layer_matrices(1), token))
    sums, _ = _scatter_next(state, step(_layer_matrices(0)[-2:], token))
    sink.sums.update(dict(zip(items, [jnp.concatenate(sums[:2], axis=0), sums[2]])))
    step(items)
    vec_sums = _unpack_vectors(sums[-1], ())
    for k in _VECTORS:
        results[k] = (vec_sums[k], _adamw(shards[k], vec_sums[k], moms[k], vels[k], f"adamw_{k}"))
    grads_out = {k: g if not isinstance(g, dict) else jnp.stack([g[j] for j in range(len(g))])
                 for k, (g, _) in results.items()}
    stepped = {k: [o.transpose(0, 2, 1) if k in flipped else o for o in outs] for k, (_, outs) in results.items()}
    return (loss, grad_x[None], *[grads_out[k] for k in _NAMES], *[stepped[k][0] for k in _NAMES],
            *[stepped[k][1] for k in _NAMES], *[stepped[k][2] for k in _NAMES])
```
